```python
import jax, jax.numpy as jnp
from jax import lax
import numpy as np

D_MODEL = 1024
BATCH = 8
SEQ = 16384
DEPTH = 1

N_META = 16
EPS = 1e-6
ATT_HEADS = 8
Q_LORA = 384
KV_LORA = 256
QK_NOPE = 128
QK_ROPE = 64
V_HEAD = 128
ROPE_THETA = 10000.0
ATT_WIDTH = ATT_HEADS * V_HEAD
Q_BLOCK = 128
SSM_HEADS = 16
SSM_HEAD_DIM = 64
SSM_WIDTH = SSM_HEADS * SSM_HEAD_DIM
SSM_GROUPS = 2
SSM_HEADS_PER_GROUP = SSM_HEADS // SSM_GROUPS
SSM_STATE = 128
CONV_K = 4
CHUNK = 128
CONV_DIM = SSM_WIDTH + 2 * SSM_GROUPS * SSM_STATE
MIX_WIDTH = ATT_WIDTH + SSM_WIDTH
D_FF = 4 * D_MODEL
IN_SPLITS = (Q_LORA, KV_LORA, QK_ROPE, SSM_WIDTH, CONV_DIM, SSM_HEADS)
IN_WIDTH = sum(IN_SPLITS)
IN_OFFSETS = tuple(int(v) for v in np.cumsum(IN_SPLITS)[:-1])

kernel_name = "hymba_mla_ssd_sandwich_layer"


def rmsnorm(x, w):
    x32 = x.astype(jnp.float32)
    y = x32 * lax.rsqrt(jnp.mean(jnp.square(x32), axis=-1, keepdims=True) + EPS)
    return y.astype(x.dtype) * w


def gated_group_rmsnorm(y, z, w):
    g = (y * jax.nn.silu(z)).astype(jnp.float32)
    shp = g.shape
    g = g.reshape(*shp[:-1], SSM_GROUPS, shp[-1] // SSM_GROUPS)
    g = g * lax.rsqrt(jnp.mean(jnp.square(g), axis=-1, keepdims=True) + EPS)
    return g.reshape(shp).astype(z.dtype) * w


def rope_tables(length, dtype):
    inv_freq = ROPE_THETA ** (-jnp.arange(0, QK_ROPE, 2, dtype=jnp.float32) / QK_ROPE)
    ang = jnp.arange(length, dtype=jnp.float32)[:, None] * inv_freq[None, :]
    return jnp.cos(ang).astype(dtype), jnp.sin(ang).astype(dtype)


def apply_rope(x, cos, sin):
    x1, x2 = jnp.split(x, 2, axis=-1)
    return jnp.concatenate([x1 * cos - x2 * sin, x2 * cos + x1 * sin], axis=-1)


def mla_attention(q_nope, q_rope, k_nope, k_rope, v):
    bsz, length = q_nope.shape[:2]
    scale = (QK_NOPE + QK_ROPE) ** -0.5
    pos = jnp.arange(length)

    def attend(qn, qr, qpos, kn, kr, vv, kpos):
        s = jnp.einsum('bqhd,bkhd->bhqk', qn, kn) + jnp.einsum('bqhd,bkd->bhqk', qr, kr)
        s = s.astype(jnp.float32) * scale
        s = jnp.where(kpos[None, None, None, :] <= qpos[None, None, :, None], s, -jnp.inf)
        p = jax.nn.softmax(s, axis=-1).astype(vv.dtype)
        return jnp.einsum('bhqk,bkhd->bqhd', p, vv)

    out_meta = attend(q_nope[:, :N_META], q_rope[:, :N_META], pos[:N_META],
                      k_nope[:, :N_META], k_rope[:, :N_META], v[:, :N_META], pos[:N_META])
    n_blocks = (length - N_META) // Q_BLOCK

    def to_blocks(t):
        t = t[:, N_META:]
        return jnp.moveaxis(t.reshape(bsz, n_blocks, Q_BLOCK, *t.shape[2:]), 1, 0)

    def block_fn(args):
        qn, qr, qp = args
        return attend(qn, qr, qp, k_nope, k_rope, v, pos)

    out_real = lax.map(block_fn, (to_blocks(q_nope), to_blocks(q_rope), pos[N_META:].reshape(n_blocks, Q_BLOCK)))
    out_real = jnp.moveaxis(out_real, 0, 1).reshape(bsz, length - N_META, ATT_HEADS, V_HEAD)
    return jnp.concatenate([out_meta, out_real], axis=1)


def causal_depthwise_conv(u, w, bias):
    out = lax.conv_general_dilated(u, w[:, None, :], window_strides=(1,), padding=((CONV_K - 1, 0),),
                                   dimension_numbers=('NWC', 'WIO', 'NWC'), feature_group_count=u.shape[-1])
    return out + bias


def ssd_chunked(x, dt, a, b_mat, c_mat, state0, chunk):
    bsz, total = x.shape[:2]
    nc = total // chunk

    def chunks(t):
        return t.reshape(bsz, nc, chunk, *t.shape[2:])

    x, dt, b_mat, c_mat = chunks(x), chunks(dt), chunks(b_mat), chunks(c_mat)
    a_cum = jnp.cumsum(dt * a, axis=2)
    seg = a_cum[:, :, :, None] - a_cum[:, :, None, :]
    causal = jnp.tril(jnp.ones((chunk, chunk), dtype=bool))[:, :, None, None]
    decay = jnp.where(causal, jnp.exp(jnp.where(causal, seg, 0.0)), 0.0)
    cb = jnp.einsum('bclgn,bcsgn->bclsg', c_mat, b_mat)
    w_ls = cb[..., None] * decay * dt[:, :, None]
    y_diag = jnp.einsum('bclsgr,bcsgrp->bclgrp', w_ls, x)
    decay_to_end = jnp.exp(a_cum[:, :, -1:] - a_cum)
    states = jnp.einsum('bcsgn,bcsgr,bcsgrp->bcgrpn', b_mat, decay_to_end * dt, x)
    chunk_decay = jnp.exp(a_cum[:, :, -1])

    def step(h, inp):
        dec, st = inp
        return dec[..., None, None] * h + st, h

    final, prev = lax.scan(step, state0, (jnp.moveaxis(chunk_decay, 1, 0), jnp.moveaxis(states, 1, 0)))
    prev = jnp.moveaxis(prev, 0, 1)
    y_off = jnp.einsum('bclgn,bcgrpn,bclgr->bclgrp', c_mat, prev, jnp.exp(a_cum))
    y = (y_diag + y_off).reshape(bsz, total, *x.shape[3:])
    return y, final


def hybrid_mixer(h, w_in, q_a_norm, w_q_up, kv_a_norm, w_kv_up, conv_w, conv_b,
                 dt_bias, a_log, d_skip, ssm_norm, w_out, cos, sin):
    bsz, length, _ = h.shape
    proj = h @ w_in
    c_q, c_kv, k_rope, z, xbc, dt_raw = jnp.split(proj, IN_OFFSETS, axis=-1)

    q = (rmsnorm(c_q, q_a_norm) @ w_q_up).reshape(bsz, length, ATT_HEADS, QK_NOPE + QK_ROPE)
    q_nope, q_rope = q[..., :QK_NOPE], q[..., QK_NOPE:]
    kv = (rmsnorm(c_kv, kv_a_norm) @ w_kv_up).reshape(bsz, length, ATT_HEADS, QK_NOPE + V_HEAD)
    k_nope, v = kv[..., :QK_NOPE], kv[..., QK_NOPE:]
    q_rope = apply_rope(q_rope, cos[None, :, None, :], sin[None, :, None, :])
    k_rope = apply_rope(k_rope, cos[None], sin[None])
    att = mla_attention(q_nope, q_rope, k_nope, k_rope, v).reshape(bsz, length, ATT_WIDTH)

    xbc = jax.nn.silu(causal_depthwise_conv(xbc, conv_w, conv_b))
    xs, bm, cm = jnp.split(xbc, (SSM_WIDTH, SSM_WIDTH + SSM_GROUPS * SSM_STATE), axis=-1)
    xs32 = xs.astype(jnp.float32).reshape(bsz, length, SSM_GROUPS, SSM_HEADS_PER_GROUP, SSM_HEAD_DIM)
    bm = bm.astype(jnp.float32).reshape(bsz, length, SSM_GROUPS, SSM_STATE)
    cm = cm.astype(jnp.float32).reshape(bsz, length, SSM_GROUPS, SSM_STATE)
    dt = jax.nn.softplus(dt_raw.astype(jnp.float32) + dt_bias.astype(jnp.float32))
    dt = dt.reshape(bsz, length, SSM_GROUPS, SSM_HEADS_PER_GROUP)
    a = -jnp.exp(a_log.astype(jnp.float32)).reshape(SSM_GROUPS, SSM_HEADS_PER_GROUP)
    state0 = jnp.zeros((bsz, SSM_GROUPS, SSM_HEADS_PER_GROUP, SSM_HEAD_DIM, SSM_STATE), jnp.float32)
    y_meta, st = ssd_chunked(xs32[:, :N_META], dt[:, :N_META], a, bm[:, :N_META], cm[:, :N_META], state0, N_META)
    y_real, _ = ssd_chunked(xs32[:, N_META:], dt[:, N_META:], a, bm[:, N_META:], cm[:, N_META:], st, CHUNK)
    y = jnp.concatenate([y_meta, y_real], axis=1)
    y = y + d_skip.astype(jnp.float32).reshape(SSM_GROUPS, SSM_HEADS_PER_GROUP)[:, :, None] * xs32
    y = y.reshape(bsz, length, SSM_WIDTH).astype(h.dtype)
    ssm = gated_group_rmsnorm(y, z, ssm_norm)

    return jnp.concatenate([att, ssm], axis=-1) @ w_out


def _fwd_setup_inputs(seed: int = 0) -> dict:
    key = jax.random.key(seed)
    ks = jax.random.split(key, 24)
    f32 = jnp.float32

    def nrm(k, shape, fan_in):
        return jax.random.normal(k, shape, f32) * (fan_in ** -0.5)

    def gain(k, shape):
        return 1.0 + 0.05 * jax.random.normal(k, shape, f32)

    dt0 = jnp.exp(jax.random.uniform(ks[10], (DEPTH, SSM_HEADS), f32) * (jnp.log(0.1) - jnp.log(0.001)) + jnp.log(0.001))
    return {
        "x": jax.random.normal(ks[0], (BATCH, SEQ, D_MODEL), f32),
        "meta_tokens": jax.random.normal(ks[1], (N_META, D_MODEL), f32),
        "norm_mix_pre": gain(ks[2], (DEPTH, D_MODEL)),
        "w_in": nrm(ks[3], (DEPTH, D_MODEL, IN_WIDTH), D_MODEL),
        "q_a_norm": gain(ks[4], (DEPTH, Q_LORA)),
        "w_q_up": nrm(ks[5], (DEPTH, Q_LORA, ATT_HEADS * (QK_NOPE + QK_ROPE)), Q_LORA),
        "kv_a_norm": gain(ks[6], (DEPTH, KV_LORA)),
        "w_kv_up": nrm(ks[7], (DEPTH, KV_LORA, ATT_HEADS * (QK_NOPE + V_HEAD)), KV_LORA),
        "conv_w": nrm(ks[8], (DEPTH, CONV_K, CONV_DIM), CONV_K),
        "conv_b": 0.01 * jax.random.normal(ks[9], (DEPTH, CONV_DIM), f32),
        "dt_bias": dt0 + jnp.log(-jnp.expm1(-dt0)),
        "a_log": jnp.log(jax.random.uniform(ks[11], (DEPTH, SSM_HEADS), f32, 1.0, 16.0)),
        "d_skip": gain(ks[12], (DEPTH, SSM_HEADS)),
        "ssm_norm": gain(ks[13], (DEPTH, SSM_WIDTH)),
        "w_out": nrm(ks[14], (DEPTH, MIX_WIDTH, D_MODEL), MIX_WIDTH),
        "norm_mix_post": gain(ks[15], (DEPTH, D_MODEL)),
        "norm_mlp_pre": gain(ks[16], (DEPTH, D_MODEL)),
        "w_mlp_up": nrm(ks[17], (DEPTH, D_MODEL, D_FF), D_MODEL),
        "w_mlp_down": nrm(ks[18], (DEPTH, D_FF, D_MODEL), D_FF),
        "norm_mlp_post": gain(ks[19], (DEPTH, D_MODEL)),
    }


def _fwd_reference(x, meta_tokens, norm_mix_pre, w_in, q_a_norm, w_q_up, kv_a_norm, w_kv_up, conv_w, conv_b,
              dt_bias, a_log, d_skip, ssm_norm, w_out, norm_mix_post, norm_mlp_pre, w_mlp_up, w_mlp_down,
              norm_mlp_post):
    bsz = x.shape[0]
    h = jnp.concatenate([jnp.broadcast_to(meta_tokens[None].astype(x.dtype), (bsz, N_META, D_MODEL)), x], axis=1)
    cos, sin = rope_tables(h.shape[1], h.dtype)
    for layer in range(DEPTH):
        mix = hybrid_mixer(rmsnorm(h, norm_mix_pre[layer]), w_in[layer], q_a_norm[layer], w_q_up[layer],
                           kv_a_norm[layer], w_kv_up[layer], conv_w[layer], conv_b[layer], dt_bias[layer],
                           a_log[layer], d_skip[layer], ssm_norm[layer], w_out[layer], cos, sin)
        h = h + rmsnorm(mix, norm_mix_post[layer])
        f = jnp.square(jax.nn.relu(rmsnorm(h, norm_mlp_pre[layer]) @ w_mlp_up[layer])) @ w_mlp_down[layer]
        h = h + rmsnorm(f, norm_mlp_post[layer])
    return h[:, N_META:]


import jax as _jax
import jax.numpy as _jnp

TWIN_FORMAT = 'train_step'
FWD_PARAMS = ['x', 'meta_tokens', 'norm_mix_pre', 'w_in', 'q_a_norm', 'w_q_up', 'kv_a_norm', 'w_kv_up', 'conv_w', 'conv_b', 'dt_bias', 'a_log', 'd_skip', 'ssm_norm', 'w_out', 'norm_mix_post', 'norm_mlp_pre', 'w_mlp_up', 'w_mlp_down', 'norm_mlp_post']
TWIN_WEIGHTS = ['meta_tokens', 'norm_mix_pre', 'w_in', 'q_a_norm', 'w_q_up', 'kv_a_norm', 'w_kv_up', 'conv_w', 'conv_b', 'dt_bias', 'a_log', 'd_skip', 'ssm_norm', 'w_out', 'norm_mix_post', 'norm_mlp_pre', 'w_mlp_up', 'w_mlp_down', 'norm_mlp_post']
TWIN_DIFF_INPUT = 'x'
TWIN_INPUTS = ['x', 'meta_tokens', 'norm_mix_pre', 'w_in', 'q_a_norm', 'w_q_up', 'kv_a_norm', 'w_kv_up', 'conv_w', 'conv_b', 'dt_bias', 'a_log', 'd_skip', 'ssm_norm', 'w_out', 'norm_mix_post', 'norm_mlp_pre', 'w_mlp_up', 'w_mlp_down', 'norm_mlp_post', 'loss_target', 'm_meta_tokens', 'm_norm_mix_pre', 'm_w_in', 'm_q_a_norm', 'm_w_q_up', 'm_kv_a_norm', 'm_w_kv_up', 'm_conv_w', 'm_conv_b', 'm_dt_bias', 'm_a_log', 'm_d_skip', 'm_ssm_norm', 'm_w_out', 'm_norm_mix_post', 'm_norm_mlp_pre', 'm_w_mlp_up', 'm_w_mlp_down', 'm_norm_mlp_post', 'v_meta_tokens', 'v_norm_mix_pre', 'v_w_in', 'v_q_a_norm', 'v_w_q_up', 'v_kv_a_norm', 'v_w_kv_up', 'v_conv_w', 'v_conv_b', 'v_dt_bias', 'v_a_log', 'v_d_skip', 'v_ssm_norm', 'v_w_out', 'v_norm_mix_post', 'v_norm_mlp_pre', 'v_w_mlp_up', 'v_w_mlp_down', 'v_norm_mlp_post']
TWIN_OUTPUTS = ['loss', 'grad_x', 'grad_meta_tokens', 'grad_norm_mix_pre', 'grad_w_in', 'grad_q_a_norm', 'grad_w_q_up', 'grad_kv_a_norm', 'grad_w_kv_up', 'grad_conv_w', 'grad_conv_b', 'grad_dt_bias', 'grad_a_log', 'grad_d_skip', 'grad_ssm_norm', 'grad_w_out', 'grad_norm_mix_post', 'grad_norm_mlp_pre', 'grad_w_mlp_up', 'grad_w_mlp_down', 'grad_norm_mlp_post', 'delta_meta_tokens', 'delta_norm_mix_pre', 'delta_w_in', 'delta_q_a_norm', 'delta_w_q_up', 'delta_kv_a_norm', 'delta_w_kv_up', 'delta_conv_w', 'delta_conv_b', 'delta_dt_bias', 'delta_a_log', 'delta_d_skip', 'delta_ssm_norm', 'delta_w_out', 'delta_norm_mix_post', 'delta_norm_mlp_pre', 'delta_w_mlp_up', 'delta_w_mlp_down', 'delta_norm_mlp_post', 'new_m_meta_tokens', 'new_m_norm_mix_pre', 'new_m_w_in', 'new_m_q_a_norm', 'new_m_w_q_up', 'new_m_kv_a_norm', 'new_m_w_kv_up', 'new_m_conv_w', 'new_m_conv_b', 'new_m_dt_bias', 'new_m_a_log', 'new_m_d_skip', 'new_m_ssm_norm', 'new_m_w_out', 'new_m_norm_mix_post', 'new_m_norm_mlp_pre', 'new_m_w_mlp_up', 'new_m_w_mlp_down', 'new_m_norm_mlp_post', 'new_v_meta_tokens', 'new_v_norm_mix_pre', 'new_v_w_in', 'new_v_q_a_norm', 'new_v_w_q_up', 'new_v_kv_a_norm', 'new_v_w_kv_up', 'new_v_conv_w', 'new_v_conv_b', 'new_v_dt_bias', 'new_v_a_log', 'new_v_d_skip', 'new_v_ssm_norm', 'new_v_w_out', 'new_v_norm_mix_post', 'new_v_norm_mlp_pre', 'new_v_w_mlp_up', 'new_v_w_mlp_down', 'new_v_norm_mlp_post']
TWIN_LEAF_KINDS = {'loss': 'loss', 'grad_x': 'grad_x', 'grad_meta_tokens': 'grad_w', 'grad_norm_mix_pre': 'grad_w', 'grad_w_in': 'grad_w', 'grad_q_a_norm': 'grad_w', 'grad_w_q_up': 'grad_w', 'grad_kv_a_norm': 'grad_w', 'grad_w_kv_up': 'grad_w', 'grad_conv_w': 'grad_w', 'grad_conv_b': 'grad_w', 'grad_dt_bias': 'grad_w', 'grad_a_log': 'grad_w', 'grad_d_skip': 'grad_w', 'grad_ssm_norm': 'grad_w', 'grad_w_out': 'grad_w', 'grad_norm_mix_post': 'grad_w', 'grad_norm_mlp_pre': 'grad_w', 'grad_w_mlp_up': 'grad_w', 'grad_w_mlp_down': 'grad_w', 'grad_norm_mlp_post': 'grad_w', 'delta_meta_tokens': 'delta_w', 'delta_norm_mix_pre': 'delta_w', 'delta_w_in': 'delta_w', 'delta_q_a_norm': 'delta_w', 'delta_w_q_up': 'delta_w', 'delta_kv_a_norm': 'delta_w', 'delta_w_kv_up': 'delta_w', 'delta_conv_w': 'delta_w', 'delta_conv_b': 'delta_w', 'delta_dt_bias': 'delta_w', 'delta_a_log': 'delta_w', 'delta_d_skip': 'delta_w', 'delta_ssm_norm': 'delta_w', 'delta_w_out': 'delta_w', 'delta_norm_mix_post': 'delta_w', 'delta_norm_mlp_pre': 'delta_w', 'delta_w_mlp_up': 'delta_w', 'delta_w_mlp_down': 'delta_w', 'delta_norm_mlp_post': 'delta_w', 'new_m_meta_tokens': 'new_m', 'new_m_norm_mix_pre': 'new_m', 'new_m_w_in': 'new_m', 'new_m_q_a_norm': 'new_m', 'new_m_w_q_up': 'new_m', 'new_m_kv_a_norm': 'new_m', 'new_m_w_kv_up': 'new_m', 'new_m_conv_w': 'new_m', 'new_m_conv_b': 'new_m', 'new_m_dt_bias': 'new_m', 'new_m_a_log': 'new_m', 'new_m_d_skip': 'new_m', 'new_m_ssm_norm': 'new_m', 'new_m_w_out': 'new_m', 'new_m_norm_mix_post': 'new_m', 'new_m_norm_mlp_pre': 'new_m', 'new_m_w_mlp_up': 'new_m', 'new_m_w_mlp_down': 'new_m', 'new_m_norm_mlp_post': 'new_m', 'new_v_meta_tokens': 'new_v', 'new_v_norm_mix_pre': 'new_v', 'new_v_w_in': 'new_v', 'new_v_q_a_norm': 'new_v', 'new_v_w_q_up': 'new_v', 'new_v_kv_a_norm': 'new_v', 'new_v_w_kv_up': 'new_v', 'new_v_conv_w': 'new_v', 'new_v_conv_b': 'new_v', 'new_v_dt_bias': 'new_v', 'new_v_a_log': 'new_v', 'new_v_d_skip': 'new_v', 'new_v_ssm_norm': 'new_v', 'new_v_w_out': 'new_v', 'new_v_norm_mix_post': 'new_v', 'new_v_norm_mlp_pre': 'new_v', 'new_v_w_mlp_up': 'new_v', 'new_v_w_mlp_down': 'new_v', 'new_v_norm_mlp_post': 'new_v'}


def _forward(args):
    return _fwd_reference(*[args[k] for k in FWD_PARAMS])


def _output_shape():
    def fwd():
        inp = _fwd_setup_inputs(0)
        return _fwd_reference(*[inp[k] for k in FWD_PARAMS])
    out = _jax.eval_shape(fwd)
    return out.shape, out.dtype

N_MICROBATCH = 1
ADAM_LR = 0.001
ADAM_B1 = 0.9
ADAM_B2 = 0.999
ADAM_EPS = 1e-08
ADAM_WD = 0.01
ADAM_STEP = 10
PER_EXAMPLE_BATCH_AXIS = {'x': 0, 'loss_target': 0}
SHARED_INPUTS = []
_WEIGHT_DTYPES = {'meta_tokens': _jnp.float32, 'norm_mix_pre': _jnp.float32, 'w_in': _jnp.float32, 'q_a_norm': _jnp.float32, 'w_q_up': _jnp.float32, 'kv_a_norm': _jnp.float32, 'w_kv_up': _jnp.float32, 'conv_w': _jnp.float32, 'conv_b': _jnp.float32, 'dt_bias': _jnp.float32, 'a_log': _jnp.float32, 'd_skip': _jnp.float32, 'ssm_norm': _jnp.float32, 'w_out': _jnp.float32, 'norm_mix_post': _jnp.float32, 'norm_mlp_pre': _jnp.float32, 'w_mlp_up': _jnp.float32, 'w_mlp_down': _jnp.float32, 'norm_mlp_post': _jnp.float32}
MOMENT_SCALE = {'meta_tokens': 8.146269e-02, 'norm_mix_pre': 1.583866e+00, 'w_in': 9.313748e-01, 'q_a_norm': 2.305117e-01, 'w_q_up': 1.095301e-01, 'kv_a_norm': 6.642342e-01, 'w_kv_up': 2.727724e-01, 'conv_w': 5.581626e+00, 'conv_b': 2.029916e+01, 'dt_bias': 2.660719e+00, 'a_log': 2.169768e+01, 'd_skip': 4.669727e+01, 'ssm_norm': 1.183362e+01, 'w_out': 9.862817e+00, 'norm_mix_post': 1.293368e+02, 'norm_mlp_pre': 3.484188e+00, 'w_mlp_up': 1.819960e+00, 'w_mlp_down': 9.911731e+00, 'norm_mlp_post': 1.331515e+02}


def _to_microbatches(a, axis):
    t = _jnp.moveaxis(a, axis, 0)
    t = t.reshape((N_MICROBATCH, t.shape[0] // N_MICROBATCH) + t.shape[1:])
    return _jnp.moveaxis(t, 1, axis + 1)


def setup_inputs(seed: int = 0) -> dict:
    inp = _fwd_setup_inputs(seed)
    key = _jax.random.fold_in(_jax.random.key(seed), 7919)
    shape, _ = _output_shape()
    out = dict(inp)
    out["loss_target"] = _jax.random.normal(_jax.random.fold_in(key, 0), shape, _jnp.float32)
    for i, name in enumerate(TWIN_WEIGHTS):
        w = inp[name].astype(_jnp.float32)
        if MOMENT_SCALE is None:
            s = _jnp.sqrt(_jnp.mean(_jnp.square(w)) + 1e-30)
        else:
            s = MOMENT_SCALE[name]
        km, kv = _jax.random.split(_jax.random.fold_in(key, i + 1))
        out[name] = w
        out["m_" + name] = s * _jax.random.normal(km, w.shape, _jnp.float32)
        out["v_" + name] = (s * s) * _jax.random.uniform(kv, w.shape, _jnp.float32, 0.5, 1.5)
    if N_MICROBATCH > 1:
        for name, axis in PER_EXAMPLE_BATCH_AXIS.items():
            out[name] = _to_microbatches(out[name], axis)
    return {'x': out['x'], 'meta_tokens': out['meta_tokens'], 'norm_mix_pre': out['norm_mix_pre'], 'w_in': out['w_in'], 'q_a_norm': out['q_a_norm'], 'w_q_up': out['w_q_up'], 'kv_a_norm': out['kv_a_norm'], 'w_kv_up': out['w_kv_up'], 'conv_w': out['conv_w'], 'conv_b': out['conv_b'], 'dt_bias': out['dt_bias'], 'a_log': out['a_log'], 'd_skip': out['d_skip'], 'ssm_norm': out['ssm_norm'], 'w_out': out['w_out'], 'norm_mix_post': out['norm_mix_post'], 'norm_mlp_pre': out['norm_mlp_pre'], 'w_mlp_up': out['w_mlp_up'], 'w_mlp_down': out['w_mlp_down'], 'norm_mlp_post': out['norm_mlp_post'], 'loss_target': out['loss_target'], 'm_meta_tokens': out['m_meta_tokens'], 'm_norm_mix_pre': out['m_norm_mix_pre'], 'm_w_in': out['m_w_in'], 'm_q_a_norm': out['m_q_a_norm'], 'm_w_q_up': out['m_w_q_up'], 'm_kv_a_norm': out['m_kv_a_norm'], 'm_w_kv_up': out['m_w_kv_up'], 'm_conv_w': out['m_conv_w'], 'm_conv_b': out['m_conv_b'], 'm_dt_bias': out['m_dt_bias'], 'm_a_log': out['m_a_log'], 'm_d_skip': out['m_d_skip'], 'm_ssm_norm': out['m_ssm_norm'], 'm_w_out': out['m_w_out'], 'm_norm_mix_post': out['m_norm_mix_post'], 'm_norm_mlp_pre': out['m_norm_mlp_pre'], 'm_w_mlp_up': out['m_w_mlp_up'], 'm_w_mlp_down': out['m_w_mlp_down'], 'm_norm_mlp_post': out['m_norm_mlp_post'], 'v_meta_tokens': out['v_meta_tokens'], 'v_norm_mix_pre': out['v_norm_mix_pre'], 'v_w_in': out['v_w_in'], 'v_q_a_norm': out['v_q_a_norm'], 'v_w_q_up': out['v_w_q_up'], 'v_kv_a_norm': out['v_kv_a_norm'], 'v_w_kv_up': out['v_w_kv_up'], 'v_conv_w': out['v_conv_w'], 'v_conv_b': out['v_conv_b'], 'v_dt_bias': out['v_dt_bias'], 'v_a_log': out['v_a_log'], 'v_d_skip': out['v_d_skip'], 'v_ssm_norm': out['v_ssm_norm'], 'v_w_out': out['v_w_out'], 'v_norm_mix_post': out['v_norm_mix_post'], 'v_norm_mlp_pre': out['v_norm_mlp_pre'], 'v_w_mlp_up': out['v_w_mlp_up'], 'v_w_mlp_down': out['v_w_mlp_down'], 'v_norm_mlp_post': out['v_norm_mlp_post']}


def _loss(weights, diff, rest, loss_target):
    with _jax.named_scope("forward"):
        args = {**rest, TWIN_DIFF_INPUT: diff, **{k: w.astype(_WEIGHT_DTYPES[k]) for k, w in weights.items()}}
        y = _forward(args)
    with _jax.named_scope("loss_head"):
        err = _jnp.square(y.astype(_jnp.float32) - loss_target)
        return 0.5 * _jnp.sum(_jnp.mean(err, axis=-1)) if err.ndim else 0.5 * err


def _adamw(w, g, m, v):
    m = ADAM_B1 * m + (1.0 - ADAM_B1) * g
    v = ADAM_B2 * v + (1.0 - ADAM_B2) * _jnp.square(g)
    m_hat = m / (1.0 - ADAM_B1 ** ADAM_STEP)
    v_hat = v / (1.0 - ADAM_B2 ** ADAM_STEP)
    delta = -ADAM_LR * (m_hat / (_jnp.sqrt(v_hat) + ADAM_EPS) + ADAM_WD * w)
    return delta, m, v


def reference(x, meta_tokens, norm_mix_pre, w_in, q_a_norm, w_q_up, kv_a_norm, w_kv_up, conv_w, conv_b, dt_bias, a_log, d_skip, ssm_norm, w_out, norm_mix_post, norm_mlp_pre, w_mlp_up, w_mlp_down, norm_mlp_post, loss_target, m_meta_tokens, m_norm_mix_pre, m_w_in, m_q_a_norm, m_w_q_up, m_kv_a_norm, m_w_kv_up, m_conv_w, m_conv_b, m_dt_bias, m_a_log, m_d_skip, m_ssm_norm, m_w_out, m_norm_mix_post, m_norm_mlp_pre, m_w_mlp_up, m_w_mlp_down, m_norm_mlp_post, v_meta_tokens, v_norm_mix_pre, v_w_in, v_q_a_norm, v_w_q_up, v_kv_a_norm, v_w_kv_up, v_conv_w, v_conv_b, v_dt_bias, v_a_log, v_d_skip, v_ssm_norm, v_w_out, v_norm_mix_post, v_norm_mlp_pre, v_w_mlp_up, v_w_mlp_down, v_norm_mlp_post):
    given = dict(x=x, meta_tokens=meta_tokens, norm_mix_pre=norm_mix_pre, w_in=w_in, q_a_norm=q_a_norm, w_q_up=w_q_up, kv_a_norm=kv_a_norm, w_kv_up=w_kv_up, conv_w=conv_w, conv_b=conv_b, dt_bias=dt_bias, a_log=a_log, d_skip=d_skip, ssm_norm=ssm_norm, w_out=w_out, norm_mix_post=norm_mix_post, norm_mlp_pre=norm_mlp_pre, w_mlp_up=w_mlp_up, w_mlp_down=w_mlp_down, norm_mlp_post=norm_mlp_post, loss_target=loss_target, m_meta_tokens=m_meta_tokens, m_norm_mix_pre=m_norm_mix_pre, m_w_in=m_w_in, m_q_a_norm=m_q_a_norm, m_w_q_up=m_w_q_up, m_kv_a_norm=m_kv_a_norm, m_w_kv_up=m_w_kv_up, m_conv_w=m_conv_w, m_conv_b=m_conv_b, m_dt_bias=m_dt_bias, m_a_log=m_a_log, m_d_skip=m_d_skip, m_ssm_norm=m_ssm_norm, m_w_out=m_w_out, m_norm_mix_post=m_norm_mix_post, m_norm_mlp_pre=m_norm_mlp_pre, m_w_mlp_up=m_w_mlp_up, m_w_mlp_down=m_w_mlp_down, m_norm_mlp_post=m_norm_mlp_post, v_meta_tokens=v_meta_tokens, v_norm_mix_pre=v_norm_mix_pre, v_w_in=v_w_in, v_q_a_norm=v_q_a_norm, v_w_q_up=v_w_q_up, v_kv_a_norm=v_kv_a_norm, v_w_kv_up=v_w_kv_up, v_conv_w=v_conv_w, v_conv_b=v_conv_b, v_dt_bias=v_dt_bias, v_a_log=v_a_log, v_d_skip=v_d_skip, v_ssm_norm=v_ssm_norm, v_w_out=v_w_out, v_norm_mix_post=v_norm_mix_post, v_norm_mlp_pre=v_norm_mlp_pre, v_w_mlp_up=v_w_mlp_up, v_w_mlp_down=v_w_mlp_down, v_norm_mlp_post=v_norm_mlp_post)
    weights = {n: given[n] for n in TWIN_WEIGHTS}
    shared = {n: given[n] for n in SHARED_INPUTS}
    per_example = {n: given[n] for n in ['x']}
    grad_fn = _jax.value_and_grad(_loss, argnums=(0, 1))

    def one_microbatch(ex, loss_target):
        ex = dict(ex)
        diff = ex.pop(TWIN_DIFF_INPUT)
        return grad_fn(weights, diff, {**shared, **ex}, loss_target)

    if N_MICROBATCH == 1:
        loss, (grad_w, grad_x) = one_microbatch(per_example, given["loss_target"])
    else:
        def body(carry, xs):
            loss_sum, grad_sum = carry
            l_k, (gw_k, gx_k) = one_microbatch(xs[0], xs[1])
            with _jax.named_scope("update"):
                return (loss_sum + l_k, _jax.tree.map(_jnp.add, grad_sum, gw_k)), gx_k

        init = (_jnp.zeros((), _jnp.float32), _jax.tree.map(_jnp.zeros_like, weights))
        (loss, grad_w), grad_x = _jax.lax.scan(body, init, (per_example, given["loss_target"]))
    with _jax.named_scope("update"):
        delta_w, new_m, new_v = {}, {}, {}
        for n in TWIN_WEIGHTS:
            delta_w[n], new_m[n], new_v[n] = _adamw(weights[n], grad_w[n], given["m_" + n], given["v_" + n])
    return (loss, grad_x, *[grad_w[n] for n in TWIN_WEIGHTS], *[delta_w[n] for n in TWIN_WEIGHTS],
            *[new_m[n] for n in TWIN_WEIGHTS], *[new_v[n] for n in TWIN_WEIGHTS])
```

```python
import functools

import numpy as np
import jax
import jax.numpy as jnp
from jax import lax
from jax.experimental import pallas as pl
from jax.experimental.pallas import tpu as pltpu

F32 = jnp.float32
BF16 = jnp.bfloat16

D_MODEL = 1024
N_META = 16
EPS = 1e-6
ATT_HEADS = 8
Q_LORA = 384
KV_LORA = 256
QK_NOPE = 128
QK_ROPE = 64
V_HEAD = 128
ROPE_THETA = 10000.0
SSM_HEADS = 16
SSM_HEAD_DIM = 64
SSM_WIDTH = 1024
SSM_STATE = 128
CONV_K = 4
D_FF = 4096
ATT_SCALE = float((QK_NOPE + QK_ROPE) ** -0.5)

ADAM_LR = 0.001
ADAM_B1 = 0.9
ADAM_B2 = 0.999
ADAM_EPS = 1e-08
ADAM_WD = 0.01
ADAM_STEP = 10

SEG_KV, SEG_KR, SEG_CQ, SEG_DT, SEG_Z, SEG_XS, SEG_BC = 0, 256, 384, 768, 1024, 2048, 3072
PROJ_W = 3584
QP_W = 256

ROW_BLOCK = 256
MM_BLOCK = 512
SSD_CHUNK = 128
VMEM_LIMIT = 56 * 1024 * 1024
NEG_BIG = -1e30

PACK_W = 1024
N_CHIPS = 4
MESH_ID = pl.DeviceIdType.MESH


def _params(sem):
    return pltpu.CompilerParams(dimension_semantics=sem, vmem_limit_bytes=VMEM_LIMIT)


def _rb(rows, width, cb=0):
    return pl.BlockSpec((rows, width), lambda i: (i, cb))


def _full(shape):
    zeros = (0,) * len(shape)
    return pl.BlockSpec(shape, lambda i: zeros)


def _acc_add(ref, val):
    first = pl.program_id(0) == 0

    @pl.when(first)
    def _():
        ref[...] = val

    @pl.when(jnp.logical_not(first))
    def _():
        ref[...] += val


def _rms(x, g):
    r = lax.rsqrt(jnp.mean(x * x, axis=-1, keepdims=True) + EPS)
    return x * r * g


def _rms_bwd(x, g, dy):
    r = lax.rsqrt(jnp.mean(x * x, axis=-1, keepdims=True) + EPS)
    dyg = dy * g
    dx = r * dyg - x * (r * r * r) * jnp.mean(x * dyg, axis=-1, keepdims=True)
    dg = jnp.sum(dy * x * r, axis=0, keepdims=True)
    return dx, dg


def _sigmoid(x):
    return 1.0 / (1.0 + jnp.exp(-x))


def _swap32(x):
    lane = lax.broadcasted_iota(jnp.int32, x.shape, 1)
    return jnp.where((lane % 64) < 32, pltpu.roll(x, 96, 1), pltpu.roll(x, 32, 1))


def _rope(x, cos_t, sin_t):
    return x * cos_t + _swap32(x) * sin_t


def _rope_bwd(dr, cos_t, sin_t):
    return dr * cos_t + _swap32(dr * sin_t)


def _tile(n, cap):
    if n <= cap:
        return n
    best = 128
    for t in range(128, cap + 1, 128):
        if n % t == 0:
            best = t
    assert n % best == 0, (n, cap)
    return best


def _mm(name, a, b, outs=((F32, None),), epi_ins=(), tn_cap=512, tk_cap=1024):
    m, k = a.shape
    n = b.shape[1]
    tm, tn, tk = MM_BLOCK, _tile(n, tn_cap), _tile(k, tk_cap)
    nk = k // tk
    n_epi = len(epi_ins)

    def body(a_ref, b_ref, *rest):
        epi_refs = rest[:n_epi]
        out_refs = rest[n_epi:n_epi + len(outs)]
        acc = rest[-1]
        kk = pl.program_id(2)

        @pl.when(kk == 0)
        def _():
            acc[...] = jnp.zeros_like(acc)

        acc[...] += jnp.dot(a_ref[...].astype(BF16), b_ref[...].astype(BF16), preferred_element_type=F32)

        @pl.when(kk == nk - 1)
        def _():
            r = acc[...]
            blocks = [e[...] for e in epi_refs]
            for o_ref, (dt, fn) in zip(out_refs, outs):
                o_ref[...] = (r if fn is None else fn(r, *blocks)).astype(dt)

    out_spec = pl.BlockSpec((tm, tn), lambda i, j, kk: (i, j))
    res = pl.pallas_call(
        body,
        out_shape=[jax.ShapeDtypeStruct((m, n), dt) for dt, _ in outs],
        grid=(m // tm, n // tn, nk),
        in_specs=[pl.BlockSpec((tm, tk), lambda i, j, kk: (i, kk)), pl.BlockSpec((tk, tn), lambda i, j, kk: (kk, j))]
        + [out_spec] * n_epi,
        out_specs=[out_spec] * len(outs),
        scratch_shapes=[pltpu.VMEM((tm, tn), F32)],
        name=name,
        compiler_params=_params(("parallel", "parallel", "arbitrary")),
    )(a, b, *epi_ins)
    return res[0] if len(outs) == 1 else res


def _mm_tn(name, x, dy, ta_cap=1024, tn_cap=512):
    l, a = x.shape
    n = dy.shape[1]
    ta, tn, tl = _tile(a, ta_cap), _tile(n, tn_cap), MM_BLOCK
    nl = l // tl

    def body(x_ref, dy_ref, o_ref):
        ll = pl.program_id(2)

        @pl.when(ll == 0)
        def _():
            o_ref[...] = jnp.zeros_like(o_ref)

        o_ref[...] += lax.dot_general(x_ref[...].astype(BF16), dy_ref[...].astype(BF16), (((0,), (0,)), ((), ())),
                                      preferred_element_type=F32)

    return pl.pallas_call(
        body,
        out_shape=jax.ShapeDtypeStruct((a, n), F32),
        grid=(a // ta, n // tn, nl),
        in_specs=[pl.BlockSpec((tl, ta), lambda i, j, ll: (ll, i)), pl.BlockSpec((tl, tn), lambda i, j, ll: (ll, j))],
        out_specs=pl.BlockSpec((ta, tn), lambda i, j, ll: (i, j)),
        name=name,
        compiler_params=_params(("parallel", "parallel", "arbitrary")),
    )(x, dy)


def _norm_in(h0, g_pre):
    lp = h0.shape[0]
    t = ROW_BLOCK

    def body(h_ref, g_ref, o_ref):
        o_ref[...] = _rms(h_ref[...], g_ref[...]).astype(BF16)

    return pl.pallas_call(
        body, out_shape=jax.ShapeDtypeStruct((lp, D_MODEL), BF16), grid=(lp // t,),
        in_specs=[_rb(t, D_MODEL), _full((1, D_MODEL))], out_specs=_rb(t, D_MODEL),
        name="norm_in", compiler_params=_params(("parallel",)))(h0, g_pre)


def _attn_prep(proj, g_q, g_kv, cos_t, sin_t):
    lp = proj.shape[0]
    t = ROW_BLOCK

    def body(ckv_ref, kr_ref, cq_ref, gq_ref, gkv_ref, cos_ref, sin_ref, cqn_ref, ckvn_ref, krr_ref):
        cqn_ref[...] = _rms(cq_ref[...], gq_ref[...]).astype(BF16)
        ckvn_ref[...] = _rms(ckv_ref[...], gkv_ref[...]).astype(BF16)
        roped = _rope(kr_ref[...], cos_ref[...], sin_ref[...])
        krr_ref[...] = roped + pltpu.roll(roped, 64, 1)

    return pl.pallas_call(
        body,
        out_shape=[jax.ShapeDtypeStruct((lp, Q_LORA), BF16), jax.ShapeDtypeStruct((lp, KV_LORA), BF16),
                   jax.ShapeDtypeStruct((lp, 128), F32)],
        grid=(lp // t,),
        in_specs=[_rb(t, KV_LORA, SEG_KV // KV_LORA), _rb(t, 128, SEG_KR // 128), _rb(t, Q_LORA, SEG_CQ // Q_LORA),
                  _full((1, Q_LORA)), _full((1, KV_LORA)), _rb(t, 128), _rb(t, 128)],
        out_specs=[_rb(t, Q_LORA), _rb(t, KV_LORA), _rb(t, 128)],
        name="attn_prep", compiler_params=_params(("parallel",)))(proj, proj, proj, g_q, g_kv, cos_t, sin_t)


def _qk_pack(q, kv, krr, cos_t, sin_t):
    lp = q.shape[0]
    t = ROW_BLOCK

    def body(q_ref, kv_ref, krr_ref, cos_ref, sin_ref, qs_ref, ks_ref, vs_ref):
        lane = lax.broadcasted_iota(jnp.int32, (t, 128), 1)
        lo = lane < 64
        krr = krr_ref[...].astype(BF16)
        for j in range(ATT_HEADS // 2):
            pr = _rope(q_ref[:, 1024 + 128 * j:1024 + 128 * (j + 1)], cos_ref[...], sin_ref[...])
            for h, keep in ((2 * j, lo), (2 * j + 1, jnp.logical_not(lo))):
                qs_ref[h, :, 0:128] = q_ref[:, 128 * h:128 * (h + 1)].astype(BF16)
                qs_ref[h, :, 128:256] = jnp.where(keep, pr, 0.0).astype(BF16)
        for h in range(ATT_HEADS):
            ks_ref[h, :, 0:128] = kv_ref[:, 256 * h:256 * h + 128].astype(BF16)
            ks_ref[h, :, 128:256] = krr
            vs_ref[h] = kv_ref[:, 256 * h + 128:256 * (h + 1)].astype(BF16)

    slab = lambda w: pl.BlockSpec((ATT_HEADS, t, w), lambda i: (0, i, 0))
    return pl.pallas_call(
        body,
        out_shape=[jax.ShapeDtypeStruct((ATT_HEADS, lp, QP_W), BF16), jax.ShapeDtypeStruct((ATT_HEADS, lp, QP_W), BF16),
                   jax.ShapeDtypeStruct((ATT_HEADS, lp, V_HEAD), BF16)],
        grid=(lp // t,),
        in_specs=[_rb(t, 1536), _rb(t, 2048), _rb(t, 128), _rb(t, 128), _rb(t, 128)],
        out_specs=[slab(QP_W), slab(QP_W), slab(V_HEAD)],
        name="qk_pack", compiler_params=_params(("parallel",)))(q, kv, krr, cos_t, sin_t)


def _shifted(ext, t, shift):
    if shift == 0:
        return ext[8:, :]
    return pltpu.roll(ext, shift, 0)[8:, :]


def _conv_fwd(name, proj, seg, width, conv_w, conv_b):
    lp = proj.shape[0]
    t = ROW_BLOCK
    cb = seg // width

    def body(u_ref, halo_ref, w_ref, b_ref, pre_ref, act_ref):
        i = pl.program_id(0)
        u = u_ref[...]
        halo = jnp.where(i > 0, halo_ref[...], 0.0)
        ext = jnp.concatenate([halo, u], axis=0)
        pre = jnp.broadcast_to(b_ref[...], (t, width))
        for k in range(CONV_K):
            pre = pre + w_ref[k:k + 1, :] * _shifted(ext, t, CONV_K - 1 - k)
        pre_ref[...] = pre
        act_ref[...] = pre * _sigmoid(pre)

    return pl.pallas_call(
        body,
        out_shape=[jax.ShapeDtypeStruct((lp, width), F32)] * 2,
        grid=(lp // t,),
        in_specs=[_rb(t, width, cb),
                  pl.BlockSpec((8, width), lambda i: (jnp.maximum(i * (t // 8) - 1, 0), cb)),
                  _full((CONV_K, width)), _full((1, width))],
        out_specs=[_rb(t, width), _rb(t, width)],
        name=name, compiler_params=_params(("parallel",)))(proj, proj, conv_w, conv_b)


def _softplus(x):
    return jnp.maximum(x, 0.0) + jnp.log1p(jnp.exp(-jnp.abs(x)))


def _dt_fwd(proj, dt_bias_pad):
    lp = proj.shape[0]
    t = ROW_BLOCK

    def body(x_ref, b_ref, o_ref):
        o_ref[...] = _softplus(x_ref[...] + b_ref[...])

    return pl.pallas_call(
        body, out_shape=jax.ShapeDtypeStruct((lp, 128), F32), grid=(lp // t,),
        in_specs=[_rb(t, 128, SEG_DT // 128), _full((1, 128))], out_specs=_rb(t, 128),
        name="dt_fwd", compiler_params=_params(("parallel",)))(proj, dt_bias_pad)


def _gated_norm_group(y, z, w):
    g = y * (z * _sigmoid(z))
    return g * lax.rsqrt(jnp.mean(g * g, axis=-1, keepdims=True) + EPS) * w


def _gated_norm_fwd(y, proj, w):
    lp = y.shape[0]
    t = ROW_BLOCK
    gw = SSM_WIDTH // 2

    def body(y0, y1, z0, z1, w0, w1, o_ref):
        o_ref[:, 0:gw] = _gated_norm_group(y0[...], z0[...], w0[...]).astype(BF16)
        o_ref[:, gw:] = _gated_norm_group(y1[...], z1[...], w1[...]).astype(BF16)

    zb = SEG_Z // gw
    return pl.pallas_call(
        body, out_shape=jax.ShapeDtypeStruct((lp, SSM_WIDTH), BF16), grid=(lp // t,),
        in_specs=[_rb(t, gw, 0), _rb(t, gw, 1), _rb(t, gw, zb), _rb(t, gw, zb + 1),
                  pl.BlockSpec((1, gw), lambda i: (0, 0)), pl.BlockSpec((1, gw), lambda i: (0, 1))],
        out_specs=_rb(t, SSM_WIDTH),
        name="gated_norm_fwd", compiler_params=_params(("parallel",)))(y, y, proj, proj, w, w)


def _gated_norm_bwd(y, proj, w, dssm):
    lp = y.shape[0]
    t = ROW_BLOCK
    gw = SSM_WIDTH // 2

    def body(y0, y1, z0, z1, w0, w1, d0, d1, dy_ref, dz_ref, dw_ref):
        dws = []
        for g, (yr, zr, wr, dr) in enumerate(((y0, z0, w0, d0), (y1, z1, w1, d1))):
            _, vjp = jax.vjp(_gated_norm_group, yr[...], zr[...], wr[...])
            dyg, dzg, dwg = vjp(dr[...])
            dy_ref[:, g * gw:(g + 1) * gw] = dyg
            dz_ref[:, g * gw:(g + 1) * gw] = dzg
            dws.append(dwg)
        _acc_add(dw_ref, jnp.concatenate(dws, axis=1))

    zb = SEG_Z // gw
    return pl.pallas_call(
        body,
        out_shape=[jax.ShapeDtypeStruct((lp, SSM_WIDTH), F32), jax.ShapeDtypeStruct((lp, SSM_WIDTH), F32),
                   jax.ShapeDtypeStruct((1, SSM_WIDTH), F32)],
        grid=(lp // t,),
        in_specs=[_rb(t, gw, 0), _rb(t, gw, 1), _rb(t, gw, zb), _rb(t, gw, zb + 1),
                  pl.BlockSpec((1, gw), lambda i: (0, 0)), pl.BlockSpec((1, gw), lambda i: (0, 1)),
                  _rb(t, gw, 2), _rb(t, gw, 3)],
        out_specs=[_rb(t, SSM_WIDTH), _rb(t, SSM_WIDTH), _full((1, SSM_WIDTH))],
        name="gated_norm_bwd", compiler_params=_params(("arbitrary",)))(y, y, proj, proj, w, w, dssm, dssm)


def _mix_residual(h0, mix, g_post, g_mlp_pre):
    lp = h0.shape[0]
    t = ROW_BLOCK

    def body(h_ref, m_ref, gp_ref, gm_ref, h1_ref, n2_ref):
        h1 = h_ref[...] + _rms(m_ref[...], gp_ref[...])
        h1_ref[...] = h1
        n2_ref[...] = _rms(h1, gm_ref[...]).astype(BF16)

    return pl.pallas_call(
        body, out_shape=[jax.ShapeDtypeStruct((lp, D_MODEL), F32), jax.ShapeDtypeStruct((lp, D_MODEL), BF16)],
        grid=(lp // t,),
        in_specs=[_rb(t, D_MODEL), _rb(t, D_MODEL), _full((1, D_MODEL)), _full((1, D_MODEL))],
        out_specs=[_rb(t, D_MODEL), _rb(t, D_MODEL)],
        name="mix_residual", compiler_params=_params(("parallel",)))(h0, mix, g_post, g_mlp_pre)


def _loss_and_grad(h1, f, g_post, tgt, n_real):
    lp = h1.shape[0]
    t = ROW_BLOCK

    def body(h1_ref, f_ref, g_ref, t_ref, loss_ref, dh2_ref, df_ref, dg_ref):
        i = pl.program_id(0)
        fx = f_ref[...]
        h2 = h1_ref[...] + _rms(fx, g_ref[...])
        row = i * t + lax.broadcasted_iota(jnp.int32, (t, 1), 0)
        real = jnp.logical_and(row >= N_META, row < N_META + n_real)
        diff = jnp.where(real, h2 - t_ref[...], 0.0)
        part = 0.5 * jnp.sum(jnp.sum(diff * diff, axis=-1, keepdims=True) / D_MODEL, axis=0, keepdims=True)
        _acc_add(loss_ref, jnp.broadcast_to(part, (1, 128)))
        dh2 = diff / D_MODEL
        dh2_ref[...] = dh2
        dfx, dg = _rms_bwd(fx, g_ref[...], dh2)
        df_ref[...] = dfx.astype(BF16)
        _acc_add(dg_ref, dg)

    return pl.pallas_call(
        body,
        out_shape=[jax.ShapeDtypeStruct((1, 128), F32), jax.ShapeDtypeStruct((lp, D_MODEL), F32),
                   jax.ShapeDtypeStruct((lp, D_MODEL), BF16), jax.ShapeDtypeStruct((1, D_MODEL), F32)],
        grid=(lp // t,),
        in_specs=[_rb(t, D_MODEL), _rb(t, D_MODEL), _full((1, D_MODEL)), _rb(t, D_MODEL)],
        out_specs=[_full((1, 128)), _rb(t, D_MODEL), _rb(t, D_MODEL), _full((1, D_MODEL))],
        name="loss_and_grad", compiler_params=_params(("arbitrary",)))(h1, f, g_post, tgt)


def _mlp_residual_bwd(dh2, dn2, h1, g_mlp_pre, mix, g_post):
    lp = h1.shape[0]
    t = ROW_BLOCK

    def body(dh2_ref, dn2_ref, h1_ref, gm_ref, mix_ref, gp_ref, dh1_ref, dmix_ref, dgm_ref, dgp_ref):
        dx, dgm = _rms_bwd(h1_ref[...], gm_ref[...], dn2_ref[...])
        dh1 = dh2_ref[...] + dx
        dh1_ref[...] = dh1
        dmix, dgp = _rms_bwd(mix_ref[...], gp_ref[...], dh1)
        dmix_ref[...] = dmix.astype(BF16)
        _acc_add(dgm_ref, dgm)
        _acc_add(dgp_ref, dgp)

    return pl.pallas_call(
        body,
        out_shape=[jax.ShapeDtypeStruct((lp, D_MODEL), F32), jax.ShapeDtypeStruct((lp, D_MODEL), BF16),
                   jax.ShapeDtypeStruct((1, D_MODEL), F32), jax.ShapeDtypeStruct((1, D_MODEL), F32)],
        grid=(lp // t,),
        in_specs=[_rb(t, D_MODEL), _rb(t, D_MODEL), _rb(t, D_MODEL), _full((1, D_MODEL)), _rb(t, D_MODEL),
                  _full((1, D_MODEL))],
        out_specs=[_rb(t, D_MODEL), _rb(t, D_MODEL), _full((1, D_MODEL)), _full((1, D_MODEL))],
        name="mlp_residual_bwd", compiler_params=_params(("arbitrary",)))(dh2, dn2, h1, g_mlp_pre, mix, g_post)


def _input_norm_bwd(dh1, dn1, h0, g_pre):
    lp = h0.shape[0]
    t = ROW_BLOCK

    def body(dh1_ref, dn1_ref, h0_ref, g_ref, dh0_ref, dg_ref):
        dx, dg = _rms_bwd(h0_ref[...], g_ref[...], dn1_ref[...])
        dh0_ref[...] = dh1_ref[...] + dx
        _acc_add(dg_ref, dg)

    return pl.pallas_call(
        body, out_shape=[jax.ShapeDtypeStruct((lp, D_MODEL), F32), jax.ShapeDtypeStruct((1, D_MODEL), F32)],
        grid=(lp // t,),
        in_specs=[_rb(t, D_MODEL), _rb(t, D_MODEL), _rb(t, D_MODEL), _full((1, D_MODEL))],
        out_specs=[_rb(t, D_MODEL), _full((1, D_MODEL))],
        name="input_norm_bwd", compiler_params=_params(("arbitrary",)))(dh1, dn1, h0, g_pre)


def _conv_bwd(name, dact, pre, proj, seg, width, conv_w):
    lp = proj.shape[0]
    t = ROW_BLOCK
    cb = seg // width
    nblk = lp // t

    def dsilu(p):
        s = _sigmoid(p)
        return s * (1.0 + p * (1.0 - s))

    def body(da_ref, dan_ref, pre_ref, pren_ref, u_ref, halo_ref, w_ref, du_ref, dw_ref, db_ref):
        i = pl.program_id(0)
        dpre = da_ref[...] * dsilu(pre_ref[...])
        dpre_next = jnp.where(i < nblk - 1, dan_ref[...] * dsilu(pren_ref[...]), 0.0)
        extd = jnp.concatenate([dpre, dpre_next], axis=0)
        halo = jnp.where(i > 0, halo_ref[...], 0.0)
        ext = jnp.concatenate([halo, u_ref[...]], axis=0)
        du = jnp.zeros((t, width), F32)
        dws = []
        for k in range(CONV_K):
            m = CONV_K - 1 - k
            ahead = dpre if m == 0 else pltpu.roll(extd, t + 8 - m, 0)[:t, :]
            du = du + w_ref[k:k + 1, :] * ahead
            dws.append(jnp.sum(dpre * _shifted(ext, t, m), axis=0, keepdims=True))
        du_ref[...] = du
        _acc_add(dw_ref, jnp.concatenate(dws + [jnp.zeros((8 - CONV_K, width), F32)], axis=0))
        _acc_add(db_ref, jnp.sum(dpre, axis=0, keepdims=True))

    nxt = lambda i: (jnp.minimum((i + 1) * (t // 8), lp // 8 - 1), 0)
    return pl.pallas_call(
        body,
        out_shape=[jax.ShapeDtypeStruct((lp, width), F32), jax.ShapeDtypeStruct((8, width), F32),
                   jax.ShapeDtypeStruct((1, width), F32)],
        grid=(nblk,),
        in_specs=[_rb(t, width), pl.BlockSpec((8, width), nxt), _rb(t, width), pl.BlockSpec((8, width), nxt),
                  _rb(t, width, cb),
                  pl.BlockSpec((8, width), lambda i: (jnp.maximum(i * (t // 8) - 1, 0), cb)),
                  _full((CONV_K, width))],
        out_specs=[_rb(t, width), _full((8, width)), _full((1, width))],
        name=name, compiler_params=_params(("arbitrary",)))(dact, dact, pre, pre, proj, proj, conv_w)


def _qk_unpack_bwd(dqs, dks, dvs, cos_t, sin_t):
    lp = dqs.shape[1]
    t = ROW_BLOCK

    def body(dqs_ref, dks_ref, dvs_ref, cos_ref, sin_ref, dq_ref, dkv_ref, dkr_ref):
        lane = lax.broadcasted_iota(jnp.int32, (t, 128), 1)
        lo = lane < 64
        for j in range(ATT_HEADS // 2):
            dpr = jnp.where(lo, dqs_ref[2 * j, :, 128:256], dqs_ref[2 * j + 1, :, 128:256])
            dq_ref[:, 1024 + 128 * j:1024 + 128 * (j + 1)] = _rope_bwd(dpr, cos_ref[...], sin_ref[...]).astype(BF16)
        dkrr = jnp.zeros((t, 128), F32)
        for h in range(ATT_HEADS):
            dq_ref[:, 128 * h:128 * (h + 1)] = dqs_ref[h, :, 0:128].astype(BF16)
            dkv_ref[:, 256 * h:256 * h + 128] = dks_ref[h, :, 0:128].astype(BF16)
            dkv_ref[:, 256 * h + 128:256 * (h + 1)] = dvs_ref[h].astype(BF16)
            dkrr = dkrr + dks_ref[h, :, 128:256]
        droped = jnp.where(lo, dkrr + pltpu.roll(dkrr, 64, 1), 0.0)
        dkr_ref[...] = _rope_bwd(droped, cos_ref[...], sin_ref[...])

    slab = lambda w: pl.BlockSpec((ATT_HEADS, t, w), lambda i: (0, i, 0))
    return pl.pallas_call(
        body,
        out_shape=[jax.ShapeDtypeStruct((lp, 1536), BF16), jax.ShapeDtypeStruct((lp, 2048), BF16),
                   jax.ShapeDtypeStruct((lp, 128), F32)],
        grid=(lp // t,),
        in_specs=[slab(QP_W), slab(QP_W), slab(V_HEAD), _rb(t, 128), _rb(t, 128)],
        out_specs=[_rb(t, 1536), _rb(t, 2048), _rb(t, 128)],
        name="qk_unpack_bwd", compiler_params=_params(("parallel",)))(dqs, dks, dvs, cos_t, sin_t)


def _proj_grad(proj, dcqn, dckvn, g_q, g_kv, dkr, ddt_pad, dt_bias_pad, dz, dxs, dbc):
    lp = proj.shape[0]
    t = ROW_BLOCK

    def body(ckv_ref, cq_ref, pdt_ref, dcq_ref, dckv_ref, gq_ref, gkv_ref, dkr_ref, ddt_ref, b_ref, dz_ref, dxs_ref,
             dbc_ref, dp_ref, dgq_ref, dgkv_ref, db_ref):
        dckv, dgkv = _rms_bwd(ckv_ref[...], gkv_ref[...], dckv_ref[...])
        dcq, dgq = _rms_bwd(cq_ref[...], gq_ref[...], dcq_ref[...])
        ddt_raw = ddt_ref[...] * _sigmoid(pdt_ref[...] + b_ref[...])
        dp_ref[:, SEG_KV:SEG_KV + KV_LORA] = dckv.astype(BF16)
        dp_ref[:, SEG_KR:SEG_KR + 128] = dkr_ref[...].astype(BF16)
        dp_ref[:, SEG_CQ:SEG_CQ + Q_LORA] = dcq.astype(BF16)
        dp_ref[:, SEG_DT:SEG_DT + 128] = ddt_raw.astype(BF16)
        dp_ref[:, SEG_DT + 128:SEG_Z] = jnp.zeros((t, SEG_Z - SEG_DT - 128), BF16)
        dp_ref[:, SEG_Z:SEG_XS] = dz_ref[...].astype(BF16)
        dp_ref[:, SEG_XS:SEG_BC] = dxs_ref[...].astype(BF16)
        dp_ref[:, SEG_BC:PROJ_W] = dbc_ref[...].astype(BF16)
        _acc_add(dgq_ref, dgq)
        _acc_add(dgkv_ref, dgkv)
        _acc_add(db_ref, jnp.sum(ddt_raw, axis=0, keepdims=True))

    return pl.pallas_call(
        body,
        out_shape=[jax.ShapeDtypeStruct((lp, PROJ_W), BF16), jax.ShapeDtypeStruct((1, Q_LORA), F32),
                   jax.ShapeDtypeStruct((1, KV_LORA), F32), jax.ShapeDtypeStruct((1, 128), F32)],
        grid=(lp // t,),
        in_specs=[_rb(t, KV_LORA, SEG_KV // KV_LORA), _rb(t, Q_LORA, SEG_CQ // Q_LORA), _rb(t, 128, SEG_DT // 128),
                  _rb(t, Q_LORA), _rb(t, KV_LORA), _full((1, Q_LORA)), _full((1, KV_LORA)), _rb(t, 128), _rb(t, 128),
                  _full((1, 128)), _rb(t, SSM_WIDTH), _rb(t, SSM_WIDTH), _rb(t, 512)],
        out_specs=[_rb(t, PROJ_W), _full((1, Q_LORA)), _full((1, KV_LORA)), _full((1, 128))],
        name="proj_grad", compiler_params=_params(("arbitrary",)))(
            proj, proj, proj, dcqn, dckvn, g_q, g_kv, dkr, ddt_pad, dt_bias_pad, dz, dxs, dbc)


def _pair_tables(n):
    qmaj = [(i, j) for i in range(n) for j in range(i + 1)]
    kmaj = [(i, j) for j in range(n) for i in range(j, n)]
    to = lambda ps, c: jnp.asarray(np.array([p[c] for p in ps], np.int32))
    return (to(qmaj, 0), to(qmaj, 1)), (to(kmaj, 0), to(kmaj, 1))


def _nt(a, b):
    return lax.dot_general(a, b, (((1,), (1,)), ((), ())), preferred_element_type=F32)


def _attn_fwd(qs, ks, vs):
    lp = qs.shape[1]
    t = MM_BLOCK
    n = lp // t
    (qi, kj), _ = _pair_tables(n)

    def body(qi_ref, kj_ref, q_ref, k_ref, v_ref, o_ref, lse_ref, m_s, l_s, acc_s):
        p = pl.program_id(1)
        i, j = qi_ref[p], kj_ref[p]

        @pl.when(j == 0)
        def _():
            m_s[...] = jnp.full_like(m_s, NEG_BIG)
            l_s[...] = jnp.zeros_like(l_s)
            acc_s[...] = jnp.zeros_like(acc_s)

        s = _nt(q_ref[0], k_ref[0]) * ATT_SCALE

        def update(sc):
            m_prev = m_s[...]
            m_new = jnp.maximum(m_prev, jnp.max(sc, axis=-1, keepdims=True))
            alpha = jnp.exp(m_prev - m_new)
            pexp = jnp.exp(sc - m_new)
            l_s[...] = alpha * l_s[...] + jnp.sum(pexp, axis=-1, keepdims=True)
            acc_s[...] = alpha * acc_s[...] + jnp.dot(pexp.astype(BF16), v_ref[0], preferred_element_type=F32)
            m_s[...] = m_new

        @pl.when(j < i)
        def _():
            update(s)

        @pl.when(j == i)
        def _():
            keep = lax.broadcasted_iota(jnp.int32, (t, t), 0) >= lax.broadcasted_iota(jnp.int32, (t, t), 1)
            update(jnp.where(keep, s, NEG_BIG))
            o_ref[...] = acc_s[...] / l_s[...]
            lse_ref[0] = m_s[...] + jnp.log(l_s[...])

    grid_spec = pltpu.PrefetchScalarGridSpec(
        num_scalar_prefetch=2, grid=(ATT_HEADS, int(qi.shape[0])),
        in_specs=[pl.BlockSpec((1, t, QP_W), lambda h, p, qi, kj: (h, qi[p], 0)),
                  pl.BlockSpec((1, t, QP_W), lambda h, p, qi, kj: (h, kj[p], 0)),
                  pl.BlockSpec((1, t, V_HEAD), lambda h, p, qi, kj: (h, kj[p], 0))],
        out_specs=[pl.BlockSpec((t, V_HEAD), lambda h, p, qi, kj: (qi[p], h)),
                   pl.BlockSpec((1, t, 1), lambda h, p, qi, kj: (h, qi[p], 0))],
        scratch_shapes=[pltpu.VMEM((t, 1), F32), pltpu.VMEM((t, 1), F32), pltpu.VMEM((t, V_HEAD), F32)])
    return pl.pallas_call(
        body, grid_spec=grid_spec,
        out_shape=[jax.ShapeDtypeStruct((lp, ATT_HEADS * V_HEAD), F32), jax.ShapeDtypeStruct((ATT_HEADS, lp, 1), F32)],
        name="attn_fwd", compiler_params=_params(("parallel", "arbitrary")))(qi, kj, qs, ks, vs)


def _attn_bwd_q(qs, ks, vs, datt, att, lse):
    lp = qs.shape[1]
    t = MM_BLOCK
    n = lp // t
    (qi, kj), _ = _pair_tables(n)

    def body(qi_ref, kj_ref, q_ref, k_ref, v_ref, do_ref, o_ref, lse_ref, dq_ref, dl_ref, dq_s, dl_s):
        p = pl.program_id(1)
        i, j = qi_ref[p], kj_ref[p]

        @pl.when(j == 0)
        def _():
            dq_s[...] = jnp.zeros_like(dq_s)
            dl_s[...] = jnp.sum(do_ref[...] * o_ref[...], axis=-1, keepdims=True)

        s = _nt(q_ref[0], k_ref[0]) * ATT_SCALE
        dp = _nt(do_ref[...].astype(BF16), v_ref[0])

        def step(pr):
            ds = pr * (dp - dl_s[...]) * ATT_SCALE
            dq_s[...] += jnp.dot(ds.astype(BF16), k_ref[0], preferred_element_type=F32)

        @pl.when(j < i)
        def _():
            step(jnp.exp(s - lse_ref[0]))

        @pl.when(j == i)
        def _():
            keep = lax.broadcasted_iota(jnp.int32, (t, t), 0) >= lax.broadcasted_iota(jnp.int32, (t, t), 1)
            step(jnp.where(keep, jnp.exp(s - lse_ref[0]), 0.0))
            dq_ref[0] = dq_s[...]
            dl_ref[0] = dl_s[...]

    grid_spec = pltpu.PrefetchScalarGridSpec(
        num_scalar_prefetch=2, grid=(ATT_HEADS, int(qi.shape[0])),
        in_specs=[pl.BlockSpec((1, t, QP_W), lambda h, p, qi, kj: (h, qi[p], 0)),
                  pl.BlockSpec((1, t, QP_W), lambda h, p, qi, kj: (h, kj[p], 0)),
                  pl.BlockSpec((1, t, V_HEAD), lambda h, p, qi, kj: (h, kj[p], 0)),
                  pl.BlockSpec((t, V_HEAD), lambda h, p, qi, kj: (qi[p], h)),
                  pl.BlockSpec((t, V_HEAD), lambda h, p, qi, kj: (qi[p], h)),
                  pl.BlockSpec((1, t, 1), lambda h, p, qi, kj: (h, qi[p], 0))],
        out_specs=[pl.BlockSpec((1, t, QP_W), lambda h, p, qi, kj: (h, qi[p], 0)),
                   pl.BlockSpec((1, t, 1), lambda h, p, qi, kj: (h, qi[p], 0))],
        scratch_shapes=[pltpu.VMEM((t, QP_W), F32), pltpu.VMEM((t, 1), F32)])
    return pl.pallas_call(
        body, grid_spec=grid_spec,
        out_shape=[jax.ShapeDtypeStruct((ATT_HEADS, lp, QP_W), F32), jax.ShapeDtypeStruct((ATT_HEADS, lp, 1), F32)],
        name="attn_bwd_q", compiler_params=_params(("parallel", "arbitrary")))(qi, kj, qs, ks, vs, datt, att, lse)


def _attn_bwd_kv(qs, ks, vs, datt, lse_row, delta_row):
    lp = qs.shape[1]
    t = MM_BLOCK
    n = lp // t
    _, (qi, kj) = _pair_tables(n)

    def body(qi_ref, kj_ref, k_ref, v_ref, q_ref, do_ref, lse_ref, dl_ref, dk_ref, dv_ref, dk_s, dv_s):
        p = pl.program_id(1)
        i, j = qi_ref[p], kj_ref[p]

        @pl.when(i == j)
        def _():
            dk_s[...] = jnp.zeros_like(dk_s)
            dv_s[...] = jnp.zeros_like(dv_s)

        do = do_ref[...].astype(BF16)
        st = _nt(k_ref[0], q_ref[0]) * ATT_SCALE
        dpt = _nt(v_ref[0], do)

        def step(pt):
            dst = pt * (dpt - dl_ref[0]) * ATT_SCALE
            dv_s[...] += jnp.dot(pt.astype(BF16), do, preferred_element_type=F32)
            dk_s[...] += jnp.dot(dst.astype(BF16), q_ref[0], preferred_element_type=F32)

        @pl.when(i > j)
        def _():
            step(jnp.exp(st - lse_ref[0]))

        @pl.when(i == j)
        def _():
            keep = lax.broadcasted_iota(jnp.int32, (t, t), 1) >= lax.broadcasted_iota(jnp.int32, (t, t), 0)
            step(jnp.where(keep, jnp.exp(st - lse_ref[0]), 0.0))

        @pl.when(i == n - 1)
        def _():
            dk_ref[0] = dk_s[...]
            dv_ref[0] = dv_s[...]

    grid_spec = pltpu.PrefetchScalarGridSpec(
        num_scalar_prefetch=2, grid=(ATT_HEADS, int(qi.shape[0])),
        in_specs=[pl.BlockSpec((1, t, QP_W), lambda h, p, qi, kj: (h, kj[p], 0)),
                  pl.BlockSpec((1, t, V_HEAD), lambda h, p, qi, kj: (h, kj[p], 0)),
                  pl.BlockSpec((1, t, QP_W), lambda h, p, qi, kj: (h, qi[p], 0)),
                  pl.BlockSpec((t, V_HEAD), lambda h, p, qi, kj: (qi[p], h)),
                  pl.BlockSpec((1, 1, t), lambda h, p, qi, kj: (h, 0, qi[p])),
                  pl.BlockSpec((1, 1, t), lambda h, p, qi, kj: (h, 0, qi[p]))],
        out_specs=[pl.BlockSpec((1, t, QP_W), lambda h, p, qi, kj: (h, kj[p], 0)),
                   pl.BlockSpec((1, t, V_HEAD), lambda h, p, qi, kj: (h, kj[p], 0))],
        scratch_shapes=[pltpu.VMEM((t, QP_W), F32), pltpu.VMEM((t, V_HEAD), F32)])
    return pl.pallas_call(
        body, grid_spec=grid_spec,
        out_shape=[jax.ShapeDtypeStruct((ATT_HEADS, lp, QP_W), F32), jax.ShapeDtypeStruct((ATT_HEADS, lp, V_HEAD), F32)],
        name="attn_bwd_kv", compiler_params=_params(("parallel", "arbitrary")))(
            qi, kj, ks, vs, qs, datt, lse_row, delta_row)


N_PAIRS = SSM_HEADS // 2
HI = lax.Precision.HIGHEST


def _ssd_chunk(xp, bs, cs, dt, dt_t, alr, alc, dsk, st):
    q = dt.shape[0]
    li = lax.broadcasted_iota(jnp.int32, (q, q), 0)
    si = lax.broadcasted_iota(jnp.int32, (q, q), 1)
    tri = (si <= li).astype(F32)
    tri_t = (li <= si).astype(F32)
    lo = lax.broadcasted_iota(jnp.int32, (1, 128), 1) < 64
    h_r = lax.broadcasted_iota(jnp.int32, (1, SSM_HEADS), 1)
    h_c = lax.broadcasted_iota(jnp.int32, (SSM_HEADS, 1), 0)
    a = dt * (-jnp.exp(alr))
    a_t = dt_t * (-jnp.exp(alc))
    acum = jnp.dot(tri, a, precision=HI, preferred_element_type=F32)
    acum_t = jnp.dot(a_t, tri_t, precision=HI, preferred_element_type=F32)
    last = (lax.broadcasted_iota(jnp.int32, (q, 1), 0) == q - 1).astype(F32)
    alast = jnp.sum(acum * last, axis=0, keepdims=True)
    e = jnp.exp(acum)
    rdt = jnp.exp(alast - acum) * dt
    e_last = jnp.exp(alast)

    def col(m, h):
        return jnp.sum(m * (h_r == h).astype(F32), axis=1, keepdims=True)

    def row(m, h):
        return jnp.sum(m * (h_c == h).astype(F32), axis=0, keepdims=True)

    def pair(m, ha):
        return jnp.where(lo, col(m, ha), col(m, ha + 1))

    ys, st_new = [], []
    for g in range(2):
        c_b = cs[g].astype(BF16)
        b_b = bs[g].astype(BF16)
        cb = _nt(c_b, b_b)
        for j in range(N_PAIRS // 2):
            p = (N_PAIRS // 2) * g + j
            ha = 2 * p
            x = xp[p]
            x_b = x.astype(BF16)

            def w_of(h):
                seg = col(acum, h) - row(acum_t, h)
                return (cb * jnp.exp(jnp.minimum(seg, 0.0)) * tri * row(dt_t, h)).astype(BF16)

            y_diag = jnp.where(lo, jnp.dot(w_of(ha), x_b, preferred_element_type=F32),
                               jnp.dot(w_of(ha + 1), x_b, preferred_element_type=F32))
            y_off = jnp.dot(c_b, st[p].astype(BF16), preferred_element_type=F32) * pair(e, ha)
            ys.append(y_diag + y_off + pair(dsk, ha) * x)
            xw = (x * pair(rdt, ha)).astype(BF16)
            st_new.append(st[p] * pair(e_last, ha)
                          + lax.dot_general(b_b, xw, (((0,), (0,)), ((), ())), preferred_element_type=F32))
    return ys, st_new


def _ssd_fwd(xs, bc, dt, dt_t, alr, alc, dsk):
    lp = xs.shape[0]
    q = SSD_CHUNK
    nc = lp // q

    def body(x_ref, b_ref, c_ref, dt_ref, dtt_ref, alr_ref, alc_ref, dsk_ref, y_ref, sp_ref, st_s):
        @pl.when(pl.program_id(0) == 0)
        def _():
            st_s[...] = jnp.zeros_like(st_s)

        sp_ref[0] = st_s[...]
        xp = [x_ref[:, 128 * p:128 * (p + 1)] for p in range(N_PAIRS)]
        bs = [b_ref[:, 0:128], b_ref[:, 128:256]]
        cs = [c_ref[:, 0:128], c_ref[:, 128:256]]
        ys, st_new = _ssd_chunk(xp, bs, cs, dt_ref[...], dtt_ref[...], alr_ref[...], alc_ref[...], dsk_ref[...],
                                [st_s[p] for p in range(N_PAIRS)])
        for p in range(N_PAIRS):
            y_ref[:, 128 * p:128 * (p + 1)] = ys[p]
            st_s[p] = st_new[p]

    return pl.pallas_call(
        body,
        out_shape=[jax.ShapeDtypeStruct((lp, SSM_WIDTH), F32), jax.ShapeDtypeStruct((nc, N_PAIRS, 128, 128), F32)],
        grid=(nc,),
        in_specs=[_rb(q, SSM_WIDTH), _rb(q, 256, 0), _rb(q, 256, 1), _rb(q, SSM_HEADS),
                  pl.BlockSpec((SSM_HEADS, q), lambda i: (0, i)),
                  _full((1, SSM_HEADS)), _full((SSM_HEADS, 1)), _full((1, SSM_HEADS))],
        out_specs=[_rb(q, SSM_WIDTH), pl.BlockSpec((1, N_PAIRS, 128, 128), lambda i: (i, 0, 0, 0))],
        scratch_shapes=[pltpu.VMEM((N_PAIRS, 128, 128), F32)],
        name="ssd_fwd", compiler_params=_params(("arbitrary",)))(xs, bc, bc, dt, dt_t, alr, alc, dsk)


def _ssd_bwd(xs, bc, dt, dt_t, alr, alc, dsk, sprev, dy):
    lp = xs.shape[0]
    q = SSD_CHUNK
    nc = lp // q

    def body(x_ref, b_ref, c_ref, dt_ref, dtt_ref, alr_ref, alc_ref, dsk_ref, sp_ref, dy_ref,
             dx_ref, dbc_ref, ddt_ref, ddtt_ref, dalr_ref, dalc_ref, ddsk_ref, ds_s):
        @pl.when(pl.program_id(0) == 0)
        def _():
            ds_s[...] = jnp.zeros_like(ds_s)

        xp = [x_ref[:, 128 * p:128 * (p + 1)] for p in range(N_PAIRS)]
        bs = [b_ref[:, 0:128], b_ref[:, 128:256]]
        cs = [c_ref[:, 0:128], c_ref[:, 128:256]]
        st = [sp_ref[0, p] for p in range(N_PAIRS)]
        _, vjp = jax.vjp(_ssd_chunk, xp, bs, cs, dt_ref[...], dtt_ref[...], alr_ref[...], alc_ref[...], dsk_ref[...],
                         st)
        dys = [dy_ref[:, 128 * p:128 * (p + 1)] for p in range(N_PAIRS)]
        dxp, dbs, dcs, ddt, ddtt, dalr, dalc, ddsk, dst = vjp((dys, [ds_s[p] for p in range(N_PAIRS)]))
        for p in range(N_PAIRS):
            dx_ref[:, 128 * p:128 * (p + 1)] = dxp[p]
            ds_s[p] = dst[p]
        for g in range(2):
            dbc_ref[:, 128 * g:128 * (g + 1)] = dbs[g]
            dbc_ref[:, 256 + 128 * g:256 + 128 * (g + 1)] = dcs[g]
        ddt_ref[...] = ddt
        ddtt_ref[...] = ddtt
        _acc_add(dalr_ref, dalr)
        _acc_add(dalc_ref, dalc)
        _acc_add(ddsk_ref, ddsk)

    rev = lambda width, cb=0: pl.BlockSpec((q, width), lambda i: (nc - 1 - i, cb))
    return pl.pallas_call(
        body,
        out_shape=[jax.ShapeDtypeStruct((lp, SSM_WIDTH), F32), jax.ShapeDtypeStruct((lp, 512), F32),
                   jax.ShapeDtypeStruct((lp, SSM_HEADS), F32), jax.ShapeDtypeStruct((SSM_HEADS, lp), F32),
                   jax.ShapeDtypeStruct((1, SSM_HEADS), F32), jax.ShapeDtypeStruct((SSM_HEADS, 1), F32),
                   jax.ShapeDtypeStruct((1, SSM_HEADS), F32)],
        grid=(nc,),
        in_specs=[rev(SSM_WIDTH), rev(256, 0), rev(256, 1), rev(SSM_HEADS),
                  pl.BlockSpec((SSM_HEADS, q), lambda i: (0, nc - 1 - i)),
                  _full((1, SSM_HEADS)), _full((SSM_HEADS, 1)), _full((1, SSM_HEADS)),
                  pl.BlockSpec((1, N_PAIRS, 128, 128), lambda i: (nc - 1 - i, 0, 0, 0)), rev(SSM_WIDTH)],
        out_specs=[rev(SSM_WIDTH), rev(512), rev(SSM_HEADS), pl.BlockSpec((SSM_HEADS, q), lambda i: (0, nc - 1 - i)),
                   _full((1, SSM_HEADS)), _full((SSM_HEADS, 1)), _full((1, SSM_HEADS))],
        scratch_shapes=[pltpu.VMEM((N_PAIRS, 128, 128), F32)],
        name="ssd_bwd", compiler_params=_params(("arbitrary",)))(xs, bc, bc, dt, dt_t, alr, alc, dsk, sprev, dy)


def _q_to_slab_order(w):
    hd = QK_NOPE + QK_ROPE
    nope = [w[:, h * hd:h * hd + QK_NOPE] for h in range(ATT_HEADS)]
    rope = [w[:, h * hd + QK_NOPE:(h + 1) * hd] for h in range(ATT_HEADS)]
    return jnp.concatenate(nope + rope, axis=1)


def _q_from_slab_order(wp):
    base = ATT_HEADS * QK_NOPE
    parts = []
    for h in range(ATT_HEADS):
        parts += [wp[:, QK_NOPE * h:QK_NOPE * (h + 1)], wp[:, base + QK_ROPE * h:base + QK_ROPE * (h + 1)]]
    return jnp.concatenate(parts, axis=1)


_IN_CQ, _IN_CKV, _IN_KR, _IN_Z, _IN_XS, _IN_BC, _IN_DT = (0, 384), (384, 640), (640, 704), (704, 1728), (1728, 2752), \
    (2752, 3264), (3264, 3280)


def _pack_w_in(w):
    z = lambda n: jnp.zeros((w.shape[0], n), w.dtype)
    s = lambda r: w[:, r[0]:r[1]]
    return jnp.concatenate([s(_IN_CKV), s(_IN_KR), z(64), s(_IN_CQ), s(_IN_DT), z(112), z(128), s(_IN_Z), s(_IN_XS),
                            s(_IN_BC)], axis=1)


def _unpack_w_in(wp):
    s = lambda off, n: wp[:, off:off + n]
    return jnp.concatenate([s(SEG_CQ, 384), s(SEG_KV, 256), s(SEG_KR, 64), s(SEG_Z, 1024), s(SEG_XS, 1024),
                            s(SEG_BC, 512), s(SEG_DT, 16)], axis=1)


def _rope_tables(lp):
    inv_freq = ROPE_THETA ** (-jnp.arange(0, QK_ROPE, 2, dtype=F32) / QK_ROPE)
    ang = jnp.arange(lp, dtype=F32)[:, None] * inv_freq[None, :]
    cos, sin = jnp.cos(ang), jnp.sin(ang)
    return jnp.tile(cos, (1, 4)), jnp.concatenate([-sin, sin, -sin, sin], axis=1)


def _local_step(x, tgt, w):
    n_real = x.shape[0]
    l = N_META + n_real
    lp = -(-l // MM_BLOCK) * MM_BLOCK
    pad = lambda a: jnp.concatenate([a, jnp.zeros((lp - l, D_MODEL), F32)], axis=0)
    h0 = pad(jnp.concatenate([w["meta_tokens"], x], axis=0))
    tgt_p = pad(jnp.concatenate([jnp.zeros((N_META, D_MODEL), F32), tgt], axis=0))
    cos_t, sin_t = _rope_tables(lp)

    w_in_p = _pack_w_in(w["w_in"])
    w_q_p = _q_to_slab_order(w["w_q_up"])
    w_kv, w_out, w_up, w_down = w["w_kv_up"], w["w_out"], w["w_mlp_up"], w["w_mlp_down"]
    conv_w, conv_b = w["conv_w"], w["conv_b"]
    dt_bias_pad = jnp.concatenate([w["dt_bias"], jnp.zeros((1, 128 - SSM_HEADS), F32)], axis=1)
    alr, dsk = w["a_log"], w["d_skip"]
    alc = alr.reshape(SSM_HEADS, 1)

    n1 = _norm_in(h0, w["norm_mix_pre"])
    proj = _mm("proj", n1, w_in_p)
    cqn, ckvn, krr = _attn_prep(proj, w["q_a_norm"], w["kv_a_norm"], cos_t, sin_t)
    q = _mm("q_up", cqn, w_q_p)
    kv = _mm("kv_up", ckvn, w_kv)
    qs, ks, vs = _qk_pack(q, kv, krr, cos_t, sin_t)
    att, lse = _attn_fwd(qs, ks, vs)
    xs_pre, xs_act = _conv_fwd("conv_xs_fwd", proj, SEG_XS, SSM_WIDTH, conv_w[:, :SSM_WIDTH], conv_b[:, :SSM_WIDTH])
    bc_pre, bc_act = _conv_fwd("conv_bc_fwd", proj, SEG_BC, 512, conv_w[:, SSM_WIDTH:], conv_b[:, SSM_WIDTH:])
    dt = _dt_fwd(proj, dt_bias_pad)[:, :SSM_HEADS]
    dt_t = dt.T
    y, sprev = _ssd_fwd(xs_act, bc_act, dt, dt_t, alr, alc, dsk)
    ssm = _gated_norm_fwd(y, proj, w["ssm_norm"])
    cat = jnp.concatenate([att.astype(BF16), ssm], axis=1)
    mix = _mm("out_proj", cat, w_out)
    h1, n2 = _mix_residual(h0, mix, w["norm_mix_post"], w["norm_mlp_pre"])
    relu2 = lambda r: jnp.square(jnp.maximum(r, 0.0))
    u, act = _mm("mlp_up", n2, w_up, outs=((F32, None), (BF16, relu2)))
    f = _mm("mlp_down", act, w_down)
    loss, dh2, df, dg_mlp_post = _loss_and_grad(h1, f, w["norm_mlp_post"], tgt_p, n_real)

    g = {"norm_mlp_post": dg_mlp_post}
    g["w_mlp_down"] = _mm_tn("d_w_mlp_down", act, df)
    du = _mm("d_mlp_act", df, w_down.T, outs=((BF16, lambda r, ub: r * (2.0 * jnp.maximum(ub, 0.0))),), epi_ins=(u,))
    g["w_mlp_up"] = _mm_tn("d_w_mlp_up", n2, du)
    dn2 = _mm("d_n2", du, w_up.T)
    dh1, dmix, g["norm_mlp_pre"], g["norm_mix_post"] = _mlp_residual_bwd(dh2, dn2, h1, w["norm_mlp_pre"], mix,
                                                                         w["norm_mix_post"])
    g["w_out"] = _mm_tn("d_w_out", cat, dmix)
    dcat = _mm("d_cat", dmix, w_out.T)
    dy, dz, g["ssm_norm"] = _gated_norm_bwd(y, proj, w["ssm_norm"], dcat)
    dxs_act, dbc_act, ddt, ddt_t, dalr, dalc, g["d_skip"] = _ssd_bwd(xs_act, bc_act, dt, dt_t, alr, alc, dsk, sprev, dy)
    g["a_log"] = dalr + dalc.reshape(1, SSM_HEADS)
    dxs, dcw_xs, dcb_xs = _conv_bwd("conv_xs_bwd", dxs_act, xs_pre, proj, SEG_XS, SSM_WIDTH, conv_w[:, :SSM_WIDTH])
    dbc, dcw_bc, dcb_bc = _conv_bwd("conv_bc_bwd", dbc_act, bc_pre, proj, SEG_BC, 512, conv_w[:, SSM_WIDTH:])
    g["conv_w"] = jnp.concatenate([dcw_xs[:CONV_K], dcw_bc[:CONV_K]], axis=1)
    g["conv_b"] = jnp.concatenate([dcb_xs, dcb_bc], axis=1)
    ddt_pad = jnp.concatenate([ddt + ddt_t.T, jnp.zeros((lp, 128 - SSM_HEADS), F32)], axis=1)

    dqs, delta = _attn_bwd_q(qs, ks, vs, dcat, att, lse)
    dks, dvs = _attn_bwd_kv(qs, ks, vs, dcat, lse.reshape(ATT_HEADS, 1, lp), delta.reshape(ATT_HEADS, 1, lp))
    dq, dkv, dkr = _qk_unpack_bwd(dqs, dks, dvs, cos_t, sin_t)
    g["w_q_up"] = _q_from_slab_order(_mm_tn("d_w_q_up", cqn, dq))
    g["w_kv_up"] = _mm_tn("d_w_kv_up", ckvn, dkv)
    dcqn = _mm("d_cqn", dq, w_q_p.T)
    dckvn = _mm("d_ckvn", dkv, w_kv.T)
    dproj, g["q_a_norm"], g["kv_a_norm"], ddtb = _proj_grad(proj, dcqn, dckvn, w["q_a_norm"], w["kv_a_norm"], dkr,
                                                          ddt_pad, dt_bias_pad, dz, dxs, dbc)
    g["dt_bias"] = ddtb[:, :SSM_HEADS]
    g["w_in"] = _unpack_w_in(_mm_tn("d_w_in", n1, dproj))
    dn1 = _mm("d_n1", dproj, w_in_p.T)
    dh0, g["norm_mix_pre"] = _input_norm_bwd(dh1, dn1, h0, w["norm_mix_pre"])
    g["meta_tokens"] = dh0[:N_META]
    return loss, dh0, g


WEIGHTS = ["meta_tokens", "norm_mix_pre", "w_in", "q_a_norm", "w_q_up", "kv_a_norm", "w_kv_up", "conv_w", "conv_b",
           "dt_bias", "a_log", "d_skip", "ssm_norm", "w_out", "norm_mix_post", "norm_mlp_pre", "w_mlp_up",
           "w_mlp_down", "norm_mlp_post"]
SHARD_AXIS = {"meta_tokens": 1, "w_in": 1, "w_q_up": 1, "w_kv_up": 1, "conv_w": 1, "w_out": 0, "w_mlp_up": 1,
              "w_mlp_down": 0}
FULL_SHAPE = {"meta_tokens": (16, 1024), "norm_mix_pre": (1, 1024), "w_in": (1024, 3280), "q_a_norm": (1, 384),
              "w_q_up": (384, 1536), "kv_a_norm": (1, 256), "w_kv_up": (256, 2048), "conv_w": (4, 1536),
              "conv_b": (1, 1536), "dt_bias": (1, 16), "a_log": (1, 16), "d_skip": (1, 16), "ssm_norm": (1, 1024),
              "w_out": (2048, 1024), "norm_mix_post": (1, 1024), "norm_mlp_pre": (1, 1024), "w_mlp_up": (1024, 4096),
              "w_mlp_down": (4096, 1024), "norm_mlp_post": (1, 1024)}
GATHER_BF16 = ["w_in", "w_q_up", "w_kv_up", "w_out", "w_mlp_up", "w_mlp_down"]
GATHER_F32 = ["meta_tokens", "conv_w"]


def _shard_shape(name):
    shp = list(FULL_SHAPE[name])
    if name in SHARD_AXIS:
        shp[SHARD_AXIS[name]] //= N_CHIPS
    return tuple(shp)


def _pack_rows(arrays, row_multiple):
    flat = jnp.concatenate([a.reshape(-1) for a in arrays])
    n = flat.shape[0]
    rows = -(-n // (PACK_W * row_multiple)) * row_multiple
    return jnp.concatenate([flat, jnp.zeros((rows * PACK_W - n,), flat.dtype)]).reshape(rows, PACK_W)


def _unpack_rows(packed, shapes):
    flat = packed.reshape(-1)
    out, off = [], 0
    for shp in shapes:
        n = int(np.prod(shp))
        out.append(flat[off:off + n].reshape(shp))
        off += n
    return out


def _chip_slice(full, name, t):
    if name not in SHARD_AXIS:
        return full
    ax = SHARD_AXIS[name]
    n = FULL_SHAPE[name][ax] // N_CHIPS
    return lax.slice_in_dim(full, t * n, (t + 1) * n, axis=ax)


HBM_SPEC = pl.BlockSpec(memory_space=pl.ANY)
CHIP_FLIPS = ((1, 0), (0, 1), (1, 1))


def _gather_chips(bufs):
    nb = len(bufs)

    def body(*refs):
        ins, outs = refs[:nb], refs[nb:2 * nb]
        send, recv, loc = refs[2 * nb:]
        x, y, c = lax.axis_index("x"), lax.axis_index("y"), lax.axis_index("c")
        me = 2 * x + y
        copies = []
        for b in range(nb):
            own = pltpu.make_async_copy(ins[b], outs[b].at[me], loc.at[b])
            own.start()
            copies.append(own)
            for k, (fx, fy) in enumerate(CHIP_FLIPS):
                cp = pltpu.make_async_remote_copy(
                    src_ref=ins[b], dst_ref=outs[b].at[me], send_sem=send.at[b, k], recv_sem=recv.at[b, k],
                    device_id=(x ^ fx, y ^ fy, c), device_id_type=MESH_ID)
                cp.start()
                copies.append(cp)
        for cp in copies:
            cp.wait()

    return pl.pallas_call(
        body,
        out_shape=[jax.ShapeDtypeStruct((N_CHIPS,) + b.shape, b.dtype) for b in bufs],
        in_specs=[HBM_SPEC] * nb, out_specs=[HBM_SPEC] * nb,
        scratch_shapes=[pltpu.SemaphoreType.DMA((nb, 3)), pltpu.SemaphoreType.DMA((nb, 3)),
                        pltpu.SemaphoreType.DMA((nb,))],
        name="gather_chips")(*bufs)


def _sibling_swap(name, buf):
    def body(src, dst, send, recv):
        x, y, c = lax.axis_index("x"), lax.axis_index("y"), lax.axis_index("c")
        cp = pltpu.make_async_remote_copy(src_ref=src, dst_ref=dst, send_sem=send, recv_sem=recv,
                                          device_id=(x, y, 1 - c), device_id_type=MESH_ID)
        cp.start()
        cp.wait()

    return pl.pallas_call(
        body, out_shape=jax.ShapeDtypeStruct(buf.shape, buf.dtype), in_specs=[HBM_SPEC], out_specs=HBM_SPEC,
        scratch_shapes=[pltpu.SemaphoreType.DMA, pltpu.SemaphoreType.DMA], name=name)(buf)


def _scatter_chips(part):
    def body(src, dst, send, recv):
        x, y, c = lax.axis_index("x"), lax.axis_index("y"), lax.axis_index("c")
        copies = []
        for k, (fx, fy) in enumerate(CHIP_FLIPS):
            tx, ty = x ^ fx, y ^ fy
            cp = pltpu.make_async_remote_copy(
                src_ref=src.at[2 * tx + ty], dst_ref=dst.at[k], send_sem=send.at[k], recv_sem=recv.at[k],
                device_id=(tx, ty, c), device_id_type=MESH_ID)
            cp.start()
            copies.append(cp)
        for cp in copies:
            cp.wait()

    return pl.pallas_call(
        body, out_shape=jax.ShapeDtypeStruct((3,) + part.shape[1:], part.dtype), in_specs=[HBM_SPEC],
        out_specs=HBM_SPEC, scratch_shapes=[pltpu.SemaphoreType.DMA((3,)), pltpu.SemaphoreType.DMA((3,))],
        name="scatter_chips")(part)


def _add_rows(name, terms):
    rows = terms[0].shape[0]
    t = _row_tile(rows)

    def body(*refs):
        acc = refs[0][...]
        for r in refs[1:-1]:
            acc = acc + r[...]
        refs[-1][...] = acc

    return pl.pallas_call(
        body, out_shape=jax.ShapeDtypeStruct(terms[0].shape, F32), grid=(rows // t,),
        in_specs=[_rb(t, PACK_W)] * len(terms), out_specs=_rb(t, PACK_W),
        name=name, compiler_params=_params(("parallel",)))(*terms)


def _row_tile(rows):
    best = 8
    for t in range(8, 513, 8):
        if rows % t == 0:
            best = t
    return best


def _adamw(g, w, m, v):
    rows = g.shape[0]
    t = _row_tile(rows)
    c1 = 1.0 - ADAM_B1 ** ADAM_STEP
    c2 = 1.0 - ADAM_B2 ** ADAM_STEP

    def body(g_ref, w_ref, m_ref, v_ref, d_ref, mo_ref, vo_ref):
        gg = g_ref[...]
        mn = ADAM_B1 * m_ref[...] + (1.0 - ADAM_B1) * gg
        vn = ADAM_B2 * v_ref[...] + (1.0 - ADAM_B2) * (gg * gg)
        d_ref[...] = -ADAM_LR * ((mn / c1) / (jnp.sqrt(vn / c2) + ADAM_EPS) + ADAM_WD * w_ref[...])
        mo_ref[...] = mn
        vo_ref[...] = vn

    return pl.pallas_call(
        body, out_shape=[jax.ShapeDtypeStruct(g.shape, F32)] * 3, grid=(rows // t,),
        in_specs=[_rb(t, PACK_W)] * 4, out_specs=[_rb(t, PACK_W)] * 3,
        name="adamw", compiler_params=_params(("parallel",)))(g, w, m, v)


def kernel(x, meta_tokens, norm_mix_pre, w_in, q_a_norm, w_q_up, kv_a_norm, w_kv_up, conv_w, conv_b, dt_bias, a_log, d_skip, ssm_norm, w_out, norm_mix_post, norm_mlp_pre, w_mlp_up, w_mlp_down, norm_mlp_post, loss_target, m_meta_tokens, m_norm_mix_pre, m_w_in, m_q_a_norm, m_w_q_up, m_kv_a_norm, m_w_kv_up, m_conv_w, m_conv_b, m_dt_bias, m_a_log, m_d_skip, m_ssm_norm, m_w_out, m_norm_mix_post, m_norm_mlp_pre, m_w_mlp_up, m_w_mlp_down, m_norm_mlp_post, v_meta_tokens, v_norm_mix_pre, v_w_in, v_q_a_norm, v_w_q_up, v_kv_a_norm, v_w_kv_up, v_conv_w, v_conv_b, v_dt_bias, v_a_log, v_d_skip, v_ssm_norm, v_w_out, v_norm_mix_post, v_norm_mlp_pre, v_w_mlp_up, v_w_mlp_down, v_norm_mlp_post):
    given = dict(locals())
    drop = lambda name, a: a[0] if a.ndim == 3 else a
    w_loc = {n: drop(n, given[n]) for n in WEIGHTS}
    m_loc = {n: drop(n, given["m_" + n]) for n in WEIGHTS}
    v_loc = {n: drop(n, given["v_" + n]) for n in WEIGHTS}
    ix, iy, ic = lax.axis_index("x"), lax.axis_index("y"), lax.axis_index("c")
    me = 2 * ix + iy

    sent16 = _pack_rows([w_loc[n].astype(BF16) for n in GATHER_BF16], 16)
    sent32 = _pack_rows([w_loc[n] for n in GATHER_F32], 8)
    got16, got32 = _gather_chips([sent16, sent32])
    w_full = {n: w_loc[n] for n in WEIGHTS if n not in SHARD_AXIS}
    for names, got in ((GATHER_BF16, got16), (GATHER_F32, got32)):
        per_chip = [_unpack_rows(got[t], [_shard_shape(n) for n in names]) for t in range(N_CHIPS)]
        for k, n in enumerate(names):
            w_full[n] = jnp.concatenate([per_chip[t][k] for t in range(N_CHIPS)], axis=SHARD_AXIS[n])

    loss, dh0, g_full = _local_step(x[0], loss_target[0], w_full)
    n_real = x.shape[1]
    grad_x = dh0[N_META:N_META + n_real][None]

    shapes = [_shard_shape(n) for n in WEIGHTS]
    slots = [_pack_rows([_chip_slice(g_full[n], n, t) for n in WEIGHTS], 16) for t in range(N_CHIPS)]
    rows = slots[0].shape[0]
    half = rows // 2
    packed = jnp.stack(slots).reshape(N_CHIPS * 2, half, PACK_W)
    mine = lambda a, hh: a.reshape(N_CHIPS, 2, half, PACK_W)[:, hh].reshape(N_CHIPS * half, PACK_W)
    keep = jnp.where(ic == 0, mine(packed, 0), mine(packed, 1))
    give = jnp.where(ic == 0, mine(packed, 1), mine(packed, 0))
    from_sibling = _sibling_swap("sibling_swap", give)
    chip_part = _add_rows("chip_partial", [keep, from_sibling]).reshape(N_CHIPS, half, PACK_W)
    from_chips = _scatter_chips(chip_part)
    own = lax.dynamic_index_in_dim(chip_part, me, axis=0, keepdims=False)
    rel_of_xor = {2: 0, 1: 1, 3: 2}
    terms = []
    for t in range(N_CHIPS):
        cand = own
        for xr, k in rel_of_xor.items():
            cand = jnp.where((me ^ t) == xr, from_chips[k], cand)
        terms.append(cand)
    my_half = _add_rows("chip_total", terms)
    other_half = _sibling_swap("sibling_gather", my_half)
    g_red = jnp.where(ic == 0, jnp.concatenate([my_half, other_half], axis=0),
                      jnp.concatenate([other_half, my_half], axis=0))

    pack_loc = lambda d: _pack_rows([d[n] for n in WEIGHTS], 16)
    delta, new_m, new_v = _adamw(g_red, pack_loc(w_loc), pack_loc(m_loc), pack_loc(v_loc))

    def outputs(packed_arr):
        parts = _unpack_rows(packed_arr, shapes)
        return [p[None] if given[n].ndim == 3 else p for n, p in zip(WEIGHTS, parts)]

    total = lax.psum(loss[0, 0], ("x", "y", "c"))
    return (total, grad_x, *outputs(g_red), *outputs(delta), *outputs(new_m), *outputs(new_v))
```

```python
import functools

import numpy as np
import jax
import jax.numpy as jnp
from jax import lax
from jax.experimental import pallas as pl
from jax.experimental.pallas import tpu as pltpu

F32 = jnp.float32
BF16 = jnp.bfloat16

D_MODEL = 1024
N_META = 16
EPS = 1e-6
ATT_HEADS = 8
Q_LORA = 384
KV_LORA = 256
QK_NOPE = 128
QK_ROPE = 64
V_HEAD = 128
ROPE_THETA = 10000.0
SSM_HEADS = 16
SSM_HEAD_DIM = 64
SSM_WIDTH = 1024
SSM_STATE = 128
CONV_K = 4
D_FF = 4096
ATT_SCALE = float((QK_NOPE + QK_ROPE) ** -0.5)
ATT_SCALE_LOG2 = float(ATT_SCALE * np.log2(np.e))

ADAM_LR = 0.001
ADAM_B1 = 0.9
ADAM_B2 = 0.999
ADAM_EPS = 1e-08
ADAM_WD = 0.01
ADAM_STEP = 10

SEG_KV, SEG_KR, SEG_CQ, SEG_DT, SEG_Z, SEG_XS, SEG_BC = 0, 256, 384, 768, 1024, 2048, 3072
PROJ_W = 3584
QP_W = 256

ROW_BLOCK = 256
MM_BLOCK = 512
SSD_CHUNK = 128
ATT_SPLIT = 1
VMEM_LIMIT = 56 * 1024 * 1024
NEG_BIG = -1e30

PACK_W = 1024
N_CHIPS = 4
MESH_ID = pl.DeviceIdType.MESH


def _params(sem):
    return pltpu.CompilerParams(dimension_semantics=sem, vmem_limit_bytes=VMEM_LIMIT)


def _rb(rows, width, cb=0):
    return pl.BlockSpec((rows, width), lambda i: (i, cb))


def _full(shape):
    zeros = (0,) * len(shape)
    return pl.BlockSpec(shape, lambda i: zeros)


def _acc_add(ref, val):
    first = pl.program_id(0) == 0

    @pl.when(first)
    def _():
        ref[...] = val

    @pl.when(jnp.logical_not(first))
    def _():
        ref[...] += val


def _rms(x, g):
    r = lax.rsqrt(jnp.mean(x * x, axis=-1, keepdims=True) + EPS)
    return x * r * g


def _rms_bwd(x, g, dy):
    r = lax.rsqrt(jnp.mean(x * x, axis=-1, keepdims=True) + EPS)
    dyg = dy * g
    dx = r * dyg - x * (r * r * r) * jnp.mean(x * dyg, axis=-1, keepdims=True)
    dg = jnp.sum(dy * x * r, axis=0, keepdims=True)
    return dx, dg


def _sigmoid(x):
    return 1.0 / (1.0 + jnp.exp(-x))


def _swap32(x):
    lane = lax.broadcasted_iota(jnp.int32, x.shape, 1)
    return jnp.where((lane % 64) < 32, pltpu.roll(x, 96, 1), pltpu.roll(x, 32, 1))


def _rope(x, cos_t, sin_t):
    return x * cos_t + _swap32(x) * sin_t


def _rope_bwd(dr, cos_t, sin_t):
    return dr * cos_t + _swap32(dr * sin_t)


def _tile(n, cap):
    if n <= cap:
        return n
    best = 128
    for t in range(128, cap + 1, 128):
        if n % t == 0:
            best = t
    assert n % best == 0, (n, cap)
    return best


MM_VMEM_BUDGET = 40 * 1024 * 1024


def _mm(name, a, b, outs=((F32, None),), epi_ins=(), trans_b=False):
    m, k = a.shape
    n = b.shape[0] if trans_b else b.shape[1]
    tm = MM_BLOCK
    n_epi = len(epi_ins)
    out_bytes = sum(jnp.dtype(dt).itemsize for dt, _ in outs) + sum(e.dtype.itemsize for e in epi_ins)
    step_bytes = lambda tn: 2 * (tm * k * a.dtype.itemsize + k * tn * b.dtype.itemsize + tm * tn * out_bytes)
    tn = n
    while step_bytes(tn) > MM_VMEM_BUDGET and tn % 256 == 0:
        tn //= 2
    assert n % tn == 0 and step_bytes(tn) <= MM_VMEM_BUDGET, (name, n, tn)

    def body(a_ref, b_ref, *rest):
        epi_refs = rest[:n_epi]
        out_refs = rest[n_epi:]
        lhs, rhs = a_ref[...].astype(BF16), b_ref[...].astype(BF16)
        r = _nt(lhs, rhs) if trans_b else jnp.dot(lhs, rhs, preferred_element_type=F32)
        blocks = [e[...] for e in epi_refs]
        for o_ref, (dt, fn) in zip(out_refs, outs):
            o_ref[...] = (r if fn is None else fn(r, *blocks)).astype(dt)

    out_spec = pl.BlockSpec((tm, tn), lambda j, i: (i, j))
    b_spec = pl.BlockSpec((tn, k), lambda j, i: (j, 0)) if trans_b else pl.BlockSpec((k, tn), lambda j, i: (0, j))
    res = pl.pallas_call(
        body,
        out_shape=[jax.ShapeDtypeStruct((m, n), dt) for dt, _ in outs],
        grid=(n // tn, m // tm),
        in_specs=[pl.BlockSpec((tm, k), lambda j, i: (i, 0)), b_spec] + [out_spec] * n_epi,
        out_specs=[out_spec] * len(outs),
        name=name,
        compiler_params=_params(("parallel", "parallel")),
    )(a, b, *epi_ins)
    return res[0] if len(outs) == 1 else res


def _mm_tn(name, x, dy, ta_cap=1024, tn_cap=1024):
    l, a = x.shape
    n = dy.shape[1]
    ta, tn, tl = _tile(a, ta_cap), _tile(n, tn_cap), MM_BLOCK
    nl = l // tl

    def body(x_ref, dy_ref, o_ref):
        ll = pl.program_id(2)

        @pl.when(ll == 0)
        def _():
            o_ref[...] = jnp.zeros_like(o_ref)

        o_ref[...] += lax.dot_general(x_ref[...].astype(BF16), dy_ref[...].astype(BF16), (((0,), (0,)), ((), ())),
                                      preferred_element_type=F32)

    return pl.pallas_call(
        body,
        out_shape=jax.ShapeDtypeStruct((a, n), F32),
        grid=(a // ta, n // tn, nl),
        in_specs=[pl.BlockSpec((tl, ta), lambda i, j, ll: (ll, i)), pl.BlockSpec((tl, tn), lambda i, j, ll: (ll, j))],
        out_specs=pl.BlockSpec((ta, tn), lambda i, j, ll: (i, j)),
        name=name,
        compiler_params=_params(("parallel", "parallel", "arbitrary")),
    )(x, dy)


def _norm_in(h0, g_pre):
    lp = h0.shape[0]
    t = ROW_BLOCK

    def body(h_ref, g_ref, o_ref):
        o_ref[...] = _rms(h_ref[...], g_ref[...]).astype(BF16)

    return pl.pallas_call(
        body, out_shape=jax.ShapeDtypeStruct((lp, D_MODEL), BF16), grid=(lp // t,),
        in_specs=[_rb(t, D_MODEL), _full((1, D_MODEL))], out_specs=_rb(t, D_MODEL),
        name="norm_in", compiler_params=_params(("parallel",)))(h0, g_pre)


def _attn_prep(proj, g_q, g_kv, cos_t, sin_t):
    lp = proj.shape[0]
    t = ROW_BLOCK

    def body(ckv_ref, kr_ref, cq_ref, gq_ref, gkv_ref, cos_ref, sin_ref, cqn_ref, ckvn_ref, krr_ref):
        cqn_ref[...] = _rms(cq_ref[...], gq_ref[...]).astype(BF16)
        ckvn_ref[...] = _rms(ckv_ref[...], gkv_ref[...]).astype(BF16)
        roped = _rope(kr_ref[...], cos_ref[...], sin_ref[...])
        krr_ref[...] = roped + pltpu.roll(roped, 64, 1)

    return pl.pallas_call(
        body,
        out_shape=[jax.ShapeDtypeStruct((lp, Q_LORA), BF16), jax.ShapeDtypeStruct((lp, KV_LORA), BF16),
                   jax.ShapeDtypeStruct((lp, 128), F32)],
        grid=(lp // t,),
        in_specs=[_rb(t, KV_LORA, SEG_KV // KV_LORA), _rb(t, 128, SEG_KR // 128), _rb(t, Q_LORA, SEG_CQ // Q_LORA),
                  _full((1, Q_LORA)), _full((1, KV_LORA)), _rb(t, 128), _rb(t, 128)],
        out_specs=[_rb(t, Q_LORA), _rb(t, KV_LORA), _rb(t, 128)],
        name="attn_prep", compiler_params=_params(("parallel",)))(proj, proj, proj, g_q, g_kv, cos_t, sin_t)


def _qk_pack(q, kv, krr, cos_t, sin_t):
    lp = q.shape[0]
    t = ROW_BLOCK

    def body(q_ref, kv_ref, krr_ref, cos_ref, sin_ref, qs_ref, ks_ref, vs_ref, vts_ref):
        lane = lax.broadcasted_iota(jnp.int32, (t, 128), 1)
        lo = lane < 64
        krr = krr_ref[...].astype(BF16)
        for j in range(ATT_HEADS // 2):
            pr = _rope(q_ref[:, 1024 + 128 * j:1024 + 128 * (j + 1)], cos_ref[...], sin_ref[...])
            for h, keep in ((2 * j, lo), (2 * j + 1, jnp.logical_not(lo))):
                qs_ref[h, :, 0:128] = q_ref[:, 128 * h:128 * (h + 1)].astype(BF16)
                qs_ref[h, :, 128:256] = jnp.where(keep, pr, 0.0).astype(BF16)
        for h in range(ATT_HEADS):
            ks_ref[h, :, 0:128] = kv_ref[:, 256 * h:256 * h + 128].astype(BF16)
            ks_ref[h, :, 128:256] = krr
            v = kv_ref[:, 256 * h + 128:256 * (h + 1)]
            vs_ref[h] = v.astype(BF16)
            vts_ref[h] = v.T.astype(BF16)

    slab = lambda w: pl.BlockSpec((ATT_HEADS, t, w), lambda i: (0, i, 0))
    return pl.pallas_call(
        body,
        out_shape=[jax.ShapeDtypeStruct((ATT_HEADS, lp, QP_W), BF16), jax.ShapeDtypeStruct((ATT_HEADS, lp, QP_W), BF16),
                   jax.ShapeDtypeStruct((ATT_HEADS, lp, V_HEAD), BF16), jax.ShapeDtypeStruct((ATT_HEADS, V_HEAD, lp), BF16)],
        grid=(lp // t,),
        in_specs=[_rb(t, 1536), _rb(t, 2048), _rb(t, 128), _rb(t, 128), _rb(t, 128)],
        out_specs=[slab(QP_W), slab(QP_W), slab(V_HEAD), pl.BlockSpec((ATT_HEADS, V_HEAD, t), lambda i: (0, 0, i))],
        name="qk_pack", compiler_params=_params(("parallel",)))(q, kv, krr, cos_t, sin_t)


def _shifted(ext, t, shift):
    if shift == 0:
        return ext[8:, :]
    return pltpu.roll(ext, shift, 0)[8:, :]


def _conv_fwd(name, proj, seg, width, conv_w, conv_b):
    lp = proj.shape[0]
    t = ROW_BLOCK
    cb = seg // width

    def body(u_ref, halo_ref, w_ref, b_ref, pre_ref, act_ref):
        i = pl.program_id(0)
        u = u_ref[...]
        halo = jnp.where(i > 0, halo_ref[...], 0.0)
        ext = jnp.concatenate([halo, u], axis=0)
        pre = jnp.broadcast_to(b_ref[...], (t, width))
        for k in range(CONV_K):
            pre = pre + w_ref[k:k + 1, :] * _shifted(ext, t, CONV_K - 1 - k)
        pre_ref[...] = pre
        act_ref[...] = pre * _sigmoid(pre)

    return pl.pallas_call(
        body,
        out_shape=[jax.ShapeDtypeStruct((lp, width), F32)] * 2,
        grid=(lp // t,),
        in_specs=[_rb(t, width, cb),
                  pl.BlockSpec((8, width), lambda i: (jnp.maximum(i * (t // 8) - 1, 0), cb)),
                  _full((CONV_K, width)), _full((1, width))],
        out_specs=[_rb(t, width), _rb(t, width)],
        name=name, compiler_params=_params(("parallel",)))(proj, proj, conv_w, conv_b)


def _softplus(x):
    return jnp.maximum(x, 0.0) + jnp.log1p(jnp.exp(-jnp.abs(x)))


def _dt_fwd(proj, dt_bias_pad):
    lp = proj.shape[0]
    t = ROW_BLOCK

    def body(x_ref, b_ref, o_ref):
        o_ref[...] = _softplus(x_ref[...] + b_ref[...])

    return pl.pallas_call(
        body, out_shape=jax.ShapeDtypeStruct((lp, 128), F32), grid=(lp // t,),
        in_specs=[_rb(t, 128, SEG_DT // 128), _full((1, 128))], out_specs=_rb(t, 128),
        name="dt_fwd", compiler_params=_params(("parallel",)))(proj, dt_bias_pad)


def _gated_norm_group(y, z, w):
    g = y * (z * _sigmoid(z))
    return g * lax.rsqrt(jnp.mean(g * g, axis=-1, keepdims=True) + EPS) * w


def _gated_norm_fwd(y, proj, w):
    lp = y.shape[0]
    t = ROW_BLOCK
    gw = SSM_WIDTH // 2

    def body(y0, y1, z0, z1, w0, w1, o_ref):
        o_ref[:, 0:gw] = _gated_norm_group(y0[...], z0[...], w0[...]).astype(BF16)
        o_ref[:, gw:] = _gated_norm_group(y1[...], z1[...], w1[...]).astype(BF16)

    zb = SEG_Z // gw
    return pl.pallas_call(
        body, out_shape=jax.ShapeDtypeStruct((lp, SSM_WIDTH), BF16), grid=(lp // t,),
        in_specs=[_rb(t, gw, 0), _rb(t, gw, 1), _rb(t, gw, zb), _rb(t, gw, zb + 1),
                  pl.BlockSpec((1, gw), lambda i: (0, 0)), pl.BlockSpec((1, gw), lambda i: (0, 1))],
        out_specs=_rb(t, SSM_WIDTH),
        name="gated_norm_fwd", compiler_params=_params(("parallel",)))(y, y, proj, proj, w, w)


def _gated_norm_bwd(y, proj, w, dssm):
    lp = y.shape[0]
    t = ROW_BLOCK
    gw = SSM_WIDTH // 2

    def body(y0, y1, z0, z1, w0, w1, d0, d1, dy_ref, dz_ref, dw_ref):
        dws = []
        for g, (yr, zr, wr, dr) in enumerate(((y0, z0, w0, d0), (y1, z1, w1, d1))):
            _, vjp = jax.vjp(_gated_norm_group, yr[...], zr[...], wr[...])
            dyg, dzg, dwg = vjp(dr[...])
            dy_ref[:, g * gw:(g + 1) * gw] = dyg
            dz_ref[:, g * gw:(g + 1) * gw] = dzg
            dws.append(dwg)
        _acc_add(dw_ref, jnp.concatenate(dws, axis=1))

    zb = SEG_Z // gw
    return pl.pallas_call(
        body,
        out_shape=[jax.ShapeDtypeStruct((lp, SSM_WIDTH), F32), jax.ShapeDtypeStruct((lp, SSM_WIDTH), F32),
                   jax.ShapeDtypeStruct((1, SSM_WIDTH), F32)],
        grid=(lp // t,),
        in_specs=[_rb(t, gw, 0), _rb(t, gw, 1), _rb(t, gw, zb), _rb(t, gw, zb + 1),
                  pl.BlockSpec((1, gw), lambda i: (0, 0)), pl.BlockSpec((1, gw), lambda i: (0, 1)),
                  _rb(t, gw, 2), _rb(t, gw, 3)],
        out_specs=[_rb(t, SSM_WIDTH), _rb(t, SSM_WIDTH), _full((1, SSM_WIDTH))],
        name="gated_norm_bwd", compiler_params=_params(("arbitrary",)))(y, y, proj, proj, w, w, dssm, dssm)


def _mix_residual(h0, mix, g_post, g_mlp_pre):
    lp = h0.shape[0]
    t = ROW_BLOCK

    def body(h_ref, m_ref, gp_ref, gm_ref, h1_ref, n2_ref):
        h1 = h_ref[...] + _rms(m_ref[...], gp_ref[...])
        h1_ref[...] = h1
        n2_ref[...] = _rms(h1, gm_ref[...]).astype(BF16)

    return pl.pallas_call(
        body, out_shape=[jax.ShapeDtypeStruct((lp, D_MODEL), F32), jax.ShapeDtypeStruct((lp, D_MODEL), BF16)],
        grid=(lp // t,),
        in_specs=[_rb(t, D_MODEL), _rb(t, D_MODEL), _full((1, D_MODEL)), _full((1, D_MODEL))],
        out_specs=[_rb(t, D_MODEL), _rb(t, D_MODEL)],
        name="mix_residual", compiler_params=_params(("parallel",)))(h0, mix, g_post, g_mlp_pre)


def _loss_and_grad(h1, f, g_post, tgt, n_real):
    lp = h1.shape[0]
    t = ROW_BLOCK

    def body(h1_ref, f_ref, g_ref, t_ref, loss_ref, dh2_ref, df_ref, dg_ref):
        i = pl.program_id(0)
        fx = f_ref[...]
        h2 = h1_ref[...] + _rms(fx, g_ref[...])
        row = i * t + lax.broadcasted_iota(jnp.int32, (t, 1), 0)
        real = jnp.logical_and(row >= N_META, row < N_META + n_real)
        diff = jnp.where(real, h2 - t_ref[...], 0.0)
        part = 0.5 * jnp.sum(jnp.sum(diff * diff, axis=-1, keepdims=True) / D_MODEL, axis=0, keepdims=True)
        _acc_add(loss_ref, jnp.broadcast_to(part, (1, 128)))
        dh2 = diff / D_MODEL
        dh2_ref[...] = dh2
        dfx, dg = _rms_bwd(fx, g_ref[...], dh2)
        df_ref[...] = dfx.astype(BF16)
        _acc_add(dg_ref, dg)

    return pl.pallas_call(
        body,
        out_shape=[jax.ShapeDtypeStruct((1, 128), F32), jax.ShapeDtypeStruct((lp, D_MODEL), F32),
                   jax.ShapeDtypeStruct((lp, D_MODEL), BF16), jax.ShapeDtypeStruct((1, D_MODEL), F32)],
        grid=(lp // t,),
        in_specs=[_rb(t, D_MODEL), _rb(t, D_MODEL), _full((1, D_MODEL)), _rb(t, D_MODEL)],
        out_specs=[_full((1, 128)), _rb(t, D_MODEL), _rb(t, D_MODEL), _full((1, D_MODEL))],
        name="loss_and_grad", compiler_params=_params(("arbitrary",)))(h1, f, g_post, tgt)


def _mlp_residual_bwd(dh2, dn2, h1, g_mlp_pre, mix, g_post):
    lp = h1.shape[0]
    t = ROW_BLOCK

    def body(dh2_ref, dn2_ref, h1_ref, gm_ref, mix_ref, gp_ref, dh1_ref, dmix_ref, dgm_ref, dgp_ref):
        dx, dgm = _rms_bwd(h1_ref[...], gm_ref[...], dn2_ref[...])
        dh1 = dh2_ref[...] + dx
        dh1_ref[...] = dh1
        dmix, dgp = _rms_bwd(mix_ref[...], gp_ref[...], dh1)
        dmix_ref[...] = dmix.astype(BF16)
        _acc_add(dgm_ref, dgm)
        _acc_add(dgp_ref, dgp)

    return pl.pallas_call(
        body,
        out_shape=[jax.ShapeDtypeStruct((lp, D_MODEL), F32), jax.ShapeDtypeStruct((lp, D_MODEL), BF16),
                   jax.ShapeDtypeStruct((1, D_MODEL), F32), jax.ShapeDtypeStruct((1, D_MODEL), F32)],
        grid=(lp // t,),
        in_specs=[_rb(t, D_MODEL), _rb(t, D_MODEL), _rb(t, D_MODEL), _full((1, D_MODEL)), _rb(t, D_MODEL),
                  _full((1, D_MODEL))],
        out_specs=[_rb(t, D_MODEL), _rb(t, D_MODEL), _full((1, D_MODEL)), _full((1, D_MODEL))],
        name="mlp_residual_bwd", compiler_params=_params(("arbitrary",)))(dh2, dn2, h1, g_mlp_pre, mix, g_post)


def _input_norm_bwd(dh1, dn1, h0, g_pre):
    lp = h0.shape[0]
    t = ROW_BLOCK

    def body(dh1_ref, dn1_ref, h0_ref, g_ref, dh0_ref, dg_ref):
        dx, dg = _rms_bwd(h0_ref[...], g_ref[...], dn1_ref[...])
        dh0_ref[...] = dh1_ref[...] + dx
        _acc_add(dg_ref, dg)

    return pl.pallas_call(
        body, out_shape=[jax.ShapeDtypeStruct((lp, D_MODEL), F32), jax.ShapeDtypeStruct((1, D_MODEL), F32)],
        grid=(lp // t,),
        in_specs=[_rb(t, D_MODEL), _rb(t, D_MODEL), _rb(t, D_MODEL), _full((1, D_MODEL))],
        out_specs=[_rb(t, D_MODEL), _full((1, D_MODEL))],
        name="input_norm_bwd", compiler_params=_params(("arbitrary",)))(dh1, dn1, h0, g_pre)


def _conv_bwd(name, dact, pre, proj, seg, width, conv_w):
    lp = proj.shape[0]
    t = ROW_BLOCK
    cb = seg // width
    nblk = lp // t

    def dsilu(p):
        s = _sigmoid(p)
        return s * (1.0 + p * (1.0 - s))

    def body(da_ref, dan_ref, pre_ref, pren_ref, u_ref, halo_ref, w_ref, du_ref, dw_ref, db_ref):
        i = pl.program_id(0)
        dpre = da_ref[...] * dsilu(pre_ref[...])
        dpre_next = jnp.where(i < nblk - 1, dan_ref[...] * dsilu(pren_ref[...]), 0.0)
        extd = jnp.concatenate([dpre, dpre_next], axis=0)
        halo = jnp.where(i > 0, halo_ref[...], 0.0)
        ext = jnp.concatenate([halo, u_ref[...]], axis=0)
        du = jnp.zeros((t, width), F32)
        dws = []
        for k in range(CONV_K):
            m = CONV_K - 1 - k
            ahead = dpre if m == 0 else pltpu.roll(extd, t + 8 - m, 0)[:t, :]
            du = du + w_ref[k:k + 1, :] * ahead
            dws.append(jnp.sum(dpre * _shifted(ext, t, m), axis=0, keepdims=True))
        du_ref[...] = du
        _acc_add(dw_ref, jnp.concatenate(dws + [jnp.zeros((8 - CONV_K, width), F32)], axis=0))
        _acc_add(db_ref, jnp.sum(dpre, axis=0, keepdims=True))

    nxt = lambda i: (jnp.minimum((i + 1) * (t // 8), lp // 8 - 1), 0)
    return pl.pallas_call(
        body,
        out_shape=[jax.ShapeDtypeStruct((lp, width), F32), jax.ShapeDtypeStruct((8, width), F32),
                   jax.ShapeDtypeStruct((1, width), F32)],
        grid=(nblk,),
        in_specs=[_rb(t, width), pl.BlockSpec((8, width), nxt), _rb(t, width), pl.BlockSpec((8, width), nxt),
                  _rb(t, width, cb),
                  pl.BlockSpec((8, width), lambda i: (jnp.maximum(i * (t // 8) - 1, 0), cb)),
                  _full((CONV_K, width))],
        out_specs=[_rb(t, width), _full((8, width)), _full((1, width))],
        name=name, compiler_params=_params(("arbitrary",)))(dact, dact, pre, pre, proj, proj, conv_w)


def _qk_unpack_bwd(dqs, dks, dvs, cos_t, sin_t):
    lp = dqs.shape[1]
    t = ROW_BLOCK

    def body(dqs_ref, dks_ref, dvs_ref, cos_ref, sin_ref, dq_ref, dkv_ref, dkr_ref):
        lane = lax.broadcasted_iota(jnp.int32, (t, 128), 1)
        lo = lane < 64
        for j in range(ATT_HEADS // 2):
            dpr = jnp.where(lo, dqs_ref[2 * j, :, 128:256], dqs_ref[2 * j + 1, :, 128:256])
            dq_ref[:, 1024 + 128 * j:1024 + 128 * (j + 1)] = _rope_bwd(dpr, cos_ref[...], sin_ref[...]).astype(BF16)
        dkrr = jnp.zeros((t, 128), F32)
        for h in range(ATT_HEADS):
            dq_ref[:, 128 * h:128 * (h + 1)] = dqs_ref[h, :, 0:128].astype(BF16)
            dkv_ref[:, 256 * h:256 * h + 128] = dks_ref[h, :, 0:128].astype(BF16)
            dkv_ref[:, 256 * h + 128:256 * (h + 1)] = dvs_ref[h].astype(BF16)
            dkrr = dkrr + dks_ref[h, :, 128:256]
        droped = jnp.where(lo, dkrr + pltpu.roll(dkrr, 64, 1), 0.0)
        dkr_ref[...] = _rope_bwd(droped, cos_ref[...], sin_ref[...])

    slab = lambda w: pl.BlockSpec((ATT_HEADS, t, w), lambda i: (0, i, 0))
    return pl.pallas_call(
        body,
        out_shape=[jax.ShapeDtypeStruct((lp, 1536), BF16), jax.ShapeDtypeStruct((lp, 2048), BF16),
                   jax.ShapeDtypeStruct((lp, 128), F32)],
        grid=(lp // t,),
        in_specs=[slab(QP_W), slab(QP_W), slab(V_HEAD), _rb(t, 128), _rb(t, 128)],
        out_specs=[_rb(t, 1536), _rb(t, 2048), _rb(t, 128)],
        name="qk_unpack_bwd", compiler_params=_params(("parallel",)))(dqs, dks, dvs, cos_t, sin_t)


def _proj_grad(proj, dcqn, dckvn, g_q, g_kv, dkr, ddt_pad, dt_bias_pad, dz, dxs, dbc):
    lp = proj.shape[0]
    t = ROW_BLOCK

    def body(ckv_ref, cq_ref, pdt_ref, dcq_ref, dckv_ref, gq_ref, gkv_ref, dkr_ref, ddt_ref, b_ref, dz_ref, dxs_ref,
             dbc_ref, dp_ref, dgq_ref, dgkv_ref, db_ref):
        dckv, dgkv = _rms_bwd(ckv_ref[...], gkv_ref[...], dckv_ref[...])
        dcq, dgq = _rms_bwd(cq_ref[...], gq_ref[...], dcq_ref[...])
        ddt_raw = ddt_ref[...] * _sigmoid(pdt_ref[...] + b_ref[...])
        dp_ref[:, SEG_KV:SEG_KV + KV_LORA] = dckv.astype(BF16)
        dp_ref[:, SEG_KR:SEG_KR + 128] = dkr_ref[...].astype(BF16)
        dp_ref[:, SEG_CQ:SEG_CQ + Q_LORA] = dcq.astype(BF16)
        dp_ref[:, SEG_DT:SEG_DT + 128] = ddt_raw.astype(BF16)
        dp_ref[:, SEG_DT + 128:SEG_Z] = jnp.zeros((t, SEG_Z - SEG_DT - 128), BF16)
        dp_ref[:, SEG_Z:SEG_XS] = dz_ref[...].astype(BF16)
        dp_ref[:, SEG_XS:SEG_BC] = dxs_ref[...].astype(BF16)
        dp_ref[:, SEG_BC:PROJ_W] = dbc_ref[...].astype(BF16)
        _acc_add(dgq_ref, dgq)
        _acc_add(dgkv_ref, dgkv)
        _acc_add(db_ref, jnp.sum(ddt_raw, axis=0, keepdims=True))

    return pl.pallas_call(
        body,
        out_shape=[jax.ShapeDtypeStruct((lp, PROJ_W), BF16), jax.ShapeDtypeStruct((1, Q_LORA), F32),
                   jax.ShapeDtypeStruct((1, KV_LORA), F32), jax.ShapeDtypeStruct((1, 128), F32)],
        grid=(lp // t,),
        in_specs=[_rb(t, KV_LORA, SEG_KV // KV_LORA), _rb(t, Q_LORA, SEG_CQ // Q_LORA), _rb(t, 128, SEG_DT // 128),
                  _rb(t, Q_LORA), _rb(t, KV_LORA), _full((1, Q_LORA)), _full((1, KV_LORA)), _rb(t, 128), _rb(t, 128),
                  _full((1, 128)), _rb(t, SSM_WIDTH), _rb(t, SSM_WIDTH), _rb(t, 512)],
        out_specs=[_rb(t, PROJ_W), _full((1, Q_LORA)), _full((1, KV_LORA)), _full((1, 128))],
        name="proj_grad", compiler_params=_params(("arbitrary",)))(
            proj, proj, proj, dcqn, dckvn, g_q, g_kv, dkr, ddt_pad, dt_bias_pad, dz, dxs, dbc)


def _pair_tables(n):
    qmaj = [(i, j) for i in range(n) for j in range(i + 1)]
    kmaj = [(i, j) for j in range(n) for i in range(j, n)]
    to = lambda ps, c: jnp.asarray(np.array([p[c] for p in ps], np.int32))
    return (to(qmaj, 0), to(qmaj, 1)), (to(kmaj, 0), to(kmaj, 1))


def _nt(a, b):
    return lax.dot_general(a, b, (((1,), (1,)), ((), ())), preferred_element_type=F32)


def _attn_fwd(qs, ks, vts):
    lp = qs.shape[1]
    t = MM_BLOCK
    n = lp // t
    (qi, kj), _ = _pair_tables(n)
    tc = t // ATT_SPLIT

    def body(qi_ref, kj_ref, q_ref, k_ref, vt_ref, o_ref, lse_ref, m_s, l_s, acc_s):
        p = pl.program_id(1)
        i, j = qi_ref[p], kj_ref[p]

        @pl.when(j == 0)
        def _():
            m_s[...] = jnp.full_like(m_s, NEG_BIG)
            l_s[...] = jnp.zeros_like(l_s)
            acc_s[...] = jnp.zeros_like(acc_s)

        def update(masked):
            m_all, l_all, acc_all = m_s[...], l_s[...], acc_s[...]
            m_out, l_out, acc_out = [], [], []
            for c in range(ATT_SPLIT):
                cols = slice(c * tc, (c + 1) * tc)
                sc = _nt(k_ref[0], q_ref[0, cols, :]) * ATT_SCALE_LOG2
                if masked:
                    keep = (lax.broadcasted_iota(jnp.int32, (t, tc), 1) + c * tc
                            >= lax.broadcasted_iota(jnp.int32, (t, tc), 0))
                    sc = jnp.where(keep, sc, NEG_BIG)
                m_prev = m_all[:, cols]
                m_new = jnp.maximum(m_prev, jnp.max(sc, axis=0, keepdims=True))
                alpha = jnp.exp2(m_prev - m_new)
                pexp = jnp.exp2(sc - m_new)
                l_out.append(alpha * l_all[:, cols] + jnp.sum(pexp, axis=0, keepdims=True))
                acc_out.append(alpha * acc_all[:, cols] + jnp.dot(vt_ref[0], pexp.astype(BF16),
                                                                  preferred_element_type=F32))
                m_out.append(m_new)
            cat = lambda parts: parts[0] if len(parts) == 1 else jnp.concatenate(parts, axis=1)
            m_s[...], l_s[...], acc_s[...] = cat(m_out), cat(l_out), cat(acc_out)

        @pl.when(j < i)
        def _():
            update(False)

        @pl.when(j == i)
        def _():
            update(True)
            o_ref[...] = (acc_s[...] / l_s[...]).T
            lse_ref[0] = m_s[...] + jnp.log2(l_s[...])

    grid_spec = pltpu.PrefetchScalarGridSpec(
        num_scalar_prefetch=2, grid=(ATT_HEADS, int(qi.shape[0])),
        in_specs=[pl.BlockSpec((1, t, QP_W), lambda h, p, qi, kj: (h, qi[p], 0)),
                  pl.BlockSpec((1, t, QP_W), lambda h, p, qi, kj: (h, kj[p], 0)),
                  pl.BlockSpec((1, V_HEAD, t), lambda h, p, qi, kj: (h, 0, kj[p]))],
        out_specs=[pl.BlockSpec((t, V_HEAD), lambda h, p, qi, kj: (qi[p], h)),
                   pl.BlockSpec((1, 1, t), lambda h, p, qi, kj: (h, 0, qi[p]))],
        scratch_shapes=[pltpu.VMEM((1, t), F32), pltpu.VMEM((1, t), F32), pltpu.VMEM((V_HEAD, t), F32)])
    return pl.pallas_call(
        body, grid_spec=grid_spec,
        out_shape=[jax.ShapeDtypeStruct((lp, ATT_HEADS * V_HEAD), F32), jax.ShapeDtypeStruct((ATT_HEADS, 1, lp), F32)],
        name="attn_fwd", compiler_params=_params(("parallel", "arbitrary")))(qi, kj, qs, ks, vts)


def _attn_delta(datt, att):
    lp = att.shape[0]
    t = MM_BLOCK

    def body(do_ref, o_ref, d_ref):
        prod = do_ref[...] * o_ref[...]
        d_ref[0] = jnp.sum(prod.T, axis=0, keepdims=True)

    blk = pl.BlockSpec((t, V_HEAD), lambda h, i: (i, h))
    return pl.pallas_call(
        body, out_shape=jax.ShapeDtypeStruct((ATT_HEADS, 1, lp), F32), grid=(ATT_HEADS, lp // t),
        in_specs=[blk, blk], out_specs=pl.BlockSpec((1, 1, t), lambda h, i: (h, 0, i)),
        name="attn_delta", compiler_params=_params(("parallel", "parallel")))(datt, att)


def _attn_bwd(qs, ks, vs, datt16, lse2, delta):
    lp = qs.shape[1]
    t = MM_BLOCK
    n = lp // t
    _, (qi, kj) = _pair_tables(n)
    tc = t // ATT_SPLIT

    def body(qi_ref, kj_ref, k_ref, v_ref, q_ref, do_ref, lse_ref, dl_ref, dq_ref, dk_ref, dv_ref, dk_s, dv_s):
        p = pl.program_id(1)
        i, j = qi_ref[p], kj_ref[p]

        @pl.when(p == 0)
        def _():
            dq_ref[...] = jnp.zeros_like(dq_ref)

        @pl.when(i == j)
        def _():
            dk_s[...] = jnp.zeros_like(dk_s)
            dv_s[...] = jnp.zeros_like(dv_s)

        def step(masked):
            for c in range(ATT_SPLIT):
                cols = slice(c * tc, (c + 1) * tc)
                q = q_ref[0, cols, :]
                do = do_ref[cols, :]
                pt = jnp.exp2(_nt(k_ref[0], q) * ATT_SCALE_LOG2 - lse_ref[0, :, cols])
                if masked:
                    keep = (lax.broadcasted_iota(jnp.int32, (t, tc), 1) + c * tc
                            >= lax.broadcasted_iota(jnp.int32, (t, tc), 0))
                    pt = jnp.where(keep, pt, 0.0)
                dst = (pt * (_nt(v_ref[0], do) - dl_ref[0, :, cols]) * ATT_SCALE).astype(BF16)
                dv_s[...] += jnp.dot(pt.astype(BF16), do, preferred_element_type=F32)
                dk_s[...] += jnp.dot(dst, q, preferred_element_type=F32)
                rows = pl.ds(pl.multiple_of(i * t + c * tc, tc), tc)
                dq_ref[0, rows, :] += lax.dot_general(dst, k_ref[0], (((0,), (0,)), ((), ())),
                                                      preferred_element_type=F32)

        @pl.when(i > j)
        def _():
            step(False)

        @pl.when(i == j)
        def _():
            step(True)

        @pl.when(i == n - 1)
        def _():
            dk_ref[0] = dk_s[...]
            dv_ref[0] = dv_s[...]

    grid_spec = pltpu.PrefetchScalarGridSpec(
        num_scalar_prefetch=2, grid=(ATT_HEADS, int(qi.shape[0])),
        in_specs=[pl.BlockSpec((1, t, QP_W), lambda h, p, qi, kj: (h, kj[p], 0)),
                  pl.BlockSpec((1, t, V_HEAD), lambda h, p, qi, kj: (h, kj[p], 0)),
                  pl.BlockSpec((1, t, QP_W), lambda h, p, qi, kj: (h, qi[p], 0)),
                  pl.BlockSpec((t, V_HEAD), lambda h, p, qi, kj: (qi[p], h)),
                  pl.BlockSpec((1, 1, t), lambda h, p, qi, kj: (h, 0, qi[p])),
                  pl.BlockSpec((1, 1, t), lambda h, p, qi, kj: (h, 0, qi[p]))],
        out_specs=[pl.BlockSpec((1, lp, QP_W), lambda h, p, qi, kj: (h, 0, 0)),
                   pl.BlockSpec((1, t, QP_W), lambda h, p, qi, kj: (h, kj[p], 0)),
                   pl.BlockSpec((1, t, V_HEAD), lambda h, p, qi, kj: (h, kj[p], 0))],
        scratch_shapes=[pltpu.VMEM((t, QP_W), F32), pltpu.VMEM((t, V_HEAD), F32)])
    return pl.pallas_call(
        body, grid_spec=grid_spec,
        out_shape=[jax.ShapeDtypeStruct((ATT_HEADS, lp, QP_W), F32), jax.ShapeDtypeStruct((ATT_HEADS, lp, QP_W), F32),
                   jax.ShapeDtypeStruct((ATT_HEADS, lp, V_HEAD), F32)],
        name="attn_bwd", compiler_params=_params(("arbitrary", "arbitrary")))(
            qi, kj, ks, vs, qs, datt16, lse2, delta)


N_PAIRS = SSM_HEADS // 2
HI = lax.Precision.HIGHEST


def _ssd_chunk(xp, bs, cs, dt, dt_t, alr, alc, dsk, st):
    q = dt.shape[0]
    li = lax.broadcasted_iota(jnp.int32, (q, q), 0)
    si = lax.broadcasted_iota(jnp.int32, (q, q), 1)
    tri = (si <= li).astype(F32)
    tri_t = (li <= si).astype(F32)
    lo = lax.broadcasted_iota(jnp.int32, (1, 128), 1) < 64
    h_r = lax.broadcasted_iota(jnp.int32, (1, SSM_HEADS), 1)
    h_c = lax.broadcasted_iota(jnp.int32, (SSM_HEADS, 1), 0)
    a = dt * (-jnp.exp(alr))
    a_t = dt_t * (-jnp.exp(alc))
    acum = jnp.dot(tri, a, precision=HI, preferred_element_type=F32)
    acum_t = jnp.dot(a_t, tri_t, precision=HI, preferred_element_type=F32)
    last = (lax.broadcasted_iota(jnp.int32, (q, 1), 0) == q - 1).astype(F32)
    alast = jnp.sum(acum * last, axis=0, keepdims=True)
    e = jnp.exp(acum)
    rdt = jnp.exp(alast - acum) * dt
    e_last = jnp.exp(alast)

    def col(m, h):
        return jnp.sum(m * (h_r == h).astype(F32), axis=1, keepdims=True)

    def row(m, h):
        return jnp.sum(m * (h_c == h).astype(F32), axis=0, keepdims=True)

    def pair(m, ha):
        return jnp.where(lo, col(m, ha), col(m, ha + 1))

    ys, st_new = [], []
    for g in range(2):
        c_b = cs[g].astype(BF16)
        b_b = bs[g].astype(BF16)
        cb = _nt(c_b, b_b)
        for j in range(N_PAIRS // 2):
            p = (N_PAIRS // 2) * g + j
            ha = 2 * p
            x = xp[p]
            x_b = x.astype(BF16)

            def w_of(h):
                seg = col(acum, h) - row(acum_t, h)
                return (cb * jnp.exp(jnp.minimum(seg, 0.0)) * tri * row(dt_t, h)).astype(BF16)

            y_diag = jnp.where(lo, jnp.dot(w_of(ha), x_b, preferred_element_type=F32),
                               jnp.dot(w_of(ha + 1), x_b, preferred_element_type=F32))
            y_off = jnp.dot(c_b, st[p].astype(BF16), preferred_element_type=F32) * pair(e, ha)
            ys.append(y_diag + y_off + pair(dsk, ha) * x)
            xw = (x * pair(rdt, ha)).astype(BF16)
            st_new.append(st[p] * pair(e_last, ha)
                          + lax.dot_general(b_b, xw, (((0,), (0,)), ((), ())), preferred_element_type=F32))
    return ys, st_new


def _ssd_fwd(xs, bc, dt, dt_t, alr, alc, dsk):
    lp = xs.shape[0]
    q = SSD_CHUNK
    nc = lp // q

    def body(x_ref, b_ref, c_ref, dt_ref, dtt_ref, alr_ref, alc_ref, dsk_ref, y_ref, sp_ref, st_s):
        @pl.when(pl.program_id(0) == 0)
        def _():
            st_s[...] = jnp.zeros_like(st_s)

        sp_ref[0] = st_s[...]
        xp = [x_ref[:, 128 * p:128 * (p + 1)] for p in range(N_PAIRS)]
        bs = [b_ref[:, 0:128], b_ref[:, 128:256]]
        cs = [c_ref[:, 0:128], c_ref[:, 128:256]]
        ys, st_new = _ssd_chunk(xp, bs, cs, dt_ref[...], dtt_ref[...], alr_ref[...], alc_ref[...], dsk_ref[...],
                                [st_s[p] for p in range(N_PAIRS)])
        for p in range(N_PAIRS):
            y_ref[:, 128 * p:128 * (p + 1)] = ys[p]
            st_s[p] = st_new[p]

    return pl.pallas_call(
        body,
        out_shape=[jax.ShapeDtypeStruct((lp, SSM_WIDTH), F32), jax.ShapeDtypeStruct((nc, N_PAIRS, 128, 128), F32)],
        grid=(nc,),
        in_specs=[_rb(q, SSM_WIDTH), _rb(q, 256, 0), _rb(q, 256, 1), _rb(q, SSM_HEADS),
                  pl.BlockSpec((SSM_HEADS, q), lambda i: (0, i)),
                  _full((1, SSM_HEADS)), _full((SSM_HEADS, 1)), _full((1, SSM_HEADS))],
        out_specs=[_rb(q, SSM_WIDTH), pl.BlockSpec((1, N_PAIRS, 128, 128), lambda i: (i, 0, 0, 0))],
        scratch_shapes=[pltpu.VMEM((N_PAIRS, 128, 128), F32)],
        name="ssd_fwd", compiler_params=_params(("arbitrary",)))(xs, bc, bc, dt, dt_t, alr, alc, dsk)


def _ssd_bwd(xs, bc, dt, dt_t, alr, alc, dsk, sprev, dy):
    lp = xs.shape[0]
    q = SSD_CHUNK
    nc = lp // q

    def body(x_ref, b_ref, c_ref, dt_ref, dtt_ref, alr_ref, alc_ref, dsk_ref, sp_ref, dy_ref,
             dx_ref, dbc_ref, ddt_ref, ddtt_ref, dalr_ref, dalc_ref, ddsk_ref, ds_s):
        @pl.when(pl.program_id(0) == 0)
        def _():
            ds_s[...] = jnp.zeros_like(ds_s)

        xp = [x_ref[:, 128 * p:128 * (p + 1)] for p in range(N_PAIRS)]
        bs = [b_ref[:, 0:128], b_ref[:, 128:256]]
        cs = [c_ref[:, 0:128], c_ref[:, 128:256]]
        st = [sp_ref[0, p] for p in range(N_PAIRS)]
        _, vjp = jax.vjp(_ssd_chunk, xp, bs, cs, dt_ref[...], dtt_ref[...], alr_ref[...], alc_ref[...], dsk_ref[...],
                         st)
        dys = [dy_ref[:, 128 * p:128 * (p + 1)] for p in range(N_PAIRS)]
        dxp, dbs, dcs, ddt, ddtt, dalr, dalc, ddsk, dst = vjp((dys, [ds_s[p] for p in range(N_PAIRS)]))
        for p in range(N_PAIRS):
            dx_ref[:, 128 * p:128 * (p + 1)] = dxp[p]
            ds_s[p] = dst[p]
        for g in range(2):
            dbc_ref[:, 128 * g:128 * (g + 1)] = dbs[g]
            dbc_ref[:, 256 + 128 * g:256 + 128 * (g + 1)] = dcs[g]
        ddt_ref[...] = ddt
        ddtt_ref[...] = ddtt
        _acc_add(dalr_ref, dalr)
        _acc_add(dalc_ref, dalc)
        _acc_add(ddsk_ref, ddsk)

    rev = lambda width, cb=0: pl.BlockSpec((q, width), lambda i: (nc - 1 - i, cb))
    return pl.pallas_call(
        body,
        out_shape=[jax.ShapeDtypeStruct((lp, SSM_WIDTH), F32), jax.ShapeDtypeStruct((lp, 512), F32),
                   jax.ShapeDtypeStruct((lp, SSM_HEADS), F32), jax.ShapeDtypeStruct((SSM_HEADS, lp), F32),
                   jax.ShapeDtypeStruct((1, SSM_HEADS), F32), jax.ShapeDtypeStruct((SSM_HEADS, 1), F32),
                   jax.ShapeDtypeStruct((1, SSM_HEADS), F32)],
        grid=(nc,),
        in_specs=[rev(SSM_WIDTH), rev(256, 0), rev(256, 1), rev(SSM_HEADS),
                  pl.BlockSpec((SSM_HEADS, q), lambda i: (0, nc - 1 - i)),
                  _full((1, SSM_HEADS)), _full((SSM_HEADS, 1)), _full((1, SSM_HEADS)),
                  pl.BlockSpec((1, N_PAIRS, 128, 128), lambda i: (nc - 1 - i, 0, 0, 0)), rev(SSM_WIDTH)],
        out_specs=[rev(SSM_WIDTH), rev(512), rev(SSM_HEADS), pl.BlockSpec((SSM_HEADS, q), lambda i: (0, nc - 1 - i)),
                   _full((1, SSM_HEADS)), _full((SSM_HEADS, 1)), _full((1, SSM_HEADS))],
        scratch_shapes=[pltpu.VMEM((N_PAIRS, 128, 128), F32)],
        name="ssd_bwd", compiler_params=_params(("arbitrary",)))(xs, bc, bc, dt, dt_t, alr, alc, dsk, sprev, dy)


def _q_to_slab_order(w):
    hd = QK_NOPE + QK_ROPE
    nope = [w[:, h * hd:h * hd + QK_NOPE] for h in range(ATT_HEADS)]
    rope = [w[:, h * hd + QK_NOPE:(h + 1) * hd] for h in range(ATT_HEADS)]
    return jnp.concatenate(nope + rope, axis=1)


def _q_from_slab_order(wp):
    base = ATT_HEADS * QK_NOPE
    parts = []
    for h in range(ATT_HEADS):
        parts += [wp[:, QK_NOPE * h:QK_NOPE * (h + 1)], wp[:, base + QK_ROPE * h:base + QK_ROPE * (h + 1)]]
    return jnp.concatenate(parts, axis=1)


_IN_CQ, _IN_CKV, _IN_KR, _IN_Z, _IN_XS, _IN_BC, _IN_DT = (0, 384), (384, 640), (640, 704), (704, 1728), (1728, 2752), \
    (2752, 3264), (3264, 3280)


def _pack_w_in(w):
    z = lambda n: jnp.zeros((w.shape[0], n), w.dtype)
    s = lambda r: w[:, r[0]:r[1]]
    return jnp.concatenate([s(_IN_CKV), s(_IN_KR), z(64), s(_IN_CQ), s(_IN_DT), z(112), z(128), s(_IN_Z), s(_IN_XS),
                            s(_IN_BC)], axis=1)


def _unpack_w_in(wp):
    s = lambda off, n: wp[:, off:off + n]
    return jnp.concatenate([s(SEG_CQ, 384), s(SEG_KV, 256), s(SEG_KR, 64), s(SEG_Z, 1024), s(SEG_XS, 1024),
                            s(SEG_BC, 512), s(SEG_DT, 16)], axis=1)


def _rope_tables(lp):
    inv_freq = ROPE_THETA ** (-jnp.arange(0, QK_ROPE, 2, dtype=F32) / QK_ROPE)
    ang = jnp.arange(lp, dtype=F32)[:, None] * inv_freq[None, :]
    cos, sin = jnp.cos(ang), jnp.sin(ang)
    return jnp.tile(cos, (1, 4)), jnp.concatenate([-sin, sin, -sin, sin], axis=1)


def _local_step(x, tgt, w):
    n_real = x.shape[0]
    l = N_META + n_real
    lp = -(-l // MM_BLOCK) * MM_BLOCK
    pad = lambda a: jnp.concatenate([a, jnp.zeros((lp - l, D_MODEL), F32)], axis=0)
    h0 = pad(jnp.concatenate([w["meta_tokens"], x], axis=0))
    tgt_p = pad(jnp.concatenate([jnp.zeros((N_META, D_MODEL), F32), tgt], axis=0))
    cos_t, sin_t = _rope_tables(lp)

    w_in_p = _pack_w_in(w["w_in"])
    w_q_p = _q_to_slab_order(w["w_q_up"])
    w_kv, w_out, w_up, w_down = w["w_kv_up"], w["w_out"], w["w_mlp_up"], w["w_mlp_down"]
    conv_w, conv_b = w["conv_w"], w["conv_b"]
    dt_bias_pad = jnp.concatenate([w["dt_bias"], jnp.zeros((1, 128 - SSM_HEADS), F32)], axis=1)
    alr, dsk = w["a_log"], w["d_skip"]
    alc = alr.reshape(SSM_HEADS, 1)

    n1 = _norm_in(h0, w["norm_mix_pre"])
    proj = _mm("proj", n1, w_in_p)
    cqn, ckvn, krr = _attn_prep(proj, w["q_a_norm"], w["kv_a_norm"], cos_t, sin_t)
    q = _mm("q_up", cqn, w_q_p)
    kv = _mm("kv_up", ckvn, w_kv)
    qs, ks, vs, vts = _qk_pack(q, kv, krr, cos_t, sin_t)
    att, lse2 = _attn_fwd(qs, ks, vts)
    xs_pre, xs_act = _conv_fwd("conv_xs_fwd", proj, SEG_XS, SSM_WIDTH, conv_w[:, :SSM_WIDTH], conv_b[:, :SSM_WIDTH])
    bc_pre, bc_act = _conv_fwd("conv_bc_fwd", proj, SEG_BC, 512, conv_w[:, SSM_WIDTH:], conv_b[:, SSM_WIDTH:])
    dt = _dt_fwd(proj, dt_bias_pad)[:, :SSM_HEADS]
    dt_t = dt.T
    y, sprev = _ssd_fwd(xs_act, bc_act, dt, dt_t, alr, alc, dsk)
    ssm = _gated_norm_fwd(y, proj, w["ssm_norm"])
    cat = jnp.concatenate([att.astype(BF16), ssm], axis=1)
    mix = _mm("out_proj", cat, w_out)
    h1, n2 = _mix_residual(h0, mix, w["norm_mix_post"], w["norm_mlp_pre"])
    relu2 = lambda r: jnp.square(jnp.maximum(r, 0.0))
    u, act = _mm("mlp_up", n2, w_up, outs=((F32, None), (BF16, relu2)))
    f = _mm("mlp_down", act, w_down)
    loss, dh2, df, dg_mlp_post = _loss_and_grad(h1, f, w["norm_mlp_post"], tgt_p, n_real)

    g = {"norm_mlp_post": dg_mlp_post}
    g["w_mlp_down"] = _mm_tn("d_w_mlp_down", act, df)
    du = _mm("d_mlp_act", df, w_down, outs=((BF16, lambda r, ub: r * (2.0 * jnp.maximum(ub, 0.0))),), epi_ins=(u,),
             trans_b=True)
    g["w_mlp_up"] = _mm_tn("d_w_mlp_up", n2, du)
    dn2 = _mm("d_n2", du, w_up, trans_b=True)
    dh1, dmix, g["norm_mlp_pre"], g["norm_mix_post"] = _mlp_residual_bwd(dh2, dn2, h1, w["norm_mlp_pre"], mix,
                                                                         w["norm_mix_post"])
    g["w_out"] = _mm_tn("d_w_out", cat, dmix)
    dcat, dcat16 = _mm("d_cat", dmix, w_out, outs=((F32, None), (BF16, None)), trans_b=True)
    dy, dz, g["ssm_norm"] = _gated_norm_bwd(y, proj, w["ssm_norm"], dcat)
    dxs_act, dbc_act, ddt, ddt_t, dalr, dalc, g["d_skip"] = _ssd_bwd(xs_act, bc_act, dt, dt_t, alr, alc, dsk, sprev, dy)
    g["a_log"] = dalr + dalc.reshape(1, SSM_HEADS)
    dxs, dcw_xs, dcb_xs = _conv_bwd("conv_xs_bwd", dxs_act, xs_pre, proj, SEG_XS, SSM_WIDTH, conv_w[:, :SSM_WIDTH])
    dbc, dcw_bc, dcb_bc = _conv_bwd("conv_bc_bwd", dbc_act, bc_pre, proj, SEG_BC, 512, conv_w[:, SSM_WIDTH:])
    g["conv_w"] = jnp.concatenate([dcw_xs[:CONV_K], dcw_bc[:CONV_K]], axis=1)
    g["conv_b"] = jnp.concatenate([dcb_xs, dcb_bc], axis=1)
    ddt_pad = jnp.concatenate([ddt + ddt_t.T, jnp.zeros((lp, 128 - SSM_HEADS), F32)], axis=1)

    dqs, dks, dvs = _attn_bwd(qs, ks, vs, dcat16, lse2, _attn_delta(dcat, att))
    dq, dkv, dkr = _qk_unpack_bwd(dqs, dks, dvs, cos_t, sin_t)
    g["w_q_up"] = _q_from_slab_order(_mm_tn("d_w_q_up", cqn, dq))
    g["w_kv_up"] = _mm_tn("d_w_kv_up", ckvn, dkv)
    dcqn = _mm("d_cqn", dq, w_q_p, trans_b=True)
    dckvn = _mm("d_ckvn", dkv, w_kv, trans_b=True)
    dproj, g["q_a_norm"], g["kv_a_norm"], ddtb = _proj_grad(proj, dcqn, dckvn, w["q_a_norm"], w["kv_a_norm"], dkr,
                                                          ddt_pad, dt_bias_pad, dz, dxs, dbc)
    g["dt_bias"] = ddtb[:, :SSM_HEADS]
    g["w_in"] = _unpack_w_in(_mm_tn("d_w_in", n1, dproj))
    dn1 = _mm("d_n1", dproj, w_in_p, trans_b=True)
    dh0, g["norm_mix_pre"] = _input_norm_bwd(dh1, dn1, h0, w["norm_mix_pre"])
    g["meta_tokens"] = dh0[:N_META]
    return loss, dh0, g


WEIGHTS = ["meta_tokens", "norm_mix_pre", "w_in", "q_a_norm", "w_q_up", "kv_a_norm", "w_kv_up", "conv_w", "conv_b",
           "dt_bias", "a_log", "d_skip", "ssm_norm", "w_out", "norm_mix_post", "norm_mlp_pre", "w_mlp_up",
           "w_mlp_down", "norm_mlp_post"]
SHARD_AXIS = {"meta_tokens": 1, "w_in": 1, "w_q_up": 1, "w_kv_up": 1, "conv_w": 1, "w_out": 0, "w_mlp_up": 1,
              "w_mlp_down": 0}
FULL_SHAPE = {"meta_tokens": (16, 1024), "norm_mix_pre": (1, 1024), "w_in": (1024, 3280), "q_a_norm": (1, 384),
              "w_q_up": (384, 1536), "kv_a_norm": (1, 256), "w_kv_up": (256, 2048), "conv_w": (4, 1536),
              "conv_b": (1, 1536), "dt_bias": (1, 16), "a_log": (1, 16), "d_skip": (1, 16), "ssm_norm": (1, 1024),
              "w_out": (2048, 1024), "norm_mix_post": (1, 1024), "norm_mlp_pre": (1, 1024), "w_mlp_up": (1024, 4096),
              "w_mlp_down": (4096, 1024), "norm_mlp_post": (1, 1024)}
GATHER_BF16 = ["w_in", "w_q_up", "w_kv_up", "w_out", "w_mlp_up", "w_mlp_down"]
GATHER_F32 = ["meta_tokens", "conv_w"]


def _shard_shape(name):
    shp = list(FULL_SHAPE[name])
    if name in SHARD_AXIS:
        shp[SHARD_AXIS[name]] //= N_CHIPS
    return tuple(shp)


PACK_ORDER = sorted(WEIGHTS, key=lambda n: -_shard_shape(n)[0])


def _packed_rows(shape):
    r, c = shape
    return r if c <= PACK_W else -(-c // PACK_W)


def _pack_rows(arrays, row_multiple):
    parts = []
    for a in arrays:
        r, c = a.shape
        if c > PACK_W:
            assert r == 1, a.shape
            folded = _packed_rows(a.shape)
            a = jnp.pad(a, ((0, 0), (0, folded * PACK_W - c))).reshape(folded, PACK_W)
        elif c < PACK_W:
            a = jnp.pad(a, ((0, 0), (0, PACK_W - c)))
        parts.append(a)
    rows = sum(p.shape[0] for p in parts)
    if rows % row_multiple:
        parts.append(jnp.zeros((row_multiple - rows % row_multiple, PACK_W), parts[0].dtype))
    return jnp.concatenate(parts, axis=0)


def _unpack_rows(packed, shapes):
    out, off = [], 0
    for r, c in shapes:
        nr = _packed_rows((r, c))
        blk = packed[off:off + nr]
        out.append(blk[:, :c] if c <= PACK_W else blk.reshape(1, nr * PACK_W)[:, :c])
        off += nr
    return out


def _chip_slice(full, name, t):
    if name not in SHARD_AXIS:
        return full
    ax = SHARD_AXIS[name]
    n = FULL_SHAPE[name][ax] // N_CHIPS
    return lax.slice_in_dim(full, t * n, (t + 1) * n, axis=ax)


HBM_SPEC = pl.BlockSpec(memory_space=pl.ANY)
CHIP_FLIPS = ((1, 0), (0, 1), (1, 1))


def _gather_chips(bufs):
    nb = len(bufs)

    def body(*refs):
        ins, outs = refs[:nb], refs[nb:2 * nb]
        send, recv, loc = refs[2 * nb:]
        x, y, c = lax.axis_index("x"), lax.axis_index("y"), lax.axis_index("c")
        me = 2 * x + y
        copies = []
        for b in range(nb):
            own = pltpu.make_async_copy(ins[b], outs[b].at[me], loc.at[b])
            own.start()
            copies.append(own)
            for k, (fx, fy) in enumerate(CHIP_FLIPS):
                cp = pltpu.make_async_remote_copy(
                    src_ref=ins[b], dst_ref=outs[b].at[me], send_sem=send.at[b, k], recv_sem=recv.at[b, k],
                    device_id=(x ^ fx, y ^ fy, c), device_id_type=MESH_ID)
                cp.start()
                copies.append(cp)
        for cp in copies:
            cp.wait()

    return pl.pallas_call(
        body,
        out_shape=[jax.ShapeDtypeStruct((N_CHIPS,) + b.shape, b.dtype) for b in bufs],
        in_specs=[HBM_SPEC] * nb, out_specs=[HBM_SPEC] * nb,
        scratch_shapes=[pltpu.SemaphoreType.DMA((nb, 3)), pltpu.SemaphoreType.DMA((nb, 3)),
                        pltpu.SemaphoreType.DMA((nb,))],
        name="gather_chips")(*bufs)


def _sibling_swap(name, buf):
    def body(src, dst, send, recv):
        x, y, c = lax.axis_index("x"), lax.axis_index("y"), lax.axis_index("c")
        cp = pltpu.make_async_remote_copy(src_ref=src, dst_ref=dst, send_sem=send, recv_sem=recv,
                                          device_id=(x, y, 1 - c), device_id_type=MESH_ID)
        cp.start()
        cp.wait()

    return pl.pallas_call(
        body, out_shape=jax.ShapeDtypeStruct(buf.shape, buf.dtype), in_specs=[HBM_SPEC], out_specs=HBM_SPEC,
        scratch_shapes=[pltpu.SemaphoreType.DMA, pltpu.SemaphoreType.DMA], name=name)(buf)


def _scatter_chips(part):
    def body(src, dst, send, recv):
        x, y, c = lax.axis_index("x"), lax.axis_index("y"), lax.axis_index("c")
        copies = []
        for k, (fx, fy) in enumerate(CHIP_FLIPS):
            tx, ty = x ^ fx, y ^ fy
            cp = pltpu.make_async_remote_copy(
                src_ref=src.at[2 * tx + ty], dst_ref=dst.at[k], send_sem=send.at[k], recv_sem=recv.at[k],
                device_id=(tx, ty, c), device_id_type=MESH_ID)
            cp.start()
            copies.append(cp)
        for cp in copies:
            cp.wait()

    return pl.pallas_call(
        body, out_shape=jax.ShapeDtypeStruct((3,) + part.shape[1:], part.dtype), in_specs=[HBM_SPEC],
        out_specs=HBM_SPEC, scratch_shapes=[pltpu.SemaphoreType.DMA((3,)), pltpu.SemaphoreType.DMA((3,))],
        name="scatter_chips")(part)


def _add_rows(name, terms):
    rows = terms[0].shape[0]
    t = _row_tile(rows)

    def body(*refs):
        acc = refs[0][...]
        for r in refs[1:-1]:
            acc = acc + r[...]
        refs[-1][...] = acc

    return pl.pallas_call(
        body, out_shape=jax.ShapeDtypeStruct(terms[0].shape, F32), grid=(rows // t,),
        in_specs=[_rb(t, PACK_W)] * len(terms), out_specs=_rb(t, PACK_W),
        name=name, compiler_params=_params(("parallel",)))(*terms)


def _row_tile(rows):
    best = 8
    for t in range(8, 513, 8):
        if rows % t == 0:
            best = t
    return best


def _adamw(g, w, m, v):
    rows = g.shape[0]
    t = _row_tile(rows)
    c1 = 1.0 - ADAM_B1 ** ADAM_STEP
    c2 = 1.0 - ADAM_B2 ** ADAM_STEP

    def body(g_ref, w_ref, m_ref, v_ref, d_ref, mo_ref, vo_ref):
        gg = g_ref[...]
        mn = ADAM_B1 * m_ref[...] + (1.0 - ADAM_B1) * gg
        vn = ADAM_B2 * v_ref[...] + (1.0 - ADAM_B2) * (gg * gg)
        d_ref[...] = -ADAM_LR * ((mn / c1) / (jnp.sqrt(vn / c2) + ADAM_EPS) + ADAM_WD * w_ref[...])
        mo_ref[...] = mn
        vo_ref[...] = vn

    return pl.pallas_call(
        body, out_shape=[jax.ShapeDtypeStruct(g.shape, F32)] * 3, grid=(rows // t,),
        in_specs=[_rb(t, PACK_W)] * 4, out_specs=[_rb(t, PACK_W)] * 3,
        name="adamw", compiler_params=_params(("parallel",)))(g, w, m, v)


def kernel(x, meta_tokens, norm_mix_pre, w_in, q_a_norm, w_q_up, kv_a_norm, w_kv_up, conv_w, conv_b, dt_bias, a_log, d_skip, ssm_norm, w_out, norm_mix_post, norm_mlp_pre, w_mlp_up, w_mlp_down, norm_mlp_post, loss_target, m_meta_tokens, m_norm_mix_pre, m_w_in, m_q_a_norm, m_w_q_up, m_kv_a_norm, m_w_kv_up, m_conv_w, m_conv_b, m_dt_bias, m_a_log, m_d_skip, m_ssm_norm, m_w_out, m_norm_mix_post, m_norm_mlp_pre, m_w_mlp_up, m_w_mlp_down, m_norm_mlp_post, v_meta_tokens, v_norm_mix_pre, v_w_in, v_q_a_norm, v_w_q_up, v_kv_a_norm, v_w_kv_up, v_conv_w, v_conv_b, v_dt_bias, v_a_log, v_d_skip, v_ssm_norm, v_w_out, v_norm_mix_post, v_norm_mlp_pre, v_w_mlp_up, v_w_mlp_down, v_norm_mlp_post):
    given = dict(locals())
    drop = lambda name, a: a[0] if a.ndim == 3 else a
    w_loc = {n: drop(n, given[n]) for n in WEIGHTS}
    m_loc = {n: drop(n, given["m_" + n]) for n in WEIGHTS}
    v_loc = {n: drop(n, given["v_" + n]) for n in WEIGHTS}
    ix, iy, ic = lax.axis_index("x"), lax.axis_index("y"), lax.axis_index("c")
    me = 2 * ix + iy

    sent16 = _pack_rows([w_loc[n].astype(BF16) for n in GATHER_BF16], 16)
    sent32 = _pack_rows([w_loc[n] for n in GATHER_F32], 8)
    got16, got32 = _gather_chips([sent16, sent32])
    w_full = {n: w_loc[n] for n in WEIGHTS if n not in SHARD_AXIS}
    for names, got in ((GATHER_BF16, got16), (GATHER_F32, got32)):
        per_chip = [_unpack_rows(got[t], [_shard_shape(n) for n in names]) for t in range(N_CHIPS)]
        for k, n in enumerate(names):
            w_full[n] = jnp.concatenate([per_chip[t][k] for t in range(N_CHIPS)], axis=SHARD_AXIS[n])

    loss, dh0, g_full = _local_step(x[0], loss_target[0], w_full)
    n_real = x.shape[1]
    grad_x = dh0[N_META:N_META + n_real][None]

    shapes = [_shard_shape(n) for n in PACK_ORDER]
    slots = [_pack_rows([_chip_slice(g_full[n], n, t) for n in PACK_ORDER], 16) for t in range(N_CHIPS)]
    rows = slots[0].shape[0]
    half = rows // 2
    halves = lambda hh: jnp.concatenate([lax.dynamic_slice_in_dim(s, hh * half, half, axis=0) for s in slots], axis=0)
    keep, give = halves(ic), halves(1 - ic)
    from_sibling = _sibling_swap("sibling_swap", give)
    chip_part = _add_rows("chip_partial", [keep, from_sibling]).reshape(N_CHIPS, half, PACK_W)
    from_chips = _scatter_chips(chip_part)
    own = lax.dynamic_index_in_dim(chip_part, me, axis=0, keepdims=False)
    rel_of_xor = {2: 0, 1: 1, 3: 2}
    terms = []
    for t in range(N_CHIPS):
        cand = own
        for xr, k in rel_of_xor.items():
            cand = jnp.where((me ^ t) == xr, from_chips[k], cand)
        terms.append(cand)
    my_half = _add_rows("chip_total", terms)
    other_half = _sibling_swap("sibling_gather", my_half)
    g_red = jnp.where(ic == 0, jnp.concatenate([my_half, other_half], axis=0),
                      jnp.concatenate([other_half, my_half], axis=0))

    pack_loc = lambda d: _pack_rows([d[n] for n in PACK_ORDER], 16)
    delta, new_m, new_v = _adamw(g_red, pack_loc(w_loc), pack_loc(m_loc), pack_loc(v_loc))

    def outputs(packed_arr):
        parts = dict(zip(PACK_ORDER, _unpack_rows(packed_arr, shapes)))
        return [parts[n][None] if given[n].ndim == 3 else parts[n] for n in WEIGHTS]

    total = lax.psum(loss[0, 0], ("x", "y", "c"))
    return (total, grad_x, *outputs(g_red), *outputs(delta), *outputs(new_m), *outputs(new_v))
```

```python
import functools

import numpy as np
import jax
import jax.numpy as jnp
from jax import lax
from jax.experimental import pallas as pl
from jax.experimental.pallas import tpu as pltpu

F32 = jnp.float32
BF16 = jnp.bfloat16

D_MODEL = 1024
N_META = 16
EPS = 1e-6
ATT_HEADS = 8
Q_LORA = 384
KV_LORA = 256
QK_NOPE = 128
QK_ROPE = 64
V_HEAD = 128
ROPE_THETA = 10000.0
SSM_HEADS = 16
SSM_HEAD_DIM = 64
SSM_WIDTH = 1024
SSM_STATE = 128
CONV_K = 4
D_FF = 4096
ATT_SCALE = float((QK_NOPE + QK_ROPE) ** -0.5)
ATT_SCALE_LOG2 = float(ATT_SCALE * np.log2(np.e))

ADAM_LR = 0.001
ADAM_B1 = 0.9
ADAM_B2 = 0.999
ADAM_EPS = 1e-08
ADAM_WD = 0.01
ADAM_STEP = 10

SEG_KV, SEG_KR, SEG_CQ, SEG_DT, SEG_Z, SEG_XS, SEG_BC = 0, 256, 384, 768, 1024, 2048, 3072
PROJ_W = 3584
QP_W = 256

ROW_BLOCK = 256
MM_BLOCK = 512
SSD_CHUNK = 128
ATT_SPLIT = 1
ATT_BLOCK = 768
VMEM_LIMIT = 56 * 1024 * 1024
NEG_BIG = -1e30

PACK_W = 1024
N_CHIPS = 4
MESH_ID = pl.DeviceIdType.MESH


def _params(sem):
    return pltpu.CompilerParams(dimension_semantics=sem, vmem_limit_bytes=VMEM_LIMIT)


def _rb(rows, width, cb=0):
    return pl.BlockSpec((rows, width), lambda i: (i, cb))


def _full(shape):
    zeros = (0,) * len(shape)
    return pl.BlockSpec(shape, lambda i: zeros)


def _acc_add(ref, val):
    first = pl.program_id(0) == 0

    @pl.when(first)
    def _():
        ref[...] = val

    @pl.when(jnp.logical_not(first))
    def _():
        ref[...] += val


def _rms(x, g):
    r = lax.rsqrt(jnp.mean(x * x, axis=-1, keepdims=True) + EPS)
    return x * r * g


def _rms_bwd(x, g, dy):
    r = lax.rsqrt(jnp.mean(x * x, axis=-1, keepdims=True) + EPS)
    dyg = dy * g
    dx = r * dyg - x * (r * r * r) * jnp.mean(x * dyg, axis=-1, keepdims=True)
    dg = jnp.sum(dy * x * r, axis=0, keepdims=True)
    return dx, dg


def _sigmoid(x):
    return 1.0 / (1.0 + jnp.exp(-x))


def _swap32(x):
    lane = lax.broadcasted_iota(jnp.int32, x.shape, 1)
    return jnp.where((lane % 64) < 32, pltpu.roll(x, 96, 1), pltpu.roll(x, 32, 1))


def _rope(x, cos_t, sin_t):
    return x * cos_t + _swap32(x) * sin_t


def _rope_bwd(dr, cos_t, sin_t):
    return dr * cos_t + _swap32(dr * sin_t)


def _tile(n, cap):
    if n <= cap:
        return n
    best = 128
    for t in range(128, cap + 1, 128):
        if n % t == 0:
            best = t
    assert n % best == 0, (n, cap)
    return best


MM_VMEM_BUDGET = 40 * 1024 * 1024


def _mm(name, a, b, outs=((F32, None),), epi_ins=(), trans_b=False):
    m, k = a.shape
    n = b.shape[0] if trans_b else b.shape[1]
    tm = MM_BLOCK
    n_epi = len(epi_ins)
    out_bytes = sum(jnp.dtype(dt).itemsize for dt, _ in outs) + sum(e.dtype.itemsize for e in epi_ins)
    step_bytes = lambda tn: 2 * (tm * k * a.dtype.itemsize + k * tn * b.dtype.itemsize + tm * tn * out_bytes)
    tn = n
    while step_bytes(tn) > MM_VMEM_BUDGET and tn % 256 == 0:
        tn //= 2
    assert n % tn == 0 and step_bytes(tn) <= MM_VMEM_BUDGET, (name, n, tn)

    def body(a_ref, b_ref, *rest):
        epi_refs = rest[:n_epi]
        out_refs = rest[n_epi:]
        lhs, rhs = a_ref[...].astype(BF16), b_ref[...].astype(BF16)
        r = _nt(lhs, rhs) if trans_b else jnp.dot(lhs, rhs, preferred_element_type=F32)
        blocks = [e[...] for e in epi_refs]
        for o_ref, (dt, fn) in zip(out_refs, outs):
            o_ref[...] = (r if fn is None else fn(r, *blocks)).astype(dt)

    out_spec = pl.BlockSpec((tm, tn), lambda j, i: (i, j))
    b_spec = pl.BlockSpec((tn, k), lambda j, i: (j, 0)) if trans_b else pl.BlockSpec((k, tn), lambda j, i: (0, j))
    res = pl.pallas_call(
        body,
        out_shape=[jax.ShapeDtypeStruct((m, n), dt) for dt, _ in outs],
        grid=(n // tn, m // tm),
        in_specs=[pl.BlockSpec((tm, k), lambda j, i: (i, 0)), b_spec] + [out_spec] * n_epi,
        out_specs=[out_spec] * len(outs),
        name=name,
        compiler_params=_params(("parallel", "parallel")),
    )(a, b, *epi_ins)
    return res[0] if len(outs) == 1 else res


def _mm_tn(name, x, dy, ta_cap=1024, tn_cap=1024):
    l, a = x.shape
    n = dy.shape[1]
    ta, tn, tl = _tile(a, ta_cap), _tile(n, tn_cap), MM_BLOCK
    nl = l // tl

    def body(x_ref, dy_ref, o_ref):
        ll = pl.program_id(2)

        @pl.when(ll == 0)
        def _():
            o_ref[...] = jnp.zeros_like(o_ref)

        o_ref[...] += lax.dot_general(x_ref[...].astype(BF16), dy_ref[...].astype(BF16), (((0,), (0,)), ((), ())),
                                      preferred_element_type=F32)

    return pl.pallas_call(
        body,
        out_shape=jax.ShapeDtypeStruct((a, n), F32),
        grid=(a // ta, n // tn, nl),
        in_specs=[pl.BlockSpec((tl, ta), lambda i, j, ll: (ll, i)), pl.BlockSpec((tl, tn), lambda i, j, ll: (ll, j))],
        out_specs=pl.BlockSpec((ta, tn), lambda i, j, ll: (i, j)),
        name=name,
        compiler_params=_params(("parallel", "parallel", "arbitrary")),
    )(x, dy)


def _norm_in(h0, g_pre):
    lp = h0.shape[0]
    t = ROW_BLOCK

    def body(h_ref, g_ref, o_ref):
        o_ref[...] = _rms(h_ref[...], g_ref[...]).astype(BF16)

    return pl.pallas_call(
        body, out_shape=jax.ShapeDtypeStruct((lp, D_MODEL), BF16), grid=(lp // t,),
        in_specs=[_rb(t, D_MODEL), _full((1, D_MODEL))], out_specs=_rb(t, D_MODEL),
        name="norm_in", compiler_params=_params(("parallel",)))(h0, g_pre)


def _attn_prep(proj, g_q, g_kv, cos_t, sin_t):
    lp = proj.shape[0]
    t = ROW_BLOCK

    def body(ckv_ref, kr_ref, cq_ref, gq_ref, gkv_ref, cos_ref, sin_ref, cqn_ref, ckvn_ref, krr_ref):
        cqn_ref[...] = _rms(cq_ref[...], gq_ref[...]).astype(BF16)
        ckvn_ref[...] = _rms(ckv_ref[...], gkv_ref[...]).astype(BF16)
        roped = _rope(kr_ref[...], cos_ref[...], sin_ref[...])
        krr_ref[...] = roped + pltpu.roll(roped, 64, 1)

    return pl.pallas_call(
        body,
        out_shape=[jax.ShapeDtypeStruct((lp, Q_LORA), BF16), jax.ShapeDtypeStruct((lp, KV_LORA), BF16),
                   jax.ShapeDtypeStruct((lp, 128), F32)],
        grid=(lp // t,),
        in_specs=[_rb(t, KV_LORA, SEG_KV // KV_LORA), _rb(t, 128, SEG_KR // 128), _rb(t, Q_LORA, SEG_CQ // Q_LORA),
                  _full((1, Q_LORA)), _full((1, KV_LORA)), _rb(t, 128), _rb(t, 128)],
        out_specs=[_rb(t, Q_LORA), _rb(t, KV_LORA), _rb(t, 128)],
        name="attn_prep", compiler_params=_params(("parallel",)))(proj, proj, proj, g_q, g_kv, cos_t, sin_t)


def _qk_pack(q, kv, krr, cos_t, sin_t):
    lp = q.shape[0]
    t = ROW_BLOCK

    def body(q_ref, kv_ref, krr_ref, cos_ref, sin_ref, qs_ref, ks_ref, vs_ref, vts_ref):
        lane = lax.broadcasted_iota(jnp.int32, (t, 128), 1)
        lo = lane < 64
        krr = krr_ref[...].astype(BF16)
        for j in range(ATT_HEADS // 2):
            pr = _rope(q_ref[:, 1024 + 128 * j:1024 + 128 * (j + 1)], cos_ref[...], sin_ref[...])
            for h, keep in ((2 * j, lo), (2 * j + 1, jnp.logical_not(lo))):
                qs_ref[h, :, 0:128] = q_ref[:, 128 * h:128 * (h + 1)].astype(BF16)
                qs_ref[h, :, 128:256] = jnp.where(keep, pr, 0.0).astype(BF16)
        for h in range(ATT_HEADS):
            ks_ref[h, :, 0:128] = kv_ref[:, 256 * h:256 * h + 128].astype(BF16)
            ks_ref[h, :, 128:256] = krr
            v = kv_ref[:, 256 * h + 128:256 * (h + 1)]
            vs_ref[h] = v.astype(BF16)
            vts_ref[h] = v.T.astype(BF16)

    slab = lambda w: pl.BlockSpec((ATT_HEADS, t, w), lambda i: (0, i, 0))
    return pl.pallas_call(
        body,
        out_shape=[jax.ShapeDtypeStruct((ATT_HEADS, lp, QP_W), BF16), jax.ShapeDtypeStruct((ATT_HEADS, lp, QP_W), BF16),
                   jax.ShapeDtypeStruct((ATT_HEADS, lp, V_HEAD), BF16), jax.ShapeDtypeStruct((ATT_HEADS, V_HEAD, lp), BF16)],
        grid=(lp // t,),
        in_specs=[_rb(t, 1536), _rb(t, 2048), _rb(t, 128), _rb(t, 128), _rb(t, 128)],
        out_specs=[slab(QP_W), slab(QP_W), slab(V_HEAD), pl.BlockSpec((ATT_HEADS, V_HEAD, t), lambda i: (0, 0, i))],
        name="qk_pack", compiler_params=_params(("parallel",)))(q, kv, krr, cos_t, sin_t)


def _shifted(ext, t, shift):
    if shift == 0:
        return ext[8:, :]
    return pltpu.roll(ext, shift, 0)[8:, :]


def _conv_fwd(name, proj, seg, width, conv_w, conv_b):
    lp = proj.shape[0]
    t = ROW_BLOCK
    cb = seg // width

    def body(u_ref, halo_ref, w_ref, b_ref, pre_ref, act_ref):
        i = pl.program_id(0)
        u = u_ref[...]
        halo = jnp.where(i > 0, halo_ref[...], 0.0)
        ext = jnp.concatenate([halo, u], axis=0)
        pre = jnp.broadcast_to(b_ref[...], (t, width))
        for k in range(CONV_K):
            pre = pre + w_ref[k:k + 1, :] * _shifted(ext, t, CONV_K - 1 - k)
        pre_ref[...] = pre
        act_ref[...] = pre * _sigmoid(pre)

    return pl.pallas_call(
        body,
        out_shape=[jax.ShapeDtypeStruct((lp, width), F32)] * 2,
        grid=(lp // t,),
        in_specs=[_rb(t, width, cb),
                  pl.BlockSpec((8, width), lambda i: (jnp.maximum(i * (t // 8) - 1, 0), cb)),
                  _full((CONV_K, width)), _full((1, width))],
        out_specs=[_rb(t, width), _rb(t, width)],
        name=name, compiler_params=_params(("parallel",)))(proj, proj, conv_w, conv_b)


def _softplus(x):
    return jnp.maximum(x, 0.0) + jnp.log1p(jnp.exp(-jnp.abs(x)))


def _dt_fwd(proj, dt_bias_pad):
    lp = proj.shape[0]
    t = ROW_BLOCK

    def body(x_ref, b_ref, o_ref):
        o_ref[...] = _softplus(x_ref[...] + b_ref[...])

    return pl.pallas_call(
        body, out_shape=jax.ShapeDtypeStruct((lp, 128), F32), grid=(lp // t,),
        in_specs=[_rb(t, 128, SEG_DT // 128), _full((1, 128))], out_specs=_rb(t, 128),
        name="dt_fwd", compiler_params=_params(("parallel",)))(proj, dt_bias_pad)


def _gated_norm_group(y, z, w):
    g = y * (z * _sigmoid(z))
    return g * lax.rsqrt(jnp.mean(g * g, axis=-1, keepdims=True) + EPS) * w


def _gated_norm_fwd(y, proj, w):
    lp = y.shape[0]
    t = ROW_BLOCK
    gw = SSM_WIDTH // 2

    def body(y0, y1, z0, z1, w0, w1, o_ref):
        o_ref[:, 0:gw] = _gated_norm_group(y0[...], z0[...], w0[...]).astype(BF16)
        o_ref[:, gw:] = _gated_norm_group(y1[...], z1[...], w1[...]).astype(BF16)

    zb = SEG_Z // gw
    return pl.pallas_call(
        body, out_shape=jax.ShapeDtypeStruct((lp, SSM_WIDTH), BF16), grid=(lp // t,),
        in_specs=[_rb(t, gw, 0), _rb(t, gw, 1), _rb(t, gw, zb), _rb(t, gw, zb + 1),
                  pl.BlockSpec((1, gw), lambda i: (0, 0)), pl.BlockSpec((1, gw), lambda i: (0, 1))],
        out_specs=_rb(t, SSM_WIDTH),
        name="gated_norm_fwd", compiler_params=_params(("parallel",)))(y, y, proj, proj, w, w)


def _gated_norm_bwd(y, proj, w, dssm):
    lp = y.shape[0]
    t = ROW_BLOCK
    gw = SSM_WIDTH // 2

    def body(y0, y1, z0, z1, w0, w1, d0, d1, dy_ref, dz_ref, dw_ref):
        dws = []
        for g, (yr, zr, wr, dr) in enumerate(((y0, z0, w0, d0), (y1, z1, w1, d1))):
            _, vjp = jax.vjp(_gated_norm_group, yr[...], zr[...], wr[...])
            dyg, dzg, dwg = vjp(dr[...])
            dy_ref[:, g * gw:(g + 1) * gw] = dyg
            dz_ref[:, g * gw:(g + 1) * gw] = dzg
            dws.append(dwg)
        _acc_add(dw_ref, jnp.concatenate(dws, axis=1))

    zb = SEG_Z // gw
    return pl.pallas_call(
        body,
        out_shape=[jax.ShapeDtypeStruct((lp, SSM_WIDTH), F32), jax.ShapeDtypeStruct((lp, SSM_WIDTH), F32),
                   jax.ShapeDtypeStruct((1, SSM_WIDTH), F32)],
        grid=(lp // t,),
        in_specs=[_rb(t, gw, 0), _rb(t, gw, 1), _rb(t, gw, zb), _rb(t, gw, zb + 1),
                  pl.BlockSpec((1, gw), lambda i: (0, 0)), pl.BlockSpec((1, gw), lambda i: (0, 1)),
                  _rb(t, gw, 2), _rb(t, gw, 3)],
        out_specs=[_rb(t, SSM_WIDTH), _rb(t, SSM_WIDTH), _full((1, SSM_WIDTH))],
        name="gated_norm_bwd", compiler_params=_params(("arbitrary",)))(y, y, proj, proj, w, w, dssm, dssm)


def _mix_residual(h0, mix, g_post, g_mlp_pre):
    lp = h0.shape[0]
    t = ROW_BLOCK

    def body(h_ref, m_ref, gp_ref, gm_ref, h1_ref, n2_ref):
        h1 = h_ref[...] + _rms(m_ref[...], gp_ref[...])
        h1_ref[...] = h1
        n2_ref[...] = _rms(h1, gm_ref[...]).astype(BF16)

    return pl.pallas_call(
        body, out_shape=[jax.ShapeDtypeStruct((lp, D_MODEL), F32), jax.ShapeDtypeStruct((lp, D_MODEL), BF16)],
        grid=(lp // t,),
        in_specs=[_rb(t, D_MODEL), _rb(t, D_MODEL), _full((1, D_MODEL)), _full((1, D_MODEL))],
        out_specs=[_rb(t, D_MODEL), _rb(t, D_MODEL)],
        name="mix_residual", compiler_params=_params(("parallel",)))(h0, mix, g_post, g_mlp_pre)


def _loss_and_grad(h1, f, g_post, tgt, n_real):
    lp = h1.shape[0]
    t = ROW_BLOCK

    def body(h1_ref, f_ref, g_ref, t_ref, loss_ref, dh2_ref, df_ref, dg_ref):
        i = pl.program_id(0)
        fx = f_ref[...]
        h2 = h1_ref[...] + _rms(fx, g_ref[...])
        row = i * t + lax.broadcasted_iota(jnp.int32, (t, 1), 0)
        real = jnp.logical_and(row >= N_META, row < N_META + n_real)
        diff = jnp.where(real, h2 - t_ref[...], 0.0)
        part = 0.5 * jnp.sum(jnp.sum(diff * diff, axis=-1, keepdims=True) / D_MODEL, axis=0, keepdims=True)
        _acc_add(loss_ref, jnp.broadcast_to(part, (1, 128)))
        dh2 = diff / D_MODEL
        dh2_ref[...] = dh2
        dfx, dg = _rms_bwd(fx, g_ref[...], dh2)
        df_ref[...] = dfx.astype(BF16)
        _acc_add(dg_ref, dg)

    return pl.pallas_call(
        body,
        out_shape=[jax.ShapeDtypeStruct((1, 128), F32), jax.ShapeDtypeStruct((lp, D_MODEL), F32),
                   jax.ShapeDtypeStruct((lp, D_MODEL), BF16), jax.ShapeDtypeStruct((1, D_MODEL), F32)],
        grid=(lp // t,),
        in_specs=[_rb(t, D_MODEL), _rb(t, D_MODEL), _full((1, D_MODEL)), _rb(t, D_MODEL)],
        out_specs=[_full((1, 128)), _rb(t, D_MODEL), _rb(t, D_MODEL), _full((1, D_MODEL))],
        name="loss_and_grad", compiler_params=_params(("arbitrary",)))(h1, f, g_post, tgt)


def _mlp_residual_bwd(dh2, dn2, h1, g_mlp_pre, mix, g_post):
    lp = h1.shape[0]
    t = ROW_BLOCK

    def body(dh2_ref, dn2_ref, h1_ref, gm_ref, mix_ref, gp_ref, dh1_ref, dmix_ref, dgm_ref, dgp_ref):
        dx, dgm = _rms_bwd(h1_ref[...], gm_ref[...], dn2_ref[...])
        dh1 = dh2_ref[...] + dx
        dh1_ref[...] = dh1
        dmix, dgp = _rms_bwd(mix_ref[...], gp_ref[...], dh1)
        dmix_ref[...] = dmix.astype(BF16)
        _acc_add(dgm_ref, dgm)
        _acc_add(dgp_ref, dgp)

    return pl.pallas_call(
        body,
        out_shape=[jax.ShapeDtypeStruct((lp, D_MODEL), F32), jax.ShapeDtypeStruct((lp, D_MODEL), BF16),
                   jax.ShapeDtypeStruct((1, D_MODEL), F32), jax.ShapeDtypeStruct((1, D_MODEL), F32)],
        grid=(lp // t,),
        in_specs=[_rb(t, D_MODEL), _rb(t, D_MODEL), _rb(t, D_MODEL), _full((1, D_MODEL)), _rb(t, D_MODEL),
                  _full((1, D_MODEL))],
        out_specs=[_rb(t, D_MODEL), _rb(t, D_MODEL), _full((1, D_MODEL)), _full((1, D_MODEL))],
        name="mlp_residual_bwd", compiler_params=_params(("arbitrary",)))(dh2, dn2, h1, g_mlp_pre, mix, g_post)


def _input_norm_bwd(dh1, dn1, h0, g_pre):
    lp = h0.shape[0]
    t = ROW_BLOCK

    def body(dh1_ref, dn1_ref, h0_ref, g_ref, dh0_ref, dg_ref):
        dx, dg = _rms_bwd(h0_ref[...], g_ref[...], dn1_ref[...])
        dh0_ref[...] = dh1_ref[...] + dx
        _acc_add(dg_ref, dg)

    return pl.pallas_call(
        body, out_shape=[jax.ShapeDtypeStruct((lp, D_MODEL), F32), jax.ShapeDtypeStruct((1, D_MODEL), F32)],
        grid=(lp // t,),
        in_specs=[_rb(t, D_MODEL), _rb(t, D_MODEL), _rb(t, D_MODEL), _full((1, D_MODEL))],
        out_specs=[_rb(t, D_MODEL), _full((1, D_MODEL))],
        name="input_norm_bwd", compiler_params=_params(("arbitrary",)))(dh1, dn1, h0, g_pre)


def _conv_bwd(name, dact, pre, proj, seg, width, conv_w):
    lp = proj.shape[0]
    t = ROW_BLOCK
    cb = seg // width
    nblk = lp // t

    def dsilu(p):
        s = _sigmoid(p)
        return s * (1.0 + p * (1.0 - s))

    def body(da_ref, dan_ref, pre_ref, pren_ref, u_ref, halo_ref, w_ref, du_ref, dw_ref, db_ref):
        i = pl.program_id(0)
        dpre = da_ref[...] * dsilu(pre_ref[...])
        dpre_next = jnp.where(i < nblk - 1, dan_ref[...] * dsilu(pren_ref[...]), 0.0)
        extd = jnp.concatenate([dpre, dpre_next], axis=0)
        halo = jnp.where(i > 0, halo_ref[...], 0.0)
        ext = jnp.concatenate([halo, u_ref[...]], axis=0)
        du = jnp.zeros((t, width), F32)
        dws = []
        for k in range(CONV_K):
            m = CONV_K - 1 - k
            ahead = dpre if m == 0 else pltpu.roll(extd, t + 8 - m, 0)[:t, :]
            du = du + w_ref[k:k + 1, :] * ahead
            dws.append(jnp.sum(dpre * _shifted(ext, t, m), axis=0, keepdims=True))
        du_ref[...] = du
        _acc_add(dw_ref, jnp.concatenate(dws + [jnp.zeros((8 - CONV_K, width), F32)], axis=0))
        _acc_add(db_ref, jnp.sum(dpre, axis=0, keepdims=True))

    nxt = lambda i: (jnp.minimum((i + 1) * (t // 8), lp // 8 - 1), 0)
    return pl.pallas_call(
        body,
        out_shape=[jax.ShapeDtypeStruct((lp, width), F32), jax.ShapeDtypeStruct((8, width), F32),
                   jax.ShapeDtypeStruct((1, width), F32)],
        grid=(nblk,),
        in_specs=[_rb(t, width), pl.BlockSpec((8, width), nxt), _rb(t, width), pl.BlockSpec((8, width), nxt),
                  _rb(t, width, cb),
                  pl.BlockSpec((8, width), lambda i: (jnp.maximum(i * (t // 8) - 1, 0), cb)),
                  _full((CONV_K, width))],
        out_specs=[_rb(t, width), _full((8, width)), _full((1, width))],
        name=name, compiler_params=_params(("arbitrary",)))(dact, dact, pre, pre, proj, proj, conv_w)


def _qk_unpack_bwd(dqs, dks, dvs, cos_t, sin_t):
    lp = dqs.shape[1]
    t = ROW_BLOCK

    def body(dqs_ref, dks_ref, dvs_ref, cos_ref, sin_ref, dq_ref, dkv_ref, dkr_ref):
        lane = lax.broadcasted_iota(jnp.int32, (t, 128), 1)
        lo = lane < 64
        for j in range(ATT_HEADS // 2):
            dpr = jnp.where(lo, dqs_ref[2 * j, :, 128:256], dqs_ref[2 * j + 1, :, 128:256])
            dq_ref[:, 1024 + 128 * j:1024 + 128 * (j + 1)] = _rope_bwd(dpr, cos_ref[...], sin_ref[...]).astype(BF16)
        dkrr = jnp.zeros((t, 128), F32)
        for h in range(ATT_HEADS):
            dq_ref[:, 128 * h:128 * (h + 1)] = dqs_ref[h, :, 0:128].astype(BF16)
            dkv_ref[:, 256 * h:256 * h + 128] = dks_ref[h, :, 0:128].astype(BF16)
            dkv_ref[:, 256 * h + 128:256 * (h + 1)] = dvs_ref[h].astype(BF16)
            dkrr = dkrr + dks_ref[h, :, 128:256]
        droped = jnp.where(lo, dkrr + pltpu.roll(dkrr, 64, 1), 0.0)
        dkr_ref[...] = _rope_bwd(droped, cos_ref[...], sin_ref[...])

    slab = lambda w: pl.BlockSpec((ATT_HEADS, t, w), lambda i: (0, i, 0))
    return pl.pallas_call(
        body,
        out_shape=[jax.ShapeDtypeStruct((lp, 1536), BF16), jax.ShapeDtypeStruct((lp, 2048), BF16),
                   jax.ShapeDtypeStruct((lp, 128), F32)],
        grid=(lp // t,),
        in_specs=[slab(QP_W), slab(QP_W), slab(V_HEAD), _rb(t, 128), _rb(t, 128)],
        out_specs=[_rb(t, 1536), _rb(t, 2048), _rb(t, 128)],
        name="qk_unpack_bwd", compiler_params=_params(("parallel",)))(dqs, dks, dvs, cos_t, sin_t)


def _proj_grad(proj, dcqn, dckvn, g_q, g_kv, dkr, ddt_pad, dt_bias_pad, dz, dxs, dbc):
    lp = proj.shape[0]
    t = ROW_BLOCK

    def body(ckv_ref, cq_ref, pdt_ref, dcq_ref, dckv_ref, gq_ref, gkv_ref, dkr_ref, ddt_ref, b_ref, dz_ref, dxs_ref,
             dbc_ref, dp_ref, dgq_ref, dgkv_ref, db_ref):
        dckv, dgkv = _rms_bwd(ckv_ref[...], gkv_ref[...], dckv_ref[...])
        dcq, dgq = _rms_bwd(cq_ref[...], gq_ref[...], dcq_ref[...])
        ddt_raw = ddt_ref[...] * _sigmoid(pdt_ref[...] + b_ref[...])
        dp_ref[:, SEG_KV:SEG_KV + KV_LORA] = dckv.astype(BF16)
        dp_ref[:, SEG_KR:SEG_KR + 128] = dkr_ref[...].astype(BF16)
        dp_ref[:, SEG_CQ:SEG_CQ + Q_LORA] = dcq.astype(BF16)
        dp_ref[:, SEG_DT:SEG_DT + 128] = ddt_raw.astype(BF16)
        dp_ref[:, SEG_DT + 128:SEG_Z] = jnp.zeros((t, SEG_Z - SEG_DT - 128), BF16)
        dp_ref[:, SEG_Z:SEG_XS] = dz_ref[...].astype(BF16)
        dp_ref[:, SEG_XS:SEG_BC] = dxs_ref[...].astype(BF16)
        dp_ref[:, SEG_BC:PROJ_W] = dbc_ref[...].astype(BF16)
        _acc_add(dgq_ref, dgq)
        _acc_add(dgkv_ref, dgkv)
        _acc_add(db_ref, jnp.sum(ddt_raw, axis=0, keepdims=True))

    return pl.pallas_call(
        body,
        out_shape=[jax.ShapeDtypeStruct((lp, PROJ_W), BF16), jax.ShapeDtypeStruct((1, Q_LORA), F32),
                   jax.ShapeDtypeStruct((1, KV_LORA), F32), jax.ShapeDtypeStruct((1, 128), F32)],
        grid=(lp // t,),
        in_specs=[_rb(t, KV_LORA, SEG_KV // KV_LORA), _rb(t, Q_LORA, SEG_CQ // Q_LORA), _rb(t, 128, SEG_DT // 128),
                  _rb(t, Q_LORA), _rb(t, KV_LORA), _full((1, Q_LORA)), _full((1, KV_LORA)), _rb(t, 128), _rb(t, 128),
                  _full((1, 128)), _rb(t, SSM_WIDTH), _rb(t, SSM_WIDTH), _rb(t, 512)],
        out_specs=[_rb(t, PROJ_W), _full((1, Q_LORA)), _full((1, KV_LORA)), _full((1, 128))],
        name="proj_grad", compiler_params=_params(("arbitrary",)))(
            proj, proj, proj, dcqn, dckvn, g_q, g_kv, dkr, ddt_pad, dt_bias_pad, dz, dxs, dbc)


def _pair_tables(n):
    qmaj = [(i, j) for i in range(n) for j in range(i + 1)]
    kmaj = [(i, j) for j in range(n) for i in range(j, n)]
    to = lambda ps, c: jnp.asarray(np.array([p[c] for p in ps], np.int32))
    return (to(qmaj, 0), to(qmaj, 1)), (to(kmaj, 0), to(kmaj, 1))


def _att_block(lp):
    return ATT_BLOCK if lp % ATT_BLOCK == 0 else MM_BLOCK


def _nt(a, b):
    return lax.dot_general(a, b, (((1,), (1,)), ((), ())), preferred_element_type=F32)


def _attn_fwd(qs, ks, vts):
    lp = qs.shape[1]
    t = _att_block(lp)
    n = lp // t
    (qi, kj), _ = _pair_tables(n)
    tc = t // ATT_SPLIT

    def body(qi_ref, kj_ref, q_ref, k_ref, vt_ref, o_ref, lse_ref, m_s, l_s, acc_s):
        p = pl.program_id(1)
        i, j = qi_ref[p], kj_ref[p]

        @pl.when(j == 0)
        def _():
            m_s[...] = jnp.full_like(m_s, NEG_BIG)
            l_s[...] = jnp.zeros_like(l_s)
            acc_s[...] = jnp.zeros_like(acc_s)

        def update(masked):
            m_all, l_all, acc_all = m_s[...], l_s[...], acc_s[...]
            m_out, l_out, acc_out = [], [], []
            qk = lambda c: _nt(k_ref[0], q_ref[0, c * tc:(c + 1) * tc, :])
            ahead = qk(0)
            for c in range(ATT_SPLIT):
                cols = slice(c * tc, (c + 1) * tc)
                sc = ahead * ATT_SCALE_LOG2
                if c + 1 < ATT_SPLIT:
                    ahead = qk(c + 1)
                if masked:
                    keep = (lax.broadcasted_iota(jnp.int32, (t, tc), 1) + c * tc
                            >= lax.broadcasted_iota(jnp.int32, (t, tc), 0))
                    sc = jnp.where(keep, sc, NEG_BIG)
                m_prev = m_all[:, cols]
                m_new = jnp.maximum(m_prev, jnp.max(sc, axis=0, keepdims=True))
                alpha = jnp.exp2(m_prev - m_new)
                pexp = jnp.exp2(sc - m_new)
                l_out.append(alpha * l_all[:, cols] + jnp.sum(pexp, axis=0, keepdims=True))
                acc_out.append(alpha * acc_all[:, cols] + jnp.dot(vt_ref[0], pexp.astype(BF16),
                                                                  preferred_element_type=F32))
                m_out.append(m_new)
            cat = lambda parts: parts[0] if len(parts) == 1 else jnp.concatenate(parts, axis=1)
            m_s[...], l_s[...], acc_s[...] = cat(m_out), cat(l_out), cat(acc_out)

        @pl.when(j < i)
        def _():
            update(False)

        @pl.when(j == i)
        def _():
            update(True)
            o_ref[...] = (acc_s[...] / l_s[...]).T
            lse_ref[0] = m_s[...] + jnp.log2(l_s[...])

    grid_spec = pltpu.PrefetchScalarGridSpec(
        num_scalar_prefetch=2, grid=(ATT_HEADS, int(qi.shape[0])),
        in_specs=[pl.BlockSpec((1, t, QP_W), lambda h, p, qi, kj: (h, qi[p], 0)),
                  pl.BlockSpec((1, t, QP_W), lambda h, p, qi, kj: (h, kj[p], 0)),
                  pl.BlockSpec((1, V_HEAD, t), lambda h, p, qi, kj: (h, 0, kj[p]))],
        out_specs=[pl.BlockSpec((t, V_HEAD), lambda h, p, qi, kj: (qi[p], h)),
                   pl.BlockSpec((1, 1, t), lambda h, p, qi, kj: (h, 0, qi[p]))],
        scratch_shapes=[pltpu.VMEM((1, t), F32), pltpu.VMEM((1, t), F32), pltpu.VMEM((V_HEAD, t), F32)])
    return pl.pallas_call(
        body, grid_spec=grid_spec,
        out_shape=[jax.ShapeDtypeStruct((lp, ATT_HEADS * V_HEAD), F32), jax.ShapeDtypeStruct((ATT_HEADS, 1, lp), F32)],
        name="attn_fwd", compiler_params=_params(("parallel", "arbitrary")))(qi, kj, qs, ks, vts)


def _attn_delta(datt, att):
    lp = att.shape[0]
    t = MM_BLOCK

    def body(do_ref, o_ref, d_ref):
        prod = do_ref[...] * o_ref[...]
        d_ref[0] = jnp.sum(prod.T, axis=0, keepdims=True)

    blk = pl.BlockSpec((t, V_HEAD), lambda h, i: (i, h))
    return pl.pallas_call(
        body, out_shape=jax.ShapeDtypeStruct((ATT_HEADS, 1, lp), F32), grid=(ATT_HEADS, lp // t),
        in_specs=[blk, blk], out_specs=pl.BlockSpec((1, 1, t), lambda h, i: (h, 0, i)),
        name="attn_delta", compiler_params=_params(("parallel", "parallel")))(datt, att)


def _attn_bwd(qs, ks, vs, datt16, lse2, delta):
    lp = qs.shape[1]
    t = _att_block(lp)
    n = lp // t
    _, (qi, kj) = _pair_tables(n)
    tc = t // ATT_SPLIT

    def body(qi_ref, kj_ref, k_ref, v_ref, q_ref, do_ref, lse_ref, dl_ref, dq_ref, dk_ref, dv_ref, dk_s, dv_s):
        p = pl.program_id(1)
        i, j = qi_ref[p], kj_ref[p]

        @pl.when(p == 0)
        def _():
            dq_ref[...] = jnp.zeros_like(dq_ref)

        @pl.when(i == j)
        def _():
            dk_s[...] = jnp.zeros_like(dk_s)
            dv_s[...] = jnp.zeros_like(dv_s)

        def step(masked):
            for c in range(ATT_SPLIT):
                cols = slice(c * tc, (c + 1) * tc)
                q = q_ref[0, cols, :]
                do = do_ref[cols, :]
                pt = jnp.exp2(_nt(k_ref[0], q) * ATT_SCALE_LOG2 - lse_ref[0, :, cols])
                if masked:
                    keep = (lax.broadcasted_iota(jnp.int32, (t, tc), 1) + c * tc
                            >= lax.broadcasted_iota(jnp.int32, (t, tc), 0))
                    pt = jnp.where(keep, pt, 0.0)
                dst = (pt * (_nt(v_ref[0], do) - dl_ref[0, :, cols]) * ATT_SCALE).astype(BF16)
                dv_s[...] += jnp.dot(pt.astype(BF16), do, preferred_element_type=F32)
                dk_s[...] += jnp.dot(dst, q, preferred_element_type=F32)
                rows = pl.ds(pl.multiple_of(i * t + c * tc, tc), tc)
                dq_ref[0, rows, :] += lax.dot_general(dst, k_ref[0], (((0,), (0,)), ((), ())),
                                                      preferred_element_type=F32)

        @pl.when(i > j)
        def _():
            step(False)

        @pl.when(i == j)
        def _():
            step(True)

        @pl.when(i == n - 1)
        def _():
            dk_ref[0] = dk_s[...]
            dv_ref[0] = dv_s[...]

    grid_spec = pltpu.PrefetchScalarGridSpec(
        num_scalar_prefetch=2, grid=(ATT_HEADS, int(qi.shape[0])),
        in_specs=[pl.BlockSpec((1, t, QP_W), lambda h, p, qi, kj: (h, kj[p], 0)),
                  pl.BlockSpec((1, t, V_HEAD), lambda h, p, qi, kj: (h, kj[p], 0)),
                  pl.BlockSpec((1, t, QP_W), lambda h, p, qi, kj: (h, qi[p], 0)),
                  pl.BlockSpec((t, V_HEAD), lambda h, p, qi, kj: (qi[p], h)),
                  pl.BlockSpec((1, 1, t), lambda h, p, qi, kj: (h, 0, qi[p])),
                  pl.BlockSpec((1, 1, t), lambda h, p, qi, kj: (h, 0, qi[p]))],
        out_specs=[pl.BlockSpec((1, lp, QP_W), lambda h, p, qi, kj: (h, 0, 0)),
                   pl.BlockSpec((1, t, QP_W), lambda h, p, qi, kj: (h, kj[p], 0)),
                   pl.BlockSpec((1, t, V_HEAD), lambda h, p, qi, kj: (h, kj[p], 0))],
        scratch_shapes=[pltpu.VMEM((t, QP_W), F32), pltpu.VMEM((t, V_HEAD), F32)])
    return pl.pallas_call(
        body, grid_spec=grid_spec,
        out_shape=[jax.ShapeDtypeStruct((ATT_HEADS, lp, QP_W), F32), jax.ShapeDtypeStruct((ATT_HEADS, lp, QP_W), F32),
                   jax.ShapeDtypeStruct((ATT_HEADS, lp, V_HEAD), F32)],
        name="attn_bwd", compiler_params=_params(("arbitrary", "arbitrary")))(
            qi, kj, ks, vs, qs, datt16, lse2, delta)


N_PAIRS = SSM_HEADS // 2
HI = lax.Precision.HIGHEST


def _ssd_chunk(xp, bs, cs, dt, dt_t, alr, alc, dsk, st):
    q = dt.shape[0]
    li = lax.broadcasted_iota(jnp.int32, (q, q), 0)
    si = lax.broadcasted_iota(jnp.int32, (q, q), 1)
    tri = (si <= li).astype(F32)
    tri_t = (li <= si).astype(F32)
    lo = lax.broadcasted_iota(jnp.int32, (1, 128), 1) < 64
    h_r = lax.broadcasted_iota(jnp.int32, (1, SSM_HEADS), 1)
    h_c = lax.broadcasted_iota(jnp.int32, (SSM_HEADS, 1), 0)
    a = dt * (-jnp.exp(alr))
    a_t = dt_t * (-jnp.exp(alc))
    acum = jnp.dot(tri, a, precision=HI, preferred_element_type=F32)
    acum_t = jnp.dot(a_t, tri_t, precision=HI, preferred_element_type=F32)
    last = (lax.broadcasted_iota(jnp.int32, (q, 1), 0) == q - 1).astype(F32)
    alast = jnp.sum(acum * last, axis=0, keepdims=True)
    e = jnp.exp(acum)
    rdt = jnp.exp(alast - acum) * dt
    e_last = jnp.exp(alast)

    def col(m, h):
        return jnp.sum(m * (h_r == h).astype(F32), axis=1, keepdims=True)

    def row(m, h):
        return jnp.sum(m * (h_c == h).astype(F32), axis=0, keepdims=True)

    def pair(m, ha):
        return jnp.where(lo, col(m, ha), col(m, ha + 1))

    ys, st_new = [], []
    for g in range(2):
        c_b = cs[g].astype(BF16)
        b_b = bs[g].astype(BF16)
        cb = _nt(c_b, b_b)
        for j in range(N_PAIRS // 2):
            p = (N_PAIRS // 2) * g + j
            ha = 2 * p
            x = xp[p]
            x_b = x.astype(BF16)

            def w_of(h):
                seg = col(acum, h) - row(acum_t, h)
                return (cb * jnp.exp(jnp.minimum(seg, 0.0)) * tri * row(dt_t, h)).astype(BF16)

            y_diag = jnp.where(lo, jnp.dot(w_of(ha), x_b, preferred_element_type=F32),
                               jnp.dot(w_of(ha + 1), x_b, preferred_element_type=F32))
            y_off = jnp.dot(c_b, st[p].astype(BF16), preferred_element_type=F32) * pair(e, ha)
            ys.append(y_diag + y_off + pair(dsk, ha) * x)
            xw = (x * pair(rdt, ha)).astype(BF16)
            st_new.append(st[p] * pair(e_last, ha)
                          + lax.dot_general(b_b, xw, (((0,), (0,)), ((), ())), preferred_element_type=F32))
    return ys, st_new


def _ssd_fwd(xs, bc, dt, dt_t, alr, alc, dsk):
    lp = xs.shape[0]
    q = SSD_CHUNK
    nc = lp // q

    def body(x_ref, b_ref, c_ref, dt_ref, dtt_ref, alr_ref, alc_ref, dsk_ref, y_ref, sp_ref, st_s):
        @pl.when(pl.program_id(0) == 0)
        def _():
            st_s[...] = jnp.zeros_like(st_s)

        sp_ref[0] = st_s[...]
        xp = [x_ref[:, 128 * p:128 * (p + 1)] for p in range(N_PAIRS)]
        bs = [b_ref[:, 0:128], b_ref[:, 128:256]]
        cs = [c_ref[:, 0:128], c_ref[:, 128:256]]
        ys, st_new = _ssd_chunk(xp, bs, cs, dt_ref[...], dtt_ref[...], alr_ref[...], alc_ref[...], dsk_ref[...],
                                [st_s[p] for p in range(N_PAIRS)])
        for p in range(N_PAIRS):
            y_ref[:, 128 * p:128 * (p + 1)] = ys[p]
            st_s[p] = st_new[p]

    return pl.pallas_call(
        body,
        out_shape=[jax.ShapeDtypeStruct((lp, SSM_WIDTH), F32), jax.ShapeDtypeStruct((nc, N_PAIRS, 128, 128), F32)],
        grid=(nc,),
        in_specs=[_rb(q, SSM_WIDTH), _rb(q, 256, 0), _rb(q, 256, 1), _rb(q, SSM_HEADS),
                  pl.BlockSpec((SSM_HEADS, q), lambda i: (0, i)),
                  _full((1, SSM_HEADS)), _full((SSM_HEADS, 1)), _full((1, SSM_HEADS))],
        out_specs=[_rb(q, SSM_WIDTH), pl.BlockSpec((1, N_PAIRS, 128, 128), lambda i: (i, 0, 0, 0))],
        scratch_shapes=[pltpu.VMEM((N_PAIRS, 128, 128), F32)],
        name="ssd_fwd", compiler_params=_params(("arbitrary",)))(xs, bc, bc, dt, dt_t, alr, alc, dsk)


def _ssd_bwd(xs, bc, dt, dt_t, alr, alc, dsk, sprev, dy):
    lp = xs.shape[0]
    q = SSD_CHUNK
    nc = lp // q

    def body(x_ref, b_ref, c_ref, dt_ref, dtt_ref, alr_ref, alc_ref, dsk_ref, sp_ref, dy_ref,
             dx_ref, dbc_ref, ddt_ref, ddtt_ref, dalr_ref, dalc_ref, ddsk_ref, ds_s):
        @pl.when(pl.program_id(0) == 0)
        def _():
            ds_s[...] = jnp.zeros_like(ds_s)

        xp = [x_ref[:, 128 * p:128 * (p + 1)] for p in range(N_PAIRS)]
        bs = [b_ref[:, 0:128], b_ref[:, 128:256]]
        cs = [c_ref[:, 0:128], c_ref[:, 128:256]]
        st = [sp_ref[0, p] for p in range(N_PAIRS)]
        _, vjp = jax.vjp(_ssd_chunk, xp, bs, cs, dt_ref[...], dtt_ref[...], alr_ref[...], alc_ref[...], dsk_ref[...],
                         st)
        dys = [dy_ref[:, 128 * p:128 * (p + 1)] for p in range(N_PAIRS)]
        dxp, dbs, dcs, ddt, ddtt, dalr, dalc, ddsk, dst = vjp((dys, [ds_s[p] for p in range(N_PAIRS)]))
        for p in range(N_PAIRS):
            dx_ref[:, 128 * p:128 * (p + 1)] = dxp[p]
            ds_s[p] = dst[p]
        for g in range(2):
            dbc_ref[:, 128 * g:128 * (g + 1)] = dbs[g]
            dbc_ref[:, 256 + 128 * g:256 + 128 * (g + 1)] = dcs[g]
        ddt_ref[...] = ddt
        ddtt_ref[...] = ddtt
        _acc_add(dalr_ref, dalr)
        _acc_add(dalc_ref, dalc)
        _acc_add(ddsk_ref, ddsk)

    rev = lambda width, cb=0: pl.BlockSpec((q, width), lambda i: (nc - 1 - i, cb))
    return pl.pallas_call(
        body,
        out_shape=[jax.ShapeDtypeStruct((lp, SSM_WIDTH), F32), jax.ShapeDtypeStruct((lp, 512), F32),
                   jax.ShapeDtypeStruct((lp, SSM_HEADS), F32), jax.ShapeDtypeStruct((SSM_HEADS, lp), F32),
                   jax.ShapeDtypeStruct((1, SSM_HEADS), F32), jax.ShapeDtypeStruct((SSM_HEADS, 1), F32),
                   jax.ShapeDtypeStruct((1, SSM_HEADS), F32)],
        grid=(nc,),
        in_specs=[rev(SSM_WIDTH), rev(256, 0), rev(256, 1), rev(SSM_HEADS),
                  pl.BlockSpec((SSM_HEADS, q), lambda i: (0, nc - 1 - i)),
                  _full((1, SSM_HEADS)), _full((SSM_HEADS, 1)), _full((1, SSM_HEADS)),
                  pl.BlockSpec((1, N_PAIRS, 128, 128), lambda i: (nc - 1 - i, 0, 0, 0)), rev(SSM_WIDTH)],
        out_specs=[rev(SSM_WIDTH), rev(512), rev(SSM_HEADS), pl.BlockSpec((SSM_HEADS, q), lambda i: (0, nc - 1 - i)),
                   _full((1, SSM_HEADS)), _full((SSM_HEADS, 1)), _full((1, SSM_HEADS))],
        scratch_shapes=[pltpu.VMEM((N_PAIRS, 128, 128), F32)],
        name="ssd_bwd", compiler_params=_params(("arbitrary",)))(xs, bc, bc, dt, dt_t, alr, alc, dsk, sprev, dy)


def _q_to_slab_order(w):
    hd = QK_NOPE + QK_ROPE
    nope = [w[:, h * hd:h * hd + QK_NOPE] for h in range(ATT_HEADS)]
    rope = [w[:, h * hd + QK_NOPE:(h + 1) * hd] for h in range(ATT_HEADS)]
    return jnp.concatenate(nope + rope, axis=1)


def _q_from_slab_order(wp):
    base = ATT_HEADS * QK_NOPE
    parts = []
    for h in range(ATT_HEADS):
        parts += [wp[:, QK_NOPE * h:QK_NOPE * (h + 1)], wp[:, base + QK_ROPE * h:base + QK_ROPE * (h + 1)]]
    return jnp.concatenate(parts, axis=1)


_IN_CQ, _IN_CKV, _IN_KR, _IN_Z, _IN_XS, _IN_BC, _IN_DT = (0, 384), (384, 640), (640, 704), (704, 1728), (1728, 2752), \
    (2752, 3264), (3264, 3280)


def _pack_w_in(w):
    z = lambda n: jnp.zeros((w.shape[0], n), w.dtype)
    s = lambda r: w[:, r[0]:r[1]]
    return jnp.concatenate([s(_IN_CKV), s(_IN_KR), z(64), s(_IN_CQ), s(_IN_DT), z(112), z(128), s(_IN_Z), s(_IN_XS),
                            s(_IN_BC)], axis=1)


def _unpack_w_in(wp):
    s = lambda off, n: wp[:, off:off + n]
    return jnp.concatenate([s(SEG_CQ, 384), s(SEG_KV, 256), s(SEG_KR, 64), s(SEG_Z, 1024), s(SEG_XS, 1024),
                            s(SEG_BC, 512), s(SEG_DT, 16)], axis=1)


def _rope_tables(lp):
    inv_freq = ROPE_THETA ** (-jnp.arange(0, QK_ROPE, 2, dtype=F32) / QK_ROPE)
    ang = jnp.arange(lp, dtype=F32)[:, None] * inv_freq[None, :]
    cos, sin = jnp.cos(ang), jnp.sin(ang)
    return jnp.tile(cos, (1, 4)), jnp.concatenate([-sin, sin, -sin, sin], axis=1)


def _local_step(x, tgt, w):
    n_real = x.shape[0]
    l = N_META + n_real
    lp = -(-l // MM_BLOCK) * MM_BLOCK
    pad = lambda a: jnp.concatenate([a, jnp.zeros((lp - l, D_MODEL), F32)], axis=0)
    h0 = pad(jnp.concatenate([w["meta_tokens"], x], axis=0))
    tgt_p = pad(jnp.concatenate([jnp.zeros((N_META, D_MODEL), F32), tgt], axis=0))
    cos_t, sin_t = _rope_tables(lp)

    w_in_p = _pack_w_in(w["w_in"])
    w_q_p = _q_to_slab_order(w["w_q_up"])
    w_kv, w_out, w_up, w_down = w["w_kv_up"], w["w_out"], w["w_mlp_up"], w["w_mlp_down"]
    conv_w, conv_b = w["conv_w"], w["conv_b"]
    dt_bias_pad = jnp.concatenate([w["dt_bias"], jnp.zeros((1, 128 - SSM_HEADS), F32)], axis=1)
    alr, dsk = w["a_log"], w["d_skip"]
    alc = alr.reshape(SSM_HEADS, 1)

    n1 = _norm_in(h0, w["norm_mix_pre"])
    proj = _mm("proj", n1, w_in_p)
    cqn, ckvn, krr = _attn_prep(proj, w["q_a_norm"], w["kv_a_norm"], cos_t, sin_t)
    q = _mm("q_up", cqn, w_q_p)
    kv = _mm("kv_up", ckvn, w_kv)
    qs, ks, vs, vts = _qk_pack(q, kv, krr, cos_t, sin_t)
    att, lse2 = _attn_fwd(qs, ks, vts)
    xs_pre, xs_act = _conv_fwd("conv_xs_fwd", proj, SEG_XS, SSM_WIDTH, conv_w[:, :SSM_WIDTH], conv_b[:, :SSM_WIDTH])
    bc_pre, bc_act = _conv_fwd("conv_bc_fwd", proj, SEG_BC, 512, conv_w[:, SSM_WIDTH:], conv_b[:, SSM_WIDTH:])
    dt = _dt_fwd(proj, dt_bias_pad)[:, :SSM_HEADS]
    dt_t = dt.T
    y, sprev = _ssd_fwd(xs_act, bc_act, dt, dt_t, alr, alc, dsk)
    ssm = _gated_norm_fwd(y, proj, w["ssm_norm"])
    cat = jnp.concatenate([att.astype(BF16), ssm], axis=1)
    mix = _mm("out_proj", cat, w_out)
    h1, n2 = _mix_residual(h0, mix, w["norm_mix_post"], w["norm_mlp_pre"])
    relu2 = lambda r: jnp.square(jnp.maximum(r, 0.0))
    u, act = _mm("mlp_up", n2, w_up, outs=((F32, None), (BF16, relu2)))
    f = _mm("mlp_down", act, w_down)
    loss, dh2, df, dg_mlp_post = _loss_and_grad(h1, f, w["norm_mlp_post"], tgt_p, n_real)

    g = {"norm_mlp_post": dg_mlp_post}
    g["w_mlp_down"] = _mm_tn("d_w_mlp_down", act, df)
    du = _mm("d_mlp_act", df, w_down, outs=((BF16, lambda r, ub: r * (2.0 * jnp.maximum(ub, 0.0))),), epi_ins=(u,),
             trans_b=True)
    g["w_mlp_up"] = _mm_tn("d_w_mlp_up", n2, du)
    dn2 = _mm("d_n2", du, w_up, trans_b=True)
    dh1, dmix, g["norm_mlp_pre"], g["norm_mix_post"] = _mlp_residual_bwd(dh2, dn2, h1, w["norm_mlp_pre"], mix,
                                                                         w["norm_mix_post"])
    g["w_out"] = _mm_tn("d_w_out", cat, dmix)
    dcat, dcat16 = _mm("d_cat", dmix, w_out, outs=((F32, None), (BF16, None)), trans_b=True)
    dy, dz, g["ssm_norm"] = _gated_norm_bwd(y, proj, w["ssm_norm"], dcat)
    dxs_act, dbc_act, ddt, ddt_t, dalr, dalc, g["d_skip"] = _ssd_bwd(xs_act, bc_act, dt, dt_t, alr, alc, dsk, sprev, dy)
    g["a_log"] = dalr + dalc.reshape(1, SSM_HEADS)
    dxs, dcw_xs, dcb_xs = _conv_bwd("conv_xs_bwd", dxs_act, xs_pre, proj, SEG_XS, SSM_WIDTH, conv_w[:, :SSM_WIDTH])
    dbc, dcw_bc, dcb_bc = _conv_bwd("conv_bc_bwd", dbc_act, bc_pre, proj, SEG_BC, 512, conv_w[:, SSM_WIDTH:])
    g["conv_w"] = jnp.concatenate([dcw_xs[:CONV_K], dcw_bc[:CONV_K]], axis=1)
    g["conv_b"] = jnp.concatenate([dcb_xs, dcb_bc], axis=1)
    ddt_pad = jnp.concatenate([ddt + ddt_t.T, jnp.zeros((lp, 128 - SSM_HEADS), F32)], axis=1)

    dqs, dks, dvs = _attn_bwd(qs, ks, vs, dcat16, lse2, _attn_delta(dcat, att))
    dq, dkv, dkr = _qk_unpack_bwd(dqs, dks, dvs, cos_t, sin_t)
    g["w_q_up"] = _q_from_slab_order(_mm_tn("d_w_q_up", cqn, dq))
    g["w_kv_up"] = _mm_tn("d_w_kv_up", ckvn, dkv)
    dcqn = _mm("d_cqn", dq, w_q_p, trans_b=True)
    dckvn = _mm("d_ckvn", dkv, w_kv, trans_b=True)
    dproj, g["q_a_norm"], g["kv_a_norm"], ddtb = _proj_grad(proj, dcqn, dckvn, w["q_a_norm"], w["kv_a_norm"], dkr,
                                                          ddt_pad, dt_bias_pad, dz, dxs, dbc)
    g["dt_bias"] = ddtb[:, :SSM_HEADS]
    g["w_in"] = _unpack_w_in(_mm_tn("d_w_in", n1, dproj))
    dn1 = _mm("d_n1", dproj, w_in_p, trans_b=True)
    dh0, g["norm_mix_pre"] = _input_norm_bwd(dh1, dn1, h0, w["norm_mix_pre"])
    g["meta_tokens"] = dh0[:N_META]
    return loss, dh0, g


WEIGHTS = ["meta_tokens", "norm_mix_pre", "w_in", "q_a_norm", "w_q_up", "kv_a_norm", "w_kv_up", "conv_w", "conv_b",
           "dt_bias", "a_log", "d_skip", "ssm_norm", "w_out", "norm_mix_post", "norm_mlp_pre", "w_mlp_up",
           "w_mlp_down", "norm_mlp_post"]
SHARD_AXIS = {"meta_tokens": 1, "w_in": 1, "w_q_up": 1, "w_kv_up": 1, "conv_w": 1, "w_out": 0, "w_mlp_up": 1,
              "w_mlp_down": 0}
FULL_SHAPE = {"meta_tokens": (16, 1024), "norm_mix_pre": (1, 1024), "w_in": (1024, 3280), "q_a_norm": (1, 384),
              "w_q_up": (384, 1536), "kv_a_norm": (1, 256), "w_kv_up": (256, 2048), "conv_w": (4, 1536),
              "conv_b": (1, 1536), "dt_bias": (1, 16), "a_log": (1, 16), "d_skip": (1, 16), "ssm_norm": (1, 1024),
              "w_out": (2048, 1024), "norm_mix_post": (1, 1024), "norm_mlp_pre": (1, 1024), "w_mlp_up": (1024, 4096),
              "w_mlp_down": (4096, 1024), "norm_mlp_post": (1, 1024)}
GATHER_BF16 = ["w_in", "w_q_up", "w_kv_up", "w_out", "w_mlp_up", "w_mlp_down"]
GATHER_F32 = ["meta_tokens", "conv_w"]


def _shard_shape(name):
    shp = list(FULL_SHAPE[name])
    if name in SHARD_AXIS:
        shp[SHARD_AXIS[name]] //= N_CHIPS
    return tuple(shp)


PACK_ORDER = sorted(WEIGHTS, key=lambda n: -_shard_shape(n)[0])


def _packed_rows(shape):
    r, c = shape
    return r if c <= PACK_W else -(-c // PACK_W)


def _pack_rows(arrays, row_multiple):
    parts = []
    for a in arrays:
        r, c = a.shape
        if c > PACK_W:
            assert r == 1, a.shape
            folded = _packed_rows(a.shape)
            a = jnp.pad(a, ((0, 0), (0, folded * PACK_W - c))).reshape(folded, PACK_W)
        elif c < PACK_W:
            a = jnp.pad(a, ((0, 0), (0, PACK_W - c)))
        parts.append(a)
    rows = sum(p.shape[0] for p in parts)
    if rows % row_multiple:
        parts.append(jnp.zeros((row_multiple - rows % row_multiple, PACK_W), parts[0].dtype))
    return jnp.concatenate(parts, axis=0)


def _unpack_rows(packed, shapes):
    out, off = [], 0
    for r, c in shapes:
        nr = _packed_rows((r, c))
        blk = packed[off:off + nr]
        out.append(blk[:, :c] if c <= PACK_W else blk.reshape(1, nr * PACK_W)[:, :c])
        off += nr
    return out


def _chip_slice(full, name, t):
    if name not in SHARD_AXIS:
        return full
    ax = SHARD_AXIS[name]
    n = FULL_SHAPE[name][ax] // N_CHIPS
    return lax.slice_in_dim(full, t * n, (t + 1) * n, axis=ax)


HBM_SPEC = pl.BlockSpec(memory_space=pl.ANY)
CHIP_FLIPS = ((1, 0), (0, 1), (1, 1))


def _gather_chips(bufs):
    nb = len(bufs)

    def body(*refs):
        ins, outs = refs[:nb], refs[nb:2 * nb]
        send, recv, loc = refs[2 * nb:]
        x, y, c = lax.axis_index("x"), lax.axis_index("y"), lax.axis_index("c")
        me = 2 * x + y
        copies = []
        for b in range(nb):
            own = pltpu.make_async_copy(ins[b], outs[b].at[me], loc.at[b])
            own.start()
            copies.append(own)
            for k, (fx, fy) in enumerate(CHIP_FLIPS):
                cp = pltpu.make_async_remote_copy(
                    src_ref=ins[b], dst_ref=outs[b].at[me], send_sem=send.at[b, k], recv_sem=recv.at[b, k],
                    device_id=(x ^ fx, y ^ fy, c), device_id_type=MESH_ID)
                cp.start()
                copies.append(cp)
        for cp in copies:
            cp.wait()

    return pl.pallas_call(
        body,
        out_shape=[jax.ShapeDtypeStruct((N_CHIPS,) + b.shape, b.dtype) for b in bufs],
        in_specs=[HBM_SPEC] * nb, out_specs=[HBM_SPEC] * nb,
        scratch_shapes=[pltpu.SemaphoreType.DMA((nb, 3)), pltpu.SemaphoreType.DMA((nb, 3)),
                        pltpu.SemaphoreType.DMA((nb,))],
        name="gather_chips")(*bufs)


def _sibling_swap(name, buf):
    def body(src, dst, send, recv):
        x, y, c = lax.axis_index("x"), lax.axis_index("y"), lax.axis_index("c")
        cp = pltpu.make_async_remote_copy(src_ref=src, dst_ref=dst, send_sem=send, recv_sem=recv,
                                          device_id=(x, y, 1 - c), device_id_type=MESH_ID)
        cp.start()
        cp.wait()

    return pl.pallas_call(
        body, out_shape=jax.ShapeDtypeStruct(buf.shape, buf.dtype), in_specs=[HBM_SPEC], out_specs=HBM_SPEC,
        scratch_shapes=[pltpu.SemaphoreType.DMA, pltpu.SemaphoreType.DMA], name=name)(buf)


def _scatter_chips(part):
    def body(src, dst, send, recv):
        x, y, c = lax.axis_index("x"), lax.axis_index("y"), lax.axis_index("c")
        copies = []
        for k, (fx, fy) in enumerate(CHIP_FLIPS):
            tx, ty = x ^ fx, y ^ fy
            cp = pltpu.make_async_remote_copy(
                src_ref=src.at[2 * tx + ty], dst_ref=dst.at[k], send_sem=send.at[k], recv_sem=recv.at[k],
                device_id=(tx, ty, c), device_id_type=MESH_ID)
            cp.start()
            copies.append(cp)
        for cp in copies:
            cp.wait()

    return pl.pallas_call(
        body, out_shape=jax.ShapeDtypeStruct((3,) + part.shape[1:], part.dtype), in_specs=[HBM_SPEC],
        out_specs=HBM_SPEC, scratch_shapes=[pltpu.SemaphoreType.DMA((3,)), pltpu.SemaphoreType.DMA((3,))],
        name="scatter_chips")(part)


def _add_rows(name, terms):
    rows = terms[0].shape[0]
    t = _row_tile(rows)

    def body(*refs):
        acc = refs[0][...]
        for r in refs[1:-1]:
            acc = acc + r[...]
        refs[-1][...] = acc

    return pl.pallas_call(
        body, out_shape=jax.ShapeDtypeStruct(terms[0].shape, F32), grid=(rows // t,),
        in_specs=[_rb(t, PACK_W)] * len(terms), out_specs=_rb(t, PACK_W),
        name=name, compiler_params=_params(("parallel",)))(*terms)


def _row_tile(rows):
    best = 8
    for t in range(8, 513, 8):
        if rows % t == 0:
            best = t
    return best


def _adamw(g, w, m, v):
    rows = g.shape[0]
    t = _row_tile(rows)
    c1 = 1.0 - ADAM_B1 ** ADAM_STEP
    c2 = 1.0 - ADAM_B2 ** ADAM_STEP

    def body(g_ref, w_ref, m_ref, v_ref, d_ref, mo_ref, vo_ref):
        gg = g_ref[...]
        mn = ADAM_B1 * m_ref[...] + (1.0 - ADAM_B1) * gg
        vn = ADAM_B2 * v_ref[...] + (1.0 - ADAM_B2) * (gg * gg)
        d_ref[...] = -ADAM_LR * ((mn / c1) / (jnp.sqrt(vn / c2) + ADAM_EPS) + ADAM_WD * w_ref[...])
        mo_ref[...] = mn
        vo_ref[...] = vn

    return pl.pallas_call(
        body, out_shape=[jax.ShapeDtypeStruct(g.shape, F32)] * 3, grid=(rows // t,),
        in_specs=[_rb(t, PACK_W)] * 4, out_specs=[_rb(t, PACK_W)] * 3,
        name="adamw", compiler_params=_params(("parallel",)))(g, w, m, v)


def kernel(x, meta_tokens, norm_mix_pre, w_in, q_a_norm, w_q_up, kv_a_norm, w_kv_up, conv_w, conv_b, dt_bias, a_log, d_skip, ssm_norm, w_out, norm_mix_post, norm_mlp_pre, w_mlp_up, w_mlp_down, norm_mlp_post, loss_target, m_meta_tokens, m_norm_mix_pre, m_w_in, m_q_a_norm, m_w_q_up, m_kv_a_norm, m_w_kv_up, m_conv_w, m_conv_b, m_dt_bias, m_a_log, m_d_skip, m_ssm_norm, m_w_out, m_norm_mix_post, m_norm_mlp_pre, m_w_mlp_up, m_w_mlp_down, m_norm_mlp_post, v_meta_tokens, v_norm_mix_pre, v_w_in, v_q_a_norm, v_w_q_up, v_kv_a_norm, v_w_kv_up, v_conv_w, v_conv_b, v_dt_bias, v_a_log, v_d_skip, v_ssm_norm, v_w_out, v_norm_mix_post, v_norm_mlp_pre, v_w_mlp_up, v_w_mlp_down, v_norm_mlp_post):
    given = dict(locals())
    drop = lambda name, a: a[0] if a.ndim == 3 else a
    w_loc = {n: drop(n, given[n]) for n in WEIGHTS}
    m_loc = {n: drop(n, given["m_" + n]) for n in WEIGHTS}
    v_loc = {n: drop(n, given["v_" + n]) for n in WEIGHTS}
    ix, iy, ic = lax.axis_index("x"), lax.axis_index("y"), lax.axis_index("c")
    me = 2 * ix + iy

    sent16 = _pack_rows([w_loc[n].astype(BF16) for n in GATHER_BF16], 16)
    sent32 = _pack_rows([w_loc[n] for n in GATHER_F32], 8)
    got16, got32 = _gather_chips([sent16, sent32])
    w_full = {n: w_loc[n] for n in WEIGHTS if n not in SHARD_AXIS}
    for names, got in ((GATHER_BF16, got16), (GATHER_F32, got32)):
        per_chip = [_unpack_rows(got[t], [_shard_shape(n) for n in names]) for t in range(N_CHIPS)]
        for k, n in enumerate(names):
            w_full[n] = jnp.concatenate([per_chip[t][k] for t in range(N_CHIPS)], axis=SHARD_AXIS[n])

    loss, dh0, g_full = _local_step(x[0], loss_target[0], w_full)
    n_real = x.shape[1]
    grad_x = dh0[N_META:N_META + n_real][None]

    shapes = [_shard_shape(n) for n in PACK_ORDER]
    slots = [_pack_rows([_chip_slice(g_full[n], n, t) for n in PACK_ORDER], 16) for t in range(N_CHIPS)]
    rows = slots[0].shape[0]
    half = rows // 2
    halves = lambda hh: jnp.concatenate([lax.dynamic_slice_in_dim(s, hh * half, half, axis=0) for s in slots], axis=0)
    keep, give = halves(ic), halves(1 - ic)
    from_sibling = _sibling_swap("sibling_swap", give)
    chip_part = _add_rows("chip_partial", [keep, from_sibling]).reshape(N_CHIPS, half, PACK_W)
    from_chips = _scatter_chips(chip_part)
    own = lax.dynamic_index_in_dim(chip_part, me, axis=0, keepdims=False)
    rel_of_xor = {2: 0, 1: 1, 3: 2}
    terms = []
    for t in range(N_CHIPS):
        cand = own
        for xr, k in rel_of_xor.items():
            cand = jnp.where((me ^ t) == xr, from_chips[k], cand)
        terms.append(cand)
    my_half = _add_rows("chip_total", terms)
    other_half = _sibling_swap("sibling_gather", my_half)
    g_red = jnp.where(ic == 0, jnp.concatenate([my_half, other_half], axis=0),
                      jnp.concatenate([other_half, my_half], axis=0))

    pack_loc = lambda d: _pack_rows([d[n] for n in PACK_ORDER], 16)
    delta, new_m, new_v = _adamw(g_red, pack_loc(w_loc), pack_loc(m_loc), pack_loc(v_loc))

    def outputs(packed_arr):
        parts = dict(zip(PACK_ORDER, _unpack_rows(packed_arr, shapes)))
        return [parts[n][None] if given[n].ndim == 3 else parts[n] for n in WEIGHTS]

    total = lax.psum(loss[0, 0], ("x", "y", "c"))
    return (total, grad_x, *outputs(g_red), *outputs(delta), *outputs(new_m), *outputs(new_v))
```

```python
import functools

import numpy as np
import jax
import jax.numpy as jnp
from jax import lax
from jax.experimental import pallas as pl
from jax.experimental.pallas import tpu as pltpu

F32 = jnp.float32
BF16 = jnp.bfloat16

D_MODEL = 1024
N_META = 16
EPS = 1e-6
ATT_HEADS = 8
Q_LORA = 384
KV_LORA = 256
QK_NOPE = 128
QK_ROPE = 64
V_HEAD = 128
ROPE_THETA = 10000.0
SSM_HEADS = 16
SSM_HEAD_DIM = 64
SSM_WIDTH = 1024
SSM_STATE = 128
CONV_K = 4
D_FF = 4096
ATT_SCALE = float((QK_NOPE + QK_ROPE) ** -0.5)
ATT_SCALE_LOG2 = float(ATT_SCALE * np.log2(np.e))

ADAM_LR = 0.001
ADAM_B1 = 0.9
ADAM_B2 = 0.999
ADAM_EPS = 1e-08
ADAM_WD = 0.01
ADAM_STEP = 10

SEG_KV, SEG_KR, SEG_CQ, SEG_DT, SEG_Z, SEG_XS, SEG_BC = 0, 256, 384, 768, 1024, 2048, 3072
PROJ_W = 3584
QP_W = 256

ROW_BLOCK = 256
MM_BLOCK = 512
SSD_CHUNK = 128
ATT_SPLIT = 1
ATT_BLOCK = 768
ATT_BLOCK_FWD = 1536
VMEM_LIMIT = 56 * 1024 * 1024
NEG_BIG = -1e30

PACK_W = 1024
N_CHIPS = 4
MESH_ID = pl.DeviceIdType.MESH


def _params(sem):
    return pltpu.CompilerParams(dimension_semantics=sem, vmem_limit_bytes=VMEM_LIMIT)


def _rb(rows, width, cb=0):
    return pl.BlockSpec((rows, width), lambda i: (i, cb))


def _full(shape):
    zeros = (0,) * len(shape)
    return pl.BlockSpec(shape, lambda i: zeros)


def _acc_add(ref, val):
    first = pl.program_id(0) == 0

    @pl.when(first)
    def _():
        ref[...] = val

    @pl.when(jnp.logical_not(first))
    def _():
        ref[...] += val


def _rms(x, g):
    r = lax.rsqrt(jnp.mean(x * x, axis=-1, keepdims=True) + EPS)
    return x * r * g


def _rms_bwd(x, g, dy):
    r = lax.rsqrt(jnp.mean(x * x, axis=-1, keepdims=True) + EPS)
    dyg = dy * g
    dx = r * dyg - x * (r * r * r) * jnp.mean(x * dyg, axis=-1, keepdims=True)
    dg = jnp.sum(dy * x * r, axis=0, keepdims=True)
    return dx, dg


def _sigmoid(x):
    return 1.0 / (1.0 + jnp.exp(-x))


def _swap32(x):
    lane = lax.broadcasted_iota(jnp.int32, x.shape, 1)
    return jnp.where((lane % 64) < 32, pltpu.roll(x, 96, 1), pltpu.roll(x, 32, 1))


def _rope(x, cos_t, sin_t):
    return x * cos_t + _swap32(x) * sin_t


def _rope_bwd(dr, cos_t, sin_t):
    return dr * cos_t + _swap32(dr * sin_t)


def _tile(n, cap):
    if n <= cap:
        return n
    best = 128
    for t in range(128, cap + 1, 128):
        if n % t == 0:
            best = t
    assert n % best == 0, (n, cap)
    return best


MM_VMEM_BUDGET = 40 * 1024 * 1024


def _mm(name, a, b, outs=((F32, None),), epi_ins=(), trans_b=False):
    m, k = a.shape
    n = b.shape[0] if trans_b else b.shape[1]
    tm = MM_BLOCK
    n_epi = len(epi_ins)
    out_bytes = sum(jnp.dtype(dt).itemsize for dt, _ in outs) + sum(e.dtype.itemsize for e in epi_ins)
    step_bytes = lambda tn: 2 * (tm * k * a.dtype.itemsize + k * tn * b.dtype.itemsize + tm * tn * out_bytes)
    tn = n
    while step_bytes(tn) > MM_VMEM_BUDGET and tn % 256 == 0:
        tn //= 2
    assert n % tn == 0 and step_bytes(tn) <= MM_VMEM_BUDGET, (name, n, tn)

    def body(a_ref, b_ref, *rest):
        epi_refs = rest[:n_epi]
        out_refs = rest[n_epi:]
        lhs, rhs = a_ref[...].astype(BF16), b_ref[...].astype(BF16)
        r = _nt(lhs, rhs) if trans_b else jnp.dot(lhs, rhs, preferred_element_type=F32)
        blocks = [e[...] for e in epi_refs]
        for o_ref, (dt, fn) in zip(out_refs, outs):
            o_ref[...] = (r if fn is None else fn(r, *blocks)).astype(dt)

    out_spec = pl.BlockSpec((tm, tn), lambda j, i: (i, j))
    b_spec = pl.BlockSpec((tn, k), lambda j, i: (j, 0)) if trans_b else pl.BlockSpec((k, tn), lambda j, i: (0, j))
    res = pl.pallas_call(
        body,
        out_shape=[jax.ShapeDtypeStruct((m, n), dt) for dt, _ in outs],
        grid=(n // tn, m // tm),
        in_specs=[pl.BlockSpec((tm, k), lambda j, i: (i, 0)), b_spec] + [out_spec] * n_epi,
        out_specs=[out_spec] * len(outs),
        name=name,
        compiler_params=_params(("parallel", "parallel")),
    )(a, b, *epi_ins)
    return res[0] if len(outs) == 1 else res


def _mm_tn(name, x, dy, ta_cap=1024, tn_cap=1024):
    l, a = x.shape
    n = dy.shape[1]
    ta, tn, tl = _tile(a, ta_cap), _tile(n, tn_cap), MM_BLOCK
    nl = l // tl

    def body(x_ref, dy_ref, o_ref):
        ll = pl.program_id(2)

        @pl.when(ll == 0)
        def _():
            o_ref[...] = jnp.zeros_like(o_ref)

        o_ref[...] += lax.dot_general(x_ref[...].astype(BF16), dy_ref[...].astype(BF16), (((0,), (0,)), ((), ())),
                                      preferred_element_type=F32)

    return pl.pallas_call(
        body,
        out_shape=jax.ShapeDtypeStruct((a, n), F32),
        grid=(a // ta, n // tn, nl),
        in_specs=[pl.BlockSpec((tl, ta), lambda i, j, ll: (ll, i)), pl.BlockSpec((tl, tn), lambda i, j, ll: (ll, j))],
        out_specs=pl.BlockSpec((ta, tn), lambda i, j, ll: (i, j)),
        name=name,
        compiler_params=_params(("parallel", "parallel", "arbitrary")),
    )(x, dy)


def _norm_in(h0, g_pre):
    lp = h0.shape[0]
    t = ROW_BLOCK

    def body(h_ref, g_ref, o_ref):
        o_ref[...] = _rms(h_ref[...], g_ref[...]).astype(BF16)

    return pl.pallas_call(
        body, out_shape=jax.ShapeDtypeStruct((lp, D_MODEL), BF16), grid=(lp // t,),
        in_specs=[_rb(t, D_MODEL), _full((1, D_MODEL))], out_specs=_rb(t, D_MODEL),
        name="norm_in", compiler_params=_params(("parallel",)))(h0, g_pre)


def _attn_prep(proj, g_q, g_kv, cos_t, sin_t):
    lp = proj.shape[0]
    t = ROW_BLOCK

    def body(ckv_ref, kr_ref, cq_ref, gq_ref, gkv_ref, cos_ref, sin_ref, cqn_ref, ckvn_ref, krr_ref):
        cqn_ref[...] = _rms(cq_ref[...], gq_ref[...]).astype(BF16)
        ckvn_ref[...] = _rms(ckv_ref[...], gkv_ref[...]).astype(BF16)
        roped = _rope(kr_ref[...], cos_ref[...], sin_ref[...])
        krr_ref[...] = roped + pltpu.roll(roped, 64, 1)

    return pl.pallas_call(
        body,
        out_shape=[jax.ShapeDtypeStruct((lp, Q_LORA), BF16), jax.ShapeDtypeStruct((lp, KV_LORA), BF16),
                   jax.ShapeDtypeStruct((lp, 128), F32)],
        grid=(lp // t,),
        in_specs=[_rb(t, KV_LORA, SEG_KV // KV_LORA), _rb(t, 128, SEG_KR // 128), _rb(t, Q_LORA, SEG_CQ // Q_LORA),
                  _full((1, Q_LORA)), _full((1, KV_LORA)), _rb(t, 128), _rb(t, 128)],
        out_specs=[_rb(t, Q_LORA), _rb(t, KV_LORA), _rb(t, 128)],
        name="attn_prep", compiler_params=_params(("parallel",)))(proj, proj, proj, g_q, g_kv, cos_t, sin_t)


def _qk_pack(q, kv, krr, cos_t, sin_t):
    lp = q.shape[0]
    t = ROW_BLOCK

    def body(q_ref, kv_ref, krr_ref, cos_ref, sin_ref, qs_ref, ks_ref, vs_ref, vts_ref):
        lane = lax.broadcasted_iota(jnp.int32, (t, 128), 1)
        lo = lane < 64
        krr = krr_ref[...].astype(BF16)
        for j in range(ATT_HEADS // 2):
            pr = _rope(q_ref[:, 1024 + 128 * j:1024 + 128 * (j + 1)], cos_ref[...], sin_ref[...])
            for h, keep in ((2 * j, lo), (2 * j + 1, jnp.logical_not(lo))):
                qs_ref[h, :, 0:128] = q_ref[:, 128 * h:128 * (h + 1)].astype(BF16)
                qs_ref[h, :, 128:256] = jnp.where(keep, pr, 0.0).astype(BF16)
        for h in range(ATT_HEADS):
            ks_ref[h, :, 0:128] = kv_ref[:, 256 * h:256 * h + 128].astype(BF16)
            ks_ref[h, :, 128:256] = krr
            v = kv_ref[:, 256 * h + 128:256 * (h + 1)]
            vs_ref[h] = v.astype(BF16)
            vts_ref[h] = v.T.astype(BF16)

    slab = lambda w: pl.BlockSpec((ATT_HEADS, t, w), lambda i: (0, i, 0))
    return pl.pallas_call(
        body,
        out_shape=[jax.ShapeDtypeStruct((ATT_HEADS, lp, QP_W), BF16), jax.ShapeDtypeStruct((ATT_HEADS, lp, QP_W), BF16),
                   jax.ShapeDtypeStruct((ATT_HEADS, lp, V_HEAD), BF16), jax.ShapeDtypeStruct((ATT_HEADS, V_HEAD, lp), BF16)],
        grid=(lp // t,),
        in_specs=[_rb(t, 1536), _rb(t, 2048), _rb(t, 128), _rb(t, 128), _rb(t, 128)],
        out_specs=[slab(QP_W), slab(QP_W), slab(V_HEAD), pl.BlockSpec((ATT_HEADS, V_HEAD, t), lambda i: (0, 0, i))],
        name="qk_pack", compiler_params=_params(("parallel",)))(q, kv, krr, cos_t, sin_t)


def _shifted(ext, t, shift):
    if shift == 0:
        return ext[8:, :]
    return pltpu.roll(ext, shift, 0)[8:, :]


def _conv_fwd(name, proj, seg, width, conv_w, conv_b):
    lp = proj.shape[0]
    t = ROW_BLOCK
    cb = seg // width

    def body(u_ref, halo_ref, w_ref, b_ref, pre_ref, act_ref):
        i = pl.program_id(0)
        u = u_ref[...]
        halo = jnp.where(i > 0, halo_ref[...], 0.0)
        ext = jnp.concatenate([halo, u], axis=0)
        pre = jnp.broadcast_to(b_ref[...], (t, width))
        for k in range(CONV_K):
            pre = pre + w_ref[k:k + 1, :] * _shifted(ext, t, CONV_K - 1 - k)
        pre_ref[...] = pre
        act_ref[...] = pre * _sigmoid(pre)

    return pl.pallas_call(
        body,
        out_shape=[jax.ShapeDtypeStruct((lp, width), F32)] * 2,
        grid=(lp // t,),
        in_specs=[_rb(t, width, cb),
                  pl.BlockSpec((8, width), lambda i: (jnp.maximum(i * (t // 8) - 1, 0), cb)),
                  _full((CONV_K, width)), _full((1, width))],
        out_specs=[_rb(t, width), _rb(t, width)],
        name=name, compiler_params=_params(("parallel",)))(proj, proj, conv_w, conv_b)


def _softplus(x):
    return jnp.maximum(x, 0.0) + jnp.log1p(jnp.exp(-jnp.abs(x)))


def _dt_fwd(proj, dt_bias_pad):
    lp = proj.shape[0]
    t = ROW_BLOCK

    def body(x_ref, b_ref, o_ref):
        o_ref[...] = _softplus(x_ref[...] + b_ref[...])

    return pl.pallas_call(
        body, out_shape=jax.ShapeDtypeStruct((lp, 128), F32), grid=(lp // t,),
        in_specs=[_rb(t, 128, SEG_DT // 128), _full((1, 128))], out_specs=_rb(t, 128),
        name="dt_fwd", compiler_params=_params(("parallel",)))(proj, dt_bias_pad)


def _gated_norm_group(y, z, w):
    g = y * (z * _sigmoid(z))
    return g * lax.rsqrt(jnp.mean(g * g, axis=-1, keepdims=True) + EPS) * w


def _gated_norm_fwd(y, proj, w):
    lp = y.shape[0]
    t = ROW_BLOCK
    gw = SSM_WIDTH // 2

    def body(y0, y1, z0, z1, w0, w1, o_ref):
        o_ref[:, 0:gw] = _gated_norm_group(y0[...], z0[...], w0[...]).astype(BF16)
        o_ref[:, gw:] = _gated_norm_group(y1[...], z1[...], w1[...]).astype(BF16)

    zb = SEG_Z // gw
    return pl.pallas_call(
        body, out_shape=jax.ShapeDtypeStruct((lp, SSM_WIDTH), BF16), grid=(lp // t,),
        in_specs=[_rb(t, gw, 0), _rb(t, gw, 1), _rb(t, gw, zb), _rb(t, gw, zb + 1),
                  pl.BlockSpec((1, gw), lambda i: (0, 0)), pl.BlockSpec((1, gw), lambda i: (0, 1))],
        out_specs=_rb(t, SSM_WIDTH),
        name="gated_norm_fwd", compiler_params=_params(("parallel",)))(y, y, proj, proj, w, w)


def _gated_norm_bwd(y, proj, w, dssm):
    lp = y.shape[0]
    t = ROW_BLOCK
    gw = SSM_WIDTH // 2

    def body(y0, y1, z0, z1, w0, w1, d0, d1, dy_ref, dz_ref, dw_ref):
        dws = []
        for g, (yr, zr, wr, dr) in enumerate(((y0, z0, w0, d0), (y1, z1, w1, d1))):
            _, vjp = jax.vjp(_gated_norm_group, yr[...], zr[...], wr[...])
            dyg, dzg, dwg = vjp(dr[...])
            dy_ref[:, g * gw:(g + 1) * gw] = dyg
            dz_ref[:, g * gw:(g + 1) * gw] = dzg
            dws.append(dwg)
        _acc_add(dw_ref, jnp.concatenate(dws, axis=1))

    zb = SEG_Z // gw
    return pl.pallas_call(
        body,
        out_shape=[jax.ShapeDtypeStruct((lp, SSM_WIDTH), F32), jax.ShapeDtypeStruct((lp, SSM_WIDTH), F32),
                   jax.ShapeDtypeStruct((1, SSM_WIDTH), F32)],
        grid=(lp // t,),
        in_specs=[_rb(t, gw, 0), _rb(t, gw, 1), _rb(t, gw, zb), _rb(t, gw, zb + 1),
                  pl.BlockSpec((1, gw), lambda i: (0, 0)), pl.BlockSpec((1, gw), lambda i: (0, 1)),
                  _rb(t, gw, 2), _rb(t, gw, 3)],
        out_specs=[_rb(t, SSM_WIDTH), _rb(t, SSM_WIDTH), _full((1, SSM_WIDTH))],
        name="gated_norm_bwd", compiler_params=_params(("arbitrary",)))(y, y, proj, proj, w, w, dssm, dssm)


def _mix_residual(h0, mix, g_post, g_mlp_pre):
    lp = h0.shape[0]
    t = ROW_BLOCK

    def body(h_ref, m_ref, gp_ref, gm_ref, h1_ref, n2_ref):
        h1 = h_ref[...] + _rms(m_ref[...], gp_ref[...])
        h1_ref[...] = h1
        n2_ref[...] = _rms(h1, gm_ref[...]).astype(BF16)

    return pl.pallas_call(
        body, out_shape=[jax.ShapeDtypeStruct((lp, D_MODEL), F32), jax.ShapeDtypeStruct((lp, D_MODEL), BF16)],
        grid=(lp // t,),
        in_specs=[_rb(t, D_MODEL), _rb(t, D_MODEL), _full((1, D_MODEL)), _full((1, D_MODEL))],
        out_specs=[_rb(t, D_MODEL), _rb(t, D_MODEL)],
        name="mix_residual", compiler_params=_params(("parallel",)))(h0, mix, g_post, g_mlp_pre)


def _loss_and_grad(h1, f, g_post, tgt, n_real):
    lp = h1.shape[0]
    t = ROW_BLOCK

    def body(h1_ref, f_ref, g_ref, t_ref, loss_ref, dh2_ref, df_ref, dg_ref):
        i = pl.program_id(0)
        fx = f_ref[...]
        h2 = h1_ref[...] + _rms(fx, g_ref[...])
        row = i * t + lax.broadcasted_iota(jnp.int32, (t, 1), 0)
        real = jnp.logical_and(row >= N_META, row < N_META + n_real)
        diff = jnp.where(real, h2 - t_ref[...], 0.0)
        part = 0.5 * jnp.sum(jnp.sum(diff * diff, axis=-1, keepdims=True) / D_MODEL, axis=0, keepdims=True)
        _acc_add(loss_ref, jnp.broadcast_to(part, (1, 128)))
        dh2 = diff / D_MODEL
        dh2_ref[...] = dh2
        dfx, dg = _rms_bwd(fx, g_ref[...], dh2)
        df_ref[...] = dfx.astype(BF16)
        _acc_add(dg_ref, dg)

    return pl.pallas_call(
        body,
        out_shape=[jax.ShapeDtypeStruct((1, 128), F32), jax.ShapeDtypeStruct((lp, D_MODEL), F32),
                   jax.ShapeDtypeStruct((lp, D_MODEL), BF16), jax.ShapeDtypeStruct((1, D_MODEL), F32)],
        grid=(lp // t,),
        in_specs=[_rb(t, D_MODEL), _rb(t, D_MODEL), _full((1, D_MODEL)), _rb(t, D_MODEL)],
        out_specs=[_full((1, 128)), _rb(t, D_MODEL), _rb(t, D_MODEL), _full((1, D_MODEL))],
        name="loss_and_grad", compiler_params=_params(("arbitrary",)))(h1, f, g_post, tgt)


def _mlp_residual_bwd(dh2, dn2, h1, g_mlp_pre, mix, g_post):
    lp = h1.shape[0]
    t = ROW_BLOCK

    def body(dh2_ref, dn2_ref, h1_ref, gm_ref, mix_ref, gp_ref, dh1_ref, dmix_ref, dgm_ref, dgp_ref):
        dx, dgm = _rms_bwd(h1_ref[...], gm_ref[...], dn2_ref[...])
        dh1 = dh2_ref[...] + dx
        dh1_ref[...] = dh1
        dmix, dgp = _rms_bwd(mix_ref[...], gp_ref[...], dh1)
        dmix_ref[...] = dmix.astype(BF16)
        _acc_add(dgm_ref, dgm)
        _acc_add(dgp_ref, dgp)

    return pl.pallas_call(
        body,
        out_shape=[jax.ShapeDtypeStruct((lp, D_MODEL), F32), jax.ShapeDtypeStruct((lp, D_MODEL), BF16),
                   jax.ShapeDtypeStruct((1, D_MODEL), F32), jax.ShapeDtypeStruct((1, D_MODEL), F32)],
        grid=(lp // t,),
        in_specs=[_rb(t, D_MODEL), _rb(t, D_MODEL), _rb(t, D_MODEL), _full((1, D_MODEL)), _rb(t, D_MODEL),
                  _full((1, D_MODEL))],
        out_specs=[_rb(t, D_MODEL), _rb(t, D_MODEL), _full((1, D_MODEL)), _full((1, D_MODEL))],
        name="mlp_residual_bwd", compiler_params=_params(("arbitrary",)))(dh2, dn2, h1, g_mlp_pre, mix, g_post)


def _input_norm_bwd(dh1, dn1, h0, g_pre):
    lp = h0.shape[0]
    t = ROW_BLOCK

    def body(dh1_ref, dn1_ref, h0_ref, g_ref, dh0_ref, dg_ref):
        dx, dg = _rms_bwd(h0_ref[...], g_ref[...], dn1_ref[...])
        dh0_ref[...] = dh1_ref[...] + dx
        _acc_add(dg_ref, dg)

    return pl.pallas_call(
        body, out_shape=[jax.ShapeDtypeStruct((lp, D_MODEL), F32), jax.ShapeDtypeStruct((1, D_MODEL), F32)],
        grid=(lp // t,),
        in_specs=[_rb(t, D_MODEL), _rb(t, D_MODEL), _rb(t, D_MODEL), _full((1, D_MODEL))],
        out_specs=[_rb(t, D_MODEL), _full((1, D_MODEL))],
        name="input_norm_bwd", compiler_params=_params(("arbitrary",)))(dh1, dn1, h0, g_pre)


def _conv_bwd(name, dact, pre, proj, seg, width, conv_w):
    lp = proj.shape[0]
    t = ROW_BLOCK
    cb = seg // width
    nblk = lp // t

    def dsilu(p):
        s = _sigmoid(p)
        return s * (1.0 + p * (1.0 - s))

    def body(da_ref, dan_ref, pre_ref, pren_ref, u_ref, halo_ref, w_ref, du_ref, dw_ref, db_ref):
        i = pl.program_id(0)
        dpre = da_ref[...] * dsilu(pre_ref[...])
        dpre_next = jnp.where(i < nblk - 1, dan_ref[...] * dsilu(pren_ref[...]), 0.0)
        extd = jnp.concatenate([dpre, dpre_next], axis=0)
        halo = jnp.where(i > 0, halo_ref[...], 0.0)
        ext = jnp.concatenate([halo, u_ref[...]], axis=0)
        du = jnp.zeros((t, width), F32)
        dws = []
        for k in range(CONV_K):
            m = CONV_K - 1 - k
            ahead = dpre if m == 0 else pltpu.roll(extd, t + 8 - m, 0)[:t, :]
            du = du + w_ref[k:k + 1, :] * ahead
            dws.append(jnp.sum(dpre * _shifted(ext, t, m), axis=0, keepdims=True))
        du_ref[...] = du
        _acc_add(dw_ref, jnp.concatenate(dws + [jnp.zeros((8 - CONV_K, width), F32)], axis=0))
        _acc_add(db_ref, jnp.sum(dpre, axis=0, keepdims=True))

    nxt = lambda i: (jnp.minimum((i + 1) * (t // 8), lp // 8 - 1), 0)
    return pl.pallas_call(
        body,
        out_shape=[jax.ShapeDtypeStruct((lp, width), F32), jax.ShapeDtypeStruct((8, width), F32),
                   jax.ShapeDtypeStruct((1, width), F32)],
        grid=(nblk,),
        in_specs=[_rb(t, width), pl.BlockSpec((8, width), nxt), _rb(t, width), pl.BlockSpec((8, width), nxt),
                  _rb(t, width, cb),
                  pl.BlockSpec((8, width), lambda i: (jnp.maximum(i * (t // 8) - 1, 0), cb)),
                  _full((CONV_K, width))],
        out_specs=[_rb(t, width), _full((8, width)), _full((1, width))],
        name=name, compiler_params=_params(("arbitrary",)))(dact, dact, pre, pre, proj, proj, conv_w)


def _qk_unpack_bwd(dqs, dks, dvs, cos_t, sin_t):
    lp = dqs.shape[1]
    t = ROW_BLOCK

    def body(dqs_ref, dks_ref, dvs_ref, cos_ref, sin_ref, dq_ref, dkv_ref, dkr_ref):
        lane = lax.broadcasted_iota(jnp.int32, (t, 128), 1)
        lo = lane < 64
        for j in range(ATT_HEADS // 2):
            dpr = jnp.where(lo, dqs_ref[2 * j, :, 128:256], dqs_ref[2 * j + 1, :, 128:256])
            dq_ref[:, 1024 + 128 * j:1024 + 128 * (j + 1)] = _rope_bwd(dpr, cos_ref[...], sin_ref[...]).astype(BF16)
        dkrr = jnp.zeros((t, 128), F32)
        for h in range(ATT_HEADS):
            dq_ref[:, 128 * h:128 * (h + 1)] = dqs_ref[h, :, 0:128].astype(BF16)
            dkv_ref[:, 256 * h:256 * h + 128] = dks_ref[h, :, 0:128].astype(BF16)
            dkv_ref[:, 256 * h + 128:256 * (h + 1)] = dvs_ref[h].astype(BF16)
            dkrr = dkrr + dks_ref[h, :, 128:256]
        droped = jnp.where(lo, dkrr + pltpu.roll(dkrr, 64, 1), 0.0)
        dkr_ref[...] = _rope_bwd(droped, cos_ref[...], sin_ref[...])

    slab = lambda w: pl.BlockSpec((ATT_HEADS, t, w), lambda i: (0, i, 0))
    return pl.pallas_call(
        body,
        out_shape=[jax.ShapeDtypeStruct((lp, 1536), BF16), jax.ShapeDtypeStruct((lp, 2048), BF16),
                   jax.ShapeDtypeStruct((lp, 128), F32)],
        grid=(lp // t,),
        in_specs=[slab(QP_W), slab(QP_W), slab(V_HEAD), _rb(t, 128), _rb(t, 128)],
        out_specs=[_rb(t, 1536), _rb(t, 2048), _rb(t, 128)],
        name="qk_unpack_bwd", compiler_params=_params(("parallel",)))(dqs, dks, dvs, cos_t, sin_t)


def _proj_grad(proj, dcqn, dckvn, g_q, g_kv, dkr, ddt_pad, dt_bias_pad, dz, dxs, dbc):
    lp = proj.shape[0]
    t = ROW_BLOCK

    def body(ckv_ref, cq_ref, pdt_ref, dcq_ref, dckv_ref, gq_ref, gkv_ref, dkr_ref, ddt_ref, b_ref, dz_ref, dxs_ref,
             dbc_ref, dp_ref, dgq_ref, dgkv_ref, db_ref):
        dckv, dgkv = _rms_bwd(ckv_ref[...], gkv_ref[...], dckv_ref[...])
        dcq, dgq = _rms_bwd(cq_ref[...], gq_ref[...], dcq_ref[...])
        ddt_raw = ddt_ref[...] * _sigmoid(pdt_ref[...] + b_ref[...])
        dp_ref[:, SEG_KV:SEG_KV + KV_LORA] = dckv.astype(BF16)
        dp_ref[:, SEG_KR:SEG_KR + 128] = dkr_ref[...].astype(BF16)
        dp_ref[:, SEG_CQ:SEG_CQ + Q_LORA] = dcq.astype(BF16)
        dp_ref[:, SEG_DT:SEG_DT + 128] = ddt_raw.astype(BF16)
        dp_ref[:, SEG_DT + 128:SEG_Z] = jnp.zeros((t, SEG_Z - SEG_DT - 128), BF16)
        dp_ref[:, SEG_Z:SEG_XS] = dz_ref[...].astype(BF16)
        dp_ref[:, SEG_XS:SEG_BC] = dxs_ref[...].astype(BF16)
        dp_ref[:, SEG_BC:PROJ_W] = dbc_ref[...].astype(BF16)
        _acc_add(dgq_ref, dgq)
        _acc_add(dgkv_ref, dgkv)
        _acc_add(db_ref, jnp.sum(ddt_raw, axis=0, keepdims=True))

    return pl.pallas_call(
        body,
        out_shape=[jax.ShapeDtypeStruct((lp, PROJ_W), BF16), jax.ShapeDtypeStruct((1, Q_LORA), F32),
                   jax.ShapeDtypeStruct((1, KV_LORA), F32), jax.ShapeDtypeStruct((1, 128), F32)],
        grid=(lp // t,),
        in_specs=[_rb(t, KV_LORA, SEG_KV // KV_LORA), _rb(t, Q_LORA, SEG_CQ // Q_LORA), _rb(t, 128, SEG_DT // 128),
                  _rb(t, Q_LORA), _rb(t, KV_LORA), _full((1, Q_LORA)), _full((1, KV_LORA)), _rb(t, 128), _rb(t, 128),
                  _full((1, 128)), _rb(t, SSM_WIDTH), _rb(t, SSM_WIDTH), _rb(t, 512)],
        out_specs=[_rb(t, PROJ_W), _full((1, Q_LORA)), _full((1, KV_LORA)), _full((1, 128))],
        name="proj_grad", compiler_params=_params(("arbitrary",)))(
            proj, proj, proj, dcqn, dckvn, g_q, g_kv, dkr, ddt_pad, dt_bias_pad, dz, dxs, dbc)


def _pair_tables(n):
    qmaj = [(i, j) for i in range(n) for j in range(i + 1)]
    kmaj = [(i, j) for j in range(n) for i in range(j, n)]
    to = lambda ps, c: jnp.asarray(np.array([p[c] for p in ps], np.int32))
    return (to(qmaj, 0), to(qmaj, 1)), (to(kmaj, 0), to(kmaj, 1))


def _att_block(lp, edge=ATT_BLOCK):
    return edge if lp % edge == 0 else MM_BLOCK


def _nt(a, b):
    return lax.dot_general(a, b, (((1,), (1,)), ((), ())), preferred_element_type=F32)


def _attn_fwd(qs, ks, vts):
    lp = qs.shape[1]
    t = _att_block(lp, ATT_BLOCK_FWD)
    n = lp // t
    (qi, kj), _ = _pair_tables(n)
    tc = t // ATT_SPLIT

    def body(qi_ref, kj_ref, q_ref, k_ref, vt_ref, o_ref, lse_ref, m_s, l_s, acc_s):
        p = pl.program_id(1)
        i, j = qi_ref[p], kj_ref[p]

        @pl.when(j == 0)
        def _():
            m_s[...] = jnp.full_like(m_s, NEG_BIG)
            l_s[...] = jnp.zeros_like(l_s)
            acc_s[...] = jnp.zeros_like(acc_s)

        def update(masked):
            m_all, l_all, acc_all = m_s[...], l_s[...], acc_s[...]
            m_out, l_out, acc_out = [], [], []
            qk = lambda c: _nt(k_ref[0], q_ref[0, c * tc:(c + 1) * tc, :])
            ahead = qk(0)
            for c in range(ATT_SPLIT):
                cols = slice(c * tc, (c + 1) * tc)
                sc = ahead * ATT_SCALE_LOG2
                if c + 1 < ATT_SPLIT:
                    ahead = qk(c + 1)
                if masked:
                    keep = (lax.broadcasted_iota(jnp.int32, (t, tc), 1) + c * tc
                            >= lax.broadcasted_iota(jnp.int32, (t, tc), 0))
                    sc = jnp.where(keep, sc, NEG_BIG)
                m_prev = m_all[:, cols]
                m_new = jnp.maximum(m_prev, jnp.max(sc, axis=0, keepdims=True))
                alpha = jnp.exp2(m_prev - m_new)
                pexp = jnp.exp2(sc - m_new)
                l_out.append(alpha * l_all[:, cols] + jnp.sum(pexp, axis=0, keepdims=True))
                acc_out.append(alpha * acc_all[:, cols] + jnp.dot(vt_ref[0], pexp.astype(BF16),
                                                                  preferred_element_type=F32))
                m_out.append(m_new)
            cat = lambda parts: parts[0] if len(parts) == 1 else jnp.concatenate(parts, axis=1)
            m_s[...], l_s[...], acc_s[...] = cat(m_out), cat(l_out), cat(acc_out)

        @pl.when(j < i)
        def _():
            update(False)

        @pl.when(j == i)
        def _():
            update(True)
            o_ref[...] = (acc_s[...] / l_s[...]).T
            lse_ref[0] = m_s[...] + jnp.log2(l_s[...])

    grid_spec = pltpu.PrefetchScalarGridSpec(
        num_scalar_prefetch=2, grid=(ATT_HEADS, int(qi.shape[0])),
        in_specs=[pl.BlockSpec((1, t, QP_W), lambda h, p, qi, kj: (h, qi[p], 0)),
                  pl.BlockSpec((1, t, QP_W), lambda h, p, qi, kj: (h, kj[p], 0)),
                  pl.BlockSpec((1, V_HEAD, t), lambda h, p, qi, kj: (h, 0, kj[p]))],
        out_specs=[pl.BlockSpec((t, V_HEAD), lambda h, p, qi, kj: (qi[p], h)),
                   pl.BlockSpec((1, 1, t), lambda h, p, qi, kj: (h, 0, qi[p]))],
        scratch_shapes=[pltpu.VMEM((1, t), F32), pltpu.VMEM((1, t), F32), pltpu.VMEM((V_HEAD, t), F32)])
    return pl.pallas_call(
        body, grid_spec=grid_spec,
        out_shape=[jax.ShapeDtypeStruct((lp, ATT_HEADS * V_HEAD), F32), jax.ShapeDtypeStruct((ATT_HEADS, 1, lp), F32)],
        name="attn_fwd", compiler_params=_params(("parallel", "arbitrary")))(qi, kj, qs, ks, vts)


def _attn_delta(datt, att):
    lp = att.shape[0]
    t = MM_BLOCK

    def body(do_ref, o_ref, d_ref):
        prod = do_ref[...] * o_ref[...]
        d_ref[0] = jnp.sum(prod.T, axis=0, keepdims=True)

    blk = pl.BlockSpec((t, V_HEAD), lambda h, i: (i, h))
    return pl.pallas_call(
        body, out_shape=jax.ShapeDtypeStruct((ATT_HEADS, 1, lp), F32), grid=(ATT_HEADS, lp // t),
        in_specs=[blk, blk], out_specs=pl.BlockSpec((1, 1, t), lambda h, i: (h, 0, i)),
        name="attn_delta", compiler_params=_params(("parallel", "parallel")))(datt, att)


def _attn_bwd(qs, ks, vs, datt16, lse2, delta):
    lp = qs.shape[1]
    t = _att_block(lp)
    n = lp // t
    _, (qi, kj) = _pair_tables(n)
    tc = t // ATT_SPLIT

    def body(qi_ref, kj_ref, k_ref, v_ref, q_ref, do_ref, lse_ref, dl_ref, dq_ref, dk_ref, dv_ref, dk_s, dv_s):
        p = pl.program_id(1)
        i, j = qi_ref[p], kj_ref[p]

        @pl.when(p == 0)
        def _():
            dq_ref[...] = jnp.zeros_like(dq_ref)

        @pl.when(i == j)
        def _():
            dk_s[...] = jnp.zeros_like(dk_s)
            dv_s[...] = jnp.zeros_like(dv_s)

        def step(masked):
            for c in range(ATT_SPLIT):
                cols = slice(c * tc, (c + 1) * tc)
                q = q_ref[0, cols, :]
                do = do_ref[cols, :]
                pt = jnp.exp2(_nt(k_ref[0], q) * ATT_SCALE_LOG2 - lse_ref[0, :, cols])
                if masked:
                    keep = (lax.broadcasted_iota(jnp.int32, (t, tc), 1) + c * tc
                            >= lax.broadcasted_iota(jnp.int32, (t, tc), 0))
                    pt = jnp.where(keep, pt, 0.0)
                dst = (pt * (_nt(v_ref[0], do) - dl_ref[0, :, cols]) * ATT_SCALE).astype(BF16)
                dv_s[...] += jnp.dot(pt.astype(BF16), do, preferred_element_type=F32)
                dk_s[...] += jnp.dot(dst, q, preferred_element_type=F32)
                rows = pl.ds(pl.multiple_of(i * t + c * tc, tc), tc)
                dq_ref[0, rows, :] += lax.dot_general(dst, k_ref[0], (((0,), (0,)), ((), ())),
                                                      preferred_element_type=F32)

        @pl.when(i > j)
        def _():
            step(False)

        @pl.when(i == j)
        def _():
            step(True)

        @pl.when(i == n - 1)
        def _():
            dk_ref[0] = dk_s[...]
            dv_ref[0] = dv_s[...]

    grid_spec = pltpu.PrefetchScalarGridSpec(
        num_scalar_prefetch=2, grid=(ATT_HEADS, int(qi.shape[0])),
        in_specs=[pl.BlockSpec((1, t, QP_W), lambda h, p, qi, kj: (h, kj[p], 0)),
                  pl.BlockSpec((1, t, V_HEAD), lambda h, p, qi, kj: (h, kj[p], 0)),
                  pl.BlockSpec((1, t, QP_W), lambda h, p, qi, kj: (h, qi[p], 0)),
                  pl.BlockSpec((t, V_HEAD), lambda h, p, qi, kj: (qi[p], h)),
                  pl.BlockSpec((1, 1, t), lambda h, p, qi, kj: (h, 0, qi[p])),
                  pl.BlockSpec((1, 1, t), lambda h, p, qi, kj: (h, 0, qi[p]))],
        out_specs=[pl.BlockSpec((1, lp, QP_W), lambda h, p, qi, kj: (h, 0, 0)),
                   pl.BlockSpec((1, t, QP_W), lambda h, p, qi, kj: (h, kj[p], 0)),
                   pl.BlockSpec((1, t, V_HEAD), lambda h, p, qi, kj: (h, kj[p], 0))],
        scratch_shapes=[pltpu.VMEM((t, QP_W), F32), pltpu.VMEM((t, V_HEAD), F32)])
    return pl.pallas_call(
        body, grid_spec=grid_spec,
        out_shape=[jax.ShapeDtypeStruct((ATT_HEADS, lp, QP_W), F32), jax.ShapeDtypeStruct((ATT_HEADS, lp, QP_W), F32),
                   jax.ShapeDtypeStruct((ATT_HEADS, lp, V_HEAD), F32)],
        name="attn_bwd", compiler_params=_params(("arbitrary", "arbitrary")))(
            qi, kj, ks, vs, qs, datt16, lse2, delta)


N_PAIRS = SSM_HEADS // 2
HI = lax.Precision.HIGHEST


def _ssd_chunk(xp, bs, cs, dt, dt_t, alr, alc, dsk, st):
    q = dt.shape[0]
    li = lax.broadcasted_iota(jnp.int32, (q, q), 0)
    si = lax.broadcasted_iota(jnp.int32, (q, q), 1)
    tri = (si <= li).astype(F32)
    tri_t = (li <= si).astype(F32)
    lo = lax.broadcasted_iota(jnp.int32, (1, 128), 1) < 64
    h_r = lax.broadcasted_iota(jnp.int32, (1, SSM_HEADS), 1)
    h_c = lax.broadcasted_iota(jnp.int32, (SSM_HEADS, 1), 0)
    a = dt * (-jnp.exp(alr))
    a_t = dt_t * (-jnp.exp(alc))
    acum = jnp.dot(tri, a, precision=HI, preferred_element_type=F32)
    acum_t = jnp.dot(a_t, tri_t, precision=HI, preferred_element_type=F32)
    last = (lax.broadcasted_iota(jnp.int32, (q, 1), 0) == q - 1).astype(F32)
    alast = jnp.sum(acum * last, axis=0, keepdims=True)
    e = jnp.exp(acum)
    rdt = jnp.exp(alast - acum) * dt
    e_last = jnp.exp(alast)

    def col(m, h):
        return jnp.sum(m * (h_r == h).astype(F32), axis=1, keepdims=True)

    def row(m, h):
        return jnp.sum(m * (h_c == h).astype(F32), axis=0, keepdims=True)

    def pair(m, ha):
        return jnp.where(lo, col(m, ha), col(m, ha + 1))

    ys, st_new = [], []
    for g in range(2):
        c_b = cs[g].astype(BF16)
        b_b = bs[g].astype(BF16)
        cb = _nt(c_b, b_b)
        for j in range(N_PAIRS // 2):
            p = (N_PAIRS // 2) * g + j
            ha = 2 * p
            x = xp[p]
            x_b = x.astype(BF16)

            def w_of(h):
                seg = col(acum, h) - row(acum_t, h)
                return (cb * jnp.exp(jnp.minimum(seg, 0.0)) * tri * row(dt_t, h)).astype(BF16)

            y_diag = jnp.where(lo, jnp.dot(w_of(ha), x_b, preferred_element_type=F32),
                               jnp.dot(w_of(ha + 1), x_b, preferred_element_type=F32))
            y_off = jnp.dot(c_b, st[p].astype(BF16), preferred_element_type=F32) * pair(e, ha)
            ys.append(y_diag + y_off + pair(dsk, ha) * x)
            xw = (x * pair(rdt, ha)).astype(BF16)
            st_new.append(st[p] * pair(e_last, ha)
                          + lax.dot_general(b_b, xw, (((0,), (0,)), ((), ())), preferred_element_type=F32))
    return ys, st_new


def _ssd_fwd(xs, bc, dt, dt_t, alr, alc, dsk):
    lp = xs.shape[0]
    q = SSD_CHUNK
    nc = lp // q

    def body(x_ref, b_ref, c_ref, dt_ref, dtt_ref, alr_ref, alc_ref, dsk_ref, y_ref, sp_ref, st_s):
        @pl.when(pl.program_id(0) == 0)
        def _():
            st_s[...] = jnp.zeros_like(st_s)

        sp_ref[0] = st_s[...]
        xp = [x_ref[:, 128 * p:128 * (p + 1)] for p in range(N_PAIRS)]
        bs = [b_ref[:, 0:128], b_ref[:, 128:256]]
        cs = [c_ref[:, 0:128], c_ref[:, 128:256]]
        ys, st_new = _ssd_chunk(xp, bs, cs, dt_ref[...], dtt_ref[...], alr_ref[...], alc_ref[...], dsk_ref[...],
                                [st_s[p] for p in range(N_PAIRS)])
        for p in range(N_PAIRS):
            y_ref[:, 128 * p:128 * (p + 1)] = ys[p]
            st_s[p] = st_new[p]

    return pl.pallas_call(
        body,
        out_shape=[jax.ShapeDtypeStruct((lp, SSM_WIDTH), F32), jax.ShapeDtypeStruct((nc, N_PAIRS, 128, 128), F32)],
        grid=(nc,),
        in_specs=[_rb(q, SSM_WIDTH), _rb(q, 256, 0), _rb(q, 256, 1), _rb(q, SSM_HEADS),
                  pl.BlockSpec((SSM_HEADS, q), lambda i: (0, i)),
                  _full((1, SSM_HEADS)), _full((SSM_HEADS, 1)), _full((1, SSM_HEADS))],
        out_specs=[_rb(q, SSM_WIDTH), pl.BlockSpec((1, N_PAIRS, 128, 128), lambda i: (i, 0, 0, 0))],
        scratch_shapes=[pltpu.VMEM((N_PAIRS, 128, 128), F32)],
        name="ssd_fwd", compiler_params=_params(("arbitrary",)))(xs, bc, bc, dt, dt_t, alr, alc, dsk)


def _ssd_bwd(xs, bc, dt, dt_t, alr, alc, dsk, sprev, dy):
    lp = xs.shape[0]
    q = SSD_CHUNK
    nc = lp // q

    def body(x_ref, b_ref, c_ref, dt_ref, dtt_ref, alr_ref, alc_ref, dsk_ref, sp_ref, dy_ref,
             dx_ref, dbc_ref, ddt_ref, ddtt_ref, dalr_ref, dalc_ref, ddsk_ref, ds_s):
        @pl.when(pl.program_id(0) == 0)
        def _():
            ds_s[...] = jnp.zeros_like(ds_s)

        xp = [x_ref[:, 128 * p:128 * (p + 1)] for p in range(N_PAIRS)]
        bs = [b_ref[:, 0:128], b_ref[:, 128:256]]
        cs = [c_ref[:, 0:128], c_ref[:, 128:256]]
        st = [sp_ref[0, p] for p in range(N_PAIRS)]
        _, vjp = jax.vjp(_ssd_chunk, xp, bs, cs, dt_ref[...], dtt_ref[...], alr_ref[...], alc_ref[...], dsk_ref[...],
                         st)
        dys = [dy_ref[:, 128 * p:128 * (p + 1)] for p in range(N_PAIRS)]
        dxp, dbs, dcs, ddt, ddtt, dalr, dalc, ddsk, dst = vjp((dys, [ds_s[p] for p in range(N_PAIRS)]))
        for p in range(N_PAIRS):
            dx_ref[:, 128 * p:128 * (p + 1)] = dxp[p]
            ds_s[p] = dst[p]
        for g in range(2):
            dbc_ref[:, 128 * g:128 * (g + 1)] = dbs[g]
            dbc_ref[:, 256 + 128 * g:256 + 128 * (g + 1)] = dcs[g]
        ddt_ref[...] = ddt
        ddtt_ref[...] = ddtt
        _acc_add(dalr_ref, dalr)
        _acc_add(dalc_ref, dalc)
        _acc_add(ddsk_ref, ddsk)

    rev = lambda width, cb=0: pl.BlockSpec((q, width), lambda i: (nc - 1 - i, cb))
    return pl.pallas_call(
        body,
        out_shape=[jax.ShapeDtypeStruct((lp, SSM_WIDTH), F32), jax.ShapeDtypeStruct((lp, 512), F32),
                   jax.ShapeDtypeStruct((lp, SSM_HEADS), F32), jax.ShapeDtypeStruct((SSM_HEADS, lp), F32),
                   jax.ShapeDtypeStruct((1, SSM_HEADS), F32), jax.ShapeDtypeStruct((SSM_HEADS, 1), F32),
                   jax.ShapeDtypeStruct((1, SSM_HEADS), F32)],
        grid=(nc,),
        in_specs=[rev(SSM_WIDTH), rev(256, 0), rev(256, 1), rev(SSM_HEADS),
                  pl.BlockSpec((SSM_HEADS, q), lambda i: (0, nc - 1 - i)),
                  _full((1, SSM_HEADS)), _full((SSM_HEADS, 1)), _full((1, SSM_HEADS)),
                  pl.BlockSpec((1, N_PAIRS, 128, 128), lambda i: (nc - 1 - i, 0, 0, 0)), rev(SSM_WIDTH)],
        out_specs=[rev(SSM_WIDTH), rev(512), rev(SSM_HEADS), pl.BlockSpec((SSM_HEADS, q), lambda i: (0, nc - 1 - i)),
                   _full((1, SSM_HEADS)), _full((SSM_HEADS, 1)), _full((1, SSM_HEADS))],
        scratch_shapes=[pltpu.VMEM((N_PAIRS, 128, 128), F32)],
        name="ssd_bwd", compiler_params=_params(("arbitrary",)))(xs, bc, bc, dt, dt_t, alr, alc, dsk, sprev, dy)


def _q_to_slab_order(w):
    hd = QK_NOPE + QK_ROPE
    nope = [w[:, h * hd:h * hd + QK_NOPE] for h in range(ATT_HEADS)]
    rope = [w[:, h * hd + QK_NOPE:(h + 1) * hd] for h in range(ATT_HEADS)]
    return jnp.concatenate(nope + rope, axis=1)


def _q_from_slab_order(wp):
    base = ATT_HEADS * QK_NOPE
    parts = []
    for h in range(ATT_HEADS):
        parts += [wp[:, QK_NOPE * h:QK_NOPE * (h + 1)], wp[:, base + QK_ROPE * h:base + QK_ROPE * (h + 1)]]
    return jnp.concatenate(parts, axis=1)


_IN_CQ, _IN_CKV, _IN_KR, _IN_Z, _IN_XS, _IN_BC, _IN_DT = (0, 384), (384, 640), (640, 704), (704, 1728), (1728, 2752), \
    (2752, 3264), (3264, 3280)


def _pack_w_in(w):
    z = lambda n: jnp.zeros((w.shape[0], n), w.dtype)
    s = lambda r: w[:, r[0]:r[1]]
    return jnp.concatenate([s(_IN_CKV), s(_IN_KR), z(64), s(_IN_CQ), s(_IN_DT), z(112), z(128), s(_IN_Z), s(_IN_XS),
                            s(_IN_BC)], axis=1)


def _unpack_w_in(wp):
    s = lambda off, n: wp[:, off:off + n]
    return jnp.concatenate([s(SEG_CQ, 384), s(SEG_KV, 256), s(SEG_KR, 64), s(SEG_Z, 1024), s(SEG_XS, 1024),
                            s(SEG_BC, 512), s(SEG_DT, 16)], axis=1)


def _rope_tables(lp):
    inv_freq = ROPE_THETA ** (-jnp.arange(0, QK_ROPE, 2, dtype=F32) / QK_ROPE)
    ang = jnp.arange(lp, dtype=F32)[:, None] * inv_freq[None, :]
    cos, sin = jnp.cos(ang), jnp.sin(ang)
    return jnp.tile(cos, (1, 4)), jnp.concatenate([-sin, sin, -sin, sin], axis=1)


def _local_step(x, tgt, w):
    n_real = x.shape[0]
    l = N_META + n_real
    lp = -(-l // MM_BLOCK) * MM_BLOCK
    pad = lambda a: jnp.concatenate([a, jnp.zeros((lp - l, D_MODEL), F32)], axis=0)
    h0 = pad(jnp.concatenate([w["meta_tokens"], x], axis=0))
    tgt_p = pad(jnp.concatenate([jnp.zeros((N_META, D_MODEL), F32), tgt], axis=0))
    cos_t, sin_t = _rope_tables(lp)

    w_in_p = _pack_w_in(w["w_in"])
    w_q_p = _q_to_slab_order(w["w_q_up"])
    w_kv, w_out, w_up, w_down = w["w_kv_up"], w["w_out"], w["w_mlp_up"], w["w_mlp_down"]
    conv_w, conv_b = w["conv_w"], w["conv_b"]
    dt_bias_pad = jnp.concatenate([w["dt_bias"], jnp.zeros((1, 128 - SSM_HEADS), F32)], axis=1)
    alr, dsk = w["a_log"], w["d_skip"]
    alc = alr.reshape(SSM_HEADS, 1)

    n1 = _norm_in(h0, w["norm_mix_pre"])
    proj = _mm("proj", n1, w_in_p)
    cqn, ckvn, krr = _attn_prep(proj, w["q_a_norm"], w["kv_a_norm"], cos_t, sin_t)
    q = _mm("q_up", cqn, w_q_p)
    kv = _mm("kv_up", ckvn, w_kv)
    qs, ks, vs, vts = _qk_pack(q, kv, krr, cos_t, sin_t)
    att, lse2 = _attn_fwd(qs, ks, vts)
    xs_pre, xs_act = _conv_fwd("conv_xs_fwd", proj, SEG_XS, SSM_WIDTH, conv_w[:, :SSM_WIDTH], conv_b[:, :SSM_WIDTH])
    bc_pre, bc_act = _conv_fwd("conv_bc_fwd", proj, SEG_BC, 512, conv_w[:, SSM_WIDTH:], conv_b[:, SSM_WIDTH:])
    dt = _dt_fwd(proj, dt_bias_pad)[:, :SSM_HEADS]
    dt_t = dt.T
    y, sprev = _ssd_fwd(xs_act, bc_act, dt, dt_t, alr, alc, dsk)
    ssm = _gated_norm_fwd(y, proj, w["ssm_norm"])
    cat = jnp.concatenate([att.astype(BF16), ssm], axis=1)
    mix = _mm("out_proj", cat, w_out)
    h1, n2 = _mix_residual(h0, mix, w["norm_mix_post"], w["norm_mlp_pre"])
    relu2 = lambda r: jnp.square(jnp.maximum(r, 0.0))
    u, act = _mm("mlp_up", n2, w_up, outs=((F32, None), (BF16, relu2)))
    f = _mm("mlp_down", act, w_down)
    loss, dh2, df, dg_mlp_post = _loss_and_grad(h1, f, w["norm_mlp_post"], tgt_p, n_real)

    g = {"norm_mlp_post": dg_mlp_post}
    g["w_mlp_down"] = _mm_tn("d_w_mlp_down", act, df)
    du = _mm("d_mlp_act", df, w_down, outs=((BF16, lambda r, ub: r * (2.0 * jnp.maximum(ub, 0.0))),), epi_ins=(u,),
             trans_b=True)
    g["w_mlp_up"] = _mm_tn("d_w_mlp_up", n2, du)
    dn2 = _mm("d_n2", du, w_up, trans_b=True)
    dh1, dmix, g["norm_mlp_pre"], g["norm_mix_post"] = _mlp_residual_bwd(dh2, dn2, h1, w["norm_mlp_pre"], mix,
                                                                         w["norm_mix_post"])
    g["w_out"] = _mm_tn("d_w_out", cat, dmix)
    dcat, dcat16 = _mm("d_cat", dmix, w_out, outs=((F32, None), (BF16, None)), trans_b=True)
    dy, dz, g["ssm_norm"] = _gated_norm_bwd(y, proj, w["ssm_norm"], dcat)
    dxs_act, dbc_act, ddt, ddt_t, dalr, dalc, g["d_skip"] = _ssd_bwd(xs_act, bc_act, dt, dt_t, alr, alc, dsk, sprev, dy)
    g["a_log"] = dalr + dalc.reshape(1, SSM_HEADS)
    dxs, dcw_xs, dcb_xs = _conv_bwd("conv_xs_bwd", dxs_act, xs_pre, proj, SEG_XS, SSM_WIDTH, conv_w[:, :SSM_WIDTH])
    dbc, dcw_bc, dcb_bc = _conv_bwd("conv_bc_bwd", dbc_act, bc_pre, proj, SEG_BC, 512, conv_w[:, SSM_WIDTH:])
    g["conv_w"] = jnp.concatenate([dcw_xs[:CONV_K], dcw_bc[:CONV_K]], axis=1)
    g["conv_b"] = jnp.concatenate([dcb_xs, dcb_bc], axis=1)
    ddt_pad = jnp.concatenate([ddt + ddt_t.T, jnp.zeros((lp, 128 - SSM_HEADS), F32)], axis=1)

    dqs, dks, dvs = _attn_bwd(qs, ks, vs, dcat16, lse2, _attn_delta(dcat, att))
    dq, dkv, dkr = _qk_unpack_bwd(dqs, dks, dvs, cos_t, sin_t)
    g["w_q_up"] = _q_from_slab_order(_mm_tn("d_w_q_up", cqn, dq))
    g["w_kv_up"] = _mm_tn("d_w_kv_up", ckvn, dkv)
    dcqn = _mm("d_cqn", dq, w_q_p, trans_b=True)
    dckvn = _mm("d_ckvn", dkv, w_kv, trans_b=True)
    dproj, g["q_a_norm"], g["kv_a_norm"], ddtb = _proj_grad(proj, dcqn, dckvn, w["q_a_norm"], w["kv_a_norm"], dkr,
                                                          ddt_pad, dt_bias_pad, dz, dxs, dbc)
    g["dt_bias"] = ddtb[:, :SSM_HEADS]
    g["w_in"] = _unpack_w_in(_mm_tn("d_w_in", n1, dproj))
    dn1 = _mm("d_n1", dproj, w_in_p, trans_b=True)
    dh0, g["norm_mix_pre"] = _input_norm_bwd(dh1, dn1, h0, w["norm_mix_pre"])
    g["meta_tokens"] = dh0[:N_META]
    return loss, dh0, g


WEIGHTS = ["meta_tokens", "norm_mix_pre", "w_in", "q_a_norm", "w_q_up", "kv_a_norm", "w_kv_up", "conv_w", "conv_b",
           "dt_bias", "a_log", "d_skip", "ssm_norm", "w_out", "norm_mix_post", "norm_mlp_pre", "w_mlp_up",
           "w_mlp_down", "norm_mlp_post"]
SHARD_AXIS = {"meta_tokens": 1, "w_in": 1, "w_q_up": 1, "w_kv_up": 1, "conv_w": 1, "w_out": 0, "w_mlp_up": 1,
              "w_mlp_down": 0}
FULL_SHAPE = {"meta_tokens": (16, 1024), "norm_mix_pre": (1, 1024), "w_in": (1024, 3280), "q_a_norm": (1, 384),
              "w_q_up": (384, 1536), "kv_a_norm": (1, 256), "w_kv_up": (256, 2048), "conv_w": (4, 1536),
              "conv_b": (1, 1536), "dt_bias": (1, 16), "a_log": (1, 16), "d_skip": (1, 16), "ssm_norm": (1, 1024),
              "w_out": (2048, 1024), "norm_mix_post": (1, 1024), "norm_mlp_pre": (1, 1024), "w_mlp_up": (1024, 4096),
              "w_mlp_down": (4096, 1024), "norm_mlp_post": (1, 1024)}
GATHER_BF16 = ["w_in", "w_q_up", "w_kv_up", "w_out", "w_mlp_up", "w_mlp_down"]
GATHER_F32 = ["meta_tokens", "conv_w"]


def _shard_shape(name):
    shp = list(FULL_SHAPE[name])
    if name in SHARD_AXIS:
        shp[SHARD_AXIS[name]] //= N_CHIPS
    return tuple(shp)


PACK_ORDER = sorted(WEIGHTS, key=lambda n: -_shard_shape(n)[0])


def _packed_rows(shape):
    r, c = shape
    return r if c <= PACK_W else -(-c // PACK_W)


def _pack_rows(arrays, row_multiple):
    parts = []
    for a in arrays:
        r, c = a.shape
        if c > PACK_W:
            assert r == 1, a.shape
            folded = _packed_rows(a.shape)
            a = jnp.pad(a, ((0, 0), (0, folded * PACK_W - c))).reshape(folded, PACK_W)
        elif c < PACK_W:
            a = jnp.pad(a, ((0, 0), (0, PACK_W - c)))
        parts.append(a)
    rows = sum(p.shape[0] for p in parts)
    if rows % row_multiple:
        parts.append(jnp.zeros((row_multiple - rows % row_multiple, PACK_W), parts[0].dtype))
    return jnp.concatenate(parts, axis=0)


def _unpack_rows(packed, shapes):
    out, off = [], 0
    for r, c in shapes:
        nr = _packed_rows((r, c))
        blk = packed[off:off + nr]
        out.append(blk[:, :c] if c <= PACK_W else blk.reshape(1, nr * PACK_W)[:, :c])
        off += nr
    return out


def _chip_slice(full, name, t):
    if name not in SHARD_AXIS:
        return full
    ax = SHARD_AXIS[name]
    n = FULL_SHAPE[name][ax] // N_CHIPS
    return lax.slice_in_dim(full, t * n, (t + 1) * n, axis=ax)


HBM_SPEC = pl.BlockSpec(memory_space=pl.ANY)
CHIP_FLIPS = ((1, 0), (0, 1), (1, 1))


def _gather_chips(bufs):
    nb = len(bufs)

    def body(*refs):
        ins, outs = refs[:nb], refs[nb:2 * nb]
        send, recv, loc = refs[2 * nb:]
        x, y, c = lax.axis_index("x"), lax.axis_index("y"), lax.axis_index("c")
        me = 2 * x + y
        copies = []
        for b in range(nb):
            own = pltpu.make_async_copy(ins[b], outs[b].at[me], loc.at[b])
            own.start()
            copies.append(own)
            for k, (fx, fy) in enumerate(CHIP_FLIPS):
                cp = pltpu.make_async_remote_copy(
                    src_ref=ins[b], dst_ref=outs[b].at[me], send_sem=send.at[b, k], recv_sem=recv.at[b, k],
                    device_id=(x ^ fx, y ^ fy, c), device_id_type=MESH_ID)
                cp.start()
                copies.append(cp)
        for cp in copies:
            cp.wait()

    return pl.pallas_call(
        body,
        out_shape=[jax.ShapeDtypeStruct((N_CHIPS,) + b.shape, b.dtype) for b in bufs],
        in_specs=[HBM_SPEC] * nb, out_specs=[HBM_SPEC] * nb,
        scratch_shapes=[pltpu.SemaphoreType.DMA((nb, 3)), pltpu.SemaphoreType.DMA((nb, 3)),
                        pltpu.SemaphoreType.DMA((nb,))],
        name="gather_chips")(*bufs)


def _sibling_swap(name, buf):
    def body(src, dst, send, recv):
        x, y, c = lax.axis_index("x"), lax.axis_index("y"), lax.axis_index("c")
        cp = pltpu.make_async_remote_copy(src_ref=src, dst_ref=dst, send_sem=send, recv_sem=recv,
                                          device_id=(x, y, 1 - c), device_id_type=MESH_ID)
        cp.start()
        cp.wait()

    return pl.pallas_call(
        body, out_shape=jax.ShapeDtypeStruct(buf.shape, buf.dtype), in_specs=[HBM_SPEC], out_specs=HBM_SPEC,
        scratch_shapes=[pltpu.SemaphoreType.DMA, pltpu.SemaphoreType.DMA], name=name)(buf)


def _scatter_chips(part):
    def body(src, dst, send, recv):
        x, y, c = lax.axis_index("x"), lax.axis_index("y"), lax.axis_index("c")
        copies = []
        for k, (fx, fy) in enumerate(CHIP_FLIPS):
            tx, ty = x ^ fx, y ^ fy
            cp = pltpu.make_async_remote_copy(
                src_ref=src.at[2 * tx + ty], dst_ref=dst.at[k], send_sem=send.at[k], recv_sem=recv.at[k],
                device_id=(tx, ty, c), device_id_type=MESH_ID)
            cp.start()
            copies.append(cp)
        for cp in copies:
            cp.wait()

    return pl.pallas_call(
        body, out_shape=jax.ShapeDtypeStruct((3,) + part.shape[1:], part.dtype), in_specs=[HBM_SPEC],
        out_specs=HBM_SPEC, scratch_shapes=[pltpu.SemaphoreType.DMA((3,)), pltpu.SemaphoreType.DMA((3,))],
        name="scatter_chips")(part)


def _add_rows(name, terms):
    rows = terms[0].shape[0]
    t = _row_tile(rows)

    def body(*refs):
        acc = refs[0][...]
        for r in refs[1:-1]:
            acc = acc + r[...]
        refs[-1][...] = acc

    return pl.pallas_call(
        body, out_shape=jax.ShapeDtypeStruct(terms[0].shape, F32), grid=(rows // t,),
        in_specs=[_rb(t, PACK_W)] * len(terms), out_specs=_rb(t, PACK_W),
        name=name, compiler_params=_params(("parallel",)))(*terms)


def _row_tile(rows):
    best = 8
    for t in range(8, 513, 8):
        if rows % t == 0:
            best = t
    return best


def _adamw(g, w, m, v):
    rows = g.shape[0]
    t = _row_tile(rows)
    c1 = 1.0 - ADAM_B1 ** ADAM_STEP
    c2 = 1.0 - ADAM_B2 ** ADAM_STEP

    def body(g_ref, w_ref, m_ref, v_ref, d_ref, mo_ref, vo_ref):
        gg = g_ref[...]
        mn = ADAM_B1 * m_ref[...] + (1.0 - ADAM_B1) * gg
        vn = ADAM_B2 * v_ref[...] + (1.0 - ADAM_B2) * (gg * gg)
        d_ref[...] = -ADAM_LR * ((mn / c1) / (jnp.sqrt(vn / c2) + ADAM_EPS) + ADAM_WD * w_ref[...])
        mo_ref[...] = mn
        vo_ref[...] = vn

    return pl.pallas_call(
        body, out_shape=[jax.ShapeDtypeStruct(g.shape, F32)] * 3, grid=(rows // t,),
        in_specs=[_rb(t, PACK_W)] * 4, out_specs=[_rb(t, PACK_W)] * 3,
        name="adamw", compiler_params=_params(("parallel",)))(g, w, m, v)


def kernel(x, meta_tokens, norm_mix_pre, w_in, q_a_norm, w_q_up, kv_a_norm, w_kv_up, conv_w, conv_b, dt_bias, a_log, d_skip, ssm_norm, w_out, norm_mix_post, norm_mlp_pre, w_mlp_up, w_mlp_down, norm_mlp_post, loss_target, m_meta_tokens, m_norm_mix_pre, m_w_in, m_q_a_norm, m_w_q_up, m_kv_a_norm, m_w_kv_up, m_conv_w, m_conv_b, m_dt_bias, m_a_log, m_d_skip, m_ssm_norm, m_w_out, m_norm_mix_post, m_norm_mlp_pre, m_w_mlp_up, m_w_mlp_down, m_norm_mlp_post, v_meta_tokens, v_norm_mix_pre, v_w_in, v_q_a_norm, v_w_q_up, v_kv_a_norm, v_w_kv_up, v_conv_w, v_conv_b, v_dt_bias, v_a_log, v_d_skip, v_ssm_norm, v_w_out, v_norm_mix_post, v_norm_mlp_pre, v_w_mlp_up, v_w_mlp_down, v_norm_mlp_post):
    given = dict(locals())
    drop = lambda name, a: a[0] if a.ndim == 3 else a
    w_loc = {n: drop(n, given[n]) for n in WEIGHTS}
    m_loc = {n: drop(n, given["m_" + n]) for n in WEIGHTS}
    v_loc = {n: drop(n, given["v_" + n]) for n in WEIGHTS}
    ix, iy, ic = lax.axis_index("x"), lax.axis_index("y"), lax.axis_index("c")
    me = 2 * ix + iy

    sent16 = _pack_rows([w_loc[n].astype(BF16) for n in GATHER_BF16], 16)
    sent32 = _pack_rows([w_loc[n] for n in GATHER_F32], 8)
    got16, got32 = _gather_chips([sent16, sent32])
    w_full = {n: w_loc[n] for n in WEIGHTS if n not in SHARD_AXIS}
    for names, got in ((GATHER_BF16, got16), (GATHER_F32, got32)):
        per_chip = [_unpack_rows(got[t], [_shard_shape(n) for n in names]) for t in range(N_CHIPS)]
        for k, n in enumerate(names):
            w_full[n] = jnp.concatenate([per_chip[t][k] for t in range(N_CHIPS)], axis=SHARD_AXIS[n])

    loss, dh0, g_full = _local_step(x[0], loss_target[0], w_full)
    n_real = x.shape[1]
    grad_x = dh0[N_META:N_META + n_real][None]

    shapes = [_shard_shape(n) for n in PACK_ORDER]
    slots = [_pack_rows([_chip_slice(g_full[n], n, t) for n in PACK_ORDER], 16) for t in range(N_CHIPS)]
    rows = slots[0].shape[0]
    half = rows // 2
    halves = lambda hh: jnp.concatenate([lax.dynamic_slice_in_dim(s, hh * half, half, axis=0) for s in slots], axis=0)
    keep, give = halves(ic), halves(1 - ic)
    from_sibling = _sibling_swap("sibling_swap", give)
    chip_part = _add_rows("chip_partial", [keep, from_sibling]).reshape(N_CHIPS, half, PACK_W)
    from_chips = _scatter_chips(chip_part)
    own = lax.dynamic_index_in_dim(chip_part, me, axis=0, keepdims=False)
    rel_of_xor = {2: 0, 1: 1, 3: 2}
    terms = []
    for t in range(N_CHIPS):
        cand = own
        for xr, k in rel_of_xor.items():
            cand = jnp.where((me ^ t) == xr, from_chips[k], cand)
        terms.append(cand)
    my_half = _add_rows("chip_total", terms)
    other_half = _sibling_swap("sibling_gather", my_half)
    g_red = jnp.where(ic == 0, jnp.concatenate([my_half, other_half], axis=0),
                      jnp.concatenate([other_half, my_half], axis=0))

    pack_loc = lambda d: _pack_rows([d[n] for n in PACK_ORDER], 16)
    delta, new_m, new_v = _adamw(g_red, pack_loc(w_loc), pack_loc(m_loc), pack_loc(v_loc))

    def outputs(packed_arr):
        parts = dict(zip(PACK_ORDER, _unpack_rows(packed_arr, shapes)))
        return [parts[n][None] if given[n].ndim == 3 else parts[n] for n in WEIGHTS]

    total = lax.psum(loss[0, 0], ("x", "y", "c"))
    return (total, grad_x, *outputs(g_red), *outputs(delta), *outputs(new_m), *outputs(new_v))
```

```python
import functools

import numpy as np
import jax
import jax.numpy as jnp
from jax import lax
from jax.experimental import pallas as pl
from jax.experimental.pallas import tpu as pltpu

F32 = jnp.float32
BF16 = jnp.bfloat16

D_MODEL = 1024
N_META = 16
EPS = 1e-6
ATT_HEADS = 8
Q_LORA = 384
KV_LORA = 256
QK_NOPE = 128
QK_ROPE = 64
V_HEAD = 128
ROPE_THETA = 10000.0
SSM_HEADS = 16
SSM_HEAD_DIM = 64
SSM_WIDTH = 1024
SSM_STATE = 128
CONV_K = 4
D_FF = 4096
ATT_SCALE = float((QK_NOPE + QK_ROPE) ** -0.5)
ATT_SCALE_LOG2 = float(ATT_SCALE * np.log2(np.e))

ADAM_LR = 0.001
ADAM_B1 = 0.9
ADAM_B2 = 0.999
ADAM_EPS = 1e-08
ADAM_WD = 0.01
ADAM_STEP = 10

SEG_KV, SEG_KR, SEG_CQ, SEG_DT, SEG_Z, SEG_XS, SEG_BC = 0, 256, 384, 768, 1024, 2048, 3072
PROJ_W = 3584
QP_W = 256

ROW_BLOCK = 512
MM_BLOCK = 512
SSD_CHUNK = 256
ATT_SPLIT = 1
ATT_BLOCK = 768
ATT_BLOCK_FWD = 1536
VMEM_LIMIT = 56 * 1024 * 1024
NEG_BIG = -1e30

PACK_W = 1024
N_CHIPS = 4
MESH_ID = pl.DeviceIdType.MESH


def _params(sem):
    return pltpu.CompilerParams(dimension_semantics=sem, vmem_limit_bytes=VMEM_LIMIT)


def _rb(rows, width, cb=0):
    return pl.BlockSpec((rows, width), lambda i: (i, cb))


def _full(shape):
    zeros = (0,) * len(shape)
    return pl.BlockSpec(shape, lambda i: zeros)


def _acc_add(ref, val):
    first = pl.program_id(0) == 0

    @pl.when(first)
    def _():
        ref[...] = val

    @pl.when(jnp.logical_not(first))
    def _():
        ref[...] += val


def _rms(x, g):
    r = lax.rsqrt(jnp.mean(x * x, axis=-1, keepdims=True) + EPS)
    return x * r * g


def _rms_bwd(x, g, dy):
    r = lax.rsqrt(jnp.mean(x * x, axis=-1, keepdims=True) + EPS)
    dyg = dy * g
    dx = r * dyg - x * (r * r * r) * jnp.mean(x * dyg, axis=-1, keepdims=True)
    dg = jnp.sum(dy * x * r, axis=0, keepdims=True)
    return dx, dg


def _sigmoid(x):
    return 1.0 / (1.0 + jnp.exp(-x))


def _swap32(x):
    lane = lax.broadcasted_iota(jnp.int32, x.shape, 1)
    return jnp.where((lane % 64) < 32, pltpu.roll(x, 96, 1), pltpu.roll(x, 32, 1))


def _rope(x, cos_t, sin_t):
    return x * cos_t + _swap32(x) * sin_t


def _rope_bwd(dr, cos_t, sin_t):
    return dr * cos_t + _swap32(dr * sin_t)


def _tile(n, cap):
    if n <= cap:
        return n
    best = 128
    for t in range(128, cap + 1, 128):
        if n % t == 0:
            best = t
    assert n % best == 0, (n, cap)
    return best


MM_VMEM_BUDGET = 40 * 1024 * 1024
TN_ROWS_CAP = 1536


def _mm(name, a, b, outs=((F32, None),), epi_ins=(), trans_b=False):
    m, k = a.shape
    n = b.shape[0] if trans_b else b.shape[1]
    tm = MM_BLOCK
    n_epi = len(epi_ins)
    out_bytes = sum(jnp.dtype(dt).itemsize for dt, _ in outs) + sum(e.dtype.itemsize for e in epi_ins)
    step_bytes = lambda tn: 2 * (tm * k * a.dtype.itemsize + k * tn * b.dtype.itemsize + tm * tn * out_bytes)
    tn = n
    while step_bytes(tn) > MM_VMEM_BUDGET and tn % 256 == 0:
        tn //= 2
    assert n % tn == 0 and step_bytes(tn) <= MM_VMEM_BUDGET, (name, n, tn)

    def body(a_ref, b_ref, *rest):
        epi_refs = rest[:n_epi]
        out_refs = rest[n_epi:]
        lhs, rhs = a_ref[...].astype(BF16), b_ref[...].astype(BF16)
        r = _nt(lhs, rhs) if trans_b else jnp.dot(lhs, rhs, preferred_element_type=F32)
        blocks = [e[...] for e in epi_refs]
        for o_ref, (dt, fn) in zip(out_refs, outs):
            o_ref[...] = (r if fn is None else fn(r, *blocks)).astype(dt)

    out_spec = pl.BlockSpec((tm, tn), lambda j, i: (i, j))
    b_spec = pl.BlockSpec((tn, k), lambda j, i: (j, 0)) if trans_b else pl.BlockSpec((k, tn), lambda j, i: (0, j))
    res = pl.pallas_call(
        body,
        out_shape=[jax.ShapeDtypeStruct((m, n), dt) for dt, _ in outs],
        grid=(n // tn, m // tm),
        in_specs=[pl.BlockSpec((tm, k), lambda j, i: (i, 0)), b_spec] + [out_spec] * n_epi,
        out_specs=[out_spec] * len(outs),
        name=name,
        compiler_params=_params(("parallel", "parallel")),
    )(a, b, *epi_ins)
    return res[0] if len(outs) == 1 else res


def _mm_tn(name, x, dy, ta_cap=1024, tn_cap=1024):
    l, a = x.shape
    n = dy.shape[1]
    ta, tn = _tile(a, ta_cap), _tile(n, tn_cap)
    tl = max(t for t in range(MM_BLOCK, TN_ROWS_CAP + 1, MM_BLOCK) if l % t == 0)
    nl = l // tl

    def body(x_ref, dy_ref, o_ref):
        ll = pl.program_id(2)

        @pl.when(ll == 0)
        def _():
            o_ref[...] = jnp.zeros_like(o_ref)

        o_ref[...] += lax.dot_general(x_ref[...].astype(BF16), dy_ref[...].astype(BF16), (((0,), (0,)), ((), ())),
                                      preferred_element_type=F32)

    return pl.pallas_call(
        body,
        out_shape=jax.ShapeDtypeStruct((a, n), F32),
        grid=(a // ta, n // tn, nl),
        in_specs=[pl.BlockSpec((tl, ta), lambda i, j, ll: (ll, i)), pl.BlockSpec((tl, tn), lambda i, j, ll: (ll, j))],
        out_specs=pl.BlockSpec((ta, tn), lambda i, j, ll: (i, j)),
        name=name,
        compiler_params=_params(("parallel", "parallel", "arbitrary")),
    )(x, dy)


def _norm_in(h0, g_pre):
    lp = h0.shape[0]
    t = ROW_BLOCK

    def body(h_ref, g_ref, o_ref):
        o_ref[...] = _rms(h_ref[...], g_ref[...]).astype(BF16)

    return pl.pallas_call(
        body, out_shape=jax.ShapeDtypeStruct((lp, D_MODEL), BF16), grid=(lp // t,),
        in_specs=[_rb(t, D_MODEL), _full((1, D_MODEL))], out_specs=_rb(t, D_MODEL),
        name="norm_in", compiler_params=_params(("parallel",)))(h0, g_pre)


def _attn_prep(proj, g_q, g_kv, cos_t, sin_t):
    lp = proj.shape[0]
    t = ROW_BLOCK

    def body(ckv_ref, kr_ref, cq_ref, gq_ref, gkv_ref, cos_ref, sin_ref, cqn_ref, ckvn_ref, krr_ref):
        cqn_ref[...] = _rms(cq_ref[...], gq_ref[...]).astype(BF16)
        ckvn_ref[...] = _rms(ckv_ref[...], gkv_ref[...]).astype(BF16)
        roped = _rope(kr_ref[...], cos_ref[...], sin_ref[...])
        krr_ref[...] = roped + pltpu.roll(roped, 64, 1)

    return pl.pallas_call(
        body,
        out_shape=[jax.ShapeDtypeStruct((lp, Q_LORA), BF16), jax.ShapeDtypeStruct((lp, KV_LORA), BF16),
                   jax.ShapeDtypeStruct((lp, 128), F32)],
        grid=(lp // t,),
        in_specs=[_rb(t, KV_LORA, SEG_KV // KV_LORA), _rb(t, 128, SEG_KR // 128), _rb(t, Q_LORA, SEG_CQ // Q_LORA),
                  _full((1, Q_LORA)), _full((1, KV_LORA)), _rb(t, 128), _rb(t, 128)],
        out_specs=[_rb(t, Q_LORA), _rb(t, KV_LORA), _rb(t, 128)],
        name="attn_prep", compiler_params=_params(("parallel",)))(proj, proj, proj, g_q, g_kv, cos_t, sin_t)


def _qk_pack(q, kv, krr, cos_t, sin_t):
    lp = q.shape[0]
    t = ROW_BLOCK

    def body(q_ref, kv_ref, krr_ref, cos_ref, sin_ref, qs_ref, ks_ref, vs_ref, vts_ref):
        lane = lax.broadcasted_iota(jnp.int32, (t, 128), 1)
        lo = lane < 64
        krr = krr_ref[...].astype(BF16)
        for j in range(ATT_HEADS // 2):
            pr = _rope(q_ref[:, 1024 + 128 * j:1024 + 128 * (j + 1)], cos_ref[...], sin_ref[...])
            for h, keep in ((2 * j, lo), (2 * j + 1, jnp.logical_not(lo))):
                qs_ref[h, :, 0:128] = q_ref[:, 128 * h:128 * (h + 1)].astype(BF16)
                qs_ref[h, :, 128:256] = jnp.where(keep, pr, 0.0).astype(BF16)
        for h in range(ATT_HEADS):
            ks_ref[h, :, 0:128] = kv_ref[:, 256 * h:256 * h + 128].astype(BF16)
            ks_ref[h, :, 128:256] = krr
            v = kv_ref[:, 256 * h + 128:256 * (h + 1)]
            vs_ref[h] = v.astype(BF16)
            vts_ref[h] = v.T.astype(BF16)

    slab = lambda w: pl.BlockSpec((ATT_HEADS, t, w), lambda i: (0, i, 0))
    return pl.pallas_call(
        body,
        out_shape=[jax.ShapeDtypeStruct((ATT_HEADS, lp, QP_W), BF16), jax.ShapeDtypeStruct((ATT_HEADS, lp, QP_W), BF16),
                   jax.ShapeDtypeStruct((ATT_HEADS, lp, V_HEAD), BF16), jax.ShapeDtypeStruct((ATT_HEADS, V_HEAD, lp), BF16)],
        grid=(lp // t,),
        in_specs=[_rb(t, 1536), _rb(t, 2048), _rb(t, 128), _rb(t, 128), _rb(t, 128)],
        out_specs=[slab(QP_W), slab(QP_W), slab(V_HEAD), pl.BlockSpec((ATT_HEADS, V_HEAD, t), lambda i: (0, 0, i))],
        name="qk_pack", compiler_params=_params(("parallel",)))(q, kv, krr, cos_t, sin_t)


def _shifted(ext, t, shift):
    if shift == 0:
        return ext[8:, :]
    return pltpu.roll(ext, shift, 0)[8:, :]


def _conv_fwd(name, proj, seg, width, conv_w, conv_b):
    lp = proj.shape[0]
    t = ROW_BLOCK
    cb = seg // width

    def body(u_ref, halo_ref, w_ref, b_ref, pre_ref, act_ref):
        i = pl.program_id(0)
        u = u_ref[...]
        halo = jnp.where(i > 0, halo_ref[...], 0.0)
        ext = jnp.concatenate([halo, u], axis=0)
        pre = jnp.broadcast_to(b_ref[...], (t, width))
        for k in range(CONV_K):
            pre = pre + w_ref[k:k + 1, :] * _shifted(ext, t, CONV_K - 1 - k)
        pre_ref[...] = pre
        act_ref[...] = pre * _sigmoid(pre)

    return pl.pallas_call(
        body,
        out_shape=[jax.ShapeDtypeStruct((lp, width), F32)] * 2,
        grid=(lp // t,),
        in_specs=[_rb(t, width, cb),
                  pl.BlockSpec((8, width), lambda i: (jnp.maximum(i * (t // 8) - 1, 0), cb)),
                  _full((CONV_K, width)), _full((1, width))],
        out_specs=[_rb(t, width), _rb(t, width)],
        name=name, compiler_params=_params(("parallel",)))(proj, proj, conv_w, conv_b)


def _softplus(x):
    return jnp.maximum(x, 0.0) + jnp.log1p(jnp.exp(-jnp.abs(x)))


def _dt_fwd(proj, dt_bias_pad):
    lp = proj.shape[0]
    t = ROW_BLOCK

    def body(x_ref, b_ref, o_ref):
        o_ref[...] = _softplus(x_ref[...] + b_ref[...])

    return pl.pallas_call(
        body, out_shape=jax.ShapeDtypeStruct((lp, 128), F32), grid=(lp // t,),
        in_specs=[_rb(t, 128, SEG_DT // 128), _full((1, 128))], out_specs=_rb(t, 128),
        name="dt_fwd", compiler_params=_params(("parallel",)))(proj, dt_bias_pad)


def _gated_norm_group(y, z, w):
    g = y * (z * _sigmoid(z))
    return g * lax.rsqrt(jnp.mean(g * g, axis=-1, keepdims=True) + EPS) * w


def _gated_norm_fwd(y, proj, w):
    lp = y.shape[0]
    t = ROW_BLOCK
    gw = SSM_WIDTH // 2

    def body(y0, y1, z0, z1, w0, w1, o_ref):
        o_ref[:, 0:gw] = _gated_norm_group(y0[...], z0[...], w0[...]).astype(BF16)
        o_ref[:, gw:] = _gated_norm_group(y1[...], z1[...], w1[...]).astype(BF16)

    zb = SEG_Z // gw
    return pl.pallas_call(
        body, out_shape=jax.ShapeDtypeStruct((lp, SSM_WIDTH), BF16), grid=(lp // t,),
        in_specs=[_rb(t, gw, 0), _rb(t, gw, 1), _rb(t, gw, zb), _rb(t, gw, zb + 1),
                  pl.BlockSpec((1, gw), lambda i: (0, 0)), pl.BlockSpec((1, gw), lambda i: (0, 1))],
        out_specs=_rb(t, SSM_WIDTH),
        name="gated_norm_fwd", compiler_params=_params(("parallel",)))(y, y, proj, proj, w, w)


def _gated_norm_bwd(y, proj, w, dssm):
    lp = y.shape[0]
    t = ROW_BLOCK
    gw = SSM_WIDTH // 2

    def body(y0, y1, z0, z1, w0, w1, d0, d1, dy_ref, dz_ref, dw_ref):
        dws = []
        for g, (yr, zr, wr, dr) in enumerate(((y0, z0, w0, d0), (y1, z1, w1, d1))):
            _, vjp = jax.vjp(_gated_norm_group, yr[...], zr[...], wr[...])
            dyg, dzg, dwg = vjp(dr[...])
            dy_ref[:, g * gw:(g + 1) * gw] = dyg
            dz_ref[:, g * gw:(g + 1) * gw] = dzg
            dws.append(dwg)
        _acc_add(dw_ref, jnp.concatenate(dws, axis=1))

    zb = SEG_Z // gw
    return pl.pallas_call(
        body,
        out_shape=[jax.ShapeDtypeStruct((lp, SSM_WIDTH), F32), jax.ShapeDtypeStruct((lp, SSM_WIDTH), F32),
                   jax.ShapeDtypeStruct((1, SSM_WIDTH), F32)],
        grid=(lp // t,),
        in_specs=[_rb(t, gw, 0), _rb(t, gw, 1), _rb(t, gw, zb), _rb(t, gw, zb + 1),
                  pl.BlockSpec((1, gw), lambda i: (0, 0)), pl.BlockSpec((1, gw), lambda i: (0, 1)),
                  _rb(t, gw, 2), _rb(t, gw, 3)],
        out_specs=[_rb(t, SSM_WIDTH), _rb(t, SSM_WIDTH), _full((1, SSM_WIDTH))],
        name="gated_norm_bwd", compiler_params=_params(("arbitrary",)))(y, y, proj, proj, w, w, dssm, dssm)


def _mix_residual(h0, mix, g_post, g_mlp_pre):
    lp = h0.shape[0]
    t = ROW_BLOCK

    def body(h_ref, m_ref, gp_ref, gm_ref, h1_ref, n2_ref):
        h1 = h_ref[...] + _rms(m_ref[...], gp_ref[...])
        h1_ref[...] = h1
        n2_ref[...] = _rms(h1, gm_ref[...]).astype(BF16)

    return pl.pallas_call(
        body, out_shape=[jax.ShapeDtypeStruct((lp, D_MODEL), F32), jax.ShapeDtypeStruct((lp, D_MODEL), BF16)],
        grid=(lp // t,),
        in_specs=[_rb(t, D_MODEL), _rb(t, D_MODEL), _full((1, D_MODEL)), _full((1, D_MODEL))],
        out_specs=[_rb(t, D_MODEL), _rb(t, D_MODEL)],
        name="mix_residual", compiler_params=_params(("parallel",)))(h0, mix, g_post, g_mlp_pre)


def _loss_and_grad(h1, f, g_post, tgt, n_real):
    lp = h1.shape[0]
    t = ROW_BLOCK

    def body(h1_ref, f_ref, g_ref, t_ref, loss_ref, dh2_ref, df_ref, dg_ref):
        i = pl.program_id(0)
        fx = f_ref[...]
        h2 = h1_ref[...] + _rms(fx, g_ref[...])
        row = i * t + lax.broadcasted_iota(jnp.int32, (t, 1), 0)
        real = jnp.logical_and(row >= N_META, row < N_META + n_real)
        diff = jnp.where(real, h2 - t_ref[...], 0.0)
        part = 0.5 * jnp.sum(jnp.sum(diff * diff, axis=-1, keepdims=True) / D_MODEL, axis=0, keepdims=True)
        _acc_add(loss_ref, jnp.broadcast_to(part, (1, 128)))
        dh2 = diff / D_MODEL
        dh2_ref[...] = dh2
        dfx, dg = _rms_bwd(fx, g_ref[...], dh2)
        df_ref[...] = dfx.astype(BF16)
        _acc_add(dg_ref, dg)

    return pl.pallas_call(
        body,
        out_shape=[jax.ShapeDtypeStruct((1, 128), F32), jax.ShapeDtypeStruct((lp, D_MODEL), F32),
                   jax.ShapeDtypeStruct((lp, D_MODEL), BF16), jax.ShapeDtypeStruct((1, D_MODEL), F32)],
        grid=(lp // t,),
        in_specs=[_rb(t, D_MODEL), _rb(t, D_MODEL), _full((1, D_MODEL)), _rb(t, D_MODEL)],
        out_specs=[_full((1, 128)), _rb(t, D_MODEL), _rb(t, D_MODEL), _full((1, D_MODEL))],
        name="loss_and_grad", compiler_params=_params(("arbitrary",)))(h1, f, g_post, tgt)


def _mlp_residual_bwd(dh2, dn2, h1, g_mlp_pre, mix, g_post):
    lp = h1.shape[0]
    t = ROW_BLOCK

    def body(dh2_ref, dn2_ref, h1_ref, gm_ref, mix_ref, gp_ref, dh1_ref, dmix_ref, dgm_ref, dgp_ref):
        dx, dgm = _rms_bwd(h1_ref[...], gm_ref[...], dn2_ref[...])
        dh1 = dh2_ref[...] + dx
        dh1_ref[...] = dh1
        dmix, dgp = _rms_bwd(mix_ref[...], gp_ref[...], dh1)
        dmix_ref[...] = dmix.astype(BF16)
        _acc_add(dgm_ref, dgm)
        _acc_add(dgp_ref, dgp)

    return pl.pallas_call(
        body,
        out_shape=[jax.ShapeDtypeStruct((lp, D_MODEL), F32), jax.ShapeDtypeStruct((lp, D_MODEL), BF16),
                   jax.ShapeDtypeStruct((1, D_MODEL), F32), jax.ShapeDtypeStruct((1, D_MODEL), F32)],
        grid=(lp // t,),
        in_specs=[_rb(t, D_MODEL), _rb(t, D_MODEL), _rb(t, D_MODEL), _full((1, D_MODEL)), _rb(t, D_MODEL),
                  _full((1, D_MODEL))],
        out_specs=[_rb(t, D_MODEL), _rb(t, D_MODEL), _full((1, D_MODEL)), _full((1, D_MODEL))],
        name="mlp_residual_bwd", compiler_params=_params(("arbitrary",)))(dh2, dn2, h1, g_mlp_pre, mix, g_post)


def _input_norm_bwd(dh1, dn1, h0, g_pre):
    lp = h0.shape[0]
    t = ROW_BLOCK

    def body(dh1_ref, dn1_ref, h0_ref, g_ref, dh0_ref, dg_ref):
        dx, dg = _rms_bwd(h0_ref[...], g_ref[...], dn1_ref[...])
        dh0_ref[...] = dh1_ref[...] + dx
        _acc_add(dg_ref, dg)

    return pl.pallas_call(
        body, out_shape=[jax.ShapeDtypeStruct((lp, D_MODEL), F32), jax.ShapeDtypeStruct((1, D_MODEL), F32)],
        grid=(lp // t,),
        in_specs=[_rb(t, D_MODEL), _rb(t, D_MODEL), _rb(t, D_MODEL), _full((1, D_MODEL))],
        out_specs=[_rb(t, D_MODEL), _full((1, D_MODEL))],
        name="input_norm_bwd", compiler_params=_params(("arbitrary",)))(dh1, dn1, h0, g_pre)


def _conv_bwd(name, dact, pre, proj, seg, width, conv_w):
    lp = proj.shape[0]
    t = ROW_BLOCK
    cb = seg // width
    nblk = lp // t

    def dsilu(p):
        s = _sigmoid(p)
        return s * (1.0 + p * (1.0 - s))

    def body(da_ref, dan_ref, pre_ref, pren_ref, u_ref, halo_ref, w_ref, du_ref, dw_ref, db_ref):
        i = pl.program_id(0)
        dpre = da_ref[...] * dsilu(pre_ref[...])
        dpre_next = jnp.where(i < nblk - 1, dan_ref[...] * dsilu(pren_ref[...]), 0.0)
        extd = jnp.concatenate([dpre, dpre_next], axis=0)
        halo = jnp.where(i > 0, halo_ref[...], 0.0)
        ext = jnp.concatenate([halo, u_ref[...]], axis=0)
        du = jnp.zeros((t, width), F32)
        dws = []
        for k in range(CONV_K):
            m = CONV_K - 1 - k
            ahead = dpre if m == 0 else pltpu.roll(extd, t + 8 - m, 0)[:t, :]
            du = du + w_ref[k:k + 1, :] * ahead
            dws.append(jnp.sum(dpre * _shifted(ext, t, m), axis=0, keepdims=True))
        du_ref[...] = du
        _acc_add(dw_ref, jnp.concatenate(dws + [jnp.zeros((8 - CONV_K, width), F32)], axis=0))
        _acc_add(db_ref, jnp.sum(dpre, axis=0, keepdims=True))

    nxt = lambda i: (jnp.minimum((i + 1) * (t // 8), lp // 8 - 1), 0)
    return pl.pallas_call(
        body,
        out_shape=[jax.ShapeDtypeStruct((lp, width), F32), jax.ShapeDtypeStruct((8, width), F32),
                   jax.ShapeDtypeStruct((1, width), F32)],
        grid=(nblk,),
        in_specs=[_rb(t, width), pl.BlockSpec((8, width), nxt), _rb(t, width), pl.BlockSpec((8, width), nxt),
                  _rb(t, width, cb),
                  pl.BlockSpec((8, width), lambda i: (jnp.maximum(i * (t // 8) - 1, 0), cb)),
                  _full((CONV_K, width))],
        out_specs=[_rb(t, width), _full((8, width)), _full((1, width))],
        name=name, compiler_params=_params(("arbitrary",)))(dact, dact, pre, pre, proj, proj, conv_w)


def _qk_unpack_bwd(dqs, dks, dvs, cos_t, sin_t):
    lp = dqs.shape[1]
    t = ROW_BLOCK

    def body(dqs_ref, dks_ref, dvs_ref, cos_ref, sin_ref, dq_ref, dkv_ref, dkr_ref):
        lane = lax.broadcasted_iota(jnp.int32, (t, 128), 1)
        lo = lane < 64
        for j in range(ATT_HEADS // 2):
            dpr = jnp.where(lo, dqs_ref[2 * j, :, 128:256], dqs_ref[2 * j + 1, :, 128:256])
            dq_ref[:, 1024 + 128 * j:1024 + 128 * (j + 1)] = _rope_bwd(dpr, cos_ref[...], sin_ref[...]).astype(BF16)
        dkrr = jnp.zeros((t, 128), F32)
        for h in range(ATT_HEADS):
            dq_ref[:, 128 * h:128 * (h + 1)] = dqs_ref[h, :, 0:128].astype(BF16)
            dkv_ref[:, 256 * h:256 * h + 128] = dks_ref[h, :, 0:128].astype(BF16)
            dkv_ref[:, 256 * h + 128:256 * (h + 1)] = dvs_ref[h].astype(BF16)
            dkrr = dkrr + dks_ref[h, :, 128:256]
        droped = jnp.where(lo, dkrr + pltpu.roll(dkrr, 64, 1), 0.0)
        dkr_ref[...] = _rope_bwd(droped, cos_ref[...], sin_ref[...])

    slab = lambda w: pl.BlockSpec((ATT_HEADS, t, w), lambda i: (0, i, 0))
    return pl.pallas_call(
        body,
        out_shape=[jax.ShapeDtypeStruct((lp, 1536), BF16), jax.ShapeDtypeStruct((lp, 2048), BF16),
                   jax.ShapeDtypeStruct((lp, 128), F32)],
        grid=(lp // t,),
        in_specs=[slab(QP_W), slab(QP_W), slab(V_HEAD), _rb(t, 128), _rb(t, 128)],
        out_specs=[_rb(t, 1536), _rb(t, 2048), _rb(t, 128)],
        name="qk_unpack_bwd", compiler_params=_params(("parallel",)))(dqs, dks, dvs, cos_t, sin_t)


def _proj_grad(proj, dcqn, dckvn, g_q, g_kv, dkr, ddt_pad, dt_bias_pad, dz, dxs, dbc):
    lp = proj.shape[0]
    t = ROW_BLOCK

    def body(ckv_ref, cq_ref, pdt_ref, dcq_ref, dckv_ref, gq_ref, gkv_ref, dkr_ref, ddt_ref, b_ref, dz_ref, dxs_ref,
             dbc_ref, dp_ref, dgq_ref, dgkv_ref, db_ref):
        dckv, dgkv = _rms_bwd(ckv_ref[...], gkv_ref[...], dckv_ref[...])
        dcq, dgq = _rms_bwd(cq_ref[...], gq_ref[...], dcq_ref[...])
        ddt_raw = ddt_ref[...] * _sigmoid(pdt_ref[...] + b_ref[...])
        dp_ref[:, SEG_KV:SEG_KV + KV_LORA] = dckv.astype(BF16)
        dp_ref[:, SEG_KR:SEG_KR + 128] = dkr_ref[...].astype(BF16)
        dp_ref[:, SEG_CQ:SEG_CQ + Q_LORA] = dcq.astype(BF16)
        dp_ref[:, SEG_DT:SEG_DT + 128] = ddt_raw.astype(BF16)
        dp_ref[:, SEG_DT + 128:SEG_Z] = jnp.zeros((t, SEG_Z - SEG_DT - 128), BF16)
        dp_ref[:, SEG_Z:SEG_XS] = dz_ref[...].astype(BF16)
        dp_ref[:, SEG_XS:SEG_BC] = dxs_ref[...].astype(BF16)
        dp_ref[:, SEG_BC:PROJ_W] = dbc_ref[...].astype(BF16)
        _acc_add(dgq_ref, dgq)
        _acc_add(dgkv_ref, dgkv)
        _acc_add(db_ref, jnp.sum(ddt_raw, axis=0, keepdims=True))

    return pl.pallas_call(
        body,
        out_shape=[jax.ShapeDtypeStruct((lp, PROJ_W), BF16), jax.ShapeDtypeStruct((1, Q_LORA), F32),
                   jax.ShapeDtypeStruct((1, KV_LORA), F32), jax.ShapeDtypeStruct((1, 128), F32)],
        grid=(lp // t,),
        in_specs=[_rb(t, KV_LORA, SEG_KV // KV_LORA), _rb(t, Q_LORA, SEG_CQ // Q_LORA), _rb(t, 128, SEG_DT // 128),
                  _rb(t, Q_LORA), _rb(t, KV_LORA), _full((1, Q_LORA)), _full((1, KV_LORA)), _rb(t, 128), _rb(t, 128),
                  _full((1, 128)), _rb(t, SSM_WIDTH), _rb(t, SSM_WIDTH), _rb(t, 512)],
        out_specs=[_rb(t, PROJ_W), _full((1, Q_LORA)), _full((1, KV_LORA)), _full((1, 128))],
        name="proj_grad", compiler_params=_params(("arbitrary",)))(
            proj, proj, proj, dcqn, dckvn, g_q, g_kv, dkr, ddt_pad, dt_bias_pad, dz, dxs, dbc)


def _pair_tables(n):
    qmaj = [(i, j) for i in range(n) for j in range(i + 1)]
    kmaj = [(i, j) for j in range(n) for i in range(j, n)]
    to = lambda ps, c: jnp.asarray(np.array([p[c] for p in ps], np.int32))
    return (to(qmaj, 0), to(qmaj, 1)), (to(kmaj, 0), to(kmaj, 1))


def _att_block(lp, edge=ATT_BLOCK):
    return edge if lp % edge == 0 else MM_BLOCK


def _nt(a, b):
    return lax.dot_general(a, b, (((1,), (1,)), ((), ())), preferred_element_type=F32)


def _attn_fwd(qs, ks, vts):
    lp = qs.shape[1]
    t = _att_block(lp, ATT_BLOCK_FWD)
    n = lp // t
    (qi, kj), _ = _pair_tables(n)
    tc = t // ATT_SPLIT

    def body(qi_ref, kj_ref, q_ref, k_ref, vt_ref, o_ref, lse_ref, m_s, l_s, acc_s):
        p = pl.program_id(1)
        i, j = qi_ref[p], kj_ref[p]

        @pl.when(j == 0)
        def _():
            m_s[...] = jnp.full_like(m_s, NEG_BIG)
            l_s[...] = jnp.zeros_like(l_s)
            acc_s[...] = jnp.zeros_like(acc_s)

        def update(masked):
            m_all, l_all, acc_all = m_s[...], l_s[...], acc_s[...]
            m_out, l_out, acc_out = [], [], []
            qk = lambda c: _nt(k_ref[0], q_ref[0, c * tc:(c + 1) * tc, :])
            ahead = qk(0)
            for c in range(ATT_SPLIT):
                cols = slice(c * tc, (c + 1) * tc)
                sc = ahead * ATT_SCALE_LOG2
                if c + 1 < ATT_SPLIT:
                    ahead = qk(c + 1)
                if masked:
                    keep = (lax.broadcasted_iota(jnp.int32, (t, tc), 1) + c * tc
                            >= lax.broadcasted_iota(jnp.int32, (t, tc), 0))
                    sc = jnp.where(keep, sc, NEG_BIG)
                m_prev = m_all[:, cols]
                m_new = jnp.maximum(m_prev, jnp.max(sc, axis=0, keepdims=True))
                alpha = jnp.exp2(m_prev - m_new)
                pexp = jnp.exp2(sc - m_new)
                l_out.append(alpha * l_all[:, cols] + jnp.sum(pexp, axis=0, keepdims=True))
                acc_out.append(alpha * acc_all[:, cols] + jnp.dot(vt_ref[0], pexp.astype(BF16),
                                                                  preferred_element_type=F32))
                m_out.append(m_new)
            cat = lambda parts: parts[0] if len(parts) == 1 else jnp.concatenate(parts, axis=1)
            m_s[...], l_s[...], acc_s[...] = cat(m_out), cat(l_out), cat(acc_out)

        @pl.when(j < i)
        def _():
            update(False)

        @pl.when(j == i)
        def _():
            update(True)
            o_ref[...] = (acc_s[...] / l_s[...]).T
            lse_ref[0] = m_s[...] + jnp.log2(l_s[...])

    grid_spec = pltpu.PrefetchScalarGridSpec(
        num_scalar_prefetch=2, grid=(ATT_HEADS, int(qi.shape[0])),
        in_specs=[pl.BlockSpec((1, t, QP_W), lambda h, p, qi, kj: (h, qi[p], 0)),
                  pl.BlockSpec((1, t, QP_W), lambda h, p, qi, kj: (h, kj[p], 0)),
                  pl.BlockSpec((1, V_HEAD, t), lambda h, p, qi, kj: (h, 0, kj[p]))],
        out_specs=[pl.BlockSpec((t, V_HEAD), lambda h, p, qi, kj: (qi[p], h)),
                   pl.BlockSpec((1, 1, t), lambda h, p, qi, kj: (h, 0, qi[p]))],
        scratch_shapes=[pltpu.VMEM((1, t), F32), pltpu.VMEM((1, t), F32), pltpu.VMEM((V_HEAD, t), F32)])
    return pl.pallas_call(
        body, grid_spec=grid_spec,
        out_shape=[jax.ShapeDtypeStruct((lp, ATT_HEADS * V_HEAD), F32), jax.ShapeDtypeStruct((ATT_HEADS, 1, lp), F32)],
        name="attn_fwd", compiler_params=_params(("parallel", "arbitrary")))(qi, kj, qs, ks, vts)


def _attn_delta(datt, att):
    lp = att.shape[0]
    t = MM_BLOCK

    def body(do_ref, o_ref, d_ref):
        prod = do_ref[...] * o_ref[...]
        d_ref[0] = jnp.sum(prod.T, axis=0, keepdims=True)

    blk = pl.BlockSpec((t, V_HEAD), lambda h, i: (i, h))
    return pl.pallas_call(
        body, out_shape=jax.ShapeDtypeStruct((ATT_HEADS, 1, lp), F32), grid=(ATT_HEADS, lp // t),
        in_specs=[blk, blk], out_specs=pl.BlockSpec((1, 1, t), lambda h, i: (h, 0, i)),
        name="attn_delta", compiler_params=_params(("parallel", "parallel")))(datt, att)


def _attn_bwd(qs, ks, vs, datt16, lse2, delta):
    lp = qs.shape[1]
    t = _att_block(lp)
    n = lp // t
    _, (qi, kj) = _pair_tables(n)
    tc = t // ATT_SPLIT

    def body(qi_ref, kj_ref, k_ref, v_ref, q_ref, do_ref, lse_ref, dl_ref, dq_ref, dk_ref, dv_ref, dk_s, dv_s):
        p = pl.program_id(1)
        i, j = qi_ref[p], kj_ref[p]

        @pl.when(p == 0)
        def _():
            dq_ref[...] = jnp.zeros_like(dq_ref)

        @pl.when(i == j)
        def _():
            dk_s[...] = jnp.zeros_like(dk_s)
            dv_s[...] = jnp.zeros_like(dv_s)

        def step(masked):
            for c in range(ATT_SPLIT):
                cols = slice(c * tc, (c + 1) * tc)
                q = q_ref[0, cols, :]
                do = do_ref[cols, :]
                pt = jnp.exp2(_nt(k_ref[0], q) * ATT_SCALE_LOG2 - lse_ref[0, :, cols])
                if masked:
                    keep = (lax.broadcasted_iota(jnp.int32, (t, tc), 1) + c * tc
                            >= lax.broadcasted_iota(jnp.int32, (t, tc), 0))
                    pt = jnp.where(keep, pt, 0.0)
                dst = (pt * (_nt(v_ref[0], do) - dl_ref[0, :, cols]) * ATT_SCALE).astype(BF16)
                dv_s[...] += jnp.dot(pt.astype(BF16), do, preferred_element_type=F32)
                dk_s[...] += jnp.dot(dst, q, preferred_element_type=F32)
                rows = pl.ds(pl.multiple_of(i * t + c * tc, tc), tc)
                dq_ref[0, rows, :] += lax.dot_general(dst, k_ref[0], (((0,), (0,)), ((), ())),
                                                      preferred_element_type=F32)

        @pl.when(i > j)
        def _():
            step(False)

        @pl.when(i == j)
        def _():
            step(True)

        @pl.when(i == n - 1)
        def _():
            dk_ref[0] = dk_s[...]
            dv_ref[0] = dv_s[...]

    grid_spec = pltpu.PrefetchScalarGridSpec(
        num_scalar_prefetch=2, grid=(ATT_HEADS, int(qi.shape[0])),
        in_specs=[pl.BlockSpec((1, t, QP_W), lambda h, p, qi, kj: (h, kj[p], 0)),
                  pl.BlockSpec((1, t, V_HEAD), lambda h, p, qi, kj: (h, kj[p], 0)),
                  pl.BlockSpec((1, t, QP_W), lambda h, p, qi, kj: (h, qi[p], 0)),
                  pl.BlockSpec((t, V_HEAD), lambda h, p, qi, kj: (qi[p], h)),
                  pl.BlockSpec((1, 1, t), lambda h, p, qi, kj: (h, 0, qi[p])),
                  pl.BlockSpec((1, 1, t), lambda h, p, qi, kj: (h, 0, qi[p]))],
        out_specs=[pl.BlockSpec((1, lp, QP_W), lambda h, p, qi, kj: (h, 0, 0)),
                   pl.BlockSpec((1, t, QP_W), lambda h, p, qi, kj: (h, kj[p], 0)),
                   pl.BlockSpec((1, t, V_HEAD), lambda h, p, qi, kj: (h, kj[p], 0))],
        scratch_shapes=[pltpu.VMEM((t, QP_W), F32), pltpu.VMEM((t, V_HEAD), F32)])
    return pl.pallas_call(
        body, grid_spec=grid_spec,
        out_shape=[jax.ShapeDtypeStruct((ATT_HEADS, lp, QP_W), F32), jax.ShapeDtypeStruct((ATT_HEADS, lp, QP_W), F32),
                   jax.ShapeDtypeStruct((ATT_HEADS, lp, V_HEAD), F32)],
        name="attn_bwd", compiler_params=_params(("arbitrary", "arbitrary")))(
            qi, kj, ks, vs, qs, datt16, lse2, delta)


N_PAIRS = SSM_HEADS // 2
HI = lax.Precision.HIGHEST


def _ssd_chunk(xp, bs, cs, dt, dt_t, alr, alc, dsk, st):
    q = dt.shape[0]
    li = lax.broadcasted_iota(jnp.int32, (q, q), 0)
    si = lax.broadcasted_iota(jnp.int32, (q, q), 1)
    tri = (si <= li).astype(F32)
    tri_t = (li <= si).astype(F32)
    lo = lax.broadcasted_iota(jnp.int32, (1, 128), 1) < 64
    h_r = lax.broadcasted_iota(jnp.int32, (1, SSM_HEADS), 1)
    h_c = lax.broadcasted_iota(jnp.int32, (SSM_HEADS, 1), 0)
    a = dt * (-jnp.exp(alr))
    a_t = dt_t * (-jnp.exp(alc))
    acum = jnp.dot(tri, a, precision=HI, preferred_element_type=F32)
    acum_t = jnp.dot(a_t, tri_t, precision=HI, preferred_element_type=F32)
    last = (lax.broadcasted_iota(jnp.int32, (q, 1), 0) == q - 1).astype(F32)
    alast = jnp.sum(acum * last, axis=0, keepdims=True)
    e = jnp.exp(acum)
    rdt = jnp.exp(alast - acum) * dt
    e_last = jnp.exp(alast)

    def col(m, h):
        return jnp.sum(m * (h_r == h).astype(F32), axis=1, keepdims=True)

    def row(m, h):
        return jnp.sum(m * (h_c == h).astype(F32), axis=0, keepdims=True)

    def pair(m, ha):
        return jnp.where(lo, col(m, ha), col(m, ha + 1))

    ys, st_new = [], []
    for g in range(2):
        c_b = cs[g].astype(BF16)
        b_b = bs[g].astype(BF16)
        cb = _nt(c_b, b_b)
        for j in range(N_PAIRS // 2):
            p = (N_PAIRS // 2) * g + j
            ha = 2 * p
            x = xp[p]
            x_b = x.astype(BF16)

            def w_of(h):
                seg = col(acum, h) - row(acum_t, h)
                return (cb * jnp.exp(jnp.minimum(seg, 0.0)) * tri * row(dt_t, h)).astype(BF16)

            y_diag = jnp.where(lo, jnp.dot(w_of(ha), x_b, preferred_element_type=F32),
                               jnp.dot(w_of(ha + 1), x_b, preferred_element_type=F32))
            y_off = jnp.dot(c_b, st[p].astype(BF16), preferred_element_type=F32) * pair(e, ha)
            ys.append(y_diag + y_off + pair(dsk, ha) * x)
            xw = (x * pair(rdt, ha)).astype(BF16)
            st_new.append(st[p] * pair(e_last, ha)
                          + lax.dot_general(b_b, xw, (((0,), (0,)), ((), ())), preferred_element_type=F32))
    return ys, st_new


def _ssd_fwd(xs, bc, dt, dt_t, alr, alc, dsk):
    lp = xs.shape[0]
    q = SSD_CHUNK
    nc = lp // q

    def body(x_ref, b_ref, c_ref, dt_ref, dtt_ref, alr_ref, alc_ref, dsk_ref, y_ref, sp_ref, st_s):
        @pl.when(pl.program_id(0) == 0)
        def _():
            st_s[...] = jnp.zeros_like(st_s)

        sp_ref[0] = st_s[...]
        xp = [x_ref[:, 128 * p:128 * (p + 1)] for p in range(N_PAIRS)]
        bs = [b_ref[:, 0:128], b_ref[:, 128:256]]
        cs = [c_ref[:, 0:128], c_ref[:, 128:256]]
        ys, st_new = _ssd_chunk(xp, bs, cs, dt_ref[...], dtt_ref[...], alr_ref[...], alc_ref[...], dsk_ref[...],
                                [st_s[p] for p in range(N_PAIRS)])
        for p in range(N_PAIRS):
            y_ref[:, 128 * p:128 * (p + 1)] = ys[p]
            st_s[p] = st_new[p]

    return pl.pallas_call(
        body,
        out_shape=[jax.ShapeDtypeStruct((lp, SSM_WIDTH), F32), jax.ShapeDtypeStruct((nc, N_PAIRS, 128, 128), F32)],
        grid=(nc,),
        in_specs=[_rb(q, SSM_WIDTH), _rb(q, 256, 0), _rb(q, 256, 1), _rb(q, SSM_HEADS),
                  pl.BlockSpec((SSM_HEADS, q), lambda i: (0, i)),
                  _full((1, SSM_HEADS)), _full((SSM_HEADS, 1)), _full((1, SSM_HEADS))],
        out_specs=[_rb(q, SSM_WIDTH), pl.BlockSpec((1, N_PAIRS, 128, 128), lambda i: (i, 0, 0, 0))],
        scratch_shapes=[pltpu.VMEM((N_PAIRS, 128, 128), F32)],
        name="ssd_fwd", compiler_params=_params(("arbitrary",)))(xs, bc, bc, dt, dt_t, alr, alc, dsk)


def _ssd_bwd(xs, bc, dt, dt_t, alr, alc, dsk, sprev, dy):
    lp = xs.shape[0]
    q = SSD_CHUNK
    nc = lp // q

    def body(x_ref, b_ref, c_ref, dt_ref, dtt_ref, alr_ref, alc_ref, dsk_ref, sp_ref, dy_ref,
             dx_ref, dbc_ref, ddt_ref, ddtt_ref, dalr_ref, dalc_ref, ddsk_ref, ds_s):
        @pl.when(pl.program_id(0) == 0)
        def _():
            ds_s[...] = jnp.zeros_like(ds_s)

        xp = [x_ref[:, 128 * p:128 * (p + 1)] for p in range(N_PAIRS)]
        bs = [b_ref[:, 0:128], b_ref[:, 128:256]]
        cs = [c_ref[:, 0:128], c_ref[:, 128:256]]
        st = [sp_ref[0, p] for p in range(N_PAIRS)]
        _, vjp = jax.vjp(_ssd_chunk, xp, bs, cs, dt_ref[...], dtt_ref[...], alr_ref[...], alc_ref[...], dsk_ref[...],
                         st)
        dys = [dy_ref[:, 128 * p:128 * (p + 1)] for p in range(N_PAIRS)]
        dxp, dbs, dcs, ddt, ddtt, dalr, dalc, ddsk, dst = vjp((dys, [ds_s[p] for p in range(N_PAIRS)]))
        for p in range(N_PAIRS):
            dx_ref[:, 128 * p:128 * (p + 1)] = dxp[p]
            ds_s[p] = dst[p]
        for g in range(2):
            dbc_ref[:, 128 * g:128 * (g + 1)] = dbs[g]
            dbc_ref[:, 256 + 128 * g:256 + 128 * (g + 1)] = dcs[g]
        ddt_ref[...] = ddt
        ddtt_ref[...] = ddtt
        _acc_add(dalr_ref, dalr)
        _acc_add(dalc_ref, dalc)
        _acc_add(ddsk_ref, ddsk)

    rev = lambda width, cb=0: pl.BlockSpec((q, width), lambda i: (nc - 1 - i, cb))
    return pl.pallas_call(
        body,
        out_shape=[jax.ShapeDtypeStruct((lp, SSM_WIDTH), F32), jax.ShapeDtypeStruct((lp, 512), F32),
                   jax.ShapeDtypeStruct((lp, SSM_HEADS), F32), jax.ShapeDtypeStruct((SSM_HEADS, lp), F32),
                   jax.ShapeDtypeStruct((1, SSM_HEADS), F32), jax.ShapeDtypeStruct((SSM_HEADS, 1), F32),
                   jax.ShapeDtypeStruct((1, SSM_HEADS), F32)],
        grid=(nc,),
        in_specs=[rev(SSM_WIDTH), rev(256, 0), rev(256, 1), rev(SSM_HEADS),
                  pl.BlockSpec((SSM_HEADS, q), lambda i: (0, nc - 1 - i)),
                  _full((1, SSM_HEADS)), _full((SSM_HEADS, 1)), _full((1, SSM_HEADS)),
                  pl.BlockSpec((1, N_PAIRS, 128, 128), lambda i: (nc - 1 - i, 0, 0, 0)), rev(SSM_WIDTH)],
        out_specs=[rev(SSM_WIDTH), rev(512), rev(SSM_HEADS), pl.BlockSpec((SSM_HEADS, q), lambda i: (0, nc - 1 - i)),
                   _full((1, SSM_HEADS)), _full((SSM_HEADS, 1)), _full((1, SSM_HEADS))],
        scratch_shapes=[pltpu.VMEM((N_PAIRS, 128, 128), F32)],
        name="ssd_bwd", compiler_params=_params(("arbitrary",)))(xs, bc, bc, dt, dt_t, alr, alc, dsk, sprev, dy)


def _q_to_slab_order(w):
    hd = QK_NOPE + QK_ROPE
    nope = [w[:, h * hd:h * hd + QK_NOPE] for h in range(ATT_HEADS)]
    rope = [w[:, h * hd + QK_NOPE:(h + 1) * hd] for h in range(ATT_HEADS)]
    return jnp.concatenate(nope + rope, axis=1)


def _q_from_slab_order(wp):
    base = ATT_HEADS * QK_NOPE
    parts = []
    for h in range(ATT_HEADS):
        parts += [wp[:, QK_NOPE * h:QK_NOPE * (h + 1)], wp[:, base + QK_ROPE * h:base + QK_ROPE * (h + 1)]]
    return jnp.concatenate(parts, axis=1)


_IN_CQ, _IN_CKV, _IN_KR, _IN_Z, _IN_XS, _IN_BC, _IN_DT = (0, 384), (384, 640), (640, 704), (704, 1728), (1728, 2752), \
    (2752, 3264), (3264, 3280)


def _pack_w_in(w):
    z = lambda n: jnp.zeros((w.shape[0], n), w.dtype)
    s = lambda r: w[:, r[0]:r[1]]
    return jnp.concatenate([s(_IN_CKV), s(_IN_KR), z(64), s(_IN_CQ), s(_IN_DT), z(112), z(128), s(_IN_Z), s(_IN_XS),
                            s(_IN_BC)], axis=1)


def _unpack_w_in(wp):
    s = lambda off, n: wp[:, off:off + n]
    return jnp.concatenate([s(SEG_CQ, 384), s(SEG_KV, 256), s(SEG_KR, 64), s(SEG_Z, 1024), s(SEG_XS, 1024),
                            s(SEG_BC, 512), s(SEG_DT, 16)], axis=1)


def _rope_tables(lp):
    inv_freq = ROPE_THETA ** (-jnp.arange(0, QK_ROPE, 2, dtype=F32) / QK_ROPE)
    ang = jnp.arange(lp, dtype=F32)[:, None] * inv_freq[None, :]
    cos, sin = jnp.cos(ang), jnp.sin(ang)
    return jnp.tile(cos, (1, 4)), jnp.concatenate([-sin, sin, -sin, sin], axis=1)


def _local_step(x, tgt, w):
    n_real = x.shape[0]
    l = N_META + n_real
    lp = -(-l // MM_BLOCK) * MM_BLOCK
    pad = lambda a: jnp.concatenate([a, jnp.zeros((lp - l, D_MODEL), F32)], axis=0)
    h0 = pad(jnp.concatenate([w["meta_tokens"], x], axis=0))
    tgt_p = pad(jnp.concatenate([jnp.zeros((N_META, D_MODEL), F32), tgt], axis=0))
    cos_t, sin_t = _rope_tables(lp)

    w_in_p = _pack_w_in(w["w_in"])
    w_q_p = _q_to_slab_order(w["w_q_up"])
    w_kv, w_out, w_up, w_down = w["w_kv_up"], w["w_out"], w["w_mlp_up"], w["w_mlp_down"]
    conv_w, conv_b = w["conv_w"], w["conv_b"]
    dt_bias_pad = jnp.concatenate([w["dt_bias"], jnp.zeros((1, 128 - SSM_HEADS), F32)], axis=1)
    alr, dsk = w["a_log"], w["d_skip"]
    alc = alr.reshape(SSM_HEADS, 1)

    n1 = _norm_in(h0, w["norm_mix_pre"])
    proj = _mm("proj", n1, w_in_p)
    cqn, ckvn, krr = _attn_prep(proj, w["q_a_norm"], w["kv_a_norm"], cos_t, sin_t)
    q = _mm("q_up", cqn, w_q_p)
    kv = _mm("kv_up", ckvn, w_kv)
    qs, ks, vs, vts = _qk_pack(q, kv, krr, cos_t, sin_t)
    att, lse2 = _attn_fwd(qs, ks, vts)
    xs_pre, xs_act = _conv_fwd("conv_xs_fwd", proj, SEG_XS, SSM_WIDTH, conv_w[:, :SSM_WIDTH], conv_b[:, :SSM_WIDTH])
    bc_pre, bc_act = _conv_fwd("conv_bc_fwd", proj, SEG_BC, 512, conv_w[:, SSM_WIDTH:], conv_b[:, SSM_WIDTH:])
    dt = _dt_fwd(proj, dt_bias_pad)[:, :SSM_HEADS]
    dt_t = dt.T
    y, sprev = _ssd_fwd(xs_act, bc_act, dt, dt_t, alr, alc, dsk)
    ssm = _gated_norm_fwd(y, proj, w["ssm_norm"])
    cat = jnp.concatenate([att.astype(BF16), ssm], axis=1)
    mix = _mm("out_proj", cat, w_out)
    h1, n2 = _mix_residual(h0, mix, w["norm_mix_post"], w["norm_mlp_pre"])
    relu2 = lambda r: jnp.square(jnp.maximum(r, 0.0))
    u, act = _mm("mlp_up", n2, w_up, outs=((F32, None), (BF16, relu2)))
    f = _mm("mlp_down", act, w_down)
    loss, dh2, df, dg_mlp_post = _loss_and_grad(h1, f, w["norm_mlp_post"], tgt_p, n_real)

    g = {"norm_mlp_post": dg_mlp_post}
    g["w_mlp_down"] = _mm_tn("d_w_mlp_down", act, df)
    du = _mm("d_mlp_act", df, w_down, outs=((BF16, lambda r, ub: r * (2.0 * jnp.maximum(ub, 0.0))),), epi_ins=(u,),
             trans_b=True)
    g["w_mlp_up"] = _mm_tn("d_w_mlp_up", n2, du)
    dn2 = _mm("d_n2", du, w_up, trans_b=True)
    dh1, dmix, g["norm_mlp_pre"], g["norm_mix_post"] = _mlp_residual_bwd(dh2, dn2, h1, w["norm_mlp_pre"], mix,
                                                                         w["norm_mix_post"])
    g["w_out"] = _mm_tn("d_w_out", cat, dmix)
    dcat, dcat16 = _mm("d_cat", dmix, w_out, outs=((F32, None), (BF16, None)), trans_b=True)
    dy, dz, g["ssm_norm"] = _gated_norm_bwd(y, proj, w["ssm_norm"], dcat)
    dxs_act, dbc_act, ddt, ddt_t, dalr, dalc, g["d_skip"] = _ssd_bwd(xs_act, bc_act, dt, dt_t, alr, alc, dsk, sprev, dy)
    g["a_log"] = dalr + dalc.reshape(1, SSM_HEADS)
    dxs, dcw_xs, dcb_xs = _conv_bwd("conv_xs_bwd", dxs_act, xs_pre, proj, SEG_XS, SSM_WIDTH, conv_w[:, :SSM_WIDTH])
    dbc, dcw_bc, dcb_bc = _conv_bwd("conv_bc_bwd", dbc_act, bc_pre, proj, SEG_BC, 512, conv_w[:, SSM_WIDTH:])
    g["conv_w"] = jnp.concatenate([dcw_xs[:CONV_K], dcw_bc[:CONV_K]], axis=1)
    g["conv_b"] = jnp.concatenate([dcb_xs, dcb_bc], axis=1)
    ddt_pad = jnp.concatenate([ddt + ddt_t.T, jnp.zeros((lp, 128 - SSM_HEADS), F32)], axis=1)

    dqs, dks, dvs = _attn_bwd(qs, ks, vs, dcat16, lse2, _attn_delta(dcat, att))
    dq, dkv, dkr = _qk_unpack_bwd(dqs, dks, dvs, cos_t, sin_t)
    g["w_q_up"] = _q_from_slab_order(_mm_tn("d_w_q_up", cqn, dq))
    g["w_kv_up"] = _mm_tn("d_w_kv_up", ckvn, dkv)
    dcqn = _mm("d_cqn", dq, w_q_p, trans_b=True)
    dckvn = _mm("d_ckvn", dkv, w_kv, trans_b=True)
    dproj, g["q_a_norm"], g["kv_a_norm"], ddtb = _proj_grad(proj, dcqn, dckvn, w["q_a_norm"], w["kv_a_norm"], dkr,
                                                          ddt_pad, dt_bias_pad, dz, dxs, dbc)
    g["dt_bias"] = ddtb[:, :SSM_HEADS]
    g["w_in"] = _unpack_w_in(_mm_tn("d_w_in", n1, dproj))
    dn1 = _mm("d_n1", dproj, w_in_p, trans_b=True)
    dh0, g["norm_mix_pre"] = _input_norm_bwd(dh1, dn1, h0, w["norm_mix_pre"])
    g["meta_tokens"] = dh0[:N_META]
    return loss, dh0, g


WEIGHTS = ["meta_tokens", "norm_mix_pre", "w_in", "q_a_norm", "w_q_up", "kv_a_norm", "w_kv_up", "conv_w", "conv_b",
           "dt_bias", "a_log", "d_skip", "ssm_norm", "w_out", "norm_mix_post", "norm_mlp_pre", "w_mlp_up",
           "w_mlp_down", "norm_mlp_post"]
SHARD_AXIS = {"meta_tokens": 1, "w_in": 1, "w_q_up": 1, "w_kv_up": 1, "conv_w": 1, "w_out": 0, "w_mlp_up": 1,
              "w_mlp_down": 0}
FULL_SHAPE = {"meta_tokens": (16, 1024), "norm_mix_pre": (1, 1024), "w_in": (1024, 3280), "q_a_norm": (1, 384),
              "w_q_up": (384, 1536), "kv_a_norm": (1, 256), "w_kv_up": (256, 2048), "conv_w": (4, 1536),
              "conv_b": (1, 1536), "dt_bias": (1, 16), "a_log": (1, 16), "d_skip": (1, 16), "ssm_norm": (1, 1024),
              "w_out": (2048, 1024), "norm_mix_post": (1, 1024), "norm_mlp_pre": (1, 1024), "w_mlp_up": (1024, 4096),
              "w_mlp_down": (4096, 1024), "norm_mlp_post": (1, 1024)}
GATHER_BF16 = ["w_in", "w_q_up", "w_kv_up", "w_out", "w_mlp_up", "w_mlp_down"]
GATHER_F32 = ["meta_tokens", "conv_w"]


def _shard_shape(name):
    shp = list(FULL_SHAPE[name])
    if name in SHARD_AXIS:
        shp[SHARD_AXIS[name]] //= N_CHIPS
    return tuple(shp)


PACK_ORDER = sorted(WEIGHTS, key=lambda n: -_shard_shape(n)[0])


def _packed_rows(shape):
    r, c = shape
    return r if c <= PACK_W else -(-c // PACK_W)


def _pack_rows(arrays, row_multiple):
    parts = []
    for a in arrays:
        r, c = a.shape
        if c > PACK_W:
            assert r == 1, a.shape
            folded = _packed_rows(a.shape)
            a = jnp.pad(a, ((0, 0), (0, folded * PACK_W - c))).reshape(folded, PACK_W)
        elif c < PACK_W:
            a = jnp.pad(a, ((0, 0), (0, PACK_W - c)))
        parts.append(a)
    rows = sum(p.shape[0] for p in parts)
    if rows % row_multiple:
        parts.append(jnp.zeros((row_multiple - rows % row_multiple, PACK_W), parts[0].dtype))
    return jnp.concatenate(parts, axis=0)


def _unpack_rows(packed, shapes):
    out, off = [], 0
    for r, c in shapes:
        nr = _packed_rows((r, c))
        blk = packed[off:off + nr]
        out.append(blk[:, :c] if c <= PACK_W else blk.reshape(1, nr * PACK_W)[:, :c])
        off += nr
    return out


def _chip_slice(full, name, t):
    if name not in SHARD_AXIS:
        return full
    ax = SHARD_AXIS[name]
    n = FULL_SHAPE[name][ax] // N_CHIPS
    return lax.slice_in_dim(full, t * n, (t + 1) * n, axis=ax)


HBM_SPEC = pl.BlockSpec(memory_space=pl.ANY)
CHIP_FLIPS = ((1, 0), (0, 1), (1, 1))


def _gather_chips(bufs):
    nb = len(bufs)

    def body(*refs):
        ins, outs = refs[:nb], refs[nb:2 * nb]
        send, recv, loc = refs[2 * nb:]
        x, y, c = lax.axis_index("x"), lax.axis_index("y"), lax.axis_index("c")
        me = 2 * x + y
        sibling = (x, y, 1 - c)
        sends, forwards = [], []
        for b in range(nb):
            half = bufs[b].shape[0] // 2
            mine = pl.ds(c * half, half)
            own = pltpu.make_async_copy(ins[b], outs[b].at[me], loc.at[b])
            own.start()
            sends.append(own)
            for k, (fx, fy) in enumerate(CHIP_FLIPS):
                cp = pltpu.make_async_remote_copy(
                    src_ref=ins[b].at[mine], dst_ref=outs[b].at[me, mine], send_sem=send.at[b, k],
                    recv_sem=recv.at[b, k], device_id=(x ^ fx, y ^ fy, c), device_id_type=MESH_ID)
                cp.start()
                sends.append(cp)
        for b in range(nb):
            half = bufs[b].shape[0] // 2
            mine, theirs = pl.ds(c * half, half), pl.ds((1 - c) * half, half)
            for k, (fx, fy) in enumerate(CHIP_FLIPS):
                chip = 2 * (x ^ fx) + (y ^ fy)
                landed = outs[b].at[chip, mine]
                pltpu.make_async_remote_copy(src_ref=landed, dst_ref=landed, send_sem=send.at[b, k],
                                             recv_sem=recv.at[b, k], device_id=sibling,
                                             device_id_type=MESH_ID).wait_recv()
                fw = pltpu.make_async_remote_copy(src_ref=landed, dst_ref=landed, send_sem=send.at[b, 3 + k],
                                                  recv_sem=recv.at[b, 3 + k], device_id=sibling,
                                                  device_id_type=MESH_ID)
                fw.start()
                forwards.append((fw, outs[b].at[chip, theirs], b, k))
        for fw, arriving, b, k in forwards:
            pltpu.make_async_remote_copy(src_ref=arriving, dst_ref=arriving, send_sem=send.at[b, 3 + k],
                                         recv_sem=recv.at[b, 3 + k], device_id=sibling,
                                         device_id_type=MESH_ID).wait_recv()
            fw.wait_send()
        for cp in sends[1::4] + sends[2::4] + sends[3::4]:
            cp.wait_send()
        for own in sends[0::4]:
            own.wait()

    return pl.pallas_call(
        body,
        out_shape=[jax.ShapeDtypeStruct((N_CHIPS,) + b.shape, b.dtype) for b in bufs],
        in_specs=[HBM_SPEC] * nb, out_specs=[HBM_SPEC] * nb,
        scratch_shapes=[pltpu.SemaphoreType.DMA((nb, 6)), pltpu.SemaphoreType.DMA((nb, 6)),
                        pltpu.SemaphoreType.DMA((nb,))],
        name="gather_chips")(*bufs)


def _sibling_swap(name, buf):
    def body(src, dst, send, recv):
        x, y, c = lax.axis_index("x"), lax.axis_index("y"), lax.axis_index("c")
        cp = pltpu.make_async_remote_copy(src_ref=src, dst_ref=dst, send_sem=send, recv_sem=recv,
                                          device_id=(x, y, 1 - c), device_id_type=MESH_ID)
        cp.start()
        cp.wait()

    return pl.pallas_call(
        body, out_shape=jax.ShapeDtypeStruct(buf.shape, buf.dtype), in_specs=[HBM_SPEC], out_specs=HBM_SPEC,
        scratch_shapes=[pltpu.SemaphoreType.DMA, pltpu.SemaphoreType.DMA], name=name)(buf)


def _scatter_chips(part):
    def body(src, dst, send, recv):
        x, y, c = lax.axis_index("x"), lax.axis_index("y"), lax.axis_index("c")
        copies = []
        for k, (fx, fy) in enumerate(CHIP_FLIPS):
            tx, ty = x ^ fx, y ^ fy
            cp = pltpu.make_async_remote_copy(
                src_ref=src.at[2 * tx + ty], dst_ref=dst.at[k], send_sem=send.at[k], recv_sem=recv.at[k],
                device_id=(tx, ty, c), device_id_type=MESH_ID)
            cp.start()
            copies.append(cp)
        for cp in copies:
            cp.wait()

    return pl.pallas_call(
        body, out_shape=jax.ShapeDtypeStruct((3,) + part.shape[1:], part.dtype), in_specs=[HBM_SPEC],
        out_specs=HBM_SPEC, scratch_shapes=[pltpu.SemaphoreType.DMA((3,)), pltpu.SemaphoreType.DMA((3,))],
        name="scatter_chips")(part)


def _add_rows(name, terms):
    rows = terms[0].shape[0]
    t = _row_tile(rows)

    def body(*refs):
        acc = refs[0][...]
        for r in refs[1:-1]:
            acc = acc + r[...]
        refs[-1][...] = acc

    return pl.pallas_call(
        body, out_shape=jax.ShapeDtypeStruct(terms[0].shape, F32), grid=(rows // t,),
        in_specs=[_rb(t, PACK_W)] * len(terms), out_specs=_rb(t, PACK_W),
        name=name, compiler_params=_params(("parallel",)))(*terms)


def _row_tile(rows):
    best = 8
    for t in range(8, 513, 8):
        if rows % t == 0:
            best = t
    return best


def _adamw(g, w, m, v):
    rows = g.shape[0]
    t = _row_tile(rows)
    c1 = 1.0 - ADAM_B1 ** ADAM_STEP
    c2 = 1.0 - ADAM_B2 ** ADAM_STEP

    def body(g_ref, w_ref, m_ref, v_ref, d_ref, mo_ref, vo_ref):
        gg = g_ref[...]
        mn = ADAM_B1 * m_ref[...] + (1.0 - ADAM_B1) * gg
        vn = ADAM_B2 * v_ref[...] + (1.0 - ADAM_B2) * (gg * gg)
        d_ref[...] = -ADAM_LR * ((mn / c1) / (jnp.sqrt(vn / c2) + ADAM_EPS) + ADAM_WD * w_ref[...])
        mo_ref[...] = mn
        vo_ref[...] = vn

    return pl.pallas_call(
        body, out_shape=[jax.ShapeDtypeStruct(g.shape, F32)] * 3, grid=(rows // t,),
        in_specs=[_rb(t, PACK_W)] * 4, out_specs=[_rb(t, PACK_W)] * 3,
        name="adamw", compiler_params=_params(("parallel",)))(g, w, m, v)


def kernel(x, meta_tokens, norm_mix_pre, w_in, q_a_norm, w_q_up, kv_a_norm, w_kv_up, conv_w, conv_b, dt_bias, a_log, d_skip, ssm_norm, w_out, norm_mix_post, norm_mlp_pre, w_mlp_up, w_mlp_down, norm_mlp_post, loss_target, m_meta_tokens, m_norm_mix_pre, m_w_in, m_q_a_norm, m_w_q_up, m_kv_a_norm, m_w_kv_up, m_conv_w, m_conv_b, m_dt_bias, m_a_log, m_d_skip, m_ssm_norm, m_w_out, m_norm_mix_post, m_norm_mlp_pre, m_w_mlp_up, m_w_mlp_down, m_norm_mlp_post, v_meta_tokens, v_norm_mix_pre, v_w_in, v_q_a_norm, v_w_q_up, v_kv_a_norm, v_w_kv_up, v_conv_w, v_conv_b, v_dt_bias, v_a_log, v_d_skip, v_ssm_norm, v_w_out, v_norm_mix_post, v_norm_mlp_pre, v_w_mlp_up, v_w_mlp_down, v_norm_mlp_post):
    given = dict(locals())
    drop = lambda name, a: a[0] if a.ndim == 3 else a
    w_loc = {n: drop(n, given[n]) for n in WEIGHTS}
    m_loc = {n: drop(n, given["m_" + n]) for n in WEIGHTS}
    v_loc = {n: drop(n, given["v_" + n]) for n in WEIGHTS}
    ix, iy, ic = lax.axis_index("x"), lax.axis_index("y"), lax.axis_index("c")
    me = 2 * ix + iy

    sent16 = _pack_rows([w_loc[n].astype(BF16) for n in GATHER_BF16], 32)
    sent32 = _pack_rows([w_loc[n] for n in GATHER_F32], 16)
    got16, got32 = _gather_chips([sent16, sent32])
    w_full = {n: w_loc[n] for n in WEIGHTS if n not in SHARD_AXIS}
    for names, got in ((GATHER_BF16, got16), (GATHER_F32, got32)):
        per_chip = [_unpack_rows(got[t], [_shard_shape(n) for n in names]) for t in range(N_CHIPS)]
        for k, n in enumerate(names):
            w_full[n] = jnp.concatenate([per_chip[t][k] for t in range(N_CHIPS)], axis=SHARD_AXIS[n])

    loss, dh0, g_full = _local_step(x[0], loss_target[0], w_full)
    n_real = x.shape[1]
    grad_x = dh0[N_META:N_META + n_real][None]

    shapes = [_shard_shape(n) for n in PACK_ORDER]
    slots = [_pack_rows([_chip_slice(g_full[n], n, t) for n in PACK_ORDER], 16) for t in range(N_CHIPS)]
    rows = slots[0].shape[0]
    half = rows // 2
    halves = lambda hh: jnp.concatenate([lax.dynamic_slice_in_dim(s, hh * half, half, axis=0) for s in slots], axis=0)
    keep, give = halves(ic), halves(1 - ic)
    from_sibling = _sibling_swap("sibling_swap", give)
    chip_part = _add_rows("chip_partial", [keep, from_sibling]).reshape(N_CHIPS, half, PACK_W)
    from_chips = _scatter_chips(chip_part)
    own = lax.dynamic_index_in_dim(chip_part, me, axis=0, keepdims=False)
    rel_of_xor = {2: 0, 1: 1, 3: 2}
    terms = []
    for t in range(N_CHIPS):
        cand = own
        for xr, k in rel_of_xor.items():
            cand = jnp.where((me ^ t) == xr, from_chips[k], cand)
        terms.append(cand)
    my_half = _add_rows("chip_total", terms)
    other_half = _sibling_swap("sibling_gather", my_half)
    g_red = jnp.where(ic == 0, jnp.concatenate([my_half, other_half], axis=0),
                      jnp.concatenate([other_half, my_half], axis=0))

    pack_loc = lambda d: _pack_rows([d[n] for n in PACK_ORDER], 16)
    delta, new_m, new_v = _adamw(g_red, pack_loc(w_loc), pack_loc(m_loc), pack_loc(v_loc))

    def outputs(packed_arr):
        parts = dict(zip(PACK_ORDER, _unpack_rows(packed_arr, shapes)))
        return [parts[n][None] if given[n].ndim == 3 else parts[n] for n in WEIGHTS]

    total = lax.psum(loss[0, 0], ("x", "y", "c"))
    return (total, grad_x, *outputs(g_red), *outputs(delta), *outputs(new_m), *outputs(new_v))
```

```python
import functools

import numpy as np
import jax
import jax.numpy as jnp
from jax import lax
from jax.experimental import pallas as pl
from jax.experimental.pallas import tpu as pltpu

F32 = jnp.float32
BF16 = jnp.bfloat16

D_MODEL = 1024
N_META = 16
EPS = 1e-6
ATT_HEADS = 8
Q_LORA = 384
KV_LORA = 256
QK_NOPE = 128
QK_ROPE = 64
V_HEAD = 128
ROPE_THETA = 10000.0
SSM_HEADS = 16
SSM_HEAD_DIM = 64
SSM_WIDTH = 1024
SSM_STATE = 128
CONV_K = 4
D_FF = 4096
ATT_SCALE = float((QK_NOPE + QK_ROPE) ** -0.5)
ATT_SCALE_LOG2 = float(ATT_SCALE * np.log2(np.e))

ADAM_LR = 0.001
ADAM_B1 = 0.9
ADAM_B2 = 0.999
ADAM_EPS = 1e-08
ADAM_WD = 0.01
ADAM_STEP = 10

SEG_KV, SEG_KR, SEG_CQ, SEG_DT, SEG_Z, SEG_XS, SEG_BC = 0, 256, 384, 768, 1024, 2048, 3072
PROJ_W = 3584
QP_W = 256

ROW_BLOCK = 512
MM_BLOCK = 512
SSD_CHUNK = 256
ATT_SPLIT = 1
ATT_BLOCK = 768
ATT_BLOCK_FWD = 1536
ATT_BLOCK_Q_BWD = 1536
VMEM_LIMIT = 56 * 1024 * 1024
NEG_BIG = -1e30

PACK_W = 1024
TAIL_ROWS = 16
N_CHIPS = 4
MESH_ID = pl.DeviceIdType.MESH


def _params(sem):
    return pltpu.CompilerParams(dimension_semantics=sem, vmem_limit_bytes=VMEM_LIMIT)


def _rb(rows, width, cb=0):
    return pl.BlockSpec((rows, width), lambda i: (i, cb))


def _full(shape):
    zeros = (0,) * len(shape)
    return pl.BlockSpec(shape, lambda i: zeros)


def _acc_add(ref, val):
    first = pl.program_id(0) == 0

    @pl.when(first)
    def _():
        ref[...] = val

    @pl.when(jnp.logical_not(first))
    def _():
        ref[...] += val


def _rms(x, g):
    r = lax.rsqrt(jnp.mean(x * x, axis=-1, keepdims=True) + EPS)
    return x * r * g


def _rms_bwd(x, g, dy):
    r = lax.rsqrt(jnp.mean(x * x, axis=-1, keepdims=True) + EPS)
    dyg = dy * g
    dx = r * dyg - x * (r * r * r) * jnp.mean(x * dyg, axis=-1, keepdims=True)
    dg = jnp.sum(dy * x * r, axis=0, keepdims=True)
    return dx, dg


def _sigmoid(x):
    return 1.0 / (1.0 + jnp.exp(-x))


def _swap32(x):
    lane = lax.broadcasted_iota(jnp.int32, x.shape, 1)
    return jnp.where((lane % 64) < 32, pltpu.roll(x, 96, 1), pltpu.roll(x, 32, 1))


def _rope(x, cos_t, sin_t):
    return x * cos_t + _swap32(x) * sin_t


def _rope_bwd(dr, cos_t, sin_t):
    return dr * cos_t + _swap32(dr * sin_t)


def _tile(n, cap):
    if n <= cap:
        return n
    best = 128
    for t in range(128, cap + 1, 128):
        if n % t == 0:
            best = t
    assert n % best == 0, (n, cap)
    return best


MM_VMEM_BUDGET = 40 * 1024 * 1024
TN_ROWS_CAP = 1536


def _mm(name, a, b, outs=((F32, None),), epi_ins=(), trans_b=False):
    m, k = a.shape
    n = b.shape[0] if trans_b else b.shape[1]
    tm = MM_BLOCK
    n_epi = len(epi_ins)
    out_bytes = sum(jnp.dtype(dt).itemsize for dt, _ in outs) + sum(e.dtype.itemsize for e in epi_ins)
    step_bytes = lambda tn: 2 * (tm * k * a.dtype.itemsize + k * tn * b.dtype.itemsize + tm * tn * out_bytes)
    tn = n
    while step_bytes(tn) > MM_VMEM_BUDGET and tn % 256 == 0:
        tn //= 2
    assert n % tn == 0 and step_bytes(tn) <= MM_VMEM_BUDGET, (name, n, tn)

    def body(a_ref, b_ref, *rest):
        epi_refs = rest[:n_epi]
        out_refs = rest[n_epi:]
        lhs, rhs = a_ref[...].astype(BF16), b_ref[...].astype(BF16)
        r = _nt(lhs, rhs) if trans_b else jnp.dot(lhs, rhs, preferred_element_type=F32)
        blocks = [e[...] for e in epi_refs]
        for o_ref, (dt, fn) in zip(out_refs, outs):
            o_ref[...] = (r if fn is None else fn(r, *blocks)).astype(dt)

    out_spec = pl.BlockSpec((tm, tn), lambda j, i: (i, j))
    b_spec = pl.BlockSpec((tn, k), lambda j, i: (j, 0)) if trans_b else pl.BlockSpec((k, tn), lambda j, i: (0, j))
    res = pl.pallas_call(
        body,
        out_shape=[jax.ShapeDtypeStruct((m, n), dt) for dt, _ in outs],
        grid=(n // tn, m // tm),
        in_specs=[pl.BlockSpec((tm, k), lambda j, i: (i, 0)), b_spec] + [out_spec] * n_epi,
        out_specs=[out_spec] * len(outs),
        name=name,
        compiler_params=_params(("parallel", "parallel")),
    )(a, b, *epi_ins)
    return res[0] if len(outs) == 1 else res


def _mm_tn(name, x, dy, ta_cap=1024, tn_cap=1024):
    l, a = x.shape
    n = dy.shape[1]
    ta, tn = _tile(a, ta_cap), _tile(n, tn_cap)
    tl = max(t for t in range(MM_BLOCK, TN_ROWS_CAP + 1, MM_BLOCK) if l % t == 0)
    nl = l // tl

    def body(x_ref, dy_ref, o_ref):
        ll = pl.program_id(2)

        @pl.when(ll == 0)
        def _():
            o_ref[...] = jnp.zeros_like(o_ref)

        o_ref[...] += lax.dot_general(x_ref[...].astype(BF16), dy_ref[...].astype(BF16), (((0,), (0,)), ((), ())),
                                      preferred_element_type=F32)

    return pl.pallas_call(
        body,
        out_shape=jax.ShapeDtypeStruct((a, n), F32),
        grid=(a // ta, n // tn, nl),
        in_specs=[pl.BlockSpec((tl, ta), lambda i, j, ll: (ll, i)), pl.BlockSpec((tl, tn), lambda i, j, ll: (ll, j))],
        out_specs=pl.BlockSpec((ta, tn), lambda i, j, ll: (i, j)),
        name=name,
        compiler_params=_params(("parallel", "parallel", "arbitrary")),
    )(x, dy)


def _norm_in(h0, g_pre):
    lp = h0.shape[0]
    t = ROW_BLOCK

    def body(h_ref, g_ref, o_ref):
        o_ref[...] = _rms(h_ref[...], g_ref[...]).astype(BF16)

    return pl.pallas_call(
        body, out_shape=jax.ShapeDtypeStruct((lp, D_MODEL), BF16), grid=(lp // t,),
        in_specs=[_rb(t, D_MODEL), _full((1, D_MODEL))], out_specs=_rb(t, D_MODEL),
        name="norm_in", compiler_params=_params(("parallel",)))(h0, g_pre)


def _attn_prep(proj, g_q, g_kv, cos_t, sin_t):
    lp = proj.shape[0]
    t = ROW_BLOCK

    def body(ckv_ref, kr_ref, cq_ref, gq_ref, gkv_ref, cos_ref, sin_ref, cqn_ref, ckvn_ref, krr_ref):
        cqn_ref[...] = _rms(cq_ref[...], gq_ref[...]).astype(BF16)
        ckvn_ref[...] = _rms(ckv_ref[...], gkv_ref[...]).astype(BF16)
        roped = _rope(kr_ref[...], cos_ref[...], sin_ref[...])
        krr_ref[...] = roped + pltpu.roll(roped, 64, 1)

    return pl.pallas_call(
        body,
        out_shape=[jax.ShapeDtypeStruct((lp, Q_LORA), BF16), jax.ShapeDtypeStruct((lp, KV_LORA), BF16),
                   jax.ShapeDtypeStruct((lp, 128), F32)],
        grid=(lp // t,),
        in_specs=[_rb(t, KV_LORA, SEG_KV // KV_LORA), _rb(t, 128, SEG_KR // 128), _rb(t, Q_LORA, SEG_CQ // Q_LORA),
                  _full((1, Q_LORA)), _full((1, KV_LORA)), _rb(t, 128), _rb(t, 128)],
        out_specs=[_rb(t, Q_LORA), _rb(t, KV_LORA), _rb(t, 128)],
        name="attn_prep", compiler_params=_params(("parallel",)))(proj, proj, proj, g_q, g_kv, cos_t, sin_t)


def _qk_pack(q, kv, krr, cos_t, sin_t):
    lp = q.shape[0]
    t = ROW_BLOCK

    def body(q_ref, kv_ref, krr_ref, cos_ref, sin_ref, qs_ref, ks_ref, vs_ref, vts_ref):
        lane = lax.broadcasted_iota(jnp.int32, (t, 128), 1)
        lo = lane < 64
        krr = krr_ref[...].astype(BF16)
        for j in range(ATT_HEADS // 2):
            pr = _rope(q_ref[:, 1024 + 128 * j:1024 + 128 * (j + 1)], cos_ref[...], sin_ref[...])
            for h, keep in ((2 * j, lo), (2 * j + 1, jnp.logical_not(lo))):
                qs_ref[h, :, 0:128] = q_ref[:, 128 * h:128 * (h + 1)].astype(BF16)
                qs_ref[h, :, 128:256] = jnp.where(keep, pr, 0.0).astype(BF16)
        for h in range(ATT_HEADS):
            ks_ref[h, :, 0:128] = kv_ref[:, 256 * h:256 * h + 128].astype(BF16)
            ks_ref[h, :, 128:256] = krr
            v = kv_ref[:, 256 * h + 128:256 * (h + 1)]
            vs_ref[h] = v.astype(BF16)
            vts_ref[h] = v.T.astype(BF16)

    slab = lambda w: pl.BlockSpec((ATT_HEADS, t, w), lambda i: (0, i, 0))
    return pl.pallas_call(
        body,
        out_shape=[jax.ShapeDtypeStruct((ATT_HEADS, lp, QP_W), BF16), jax.ShapeDtypeStruct((ATT_HEADS, lp, QP_W), BF16),
                   jax.ShapeDtypeStruct((ATT_HEADS, lp, V_HEAD), BF16), jax.ShapeDtypeStruct((ATT_HEADS, V_HEAD, lp), BF16)],
        grid=(lp // t,),
        in_specs=[_rb(t, 1536), _rb(t, 2048), _rb(t, 128), _rb(t, 128), _rb(t, 128)],
        out_specs=[slab(QP_W), slab(QP_W), slab(V_HEAD), pl.BlockSpec((ATT_HEADS, V_HEAD, t), lambda i: (0, 0, i))],
        name="qk_pack", compiler_params=_params(("parallel",)))(q, kv, krr, cos_t, sin_t)


def _shifted(ext, t, shift):
    if shift == 0:
        return ext[8:, :]
    return pltpu.roll(ext, shift, 0)[8:, :]


def _conv_fwd(name, proj, seg, width, conv_w, conv_b):
    lp = proj.shape[0]
    t = ROW_BLOCK
    cb = seg // width

    def body(u_ref, halo_ref, w_ref, b_ref, pre_ref, act_ref):
        i = pl.program_id(0)
        u = u_ref[...]
        halo = jnp.where(i > 0, halo_ref[...], 0.0)
        ext = jnp.concatenate([halo, u], axis=0)
        pre = jnp.broadcast_to(b_ref[...], (t, width))
        for k in range(CONV_K):
            pre = pre + w_ref[k:k + 1, :] * _shifted(ext, t, CONV_K - 1 - k)
        pre_ref[...] = pre
        act_ref[...] = pre * _sigmoid(pre)

    return pl.pallas_call(
        body,
        out_shape=[jax.ShapeDtypeStruct((lp, width), F32)] * 2,
        grid=(lp // t,),
        in_specs=[_rb(t, width, cb),
                  pl.BlockSpec((8, width), lambda i: (jnp.maximum(i * (t // 8) - 1, 0), cb)),
                  _full((CONV_K, width)), _full((1, width))],
        out_specs=[_rb(t, width), _rb(t, width)],
        name=name, compiler_params=_params(("parallel",)))(proj, proj, conv_w, conv_b)


def _softplus(x):
    return jnp.maximum(x, 0.0) + jnp.log1p(jnp.exp(-jnp.abs(x)))


def _dt_fwd(proj, dt_bias_pad):
    lp = proj.shape[0]
    t = ROW_BLOCK

    def body(x_ref, b_ref, o_ref):
        o_ref[...] = _softplus(x_ref[...] + b_ref[...])

    return pl.pallas_call(
        body, out_shape=jax.ShapeDtypeStruct((lp, 128), F32), grid=(lp // t,),
        in_specs=[_rb(t, 128, SEG_DT // 128), _full((1, 128))], out_specs=_rb(t, 128),
        name="dt_fwd", compiler_params=_params(("parallel",)))(proj, dt_bias_pad)


def _gated_norm_group(y, z, w):
    g = y * (z * _sigmoid(z))
    return g * lax.rsqrt(jnp.mean(g * g, axis=-1, keepdims=True) + EPS) * w


def _gated_norm_fwd(y, proj, w):
    lp = y.shape[0]
    t = ROW_BLOCK
    gw = SSM_WIDTH // 2

    def body(y0, y1, z0, z1, w0, w1, o_ref):
        o_ref[:, 0:gw] = _gated_norm_group(y0[...], z0[...], w0[...]).astype(BF16)
        o_ref[:, gw:] = _gated_norm_group(y1[...], z1[...], w1[...]).astype(BF16)

    zb = SEG_Z // gw
    return pl.pallas_call(
        body, out_shape=jax.ShapeDtypeStruct((lp, SSM_WIDTH), BF16), grid=(lp // t,),
        in_specs=[_rb(t, gw, 0), _rb(t, gw, 1), _rb(t, gw, zb), _rb(t, gw, zb + 1),
                  pl.BlockSpec((1, gw), lambda i: (0, 0)), pl.BlockSpec((1, gw), lambda i: (0, 1))],
        out_specs=_rb(t, SSM_WIDTH),
        name="gated_norm_fwd", compiler_params=_params(("parallel",)))(y, y, proj, proj, w, w)


def _gated_norm_bwd(y, proj, w, dssm):
    lp = y.shape[0]
    t = ROW_BLOCK
    gw = SSM_WIDTH // 2

    def body(y0, y1, z0, z1, w0, w1, d0, d1, dy_ref, dz_ref, dw_ref):
        dws = []
        for g, (yr, zr, wr, dr) in enumerate(((y0, z0, w0, d0), (y1, z1, w1, d1))):
            _, vjp = jax.vjp(_gated_norm_group, yr[...], zr[...], wr[...])
            dyg, dzg, dwg = vjp(dr[...])
            dy_ref[:, g * gw:(g + 1) * gw] = dyg
            dz_ref[:, g * gw:(g + 1) * gw] = dzg
            dws.append(dwg)
        _acc_add(dw_ref, jnp.concatenate(dws, axis=1))

    zb = SEG_Z // gw
    return pl.pallas_call(
        body,
        out_shape=[jax.ShapeDtypeStruct((lp, SSM_WIDTH), F32), jax.ShapeDtypeStruct((lp, SSM_WIDTH), F32),
                   jax.ShapeDtypeStruct((1, SSM_WIDTH), F32)],
        grid=(lp // t,),
        in_specs=[_rb(t, gw, 0), _rb(t, gw, 1), _rb(t, gw, zb), _rb(t, gw, zb + 1),
                  pl.BlockSpec((1, gw), lambda i: (0, 0)), pl.BlockSpec((1, gw), lambda i: (0, 1)),
                  _rb(t, gw, 2), _rb(t, gw, 3)],
        out_specs=[_rb(t, SSM_WIDTH), _rb(t, SSM_WIDTH), _full((1, SSM_WIDTH))],
        name="gated_norm_bwd", compiler_params=_params(("arbitrary",)))(y, y, proj, proj, w, w, dssm, dssm)


def _mix_residual(h0, mix, g_post, g_mlp_pre):
    lp = h0.shape[0]
    t = ROW_BLOCK

    def body(h_ref, m_ref, gp_ref, gm_ref, h1_ref, n2_ref):
        h1 = h_ref[...] + _rms(m_ref[...], gp_ref[...])
        h1_ref[...] = h1
        n2_ref[...] = _rms(h1, gm_ref[...]).astype(BF16)

    return pl.pallas_call(
        body, out_shape=[jax.ShapeDtypeStruct((lp, D_MODEL), F32), jax.ShapeDtypeStruct((lp, D_MODEL), BF16)],
        grid=(lp // t,),
        in_specs=[_rb(t, D_MODEL), _rb(t, D_MODEL), _full((1, D_MODEL)), _full((1, D_MODEL))],
        out_specs=[_rb(t, D_MODEL), _rb(t, D_MODEL)],
        name="mix_residual", compiler_params=_params(("parallel",)))(h0, mix, g_post, g_mlp_pre)


def _loss_and_grad(h1, f, g_post, tgt, n_real):
    lp = h1.shape[0]
    t = ROW_BLOCK

    def body(h1_ref, f_ref, g_ref, t_ref, loss_ref, dh2_ref, df_ref, dg_ref):
        i = pl.program_id(0)
        fx = f_ref[...]
        h2 = h1_ref[...] + _rms(fx, g_ref[...])
        row = i * t + lax.broadcasted_iota(jnp.int32, (t, 1), 0)
        real = jnp.logical_and(row >= N_META, row < N_META + n_real)
        diff = jnp.where(real, h2 - t_ref[...], 0.0)
        part = 0.5 * jnp.sum(jnp.sum(diff * diff, axis=-1, keepdims=True) / D_MODEL, axis=0, keepdims=True)
        _acc_add(loss_ref, jnp.broadcast_to(part, (1, 128)))
        dh2 = diff / D_MODEL
        dh2_ref[...] = dh2
        dfx, dg = _rms_bwd(fx, g_ref[...], dh2)
        df_ref[...] = dfx.astype(BF16)
        _acc_add(dg_ref, dg)

    return pl.pallas_call(
        body,
        out_shape=[jax.ShapeDtypeStruct((1, 128), F32), jax.ShapeDtypeStruct((lp, D_MODEL), F32),
                   jax.ShapeDtypeStruct((lp, D_MODEL), BF16), jax.ShapeDtypeStruct((1, D_MODEL), F32)],
        grid=(lp // t,),
        in_specs=[_rb(t, D_MODEL), _rb(t, D_MODEL), _full((1, D_MODEL)), _rb(t, D_MODEL)],
        out_specs=[_full((1, 128)), _rb(t, D_MODEL), _rb(t, D_MODEL), _full((1, D_MODEL))],
        name="loss_and_grad", compiler_params=_params(("arbitrary",)))(h1, f, g_post, tgt)


def _mlp_residual_bwd(dh2, dn2, h1, g_mlp_pre, mix, g_post):
    lp = h1.shape[0]
    t = ROW_BLOCK

    def body(dh2_ref, dn2_ref, h1_ref, gm_ref, mix_ref, gp_ref, dh1_ref, dmix_ref, dgm_ref, dgp_ref):
        dx, dgm = _rms_bwd(h1_ref[...], gm_ref[...], dn2_ref[...])
        dh1 = dh2_ref[...] + dx
        dh1_ref[...] = dh1
        dmix, dgp = _rms_bwd(mix_ref[...], gp_ref[...], dh1)
        dmix_ref[...] = dmix.astype(BF16)
        _acc_add(dgm_ref, dgm)
        _acc_add(dgp_ref, dgp)

    return pl.pallas_call(
        body,
        out_shape=[jax.ShapeDtypeStruct((lp, D_MODEL), F32), jax.ShapeDtypeStruct((lp, D_MODEL), BF16),
                   jax.ShapeDtypeStruct((1, D_MODEL), F32), jax.ShapeDtypeStruct((1, D_MODEL), F32)],
        grid=(lp // t,),
        in_specs=[_rb(t, D_MODEL), _rb(t, D_MODEL), _rb(t, D_MODEL), _full((1, D_MODEL)), _rb(t, D_MODEL),
                  _full((1, D_MODEL))],
        out_specs=[_rb(t, D_MODEL), _rb(t, D_MODEL), _full((1, D_MODEL)), _full((1, D_MODEL))],
        name="mlp_residual_bwd", compiler_params=_params(("arbitrary",)))(dh2, dn2, h1, g_mlp_pre, mix, g_post)


def _input_norm_bwd(dh1, dn1, h0, g_pre):
    lp = h0.shape[0]
    t = ROW_BLOCK

    def body(dh1_ref, dn1_ref, h0_ref, g_ref, dh0_ref, dg_ref):
        dx, dg = _rms_bwd(h0_ref[...], g_ref[...], dn1_ref[...])
        dh0_ref[...] = dh1_ref[...] + dx
        _acc_add(dg_ref, dg)

    return pl.pallas_call(
        body, out_shape=[jax.ShapeDtypeStruct((lp, D_MODEL), F32), jax.ShapeDtypeStruct((1, D_MODEL), F32)],
        grid=(lp // t,),
        in_specs=[_rb(t, D_MODEL), _rb(t, D_MODEL), _rb(t, D_MODEL), _full((1, D_MODEL))],
        out_specs=[_rb(t, D_MODEL), _full((1, D_MODEL))],
        name="input_norm_bwd", compiler_params=_params(("arbitrary",)))(dh1, dn1, h0, g_pre)


def _conv_bwd(name, dact, pre, proj, seg, width, conv_w):
    lp = proj.shape[0]
    t = ROW_BLOCK
    cb = seg // width
    nblk = lp // t

    def dsilu(p):
        s = _sigmoid(p)
        return s * (1.0 + p * (1.0 - s))

    def body(da_ref, dan_ref, pre_ref, pren_ref, u_ref, halo_ref, w_ref, du_ref, dw_ref, db_ref):
        i = pl.program_id(0)
        dpre = da_ref[...] * dsilu(pre_ref[...])
        dpre_next = jnp.where(i < nblk - 1, dan_ref[...] * dsilu(pren_ref[...]), 0.0)
        extd = jnp.concatenate([dpre, dpre_next], axis=0)
        halo = jnp.where(i > 0, halo_ref[...], 0.0)
        ext = jnp.concatenate([halo, u_ref[...]], axis=0)
        du = jnp.zeros((t, width), F32)
        dws = []
        for k in range(CONV_K):
            m = CONV_K - 1 - k
            ahead = dpre if m == 0 else pltpu.roll(extd, t + 8 - m, 0)[:t, :]
            du = du + w_ref[k:k + 1, :] * ahead
            dws.append(jnp.sum(dpre * _shifted(ext, t, m), axis=0, keepdims=True))
        du_ref[...] = du
        _acc_add(dw_ref, jnp.concatenate(dws + [jnp.zeros((8 - CONV_K, width), F32)], axis=0))
        _acc_add(db_ref, jnp.sum(dpre, axis=0, keepdims=True))

    nxt = lambda i: (jnp.minimum((i + 1) * (t // 8), lp // 8 - 1), 0)
    return pl.pallas_call(
        body,
        out_shape=[jax.ShapeDtypeStruct((lp, width), F32), jax.ShapeDtypeStruct((8, width), F32),
                   jax.ShapeDtypeStruct((1, width), F32)],
        grid=(nblk,),
        in_specs=[_rb(t, width), pl.BlockSpec((8, width), nxt), _rb(t, width), pl.BlockSpec((8, width), nxt),
                  _rb(t, width, cb),
                  pl.BlockSpec((8, width), lambda i: (jnp.maximum(i * (t // 8) - 1, 0), cb)),
                  _full((CONV_K, width))],
        out_specs=[_rb(t, width), _full((8, width)), _full((1, width))],
        name=name, compiler_params=_params(("arbitrary",)))(dact, dact, pre, pre, proj, proj, conv_w)


def _qk_unpack_bwd(dqs, dks, dvs, cos_t, sin_t):
    lp = dqs.shape[1]
    t = ROW_BLOCK

    def body(dqs_ref, dks_ref, dvs_ref, cos_ref, sin_ref, dq_ref, dkv_ref, dkr_ref):
        lane = lax.broadcasted_iota(jnp.int32, (t, 128), 1)
        lo = lane < 64
        for j in range(ATT_HEADS // 2):
            dpr = jnp.where(lo, dqs_ref[2 * j, :, 128:256], dqs_ref[2 * j + 1, :, 128:256])
            dq_ref[:, 1024 + 128 * j:1024 + 128 * (j + 1)] = _rope_bwd(dpr, cos_ref[...], sin_ref[...]).astype(BF16)
        dkrr = jnp.zeros((t, 128), F32)
        for h in range(ATT_HEADS):
            dq_ref[:, 128 * h:128 * (h + 1)] = dqs_ref[h, :, 0:128].astype(BF16)
            dkv_ref[:, 256 * h:256 * h + 128] = dks_ref[h, :, 0:128].astype(BF16)
            dkv_ref[:, 256 * h + 128:256 * (h + 1)] = dvs_ref[h].astype(BF16)
            dkrr = dkrr + dks_ref[h, :, 128:256]
        droped = jnp.where(lo, dkrr + pltpu.roll(dkrr, 64, 1), 0.0)
        dkr_ref[...] = _rope_bwd(droped, cos_ref[...], sin_ref[...])

    slab = lambda w: pl.BlockSpec((ATT_HEADS, t, w), lambda i: (0, i, 0))
    return pl.pallas_call(
        body,
        out_shape=[jax.ShapeDtypeStruct((lp, 1536), BF16), jax.ShapeDtypeStruct((lp, 2048), BF16),
                   jax.ShapeDtypeStruct((lp, 128), F32)],
        grid=(lp // t,),
        in_specs=[slab(QP_W), slab(QP_W), slab(V_HEAD), _rb(t, 128), _rb(t, 128)],
        out_specs=[_rb(t, 1536), _rb(t, 2048), _rb(t, 128)],
        name="qk_unpack_bwd", compiler_params=_params(("parallel",)))(dqs, dks, dvs, cos_t, sin_t)


def _proj_grad(proj, dcqn, dckvn, g_q, g_kv, dkr, ddt_pad, dt_bias_pad, dz, dxs, dbc):
    lp = proj.shape[0]
    t = ROW_BLOCK

    def body(ckv_ref, cq_ref, pdt_ref, dcq_ref, dckv_ref, gq_ref, gkv_ref, dkr_ref, ddt_ref, b_ref, dz_ref, dxs_ref,
             dbc_ref, dp_ref, dgq_ref, dgkv_ref, db_ref):
        dckv, dgkv = _rms_bwd(ckv_ref[...], gkv_ref[...], dckv_ref[...])
        dcq, dgq = _rms_bwd(cq_ref[...], gq_ref[...], dcq_ref[...])
        ddt_raw = ddt_ref[...] * _sigmoid(pdt_ref[...] + b_ref[...])
        dp_ref[:, SEG_KV:SEG_KV + KV_LORA] = dckv.astype(BF16)
        dp_ref[:, SEG_KR:SEG_KR + 128] = dkr_ref[...].astype(BF16)
        dp_ref[:, SEG_CQ:SEG_CQ + Q_LORA] = dcq.astype(BF16)
        dp_ref[:, SEG_DT:SEG_DT + 128] = ddt_raw.astype(BF16)
        dp_ref[:, SEG_DT + 128:SEG_Z] = jnp.zeros((t, SEG_Z - SEG_DT - 128), BF16)
        dp_ref[:, SEG_Z:SEG_XS] = dz_ref[...].astype(BF16)
        dp_ref[:, SEG_XS:SEG_BC] = dxs_ref[...].astype(BF16)
        dp_ref[:, SEG_BC:PROJ_W] = dbc_ref[...].astype(BF16)
        _acc_add(dgq_ref, dgq)
        _acc_add(dgkv_ref, dgkv)
        _acc_add(db_ref, jnp.sum(ddt_raw, axis=0, keepdims=True))

    return pl.pallas_call(
        body,
        out_shape=[jax.ShapeDtypeStruct((lp, PROJ_W), BF16), jax.ShapeDtypeStruct((1, Q_LORA), F32),
                   jax.ShapeDtypeStruct((1, KV_LORA), F32), jax.ShapeDtypeStruct((1, 128), F32)],
        grid=(lp // t,),
        in_specs=[_rb(t, KV_LORA, SEG_KV // KV_LORA), _rb(t, Q_LORA, SEG_CQ // Q_LORA), _rb(t, 128, SEG_DT // 128),
                  _rb(t, Q_LORA), _rb(t, KV_LORA), _full((1, Q_LORA)), _full((1, KV_LORA)), _rb(t, 128), _rb(t, 128),
                  _full((1, 128)), _rb(t, SSM_WIDTH), _rb(t, SSM_WIDTH), _rb(t, 512)],
        out_specs=[_rb(t, PROJ_W), _full((1, Q_LORA)), _full((1, KV_LORA)), _full((1, 128))],
        name="proj_grad", compiler_params=_params(("arbitrary",)))(
            proj, proj, proj, dcqn, dckvn, g_q, g_kv, dkr, ddt_pad, dt_bias_pad, dz, dxs, dbc)


def _pair_tables(n):
    qmaj = [(i, j) for i in range(n) for j in range(i + 1)]
    kmaj = [(i, j) for j in range(n) for i in range(j, n)]
    to = lambda ps, c: jnp.asarray(np.array([p[c] for p in ps], np.int32))
    return (to(qmaj, 0), to(qmaj, 1)), (to(kmaj, 0), to(kmaj, 1))


def _att_block(lp, edge=ATT_BLOCK):
    return edge if lp % edge == 0 else MM_BLOCK


def _nt(a, b):
    return lax.dot_general(a, b, (((1,), (1,)), ((), ())), preferred_element_type=F32)


def _attn_fwd(qs, ks, vts):
    lp = qs.shape[1]
    t = _att_block(lp, ATT_BLOCK_FWD)
    n = lp // t
    (qi, kj), _ = _pair_tables(n)
    tc = t // ATT_SPLIT

    def body(qi_ref, kj_ref, q_ref, k_ref, vt_ref, o_ref, lse_ref, m_s, l_s, acc_s):
        p = pl.program_id(1)
        i, j = qi_ref[p], kj_ref[p]

        @pl.when(j == 0)
        def _():
            m_s[...] = jnp.full_like(m_s, NEG_BIG)
            l_s[...] = jnp.zeros_like(l_s)
            acc_s[...] = jnp.zeros_like(acc_s)

        def update(masked):
            m_all, l_all, acc_all = m_s[...], l_s[...], acc_s[...]
            m_out, l_out, acc_out = [], [], []
            qk = lambda c: _nt(k_ref[0], q_ref[0, c * tc:(c + 1) * tc, :])
            ahead = qk(0)
            for c in range(ATT_SPLIT):
                cols = slice(c * tc, (c + 1) * tc)
                sc = ahead * ATT_SCALE_LOG2
                if c + 1 < ATT_SPLIT:
                    ahead = qk(c + 1)
                if masked:
                    keep = (lax.broadcasted_iota(jnp.int32, (t, tc), 1) + c * tc
                            >= lax.broadcasted_iota(jnp.int32, (t, tc), 0))
                    sc = jnp.where(keep, sc, NEG_BIG)
                m_prev = m_all[:, cols]
                m_new = jnp.maximum(m_prev, jnp.max(sc, axis=0, keepdims=True))
                alpha = jnp.exp2(m_prev - m_new)
                pexp = jnp.exp2(sc - m_new)
                l_out.append(alpha * l_all[:, cols] + jnp.sum(pexp, axis=0, keepdims=True))
                acc_out.append(alpha * acc_all[:, cols] + jnp.dot(vt_ref[0], pexp.astype(BF16),
                                                                  preferred_element_type=F32))
                m_out.append(m_new)
            cat = lambda parts: parts[0] if len(parts) == 1 else jnp.concatenate(parts, axis=1)
            m_s[...], l_s[...], acc_s[...] = cat(m_out), cat(l_out), cat(acc_out)

        @pl.when(j < i)
        def _():
            update(False)

        @pl.when(j == i)
        def _():
            update(True)
            o_ref[...] = (acc_s[...] / l_s[...]).T
            lse_ref[0] = m_s[...] + jnp.log2(l_s[...])

    grid_spec = pltpu.PrefetchScalarGridSpec(
        num_scalar_prefetch=2, grid=(ATT_HEADS, int(qi.shape[0])),
        in_specs=[pl.BlockSpec((1, t, QP_W), lambda h, p, qi, kj: (h, qi[p], 0)),
                  pl.BlockSpec((1, t, QP_W), lambda h, p, qi, kj: (h, kj[p], 0)),
                  pl.BlockSpec((1, V_HEAD, t), lambda h, p, qi, kj: (h, 0, kj[p]))],
        out_specs=[pl.BlockSpec((t, V_HEAD), lambda h, p, qi, kj: (qi[p], h)),
                   pl.BlockSpec((1, 1, t), lambda h, p, qi, kj: (h, 0, qi[p]))],
        scratch_shapes=[pltpu.VMEM((1, t), F32), pltpu.VMEM((1, t), F32), pltpu.VMEM((V_HEAD, t), F32)])
    return pl.pallas_call(
        body, grid_spec=grid_spec,
        out_shape=[jax.ShapeDtypeStruct((lp, ATT_HEADS * V_HEAD), F32), jax.ShapeDtypeStruct((ATT_HEADS, 1, lp), F32)],
        name="attn_fwd", compiler_params=_params(("parallel", "arbitrary")))(qi, kj, qs, ks, vts)


def _attn_delta(datt, att):
    lp = att.shape[0]
    t = MM_BLOCK

    def body(do_ref, o_ref, d_ref):
        prod = do_ref[...] * o_ref[...]
        d_ref[0] = jnp.sum(prod.T, axis=0, keepdims=True)

    blk = pl.BlockSpec((t, V_HEAD), lambda h, i: (i, h))
    return pl.pallas_call(
        body, out_shape=jax.ShapeDtypeStruct((ATT_HEADS, 1, lp), F32), grid=(ATT_HEADS, lp // t),
        in_specs=[blk, blk], out_specs=pl.BlockSpec((1, 1, t), lambda h, i: (h, 0, i)),
        name="attn_delta", compiler_params=_params(("parallel", "parallel")))(datt, att)


def _attn_bwd(qs, ks, vs, datt16, lse2, delta):
    lp = qs.shape[1]
    tk = _att_block(lp)
    tq = ATT_BLOCK_Q_BWD if lp % ATT_BLOCK_Q_BWD == 0 and ATT_BLOCK_Q_BWD % tk == 0 else tk
    r = tq // tk
    nk, nq = lp // tk, lp // tq
    pairs = [(i, j) for j in range(nk) for i in range(j // r, nq)]
    qi = jnp.asarray(np.array([p[0] for p in pairs], np.int32))
    kj = jnp.asarray(np.array([p[1] for p in pairs], np.int32))
    n_pairs = len(pairs)

    def body(qi_ref, kj_ref, k_ref, v_ref, q_ref, do_ref, lse_ref, dl_ref, dq_hbm, dk_ref, dv_ref, dq_s, dk_s, dv_s,
             sem):
        h, p = pl.program_id(0), pl.program_id(1)
        i, j = qi_ref[p], kj_ref[p]
        first = i == j // r

        @pl.when(p == 0)
        def _():
            dq_s[...] = jnp.zeros_like(dq_s)

        @pl.when(first)
        def _():
            dk_s[...] = jnp.zeros_like(dk_s)
            dv_s[...] = jnp.zeros_like(dv_s)

        def step(masked):
            q = q_ref[0]
            do = do_ref[...]
            pt = jnp.exp2(_nt(k_ref[0], q) * ATT_SCALE_LOG2 - lse_ref[0])
            if masked:
                ahead = j * tk - i * tq
                keep = (lax.broadcasted_iota(jnp.int32, (tk, tq), 1)
                        >= lax.broadcasted_iota(jnp.int32, (tk, tq), 0) + ahead)
                pt = jnp.where(keep, pt, 0.0)
            dst = (pt * (_nt(v_ref[0], do) - dl_ref[0]) * ATT_SCALE).astype(BF16)
            dv_s[...] += jnp.dot(pt.astype(BF16), do, preferred_element_type=F32)
            dk_s[...] += jnp.dot(dst, q, preferred_element_type=F32)
            rows = pl.ds(pl.multiple_of(i * tq, tq), tq)
            dq_s[rows, :] += lax.dot_general(dst, k_ref[0], (((0,), (0,)), ((), ())), preferred_element_type=F32)

        @pl.when(jnp.logical_not(first))
        def _():
            step(False)

        @pl.when(first)
        def _():
            step(True)

        @pl.when(i == nq - 1)
        def _():
            dk_ref[0] = dk_s[...]
            dv_ref[0] = dv_s[...]

        @pl.when(p == n_pairs - 1)
        def _():
            out = pltpu.make_async_copy(dq_s, dq_hbm.at[h], sem)
            out.start()
            out.wait()

    grid_spec = pltpu.PrefetchScalarGridSpec(
        num_scalar_prefetch=2, grid=(ATT_HEADS, n_pairs),
        in_specs=[pl.BlockSpec((1, tk, QP_W), lambda h, p, qi, kj: (h, kj[p], 0)),
                  pl.BlockSpec((1, tk, V_HEAD), lambda h, p, qi, kj: (h, kj[p], 0)),
                  pl.BlockSpec((1, tq, QP_W), lambda h, p, qi, kj: (h, qi[p], 0)),
                  pl.BlockSpec((tq, V_HEAD), lambda h, p, qi, kj: (qi[p], h)),
                  pl.BlockSpec((1, 1, tq), lambda h, p, qi, kj: (h, 0, qi[p])),
                  pl.BlockSpec((1, 1, tq), lambda h, p, qi, kj: (h, 0, qi[p]))],
        out_specs=[pl.BlockSpec(memory_space=pl.ANY),
                   pl.BlockSpec((1, tk, QP_W), lambda h, p, qi, kj: (h, kj[p], 0)),
                   pl.BlockSpec((1, tk, V_HEAD), lambda h, p, qi, kj: (h, kj[p], 0))],
        scratch_shapes=[pltpu.VMEM((lp, QP_W), F32), pltpu.VMEM((tk, QP_W), F32), pltpu.VMEM((tk, V_HEAD), F32),
                        pltpu.SemaphoreType.DMA])
    return pl.pallas_call(
        body, grid_spec=grid_spec,
        out_shape=[jax.ShapeDtypeStruct((ATT_HEADS, lp, QP_W), F32), jax.ShapeDtypeStruct((ATT_HEADS, lp, QP_W), F32),
                   jax.ShapeDtypeStruct((ATT_HEADS, lp, V_HEAD), F32)],
        name="attn_bwd", compiler_params=_params(("arbitrary", "arbitrary")))(
            qi, kj, ks, vs, qs, datt16, lse2, delta)


N_PAIRS = SSM_HEADS // 2
HI = lax.Precision.HIGHEST


def _ssd_chunk(xp, bs, cs, dt, dt_t, alr, alc, dsk, st):
    q = dt.shape[0]
    li = lax.broadcasted_iota(jnp.int32, (q, q), 0)
    si = lax.broadcasted_iota(jnp.int32, (q, q), 1)
    tri = (si <= li).astype(F32)
    tri_t = (li <= si).astype(F32)
    lo = lax.broadcasted_iota(jnp.int32, (1, 128), 1) < 64
    h_r = lax.broadcasted_iota(jnp.int32, (1, SSM_HEADS), 1)
    h_c = lax.broadcasted_iota(jnp.int32, (SSM_HEADS, 1), 0)
    a = dt * (-jnp.exp(alr))
    a_t = dt_t * (-jnp.exp(alc))
    acum = jnp.dot(tri, a, precision=HI, preferred_element_type=F32)
    acum_t = jnp.dot(a_t, tri_t, precision=HI, preferred_element_type=F32)
    last = (lax.broadcasted_iota(jnp.int32, (q, 1), 0) == q - 1).astype(F32)
    alast = jnp.sum(acum * last, axis=0, keepdims=True)
    e = jnp.exp(acum)
    rdt = jnp.exp(alast - acum) * dt
    e_last = jnp.exp(alast)

    def col(m, h):
        return jnp.sum(m * (h_r == h).astype(F32), axis=1, keepdims=True)

    def row(m, h):
        return jnp.sum(m * (h_c == h).astype(F32), axis=0, keepdims=True)

    def pair(m, ha):
        return jnp.where(lo, col(m, ha), col(m, ha + 1))

    ys, st_new = [], []
    for g in range(2):
        c_b = cs[g].astype(BF16)
        b_b = bs[g].astype(BF16)
        cb = _nt(c_b, b_b)
        for j in range(N_PAIRS // 2):
            p = (N_PAIRS // 2) * g + j
            ha = 2 * p
            x = xp[p]
            x_b = x.astype(BF16)

            def w_of(h):
                seg = col(acum, h) - row(acum_t, h)
                return (cb * jnp.exp(jnp.minimum(seg, 0.0)) * tri * row(dt_t, h)).astype(BF16)

            y_diag = jnp.where(lo, jnp.dot(w_of(ha), x_b, preferred_element_type=F32),
                               jnp.dot(w_of(ha + 1), x_b, preferred_element_type=F32))
            y_off = jnp.dot(c_b, st[p].astype(BF16), preferred_element_type=F32) * pair(e, ha)
            ys.append(y_diag + y_off + pair(dsk, ha) * x)
            xw = (x * pair(rdt, ha)).astype(BF16)
            st_new.append(st[p] * pair(e_last, ha)
                          + lax.dot_general(b_b, xw, (((0,), (0,)), ((), ())), preferred_element_type=F32))
    return ys, st_new


def _ssd_fwd(xs, bc, dt, dt_t, alr, alc, dsk):
    lp = xs.shape[0]
    q = SSD_CHUNK
    nc = lp // q

    def body(x_ref, b_ref, c_ref, dt_ref, dtt_ref, alr_ref, alc_ref, dsk_ref, y_ref, sp_ref, st_s):
        @pl.when(pl.program_id(0) == 0)
        def _():
            st_s[...] = jnp.zeros_like(st_s)

        sp_ref[0] = st_s[...]
        xp = [x_ref[:, 128 * p:128 * (p + 1)] for p in range(N_PAIRS)]
        bs = [b_ref[:, 0:128], b_ref[:, 128:256]]
        cs = [c_ref[:, 0:128], c_ref[:, 128:256]]
        ys, st_new = _ssd_chunk(xp, bs, cs, dt_ref[...], dtt_ref[...], alr_ref[...], alc_ref[...], dsk_ref[...],
                                [st_s[p] for p in range(N_PAIRS)])
        for p in range(N_PAIRS):
            y_ref[:, 128 * p:128 * (p + 1)] = ys[p]
            st_s[p] = st_new[p]

    return pl.pallas_call(
        body,
        out_shape=[jax.ShapeDtypeStruct((lp, SSM_WIDTH), F32), jax.ShapeDtypeStruct((nc, N_PAIRS, 128, 128), F32)],
        grid=(nc,),
        in_specs=[_rb(q, SSM_WIDTH), _rb(q, 256, 0), _rb(q, 256, 1), _rb(q, SSM_HEADS),
                  pl.BlockSpec((SSM_HEADS, q), lambda i: (0, i)),
                  _full((1, SSM_HEADS)), _full((SSM_HEADS, 1)), _full((1, SSM_HEADS))],
        out_specs=[_rb(q, SSM_WIDTH), pl.BlockSpec((1, N_PAIRS, 128, 128), lambda i: (i, 0, 0, 0))],
        scratch_shapes=[pltpu.VMEM((N_PAIRS, 128, 128), F32)],
        name="ssd_fwd", compiler_params=_params(("arbitrary",)))(xs, bc, bc, dt, dt_t, alr, alc, dsk)


def _ssd_bwd(xs, bc, dt, dt_t, alr, alc, dsk, sprev, dy):
    lp = xs.shape[0]
    q = SSD_CHUNK
    nc = lp // q

    def body(x_ref, b_ref, c_ref, dt_ref, dtt_ref, alr_ref, alc_ref, dsk_ref, sp_ref, dy_ref,
             dx_ref, dbc_ref, ddt_ref, ddtt_ref, dalr_ref, dalc_ref, ddsk_ref, ds_s):
        @pl.when(pl.program_id(0) == 0)
        def _():
            ds_s[...] = jnp.zeros_like(ds_s)

        xp = [x_ref[:, 128 * p:128 * (p + 1)] for p in range(N_PAIRS)]
        bs = [b_ref[:, 0:128], b_ref[:, 128:256]]
        cs = [c_ref[:, 0:128], c_ref[:, 128:256]]
        st = [sp_ref[0, p] for p in range(N_PAIRS)]
        _, vjp = jax.vjp(_ssd_chunk, xp, bs, cs, dt_ref[...], dtt_ref[...], alr_ref[...], alc_ref[...], dsk_ref[...],
                         st)
        dys = [dy_ref[:, 128 * p:128 * (p + 1)] for p in range(N_PAIRS)]
        dxp, dbs, dcs, ddt, ddtt, dalr, dalc, ddsk, dst = vjp((dys, [ds_s[p] for p in range(N_PAIRS)]))
        for p in range(N_PAIRS):
            dx_ref[:, 128 * p:128 * (p + 1)] = dxp[p]
            ds_s[p] = dst[p]
        for g in range(2):
            dbc_ref[:, 128 * g:128 * (g + 1)] = dbs[g]
            dbc_ref[:, 256 + 128 * g:256 + 128 * (g + 1)] = dcs[g]
        ddt_ref[...] = ddt
        ddtt_ref[...] = ddtt
        _acc_add(dalr_ref, dalr)
        _acc_add(dalc_ref, dalc)
        _acc_add(ddsk_ref, ddsk)

    rev = lambda width, cb=0: pl.BlockSpec((q, width), lambda i: (nc - 1 - i, cb))
    return pl.pallas_call(
        body,
        out_shape=[jax.ShapeDtypeStruct((lp, SSM_WIDTH), F32), jax.ShapeDtypeStruct((lp, 512), F32),
                   jax.ShapeDtypeStruct((lp, SSM_HEADS), F32), jax.ShapeDtypeStruct((SSM_HEADS, lp), F32),
                   jax.ShapeDtypeStruct((1, SSM_HEADS), F32), jax.ShapeDtypeStruct((SSM_HEADS, 1), F32),
                   jax.ShapeDtypeStruct((1, SSM_HEADS), F32)],
        grid=(nc,),
        in_specs=[rev(SSM_WIDTH), rev(256, 0), rev(256, 1), rev(SSM_HEADS),
                  pl.BlockSpec((SSM_HEADS, q), lambda i: (0, nc - 1 - i)),
                  _full((1, SSM_HEADS)), _full((SSM_HEADS, 1)), _full((1, SSM_HEADS)),
                  pl.BlockSpec((1, N_PAIRS, 128, 128), lambda i: (nc - 1 - i, 0, 0, 0)), rev(SSM_WIDTH)],
        out_specs=[rev(SSM_WIDTH), rev(512), rev(SSM_HEADS), pl.BlockSpec((SSM_HEADS, q), lambda i: (0, nc - 1 - i)),
                   _full((1, SSM_HEADS)), _full((SSM_HEADS, 1)), _full((1, SSM_HEADS))],
        scratch_shapes=[pltpu.VMEM((N_PAIRS, 128, 128), F32)],
        name="ssd_bwd", compiler_params=_params(("arbitrary",)))(xs, bc, bc, dt, dt_t, alr, alc, dsk, sprev, dy)


def _q_to_slab_order(w):
    hd = QK_NOPE + QK_ROPE
    nope = [w[:, h * hd:h * hd + QK_NOPE] for h in range(ATT_HEADS)]
    rope = [w[:, h * hd + QK_NOPE:(h + 1) * hd] for h in range(ATT_HEADS)]
    return jnp.concatenate(nope + rope, axis=1)


def _q_from_slab_order(wp):
    base = ATT_HEADS * QK_NOPE
    parts = []
    for h in range(ATT_HEADS):
        parts += [wp[:, QK_NOPE * h:QK_NOPE * (h + 1)], wp[:, base + QK_ROPE * h:base + QK_ROPE * (h + 1)]]
    return jnp.concatenate(parts, axis=1)


_IN_CQ, _IN_CKV, _IN_KR, _IN_Z, _IN_XS, _IN_BC, _IN_DT = (0, 384), (384, 640), (640, 704), (704, 1728), (1728, 2752), \
    (2752, 3264), (3264, 3280)


def _pack_w_in(w):
    z = lambda n: jnp.zeros((w.shape[0], n), w.dtype)
    s = lambda r: w[:, r[0]:r[1]]
    return jnp.concatenate([s(_IN_CKV), s(_IN_KR), z(64), s(_IN_CQ), s(_IN_DT), z(112), z(128), s(_IN_Z), s(_IN_XS),
                            s(_IN_BC)], axis=1)


def _unpack_w_in(wp):
    s = lambda off, n: wp[:, off:off + n]
    return jnp.concatenate([s(SEG_CQ, 384), s(SEG_KV, 256), s(SEG_KR, 64), s(SEG_Z, 1024), s(SEG_XS, 1024),
                            s(SEG_BC, 512), s(SEG_DT, 16)], axis=1)


def _rope_tables(lp):
    inv_freq = ROPE_THETA ** (-jnp.arange(0, QK_ROPE, 2, dtype=F32) / QK_ROPE)
    ang = jnp.arange(lp, dtype=F32)[:, None] * inv_freq[None, :]
    cos, sin = jnp.cos(ang), jnp.sin(ang)
    return jnp.tile(cos, (1, 4)), jnp.concatenate([-sin, sin, -sin, sin], axis=1)


def _local_step(x, tgt, w):
    n_real = x.shape[0]
    l = N_META + n_real
    lp = -(-l // MM_BLOCK) * MM_BLOCK
    pad = lambda a: jnp.concatenate([a, jnp.zeros((lp - l, D_MODEL), F32)], axis=0)
    h0 = pad(jnp.concatenate([w["meta_tokens"], x], axis=0))
    tgt_p = pad(jnp.concatenate([jnp.zeros((N_META, D_MODEL), F32), tgt], axis=0))
    cos_t, sin_t = _rope_tables(lp)

    w_in_p = _pack_w_in(w["w_in"])
    w_q_p = _q_to_slab_order(w["w_q_up"])
    w_kv, w_out, w_up, w_down = w["w_kv_up"], w["w_out"], w["w_mlp_up"], w["w_mlp_down"]
    conv_w, conv_b = w["conv_w"], w["conv_b"]
    dt_bias_pad = jnp.concatenate([w["dt_bias"], jnp.zeros((1, 128 - SSM_HEADS), F32)], axis=1)
    alr, dsk = w["a_log"], w["d_skip"]
    alc = alr.reshape(SSM_HEADS, 1)

    n1 = _norm_in(h0, w["norm_mix_pre"])
    proj = _mm("proj", n1, w_in_p)
    cqn, ckvn, krr = _attn_prep(proj, w["q_a_norm"], w["kv_a_norm"], cos_t, sin_t)
    q = _mm("q_up", cqn, w_q_p)
    kv = _mm("kv_up", ckvn, w_kv)
    qs, ks, vs, vts = _qk_pack(q, kv, krr, cos_t, sin_t)
    att, lse2 = _attn_fwd(qs, ks, vts)
    xs_pre, xs_act = _conv_fwd("conv_xs_fwd", proj, SEG_XS, SSM_WIDTH, conv_w[:, :SSM_WIDTH], conv_b[:, :SSM_WIDTH])
    bc_pre, bc_act = _conv_fwd("conv_bc_fwd", proj, SEG_BC, 512, conv_w[:, SSM_WIDTH:], conv_b[:, SSM_WIDTH:])
    dt = _dt_fwd(proj, dt_bias_pad)[:, :SSM_HEADS]
    dt_t = dt.T
    y, sprev = _ssd_fwd(xs_act, bc_act, dt, dt_t, alr, alc, dsk)
    ssm = _gated_norm_fwd(y, proj, w["ssm_norm"])
    cat = jnp.concatenate([att.astype(BF16), ssm], axis=1)
    mix = _mm("out_proj", cat, w_out)
    h1, n2 = _mix_residual(h0, mix, w["norm_mix_post"], w["norm_mlp_pre"])
    relu2 = lambda r: jnp.square(jnp.maximum(r, 0.0))
    u, act = _mm("mlp_up", n2, w_up, outs=((F32, None), (BF16, relu2)))
    f = _mm("mlp_down", act, w_down)
    loss, dh2, df, dg_mlp_post = _loss_and_grad(h1, f, w["norm_mlp_post"], tgt_p, n_real)

    g = {"norm_mlp_post": dg_mlp_post}
    g["w_mlp_down"] = _mm_tn("d_w_mlp_down", act, df)
    du = _mm("d_mlp_act", df, w_down, outs=((BF16, lambda r, ub: r * (2.0 * jnp.maximum(ub, 0.0))),), epi_ins=(u,),
             trans_b=True)
    g["w_mlp_up"] = _mm_tn("d_w_mlp_up", n2, du)
    dn2 = _mm("d_n2", du, w_up, trans_b=True)
    dh1, dmix, g["norm_mlp_pre"], g["norm_mix_post"] = _mlp_residual_bwd(dh2, dn2, h1, w["norm_mlp_pre"], mix,
                                                                         w["norm_mix_post"])
    g["w_out"] = _mm_tn("d_w_out", cat, dmix)
    dcat, dcat16 = _mm("d_cat", dmix, w_out, outs=((F32, None), (BF16, None)), trans_b=True)
    dy, dz, g["ssm_norm"] = _gated_norm_bwd(y, proj, w["ssm_norm"], dcat)
    dxs_act, dbc_act, ddt, ddt_t, dalr, dalc, g["d_skip"] = _ssd_bwd(xs_act, bc_act, dt, dt_t, alr, alc, dsk, sprev, dy)
    g["a_log"] = dalr + dalc.reshape(1, SSM_HEADS)
    dxs, dcw_xs, dcb_xs = _conv_bwd("conv_xs_bwd", dxs_act, xs_pre, proj, SEG_XS, SSM_WIDTH, conv_w[:, :SSM_WIDTH])
    dbc, dcw_bc, dcb_bc = _conv_bwd("conv_bc_bwd", dbc_act, bc_pre, proj, SEG_BC, 512, conv_w[:, SSM_WIDTH:])
    g["conv_w"] = jnp.concatenate([dcw_xs[:CONV_K], dcw_bc[:CONV_K]], axis=1)
    g["conv_b"] = jnp.concatenate([dcb_xs, dcb_bc], axis=1)
    ddt_pad = jnp.concatenate([ddt + ddt_t.T, jnp.zeros((lp, 128 - SSM_HEADS), F32)], axis=1)

    dqs, dks, dvs = _attn_bwd(qs, ks, vs, dcat16, lse2, _attn_delta(dcat, att))
    dq, dkv, dkr = _qk_unpack_bwd(dqs, dks, dvs, cos_t, sin_t)
    g["w_q_up"] = _q_from_slab_order(_mm_tn("d_w_q_up", cqn, dq))
    g["w_kv_up"] = _mm_tn("d_w_kv_up", ckvn, dkv)
    dcqn = _mm("d_cqn", dq, w_q_p, trans_b=True)
    dckvn = _mm("d_ckvn", dkv, w_kv, trans_b=True)
    dproj, g["q_a_norm"], g["kv_a_norm"], ddtb = _proj_grad(proj, dcqn, dckvn, w["q_a_norm"], w["kv_a_norm"], dkr,
                                                          ddt_pad, dt_bias_pad, dz, dxs, dbc)
    g["dt_bias"] = ddtb[:, :SSM_HEADS]
    g["w_in"] = _unpack_w_in(_mm_tn("d_w_in", n1, dproj))
    dn1 = _mm("d_n1", dproj, w_in_p, trans_b=True)
    dh0, g["norm_mix_pre"] = _input_norm_bwd(dh1, dn1, h0, w["norm_mix_pre"])
    g["meta_tokens"] = dh0[:N_META]
    return loss, dh0, g


WEIGHTS = ["meta_tokens", "norm_mix_pre", "w_in", "q_a_norm", "w_q_up", "kv_a_norm", "w_kv_up", "conv_w", "conv_b",
           "dt_bias", "a_log", "d_skip", "ssm_norm", "w_out", "norm_mix_post", "norm_mlp_pre", "w_mlp_up",
           "w_mlp_down", "norm_mlp_post"]
SHARD_AXIS = {"meta_tokens": 1, "w_in": 1, "w_q_up": 1, "w_kv_up": 1, "conv_w": 1, "w_out": 0, "w_mlp_up": 1,
              "w_mlp_down": 0}
FULL_SHAPE = {"meta_tokens": (16, 1024), "norm_mix_pre": (1, 1024), "w_in": (1024, 3280), "q_a_norm": (1, 384),
              "w_q_up": (384, 1536), "kv_a_norm": (1, 256), "w_kv_up": (256, 2048), "conv_w": (4, 1536),
              "conv_b": (1, 1536), "dt_bias": (1, 16), "a_log": (1, 16), "d_skip": (1, 16), "ssm_norm": (1, 1024),
              "w_out": (2048, 1024), "norm_mix_post": (1, 1024), "norm_mlp_pre": (1, 1024), "w_mlp_up": (1024, 4096),
              "w_mlp_down": (4096, 1024), "norm_mlp_post": (1, 1024)}
GATHER_BF16 = ["w_in", "w_q_up", "w_kv_up", "w_out", "w_mlp_up", "w_mlp_down"]
GATHER_F32 = ["meta_tokens", "conv_w"]


def _shard_shape(name):
    shp = list(FULL_SHAPE[name])
    if name in SHARD_AXIS:
        shp[SHARD_AXIS[name]] //= N_CHIPS
    return tuple(shp)


PACK_ORDER = sorted(WEIGHTS, key=lambda n: -_shard_shape(n)[0])


def _packed_rows(shape):
    r, c = shape
    return r if c <= PACK_W else -(-c // PACK_W)


def _pack_rows(arrays, row_multiple):
    parts = []
    for a in arrays:
        r, c = a.shape
        if c > PACK_W:
            assert r == 1, a.shape
            folded = _packed_rows(a.shape)
            a = jnp.pad(a, ((0, 0), (0, folded * PACK_W - c))).reshape(folded, PACK_W)
        elif c < PACK_W:
            a = jnp.pad(a, ((0, 0), (0, PACK_W - c)))
        parts.append(a)
    rows = sum(p.shape[0] for p in parts)
    if rows % row_multiple:
        parts.append(jnp.zeros((row_multiple - rows % row_multiple, PACK_W), parts[0].dtype))
    return jnp.concatenate(parts, axis=0)


def _unpack_rows(packed, shapes):
    out, off = [], 0
    for r, c in shapes:
        nr = _packed_rows((r, c))
        blk = packed[off:off + nr]
        out.append(blk[:, :c] if c <= PACK_W else blk.reshape(1, nr * PACK_W)[:, :c])
        off += nr
    return out


def _chip_slice(full, name, t):
    if name not in SHARD_AXIS:
        return full
    ax = SHARD_AXIS[name]
    n = FULL_SHAPE[name][ax] // N_CHIPS
    return lax.slice_in_dim(full, t * n, (t + 1) * n, axis=ax)


HBM_SPEC = pl.BlockSpec(memory_space=pl.ANY)
CHIP_FLIPS = ((1, 0), (0, 1), (1, 1))


def _gather_chips(bufs):
    nb = len(bufs)

    def body(*refs):
        ins, outs = refs[:nb], refs[nb:2 * nb]
        send, recv, loc = refs[2 * nb:]
        x, y, c = lax.axis_index("x"), lax.axis_index("y"), lax.axis_index("c")
        me = 2 * x + y
        sibling = (x, y, 1 - c)
        sends, forwards = [], []
        for b in range(nb):
            half = bufs[b].shape[0] // 2
            mine = pl.ds(c * half, half)
            own = pltpu.make_async_copy(ins[b], outs[b].at[me], loc.at[b])
            own.start()
            sends.append(own)
            for k, (fx, fy) in enumerate(CHIP_FLIPS):
                cp = pltpu.make_async_remote_copy(
                    src_ref=ins[b].at[mine], dst_ref=outs[b].at[me, mine], send_sem=send.at[b, k],
                    recv_sem=recv.at[b, k], device_id=(x ^ fx, y ^ fy, c), device_id_type=MESH_ID)
                cp.start()
                sends.append(cp)
        for b in range(nb):
            half = bufs[b].shape[0] // 2
            mine, theirs = pl.ds(c * half, half), pl.ds((1 - c) * half, half)
            for k, (fx, fy) in enumerate(CHIP_FLIPS):
                chip = 2 * (x ^ fx) + (y ^ fy)
                landed = outs[b].at[chip, mine]
                pltpu.make_async_remote_copy(src_ref=landed, dst_ref=landed, send_sem=send.at[b, k],
                                             recv_sem=recv.at[b, k], device_id=sibling,
                                             device_id_type=MESH_ID).wait_recv()
                fw = pltpu.make_async_remote_copy(src_ref=landed, dst_ref=landed, send_sem=send.at[b, 3 + k],
                                                  recv_sem=recv.at[b, 3 + k], device_id=sibling,
                                                  device_id_type=MESH_ID)
                fw.start()
                forwards.append((fw, outs[b].at[chip, theirs], b, k))
        for fw, arriving, b, k in forwards:
            pltpu.make_async_remote_copy(src_ref=arriving, dst_ref=arriving, send_sem=send.at[b, 3 + k],
                                         recv_sem=recv.at[b, 3 + k], device_id=sibling,
                                         device_id_type=MESH_ID).wait_recv()
            fw.wait_send()
        for cp in sends[1::4] + sends[2::4] + sends[3::4]:
            cp.wait_send()
        for own in sends[0::4]:
            own.wait()

    return pl.pallas_call(
        body,
        out_shape=[jax.ShapeDtypeStruct((N_CHIPS,) + b.shape, b.dtype) for b in bufs],
        in_specs=[HBM_SPEC] * nb, out_specs=[HBM_SPEC] * nb,
        scratch_shapes=[pltpu.SemaphoreType.DMA((nb, 6)), pltpu.SemaphoreType.DMA((nb, 6)),
                        pltpu.SemaphoreType.DMA((nb,))],
        name="gather_chips")(*bufs)


def _sibling_swap(name, buf):
    def body(src, dst, send, recv):
        x, y, c = lax.axis_index("x"), lax.axis_index("y"), lax.axis_index("c")
        cp = pltpu.make_async_remote_copy(src_ref=src, dst_ref=dst, send_sem=send, recv_sem=recv,
                                          device_id=(x, y, 1 - c), device_id_type=MESH_ID)
        cp.start()
        cp.wait()

    return pl.pallas_call(
        body, out_shape=jax.ShapeDtypeStruct(buf.shape, buf.dtype), in_specs=[HBM_SPEC], out_specs=HBM_SPEC,
        scratch_shapes=[pltpu.SemaphoreType.DMA, pltpu.SemaphoreType.DMA], name=name)(buf)


def _scatter_chips(parts):
    nb = len(parts)

    def body(*refs):
        srcs, dsts = refs[:nb], refs[nb:2 * nb]
        send, recv = refs[2 * nb:]
        x, y, c = lax.axis_index("x"), lax.axis_index("y"), lax.axis_index("c")
        copies = []
        for b in range(nb):
            for k, (fx, fy) in enumerate(CHIP_FLIPS):
                tx, ty = x ^ fx, y ^ fy
                cp = pltpu.make_async_remote_copy(
                    src_ref=srcs[b].at[2 * tx + ty], dst_ref=dsts[b].at[k], send_sem=send.at[b, k],
                    recv_sem=recv.at[b, k], device_id=(tx, ty, c), device_id_type=MESH_ID)
                cp.start()
                copies.append(cp)
        for cp in copies:
            cp.wait()

    return pl.pallas_call(
        body, out_shape=[jax.ShapeDtypeStruct((3,) + p.shape[1:], p.dtype) for p in parts], in_specs=[HBM_SPEC] * nb,
        out_specs=[HBM_SPEC] * nb,
        scratch_shapes=[pltpu.SemaphoreType.DMA((nb, 3)), pltpu.SemaphoreType.DMA((nb, 3))],
        name="scatter_chips")(*parts)


def _add_rows(name, terms, also_bf16=False):
    rows = terms[0].shape[0]
    t = _row_tile(rows)
    n_out = 2 if also_bf16 else 1

    def body(*refs):
        acc = refs[0][...].astype(F32)
        for r in refs[1:-n_out]:
            acc = acc + r[...].astype(F32)
        refs[-n_out][...] = acc
        if also_bf16:
            refs[-1][...] = acc.astype(BF16)

    res = pl.pallas_call(
        body, out_shape=[jax.ShapeDtypeStruct(terms[0].shape, dt) for dt in (F32, BF16)[:n_out]], grid=(rows // t,),
        in_specs=[_rb(t, PACK_W)] * len(terms), out_specs=[_rb(t, PACK_W)] * n_out,
        name=name, compiler_params=_params(("parallel",)))(*terms)
    return res if also_bf16 else res[0]


def _sum_chip_order(name, parts, received, me):
    stack = jnp.concatenate([lax.dynamic_index_in_dim(parts, me, axis=0, keepdims=True), received], axis=0)
    terms = []
    for chip in range(N_CHIPS):
        xr = me ^ chip
        where = jnp.where(xr == 0, 0, jnp.where(xr == 2, 1, jnp.where(xr == 1, 2, 3)))
        terms.append(lax.dynamic_index_in_dim(stack, where, axis=0, keepdims=False))
    return _add_rows(name, terms)


def _row_tile(rows):
    assert rows % 16 == 0, rows
    return max(t for t in range(16, 513, 16) if rows % t == 0)


def _adamw(g, w, m, v):
    rows = g.shape[0]
    t = _row_tile(rows)
    c1 = 1.0 - ADAM_B1 ** ADAM_STEP
    c2 = 1.0 - ADAM_B2 ** ADAM_STEP

    def body(g_ref, w_ref, m_ref, v_ref, d_ref, mo_ref, vo_ref):
        gg = g_ref[...]
        mn = ADAM_B1 * m_ref[...] + (1.0 - ADAM_B1) * gg
        vn = ADAM_B2 * v_ref[...] + (1.0 - ADAM_B2) * (gg * gg)
        d_ref[...] = -ADAM_LR * ((mn / c1) / (jnp.sqrt(vn / c2) + ADAM_EPS) + ADAM_WD * w_ref[...])
        mo_ref[...] = mn
        vo_ref[...] = vn

    return pl.pallas_call(
        body, out_shape=[jax.ShapeDtypeStruct(g.shape, F32)] * 3, grid=(rows // t,),
        in_specs=[_rb(t, PACK_W)] * 4, out_specs=[_rb(t, PACK_W)] * 3,
        name="adamw", compiler_params=_params(("parallel",)))(g, w, m, v)


def kernel(x, meta_tokens, norm_mix_pre, w_in, q_a_norm, w_q_up, kv_a_norm, w_kv_up, conv_w, conv_b, dt_bias, a_log, d_skip, ssm_norm, w_out, norm_mix_post, norm_mlp_pre, w_mlp_up, w_mlp_down, norm_mlp_post, loss_target, m_meta_tokens, m_norm_mix_pre, m_w_in, m_q_a_norm, m_w_q_up, m_kv_a_norm, m_w_kv_up, m_conv_w, m_conv_b, m_dt_bias, m_a_log, m_d_skip, m_ssm_norm, m_w_out, m_norm_mix_post, m_norm_mlp_pre, m_w_mlp_up, m_w_mlp_down, m_norm_mlp_post, v_meta_tokens, v_norm_mix_pre, v_w_in, v_q_a_norm, v_w_q_up, v_kv_a_norm, v_w_kv_up, v_conv_w, v_conv_b, v_dt_bias, v_a_log, v_d_skip, v_ssm_norm, v_w_out, v_norm_mix_post, v_norm_mlp_pre, v_w_mlp_up, v_w_mlp_down, v_norm_mlp_post):
    given = dict(locals())
    drop = lambda name, a: a[0] if a.ndim == 3 else a
    w_loc = {n: drop(n, given[n]) for n in WEIGHTS}
    m_loc = {n: drop(n, given["m_" + n]) for n in WEIGHTS}
    v_loc = {n: drop(n, given["v_" + n]) for n in WEIGHTS}
    ix, iy, ic = lax.axis_index("x"), lax.axis_index("y"), lax.axis_index("c")
    me = 2 * ix + iy

    sent16 = _pack_rows([w_loc[n].astype(BF16) for n in GATHER_BF16], 32)
    sent32 = _pack_rows([w_loc[n] for n in GATHER_F32], 16)
    got16, got32 = _gather_chips([sent16, sent32])
    w_full = {n: w_loc[n] for n in WEIGHTS if n not in SHARD_AXIS}
    for names, got in ((GATHER_BF16, got16), (GATHER_F32, got32)):
        per_chip = [_unpack_rows(got[t], [_shard_shape(n) for n in names]) for t in range(N_CHIPS)]
        for k, n in enumerate(names):
            w_full[n] = jnp.concatenate([per_chip[t][k] for t in range(N_CHIPS)], axis=SHARD_AXIS[n])

    loss, dh0, g_full = _local_step(x[0], loss_target[0], w_full)
    n_real = x.shape[1]
    grad_x = dh0[N_META:N_META + n_real][None]

    shapes = [_shard_shape(n) for n in PACK_ORDER]
    slots = [_pack_rows([_chip_slice(g_full[n], n, t) for n in PACK_ORDER], 16) for t in range(N_CHIPS)]
    rows = slots[0].shape[0]
    half = rows // 2
    halves = lambda hh: jnp.concatenate([lax.dynamic_slice_in_dim(s, hh * half, half, axis=0) for s in slots], axis=0)
    keep, give = halves(ic), halves(1 - ic)
    from_sibling = _sibling_swap("sibling_swap", give)
    part32, part16 = _add_rows("chip_partial", [keep, from_sibling], also_bf16=True)
    part32 = part32.reshape(N_CHIPS, half, PACK_W)
    part16 = part16.reshape(N_CHIPS, half, PACK_W)
    tail_start = rows - sum(_packed_rows(_shard_shape(n)) for n in PACK_ORDER if n not in SHARD_AXIS)
    assert tail_start + TAIL_ROWS >= rows and all(n in SHARD_AXIS for n in PACK_ORDER[:len(SHARD_AXIS)])
    tail32 = part32[:, half - TAIL_ROWS:, :]
    from16, from_tail = _scatter_chips([part16, tail32])
    my_half = _sum_chip_order("chip_total", part16, from16, me)
    my_tail = _sum_chip_order("chip_total_tail", tail32, from_tail, me)
    my_half = lax.dynamic_update_slice(my_half, my_tail, (half - TAIL_ROWS, 0))
    other_half = _sibling_swap("sibling_gather", my_half)
    g_red = jnp.where(ic == 0, jnp.concatenate([my_half, other_half], axis=0),
                      jnp.concatenate([other_half, my_half], axis=0))

    pack_loc = lambda d: _pack_rows([d[n] for n in PACK_ORDER], 16)
    delta, new_m, new_v = _adamw(g_red, pack_loc(w_loc), pack_loc(m_loc), pack_loc(v_loc))

    def outputs(packed_arr):
        parts = dict(zip(PACK_ORDER, _unpack_rows(packed_arr, shapes)))
        return [parts[n][None] if given[n].ndim == 3 else parts[n] for n in WEIGHTS]

    total = lax.psum(loss[0, 0], ("x", "y", "c"))
    return (total, grad_x, *outputs(g_red), *outputs(delta), *outputs(new_m), *outputs(new_v))
```

```python
import functools

import numpy as np
import jax
import jax.numpy as jnp
from jax import lax
from jax.experimental import pallas as pl
from jax.experimental.pallas import tpu as pltpu

F32 = jnp.float32
BF16 = jnp.bfloat16

D_MODEL = 1024
N_META = 16
EPS = 1e-6
ATT_HEADS = 8
Q_LORA = 384
KV_LORA = 256
QK_NOPE = 128
QK_ROPE = 64
V_HEAD = 128
ROPE_THETA = 10000.0
SSM_HEADS = 16
SSM_HEAD_DIM = 64
SSM_WIDTH = 1024
SSM_STATE = 128
CONV_K = 4
D_FF = 4096
ATT_SCALE = float((QK_NOPE + QK_ROPE) ** -0.5)
ATT_SCALE_LOG2 = float(ATT_SCALE * np.log2(np.e))

ADAM_LR = 0.001
ADAM_B1 = 0.9
ADAM_B2 = 0.999
ADAM_EPS = 1e-08
ADAM_WD = 0.01
ADAM_STEP = 10

SEG_KV, SEG_KR, SEG_CQ, SEG_DT, SEG_Z, SEG_XS, SEG_BC = 0, 256, 384, 768, 1024, 2048, 3072
PROJ_W = 3584
QP_W = 256

ROW_BLOCK = 512
MM_BLOCK = 512
SSD_CHUNK = 256
ATT_SPLIT = 1
ATT_BLOCK = 768
ATT_BLOCK_FWD = 1536
ATT_BLOCK_Q_BWD = 1536
VMEM_LIMIT = 56 * 1024 * 1024
NEG_BIG = -1e30

PACK_W = 1024
TAIL_ROWS = 16
N_CHIPS = 4
MESH_ID = pl.DeviceIdType.MESH


def _params(sem):
    return pltpu.CompilerParams(dimension_semantics=sem, vmem_limit_bytes=VMEM_LIMIT)


def _rb(rows, width, cb=0):
    return pl.BlockSpec((rows, width), lambda i: (i, cb))


def _full(shape):
    zeros = (0,) * len(shape)
    return pl.BlockSpec(shape, lambda i: zeros)


def _acc_add(ref, val):
    first = pl.program_id(0) == 0

    @pl.when(first)
    def _():
        ref[...] = val

    @pl.when(jnp.logical_not(first))
    def _():
        ref[...] += val


def _rms(x, g):
    r = lax.rsqrt(jnp.mean(x * x, axis=-1, keepdims=True) + EPS)
    return x * r * g


def _rms_bwd(x, g, dy):
    r = lax.rsqrt(jnp.mean(x * x, axis=-1, keepdims=True) + EPS)
    dyg = dy * g
    dx = r * dyg - x * (r * r * r) * jnp.mean(x * dyg, axis=-1, keepdims=True)
    dg = jnp.sum(dy * x * r, axis=0, keepdims=True)
    return dx, dg


def _sigmoid(x):
    return 1.0 / (1.0 + jnp.exp(-x))


def _swap32(x):
    lane = lax.broadcasted_iota(jnp.int32, x.shape, 1)
    return jnp.where((lane % 64) < 32, pltpu.roll(x, 96, 1), pltpu.roll(x, 32, 1))


def _rope(x, cos_t, sin_t):
    return x * cos_t + _swap32(x) * sin_t


def _rope_bwd(dr, cos_t, sin_t):
    return dr * cos_t + _swap32(dr * sin_t)


def _tile(n, cap):
    if n <= cap:
        return n
    best = 128
    for t in range(128, cap + 1, 128):
        if n % t == 0:
            best = t
    assert n % best == 0, (n, cap)
    return best


MM_VMEM_BUDGET = 40 * 1024 * 1024
TN_ROWS_CAP = 1536


def _mm(name, a, b, outs=((F32, None),), epi_ins=(), trans_b=False):
    m, k = a.shape
    n = b.shape[0] if trans_b else b.shape[1]
    tm = MM_BLOCK
    n_epi = len(epi_ins)
    out_bytes = sum(jnp.dtype(dt).itemsize for dt, _ in outs) + sum(e.dtype.itemsize for e in epi_ins)
    step_bytes = lambda tn: 2 * (tm * k * a.dtype.itemsize + k * tn * b.dtype.itemsize + tm * tn * out_bytes)
    tn = n
    while step_bytes(tn) > MM_VMEM_BUDGET and tn % 256 == 0:
        tn //= 2
    assert n % tn == 0 and step_bytes(tn) <= MM_VMEM_BUDGET, (name, n, tn)

    def body(a_ref, b_ref, *rest):
        epi_refs = rest[:n_epi]
        out_refs = rest[n_epi:]
        lhs, rhs = a_ref[...].astype(BF16), b_ref[...].astype(BF16)
        r = _nt(lhs, rhs) if trans_b else jnp.dot(lhs, rhs, preferred_element_type=F32)
        blocks = [e[...] for e in epi_refs]
        for o_ref, (dt, fn) in zip(out_refs, outs):
            o_ref[...] = (r if fn is None else fn(r, *blocks)).astype(dt)

    out_spec = pl.BlockSpec((tm, tn), lambda j, i: (i, j))
    b_spec = pl.BlockSpec((tn, k), lambda j, i: (j, 0)) if trans_b else pl.BlockSpec((k, tn), lambda j, i: (0, j))
    res = pl.pallas_call(
        body,
        out_shape=[jax.ShapeDtypeStruct((m, n), dt) for dt, _ in outs],
        grid=(n // tn, m // tm),
        in_specs=[pl.BlockSpec((tm, k), lambda j, i: (i, 0)), b_spec] + [out_spec] * n_epi,
        out_specs=[out_spec] * len(outs),
        name=name,
        compiler_params=_params(("parallel", "parallel")),
    )(a, b, *epi_ins)
    return res[0] if len(outs) == 1 else res


def _mm_tn(name, x, dy, ta_cap=1024, tn_cap=1024):
    l, a = x.shape
    n = dy.shape[1]
    ta, tn = _tile(a, ta_cap), _tile(n, tn_cap)
    tl = max(t for t in range(MM_BLOCK, TN_ROWS_CAP + 1, MM_BLOCK) if l % t == 0)
    nl = l // tl

    def body(x_ref, dy_ref, o_ref):
        ll = pl.program_id(2)

        @pl.when(ll == 0)
        def _():
            o_ref[...] = jnp.zeros_like(o_ref)

        o_ref[...] += lax.dot_general(x_ref[...].astype(BF16), dy_ref[...].astype(BF16), (((0,), (0,)), ((), ())),
                                      preferred_element_type=F32)

    return pl.pallas_call(
        body,
        out_shape=jax.ShapeDtypeStruct((a, n), F32),
        grid=(a // ta, n // tn, nl),
        in_specs=[pl.BlockSpec((tl, ta), lambda i, j, ll: (ll, i)), pl.BlockSpec((tl, tn), lambda i, j, ll: (ll, j))],
        out_specs=pl.BlockSpec((ta, tn), lambda i, j, ll: (i, j)),
        name=name,
        compiler_params=_params(("parallel", "parallel", "arbitrary")),
    )(x, dy)


def _norm_in(h0, g_pre):
    lp = h0.shape[0]
    t = ROW_BLOCK

    def body(h_ref, g_ref, o_ref):
        o_ref[...] = _rms(h_ref[...], g_ref[...]).astype(BF16)

    return pl.pallas_call(
        body, out_shape=jax.ShapeDtypeStruct((lp, D_MODEL), BF16), grid=(lp // t,),
        in_specs=[_rb(t, D_MODEL), _full((1, D_MODEL))], out_specs=_rb(t, D_MODEL),
        name="norm_in", compiler_params=_params(("parallel",)))(h0, g_pre)


def _attn_prep(proj, g_q, g_kv, cos_t, sin_t):
    lp = proj.shape[0]
    t = ROW_BLOCK

    def body(ckv_ref, kr_ref, cq_ref, gq_ref, gkv_ref, cos_ref, sin_ref, cqn_ref, ckvn_ref, krr_ref):
        cqn_ref[...] = _rms(cq_ref[...], gq_ref[...]).astype(BF16)
        ckvn_ref[...] = _rms(ckv_ref[...], gkv_ref[...]).astype(BF16)
        roped = _rope(kr_ref[...], cos_ref[...], sin_ref[...])
        krr_ref[...] = roped + pltpu.roll(roped, 64, 1)

    return pl.pallas_call(
        body,
        out_shape=[jax.ShapeDtypeStruct((lp, Q_LORA), BF16), jax.ShapeDtypeStruct((lp, KV_LORA), BF16),
                   jax.ShapeDtypeStruct((lp, 128), F32)],
        grid=(lp // t,),
        in_specs=[_rb(t, KV_LORA, SEG_KV // KV_LORA), _rb(t, 128, SEG_KR // 128), _rb(t, Q_LORA, SEG_CQ // Q_LORA),
                  _full((1, Q_LORA)), _full((1, KV_LORA)), _rb(t, 128), _rb(t, 128)],
        out_specs=[_rb(t, Q_LORA), _rb(t, KV_LORA), _rb(t, 128)],
        name="attn_prep", compiler_params=_params(("parallel",)))(proj, proj, proj, g_q, g_kv, cos_t, sin_t)


def _qk_pack(q, kv, krr, cos_t, sin_t):
    lp = q.shape[0]
    t = ROW_BLOCK

    def body(q_ref, kv_ref, krr_ref, cos_ref, sin_ref, qs_ref, ks_ref, vs_ref, vts_ref):
        lane = lax.broadcasted_iota(jnp.int32, (t, 128), 1)
        lo = lane < 64
        krr = krr_ref[...].astype(BF16)
        for j in range(ATT_HEADS // 2):
            pr = _rope(q_ref[:, 1024 + 128 * j:1024 + 128 * (j + 1)], cos_ref[...], sin_ref[...])
            for h, keep in ((2 * j, lo), (2 * j + 1, jnp.logical_not(lo))):
                qs_ref[h, :, 0:128] = q_ref[:, 128 * h:128 * (h + 1)].astype(BF16)
                qs_ref[h, :, 128:256] = jnp.where(keep, pr, 0.0).astype(BF16)
        for h in range(ATT_HEADS):
            ks_ref[h, :, 0:128] = kv_ref[:, 256 * h:256 * h + 128].astype(BF16)
            ks_ref[h, :, 128:256] = krr
            v = kv_ref[:, 256 * h + 128:256 * (h + 1)]
            vs_ref[h] = v.astype(BF16)
            vts_ref[h] = v.astype(F32).T.astype(BF16)

    slab = lambda w: pl.BlockSpec((ATT_HEADS, t, w), lambda i: (0, i, 0))
    return pl.pallas_call(
        body,
        out_shape=[jax.ShapeDtypeStruct((ATT_HEADS, lp, QP_W), BF16), jax.ShapeDtypeStruct((ATT_HEADS, lp, QP_W), BF16),
                   jax.ShapeDtypeStruct((ATT_HEADS, lp, V_HEAD), BF16), jax.ShapeDtypeStruct((ATT_HEADS, V_HEAD, lp), BF16)],
        grid=(lp // t,),
        in_specs=[_rb(t, 1536), _rb(t, 2048), _rb(t, 128), _rb(t, 128), _rb(t, 128)],
        out_specs=[slab(QP_W), slab(QP_W), slab(V_HEAD), pl.BlockSpec((ATT_HEADS, V_HEAD, t), lambda i: (0, 0, i))],
        name="qk_pack", compiler_params=_params(("parallel",)))(q, kv, krr, cos_t, sin_t)


def _shifted(ext, t, shift):
    if shift == 0:
        return ext[8:, :]
    return pltpu.roll(ext, shift, 0)[8:, :]


def _conv_fwd(name, proj, seg, width, conv_w, conv_b):
    lp = proj.shape[0]
    t = ROW_BLOCK
    cb = seg // width

    def body(u_ref, halo_ref, w_ref, b_ref, pre_ref, act_ref):
        i = pl.program_id(0)
        u = u_ref[...]
        halo = jnp.where(i > 0, halo_ref[...], 0.0)
        ext = jnp.concatenate([halo, u], axis=0)
        pre = jnp.broadcast_to(b_ref[...], (t, width))
        for k in range(CONV_K):
            pre = pre + w_ref[k:k + 1, :] * _shifted(ext, t, CONV_K - 1 - k)
        pre_ref[...] = pre
        act_ref[...] = pre * _sigmoid(pre)

    return pl.pallas_call(
        body,
        out_shape=[jax.ShapeDtypeStruct((lp, width), F32)] * 2,
        grid=(lp // t,),
        in_specs=[_rb(t, width, cb),
                  pl.BlockSpec((8, width), lambda i: (jnp.maximum(i * (t // 8) - 1, 0), cb)),
                  _full((CONV_K, width)), _full((1, width))],
        out_specs=[_rb(t, width), _rb(t, width)],
        name=name, compiler_params=_params(("parallel",)))(proj, proj, conv_w, conv_b)


def _softplus(x):
    return jnp.maximum(x, 0.0) + jnp.log1p(jnp.exp(-jnp.abs(x)))


def _dt_fwd(proj, dt_bias_pad):
    lp = proj.shape[0]
    t = ROW_BLOCK

    def body(x_ref, b_ref, o_ref):
        o_ref[...] = _softplus(x_ref[...] + b_ref[...])

    return pl.pallas_call(
        body, out_shape=jax.ShapeDtypeStruct((lp, 128), F32), grid=(lp // t,),
        in_specs=[_rb(t, 128, SEG_DT // 128), _full((1, 128))], out_specs=_rb(t, 128),
        name="dt_fwd", compiler_params=_params(("parallel",)))(proj, dt_bias_pad)


def _gated_norm_group(y, z, w):
    g = y * (z * _sigmoid(z))
    return g * lax.rsqrt(jnp.mean(g * g, axis=-1, keepdims=True) + EPS) * w


def _gated_norm_fwd(y, proj, w):
    lp = y.shape[0]
    t = ROW_BLOCK
    gw = SSM_WIDTH // 2

    def body(y0, y1, z0, z1, w0, w1, o_ref):
        o_ref[:, 0:gw] = _gated_norm_group(y0[...], z0[...], w0[...]).astype(BF16)
        o_ref[:, gw:] = _gated_norm_group(y1[...], z1[...], w1[...]).astype(BF16)

    zb = SEG_Z // gw
    return pl.pallas_call(
        body, out_shape=jax.ShapeDtypeStruct((lp, SSM_WIDTH), BF16), grid=(lp // t,),
        in_specs=[_rb(t, gw, 0), _rb(t, gw, 1), _rb(t, gw, zb), _rb(t, gw, zb + 1),
                  pl.BlockSpec((1, gw), lambda i: (0, 0)), pl.BlockSpec((1, gw), lambda i: (0, 1))],
        out_specs=_rb(t, SSM_WIDTH),
        name="gated_norm_fwd", compiler_params=_params(("parallel",)))(y, y, proj, proj, w, w)


def _gated_norm_bwd(y, proj, w, dssm):
    lp = y.shape[0]
    t = ROW_BLOCK
    gw = SSM_WIDTH // 2

    def body(y0, y1, z0, z1, w0, w1, d0, d1, dy_ref, dz_ref, dw_ref):
        dws = []
        for g, (yr, zr, wr, dr) in enumerate(((y0, z0, w0, d0), (y1, z1, w1, d1))):
            _, vjp = jax.vjp(_gated_norm_group, yr[...], zr[...], wr[...])
            dyg, dzg, dwg = vjp(dr[...])
            dy_ref[:, g * gw:(g + 1) * gw] = dyg
            dz_ref[:, g * gw:(g + 1) * gw] = dzg
            dws.append(dwg)
        _acc_add(dw_ref, jnp.concatenate(dws, axis=1))

    zb = SEG_Z // gw
    return pl.pallas_call(
        body,
        out_shape=[jax.ShapeDtypeStruct((lp, SSM_WIDTH), F32), jax.ShapeDtypeStruct((lp, SSM_WIDTH), F32),
                   jax.ShapeDtypeStruct((1, SSM_WIDTH), F32)],
        grid=(lp // t,),
        in_specs=[_rb(t, gw, 0), _rb(t, gw, 1), _rb(t, gw, zb), _rb(t, gw, zb + 1),
                  pl.BlockSpec((1, gw), lambda i: (0, 0)), pl.BlockSpec((1, gw), lambda i: (0, 1)),
                  _rb(t, gw, 2), _rb(t, gw, 3)],
        out_specs=[_rb(t, SSM_WIDTH), _rb(t, SSM_WIDTH), _full((1, SSM_WIDTH))],
        name="gated_norm_bwd", compiler_params=_params(("arbitrary",)))(y, y, proj, proj, w, w, dssm, dssm)


def _mix_residual(h0, mix, g_post, g_mlp_pre):
    lp = h0.shape[0]
    t = ROW_BLOCK

    def body(h_ref, m_ref, gp_ref, gm_ref, h1_ref, n2_ref):
        h1 = h_ref[...] + _rms(m_ref[...], gp_ref[...])
        h1_ref[...] = h1
        n2_ref[...] = _rms(h1, gm_ref[...]).astype(BF16)

    return pl.pallas_call(
        body, out_shape=[jax.ShapeDtypeStruct((lp, D_MODEL), F32), jax.ShapeDtypeStruct((lp, D_MODEL), BF16)],
        grid=(lp // t,),
        in_specs=[_rb(t, D_MODEL), _rb(t, D_MODEL), _full((1, D_MODEL)), _full((1, D_MODEL))],
        out_specs=[_rb(t, D_MODEL), _rb(t, D_MODEL)],
        name="mix_residual", compiler_params=_params(("parallel",)))(h0, mix, g_post, g_mlp_pre)


def _loss_and_grad(h1, f, g_post, tgt, n_real):
    lp = h1.shape[0]
    t = ROW_BLOCK
    assert n_real % t == 0 and t % N_META == 0
    n_tb = n_real // t

    def body(h1_ref, f_ref, g_ref, halo_ref, t_ref, loss_ref, dh2_ref, df_ref, dg_ref):
        i = pl.program_id(0)
        fx = f_ref[...]
        h2 = h1_ref[...] + _rms(fx, g_ref[...])
        row = i * t + lax.broadcasted_iota(jnp.int32, (t, 1), 0)
        real = jnp.logical_and(row >= N_META, row < N_META + n_real)
        target = jnp.concatenate([halo_ref[...], t_ref[0:t - N_META, :]], axis=0)
        diff = jnp.where(real, h2 - target, 0.0)
        part = 0.5 * jnp.sum(jnp.sum(diff * diff, axis=-1, keepdims=True) / D_MODEL, axis=0, keepdims=True)
        _acc_add(loss_ref, jnp.broadcast_to(part, (1, 128)))
        dh2 = diff / D_MODEL
        dh2_ref[...] = dh2
        dfx, dg = _rms_bwd(fx, g_ref[...], dh2)
        df_ref[...] = dfx.astype(BF16)
        _acc_add(dg_ref, dg)

    return pl.pallas_call(
        body,
        out_shape=[jax.ShapeDtypeStruct((1, 128), F32), jax.ShapeDtypeStruct((lp, D_MODEL), F32),
                   jax.ShapeDtypeStruct((lp, D_MODEL), BF16), jax.ShapeDtypeStruct((1, D_MODEL), F32)],
        grid=(lp // t,),
        in_specs=[_rb(t, D_MODEL), _rb(t, D_MODEL), _full((1, D_MODEL)),
                  pl.BlockSpec((N_META, D_MODEL),
                               lambda i: (jnp.clip(i * (t // N_META) - 1, 0, n_real // N_META - 1), 0)),
                  pl.BlockSpec((t, D_MODEL), lambda i: (jnp.minimum(i, n_tb - 1), 0))],
        out_specs=[_full((1, 128)), _rb(t, D_MODEL), _rb(t, D_MODEL), _full((1, D_MODEL))],
        name="loss_and_grad", compiler_params=_params(("arbitrary",)))(h1, f, g_post, tgt, tgt)


def _mlp_residual_bwd(dh2, dn2, h1, g_mlp_pre, mix, g_post):
    lp = h1.shape[0]
    t = ROW_BLOCK

    def body(dh2_ref, dn2_ref, h1_ref, gm_ref, mix_ref, gp_ref, dh1_ref, dmix_ref, dgm_ref, dgp_ref):
        dx, dgm = _rms_bwd(h1_ref[...], gm_ref[...], dn2_ref[...])
        dh1 = dh2_ref[...] + dx
        dh1_ref[...] = dh1
        dmix, dgp = _rms_bwd(mix_ref[...], gp_ref[...], dh1)
        dmix_ref[...] = dmix.astype(BF16)
        _acc_add(dgm_ref, dgm)
        _acc_add(dgp_ref, dgp)

    return pl.pallas_call(
        body,
        out_shape=[jax.ShapeDtypeStruct((lp, D_MODEL), F32), jax.ShapeDtypeStruct((lp, D_MODEL), BF16),
                   jax.ShapeDtypeStruct((1, D_MODEL), F32), jax.ShapeDtypeStruct((1, D_MODEL), F32)],
        grid=(lp // t,),
        in_specs=[_rb(t, D_MODEL), _rb(t, D_MODEL), _rb(t, D_MODEL), _full((1, D_MODEL)), _rb(t, D_MODEL),
                  _full((1, D_MODEL))],
        out_specs=[_rb(t, D_MODEL), _rb(t, D_MODEL), _full((1, D_MODEL)), _full((1, D_MODEL))],
        name="mlp_residual_bwd", compiler_params=_params(("arbitrary",)))(dh2, dn2, h1, g_mlp_pre, mix, g_post)


def _input_norm_bwd(dh1, dn1, h0, g_pre):
    lp = h0.shape[0]
    t = ROW_BLOCK

    def body(dh1_ref, dn1_ref, h0_ref, g_ref, dh0_ref, dg_ref):
        dx, dg = _rms_bwd(h0_ref[...], g_ref[...], dn1_ref[...])
        dh0_ref[...] = dh1_ref[...] + dx
        _acc_add(dg_ref, dg)

    return pl.pallas_call(
        body, out_shape=[jax.ShapeDtypeStruct((lp, D_MODEL), F32), jax.ShapeDtypeStruct((1, D_MODEL), F32)],
        grid=(lp // t,),
        in_specs=[_rb(t, D_MODEL), _rb(t, D_MODEL), _rb(t, D_MODEL), _full((1, D_MODEL))],
        out_specs=[_rb(t, D_MODEL), _full((1, D_MODEL))],
        name="input_norm_bwd", compiler_params=_params(("arbitrary",)))(dh1, dn1, h0, g_pre)


def _conv_bwd(name, dact, pre, proj, seg, width, conv_w):
    lp = proj.shape[0]
    t = ROW_BLOCK
    cb = seg // width
    nblk = lp // t

    def dsilu(p):
        s = _sigmoid(p)
        return s * (1.0 + p * (1.0 - s))

    def body(da_ref, dan_ref, pre_ref, pren_ref, u_ref, halo_ref, w_ref, du_ref, dw_ref, db_ref):
        i = pl.program_id(0)
        dpre = da_ref[...] * dsilu(pre_ref[...])
        dpre_next = jnp.where(i < nblk - 1, dan_ref[...] * dsilu(pren_ref[...]), 0.0)
        extd = jnp.concatenate([dpre, dpre_next], axis=0)
        halo = jnp.where(i > 0, halo_ref[...], 0.0)
        ext = jnp.concatenate([halo, u_ref[...]], axis=0)
        du = jnp.zeros((t, width), F32)
        dws = []
        for k in range(CONV_K):
            m = CONV_K - 1 - k
            ahead = dpre if m == 0 else pltpu.roll(extd, t + 8 - m, 0)[:t, :]
            du = du + w_ref[k:k + 1, :] * ahead
            dws.append(jnp.sum(dpre * _shifted(ext, t, m), axis=0, keepdims=True))
        du_ref[...] = du
        _acc_add(dw_ref, jnp.concatenate(dws + [jnp.zeros((8 - CONV_K, width), F32)], axis=0))
        _acc_add(db_ref, jnp.sum(dpre, axis=0, keepdims=True))

    nxt = lambda i: (jnp.minimum((i + 1) * (t // 8), lp // 8 - 1), 0)
    return pl.pallas_call(
        body,
        out_shape=[jax.ShapeDtypeStruct((lp, width), F32), jax.ShapeDtypeStruct((8, width), F32),
                   jax.ShapeDtypeStruct((1, width), F32)],
        grid=(nblk,),
        in_specs=[_rb(t, width), pl.BlockSpec((8, width), nxt), _rb(t, width), pl.BlockSpec((8, width), nxt),
                  _rb(t, width, cb),
                  pl.BlockSpec((8, width), lambda i: (jnp.maximum(i * (t // 8) - 1, 0), cb)),
                  _full((CONV_K, width))],
        out_specs=[_rb(t, width), _full((8, width)), _full((1, width))],
        name=name, compiler_params=_params(("arbitrary",)))(dact, dact, pre, pre, proj, proj, conv_w)


def _qk_unpack_bwd(dqs, dks, dvs, cos_t, sin_t):
    lp = dqs.shape[1]
    t = ROW_BLOCK

    def body(dqs_ref, dks_ref, dvs_ref, cos_ref, sin_ref, dq_ref, dkv_ref, dkr_ref):
        lane = lax.broadcasted_iota(jnp.int32, (t, 128), 1)
        lo = lane < 64
        for j in range(ATT_HEADS // 2):
            dpr = jnp.where(lo, dqs_ref[2 * j, :, 128:256], dqs_ref[2 * j + 1, :, 128:256])
            dq_ref[:, 1024 + 128 * j:1024 + 128 * (j + 1)] = _rope_bwd(dpr, cos_ref[...], sin_ref[...]).astype(BF16)
        dkrr = jnp.zeros((t, 128), F32)
        for h in range(ATT_HEADS):
            dq_ref[:, 128 * h:128 * (h + 1)] = dqs_ref[h, :, 0:128].astype(BF16)
            dkv_ref[:, 256 * h:256 * h + 128] = dks_ref[h, :, 0:128].astype(BF16)
            dkv_ref[:, 256 * h + 128:256 * (h + 1)] = dvs_ref[h].astype(BF16)
            dkrr = dkrr + dks_ref[h, :, 128:256]
        droped = jnp.where(lo, dkrr + pltpu.roll(dkrr, 64, 1), 0.0)
        dkr_ref[...] = _rope_bwd(droped, cos_ref[...], sin_ref[...])

    slab = lambda w: pl.BlockSpec((ATT_HEADS, t, w), lambda i: (0, i, 0))
    return pl.pallas_call(
        body,
        out_shape=[jax.ShapeDtypeStruct((lp, 1536), BF16), jax.ShapeDtypeStruct((lp, 2048), BF16),
                   jax.ShapeDtypeStruct((lp, 128), F32)],
        grid=(lp // t,),
        in_specs=[slab(QP_W), slab(QP_W), slab(V_HEAD), _rb(t, 128), _rb(t, 128)],
        out_specs=[_rb(t, 1536), _rb(t, 2048), _rb(t, 128)],
        name="qk_unpack_bwd", compiler_params=_params(("parallel",)))(dqs, dks, dvs, cos_t, sin_t)


def _proj_grad(proj, dcqn, dckvn, g_q, g_kv, dkr, ddt_pad, dt_bias_pad, dz, dxs, dbc):
    lp = proj.shape[0]
    t = ROW_BLOCK

    def body(ckv_ref, cq_ref, pdt_ref, dcq_ref, dckv_ref, gq_ref, gkv_ref, dkr_ref, ddt_ref, b_ref, dz_ref, dxs_ref,
             dbc_ref, dp_ref, dgq_ref, dgkv_ref, db_ref):
        dckv, dgkv = _rms_bwd(ckv_ref[...], gkv_ref[...], dckv_ref[...])
        dcq, dgq = _rms_bwd(cq_ref[...], gq_ref[...], dcq_ref[...])
        ddt_raw = ddt_ref[...] * _sigmoid(pdt_ref[...] + b_ref[...])
        dp_ref[:, SEG_KV:SEG_KV + KV_LORA] = dckv.astype(BF16)
        dp_ref[:, SEG_KR:SEG_KR + 128] = dkr_ref[...].astype(BF16)
        dp_ref[:, SEG_CQ:SEG_CQ + Q_LORA] = dcq.astype(BF16)
        dp_ref[:, SEG_DT:SEG_DT + 128] = ddt_raw.astype(BF16)
        dp_ref[:, SEG_DT + 128:SEG_Z] = jnp.zeros((t, SEG_Z - SEG_DT - 128), BF16)
        dp_ref[:, SEG_Z:SEG_XS] = dz_ref[...].astype(BF16)
        dp_ref[:, SEG_XS:SEG_BC] = dxs_ref[...].astype(BF16)
        dp_ref[:, SEG_BC:PROJ_W] = dbc_ref[...].astype(BF16)
        _acc_add(dgq_ref, dgq)
        _acc_add(dgkv_ref, dgkv)
        _acc_add(db_ref, jnp.sum(ddt_raw, axis=0, keepdims=True))

    return pl.pallas_call(
        body,
        out_shape=[jax.ShapeDtypeStruct((lp, PROJ_W), BF16), jax.ShapeDtypeStruct((1, Q_LORA), F32),
                   jax.ShapeDtypeStruct((1, KV_LORA), F32), jax.ShapeDtypeStruct((1, 128), F32)],
        grid=(lp // t,),
        in_specs=[_rb(t, KV_LORA, SEG_KV // KV_LORA), _rb(t, Q_LORA, SEG_CQ // Q_LORA), _rb(t, 128, SEG_DT // 128),
                  _rb(t, Q_LORA), _rb(t, KV_LORA), _full((1, Q_LORA)), _full((1, KV_LORA)), _rb(t, 128), _rb(t, 128),
                  _full((1, 128)), _rb(t, SSM_WIDTH), _rb(t, SSM_WIDTH), _rb(t, 512)],
        out_specs=[_rb(t, PROJ_W), _full((1, Q_LORA)), _full((1, KV_LORA)), _full((1, 128))],
        name="proj_grad", compiler_params=_params(("arbitrary",)))(
            proj, proj, proj, dcqn, dckvn, g_q, g_kv, dkr, ddt_pad, dt_bias_pad, dz, dxs, dbc)


def _pair_tables(n):
    qmaj = [(i, j) for i in range(n) for j in range(i + 1)]
    kmaj = [(i, j) for j in range(n) for i in range(j, n)]
    to = lambda ps, c: jnp.asarray(np.array([p[c] for p in ps], np.int32))
    return (to(qmaj, 0), to(qmaj, 1)), (to(kmaj, 0), to(kmaj, 1))


def _att_block(lp, edge=ATT_BLOCK):
    return edge if lp % edge == 0 else MM_BLOCK


def _nt(a, b):
    return lax.dot_general(a, b, (((1,), (1,)), ((), ())), preferred_element_type=F32)


def _attn_fwd(qs, ks, vts):
    lp = qs.shape[1]
    t = _att_block(lp, ATT_BLOCK_FWD)
    n = lp // t
    (qi, kj), _ = _pair_tables(n)
    tc = t // ATT_SPLIT

    def body(qi_ref, kj_ref, q_ref, k_ref, vt_ref, o_ref, lse_ref, m_s, l_s, acc_s):
        p = pl.program_id(1)
        i, j = qi_ref[p], kj_ref[p]

        @pl.when(j == 0)
        def _():
            m_s[...] = jnp.full_like(m_s, NEG_BIG)
            l_s[...] = jnp.zeros_like(l_s)
            acc_s[...] = jnp.zeros_like(acc_s)

        def update(masked):
            m_all, l_all, acc_all = m_s[...], l_s[...], acc_s[...]
            m_out, l_out, acc_out = [], [], []
            qk = lambda c: _nt(k_ref[0], q_ref[0, c * tc:(c + 1) * tc, :])
            ahead = qk(0)
            for c in range(ATT_SPLIT):
                cols = slice(c * tc, (c + 1) * tc)
                sc = ahead * ATT_SCALE_LOG2
                if c + 1 < ATT_SPLIT:
                    ahead = qk(c + 1)
                if masked:
                    keep = (lax.broadcasted_iota(jnp.int32, (t, tc), 1) + c * tc
                            >= lax.broadcasted_iota(jnp.int32, (t, tc), 0))
                    sc = jnp.where(keep, sc, NEG_BIG)
                m_prev = m_all[:, cols]
                m_new = jnp.maximum(m_prev, jnp.max(sc, axis=0, keepdims=True))
                alpha = jnp.exp2(m_prev - m_new)
                pexp = jnp.exp2(sc - m_new)
                l_out.append(alpha * l_all[:, cols] + jnp.sum(pexp, axis=0, keepdims=True))
                acc_out.append(alpha * acc_all[:, cols] + jnp.dot(vt_ref[0], pexp.astype(BF16),
                                                                  preferred_element_type=F32))
                m_out.append(m_new)
            cat = lambda parts: parts[0] if len(parts) == 1 else jnp.concatenate(parts, axis=1)
            m_s[...], l_s[...], acc_s[...] = cat(m_out), cat(l_out), cat(acc_out)

        @pl.when(j < i)
        def _():
            update(False)

        @pl.when(j == i)
        def _():
            update(True)
            o_ref[...] = (acc_s[...] / l_s[...]).T
            lse_ref[0] = m_s[...] + jnp.log2(l_s[...])

    grid_spec = pltpu.PrefetchScalarGridSpec(
        num_scalar_prefetch=2, grid=(ATT_HEADS, int(qi.shape[0])),
        in_specs=[pl.BlockSpec((1, t, QP_W), lambda h, p, qi, kj: (h, qi[p], 0)),
                  pl.BlockSpec((1, t, QP_W), lambda h, p, qi, kj: (h, kj[p], 0)),
                  pl.BlockSpec((1, V_HEAD, t), lambda h, p, qi, kj: (h, 0, kj[p]))],
        out_specs=[pl.BlockSpec((t, V_HEAD), lambda h, p, qi, kj: (qi[p], h)),
                   pl.BlockSpec((1, 1, t), lambda h, p, qi, kj: (h, 0, qi[p]))],
        scratch_shapes=[pltpu.VMEM((1, t), F32), pltpu.VMEM((1, t), F32), pltpu.VMEM((V_HEAD, t), F32)])
    return pl.pallas_call(
        body, grid_spec=grid_spec,
        out_shape=[jax.ShapeDtypeStruct((lp, ATT_HEADS * V_HEAD), F32), jax.ShapeDtypeStruct((ATT_HEADS, 1, lp), F32)],
        name="attn_fwd", compiler_params=_params(("parallel", "arbitrary")))(qi, kj, qs, ks, vts)


def _attn_delta(datt, att):
    lp = att.shape[0]
    t = MM_BLOCK

    def body(do_ref, o_ref, d_ref):
        d_ref[0] = jnp.sum(do_ref[...] * o_ref[...], axis=1, keepdims=True)

    blk = pl.BlockSpec((t, V_HEAD), lambda h, i: (i, h))
    cols = pl.pallas_call(
        body, out_shape=jax.ShapeDtypeStruct((ATT_HEADS, lp, 1), F32), grid=(ATT_HEADS, lp // t),
        in_specs=[blk, blk], out_specs=pl.BlockSpec((1, t, 1), lambda h, i: (h, i, 0)),
        name="attn_delta", compiler_params=_params(("parallel", "parallel")))(datt, att)
    return cols.reshape(ATT_HEADS, 1, lp)


def _attn_bwd(qs, ks, vs, datt16, lse2, delta):
    lp = qs.shape[1]
    tk = _att_block(lp)
    tq = ATT_BLOCK_Q_BWD if lp % ATT_BLOCK_Q_BWD == 0 and ATT_BLOCK_Q_BWD % tk == 0 else tk
    r = tq // tk
    nk, nq = lp // tk, lp // tq
    pairs = [(i, j) for j in range(nk) for i in range(j // r, nq)]
    qi = jnp.asarray(np.array([p[0] for p in pairs], np.int32))
    kj = jnp.asarray(np.array([p[1] for p in pairs], np.int32))
    n_pairs = len(pairs)

    def body(qi_ref, kj_ref, k_ref, v_ref, q_ref, do_ref, lse_ref, dl_ref, dq_hbm, dk_ref, dv_ref, dq_s, dk_s, dv_s,
             sem):
        h, p = pl.program_id(0), pl.program_id(1)
        i, j = qi_ref[p], kj_ref[p]
        first = i == j // r

        @pl.when(p == 0)
        def _():
            dq_s[...] = jnp.zeros_like(dq_s)

        @pl.when(first)
        def _():
            dk_s[...] = jnp.zeros_like(dk_s)
            dv_s[...] = jnp.zeros_like(dv_s)

        def step(masked):
            q = q_ref[0]
            do = do_ref[...]
            pt = jnp.exp2(_nt(k_ref[0], q) * ATT_SCALE_LOG2 - lse_ref[0])
            if masked:
                ahead = j * tk - i * tq
                keep = (lax.broadcasted_iota(jnp.int32, (tk, tq), 1)
                        >= lax.broadcasted_iota(jnp.int32, (tk, tq), 0) + ahead)
                pt = jnp.where(keep, pt, 0.0)
            dst = (pt * (_nt(v_ref[0], do) - dl_ref[0]) * ATT_SCALE).astype(BF16)
            dv_s[...] += jnp.dot(pt.astype(BF16), do, preferred_element_type=F32)
            dk_s[...] += jnp.dot(dst, q, preferred_element_type=F32)
            rows = pl.ds(pl.multiple_of(i * tq, tq), tq)
            dq_s[rows, :] += lax.dot_general(dst, k_ref[0], (((0,), (0,)), ((), ())), preferred_element_type=F32)

        @pl.when(jnp.logical_not(first))
        def _():
            step(False)

        @pl.when(first)
        def _():
            step(True)

        @pl.when(i == nq - 1)
        def _():
            dk_ref[0] = dk_s[...]
            dv_ref[0] = dv_s[...]

        @pl.when(p == n_pairs - 1)
        def _():
            out = pltpu.make_async_copy(dq_s, dq_hbm.at[h], sem)
            out.start()
            out.wait()

    grid_spec = pltpu.PrefetchScalarGridSpec(
        num_scalar_prefetch=2, grid=(ATT_HEADS, n_pairs),
        in_specs=[pl.BlockSpec((1, tk, QP_W), lambda h, p, qi, kj: (h, kj[p], 0)),
                  pl.BlockSpec((1, tk, V_HEAD), lambda h, p, qi, kj: (h, kj[p], 0)),
                  pl.BlockSpec((1, tq, QP_W), lambda h, p, qi, kj: (h, qi[p], 0)),
                  pl.BlockSpec((tq, V_HEAD), lambda h, p, qi, kj: (qi[p], h)),
                  pl.BlockSpec((1, 1, tq), lambda h, p, qi, kj: (h, 0, qi[p])),
                  pl.BlockSpec((1, 1, tq), lambda h, p, qi, kj: (h, 0, qi[p]))],
        out_specs=[pl.BlockSpec(memory_space=pl.ANY),
                   pl.BlockSpec((1, tk, QP_W), lambda h, p, qi, kj: (h, kj[p], 0)),
                   pl.BlockSpec((1, tk, V_HEAD), lambda h, p, qi, kj: (h, kj[p], 0))],
        scratch_shapes=[pltpu.VMEM((lp, QP_W), F32), pltpu.VMEM((tk, QP_W), F32), pltpu.VMEM((tk, V_HEAD), F32),
                        pltpu.SemaphoreType.DMA])
    return pl.pallas_call(
        body, grid_spec=grid_spec,
        out_shape=[jax.ShapeDtypeStruct((ATT_HEADS, lp, QP_W), F32), jax.ShapeDtypeStruct((ATT_HEADS, lp, QP_W), F32),
                   jax.ShapeDtypeStruct((ATT_HEADS, lp, V_HEAD), F32)],
        name="attn_bwd", compiler_params=_params(("arbitrary", "arbitrary")))(
            qi, kj, ks, vs, qs, datt16, lse2, delta)


N_PAIRS = SSM_HEADS // 2
HI = lax.Precision.HIGHEST


def _ssd_chunk(xp, bs, cs, dt, dt_t, alr, alc, dsk, st):
    q = dt.shape[0]
    li = lax.broadcasted_iota(jnp.int32, (q, q), 0)
    si = lax.broadcasted_iota(jnp.int32, (q, q), 1)
    tri = (si <= li).astype(F32)
    tri_t = (li <= si).astype(F32)
    lo = lax.broadcasted_iota(jnp.int32, (1, 128), 1) < 64
    h_r = lax.broadcasted_iota(jnp.int32, (1, SSM_HEADS), 1)
    h_c = lax.broadcasted_iota(jnp.int32, (SSM_HEADS, 1), 0)
    a = dt * (-jnp.exp(alr))
    a_t = dt_t * (-jnp.exp(alc))
    acum = jnp.dot(tri, a, precision=HI, preferred_element_type=F32)
    acum_t = jnp.dot(a_t, tri_t, precision=HI, preferred_element_type=F32)
    last = (lax.broadcasted_iota(jnp.int32, (q, 1), 0) == q - 1).astype(F32)
    alast = jnp.sum(acum * last, axis=0, keepdims=True)
    e = jnp.exp(acum)
    rdt = jnp.exp(alast - acum) * dt
    e_last = jnp.exp(alast)

    def col(m, h):
        return jnp.sum(m * (h_r == h).astype(F32), axis=1, keepdims=True)

    def row(m, h):
        return jnp.sum(m * (h_c == h).astype(F32), axis=0, keepdims=True)

    def pair(m, ha):
        return jnp.where(lo, col(m, ha), col(m, ha + 1))

    ys, st_new = [], []
    for g in range(2):
        c_b = cs[g].astype(BF16)
        b_b = bs[g].astype(BF16)
        cb = _nt(c_b, b_b)
        for j in range(N_PAIRS // 2):
            p = (N_PAIRS // 2) * g + j
            ha = 2 * p
            x = xp[p]
            x_b = x.astype(BF16)

            def w_of(h):
                seg = col(acum, h) - row(acum_t, h)
                return (cb * jnp.exp(jnp.minimum(seg, 0.0)) * tri * row(dt_t, h)).astype(BF16)

            y_diag = jnp.where(lo, jnp.dot(w_of(ha), x_b, preferred_element_type=F32),
                               jnp.dot(w_of(ha + 1), x_b, preferred_element_type=F32))
            y_off = jnp.dot(c_b, st[p].astype(BF16), preferred_element_type=F32) * pair(e, ha)
            ys.append(y_diag + y_off + pair(dsk, ha) * x)
            xw = (x * pair(rdt, ha)).astype(BF16)
            st_new.append(st[p] * pair(e_last, ha)
                          + lax.dot_general(b_b, xw, (((0,), (0,)), ((), ())), preferred_element_type=F32))
    return ys, st_new


def _ssd_fwd(xs, bc, dt, dt_t, alr, alc, dsk):
    lp = xs.shape[0]
    q = SSD_CHUNK
    nc = lp // q

    def body(x_ref, b_ref, c_ref, dt_ref, dtt_ref, alr_ref, alc_ref, dsk_ref, y_ref, sp_ref, st_s):
        @pl.when(pl.program_id(0) == 0)
        def _():
            st_s[...] = jnp.zeros_like(st_s)

        sp_ref[0] = st_s[...]
        xp = [x_ref[:, 128 * p:128 * (p + 1)] for p in range(N_PAIRS)]
        bs = [b_ref[:, 0:128], b_ref[:, 128:256]]
        cs = [c_ref[:, 0:128], c_ref[:, 128:256]]
        ys, st_new = _ssd_chunk(xp, bs, cs, dt_ref[...], dtt_ref[...], alr_ref[...], alc_ref[...], dsk_ref[...],
                                [st_s[p] for p in range(N_PAIRS)])
        for p in range(N_PAIRS):
            y_ref[:, 128 * p:128 * (p + 1)] = ys[p]
            st_s[p] = st_new[p]

    return pl.pallas_call(
        body,
        out_shape=[jax.ShapeDtypeStruct((lp, SSM_WIDTH), F32), jax.ShapeDtypeStruct((nc, N_PAIRS, 128, 128), F32)],
        grid=(nc,),
        in_specs=[_rb(q, SSM_WIDTH), _rb(q, 256, 0), _rb(q, 256, 1), _rb(q, SSM_HEADS),
                  pl.BlockSpec((SSM_HEADS, q), lambda i: (0, i)),
                  _full((1, SSM_HEADS)), _full((SSM_HEADS, 1)), _full((1, SSM_HEADS))],
        out_specs=[_rb(q, SSM_WIDTH), pl.BlockSpec((1, N_PAIRS, 128, 128), lambda i: (i, 0, 0, 0))],
        scratch_shapes=[pltpu.VMEM((N_PAIRS, 128, 128), F32)],
        name="ssd_fwd", compiler_params=_params(("arbitrary",)))(xs, bc, bc, dt, dt_t, alr, alc, dsk)


def _ssd_bwd(xs, bc, dt, dt_t, alr, alc, dsk, sprev, dy):
    lp = xs.shape[0]
    q = SSD_CHUNK
    nc = lp // q

    def body(x_ref, b_ref, c_ref, dt_ref, dtt_ref, alr_ref, alc_ref, dsk_ref, sp_ref, dy_ref,
             dx_ref, dbc_ref, ddt_ref, ddtt_ref, dalr_ref, dalc_ref, ddsk_ref, ds_s):
        @pl.when(pl.program_id(0) == 0)
        def _():
            ds_s[...] = jnp.zeros_like(ds_s)

        xp = [x_ref[:, 128 * p:128 * (p + 1)] for p in range(N_PAIRS)]
        bs = [b_ref[:, 0:128], b_ref[:, 128:256]]
        cs = [c_ref[:, 0:128], c_ref[:, 128:256]]
        st = [sp_ref[0, p] for p in range(N_PAIRS)]
        _, vjp = jax.vjp(_ssd_chunk, xp, bs, cs, dt_ref[...], dtt_ref[...], alr_ref[...], alc_ref[...], dsk_ref[...],
                         st)
        dys = [dy_ref[:, 128 * p:128 * (p + 1)] for p in range(N_PAIRS)]
        dxp, dbs, dcs, ddt, ddtt, dalr, dalc, ddsk, dst = vjp((dys, [ds_s[p] for p in range(N_PAIRS)]))
        for p in range(N_PAIRS):
            dx_ref[:, 128 * p:128 * (p + 1)] = dxp[p]
            ds_s[p] = dst[p]
        for g in range(2):
            dbc_ref[:, 128 * g:128 * (g + 1)] = dbs[g]
            dbc_ref[:, 256 + 128 * g:256 + 128 * (g + 1)] = dcs[g]
        ddt_ref[...] = ddt
        ddtt_ref[...] = ddtt
        _acc_add(dalr_ref, dalr)
        _acc_add(dalc_ref, dalc)
        _acc_add(ddsk_ref, ddsk)

    rev = lambda width, cb=0: pl.BlockSpec((q, width), lambda i: (nc - 1 - i, cb))
    return pl.pallas_call(
        body,
        out_shape=[jax.ShapeDtypeStruct((lp, SSM_WIDTH), F32), jax.ShapeDtypeStruct((lp, 512), F32),
                   jax.ShapeDtypeStruct((lp, SSM_HEADS), F32), jax.ShapeDtypeStruct((SSM_HEADS, lp), F32),
                   jax.ShapeDtypeStruct((1, SSM_HEADS), F32), jax.ShapeDtypeStruct((SSM_HEADS, 1), F32),
                   jax.ShapeDtypeStruct((1, SSM_HEADS), F32)],
        grid=(nc,),
        in_specs=[rev(SSM_WIDTH), rev(256, 0), rev(256, 1), rev(SSM_HEADS),
                  pl.BlockSpec((SSM_HEADS, q), lambda i: (0, nc - 1 - i)),
                  _full((1, SSM_HEADS)), _full((SSM_HEADS, 1)), _full((1, SSM_HEADS)),
                  pl.BlockSpec((1, N_PAIRS, 128, 128), lambda i: (nc - 1 - i, 0, 0, 0)), rev(SSM_WIDTH)],
        out_specs=[rev(SSM_WIDTH), rev(512), rev(SSM_HEADS), pl.BlockSpec((SSM_HEADS, q), lambda i: (0, nc - 1 - i)),
                   _full((1, SSM_HEADS)), _full((SSM_HEADS, 1)), _full((1, SSM_HEADS))],
        scratch_shapes=[pltpu.VMEM((N_PAIRS, 128, 128), F32)],
        name="ssd_bwd", compiler_params=_params(("arbitrary",)))(xs, bc, bc, dt, dt_t, alr, alc, dsk, sprev, dy)


def _q_to_slab_order(w):
    hd = QK_NOPE + QK_ROPE
    nope = [w[:, h * hd:h * hd + QK_NOPE] for h in range(ATT_HEADS)]
    rope = [w[:, h * hd + QK_NOPE:(h + 1) * hd] for h in range(ATT_HEADS)]
    return jnp.concatenate(nope + rope, axis=1)


def _q_from_slab_order(wp):
    base = ATT_HEADS * QK_NOPE
    parts = []
    for h in range(ATT_HEADS):
        parts += [wp[:, QK_NOPE * h:QK_NOPE * (h + 1)], wp[:, base + QK_ROPE * h:base + QK_ROPE * (h + 1)]]
    return jnp.concatenate(parts, axis=1)


_IN_CQ, _IN_CKV, _IN_KR, _IN_Z, _IN_XS, _IN_BC, _IN_DT = (0, 384), (384, 640), (640, 704), (704, 1728), (1728, 2752), \
    (2752, 3264), (3264, 3280)


def _pack_w_in(w):
    z = lambda n: jnp.zeros((w.shape[0], n), w.dtype)
    s = lambda r: w[:, r[0]:r[1]]
    return jnp.concatenate([s(_IN_CKV), s(_IN_KR), z(64), s(_IN_CQ), s(_IN_DT), z(112), z(128), s(_IN_Z), s(_IN_XS),
                            s(_IN_BC)], axis=1)


def _unpack_w_in(wp):
    s = lambda off, n: wp[:, off:off + n]
    return jnp.concatenate([s(SEG_CQ, 384), s(SEG_KV, 256), s(SEG_KR, 64), s(SEG_Z, 1024), s(SEG_XS, 1024),
                            s(SEG_BC, 512), s(SEG_DT, 16)], axis=1)


def _rope_tables(lp):
    inv_freq = ROPE_THETA ** (-jnp.arange(0, QK_ROPE, 2, dtype=F32) / QK_ROPE)
    ang = jnp.arange(lp, dtype=F32)[:, None] * inv_freq[None, :]
    cos, sin = jnp.cos(ang), jnp.sin(ang)
    return jnp.tile(cos, (1, 4)), jnp.concatenate([-sin, sin, -sin, sin], axis=1)


def _local_step(x, tgt, w):
    n_real = x.shape[0]
    l = N_META + n_real
    lp = -(-l // MM_BLOCK) * MM_BLOCK
    h0 = jnp.concatenate([w["meta_tokens"], x, jnp.zeros((lp - l, D_MODEL), F32)], axis=0)
    cos_t, sin_t = _rope_tables(lp)

    w_in_p = _pack_w_in(w["w_in"])
    w_q_p = _q_to_slab_order(w["w_q_up"])
    w_kv, w_out, w_up, w_down = w["w_kv_up"], w["w_out"], w["w_mlp_up"], w["w_mlp_down"]
    conv_w, conv_b = w["conv_w"], w["conv_b"]
    dt_bias_pad = jnp.concatenate([w["dt_bias"], jnp.zeros((1, 128 - SSM_HEADS), F32)], axis=1)
    alr, dsk = w["a_log"], w["d_skip"]
    alc = alr.reshape(SSM_HEADS, 1)

    n1 = _norm_in(h0, w["norm_mix_pre"])
    proj = _mm("proj", n1, w_in_p)
    cqn, ckvn, krr = _attn_prep(proj, w["q_a_norm"], w["kv_a_norm"], cos_t, sin_t)
    q = _mm("q_up", cqn, w_q_p)
    kv = _mm("kv_up", ckvn, w_kv, outs=((BF16, None),))
    qs, ks, vs, vts = _qk_pack(q, kv, krr, cos_t, sin_t)
    att, lse2 = _attn_fwd(qs, ks, vts)
    xs_pre, xs_act = _conv_fwd("conv_xs_fwd", proj, SEG_XS, SSM_WIDTH, conv_w[:, :SSM_WIDTH], conv_b[:, :SSM_WIDTH])
    bc_pre, bc_act = _conv_fwd("conv_bc_fwd", proj, SEG_BC, 512, conv_w[:, SSM_WIDTH:], conv_b[:, SSM_WIDTH:])
    dt = _dt_fwd(proj, dt_bias_pad)[:, :SSM_HEADS]
    dt_t = dt.T
    y, sprev = _ssd_fwd(xs_act, bc_act, dt, dt_t, alr, alc, dsk)
    ssm = _gated_norm_fwd(y, proj, w["ssm_norm"])
    cat = jnp.concatenate([att.astype(BF16), ssm], axis=1)
    mix = _mm("out_proj", cat, w_out)
    h1, n2 = _mix_residual(h0, mix, w["norm_mix_post"], w["norm_mlp_pre"])
    relu2 = lambda r: jnp.square(jnp.maximum(r, 0.0))
    u, act = _mm("mlp_up", n2, w_up, outs=((F32, None), (BF16, relu2)))
    f = _mm("mlp_down", act, w_down)
    loss, dh2, df, dg_mlp_post = _loss_and_grad(h1, f, w["norm_mlp_post"], tgt, n_real)

    g = {"norm_mlp_post": dg_mlp_post}
    g["w_mlp_down"] = _mm_tn("d_w_mlp_down", act, df)
    du = _mm("d_mlp_act", df, w_down, outs=((BF16, lambda r, ub: r * (2.0 * jnp.maximum(ub, 0.0))),), epi_ins=(u,),
             trans_b=True)
    g["w_mlp_up"] = _mm_tn("d_w_mlp_up", n2, du)
    dn2 = _mm("d_n2", du, w_up, trans_b=True)
    dh1, dmix, g["norm_mlp_pre"], g["norm_mix_post"] = _mlp_residual_bwd(dh2, dn2, h1, w["norm_mlp_pre"], mix,
                                                                         w["norm_mix_post"])
    g["w_out"] = _mm_tn("d_w_out", cat, dmix)
    dcat, dcat16 = _mm("d_cat", dmix, w_out, outs=((F32, None), (BF16, None)), trans_b=True)
    dy, dz, g["ssm_norm"] = _gated_norm_bwd(y, proj, w["ssm_norm"], dcat)
    dxs_act, dbc_act, ddt, ddt_t, dalr, dalc, g["d_skip"] = _ssd_bwd(xs_act, bc_act, dt, dt_t, alr, alc, dsk, sprev, dy)
    g["a_log"] = dalr + dalc.reshape(1, SSM_HEADS)
    dxs, dcw_xs, dcb_xs = _conv_bwd("conv_xs_bwd", dxs_act, xs_pre, proj, SEG_XS, SSM_WIDTH, conv_w[:, :SSM_WIDTH])
    dbc, dcw_bc, dcb_bc = _conv_bwd("conv_bc_bwd", dbc_act, bc_pre, proj, SEG_BC, 512, conv_w[:, SSM_WIDTH:])
    g["conv_w"] = jnp.concatenate([dcw_xs[:CONV_K], dcw_bc[:CONV_K]], axis=1)
    g["conv_b"] = jnp.concatenate([dcb_xs, dcb_bc], axis=1)
    ddt_pad = jnp.concatenate([ddt + ddt_t.T, jnp.zeros((lp, 128 - SSM_HEADS), F32)], axis=1)

    dqs, dks, dvs = _attn_bwd(qs, ks, vs, dcat16, lse2, _attn_delta(dcat, att))
    dq, dkv, dkr = _qk_unpack_bwd(dqs, dks, dvs, cos_t, sin_t)
    g["w_q_up"] = _q_from_slab_order(_mm_tn("d_w_q_up", cqn, dq))
    g["w_kv_up"] = _mm_tn("d_w_kv_up", ckvn, dkv)
    dcqn = _mm("d_cqn", dq, w_q_p, trans_b=True)
    dckvn = _mm("d_ckvn", dkv, w_kv, trans_b=True)
    dproj, g["q_a_norm"], g["kv_a_norm"], ddtb = _proj_grad(proj, dcqn, dckvn, w["q_a_norm"], w["kv_a_norm"], dkr,
                                                          ddt_pad, dt_bias_pad, dz, dxs, dbc)
    g["dt_bias"] = ddtb[:, :SSM_HEADS]
    g["w_in"] = _unpack_w_in(_mm_tn("d_w_in", n1, dproj))
    dn1 = _mm("d_n1", dproj, w_in_p, trans_b=True)
    dh0, g["norm_mix_pre"] = _input_norm_bwd(dh1, dn1, h0, w["norm_mix_pre"])
    g["meta_tokens"] = dh0[:N_META]
    return loss, dh0, g


WEIGHTS = ["meta_tokens", "norm_mix_pre", "w_in", "q_a_norm", "w_q_up", "kv_a_norm", "w_kv_up", "conv_w", "conv_b",
           "dt_bias", "a_log", "d_skip", "ssm_norm", "w_out", "norm_mix_post", "norm_mlp_pre", "w_mlp_up",
           "w_mlp_down", "norm_mlp_post"]
SHARD_AXIS = {"meta_tokens": 1, "w_in": 1, "w_q_up": 1, "w_kv_up": 1, "conv_w": 1, "w_out": 0, "w_mlp_up": 1,
              "w_mlp_down": 0}
FULL_SHAPE = {"meta_tokens": (16, 1024), "norm_mix_pre": (1, 1024), "w_in": (1024, 3280), "q_a_norm": (1, 384),
              "w_q_up": (384, 1536), "kv_a_norm": (1, 256), "w_kv_up": (256, 2048), "conv_w": (4, 1536),
              "conv_b": (1, 1536), "dt_bias": (1, 16), "a_log": (1, 16), "d_skip": (1, 16), "ssm_norm": (1, 1024),
              "w_out": (2048, 1024), "norm_mix_post": (1, 1024), "norm_mlp_pre": (1, 1024), "w_mlp_up": (1024, 4096),
              "w_mlp_down": (4096, 1024), "norm_mlp_post": (1, 1024)}
GATHER_BF16 = ["w_in", "w_q_up", "w_kv_up", "w_out", "w_mlp_up", "w_mlp_down"]
GATHER_F32 = ["meta_tokens", "conv_w"]
ADAM_NATURAL = GATHER_BF16


def _shard_shape(name):
    shp = list(FULL_SHAPE[name])
    if name in SHARD_AXIS:
        shp[SHARD_AXIS[name]] //= N_CHIPS
    return tuple(shp)


PACK_ORDER = sorted(WEIGHTS, key=lambda n: -_shard_shape(n)[0])


def _packed_rows(shape):
    r, c = shape
    return r if c <= PACK_W else -(-c // PACK_W)


def _pack_rows(arrays, row_multiple):
    parts = []
    for a in arrays:
        r, c = a.shape
        if c > PACK_W:
            assert r == 1, a.shape
            folded = _packed_rows(a.shape)
            a = jnp.pad(a, ((0, 0), (0, folded * PACK_W - c))).reshape(folded, PACK_W)
        elif c < PACK_W:
            a = jnp.pad(a, ((0, 0), (0, PACK_W - c)))
        parts.append(a)
    rows = sum(p.shape[0] for p in parts)
    if rows % row_multiple:
        parts.append(jnp.zeros((row_multiple - rows % row_multiple, PACK_W), parts[0].dtype))
    return jnp.concatenate(parts, axis=0)


def _unpack_rows(packed, shapes):
    out, off = [], 0
    for r, c in shapes:
        nr = _packed_rows((r, c))
        blk = packed[off:off + nr]
        out.append(blk[:, :c] if c <= PACK_W else blk.reshape(1, nr * PACK_W)[:, :c])
        off += nr
    return out


def _chip_slice(full, name, t):
    if name not in SHARD_AXIS:
        return full
    ax = SHARD_AXIS[name]
    n = FULL_SHAPE[name][ax] // N_CHIPS
    return lax.slice_in_dim(full, t * n, (t + 1) * n, axis=ax)


HBM_SPEC = pl.BlockSpec(memory_space=pl.ANY)
CHIP_FLIPS = ((1, 0), (0, 1), (1, 1))


def _gather_chips(bufs):
    nb = len(bufs)

    def body(*refs):
        ins, outs = refs[:nb], refs[nb:2 * nb]
        send, recv, loc = refs[2 * nb:]
        x, y, c = lax.axis_index("x"), lax.axis_index("y"), lax.axis_index("c")
        me = 2 * x + y
        sibling = (x, y, 1 - c)
        sends, forwards = [], []
        for b in range(nb):
            half = bufs[b].shape[0] // 2
            mine = pl.ds(c * half, half)
            own = pltpu.make_async_copy(ins[b], outs[b].at[me], loc.at[b])
            own.start()
            sends.append(own)
            for k, (fx, fy) in enumerate(CHIP_FLIPS):
                cp = pltpu.make_async_remote_copy(
                    src_ref=ins[b].at[mine], dst_ref=outs[b].at[me, mine], send_sem=send.at[b, k],
                    recv_sem=recv.at[b, k], device_id=(x ^ fx, y ^ fy, c), device_id_type=MESH_ID)
                cp.start()
                sends.append(cp)
        for b in range(nb):
            half = bufs[b].shape[0] // 2
            mine, theirs = pl.ds(c * half, half), pl.ds((1 - c) * half, half)
            for k, (fx, fy) in enumerate(CHIP_FLIPS):
                chip = 2 * (x ^ fx) + (y ^ fy)
                landed = outs[b].at[chip, mine]
                pltpu.make_async_remote_copy(src_ref=landed, dst_ref=landed, send_sem=send.at[b, k],
                                             recv_sem=recv.at[b, k], device_id=sibling,
                                             device_id_type=MESH_ID).wait_recv()
                fw = pltpu.make_async_remote_copy(src_ref=landed, dst_ref=landed, send_sem=send.at[b, 3 + k],
                                                  recv_sem=recv.at[b, 3 + k], device_id=sibling,
                                                  device_id_type=MESH_ID)
                fw.start()
                forwards.append((fw, outs[b].at[chip, theirs], b, k))
        for fw, arriving, b, k in forwards:
            pltpu.make_async_remote_copy(src_ref=arriving, dst_ref=arriving, send_sem=send.at[b, 3 + k],
                                         recv_sem=recv.at[b, 3 + k], device_id=sibling,
                                         device_id_type=MESH_ID).wait_recv()
            fw.wait_send()
        for cp in sends[1::4] + sends[2::4] + sends[3::4]:
            cp.wait_send()
        for own in sends[0::4]:
            own.wait()

    return pl.pallas_call(
        body,
        out_shape=[jax.ShapeDtypeStruct((N_CHIPS,) + b.shape, b.dtype) for b in bufs],
        in_specs=[HBM_SPEC] * nb, out_specs=[HBM_SPEC] * nb,
        scratch_shapes=[pltpu.SemaphoreType.DMA((nb, 6)), pltpu.SemaphoreType.DMA((nb, 6)),
                        pltpu.SemaphoreType.DMA((nb,))],
        name="gather_chips")(*bufs)


def _sibling_swap(name, buf):
    def body(src, dst, send, recv):
        x, y, c = lax.axis_index("x"), lax.axis_index("y"), lax.axis_index("c")
        cp = pltpu.make_async_remote_copy(src_ref=src, dst_ref=dst, send_sem=send, recv_sem=recv,
                                          device_id=(x, y, 1 - c), device_id_type=MESH_ID)
        cp.start()
        cp.wait()

    return pl.pallas_call(
        body, out_shape=jax.ShapeDtypeStruct(buf.shape, buf.dtype), in_specs=[HBM_SPEC], out_specs=HBM_SPEC,
        scratch_shapes=[pltpu.SemaphoreType.DMA, pltpu.SemaphoreType.DMA], name=name)(buf)


def _scatter_chips(parts):
    nb = len(parts)

    def body(*refs):
        srcs, dsts = refs[:nb], refs[nb:2 * nb]
        send, recv = refs[2 * nb:]
        x, y, c = lax.axis_index("x"), lax.axis_index("y"), lax.axis_index("c")
        copies = []
        for b in range(nb):
            for k, (fx, fy) in enumerate(CHIP_FLIPS):
                tx, ty = x ^ fx, y ^ fy
                cp = pltpu.make_async_remote_copy(
                    src_ref=srcs[b].at[2 * tx + ty], dst_ref=dsts[b].at[k], send_sem=send.at[b, k],
                    recv_sem=recv.at[b, k], device_id=(tx, ty, c), device_id_type=MESH_ID)
                cp.start()
                copies.append(cp)
        for cp in copies:
            cp.wait()

    return pl.pallas_call(
        body, out_shape=[jax.ShapeDtypeStruct((3,) + p.shape[1:], p.dtype) for p in parts], in_specs=[HBM_SPEC] * nb,
        out_specs=[HBM_SPEC] * nb,
        scratch_shapes=[pltpu.SemaphoreType.DMA((nb, 3)), pltpu.SemaphoreType.DMA((nb, 3))],
        name="scatter_chips")(*parts)


def _add_rows(name, terms, also_bf16=False):
    rows = terms[0].shape[0]
    t = _row_tile(rows)
    n_out = 2 if also_bf16 else 1

    def body(*refs):
        acc = refs[0][...].astype(F32)
        for r in refs[1:-n_out]:
            acc = acc + r[...].astype(F32)
        refs[-n_out][...] = acc
        if also_bf16:
            refs[-1][...] = acc.astype(BF16)

    res = pl.pallas_call(
        body, out_shape=[jax.ShapeDtypeStruct(terms[0].shape, dt) for dt in (F32, BF16)[:n_out]], grid=(rows // t,),
        in_specs=[_rb(t, PACK_W)] * len(terms), out_specs=[_rb(t, PACK_W)] * n_out,
        name=name, compiler_params=_params(("parallel",)))(*terms)
    return res if also_bf16 else res[0]


def _sum_chip_order(name, parts, received, me):
    stack = jnp.concatenate([lax.dynamic_index_in_dim(parts, me, axis=0, keepdims=True), received], axis=0)
    terms = []
    for chip in range(N_CHIPS):
        xr = me ^ chip
        where = jnp.where(xr == 0, 0, jnp.where(xr == 2, 1, jnp.where(xr == 1, 2, 3)))
        terms.append(lax.dynamic_index_in_dim(stack, where, axis=0, keepdims=False))
    return _add_rows(name, terms)


def _row_tile(rows):
    assert rows % 16 == 0, rows
    return max(t for t in range(16, 513, 16) if rows % t == 0)


def _adamw(name, g, w, m, v):
    rows, cols = g.shape
    t = max(tt for tt in range(8, 257, 8) if rows % tt == 0)
    c1 = 1.0 - ADAM_B1 ** ADAM_STEP
    c2 = 1.0 - ADAM_B2 ** ADAM_STEP

    def body(g_ref, w_ref, m_ref, v_ref, d_ref, mo_ref, vo_ref):
        gg = g_ref[...]
        mn = ADAM_B1 * m_ref[...] + (1.0 - ADAM_B1) * gg
        vn = ADAM_B2 * v_ref[...] + (1.0 - ADAM_B2) * (gg * gg)
        d_ref[...] = -ADAM_LR * ((mn / c1) / (jnp.sqrt(vn / c2) + ADAM_EPS) + ADAM_WD * w_ref[...])
        mo_ref[...] = mn
        vo_ref[...] = vn

    return pl.pallas_call(
        body, out_shape=[jax.ShapeDtypeStruct(g.shape, F32)] * 3, grid=(rows // t,),
        in_specs=[_rb(t, cols)] * 4, out_specs=[_rb(t, cols)] * 3,
        name=name, compiler_params=_params(("parallel",)))(g, w, m, v)


def kernel(x, meta_tokens, norm_mix_pre, w_in, q_a_norm, w_q_up, kv_a_norm, w_kv_up, conv_w, conv_b, dt_bias, a_log, d_skip, ssm_norm, w_out, norm_mix_post, norm_mlp_pre, w_mlp_up, w_mlp_down, norm_mlp_post, loss_target, m_meta_tokens, m_norm_mix_pre, m_w_in, m_q_a_norm, m_w_q_up, m_kv_a_norm, m_w_kv_up, m_conv_w, m_conv_b, m_dt_bias, m_a_log, m_d_skip, m_ssm_norm, m_w_out, m_norm_mix_post, m_norm_mlp_pre, m_w_mlp_up, m_w_mlp_down, m_norm_mlp_post, v_meta_tokens, v_norm_mix_pre, v_w_in, v_q_a_norm, v_w_q_up, v_kv_a_norm, v_w_kv_up, v_conv_w, v_conv_b, v_dt_bias, v_a_log, v_d_skip, v_ssm_norm, v_w_out, v_norm_mix_post, v_norm_mlp_pre, v_w_mlp_up, v_w_mlp_down, v_norm_mlp_post):
    given = dict(locals())
    drop = lambda name, a: a[0] if a.ndim == 3 else a
    w_loc = {n: drop(n, given[n]) for n in WEIGHTS}
    m_loc = {n: drop(n, given["m_" + n]) for n in WEIGHTS}
    v_loc = {n: drop(n, given["v_" + n]) for n in WEIGHTS}
    ix, iy, ic = lax.axis_index("x"), lax.axis_index("y"), lax.axis_index("c")
    me = 2 * ix + iy

    sent16 = _pack_rows([w_loc[n].astype(BF16) for n in GATHER_BF16], 32)
    sent32 = _pack_rows([w_loc[n] for n in GATHER_F32], 16)
    got16, got32 = _gather_chips([sent16, sent32])
    w_full = {n: w_loc[n] for n in WEIGHTS if n not in SHARD_AXIS}
    for names, got in ((GATHER_BF16, got16), (GATHER_F32, got32)):
        per_chip = [_unpack_rows(got[t], [_shard_shape(n) for n in names]) for t in range(N_CHIPS)]
        for k, n in enumerate(names):
            w_full[n] = jnp.concatenate([per_chip[t][k] for t in range(N_CHIPS)], axis=SHARD_AXIS[n])

    loss, dh0, g_full = _local_step(x[0], loss_target[0], w_full)
    n_real = x.shape[1]
    grad_x = dh0[N_META:N_META + n_real][None]

    shapes = [_shard_shape(n) for n in PACK_ORDER]
    slots = [_pack_rows([_chip_slice(g_full[n], n, t) for n in PACK_ORDER], 16) for t in range(N_CHIPS)]
    rows = slots[0].shape[0]
    half = rows // 2
    halves = lambda hh: jnp.concatenate([lax.dynamic_slice_in_dim(s, hh * half, half, axis=0) for s in slots], axis=0)
    keep, give = halves(ic), halves(1 - ic)
    from_sibling = _sibling_swap("sibling_swap", give)
    part32, part16 = _add_rows("chip_partial", [keep, from_sibling], also_bf16=True)
    part32 = part32.reshape(N_CHIPS, half, PACK_W)
    part16 = part16.reshape(N_CHIPS, half, PACK_W)
    tail_start = rows - sum(_packed_rows(_shard_shape(n)) for n in PACK_ORDER if n not in SHARD_AXIS)
    assert tail_start + TAIL_ROWS >= rows and all(n in SHARD_AXIS for n in PACK_ORDER[:len(SHARD_AXIS)])
    tail32 = part32[:, half - TAIL_ROWS:, :]
    from16, from_tail = _scatter_chips([part16, tail32])
    my_half = _sum_chip_order("chip_total", part16, from16, me)
    my_tail = _sum_chip_order("chip_total_tail", tail32, from_tail, me)
    my_half = lax.dynamic_update_slice(my_half, my_tail, (half - TAIL_ROWS, 0))
    other_half = _sibling_swap("sibling_gather", my_half)
    g_red = jnp.where(ic == 0, jnp.concatenate([my_half, other_half], axis=0),
                      jnp.concatenate([other_half, my_half], axis=0))

    g_own = dict(zip(PACK_ORDER, _unpack_rows(g_red, shapes)))
    small = [n for n in PACK_ORDER if n not in ADAM_NATURAL]
    pack_small = lambda d: _pack_rows([d[n] for n in small], 16)
    packed_upd = _adamw("adamw_small", pack_small(g_own), pack_small(w_loc), pack_small(m_loc), pack_small(v_loc))
    upd = [dict(zip(small, _unpack_rows(p, [_shard_shape(n) for n in small]))) for p in packed_upd]
    for n in ADAM_NATURAL:
        for k, res in enumerate(_adamw("adamw_" + n, g_own[n], w_loc[n], m_loc[n], v_loc[n])):
            upd[k][n] = res

    def outputs(parts):
        return [parts[n][None] if given[n].ndim == 3 else parts[n] for n in WEIGHTS]

    total = lax.psum(loss[0, 0], ("x", "y", "c"))
    return (total, grad_x, *outputs(g_own), *outputs(upd[0]), *outputs(upd[1]), *outputs(upd[2]))
```

```python
import functools

import numpy as np
import jax
import jax.numpy as jnp
from jax import lax
from jax.experimental import pallas as pl
from jax.experimental.pallas import tpu as pltpu

F32 = jnp.float32
BF16 = jnp.bfloat16

D_MODEL = 1024
N_META = 16
EPS = 1e-6
ATT_HEADS = 8
Q_LORA = 384
KV_LORA = 256
QK_NOPE = 128
QK_ROPE = 64
V_HEAD = 128
ROPE_THETA = 10000.0
SSM_HEADS = 16
SSM_HEAD_DIM = 64
SSM_WIDTH = 1024
SSM_STATE = 128
CONV_K = 4
D_FF = 4096
ATT_SCALE = float((QK_NOPE + QK_ROPE) ** -0.5)
ATT_SCALE_LOG2 = float(ATT_SCALE * np.log2(np.e))

ADAM_LR = 0.001
ADAM_B1 = 0.9
ADAM_B2 = 0.999
ADAM_EPS = 1e-08
ADAM_WD = 0.01
ADAM_STEP = 10

SEG_KV, SEG_KR, SEG_CQ, SEG_DT, SEG_Z, SEG_XS, SEG_BC = 0, 256, 384, 768, 1024, 2048, 3072
PROJ_W = 3584
QP_W = 256

ROW_BLOCK = 512
MM_BLOCK = 512
SSD_CHUNK = 256
ATT_SPLIT = 1
ATT_BLOCK = 768
ATT_BLOCK_FWD = 1536
ATT_BLOCK_Q_BWD = 1536
VMEM_LIMIT = 56 * 1024 * 1024
NEG_BIG = -1e30

PACK_W = 1024
TAIL_ROWS = 16
N_CHIPS = 4
MESH_ID = pl.DeviceIdType.MESH


def _params(sem):
    return pltpu.CompilerParams(dimension_semantics=sem, vmem_limit_bytes=VMEM_LIMIT)


def _rb(rows, width, cb=0):
    return pl.BlockSpec((rows, width), lambda i: (i, cb))


def _full(shape):
    zeros = (0,) * len(shape)
    return pl.BlockSpec(shape, lambda i: zeros)


def _acc_add(ref, val):
    first = pl.program_id(0) == 0

    @pl.when(first)
    def _():
        ref[...] = val

    @pl.when(jnp.logical_not(first))
    def _():
        ref[...] += val


def _rms(x, g):
    r = lax.rsqrt(jnp.mean(x * x, axis=-1, keepdims=True) + EPS)
    return x * r * g


def _rms_bwd(x, g, dy):
    r = lax.rsqrt(jnp.mean(x * x, axis=-1, keepdims=True) + EPS)
    dyg = dy * g
    dx = r * dyg - x * (r * r * r) * jnp.mean(x * dyg, axis=-1, keepdims=True)
    dg = jnp.sum(dy * x * r, axis=0, keepdims=True)
    return dx, dg


def _sigmoid(x):
    return 1.0 / (1.0 + jnp.exp(-x))


def _swap32(x):
    lane = lax.broadcasted_iota(jnp.int32, x.shape, 1)
    return jnp.where((lane % 64) < 32, pltpu.roll(x, 96, 1), pltpu.roll(x, 32, 1))


def _rope(x, cos_t, sin_t):
    return x * cos_t + _swap32(x) * sin_t


def _rope_bwd(dr, cos_t, sin_t):
    return dr * cos_t + _swap32(dr * sin_t)


def _tile(n, cap):
    if n <= cap:
        return n
    best = 128
    for t in range(128, cap + 1, 128):
        if n % t == 0:
            best = t
    assert n % best == 0, (n, cap)
    return best


MM_VMEM_BUDGET = 40 * 1024 * 1024
TN_ROWS_CAP = 1536


def _mm(name, a, b, outs=((F32, None),), epi_ins=(), trans_b=False):
    a_parts = tuple(a) if isinstance(a, (tuple, list)) else (a,)
    assert len(a_parts) == 1 or not trans_b
    m = a_parts[0].shape[0]
    widths = [p.shape[1] for p in a_parts]
    k = sum(widths)
    n = b.shape[0] if trans_b else b.shape[1]
    tm = MM_BLOCK
    n_a, n_epi = len(a_parts), len(epi_ins)
    out_bytes = sum(jnp.dtype(dt).itemsize for dt, _ in outs) + sum(e.dtype.itemsize for e in epi_ins)
    a_bytes = sum(p.shape[1] * p.dtype.itemsize for p in a_parts)
    step_bytes = lambda tn: 2 * (tm * a_bytes + k * tn * b.dtype.itemsize + tm * tn * out_bytes)
    tn = n
    while step_bytes(tn) > MM_VMEM_BUDGET and tn % 256 == 0:
        tn //= 2
    assert n % tn == 0 and step_bytes(tn) <= MM_VMEM_BUDGET, (name, n, tn)

    def body(*refs):
        a_refs, b_ref = refs[:n_a], refs[n_a]
        epi_refs = refs[n_a + 1:n_a + 1 + n_epi]
        out_refs = refs[n_a + 1 + n_epi:]
        if trans_b:
            r = _nt(a_refs[0][...].astype(BF16), b_ref[...].astype(BF16))
        else:
            r, off = None, 0
            for a_ref, wd in zip(a_refs, widths):
                part = jnp.dot(a_ref[...].astype(BF16), b_ref[off:off + wd, :].astype(BF16),
                               preferred_element_type=F32)
                r = part if r is None else r + part
                off += wd
        blocks = [e[...] for e in epi_refs]
        for o_ref, (dt, fn) in zip(out_refs, outs):
            o_ref[...] = (r if fn is None else fn(r, *blocks)).astype(dt)

    out_spec = pl.BlockSpec((tm, tn), lambda j, i: (i, j))
    b_spec = pl.BlockSpec((tn, k), lambda j, i: (j, 0)) if trans_b else pl.BlockSpec((k, tn), lambda j, i: (0, j))
    res = pl.pallas_call(
        body,
        out_shape=[jax.ShapeDtypeStruct((m, n), dt) for dt, _ in outs],
        grid=(n // tn, m // tm),
        in_specs=[pl.BlockSpec((tm, wd), lambda j, i: (i, 0)) for wd in widths] + [b_spec] + [out_spec] * n_epi,
        out_specs=[out_spec] * len(outs),
        name=name,
        compiler_params=_params(("parallel", "parallel")),
    )(*a_parts, b, *epi_ins)
    return res[0] if len(outs) == 1 else res


def _mm_tn(name, x, dy, ta_cap=1024, tn_cap=1024):
    l, a = x.shape
    n = dy.shape[1]
    ta, tn = _tile(a, ta_cap), _tile(n, tn_cap)
    tl = max(t for t in range(MM_BLOCK, TN_ROWS_CAP + 1, MM_BLOCK) if l % t == 0)
    nl = l // tl

    def body(x_ref, dy_ref, o_ref):
        ll = pl.program_id(2)

        @pl.when(ll == 0)
        def _():
            o_ref[...] = jnp.zeros_like(o_ref)

        o_ref[...] += lax.dot_general(x_ref[...].astype(BF16), dy_ref[...].astype(BF16), (((0,), (0,)), ((), ())),
                                      preferred_element_type=F32)

    return pl.pallas_call(
        body,
        out_shape=jax.ShapeDtypeStruct((a, n), F32),
        grid=(a // ta, n // tn, nl),
        in_specs=[pl.BlockSpec((tl, ta), lambda i, j, ll: (ll, i)), pl.BlockSpec((tl, tn), lambda i, j, ll: (ll, j))],
        out_specs=pl.BlockSpec((ta, tn), lambda i, j, ll: (i, j)),
        name=name,
        compiler_params=_params(("parallel", "parallel", "arbitrary")),
    )(x, dy)


def _norm_in(h0, g_pre):
    lp = h0.shape[0]
    t = ROW_BLOCK

    def body(h_ref, g_ref, o_ref):
        o_ref[...] = _rms(h_ref[...], g_ref[...]).astype(BF16)

    return pl.pallas_call(
        body, out_shape=jax.ShapeDtypeStruct((lp, D_MODEL), BF16), grid=(lp // t,),
        in_specs=[_rb(t, D_MODEL), _full((1, D_MODEL))], out_specs=_rb(t, D_MODEL),
        name="norm_in", compiler_params=_params(("parallel",)))(h0, g_pre)


def _attn_prep(proj, g_q, g_kv, cos_t, sin_t):
    lp = proj.shape[0]
    t = ROW_BLOCK

    def body(ckv_ref, kr_ref, cq_ref, gq_ref, gkv_ref, cos_ref, sin_ref, cqn_ref, ckvn_ref, krr_ref):
        cqn_ref[...] = _rms(cq_ref[...], gq_ref[...]).astype(BF16)
        ckvn_ref[...] = _rms(ckv_ref[...], gkv_ref[...]).astype(BF16)
        roped = _rope(kr_ref[...], cos_ref[...], sin_ref[...])
        krr_ref[...] = roped + pltpu.roll(roped, 64, 1)

    return pl.pallas_call(
        body,
        out_shape=[jax.ShapeDtypeStruct((lp, Q_LORA), BF16), jax.ShapeDtypeStruct((lp, KV_LORA), BF16),
                   jax.ShapeDtypeStruct((lp, 128), F32)],
        grid=(lp // t,),
        in_specs=[_rb(t, KV_LORA, SEG_KV // KV_LORA), _rb(t, 128, SEG_KR // 128), _rb(t, Q_LORA, SEG_CQ // Q_LORA),
                  _full((1, Q_LORA)), _full((1, KV_LORA)), _rb(t, 128), _rb(t, 128)],
        out_specs=[_rb(t, Q_LORA), _rb(t, KV_LORA), _rb(t, 128)],
        name="attn_prep", compiler_params=_params(("parallel",)))(proj, proj, proj, g_q, g_kv, cos_t, sin_t)


def _qk_pack(q, kv, krr, cos_t, sin_t):
    lp = q.shape[0]
    t = ROW_BLOCK

    def body(q_ref, kv_ref, krr_ref, cos_ref, sin_ref, qs_ref, ks_ref, vs_ref, vts_ref):
        lane = lax.broadcasted_iota(jnp.int32, (t, 128), 1)
        lo = lane < 64
        krr = krr_ref[...].astype(BF16)
        for j in range(ATT_HEADS // 2):
            pr = _rope(q_ref[:, 1024 + 128 * j:1024 + 128 * (j + 1)], cos_ref[...], sin_ref[...])
            for h, keep in ((2 * j, lo), (2 * j + 1, jnp.logical_not(lo))):
                qs_ref[h, :, 0:128] = q_ref[:, 128 * h:128 * (h + 1)].astype(BF16)
                qs_ref[h, :, 128:256] = jnp.where(keep, pr, 0.0).astype(BF16)
        for h in range(ATT_HEADS):
            ks_ref[h, :, 0:128] = kv_ref[:, 256 * h:256 * h + 128].astype(BF16)
            ks_ref[h, :, 128:256] = krr
            v = kv_ref[:, 256 * h + 128:256 * (h + 1)]
            vs_ref[h] = v.astype(BF16)
            vts_ref[h] = v.astype(F32).T.astype(BF16)

    slab = lambda w: pl.BlockSpec((ATT_HEADS, t, w), lambda i: (0, i, 0))
    return pl.pallas_call(
        body,
        out_shape=[jax.ShapeDtypeStruct((ATT_HEADS, lp, QP_W), BF16), jax.ShapeDtypeStruct((ATT_HEADS, lp, QP_W), BF16),
                   jax.ShapeDtypeStruct((ATT_HEADS, lp, V_HEAD), BF16), jax.ShapeDtypeStruct((ATT_HEADS, V_HEAD, lp), BF16)],
        grid=(lp // t,),
        in_specs=[_rb(t, 1536), _rb(t, 2048), _rb(t, 128), _rb(t, 128), _rb(t, 128)],
        out_specs=[slab(QP_W), slab(QP_W), slab(V_HEAD), pl.BlockSpec((ATT_HEADS, V_HEAD, t), lambda i: (0, 0, i))],
        name="qk_pack", compiler_params=_params(("parallel",)))(q, kv, krr, cos_t, sin_t)


def _shifted(ext, t, shift):
    if shift == 0:
        return ext[8:, :]
    return pltpu.roll(ext, shift, 0)[8:, :]


def _conv_fwd(name, proj, seg, width, conv_w, conv_b):
    lp = proj.shape[0]
    t = ROW_BLOCK
    cb = seg // width

    def body(u_ref, halo_ref, w_ref, b_ref, pre_ref, act_ref):
        i = pl.program_id(0)
        u = u_ref[...]
        halo = jnp.where(i > 0, halo_ref[...], 0.0)
        ext = jnp.concatenate([halo, u], axis=0)
        pre = jnp.broadcast_to(b_ref[...], (t, width))
        for k in range(CONV_K):
            pre = pre + w_ref[k:k + 1, :] * _shifted(ext, t, CONV_K - 1 - k)
        pre_ref[...] = pre
        act_ref[...] = pre * _sigmoid(pre)

    return pl.pallas_call(
        body,
        out_shape=[jax.ShapeDtypeStruct((lp, width), F32)] * 2,
        grid=(lp // t,),
        in_specs=[_rb(t, width, cb),
                  pl.BlockSpec((8, width), lambda i: (jnp.maximum(i * (t // 8) - 1, 0), cb)),
                  _full((CONV_K, width)), _full((1, width))],
        out_specs=[_rb(t, width), _rb(t, width)],
        name=name, compiler_params=_params(("parallel",)))(proj, proj, conv_w, conv_b)


def _softplus(x):
    return jnp.maximum(x, 0.0) + jnp.log1p(jnp.exp(-jnp.abs(x)))


def _dt_fwd(proj, dt_bias_pad):
    lp = proj.shape[0]
    t = ROW_BLOCK

    def body(x_ref, b_ref, o_ref):
        o_ref[...] = _softplus(x_ref[...] + b_ref[...])

    return pl.pallas_call(
        body, out_shape=jax.ShapeDtypeStruct((lp, 128), F32), grid=(lp // t,),
        in_specs=[_rb(t, 128, SEG_DT // 128), _full((1, 128))], out_specs=_rb(t, 128),
        name="dt_fwd", compiler_params=_params(("parallel",)))(proj, dt_bias_pad)


def _gated_norm_group(y, z, w):
    g = y * (z * _sigmoid(z))
    return g * lax.rsqrt(jnp.mean(g * g, axis=-1, keepdims=True) + EPS) * w


def _gated_norm_fwd(y, proj, w):
    lp = y.shape[0]
    t = ROW_BLOCK
    gw = SSM_WIDTH // 2

    def body(y0, y1, z0, z1, w0, w1, o_ref):
        o_ref[:, 0:gw] = _gated_norm_group(y0[...], z0[...], w0[...]).astype(BF16)
        o_ref[:, gw:] = _gated_norm_group(y1[...], z1[...], w1[...]).astype(BF16)

    zb = SEG_Z // gw
    return pl.pallas_call(
        body, out_shape=jax.ShapeDtypeStruct((lp, SSM_WIDTH), BF16), grid=(lp // t,),
        in_specs=[_rb(t, gw, 0), _rb(t, gw, 1), _rb(t, gw, zb), _rb(t, gw, zb + 1),
                  pl.BlockSpec((1, gw), lambda i: (0, 0)), pl.BlockSpec((1, gw), lambda i: (0, 1))],
        out_specs=_rb(t, SSM_WIDTH),
        name="gated_norm_fwd", compiler_params=_params(("parallel",)))(y, y, proj, proj, w, w)


def _gated_norm_bwd(y, proj, w, dssm):
    lp = y.shape[0]
    t = ROW_BLOCK
    gw = SSM_WIDTH // 2

    def body(y0, y1, z0, z1, w0, w1, d0, d1, dy_ref, dz_ref, dw_ref):
        dws = []
        for g, (yr, zr, wr, dr) in enumerate(((y0, z0, w0, d0), (y1, z1, w1, d1))):
            _, vjp = jax.vjp(_gated_norm_group, yr[...], zr[...], wr[...])
            dyg, dzg, dwg = vjp(dr[...])
            dy_ref[:, g * gw:(g + 1) * gw] = dyg
            dz_ref[:, g * gw:(g + 1) * gw] = dzg
            dws.append(dwg)
        _acc_add(dw_ref, jnp.concatenate(dws, axis=1))

    zb = SEG_Z // gw
    return pl.pallas_call(
        body,
        out_shape=[jax.ShapeDtypeStruct((lp, SSM_WIDTH), F32), jax.ShapeDtypeStruct((lp, SSM_WIDTH), F32),
                   jax.ShapeDtypeStruct((1, SSM_WIDTH), F32)],
        grid=(lp // t,),
        in_specs=[_rb(t, gw, 0), _rb(t, gw, 1), _rb(t, gw, zb), _rb(t, gw, zb + 1),
                  pl.BlockSpec((1, gw), lambda i: (0, 0)), pl.BlockSpec((1, gw), lambda i: (0, 1)),
                  _rb(t, gw, 2), _rb(t, gw, 3)],
        out_specs=[_rb(t, SSM_WIDTH), _rb(t, SSM_WIDTH), _full((1, SSM_WIDTH))],
        name="gated_norm_bwd", compiler_params=_params(("arbitrary",)))(y, y, proj, proj, w, w, dssm, dssm)


def _mix_residual(h0, mix, g_post, g_mlp_pre):
    lp = h0.shape[0]
    t = ROW_BLOCK

    def body(h_ref, m_ref, gp_ref, gm_ref, h1_ref, n2_ref):
        h1 = h_ref[...] + _rms(m_ref[...], gp_ref[...])
        h1_ref[...] = h1
        n2_ref[...] = _rms(h1, gm_ref[...]).astype(BF16)

    return pl.pallas_call(
        body, out_shape=[jax.ShapeDtypeStruct((lp, D_MODEL), F32), jax.ShapeDtypeStruct((lp, D_MODEL), BF16)],
        grid=(lp // t,),
        in_specs=[_rb(t, D_MODEL), _rb(t, D_MODEL), _full((1, D_MODEL)), _full((1, D_MODEL))],
        out_specs=[_rb(t, D_MODEL), _rb(t, D_MODEL)],
        name="mix_residual", compiler_params=_params(("parallel",)))(h0, mix, g_post, g_mlp_pre)


def _loss_and_grad(h1, f, g_post, tgt, n_real):
    lp = h1.shape[0]
    t = ROW_BLOCK
    assert n_real % t == 0 and t % N_META == 0
    n_tb = n_real // t

    def body(h1_ref, f_ref, g_ref, halo_ref, t_ref, loss_ref, dh2_ref, df_ref, dg_ref):
        i = pl.program_id(0)
        fx = f_ref[...]
        h2 = h1_ref[...] + _rms(fx, g_ref[...])
        row = i * t + lax.broadcasted_iota(jnp.int32, (t, 1), 0)
        real = jnp.logical_and(row >= N_META, row < N_META + n_real)
        target = jnp.concatenate([halo_ref[...], t_ref[0:t - N_META, :]], axis=0)
        diff = jnp.where(real, h2 - target, 0.0)
        part = 0.5 * jnp.sum(jnp.sum(diff * diff, axis=-1, keepdims=True) / D_MODEL, axis=0, keepdims=True)
        _acc_add(loss_ref, jnp.broadcast_to(part, (1, 128)))
        dh2 = diff / D_MODEL
        dh2_ref[...] = dh2
        dfx, dg = _rms_bwd(fx, g_ref[...], dh2)
        df_ref[...] = dfx.astype(BF16)
        _acc_add(dg_ref, dg)

    return pl.pallas_call(
        body,
        out_shape=[jax.ShapeDtypeStruct((1, 128), F32), jax.ShapeDtypeStruct((lp, D_MODEL), F32),
                   jax.ShapeDtypeStruct((lp, D_MODEL), BF16), jax.ShapeDtypeStruct((1, D_MODEL), F32)],
        grid=(lp // t,),
        in_specs=[_rb(t, D_MODEL), _rb(t, D_MODEL), _full((1, D_MODEL)),
                  pl.BlockSpec((N_META, D_MODEL),
                               lambda i: (jnp.clip(i * (t // N_META) - 1, 0, n_real // N_META - 1), 0)),
                  pl.BlockSpec((t, D_MODEL), lambda i: (jnp.minimum(i, n_tb - 1), 0))],
        out_specs=[_full((1, 128)), _rb(t, D_MODEL), _rb(t, D_MODEL), _full((1, D_MODEL))],
        name="loss_and_grad", compiler_params=_params(("arbitrary",)))(h1, f, g_post, tgt, tgt)


def _mlp_residual_bwd(dh2, dn2, h1, g_mlp_pre, mix, g_post):
    lp = h1.shape[0]
    t = ROW_BLOCK

    def body(dh2_ref, dn2_ref, h1_ref, gm_ref, mix_ref, gp_ref, dh1_ref, dmix_ref, dgm_ref, dgp_ref):
        dx, dgm = _rms_bwd(h1_ref[...], gm_ref[...], dn2_ref[...])
        dh1 = dh2_ref[...] + dx
        dh1_ref[...] = dh1
        dmix, dgp = _rms_bwd(mix_ref[...], gp_ref[...], dh1)
        dmix_ref[...] = dmix.astype(BF16)
        _acc_add(dgm_ref, dgm)
        _acc_add(dgp_ref, dgp)

    return pl.pallas_call(
        body,
        out_shape=[jax.ShapeDtypeStruct((lp, D_MODEL), F32), jax.ShapeDtypeStruct((lp, D_MODEL), BF16),
                   jax.ShapeDtypeStruct((1, D_MODEL), F32), jax.ShapeDtypeStruct((1, D_MODEL), F32)],
        grid=(lp // t,),
        in_specs=[_rb(t, D_MODEL), _rb(t, D_MODEL), _rb(t, D_MODEL), _full((1, D_MODEL)), _rb(t, D_MODEL),
                  _full((1, D_MODEL))],
        out_specs=[_rb(t, D_MODEL), _rb(t, D_MODEL), _full((1, D_MODEL)), _full((1, D_MODEL))],
        name="mlp_residual_bwd", compiler_params=_params(("arbitrary",)))(dh2, dn2, h1, g_mlp_pre, mix, g_post)


def _input_norm_bwd(dh1, dn1, h0, g_pre):
    lp = h0.shape[0]
    t = ROW_BLOCK

    def body(dh1_ref, dn1_ref, h0_ref, g_ref, dh0_ref, dg_ref):
        dx, dg = _rms_bwd(h0_ref[...], g_ref[...], dn1_ref[...])
        dh0_ref[...] = dh1_ref[...] + dx
        _acc_add(dg_ref, dg)

    return pl.pallas_call(
        body, out_shape=[jax.ShapeDtypeStruct((lp, D_MODEL), F32), jax.ShapeDtypeStruct((1, D_MODEL), F32)],
        grid=(lp // t,),
        in_specs=[_rb(t, D_MODEL), _rb(t, D_MODEL), _rb(t, D_MODEL), _full((1, D_MODEL))],
        out_specs=[_rb(t, D_MODEL), _full((1, D_MODEL))],
        name="input_norm_bwd", compiler_params=_params(("arbitrary",)))(dh1, dn1, h0, g_pre)


def _conv_bwd(name, dact, pre, proj, seg, width, conv_w):
    lp = proj.shape[0]
    t = ROW_BLOCK
    cb = seg // width
    nblk = lp // t

    def dsilu(p):
        s = _sigmoid(p)
        return s * (1.0 + p * (1.0 - s))

    def body(da_ref, dan_ref, pre_ref, pren_ref, u_ref, halo_ref, w_ref, du_ref, dw_ref, db_ref):
        i = pl.program_id(0)
        dpre = da_ref[...] * dsilu(pre_ref[...])
        dpre_next = jnp.where(i < nblk - 1, dan_ref[...] * dsilu(pren_ref[...]), 0.0)
        extd = jnp.concatenate([dpre, dpre_next], axis=0)
        halo = jnp.where(i > 0, halo_ref[...], 0.0)
        ext = jnp.concatenate([halo, u_ref[...]], axis=0)
        du = jnp.zeros((t, width), F32)
        dws = []
        for k in range(CONV_K):
            m = CONV_K - 1 - k
            ahead = dpre if m == 0 else pltpu.roll(extd, t + 8 - m, 0)[:t, :]
            du = du + w_ref[k:k + 1, :] * ahead
            dws.append(jnp.sum(dpre * _shifted(ext, t, m), axis=0, keepdims=True))
        du_ref[...] = du
        _acc_add(dw_ref, jnp.concatenate(dws + [jnp.zeros((8 - CONV_K, width), F32)], axis=0))
        _acc_add(db_ref, jnp.sum(dpre, axis=0, keepdims=True))

    nxt = lambda i: (jnp.minimum((i + 1) * (t // 8), lp // 8 - 1), 0)
    return pl.pallas_call(
        body,
        out_shape=[jax.ShapeDtypeStruct((lp, width), F32), jax.ShapeDtypeStruct((8, width), F32),
                   jax.ShapeDtypeStruct((1, width), F32)],
        grid=(nblk,),
        in_specs=[_rb(t, width), pl.BlockSpec((8, width), nxt), _rb(t, width), pl.BlockSpec((8, width), nxt),
                  _rb(t, width, cb),
                  pl.BlockSpec((8, width), lambda i: (jnp.maximum(i * (t // 8) - 1, 0), cb)),
                  _full((CONV_K, width))],
        out_specs=[_rb(t, width), _full((8, width)), _full((1, width))],
        name=name, compiler_params=_params(("arbitrary",)))(dact, dact, pre, pre, proj, proj, conv_w)


def _qk_unpack_bwd(dqs, dks, dvs, cos_t, sin_t):
    lp = dqs.shape[1]
    t = ROW_BLOCK

    def body(dqs_ref, dks_ref, dvs_ref, cos_ref, sin_ref, dq_ref, dkv_ref, dkr_ref):
        lane = lax.broadcasted_iota(jnp.int32, (t, 128), 1)
        lo = lane < 64
        for j in range(ATT_HEADS // 2):
            dpr = jnp.where(lo, dqs_ref[2 * j, :, 128:256], dqs_ref[2 * j + 1, :, 128:256])
            dq_ref[:, 1024 + 128 * j:1024 + 128 * (j + 1)] = _rope_bwd(dpr, cos_ref[...], sin_ref[...]).astype(BF16)
        dkrr = jnp.zeros((t, 128), F32)
        for h in range(ATT_HEADS):
            dq_ref[:, 128 * h:128 * (h + 1)] = dqs_ref[h, :, 0:128].astype(BF16)
            dkv_ref[:, 256 * h:256 * h + 128] = dks_ref[h, :, 0:128].astype(BF16)
            dkv_ref[:, 256 * h + 128:256 * (h + 1)] = dvs_ref[h].astype(BF16)
            dkrr = dkrr + dks_ref[h, :, 128:256]
        droped = jnp.where(lo, dkrr + pltpu.roll(dkrr, 64, 1), 0.0)
        dkr_ref[...] = _rope_bwd(droped, cos_ref[...], sin_ref[...])

    slab = lambda w: pl.BlockSpec((ATT_HEADS, t, w), lambda i: (0, i, 0))
    return pl.pallas_call(
        body,
        out_shape=[jax.ShapeDtypeStruct((lp, 1536), BF16), jax.ShapeDtypeStruct((lp, 2048), BF16),
                   jax.ShapeDtypeStruct((lp, 128), F32)],
        grid=(lp // t,),
        in_specs=[slab(QP_W), slab(QP_W), slab(V_HEAD), _rb(t, 128), _rb(t, 128)],
        out_specs=[_rb(t, 1536), _rb(t, 2048), _rb(t, 128)],
        name="qk_unpack_bwd", compiler_params=_params(("parallel",)))(dqs, dks, dvs, cos_t, sin_t)


def _proj_grad(proj, dcqn, dckvn, g_q, g_kv, dkr, ddt_pad, dt_bias_pad, dz, dxs, dbc):
    lp = proj.shape[0]
    t = ROW_BLOCK

    def body(ckv_ref, cq_ref, pdt_ref, dcq_ref, dckv_ref, gq_ref, gkv_ref, dkr_ref, ddt_ref, b_ref, dz_ref, dxs_ref,
             dbc_ref, dp_ref, dgq_ref, dgkv_ref, db_ref):
        dckv, dgkv = _rms_bwd(ckv_ref[...], gkv_ref[...], dckv_ref[...])
        dcq, dgq = _rms_bwd(cq_ref[...], gq_ref[...], dcq_ref[...])
        ddt_raw = ddt_ref[...] * _sigmoid(pdt_ref[...] + b_ref[...])
        dp_ref[:, SEG_KV:SEG_KV + KV_LORA] = dckv.astype(BF16)
        dp_ref[:, SEG_KR:SEG_KR + 128] = dkr_ref[...].astype(BF16)
        dp_ref[:, SEG_CQ:SEG_CQ + Q_LORA] = dcq.astype(BF16)
        dp_ref[:, SEG_DT:SEG_DT + 128] = ddt_raw.astype(BF16)
        dp_ref[:, SEG_DT + 128:SEG_Z] = jnp.zeros((t, SEG_Z - SEG_DT - 128), BF16)
        dp_ref[:, SEG_Z:SEG_XS] = dz_ref[...].astype(BF16)
        dp_ref[:, SEG_XS:SEG_BC] = dxs_ref[...].astype(BF16)
        dp_ref[:, SEG_BC:PROJ_W] = dbc_ref[...].astype(BF16)
        _acc_add(dgq_ref, dgq)
        _acc_add(dgkv_ref, dgkv)
        _acc_add(db_ref, jnp.sum(ddt_raw, axis=0, keepdims=True))

    return pl.pallas_call(
        body,
        out_shape=[jax.ShapeDtypeStruct((lp, PROJ_W), BF16), jax.ShapeDtypeStruct((1, Q_LORA), F32),
                   jax.ShapeDtypeStruct((1, KV_LORA), F32), jax.ShapeDtypeStruct((1, 128), F32)],
        grid=(lp // t,),
        in_specs=[_rb(t, KV_LORA, SEG_KV // KV_LORA), _rb(t, Q_LORA, SEG_CQ // Q_LORA), _rb(t, 128, SEG_DT // 128),
                  _rb(t, Q_LORA), _rb(t, KV_LORA), _full((1, Q_LORA)), _full((1, KV_LORA)), _rb(t, 128), _rb(t, 128),
                  _full((1, 128)), _rb(t, SSM_WIDTH), _rb(t, SSM_WIDTH), _rb(t, 512)],
        out_specs=[_rb(t, PROJ_W), _full((1, Q_LORA)), _full((1, KV_LORA)), _full((1, 128))],
        name="proj_grad", compiler_params=_params(("arbitrary",)))(
            proj, proj, proj, dcqn, dckvn, g_q, g_kv, dkr, ddt_pad, dt_bias_pad, dz, dxs, dbc)


def _pair_tables(n):
    qmaj = [(i, j) for i in range(n) for j in range(i + 1)]
    kmaj = [(i, j) for j in range(n) for i in range(j, n)]
    to = lambda ps, c: jnp.asarray(np.array([p[c] for p in ps], np.int32))
    return (to(qmaj, 0), to(qmaj, 1)), (to(kmaj, 0), to(kmaj, 1))


def _att_block(lp, edge=ATT_BLOCK):
    return edge if lp % edge == 0 else MM_BLOCK


def _nt(a, b):
    return lax.dot_general(a, b, (((1,), (1,)), ((), ())), preferred_element_type=F32)


def _attn_fwd(qs, ks, vts):
    lp = qs.shape[1]
    t = _att_block(lp, ATT_BLOCK_FWD)
    n = lp // t
    (qi, kj), _ = _pair_tables(n)
    tc = t // ATT_SPLIT

    def body(qi_ref, kj_ref, q_ref, k_ref, vt_ref, o_ref, o16_ref, lse_ref, m_s, l_s, acc_s):
        p = pl.program_id(1)
        i, j = qi_ref[p], kj_ref[p]

        @pl.when(j == 0)
        def _():
            m_s[...] = jnp.full_like(m_s, NEG_BIG)
            l_s[...] = jnp.zeros_like(l_s)
            acc_s[...] = jnp.zeros_like(acc_s)

        def update(masked):
            m_all, l_all, acc_all = m_s[...], l_s[...], acc_s[...]
            m_out, l_out, acc_out = [], [], []
            qk = lambda c: _nt(k_ref[0], q_ref[0, c * tc:(c + 1) * tc, :])
            ahead = qk(0)
            for c in range(ATT_SPLIT):
                cols = slice(c * tc, (c + 1) * tc)
                sc = ahead * ATT_SCALE_LOG2
                if c + 1 < ATT_SPLIT:
                    ahead = qk(c + 1)
                if masked:
                    keep = (lax.broadcasted_iota(jnp.int32, (t, tc), 1) + c * tc
                            >= lax.broadcasted_iota(jnp.int32, (t, tc), 0))
                    sc = jnp.where(keep, sc, NEG_BIG)
                m_prev = m_all[:, cols]
                m_new = jnp.maximum(m_prev, jnp.max(sc, axis=0, keepdims=True))
                alpha = jnp.exp2(m_prev - m_new)
                pexp = jnp.exp2(sc - m_new)
                l_out.append(alpha * l_all[:, cols] + jnp.sum(pexp, axis=0, keepdims=True))
                acc_out.append(alpha * acc_all[:, cols] + jnp.dot(vt_ref[0], pexp.astype(BF16),
                                                                  preferred_element_type=F32))
                m_out.append(m_new)
            cat = lambda parts: parts[0] if len(parts) == 1 else jnp.concatenate(parts, axis=1)
            m_s[...], l_s[...], acc_s[...] = cat(m_out), cat(l_out), cat(acc_out)

        @pl.when(j < i)
        def _():
            update(False)

        @pl.when(j == i)
        def _():
            update(True)
            out = (acc_s[...] / l_s[...]).T
            o_ref[...] = out
            o16_ref[...] = out.astype(BF16)
            lse_ref[0] = m_s[...] + jnp.log2(l_s[...])

    grid_spec = pltpu.PrefetchScalarGridSpec(
        num_scalar_prefetch=2, grid=(ATT_HEADS, int(qi.shape[0])),
        in_specs=[pl.BlockSpec((1, t, QP_W), lambda h, p, qi, kj: (h, qi[p], 0)),
                  pl.BlockSpec((1, t, QP_W), lambda h, p, qi, kj: (h, kj[p], 0)),
                  pl.BlockSpec((1, V_HEAD, t), lambda h, p, qi, kj: (h, 0, kj[p]))],
        out_specs=[pl.BlockSpec((t, V_HEAD), lambda h, p, qi, kj: (qi[p], h)),
                   pl.BlockSpec((t, V_HEAD), lambda h, p, qi, kj: (qi[p], h)),
                   pl.BlockSpec((1, 1, t), lambda h, p, qi, kj: (h, 0, qi[p]))],
        scratch_shapes=[pltpu.VMEM((1, t), F32), pltpu.VMEM((1, t), F32), pltpu.VMEM((V_HEAD, t), F32)])
    return pl.pallas_call(
        body, grid_spec=grid_spec,
        out_shape=[jax.ShapeDtypeStruct((lp, ATT_HEADS * V_HEAD), F32),
                   jax.ShapeDtypeStruct((lp, ATT_HEADS * V_HEAD), BF16), jax.ShapeDtypeStruct((ATT_HEADS, 1, lp), F32)],
        name="attn_fwd", compiler_params=_params(("parallel", "arbitrary")))(qi, kj, qs, ks, vts)


def _attn_delta(datt, att):
    lp = att.shape[0]
    t = MM_BLOCK
    w = ATT_HEADS * V_HEAD

    def body(do_ref, o_ref, d_ref):
        ones = jnp.ones((8, V_HEAD), BF16)
        for h in range(ATT_HEADS):
            cols = slice(h * V_HEAD, (h + 1) * V_HEAD)
            prod = do_ref[:, cols] * o_ref[:, cols]
            hi = prod.astype(BF16)
            lo = (prod - hi.astype(F32)).astype(BF16)
            d_ref[h] = (_nt(ones, hi) + _nt(ones, lo))[0:1, :]

    return pl.pallas_call(
        body, out_shape=jax.ShapeDtypeStruct((ATT_HEADS, 1, lp), F32), grid=(lp // t,),
        in_specs=[_rb(t, w), _rb(t, w)], out_specs=pl.BlockSpec((ATT_HEADS, 1, t), lambda i: (0, 0, i)),
        name="attn_delta", compiler_params=_params(("parallel",)))(datt, att)


def _attn_bwd(qs, ks, vs, datt16, lse2, delta):
    lp = qs.shape[1]
    tk = _att_block(lp)
    tq = ATT_BLOCK_Q_BWD if lp % ATT_BLOCK_Q_BWD == 0 and ATT_BLOCK_Q_BWD % tk == 0 else tk
    r = tq // tk
    nk, nq = lp // tk, lp // tq
    pairs = [(i, j) for j in range(nk) for i in range(j // r, nq)]
    qi = jnp.asarray(np.array([p[0] for p in pairs], np.int32))
    kj = jnp.asarray(np.array([p[1] for p in pairs], np.int32))
    n_pairs = len(pairs)

    def body(qi_ref, kj_ref, k_ref, v_ref, q_ref, do_ref, lse_ref, dl_ref, dq_hbm, dk_ref, dv_ref, dq_s, dk_s, dv_s,
             sem):
        h, p = pl.program_id(0), pl.program_id(1)
        i, j = qi_ref[p], kj_ref[p]
        first = i == j // r

        @pl.when(p == 0)
        def _():
            dq_s[...] = jnp.zeros_like(dq_s)

        @pl.when(first)
        def _():
            dk_s[...] = jnp.zeros_like(dk_s)
            dv_s[...] = jnp.zeros_like(dv_s)

        def step(masked):
            q = q_ref[0]
            do = do_ref[...]
            pt = jnp.exp2(_nt(k_ref[0], q) * ATT_SCALE_LOG2 - lse_ref[0])
            if masked:
                ahead = j * tk - i * tq
                keep = (lax.broadcasted_iota(jnp.int32, (tk, tq), 1)
                        >= lax.broadcasted_iota(jnp.int32, (tk, tq), 0) + ahead)
                pt = jnp.where(keep, pt, 0.0)
            dst = (pt * (_nt(v_ref[0], do) - dl_ref[0]) * ATT_SCALE).astype(BF16)
            dv_s[...] += jnp.dot(pt.astype(BF16), do, preferred_element_type=F32)
            dk_s[...] += jnp.dot(dst, q, preferred_element_type=F32)
            rows = pl.ds(pl.multiple_of(i * tq, tq), tq)
            dq_s[rows, :] += lax.dot_general(dst, k_ref[0], (((0,), (0,)), ((), ())), preferred_element_type=F32)

        @pl.when(jnp.logical_not(first))
        def _():
            step(False)

        @pl.when(first)
        def _():
            step(True)

        @pl.when(i == nq - 1)
        def _():
            dk_ref[0] = dk_s[...]
            dv_ref[0] = dv_s[...]

        @pl.when(p == n_pairs - 1)
        def _():
            out = pltpu.make_async_copy(dq_s, dq_hbm.at[h], sem)
            out.start()
            out.wait()

    grid_spec = pltpu.PrefetchScalarGridSpec(
        num_scalar_prefetch=2, grid=(ATT_HEADS, n_pairs),
        in_specs=[pl.BlockSpec((1, tk, QP_W), lambda h, p, qi, kj: (h, kj[p], 0)),
                  pl.BlockSpec((1, tk, V_HEAD), lambda h, p, qi, kj: (h, kj[p], 0)),
                  pl.BlockSpec((1, tq, QP_W), lambda h, p, qi, kj: (h, qi[p], 0)),
                  pl.BlockSpec((tq, V_HEAD), lambda h, p, qi, kj: (qi[p], h)),
                  pl.BlockSpec((1, 1, tq), lambda h, p, qi, kj: (h, 0, qi[p])),
                  pl.BlockSpec((1, 1, tq), lambda h, p, qi, kj: (h, 0, qi[p]))],
        out_specs=[pl.BlockSpec(memory_space=pl.ANY),
                   pl.BlockSpec((1, tk, QP_W), lambda h, p, qi, kj: (h, kj[p], 0)),
                   pl.BlockSpec((1, tk, V_HEAD), lambda h, p, qi, kj: (h, kj[p], 0))],
        scratch_shapes=[pltpu.VMEM((lp, QP_W), F32), pltpu.VMEM((tk, QP_W), F32), pltpu.VMEM((tk, V_HEAD), F32),
                        pltpu.SemaphoreType.DMA])
    return pl.pallas_call(
        body, grid_spec=grid_spec,
        out_shape=[jax.ShapeDtypeStruct((ATT_HEADS, lp, QP_W), F32), jax.ShapeDtypeStruct((ATT_HEADS, lp, QP_W), F32),
                   jax.ShapeDtypeStruct((ATT_HEADS, lp, V_HEAD), F32)],
        name="attn_bwd", compiler_params=_params(("arbitrary", "arbitrary")))(
            qi, kj, ks, vs, qs, datt16, lse2, delta)


N_PAIRS = SSM_HEADS // 2
HI = lax.Precision.HIGHEST


def _ssd_chunk(xp, bs, cs, dt, dt_t, alr, alc, dsk, st):
    q = dt.shape[0]
    li = lax.broadcasted_iota(jnp.int32, (q, q), 0)
    si = lax.broadcasted_iota(jnp.int32, (q, q), 1)
    tri = (si <= li).astype(F32)
    tri_t = (li <= si).astype(F32)
    lo = lax.broadcasted_iota(jnp.int32, (1, 128), 1) < 64
    h_r = lax.broadcasted_iota(jnp.int32, (1, SSM_HEADS), 1)
    h_c = lax.broadcasted_iota(jnp.int32, (SSM_HEADS, 1), 0)
    a = dt * (-jnp.exp(alr))
    a_t = dt_t * (-jnp.exp(alc))
    acum = jnp.dot(tri, a, precision=HI, preferred_element_type=F32)
    acum_t = jnp.dot(a_t, tri_t, precision=HI, preferred_element_type=F32)
    last = (lax.broadcasted_iota(jnp.int32, (q, 1), 0) == q - 1).astype(F32)
    alast = jnp.sum(acum * last, axis=0, keepdims=True)
    e = jnp.exp(acum)
    rdt = jnp.exp(alast - acum) * dt
    e_last = jnp.exp(alast)

    def col(m, h):
        return jnp.sum(m * (h_r == h).astype(F32), axis=1, keepdims=True)

    def row(m, h):
        return jnp.sum(m * (h_c == h).astype(F32), axis=0, keepdims=True)

    def pair(m, ha):
        return jnp.where(lo, col(m, ha), col(m, ha + 1))

    ys, st_new = [], []
    for g in range(2):
        c_b = cs[g].astype(BF16)
        b_b = bs[g].astype(BF16)
        cb = _nt(c_b, b_b)
        for j in range(N_PAIRS // 2):
            p = (N_PAIRS // 2) * g + j
            ha = 2 * p
            x = xp[p]
            x_b = x.astype(BF16)

            def w_of(h):
                seg = col(acum, h) - row(acum_t, h)
                return (cb * jnp.exp(jnp.minimum(seg, 0.0)) * tri * row(dt_t, h)).astype(BF16)

            y_diag = jnp.where(lo, jnp.dot(w_of(ha), x_b, preferred_element_type=F32),
                               jnp.dot(w_of(ha + 1), x_b, preferred_element_type=F32))
            y_off = jnp.dot(c_b, st[p].astype(BF16), preferred_element_type=F32) * pair(e, ha)
            ys.append(y_diag + y_off + pair(dsk, ha) * x)
            xw = (x * pair(rdt, ha)).astype(BF16)
            st_new.append(st[p] * pair(e_last, ha)
                          + lax.dot_general(b_b, xw, (((0,), (0,)), ((), ())), preferred_element_type=F32))
    return ys, st_new


def _ssd_fwd(xs, bc, dt, dt_t, alr, alc, dsk):
    lp = xs.shape[0]
    q = SSD_CHUNK
    nc = lp // q

    def body(x_ref, b_ref, c_ref, dt_ref, dtt_ref, alr_ref, alc_ref, dsk_ref, y_ref, sp_ref, st_s):
        @pl.when(pl.program_id(0) == 0)
        def _():
            st_s[...] = jnp.zeros_like(st_s)

        sp_ref[0] = st_s[...]
        xp = [x_ref[:, 128 * p:128 * (p + 1)] for p in range(N_PAIRS)]
        bs = [b_ref[:, 0:128], b_ref[:, 128:256]]
        cs = [c_ref[:, 0:128], c_ref[:, 128:256]]
        ys, st_new = _ssd_chunk(xp, bs, cs, dt_ref[...], dtt_ref[...], alr_ref[...], alc_ref[...], dsk_ref[...],
                                [st_s[p] for p in range(N_PAIRS)])
        for p in range(N_PAIRS):
            y_ref[:, 128 * p:128 * (p + 1)] = ys[p]
            st_s[p] = st_new[p]

    return pl.pallas_call(
        body,
        out_shape=[jax.ShapeDtypeStruct((lp, SSM_WIDTH), F32), jax.ShapeDtypeStruct((nc, N_PAIRS, 128, 128), F32)],
        grid=(nc,),
        in_specs=[_rb(q, SSM_WIDTH), _rb(q, 256, 0), _rb(q, 256, 1), _rb(q, SSM_HEADS),
                  pl.BlockSpec((SSM_HEADS, q), lambda i: (0, i)),
                  _full((1, SSM_HEADS)), _full((SSM_HEADS, 1)), _full((1, SSM_HEADS))],
        out_specs=[_rb(q, SSM_WIDTH), pl.BlockSpec((1, N_PAIRS, 128, 128), lambda i: (i, 0, 0, 0))],
        scratch_shapes=[pltpu.VMEM((N_PAIRS, 128, 128), F32)],
        name="ssd_fwd", compiler_params=_params(("arbitrary",)))(xs, bc, bc, dt, dt_t, alr, alc, dsk)


def _ssd_bwd(xs, bc, dt, dt_t, alr, alc, dsk, sprev, dy):
    lp = xs.shape[0]
    q = SSD_CHUNK
    nc = lp // q

    def body(x_ref, b_ref, c_ref, dt_ref, dtt_ref, alr_ref, alc_ref, dsk_ref, sp_ref, dy_ref,
             dx_ref, dbc_ref, ddt_ref, ddtt_ref, dalr_ref, dalc_ref, ddsk_ref, ds_s):
        @pl.when(pl.program_id(0) == 0)
        def _():
            ds_s[...] = jnp.zeros_like(ds_s)

        xp = [x_ref[:, 128 * p:128 * (p + 1)] for p in range(N_PAIRS)]
        bs = [b_ref[:, 0:128], b_ref[:, 128:256]]
        cs = [c_ref[:, 0:128], c_ref[:, 128:256]]
        st = [sp_ref[0, p] for p in range(N_PAIRS)]
        _, vjp = jax.vjp(_ssd_chunk, xp, bs, cs, dt_ref[...], dtt_ref[...], alr_ref[...], alc_ref[...], dsk_ref[...],
                         st)
        dys = [dy_ref[:, 128 * p:128 * (p + 1)] for p in range(N_PAIRS)]
        dxp, dbs, dcs, ddt, ddtt, dalr, dalc, ddsk, dst = vjp((dys, [ds_s[p] for p in range(N_PAIRS)]))
        for p in range(N_PAIRS):
            dx_ref[:, 128 * p:128 * (p + 1)] = dxp[p]
            ds_s[p] = dst[p]
        for g in range(2):
            dbc_ref[:, 128 * g:128 * (g + 1)] = dbs[g]
            dbc_ref[:, 256 + 128 * g:256 + 128 * (g + 1)] = dcs[g]
        ddt_ref[...] = ddt
        ddtt_ref[...] = ddtt
        _acc_add(dalr_ref, dalr)
        _acc_add(dalc_ref, dalc)
        _acc_add(ddsk_ref, ddsk)

    rev = lambda width, cb=0: pl.BlockSpec((q, width), lambda i: (nc - 1 - i, cb))
    return pl.pallas_call(
        body,
        out_shape=[jax.ShapeDtypeStruct((lp, SSM_WIDTH), F32), jax.ShapeDtypeStruct((lp, 512), F32),
                   jax.ShapeDtypeStruct((lp, SSM_HEADS), F32), jax.ShapeDtypeStruct((SSM_HEADS, lp), F32),
                   jax.ShapeDtypeStruct((1, SSM_HEADS), F32), jax.ShapeDtypeStruct((SSM_HEADS, 1), F32),
                   jax.ShapeDtypeStruct((1, SSM_HEADS), F32)],
        grid=(nc,),
        in_specs=[rev(SSM_WIDTH), rev(256, 0), rev(256, 1), rev(SSM_HEADS),
                  pl.BlockSpec((SSM_HEADS, q), lambda i: (0, nc - 1 - i)),
                  _full((1, SSM_HEADS)), _full((SSM_HEADS, 1)), _full((1, SSM_HEADS)),
                  pl.BlockSpec((1, N_PAIRS, 128, 128), lambda i: (nc - 1 - i, 0, 0, 0)), rev(SSM_WIDTH)],
        out_specs=[rev(SSM_WIDTH), rev(512), rev(SSM_HEADS), pl.BlockSpec((SSM_HEADS, q), lambda i: (0, nc - 1 - i)),
                   _full((1, SSM_HEADS)), _full((SSM_HEADS, 1)), _full((1, SSM_HEADS))],
        scratch_shapes=[pltpu.VMEM((N_PAIRS, 128, 128), F32)],
        name="ssd_bwd", compiler_params=_params(("arbitrary",)))(xs, bc, bc, dt, dt_t, alr, alc, dsk, sprev, dy)


def _q_to_slab_order(w):
    hd = QK_NOPE + QK_ROPE
    nope = [w[:, h * hd:h * hd + QK_NOPE] for h in range(ATT_HEADS)]
    rope = [w[:, h * hd + QK_NOPE:(h + 1) * hd] for h in range(ATT_HEADS)]
    return jnp.concatenate(nope + rope, axis=1)


def _q_from_slab_order(wp):
    base = ATT_HEADS * QK_NOPE
    parts = []
    for h in range(ATT_HEADS):
        parts += [wp[:, QK_NOPE * h:QK_NOPE * (h + 1)], wp[:, base + QK_ROPE * h:base + QK_ROPE * (h + 1)]]
    return jnp.concatenate(parts, axis=1)


_IN_CQ, _IN_CKV, _IN_KR, _IN_Z, _IN_XS, _IN_BC, _IN_DT = (0, 384), (384, 640), (640, 704), (704, 1728), (1728, 2752), \
    (2752, 3264), (3264, 3280)


def _pack_w_in(w):
    z = lambda n: jnp.zeros((w.shape[0], n), w.dtype)
    s = lambda r: w[:, r[0]:r[1]]
    return jnp.concatenate([s(_IN_CKV), s(_IN_KR), z(64), s(_IN_CQ), s(_IN_DT), z(112), z(128), s(_IN_Z), s(_IN_XS),
                            s(_IN_BC)], axis=1)


def _unpack_w_in(wp):
    s = lambda off, n: wp[:, off:off + n]
    return jnp.concatenate([s(SEG_CQ, 384), s(SEG_KV, 256), s(SEG_KR, 64), s(SEG_Z, 1024), s(SEG_XS, 1024),
                            s(SEG_BC, 512), s(SEG_DT, 16)], axis=1)


def _rope_tables(lp):
    inv_freq = ROPE_THETA ** (-jnp.arange(0, QK_ROPE, 2, dtype=F32) / QK_ROPE)
    ang = jnp.arange(lp, dtype=F32)[:, None] * inv_freq[None, :]
    cos, sin = jnp.cos(ang), jnp.sin(ang)
    return jnp.tile(cos, (1, 4)), jnp.concatenate([-sin, sin, -sin, sin], axis=1)


def _local_step(x, tgt, w):
    n_real = x.shape[0]
    l = N_META + n_real
    lp = -(-l // MM_BLOCK) * MM_BLOCK
    h0 = lax.optimization_barrier(jnp.concatenate([w["meta_tokens"], x, jnp.zeros((lp - l, D_MODEL), F32)], axis=0))
    cos_t, sin_t = _rope_tables(lp)

    w_in_p = _pack_w_in(w["w_in"])
    w_q_p = _q_to_slab_order(w["w_q_up"])
    w_kv, w_out, w_up, w_down = w["w_kv_up"], w["w_out"], w["w_mlp_up"], w["w_mlp_down"]
    conv_w, conv_b = w["conv_w"], w["conv_b"]
    dt_bias_pad = jnp.concatenate([w["dt_bias"], jnp.zeros((1, 128 - SSM_HEADS), F32)], axis=1)
    alr, dsk = w["a_log"], w["d_skip"]
    alc = alr.reshape(SSM_HEADS, 1)

    n1 = _norm_in(h0, w["norm_mix_pre"])
    proj = _mm("proj", n1, w_in_p)
    cqn, ckvn, krr = _attn_prep(proj, w["q_a_norm"], w["kv_a_norm"], cos_t, sin_t)
    q = _mm("q_up", cqn, w_q_p)
    kv = _mm("kv_up", ckvn, w_kv, outs=((BF16, None),))
    qs, ks, vs, vts = _qk_pack(q, kv, krr, cos_t, sin_t)
    att, att16, lse2 = _attn_fwd(qs, ks, vts)
    xs_pre, xs_act = _conv_fwd("conv_xs_fwd", proj, SEG_XS, SSM_WIDTH, conv_w[:, :SSM_WIDTH], conv_b[:, :SSM_WIDTH])
    bc_pre, bc_act = _conv_fwd("conv_bc_fwd", proj, SEG_BC, 512, conv_w[:, SSM_WIDTH:], conv_b[:, SSM_WIDTH:])
    dt = _dt_fwd(proj, dt_bias_pad)[:, :SSM_HEADS]
    dt_t = dt.T
    y, sprev = _ssd_fwd(xs_act, bc_act, dt, dt_t, alr, alc, dsk)
    ssm = _gated_norm_fwd(y, proj, w["ssm_norm"])
    mix = _mm("out_proj", (att16, ssm), w_out)
    h1, n2 = _mix_residual(h0, mix, w["norm_mix_post"], w["norm_mlp_pre"])
    relu2 = lambda r: jnp.square(jnp.maximum(r, 0.0))
    u, act = _mm("mlp_up", n2, w_up, outs=((F32, None), (BF16, relu2)))
    f = _mm("mlp_down", act, w_down)
    loss, dh2, df, dg_mlp_post = _loss_and_grad(h1, f, w["norm_mlp_post"], tgt, n_real)

    g = {"norm_mlp_post": dg_mlp_post}
    g["w_mlp_down"] = _mm_tn("d_w_mlp_down", act, df)
    du = _mm("d_mlp_act", df, w_down, outs=((BF16, lambda r, ub: r * (2.0 * jnp.maximum(ub, 0.0))),), epi_ins=(u,),
             trans_b=True)
    g["w_mlp_up"] = _mm_tn("d_w_mlp_up", n2, du)
    dn2 = _mm("d_n2", du, w_up, trans_b=True)
    dh1, dmix, g["norm_mlp_pre"], g["norm_mix_post"] = _mlp_residual_bwd(dh2, dn2, h1, w["norm_mlp_pre"], mix,
                                                                         w["norm_mix_post"])
    g["w_out"] = jnp.concatenate([_mm_tn("d_w_out_att", att16, dmix), _mm_tn("d_w_out_ssm", ssm, dmix)], axis=0)
    dcat, dcat16 = _mm("d_cat", dmix, w_out, outs=((F32, None), (BF16, None)), trans_b=True)
    dy, dz, g["ssm_norm"] = _gated_norm_bwd(y, proj, w["ssm_norm"], dcat)
    dxs_act, dbc_act, ddt, ddt_t, dalr, dalc, g["d_skip"] = _ssd_bwd(xs_act, bc_act, dt, dt_t, alr, alc, dsk, sprev, dy)
    g["a_log"] = dalr + dalc.reshape(1, SSM_HEADS)
    dxs, dcw_xs, dcb_xs = _conv_bwd("conv_xs_bwd", dxs_act, xs_pre, proj, SEG_XS, SSM_WIDTH, conv_w[:, :SSM_WIDTH])
    dbc, dcw_bc, dcb_bc = _conv_bwd("conv_bc_bwd", dbc_act, bc_pre, proj, SEG_BC, 512, conv_w[:, SSM_WIDTH:])
    g["conv_w"] = jnp.concatenate([dcw_xs[:CONV_K], dcw_bc[:CONV_K]], axis=1)
    g["conv_b"] = jnp.concatenate([dcb_xs, dcb_bc], axis=1)
    ddt_pad = jnp.concatenate([ddt + ddt_t.T, jnp.zeros((lp, 128 - SSM_HEADS), F32)], axis=1)

    dqs, dks, dvs = _attn_bwd(qs, ks, vs, dcat16, lse2, _attn_delta(dcat, att))
    dq, dkv, dkr = _qk_unpack_bwd(dqs, dks, dvs, cos_t, sin_t)
    g["w_q_up"] = _q_from_slab_order(_mm_tn("d_w_q_up", cqn, dq))
    g["w_kv_up"] = _mm_tn("d_w_kv_up", ckvn, dkv)
    dcqn = _mm("d_cqn", dq, w_q_p, trans_b=True)
    dckvn = _mm("d_ckvn", dkv, w_kv, trans_b=True)
    dproj, g["q_a_norm"], g["kv_a_norm"], ddtb = _proj_grad(proj, dcqn, dckvn, w["q_a_norm"], w["kv_a_norm"], dkr,
                                                          ddt_pad, dt_bias_pad, dz, dxs, dbc)
    g["dt_bias"] = ddtb[:, :SSM_HEADS]
    g["w_in"] = _unpack_w_in(_mm_tn("d_w_in", n1, dproj))
    dn1 = _mm("d_n1", dproj, w_in_p, trans_b=True)
    dh0, g["norm_mix_pre"] = _input_norm_bwd(dh1, dn1, h0, w["norm_mix_pre"])
    g["meta_tokens"] = dh0[:N_META]
    return loss, dh0, g


WEIGHTS = ["meta_tokens", "norm_mix_pre", "w_in", "q_a_norm", "w_q_up", "kv_a_norm", "w_kv_up", "conv_w", "conv_b",
           "dt_bias", "a_log", "d_skip", "ssm_norm", "w_out", "norm_mix_post", "norm_mlp_pre", "w_mlp_up",
           "w_mlp_down", "norm_mlp_post"]
SHARD_AXIS = {"meta_tokens": 1, "w_in": 1, "w_q_up": 1, "w_kv_up": 1, "conv_w": 1, "w_out": 0, "w_mlp_up": 1,
              "w_mlp_down": 0}
FULL_SHAPE = {"meta_tokens": (16, 1024), "norm_mix_pre": (1, 1024), "w_in": (1024, 3280), "q_a_norm": (1, 384),
              "w_q_up": (384, 1536), "kv_a_norm": (1, 256), "w_kv_up": (256, 2048), "conv_w": (4, 1536),
              "conv_b": (1, 1536), "dt_bias": (1, 16), "a_log": (1, 16), "d_skip": (1, 16), "ssm_norm": (1, 1024),
              "w_out": (2048, 1024), "norm_mix_post": (1, 1024), "norm_mlp_pre": (1, 1024), "w_mlp_up": (1024, 4096),
              "w_mlp_down": (4096, 1024), "norm_mlp_post": (1, 1024)}
GATHER_BF16 = ["w_in", "w_q_up", "w_kv_up", "w_out", "w_mlp_up", "w_mlp_down"]
GATHER_F32 = ["meta_tokens", "conv_w"]
ADAM_NATURAL = GATHER_BF16


def _shard_shape(name):
    shp = list(FULL_SHAPE[name])
    if name in SHARD_AXIS:
        shp[SHARD_AXIS[name]] //= N_CHIPS
    return tuple(shp)


PACK_ORDER = sorted(WEIGHTS, key=lambda n: -_shard_shape(n)[0])


def _packed_rows(shape):
    r, c = shape
    return r if c <= PACK_W else -(-c // PACK_W)


def _pack_rows(arrays, row_multiple):
    parts = []
    for a in arrays:
        r, c = a.shape
        if c > PACK_W:
            assert r == 1, a.shape
            folded = _packed_rows(a.shape)
            a = jnp.pad(a, ((0, 0), (0, folded * PACK_W - c))).reshape(folded, PACK_W)
        elif c < PACK_W:
            a = jnp.pad(a, ((0, 0), (0, PACK_W - c)))
        parts.append(a)
    rows = sum(p.shape[0] for p in parts)
    if rows % row_multiple:
        parts.append(jnp.zeros((row_multiple - rows % row_multiple, PACK_W), parts[0].dtype))
    return jnp.concatenate(parts, axis=0)


def _unpack_rows(packed, shapes):
    out, off = [], 0
    for r, c in shapes:
        nr = _packed_rows((r, c))
        blk = packed[off:off + nr]
        out.append(blk[:, :c] if c <= PACK_W else blk.reshape(1, nr * PACK_W)[:, :c])
        off += nr
    return out


def _chip_slice(full, name, t):
    if name not in SHARD_AXIS:
        return full
    ax = SHARD_AXIS[name]
    n = FULL_SHAPE[name][ax] // N_CHIPS
    return lax.slice_in_dim(full, t * n, (t + 1) * n, axis=ax)


HBM_SPEC = pl.BlockSpec(memory_space=pl.ANY)
CHIP_FLIPS = ((1, 0), (0, 1), (1, 1))


def _gather_chips(bufs):
    nb = len(bufs)

    def body(*refs):
        ins, outs = refs[:nb], refs[nb:2 * nb]
        send, recv, loc = refs[2 * nb:]
        x, y, c = lax.axis_index("x"), lax.axis_index("y"), lax.axis_index("c")
        me = 2 * x + y
        sibling = (x, y, 1 - c)
        sends, forwards = [], []
        for b in range(nb):
            half = bufs[b].shape[0] // 2
            mine = pl.ds(c * half, half)
            own = pltpu.make_async_copy(ins[b], outs[b].at[me], loc.at[b])
            own.start()
            sends.append(own)
            for k, (fx, fy) in enumerate(CHIP_FLIPS):
                cp = pltpu.make_async_remote_copy(
                    src_ref=ins[b].at[mine], dst_ref=outs[b].at[me, mine], send_sem=send.at[b, k],
                    recv_sem=recv.at[b, k], device_id=(x ^ fx, y ^ fy, c), device_id_type=MESH_ID)
                cp.start()
                sends.append(cp)
        for b in range(nb):
            half = bufs[b].shape[0] // 2
            mine, theirs = pl.ds(c * half, half), pl.ds((1 - c) * half, half)
            for k, (fx, fy) in enumerate(CHIP_FLIPS):
                chip = 2 * (x ^ fx) + (y ^ fy)
                landed = outs[b].at[chip, mine]
                pltpu.make_async_remote_copy(src_ref=landed, dst_ref=landed, send_sem=send.at[b, k],
                                             recv_sem=recv.at[b, k], device_id=sibling,
                                             device_id_type=MESH_ID).wait_recv()
                fw = pltpu.make_async_remote_copy(src_ref=landed, dst_ref=landed, send_sem=send.at[b, 3 + k],
                                                  recv_sem=recv.at[b, 3 + k], device_id=sibling,
                                                  device_id_type=MESH_ID)
                fw.start()
                forwards.append((fw, outs[b].at[chip, theirs], b, k))
        for fw, arriving, b, k in forwards:
            pltpu.make_async_remote_copy(src_ref=arriving, dst_ref=arriving, send_sem=send.at[b, 3 + k],
                                         recv_sem=recv.at[b, 3 + k], device_id=sibling,
                                         device_id_type=MESH_ID).wait_recv()
            fw.wait_send()
        for cp in sends[1::4] + sends[2::4] + sends[3::4]:
            cp.wait_send()
        for own in sends[0::4]:
            own.wait()

    return pl.pallas_call(
        body,
        out_shape=[jax.ShapeDtypeStruct((N_CHIPS,) + b.shape, b.dtype) for b in bufs],
        in_specs=[HBM_SPEC] * nb, out_specs=[HBM_SPEC] * nb,
        scratch_shapes=[pltpu.SemaphoreType.DMA((nb, 6)), pltpu.SemaphoreType.DMA((nb, 6)),
                        pltpu.SemaphoreType.DMA((nb,))],
        name="gather_chips")(*bufs)


def _sibling_swap(name, buf):
    def body(src, dst, send, recv):
        x, y, c = lax.axis_index("x"), lax.axis_index("y"), lax.axis_index("c")
        cp = pltpu.make_async_remote_copy(src_ref=src, dst_ref=dst, send_sem=send, recv_sem=recv,
                                          device_id=(x, y, 1 - c), device_id_type=MESH_ID)
        cp.start()
        cp.wait()

    return pl.pallas_call(
        body, out_shape=jax.ShapeDtypeStruct(buf.shape, buf.dtype), in_specs=[HBM_SPEC], out_specs=HBM_SPEC,
        scratch_shapes=[pltpu.SemaphoreType.DMA, pltpu.SemaphoreType.DMA], name=name)(buf)


def _scatter_chips(parts):
    nb = len(parts)

    def body(*refs):
        srcs, dsts = refs[:nb], refs[nb:2 * nb]
        send, recv = refs[2 * nb:]
        x, y, c = lax.axis_index("x"), lax.axis_index("y"), lax.axis_index("c")
        copies = []
        for b in range(nb):
            for k, (fx, fy) in enumerate(CHIP_FLIPS):
                tx, ty = x ^ fx, y ^ fy
                cp = pltpu.make_async_remote_copy(
                    src_ref=srcs[b].at[2 * tx + ty], dst_ref=dsts[b].at[k], send_sem=send.at[b, k],
                    recv_sem=recv.at[b, k], device_id=(tx, ty, c), device_id_type=MESH_ID)
                cp.start()
                copies.append(cp)
        for cp in copies:
            cp.wait()

    return pl.pallas_call(
        body, out_shape=[jax.ShapeDtypeStruct((3,) + p.shape[1:], p.dtype) for p in parts], in_specs=[HBM_SPEC] * nb,
        out_specs=[HBM_SPEC] * nb,
        scratch_shapes=[pltpu.SemaphoreType.DMA((nb, 3)), pltpu.SemaphoreType.DMA((nb, 3))],
        name="scatter_chips")(*parts)


def _add_rows(name, terms, also_bf16=False):
    rows = terms[0].shape[0]
    t = _row_tile(rows)
    n_out = 2 if also_bf16 else 1

    def body(*refs):
        acc = refs[0][...].astype(F32)
        for r in refs[1:-n_out]:
            acc = acc + r[...].astype(F32)
        refs[-n_out][...] = acc
        if also_bf16:
            refs[-1][...] = acc.astype(BF16)

    res = pl.pallas_call(
        body, out_shape=[jax.ShapeDtypeStruct(terms[0].shape, dt) for dt in (F32, BF16)[:n_out]], grid=(rows // t,),
        in_specs=[_rb(t, PACK_W)] * len(terms), out_specs=[_rb(t, PACK_W)] * n_out,
        name=name, compiler_params=_params(("parallel",)))(*terms)
    return res if also_bf16 else res[0]


def _sum_chip_order(name, parts, received, me):
    stack = jnp.concatenate([lax.dynamic_index_in_dim(parts, me, axis=0, keepdims=True), received], axis=0)
    terms = []
    for chip in range(N_CHIPS):
        xr = me ^ chip
        where = jnp.where(xr == 0, 0, jnp.where(xr == 2, 1, jnp.where(xr == 1, 2, 3)))
        terms.append(lax.dynamic_index_in_dim(stack, where, axis=0, keepdims=False))
    return _add_rows(name, terms)


def _row_tile(rows):
    assert rows % 16 == 0, rows
    return max(t for t in range(16, 513, 16) if rows % t == 0)


def _adamw(name, g, w, m, v):
    rows, cols = g.shape
    t = max(tt for tt in range(8, 257, 8) if rows % tt == 0)
    c1 = 1.0 - ADAM_B1 ** ADAM_STEP
    c2 = 1.0 - ADAM_B2 ** ADAM_STEP

    def body(g_ref, w_ref, m_ref, v_ref, d_ref, mo_ref, vo_ref):
        gg = g_ref[...]
        mn = ADAM_B1 * m_ref[...] + (1.0 - ADAM_B1) * gg
        vn = ADAM_B2 * v_ref[...] + (1.0 - ADAM_B2) * (gg * gg)
        d_ref[...] = -ADAM_LR * ((mn / c1) / (jnp.sqrt(vn / c2) + ADAM_EPS) + ADAM_WD * w_ref[...])
        mo_ref[...] = mn
        vo_ref[...] = vn

    return pl.pallas_call(
        body, out_shape=[jax.ShapeDtypeStruct(g.shape, F32)] * 3, grid=(rows // t,),
        in_specs=[_rb(t, cols)] * 4, out_specs=[_rb(t, cols)] * 3,
        name=name, compiler_params=_params(("parallel",)))(g, w, m, v)


def kernel(x, meta_tokens, norm_mix_pre, w_in, q_a_norm, w_q_up, kv_a_norm, w_kv_up, conv_w, conv_b, dt_bias, a_log, d_skip, ssm_norm, w_out, norm_mix_post, norm_mlp_pre, w_mlp_up, w_mlp_down, norm_mlp_post, loss_target, m_meta_tokens, m_norm_mix_pre, m_w_in, m_q_a_norm, m_w_q_up, m_kv_a_norm, m_w_kv_up, m_conv_w, m_conv_b, m_dt_bias, m_a_log, m_d_skip, m_ssm_norm, m_w_out, m_norm_mix_post, m_norm_mlp_pre, m_w_mlp_up, m_w_mlp_down, m_norm_mlp_post, v_meta_tokens, v_norm_mix_pre, v_w_in, v_q_a_norm, v_w_q_up, v_kv_a_norm, v_w_kv_up, v_conv_w, v_conv_b, v_dt_bias, v_a_log, v_d_skip, v_ssm_norm, v_w_out, v_norm_mix_post, v_norm_mlp_pre, v_w_mlp_up, v_w_mlp_down, v_norm_mlp_post):
    given = dict(locals())
    drop = lambda name, a: a[0] if a.ndim == 3 else a
    w_loc = {n: drop(n, given[n]) for n in WEIGHTS}
    m_loc = {n: drop(n, given["m_" + n]) for n in WEIGHTS}
    v_loc = {n: drop(n, given["v_" + n]) for n in WEIGHTS}
    ix, iy, ic = lax.axis_index("x"), lax.axis_index("y"), lax.axis_index("c")
    me = 2 * ix + iy

    sent16 = _pack_rows([w_loc[n].astype(BF16) for n in GATHER_BF16], 32)
    sent32 = _pack_rows([w_loc[n] for n in GATHER_F32], 16)
    got16, got32 = _gather_chips([sent16, sent32])
    w_full = {n: w_loc[n] for n in WEIGHTS if n not in SHARD_AXIS}
    for names, got in ((GATHER_BF16, got16), (GATHER_F32, got32)):
        per_chip = [_unpack_rows(got[t], [_shard_shape(n) for n in names]) for t in range(N_CHIPS)]
        for k, n in enumerate(names):
            w_full[n] = jnp.concatenate([per_chip[t][k] for t in range(N_CHIPS)], axis=SHARD_AXIS[n])

    loss, dh0, g_full = _local_step(x[0], loss_target[0], w_full)
    n_real = x.shape[1]
    grad_x = dh0[N_META:N_META + n_real][None]

    shapes = [_shard_shape(n) for n in PACK_ORDER]
    slots = [_pack_rows([_chip_slice(g_full[n], n, t) for n in PACK_ORDER], 16) for t in range(N_CHIPS)]
    rows = slots[0].shape[0]
    half = rows // 2
    halves = lambda hh: jnp.concatenate([lax.dynamic_slice_in_dim(s, hh * half, half, axis=0) for s in slots], axis=0)
    keep, give = halves(ic), halves(1 - ic)
    from_sibling = _sibling_swap("sibling_swap", give)
    part32, part16 = _add_rows("chip_partial", [keep, from_sibling], also_bf16=True)
    part32 = part32.reshape(N_CHIPS, half, PACK_W)
    part16 = part16.reshape(N_CHIPS, half, PACK_W)
    tail_start = rows - sum(_packed_rows(_shard_shape(n)) for n in PACK_ORDER if n not in SHARD_AXIS)
    assert tail_start + TAIL_ROWS >= rows and all(n in SHARD_AXIS for n in PACK_ORDER[:len(SHARD_AXIS)])
    tail32 = part32[:, half - TAIL_ROWS:, :]
    from16, from_tail = _scatter_chips([part16, tail32])
    my_half = _sum_chip_order("chip_total", part16, from16, me)
    my_tail = _sum_chip_order("chip_total_tail", tail32, from_tail, me)
    my_half = lax.dynamic_update_slice(my_half, my_tail, (half - TAIL_ROWS, 0))
    other_half = _sibling_swap("sibling_gather", my_half)
    g_red = jnp.where(ic == 0, jnp.concatenate([my_half, other_half], axis=0),
                      jnp.concatenate([other_half, my_half], axis=0))

    g_own = dict(zip(PACK_ORDER, _unpack_rows(g_red, shapes)))
    small = [n for n in PACK_ORDER if n not in ADAM_NATURAL]
    pack_small = lambda d: _pack_rows([d[n] for n in small], 16)
    packed_upd = _adamw("adamw_small", pack_small(g_own), pack_small(w_loc), pack_small(m_loc), pack_small(v_loc))
    upd = [dict(zip(small, _unpack_rows(p, [_shard_shape(n) for n in small]))) for p in packed_upd]
    for n in ADAM_NATURAL:
        for k, res in enumerate(_adamw("adamw_" + n, g_own[n], w_loc[n], m_loc[n], v_loc[n])):
            upd[k][n] = res

    def outputs(parts):
        return [parts[n][None] if given[n].ndim == 3 else parts[n] for n in WEIGHTS]

    total = lax.psum(loss[0, 0], ("x", "y", "c"))
    return (total, grad_x, *outputs(g_own), *outputs(upd[0]), *outputs(upd[1]), *outputs(upd[2]))
```

```python
import functools

import numpy as np
import jax
import jax.numpy as jnp
from jax import lax
from jax.experimental import pallas as pl
from jax.experimental.pallas import tpu as pltpu

F32 = jnp.float32
BF16 = jnp.bfloat16

D_MODEL = 1024
N_META = 16
EPS = 1e-6
ATT_HEADS = 8
Q_LORA = 384
KV_LORA = 256
QK_NOPE = 128
QK_ROPE = 64
V_HEAD = 128
ROPE_THETA = 10000.0
SSM_HEADS = 16
SSM_HEAD_DIM = 64
SSM_WIDTH = 1024
SSM_STATE = 128
CONV_K = 4
D_FF = 4096
ATT_SCALE = float((QK_NOPE + QK_ROPE) ** -0.5)
ATT_SCALE_LOG2 = float(ATT_SCALE * np.log2(np.e))

ADAM_LR = 0.001
ADAM_B1 = 0.9
ADAM_B2 = 0.999
ADAM_EPS = 1e-08
ADAM_WD = 0.01
ADAM_STEP = 10

SEG_KV, SEG_KR, SEG_CQ, SEG_DT, SEG_Z, SEG_XS, SEG_BC = 0, 256, 384, 768, 1024, 2048, 3072
PROJ_W = 3584
QP_W = 256

ROW_BLOCK = 512
MM_BLOCK = 512
SSD_CHUNK = 256
ATT_SPLIT = 1
ATT_BLOCK = 768
ATT_BLOCK_FWD = 1536
ATT_BLOCK_Q_BWD = 1536
VMEM_LIMIT = 56 * 1024 * 1024
NEG_BIG = -1e30

PACK_W = 1024
TAIL_ROWS = 16
N_CHIPS = 4
MESH_ID = pl.DeviceIdType.MESH


def _params(sem):
    return pltpu.CompilerParams(dimension_semantics=sem, vmem_limit_bytes=VMEM_LIMIT)


def _rb(rows, width, cb=0):
    return pl.BlockSpec((rows, width), lambda i: (i, cb))


def _full(shape):
    zeros = (0,) * len(shape)
    return pl.BlockSpec(shape, lambda i: zeros)


def _acc_add(ref, val):
    first = pl.program_id(0) == 0

    @pl.when(first)
    def _():
        ref[...] = val

    @pl.when(jnp.logical_not(first))
    def _():
        ref[...] += val


def _rms(x, g):
    r = lax.rsqrt(jnp.mean(x * x, axis=-1, keepdims=True) + EPS)
    return x * r * g


def _rms_bwd(x, g, dy):
    r = lax.rsqrt(jnp.mean(x * x, axis=-1, keepdims=True) + EPS)
    dyg = dy * g
    dx = r * dyg - x * (r * r * r) * jnp.mean(x * dyg, axis=-1, keepdims=True)
    dg = jnp.sum(dy * x * r, axis=0, keepdims=True)
    return dx, dg


def _sigmoid(x):
    return 1.0 / (1.0 + jnp.exp(-x))


def _swap32(x):
    lane = lax.broadcasted_iota(jnp.int32, x.shape, 1)
    return jnp.where((lane % 64) < 32, pltpu.roll(x, 96, 1), pltpu.roll(x, 32, 1))


def _rope(x, cos_t, sin_t):
    return x * cos_t + _swap32(x) * sin_t


def _rope_bwd(dr, cos_t, sin_t):
    return dr * cos_t + _swap32(dr * sin_t)


def _tile(n, cap):
    if n <= cap:
        return n
    best = 128
    for t in range(128, cap + 1, 128):
        if n % t == 0:
            best = t
    assert n % best == 0, (n, cap)
    return best


MM_VMEM_BUDGET = 40 * 1024 * 1024
TN_ROWS_CAP = 1536


def _mm(name, a, b, outs=((F32, None),), epi_ins=(), trans_b=False):
    a_parts = tuple(a) if isinstance(a, (tuple, list)) else (a,)
    assert len(a_parts) == 1 or not trans_b
    m = a_parts[0].shape[0]
    widths = [p.shape[1] for p in a_parts]
    k = sum(widths)
    n = b.shape[0] if trans_b else b.shape[1]
    tm = MM_BLOCK
    n_a, n_epi = len(a_parts), len(epi_ins)
    out_bytes = sum(jnp.dtype(dt).itemsize for dt, _ in outs) + sum(e.dtype.itemsize for e in epi_ins)
    a_bytes = sum(p.shape[1] * p.dtype.itemsize for p in a_parts)
    step_bytes = lambda tn: 2 * (tm * a_bytes + k * tn * b.dtype.itemsize + tm * tn * out_bytes)
    tn = n
    while step_bytes(tn) > MM_VMEM_BUDGET and tn % 256 == 0:
        tn //= 2
    assert n % tn == 0 and step_bytes(tn) <= MM_VMEM_BUDGET, (name, n, tn)

    def body(*refs):
        a_refs, b_ref = refs[:n_a], refs[n_a]
        epi_refs = refs[n_a + 1:n_a + 1 + n_epi]
        out_refs = refs[n_a + 1 + n_epi:]
        if trans_b:
            r = _nt(a_refs[0][...].astype(BF16), b_ref[...].astype(BF16))
        else:
            r, off = None, 0
            for a_ref, wd in zip(a_refs, widths):
                part = jnp.dot(a_ref[...].astype(BF16), b_ref[off:off + wd, :].astype(BF16),
                               preferred_element_type=F32)
                r = part if r is None else r + part
                off += wd
        blocks = [e[...] for e in epi_refs]
        for o_ref, (dt, fn) in zip(out_refs, outs):
            o_ref[...] = (r if fn is None else fn(r, *blocks)).astype(dt)

    out_spec = pl.BlockSpec((tm, tn), lambda j, i: (i, j))
    b_spec = pl.BlockSpec((tn, k), lambda j, i: (j, 0)) if trans_b else pl.BlockSpec((k, tn), lambda j, i: (0, j))
    res = pl.pallas_call(
        body,
        out_shape=[jax.ShapeDtypeStruct((m, n), dt) for dt, _ in outs],
        grid=(n // tn, m // tm),
        in_specs=[pl.BlockSpec((tm, wd), lambda j, i: (i, 0)) for wd in widths] + [b_spec] + [out_spec] * n_epi,
        out_specs=[out_spec] * len(outs),
        name=name,
        compiler_params=_params(("parallel", "parallel")),
    )(*a_parts, b, *epi_ins)
    return res[0] if len(outs) == 1 else res


def _mm_tn(name, x, dy, ta_cap=1024, tn_cap=1024):
    l, a = x.shape
    n = dy.shape[1]
    ta, tn = _tile(a, ta_cap), _tile(n, tn_cap)
    tl = max(t for t in range(MM_BLOCK, TN_ROWS_CAP + 1, MM_BLOCK) if l % t == 0)
    nl = l // tl

    def body(x_ref, dy_ref, o_ref):
        ll = pl.program_id(2)

        @pl.when(ll == 0)
        def _():
            o_ref[...] = jnp.zeros_like(o_ref)

        o_ref[...] += lax.dot_general(x_ref[...].astype(BF16), dy_ref[...].astype(BF16), (((0,), (0,)), ((), ())),
                                      preferred_element_type=F32)

    return pl.pallas_call(
        body,
        out_shape=jax.ShapeDtypeStruct((a, n), F32),
        grid=(a // ta, n // tn, nl),
        in_specs=[pl.BlockSpec((tl, ta), lambda i, j, ll: (ll, i)), pl.BlockSpec((tl, tn), lambda i, j, ll: (ll, j))],
        out_specs=pl.BlockSpec((ta, tn), lambda i, j, ll: (i, j)),
        name=name,
        compiler_params=_params(("parallel", "parallel", "arbitrary")),
    )(x, dy)


def _norm_in(h0, g_pre):
    lp = h0.shape[0]
    t = ROW_BLOCK

    def body(h_ref, g_ref, o_ref):
        o_ref[...] = _rms(h_ref[...], g_ref[...]).astype(BF16)

    return pl.pallas_call(
        body, out_shape=jax.ShapeDtypeStruct((lp, D_MODEL), BF16), grid=(lp // t,),
        in_specs=[_rb(t, D_MODEL), _full((1, D_MODEL))], out_specs=_rb(t, D_MODEL),
        name="norm_in", compiler_params=_params(("parallel",)))(h0, g_pre)


def _attn_prep(proj, g_q, g_kv, cos_t, sin_t):
    lp = proj.shape[0]
    t = ROW_BLOCK

    def body(ckv_ref, kr_ref, cq_ref, gq_ref, gkv_ref, cos_ref, sin_ref, cqn_ref, ckvn_ref, krr_ref):
        cqn_ref[...] = _rms(cq_ref[...], gq_ref[...]).astype(BF16)
        ckvn_ref[...] = _rms(ckv_ref[...], gkv_ref[...]).astype(BF16)
        roped = _rope(kr_ref[...], cos_ref[...], sin_ref[...])
        krr_ref[...] = roped + pltpu.roll(roped, 64, 1)

    return pl.pallas_call(
        body,
        out_shape=[jax.ShapeDtypeStruct((lp, Q_LORA), BF16), jax.ShapeDtypeStruct((lp, KV_LORA), BF16),
                   jax.ShapeDtypeStruct((lp, 128), F32)],
        grid=(lp // t,),
        in_specs=[_rb(t, KV_LORA, SEG_KV // KV_LORA), _rb(t, 128, SEG_KR // 128), _rb(t, Q_LORA, SEG_CQ // Q_LORA),
                  _full((1, Q_LORA)), _full((1, KV_LORA)), _rb(t, 128), _rb(t, 128)],
        out_specs=[_rb(t, Q_LORA), _rb(t, KV_LORA), _rb(t, 128)],
        name="attn_prep", compiler_params=_params(("parallel",)))(proj, proj, proj, g_q, g_kv, cos_t, sin_t)


def _qk_pack(q, kv, krr, cos_t, sin_t):
    lp = q.shape[0]
    t = ROW_BLOCK

    def body(q_ref, kv_ref, krr_ref, cos_ref, sin_ref, qs_ref, ks_ref, vs_ref, vts_ref):
        lane = lax.broadcasted_iota(jnp.int32, (t, 128), 1)
        lo = lane < 64
        krr = krr_ref[...].astype(BF16)
        for j in range(ATT_HEADS // 2):
            pr = _rope(q_ref[:, 1024 + 128 * j:1024 + 128 * (j + 1)], cos_ref[...], sin_ref[...])
            for h, keep in ((2 * j, lo), (2 * j + 1, jnp.logical_not(lo))):
                qs_ref[h, :, 0:128] = q_ref[:, 128 * h:128 * (h + 1)].astype(BF16)
                qs_ref[h, :, 128:256] = jnp.where(keep, pr, 0.0).astype(BF16)
        for h in range(ATT_HEADS):
            ks_ref[h, :, 0:128] = kv_ref[:, 256 * h:256 * h + 128].astype(BF16)
            ks_ref[h, :, 128:256] = krr
            v = kv_ref[:, 256 * h + 128:256 * (h + 1)]
            vs_ref[h] = v.astype(BF16)
            vts_ref[h] = v.astype(F32).T.astype(BF16)

    slab = lambda w: pl.BlockSpec((ATT_HEADS, t, w), lambda i: (0, i, 0))
    return pl.pallas_call(
        body,
        out_shape=[jax.ShapeDtypeStruct((ATT_HEADS, lp, QP_W), BF16), jax.ShapeDtypeStruct((ATT_HEADS, lp, QP_W), BF16),
                   jax.ShapeDtypeStruct((ATT_HEADS, lp, V_HEAD), BF16), jax.ShapeDtypeStruct((ATT_HEADS, V_HEAD, lp), BF16)],
        grid=(lp // t,),
        in_specs=[_rb(t, 1536), _rb(t, 2048), _rb(t, 128), _rb(t, 128), _rb(t, 128)],
        out_specs=[slab(QP_W), slab(QP_W), slab(V_HEAD), pl.BlockSpec((ATT_HEADS, V_HEAD, t), lambda i: (0, 0, i))],
        name="qk_pack", compiler_params=_params(("parallel",)))(q, kv, krr, cos_t, sin_t)


def _shifted(ext, t, shift):
    if shift == 0:
        return ext[8:, :]
    return pltpu.roll(ext, shift, 0)[8:, :]


def _conv_fwd(name, proj, seg, width, conv_w, conv_b):
    lp = proj.shape[0]
    t = ROW_BLOCK
    cb = seg // width

    def body(u_ref, halo_ref, w_ref, b_ref, pre_ref, act_ref):
        i = pl.program_id(0)
        u = u_ref[...]
        halo = jnp.where(i > 0, halo_ref[...], 0.0)
        ext = jnp.concatenate([halo, u], axis=0)
        pre = jnp.broadcast_to(b_ref[...], (t, width))
        for k in range(CONV_K):
            pre = pre + w_ref[k:k + 1, :] * _shifted(ext, t, CONV_K - 1 - k)
        pre_ref[...] = pre
        act_ref[...] = pre * _sigmoid(pre)

    return pl.pallas_call(
        body,
        out_shape=[jax.ShapeDtypeStruct((lp, width), F32)] * 2,
        grid=(lp // t,),
        in_specs=[_rb(t, width, cb),
                  pl.BlockSpec((8, width), lambda i: (jnp.maximum(i * (t // 8) - 1, 0), cb)),
                  _full((CONV_K, width)), _full((1, width))],
        out_specs=[_rb(t, width), _rb(t, width)],
        name=name, compiler_params=_params(("parallel",)))(proj, proj, conv_w, conv_b)


def _softplus(x):
    return jnp.maximum(x, 0.0) + jnp.log1p(jnp.exp(-jnp.abs(x)))


def _dt_fwd(proj, dt_bias_pad):
    lp = proj.shape[0]
    t = ROW_BLOCK

    def body(x_ref, b_ref, o_ref):
        o_ref[...] = _softplus(x_ref[...] + b_ref[...])

    return pl.pallas_call(
        body, out_shape=jax.ShapeDtypeStruct((lp, 128), F32), grid=(lp // t,),
        in_specs=[_rb(t, 128, SEG_DT // 128), _full((1, 128))], out_specs=_rb(t, 128),
        name="dt_fwd", compiler_params=_params(("parallel",)))(proj, dt_bias_pad)


def _gated_norm_group(y, z, w):
    g = y * (z * _sigmoid(z))
    return g * lax.rsqrt(jnp.mean(g * g, axis=-1, keepdims=True) + EPS) * w


def _gated_norm_fwd(y, proj, w):
    lp = y.shape[0]
    t = ROW_BLOCK
    gw = SSM_WIDTH // 2

    def body(y0, y1, z0, z1, w0, w1, o_ref):
        o_ref[:, 0:gw] = _gated_norm_group(y0[...], z0[...], w0[...]).astype(BF16)
        o_ref[:, gw:] = _gated_norm_group(y1[...], z1[...], w1[...]).astype(BF16)

    zb = SEG_Z // gw
    return pl.pallas_call(
        body, out_shape=jax.ShapeDtypeStruct((lp, SSM_WIDTH), BF16), grid=(lp // t,),
        in_specs=[_rb(t, gw, 0), _rb(t, gw, 1), _rb(t, gw, zb), _rb(t, gw, zb + 1),
                  pl.BlockSpec((1, gw), lambda i: (0, 0)), pl.BlockSpec((1, gw), lambda i: (0, 1))],
        out_specs=_rb(t, SSM_WIDTH),
        name="gated_norm_fwd", compiler_params=_params(("parallel",)))(y, y, proj, proj, w, w)


def _gated_norm_bwd(y, proj, w, dssm):
    lp = y.shape[0]
    t = ROW_BLOCK
    gw = SSM_WIDTH // 2

    def body(y0, y1, z0, z1, w0, w1, d0, d1, dy_ref, dz_ref, dw_ref):
        dws = []
        for g, (yr, zr, wr, dr) in enumerate(((y0, z0, w0, d0), (y1, z1, w1, d1))):
            _, vjp = jax.vjp(_gated_norm_group, yr[...], zr[...], wr[...])
            dyg, dzg, dwg = vjp(dr[...])
            dy_ref[:, g * gw:(g + 1) * gw] = dyg
            dz_ref[:, g * gw:(g + 1) * gw] = dzg
            dws.append(dwg)
        _acc_add(dw_ref, jnp.concatenate(dws, axis=1))

    zb = SEG_Z // gw
    return pl.pallas_call(
        body,
        out_shape=[jax.ShapeDtypeStruct((lp, SSM_WIDTH), F32), jax.ShapeDtypeStruct((lp, SSM_WIDTH), F32),
                   jax.ShapeDtypeStruct((1, SSM_WIDTH), F32)],
        grid=(lp // t,),
        in_specs=[_rb(t, gw, 0), _rb(t, gw, 1), _rb(t, gw, zb), _rb(t, gw, zb + 1),
                  pl.BlockSpec((1, gw), lambda i: (0, 0)), pl.BlockSpec((1, gw), lambda i: (0, 1)),
                  _rb(t, gw, 2), _rb(t, gw, 3)],
        out_specs=[_rb(t, SSM_WIDTH), _rb(t, SSM_WIDTH), _full((1, SSM_WIDTH))],
        name="gated_norm_bwd", compiler_params=_params(("arbitrary",)))(y, y, proj, proj, w, w, dssm, dssm)


def _mix_residual(h0, mix, g_post, g_mlp_pre):
    lp = h0.shape[0]
    t = ROW_BLOCK

    def body(h_ref, m_ref, gp_ref, gm_ref, h1_ref, n2_ref):
        h1 = h_ref[...] + _rms(m_ref[...], gp_ref[...])
        h1_ref[...] = h1
        n2_ref[...] = _rms(h1, gm_ref[...]).astype(BF16)

    return pl.pallas_call(
        body, out_shape=[jax.ShapeDtypeStruct((lp, D_MODEL), F32), jax.ShapeDtypeStruct((lp, D_MODEL), BF16)],
        grid=(lp // t,),
        in_specs=[_rb(t, D_MODEL), _rb(t, D_MODEL), _full((1, D_MODEL)), _full((1, D_MODEL))],
        out_specs=[_rb(t, D_MODEL), _rb(t, D_MODEL)],
        name="mix_residual", compiler_params=_params(("parallel",)))(h0, mix, g_post, g_mlp_pre)


def _loss_and_grad(h1, f, g_post, tgt, n_real):
    lp = h1.shape[0]
    t = ROW_BLOCK
    assert n_real % t == 0 and t % N_META == 0
    n_tb = n_real // t

    def body(h1_ref, f_ref, g_ref, halo_ref, t_ref, loss_ref, dh2_ref, df_ref, dg_ref):
        i = pl.program_id(0)
        fx = f_ref[...]
        h2 = h1_ref[...] + _rms(fx, g_ref[...])
        row = i * t + lax.broadcasted_iota(jnp.int32, (t, 1), 0)
        real = jnp.logical_and(row >= N_META, row < N_META + n_real)
        target = jnp.concatenate([halo_ref[...], t_ref[0:t - N_META, :]], axis=0)
        diff = jnp.where(real, h2 - target, 0.0)
        part = 0.5 * jnp.sum(jnp.sum(diff * diff, axis=-1, keepdims=True) / D_MODEL, axis=0, keepdims=True)
        _acc_add(loss_ref, jnp.broadcast_to(part, (1, 128)))
        dh2 = diff / D_MODEL
        dh2_ref[...] = dh2
        dfx, dg = _rms_bwd(fx, g_ref[...], dh2)
        df_ref[...] = dfx.astype(BF16)
        _acc_add(dg_ref, dg)

    return pl.pallas_call(
        body,
        out_shape=[jax.ShapeDtypeStruct((1, 128), F32), jax.ShapeDtypeStruct((lp, D_MODEL), F32),
                   jax.ShapeDtypeStruct((lp, D_MODEL), BF16), jax.ShapeDtypeStruct((1, D_MODEL), F32)],
        grid=(lp // t,),
        in_specs=[_rb(t, D_MODEL), _rb(t, D_MODEL), _full((1, D_MODEL)),
                  pl.BlockSpec((N_META, D_MODEL),
                               lambda i: (jnp.clip(i * (t // N_META) - 1, 0, n_real // N_META - 1), 0)),
                  pl.BlockSpec((t, D_MODEL), lambda i: (jnp.minimum(i, n_tb - 1), 0))],
        out_specs=[_full((1, 128)), _rb(t, D_MODEL), _rb(t, D_MODEL), _full((1, D_MODEL))],
        name="loss_and_grad", compiler_params=_params(("arbitrary",)))(h1, f, g_post, tgt, tgt)


def _mlp_residual_bwd(dh2, dn2, h1, g_mlp_pre, mix, g_post):
    lp = h1.shape[0]
    t = ROW_BLOCK

    def body(dh2_ref, dn2_ref, h1_ref, gm_ref, mix_ref, gp_ref, dh1_ref, dmix_ref, dgm_ref, dgp_ref):
        dx, dgm = _rms_bwd(h1_ref[...], gm_ref[...], dn2_ref[...])
        dh1 = dh2_ref[...] + dx
        dh1_ref[...] = dh1
        dmix, dgp = _rms_bwd(mix_ref[...], gp_ref[...], dh1)
        dmix_ref[...] = dmix.astype(BF16)
        _acc_add(dgm_ref, dgm)
        _acc_add(dgp_ref, dgp)

    return pl.pallas_call(
        body,
        out_shape=[jax.ShapeDtypeStruct((lp, D_MODEL), F32), jax.ShapeDtypeStruct((lp, D_MODEL), BF16),
                   jax.ShapeDtypeStruct((1, D_MODEL), F32), jax.ShapeDtypeStruct((1, D_MODEL), F32)],
        grid=(lp // t,),
        in_specs=[_rb(t, D_MODEL), _rb(t, D_MODEL), _rb(t, D_MODEL), _full((1, D_MODEL)), _rb(t, D_MODEL),
                  _full((1, D_MODEL))],
        out_specs=[_rb(t, D_MODEL), _rb(t, D_MODEL), _full((1, D_MODEL)), _full((1, D_MODEL))],
        name="mlp_residual_bwd", compiler_params=_params(("arbitrary",)))(dh2, dn2, h1, g_mlp_pre, mix, g_post)


def _input_norm_bwd(dh1, dn1, h0, g_pre):
    lp = h0.shape[0]
    t = ROW_BLOCK

    def body(dh1_ref, dn1_ref, h0_ref, g_ref, dh0_ref, dg_ref):
        dx, dg = _rms_bwd(h0_ref[...], g_ref[...], dn1_ref[...])
        dh0_ref[...] = dh1_ref[...] + dx
        _acc_add(dg_ref, dg)

    return pl.pallas_call(
        body, out_shape=[jax.ShapeDtypeStruct((lp, D_MODEL), F32), jax.ShapeDtypeStruct((1, D_MODEL), F32)],
        grid=(lp // t,),
        in_specs=[_rb(t, D_MODEL), _rb(t, D_MODEL), _rb(t, D_MODEL), _full((1, D_MODEL))],
        out_specs=[_rb(t, D_MODEL), _full((1, D_MODEL))],
        name="input_norm_bwd", compiler_params=_params(("arbitrary",)))(dh1, dn1, h0, g_pre)


def _conv_bwd(name, dact, pre, proj, seg, width, conv_w):
    lp = proj.shape[0]
    t = ROW_BLOCK
    cb = seg // width
    nblk = lp // t

    def dsilu(p):
        s = _sigmoid(p)
        return s * (1.0 + p * (1.0 - s))

    def body(da_ref, dan_ref, pre_ref, pren_ref, u_ref, halo_ref, w_ref, du_ref, dw_ref, db_ref):
        i = pl.program_id(0)
        dpre = da_ref[...] * dsilu(pre_ref[...])
        dpre_next = jnp.where(i < nblk - 1, dan_ref[...] * dsilu(pren_ref[...]), 0.0)
        extd = jnp.concatenate([dpre, dpre_next], axis=0)
        halo = jnp.where(i > 0, halo_ref[...], 0.0)
        ext = jnp.concatenate([halo, u_ref[...]], axis=0)
        du = jnp.zeros((t, width), F32)
        dws = []
        for k in range(CONV_K):
            m = CONV_K - 1 - k
            ahead = dpre if m == 0 else pltpu.roll(extd, t + 8 - m, 0)[:t, :]
            du = du + w_ref[k:k + 1, :] * ahead
            dws.append(jnp.sum(dpre * _shifted(ext, t, m), axis=0, keepdims=True))
        du_ref[...] = du
        _acc_add(dw_ref, jnp.concatenate(dws + [jnp.zeros((8 - CONV_K, width), F32)], axis=0))
        _acc_add(db_ref, jnp.sum(dpre, axis=0, keepdims=True))

    nxt = lambda i: (jnp.minimum((i + 1) * (t // 8), lp // 8 - 1), 0)
    return pl.pallas_call(
        body,
        out_shape=[jax.ShapeDtypeStruct((lp, width), F32), jax.ShapeDtypeStruct((8, width), F32),
                   jax.ShapeDtypeStruct((1, width), F32)],
        grid=(nblk,),
        in_specs=[_rb(t, width), pl.BlockSpec((8, width), nxt), _rb(t, width), pl.BlockSpec((8, width), nxt),
                  _rb(t, width, cb),
                  pl.BlockSpec((8, width), lambda i: (jnp.maximum(i * (t // 8) - 1, 0), cb)),
                  _full((CONV_K, width))],
        out_specs=[_rb(t, width), _full((8, width)), _full((1, width))],
        name=name, compiler_params=_params(("arbitrary",)))(dact, dact, pre, pre, proj, proj, conv_w)


def _qk_unpack_bwd(dqs, dks, dvs, cos_t, sin_t):
    lp = dqs.shape[1]
    t = ROW_BLOCK

    def body(dqs_ref, dks_ref, dvs_ref, cos_ref, sin_ref, dq_ref, dkv_ref, dkr_ref):
        lane = lax.broadcasted_iota(jnp.int32, (t, 128), 1)
        lo = lane < 64
        for j in range(ATT_HEADS // 2):
            dpr = jnp.where(lo, dqs_ref[2 * j, :, 128:256], dqs_ref[2 * j + 1, :, 128:256])
            dq_ref[:, 1024 + 128 * j:1024 + 128 * (j + 1)] = _rope_bwd(dpr, cos_ref[...], sin_ref[...]).astype(BF16)
        dkrr = jnp.zeros((t, 128), F32)
        for h in range(ATT_HEADS):
            dq_ref[:, 128 * h:128 * (h + 1)] = dqs_ref[h, :, 0:128].astype(BF16)
            dkv_ref[:, 256 * h:256 * h + 128] = dks_ref[h, :, 0:128].astype(BF16)
            dkv_ref[:, 256 * h + 128:256 * (h + 1)] = dvs_ref[h].astype(BF16)
            dkrr = dkrr + dks_ref[h, :, 128:256]
        droped = jnp.where(lo, dkrr + pltpu.roll(dkrr, 64, 1), 0.0)
        dkr_ref[...] = _rope_bwd(droped, cos_ref[...], sin_ref[...])

    slab = lambda w: pl.BlockSpec((ATT_HEADS, t, w), lambda i: (0, i, 0))
    return pl.pallas_call(
        body,
        out_shape=[jax.ShapeDtypeStruct((lp, 1536), BF16), jax.ShapeDtypeStruct((lp, 2048), BF16),
                   jax.ShapeDtypeStruct((lp, 128), F32)],
        grid=(lp // t,),
        in_specs=[slab(QP_W), slab(QP_W), slab(V_HEAD), _rb(t, 128), _rb(t, 128)],
        out_specs=[_rb(t, 1536), _rb(t, 2048), _rb(t, 128)],
        name="qk_unpack_bwd", compiler_params=_params(("parallel",)))(dqs, dks, dvs, cos_t, sin_t)


def _proj_grad(proj, dcqn, dckvn, g_q, g_kv, dkr, ddt_pad, dt_bias_pad, dz, dxs, dbc):
    lp = proj.shape[0]
    t = ROW_BLOCK

    def body(ckv_ref, cq_ref, pdt_ref, dcq_ref, dckv_ref, gq_ref, gkv_ref, dkr_ref, ddt_ref, b_ref, dz_ref, dxs_ref,
             dbc_ref, dp_ref, dgq_ref, dgkv_ref, db_ref):
        dckv, dgkv = _rms_bwd(ckv_ref[...], gkv_ref[...], dckv_ref[...])
        dcq, dgq = _rms_bwd(cq_ref[...], gq_ref[...], dcq_ref[...])
        ddt_raw = ddt_ref[...] * _sigmoid(pdt_ref[...] + b_ref[...])
        dp_ref[:, SEG_KV:SEG_KV + KV_LORA] = dckv.astype(BF16)
        dp_ref[:, SEG_KR:SEG_KR + 128] = dkr_ref[...].astype(BF16)
        dp_ref[:, SEG_CQ:SEG_CQ + Q_LORA] = dcq.astype(BF16)
        dp_ref[:, SEG_DT:SEG_DT + 128] = ddt_raw.astype(BF16)
        dp_ref[:, SEG_DT + 128:SEG_Z] = jnp.zeros((t, SEG_Z - SEG_DT - 128), BF16)
        dp_ref[:, SEG_Z:SEG_XS] = dz_ref[...].astype(BF16)
        dp_ref[:, SEG_XS:SEG_BC] = dxs_ref[...].astype(BF16)
        dp_ref[:, SEG_BC:PROJ_W] = dbc_ref[...].astype(BF16)
        _acc_add(dgq_ref, dgq)
        _acc_add(dgkv_ref, dgkv)
        _acc_add(db_ref, jnp.sum(ddt_raw, axis=0, keepdims=True))

    return pl.pallas_call(
        body,
        out_shape=[jax.ShapeDtypeStruct((lp, PROJ_W), BF16), jax.ShapeDtypeStruct((1, Q_LORA), F32),
                   jax.ShapeDtypeStruct((1, KV_LORA), F32), jax.ShapeDtypeStruct((1, 128), F32)],
        grid=(lp // t,),
        in_specs=[_rb(t, KV_LORA, SEG_KV // KV_LORA), _rb(t, Q_LORA, SEG_CQ // Q_LORA), _rb(t, 128, SEG_DT // 128),
                  _rb(t, Q_LORA), _rb(t, KV_LORA), _full((1, Q_LORA)), _full((1, KV_LORA)), _rb(t, 128), _rb(t, 128),
                  _full((1, 128)), _rb(t, SSM_WIDTH), _rb(t, SSM_WIDTH), _rb(t, 512)],
        out_specs=[_rb(t, PROJ_W), _full((1, Q_LORA)), _full((1, KV_LORA)), _full((1, 128))],
        name="proj_grad", compiler_params=_params(("arbitrary",)))(
            proj, proj, proj, dcqn, dckvn, g_q, g_kv, dkr, ddt_pad, dt_bias_pad, dz, dxs, dbc)


def _pair_tables(n):
    qmaj = [(i, j) for i in range(n) for j in range(i + 1)]
    kmaj = [(i, j) for j in range(n) for i in range(j, n)]
    to = lambda ps, c: jnp.asarray(np.array([p[c] for p in ps], np.int32))
    return (to(qmaj, 0), to(qmaj, 1)), (to(kmaj, 0), to(kmaj, 1))


def _att_block(lp, edge=ATT_BLOCK):
    return edge if lp % edge == 0 else MM_BLOCK


def _nt(a, b):
    return lax.dot_general(a, b, (((1,), (1,)), ((), ())), preferred_element_type=F32)


def _attn_fwd(qs, ks, vts):
    lp = qs.shape[1]
    t = _att_block(lp, ATT_BLOCK_FWD)
    n = lp // t
    (qi, kj), _ = _pair_tables(n)
    tc = t // ATT_SPLIT

    def body(qi_ref, kj_ref, q_ref, k_ref, vt_ref, o_ref, o16_ref, lse_ref, m_s, l_s, acc_s):
        p = pl.program_id(1)
        i, j = qi_ref[p], kj_ref[p]

        @pl.when(j == 0)
        def _():
            m_s[...] = jnp.full_like(m_s, NEG_BIG)
            l_s[...] = jnp.zeros_like(l_s)
            acc_s[...] = jnp.zeros_like(acc_s)

        def update(masked):
            m_all, l_all, acc_all = m_s[...], l_s[...], acc_s[...]
            m_out, l_out, acc_out = [], [], []
            qk = lambda c: _nt(k_ref[0], q_ref[0, c * tc:(c + 1) * tc, :])
            ahead = qk(0)
            for c in range(ATT_SPLIT):
                cols = slice(c * tc, (c + 1) * tc)
                sc = ahead * ATT_SCALE_LOG2
                if c + 1 < ATT_SPLIT:
                    ahead = qk(c + 1)
                if masked:
                    keep = (lax.broadcasted_iota(jnp.int32, (t, tc), 1) + c * tc
                            >= lax.broadcasted_iota(jnp.int32, (t, tc), 0))
                    sc = jnp.where(keep, sc, NEG_BIG)
                m_prev = m_all[:, cols]
                m_new = jnp.maximum(m_prev, jnp.max(sc, axis=0, keepdims=True))
                alpha = jnp.exp2(m_prev - m_new)
                pexp = jnp.exp2(sc - m_new)
                l_out.append(alpha * l_all[:, cols] + jnp.sum(pexp, axis=0, keepdims=True))
                acc_out.append(alpha * acc_all[:, cols] + jnp.dot(vt_ref[0], pexp.astype(BF16),
                                                                  preferred_element_type=F32))
                m_out.append(m_new)
            cat = lambda parts: parts[0] if len(parts) == 1 else jnp.concatenate(parts, axis=1)
            m_s[...], l_s[...], acc_s[...] = cat(m_out), cat(l_out), cat(acc_out)

        @pl.when(j < i)
        def _():
            update(False)

        @pl.when(j == i)
        def _():
            update(True)
            out = (acc_s[...] / l_s[...]).T
            o_ref[...] = out
            o16_ref[...] = out.astype(BF16)
            lse_ref[0] = m_s[...] + jnp.log2(l_s[...])

    grid_spec = pltpu.PrefetchScalarGridSpec(
        num_scalar_prefetch=2, grid=(ATT_HEADS, int(qi.shape[0])),
        in_specs=[pl.BlockSpec((1, t, QP_W), lambda h, p, qi, kj: (h, qi[p], 0)),
                  pl.BlockSpec((1, t, QP_W), lambda h, p, qi, kj: (h, kj[p], 0)),
                  pl.BlockSpec((1, V_HEAD, t), lambda h, p, qi, kj: (h, 0, kj[p]))],
        out_specs=[pl.BlockSpec((t, V_HEAD), lambda h, p, qi, kj: (qi[p], h)),
                   pl.BlockSpec((t, V_HEAD), lambda h, p, qi, kj: (qi[p], h)),
                   pl.BlockSpec((1, 1, t), lambda h, p, qi, kj: (h, 0, qi[p]))],
        scratch_shapes=[pltpu.VMEM((1, t), F32), pltpu.VMEM((1, t), F32), pltpu.VMEM((V_HEAD, t), F32)])
    return pl.pallas_call(
        body, grid_spec=grid_spec,
        out_shape=[jax.ShapeDtypeStruct((lp, ATT_HEADS * V_HEAD), F32),
                   jax.ShapeDtypeStruct((lp, ATT_HEADS * V_HEAD), BF16), jax.ShapeDtypeStruct((ATT_HEADS, 1, lp), F32)],
        name="attn_fwd", compiler_params=_params(("parallel", "arbitrary")))(qi, kj, qs, ks, vts)


def _attn_delta(datt, att):
    lp = att.shape[0]
    t = MM_BLOCK
    w = ATT_HEADS * V_HEAD

    def body(do_ref, o_ref, d_ref):
        ones = jnp.ones((8, V_HEAD), BF16)
        for h in range(ATT_HEADS):
            cols = slice(h * V_HEAD, (h + 1) * V_HEAD)
            prod = do_ref[:, cols] * o_ref[:, cols]
            hi = prod.astype(BF16)
            lo = (prod - hi.astype(F32)).astype(BF16)
            d_ref[h] = (_nt(ones, hi) + _nt(ones, lo))[0:1, :]

    return pl.pallas_call(
        body, out_shape=jax.ShapeDtypeStruct((ATT_HEADS, 1, lp), F32), grid=(lp // t,),
        in_specs=[_rb(t, w), _rb(t, w)], out_specs=pl.BlockSpec((ATT_HEADS, 1, t), lambda i: (0, 0, i)),
        name="attn_delta", compiler_params=_params(("parallel",)))(datt, att)


def _attn_bwd(qs, ks, vs, datt16, lse2, delta):
    lp = qs.shape[1]
    tk = _att_block(lp)
    tq = ATT_BLOCK_Q_BWD if lp % ATT_BLOCK_Q_BWD == 0 and ATT_BLOCK_Q_BWD % tk == 0 else tk
    r = tq // tk
    nk, nq = lp // tk, lp // tq
    pairs = [(i, j) for j in range(nk) for i in range(j // r, nq)]
    qi = jnp.asarray(np.array([p[0] for p in pairs], np.int32))
    kj = jnp.asarray(np.array([p[1] for p in pairs], np.int32))
    n_pairs = len(pairs)

    def body(qi_ref, kj_ref, k_ref, v_ref, q_ref, do_ref, lse_ref, dl_ref, dq_hbm, dk_ref, dv_ref, dq_s, dk_s, dv_s,
             sem):
        h, p = pl.program_id(0), pl.program_id(1)
        i, j = qi_ref[p], kj_ref[p]
        first = i == j // r

        @pl.when(p == 0)
        def _():
            dq_s[...] = jnp.zeros_like(dq_s)

        @pl.when(first)
        def _():
            dk_s[...] = jnp.zeros_like(dk_s)
            dv_s[...] = jnp.zeros_like(dv_s)

        def step(masked, skip=0):
            q0 = skip * tk
            nrows = tq - q0
            q = q_ref[0, q0:, :]
            do = do_ref[q0:, :]
            pt = jnp.exp2(_nt(k_ref[0], q) * ATT_SCALE_LOG2 - lse_ref[0, :, q0:])
            if masked:
                keep = (lax.broadcasted_iota(jnp.int32, (tk, nrows), 1)
                        >= lax.broadcasted_iota(jnp.int32, (tk, nrows), 0))
                pt = jnp.where(keep, pt, 0.0)
            dst = (pt * (_nt(v_ref[0], do) - dl_ref[0, :, q0:]) * ATT_SCALE).astype(BF16)
            dv_s[...] += jnp.dot(pt.astype(BF16), do, preferred_element_type=F32)
            dk_s[...] += jnp.dot(dst, q, preferred_element_type=F32)
            rows = pl.ds(pl.multiple_of(i * tq + q0, tk), nrows)
            dq_s[rows, :] += lax.dot_general(dst, k_ref[0], (((0,), (0,)), ((), ())), preferred_element_type=F32)

        @pl.when(jnp.logical_not(first))
        def _():
            step(False)

        for sub in range(r):
            @pl.when(jnp.logical_and(first, j % r == sub))
            def _(sub=sub):
                step(True, sub)

        @pl.when(i == nq - 1)
        def _():
            dk_ref[0] = dk_s[...]
            dv_ref[0] = dv_s[...]

        @pl.when(p == n_pairs - 1)
        def _():
            out = pltpu.make_async_copy(dq_s, dq_hbm.at[h], sem)
            out.start()
            out.wait()

    grid_spec = pltpu.PrefetchScalarGridSpec(
        num_scalar_prefetch=2, grid=(ATT_HEADS, n_pairs),
        in_specs=[pl.BlockSpec((1, tk, QP_W), lambda h, p, qi, kj: (h, kj[p], 0)),
                  pl.BlockSpec((1, tk, V_HEAD), lambda h, p, qi, kj: (h, kj[p], 0)),
                  pl.BlockSpec((1, tq, QP_W), lambda h, p, qi, kj: (h, qi[p], 0)),
                  pl.BlockSpec((tq, V_HEAD), lambda h, p, qi, kj: (qi[p], h)),
                  pl.BlockSpec((1, 1, tq), lambda h, p, qi, kj: (h, 0, qi[p])),
                  pl.BlockSpec((1, 1, tq), lambda h, p, qi, kj: (h, 0, qi[p]))],
        out_specs=[pl.BlockSpec(memory_space=pl.ANY),
                   pl.BlockSpec((1, tk, QP_W), lambda h, p, qi, kj: (h, kj[p], 0)),
                   pl.BlockSpec((1, tk, V_HEAD), lambda h, p, qi, kj: (h, kj[p], 0))],
        scratch_shapes=[pltpu.VMEM((lp, QP_W), F32), pltpu.VMEM((tk, QP_W), F32), pltpu.VMEM((tk, V_HEAD), F32),
                        pltpu.SemaphoreType.DMA])
    return pl.pallas_call(
        body, grid_spec=grid_spec,
        out_shape=[jax.ShapeDtypeStruct((ATT_HEADS, lp, QP_W), F32), jax.ShapeDtypeStruct((ATT_HEADS, lp, QP_W), F32),
                   jax.ShapeDtypeStruct((ATT_HEADS, lp, V_HEAD), F32)],
        name="attn_bwd", compiler_params=_params(("arbitrary", "arbitrary")))(
            qi, kj, ks, vs, qs, datt16, lse2, delta)


N_PAIRS = SSM_HEADS // 2
HI = lax.Precision.HIGHEST


def _ssd_chunk(xp, bs, cs, dt, dt_t, alr, alc, dsk, st):
    q = dt.shape[0]
    li = lax.broadcasted_iota(jnp.int32, (q, q), 0)
    si = lax.broadcasted_iota(jnp.int32, (q, q), 1)
    tri = (si <= li).astype(F32)
    tri_t = (li <= si).astype(F32)
    lo = lax.broadcasted_iota(jnp.int32, (1, 128), 1) < 64
    h_r = lax.broadcasted_iota(jnp.int32, (1, SSM_HEADS), 1)
    h_c = lax.broadcasted_iota(jnp.int32, (SSM_HEADS, 1), 0)
    a = dt * (-jnp.exp(alr))
    a_t = dt_t * (-jnp.exp(alc))
    acum = jnp.dot(tri, a, precision=HI, preferred_element_type=F32)
    acum_t = jnp.dot(a_t, tri_t, precision=HI, preferred_element_type=F32)
    last = (lax.broadcasted_iota(jnp.int32, (q, 1), 0) == q - 1).astype(F32)
    alast = jnp.sum(acum * last, axis=0, keepdims=True)
    e = jnp.exp(acum)
    rdt = jnp.exp(alast - acum) * dt
    e_last = jnp.exp(alast)

    def col(m, h):
        return jnp.sum(m * (h_r == h).astype(F32), axis=1, keepdims=True)

    def row(m, h):
        return jnp.sum(m * (h_c == h).astype(F32), axis=0, keepdims=True)

    def pair(m, ha):
        return jnp.where(lo, col(m, ha), col(m, ha + 1))

    ys, st_new = [], []
    for g in range(2):
        c_b = cs[g].astype(BF16)
        b_b = bs[g].astype(BF16)
        cb = _nt(c_b, b_b)
        for j in range(N_PAIRS // 2):
            p = (N_PAIRS // 2) * g + j
            ha = 2 * p
            x = xp[p]
            x_b = x.astype(BF16)

            def w_of(h):
                seg = col(acum, h) - row(acum_t, h)
                return (cb * jnp.exp(jnp.minimum(seg, 0.0)) * tri * row(dt_t, h)).astype(BF16)

            y_diag = jnp.where(lo, jnp.dot(w_of(ha), x_b, preferred_element_type=F32),
                               jnp.dot(w_of(ha + 1), x_b, preferred_element_type=F32))
            y_off = jnp.dot(c_b, st[p].astype(BF16), preferred_element_type=F32) * pair(e, ha)
            ys.append(y_diag + y_off + pair(dsk, ha) * x)
            xw = (x * pair(rdt, ha)).astype(BF16)
            st_new.append(st[p] * pair(e_last, ha)
                          + lax.dot_general(b_b, xw, (((0,), (0,)), ((), ())), preferred_element_type=F32))
    return ys, st_new


def _ssd_fwd(xs, bc, dt, dt_t, alr, alc, dsk):
    lp = xs.shape[0]
    q = SSD_CHUNK
    nc = lp // q

    def body(x_ref, b_ref, c_ref, dt_ref, dtt_ref, alr_ref, alc_ref, dsk_ref, y_ref, sp_ref, st_s):
        @pl.when(pl.program_id(0) == 0)
        def _():
            st_s[...] = jnp.zeros_like(st_s)

        sp_ref[0] = st_s[...]
        xp = [x_ref[:, 128 * p:128 * (p + 1)] for p in range(N_PAIRS)]
        bs = [b_ref[:, 0:128], b_ref[:, 128:256]]
        cs = [c_ref[:, 0:128], c_ref[:, 128:256]]
        ys, st_new = _ssd_chunk(xp, bs, cs, dt_ref[...], dtt_ref[...], alr_ref[...], alc_ref[...], dsk_ref[...],
                                [st_s[p] for p in range(N_PAIRS)])
        for p in range(N_PAIRS):
            y_ref[:, 128 * p:128 * (p + 1)] = ys[p]
            st_s[p] = st_new[p]

    return pl.pallas_call(
        body,
        out_shape=[jax.ShapeDtypeStruct((lp, SSM_WIDTH), F32), jax.ShapeDtypeStruct((nc, N_PAIRS, 128, 128), F32)],
        grid=(nc,),
        in_specs=[_rb(q, SSM_WIDTH), _rb(q, 256, 0), _rb(q, 256, 1), _rb(q, SSM_HEADS),
                  pl.BlockSpec((SSM_HEADS, q), lambda i: (0, i)),
                  _full((1, SSM_HEADS)), _full((SSM_HEADS, 1)), _full((1, SSM_HEADS))],
        out_specs=[_rb(q, SSM_WIDTH), pl.BlockSpec((1, N_PAIRS, 128, 128), lambda i: (i, 0, 0, 0))],
        scratch_shapes=[pltpu.VMEM((N_PAIRS, 128, 128), F32)],
        name="ssd_fwd", compiler_params=_params(("arbitrary",)))(xs, bc, bc, dt, dt_t, alr, alc, dsk)


def _ssd_bwd(xs, bc, dt, dt_t, alr, alc, dsk, sprev, dy):
    lp = xs.shape[0]
    q = SSD_CHUNK
    nc = lp // q

    def body(x_ref, b_ref, c_ref, dt_ref, dtt_ref, alr_ref, alc_ref, dsk_ref, sp_ref, dy_ref,
             dx_ref, dbc_ref, ddt_ref, ddtt_ref, dalr_ref, dalc_ref, ddsk_ref, ds_s):
        @pl.when(pl.program_id(0) == 0)
        def _():
            ds_s[...] = jnp.zeros_like(ds_s)

        xp = [x_ref[:, 128 * p:128 * (p + 1)] for p in range(N_PAIRS)]
        bs = [b_ref[:, 0:128], b_ref[:, 128:256]]
        cs = [c_ref[:, 0:128], c_ref[:, 128:256]]
        st = [sp_ref[0, p] for p in range(N_PAIRS)]
        _, vjp = jax.vjp(_ssd_chunk, xp, bs, cs, dt_ref[...], dtt_ref[...], alr_ref[...], alc_ref[...], dsk_ref[...],
                         st)
        dys = [dy_ref[:, 128 * p:128 * (p + 1)] for p in range(N_PAIRS)]
        dxp, dbs, dcs, ddt, ddtt, dalr, dalc, ddsk, dst = vjp((dys, [ds_s[p] for p in range(N_PAIRS)]))
        for p in range(N_PAIRS):
            dx_ref[:, 128 * p:128 * (p + 1)] = dxp[p]
            ds_s[p] = dst[p]
        for g in range(2):
            dbc_ref[:, 128 * g:128 * (g + 1)] = dbs[g]
            dbc_ref[:, 256 + 128 * g:256 + 128 * (g + 1)] = dcs[g]
        ddt_ref[...] = ddt
        ddtt_ref[...] = ddtt
        _acc_add(dalr_ref, dalr)
        _acc_add(dalc_ref, dalc)
        _acc_add(ddsk_ref, ddsk)

    rev = lambda width, cb=0: pl.BlockSpec((q, width), lambda i: (nc - 1 - i, cb))
    return pl.pallas_call(
        body,
        out_shape=[jax.ShapeDtypeStruct((lp, SSM_WIDTH), F32), jax.ShapeDtypeStruct((lp, 512), F32),
                   jax.ShapeDtypeStruct((lp, SSM_HEADS), F32), jax.ShapeDtypeStruct((SSM_HEADS, lp), F32),
                   jax.ShapeDtypeStruct((1, SSM_HEADS), F32), jax.ShapeDtypeStruct((SSM_HEADS, 1), F32),
                   jax.ShapeDtypeStruct((1, SSM_HEADS), F32)],
        grid=(nc,),
        in_specs=[rev(SSM_WIDTH), rev(256, 0), rev(256, 1), rev(SSM_HEADS),
                  pl.BlockSpec((SSM_HEADS, q), lambda i: (0, nc - 1 - i)),
                  _full((1, SSM_HEADS)), _full((SSM_HEADS, 1)), _full((1, SSM_HEADS)),
                  pl.BlockSpec((1, N_PAIRS, 128, 128), lambda i: (nc - 1 - i, 0, 0, 0)), rev(SSM_WIDTH)],
        out_specs=[rev(SSM_WIDTH), rev(512), rev(SSM_HEADS), pl.BlockSpec((SSM_HEADS, q), lambda i: (0, nc - 1 - i)),
                   _full((1, SSM_HEADS)), _full((SSM_HEADS, 1)), _full((1, SSM_HEADS))],
        scratch_shapes=[pltpu.VMEM((N_PAIRS, 128, 128), F32)],
        name="ssd_bwd", compiler_params=_params(("arbitrary",)))(xs, bc, bc, dt, dt_t, alr, alc, dsk, sprev, dy)


def _q_to_slab_order(w):
    hd = QK_NOPE + QK_ROPE
    nope = [w[:, h * hd:h * hd + QK_NOPE] for h in range(ATT_HEADS)]
    rope = [w[:, h * hd + QK_NOPE:(h + 1) * hd] for h in range(ATT_HEADS)]
    return jnp.concatenate(nope + rope, axis=1)


def _q_from_slab_order(wp):
    base = ATT_HEADS * QK_NOPE
    parts = []
    for h in range(ATT_HEADS):
        parts += [wp[:, QK_NOPE * h:QK_NOPE * (h + 1)], wp[:, base + QK_ROPE * h:base + QK_ROPE * (h + 1)]]
    return jnp.concatenate(parts, axis=1)


_IN_CQ, _IN_CKV, _IN_KR, _IN_Z, _IN_XS, _IN_BC, _IN_DT = (0, 384), (384, 640), (640, 704), (704, 1728), (1728, 2752), \
    (2752, 3264), (3264, 3280)


def _pack_w_in(w):
    z = lambda n: jnp.zeros((w.shape[0], n), w.dtype)
    s = lambda r: w[:, r[0]:r[1]]
    return jnp.concatenate([s(_IN_CKV), s(_IN_KR), z(64), s(_IN_CQ), s(_IN_DT), z(112), z(128), s(_IN_Z), s(_IN_XS),
                            s(_IN_BC)], axis=1)


def _unpack_w_in(wp):
    s = lambda off, n: wp[:, off:off + n]
    return jnp.concatenate([s(SEG_CQ, 384), s(SEG_KV, 256), s(SEG_KR, 64), s(SEG_Z, 1024), s(SEG_XS, 1024),
                            s(SEG_BC, 512), s(SEG_DT, 16)], axis=1)


def _rope_tables(lp):
    inv_freq = ROPE_THETA ** (-jnp.arange(0, QK_ROPE, 2, dtype=F32) / QK_ROPE)
    ang = jnp.arange(lp, dtype=F32)[:, None] * inv_freq[None, :]
    cos, sin = jnp.cos(ang), jnp.sin(ang)
    return jnp.tile(cos, (1, 4)), jnp.concatenate([-sin, sin, -sin, sin], axis=1)


def _local_step(x, tgt, w):
    n_real = x.shape[0]
    l = N_META + n_real
    lp = -(-l // MM_BLOCK) * MM_BLOCK
    h0 = lax.optimization_barrier(jnp.concatenate([w["meta_tokens"], x, jnp.zeros((lp - l, D_MODEL), F32)], axis=0))
    cos_t, sin_t = _rope_tables(lp)

    w_in_p = _pack_w_in(w["w_in"])
    w_q_p = _q_to_slab_order(w["w_q_up"])
    w_kv, w_out, w_up, w_down = w["w_kv_up"], w["w_out"], w["w_mlp_up"], w["w_mlp_down"]
    conv_w, conv_b = w["conv_w"], w["conv_b"]
    dt_bias_pad = jnp.concatenate([w["dt_bias"], jnp.zeros((1, 128 - SSM_HEADS), F32)], axis=1)
    alr, dsk = w["a_log"], w["d_skip"]
    alc = alr.reshape(SSM_HEADS, 1)

    n1 = _norm_in(h0, w["norm_mix_pre"])
    proj = _mm("proj", n1, w_in_p)
    cqn, ckvn, krr = _attn_prep(proj, w["q_a_norm"], w["kv_a_norm"], cos_t, sin_t)
    q = _mm("q_up", cqn, w_q_p)
    kv = _mm("kv_up", ckvn, w_kv, outs=((BF16, None),))
    qs, ks, vs, vts = _qk_pack(q, kv, krr, cos_t, sin_t)
    att, att16, lse2 = _attn_fwd(qs, ks, vts)
    xs_pre, xs_act = _conv_fwd("conv_xs_fwd", proj, SEG_XS, SSM_WIDTH, conv_w[:, :SSM_WIDTH], conv_b[:, :SSM_WIDTH])
    bc_pre, bc_act = _conv_fwd("conv_bc_fwd", proj, SEG_BC, 512, conv_w[:, SSM_WIDTH:], conv_b[:, SSM_WIDTH:])
    dt = _dt_fwd(proj, dt_bias_pad)[:, :SSM_HEADS]
    dt_t = dt.T
    y, sprev = _ssd_fwd(xs_act, bc_act, dt, dt_t, alr, alc, dsk)
    ssm = _gated_norm_fwd(y, proj, w["ssm_norm"])
    mix = _mm("out_proj", (att16, ssm), w_out)
    h1, n2 = _mix_residual(h0, mix, w["norm_mix_post"], w["norm_mlp_pre"])
    relu2 = lambda r: jnp.square(jnp.maximum(r, 0.0))
    act = _mm("mlp_up", n2, w_up, outs=((BF16, relu2),))
    f = _mm("mlp_down", act, w_down)
    loss, dh2, df, dg_mlp_post = _loss_and_grad(h1, f, w["norm_mlp_post"], tgt, n_real)

    g = {"norm_mlp_post": dg_mlp_post}
    g["w_mlp_down"] = _mm_tn("d_w_mlp_down", act, df)
    du = _mm("d_mlp_act", df, w_down, outs=((BF16, lambda r, ab: r * (2.0 * jnp.sqrt(ab.astype(F32)))),),
             epi_ins=(act,), trans_b=True)
    g["w_mlp_up"] = _mm_tn("d_w_mlp_up", n2, du)
    dn2 = _mm("d_n2", du, w_up, trans_b=True)
    dh1, dmix, g["norm_mlp_pre"], g["norm_mix_post"] = _mlp_residual_bwd(dh2, dn2, h1, w["norm_mlp_pre"], mix,
                                                                         w["norm_mix_post"])
    g["w_out"] = jnp.concatenate([_mm_tn("d_w_out_att", att16, dmix), _mm_tn("d_w_out_ssm", ssm, dmix)], axis=0)
    dcat, dcat16 = _mm("d_cat", dmix, w_out, outs=((F32, None), (BF16, None)), trans_b=True)
    dy, dz, g["ssm_norm"] = _gated_norm_bwd(y, proj, w["ssm_norm"], dcat)
    dxs_act, dbc_act, ddt, ddt_t, dalr, dalc, g["d_skip"] = _ssd_bwd(xs_act, bc_act, dt, dt_t, alr, alc, dsk, sprev, dy)
    g["a_log"] = dalr + dalc.reshape(1, SSM_HEADS)
    dxs, dcw_xs, dcb_xs = _conv_bwd("conv_xs_bwd", dxs_act, xs_pre, proj, SEG_XS, SSM_WIDTH, conv_w[:, :SSM_WIDTH])
    dbc, dcw_bc, dcb_bc = _conv_bwd("conv_bc_bwd", dbc_act, bc_pre, proj, SEG_BC, 512, conv_w[:, SSM_WIDTH:])
    g["conv_w"] = jnp.concatenate([dcw_xs[:CONV_K], dcw_bc[:CONV_K]], axis=1)
    g["conv_b"] = jnp.concatenate([dcb_xs, dcb_bc], axis=1)
    ddt_pad = jnp.concatenate([ddt + ddt_t.T, jnp.zeros((lp, 128 - SSM_HEADS), F32)], axis=1)

    dqs, dks, dvs = _attn_bwd(qs, ks, vs, dcat16, lse2, _attn_delta(dcat, att))
    dq, dkv, dkr = _qk_unpack_bwd(dqs, dks, dvs, cos_t, sin_t)
    g["w_q_up"] = _q_from_slab_order(_mm_tn("d_w_q_up", cqn, dq))
    g["w_kv_up"] = _mm_tn("d_w_kv_up", ckvn, dkv)
    dcqn = _mm("d_cqn", dq, w_q_p, trans_b=True)
    dckvn = _mm("d_ckvn", dkv, w_kv, trans_b=True)
    dproj, g["q_a_norm"], g["kv_a_norm"], ddtb = _proj_grad(proj, dcqn, dckvn, w["q_a_norm"], w["kv_a_norm"], dkr,
                                                          ddt_pad, dt_bias_pad, dz, dxs, dbc)
    g["dt_bias"] = ddtb[:, :SSM_HEADS]
    g["w_in"] = _unpack_w_in(_mm_tn("d_w_in", n1, dproj))
    dn1 = _mm("d_n1", dproj, w_in_p, trans_b=True)
    dh0, g["norm_mix_pre"] = _input_norm_bwd(dh1, dn1, h0, w["norm_mix_pre"])
    g["meta_tokens"] = dh0[:N_META]
    return loss, dh0, g


WEIGHTS = ["meta_tokens", "norm_mix_pre", "w_in", "q_a_norm", "w_q_up", "kv_a_norm", "w_kv_up", "conv_w", "conv_b",
           "dt_bias", "a_log", "d_skip", "ssm_norm", "w_out", "norm_mix_post", "norm_mlp_pre", "w_mlp_up",
           "w_mlp_down", "norm_mlp_post"]
SHARD_AXIS = {"meta_tokens": 1, "w_in": 1, "w_q_up": 1, "w_kv_up": 1, "conv_w": 1, "w_out": 0, "w_mlp_up": 1,
              "w_mlp_down": 0}
FULL_SHAPE = {"meta_tokens": (16, 1024), "norm_mix_pre": (1, 1024), "w_in": (1024, 3280), "q_a_norm": (1, 384),
              "w_q_up": (384, 1536), "kv_a_norm": (1, 256), "w_kv_up": (256, 2048), "conv_w": (4, 1536),
              "conv_b": (1, 1536), "dt_bias": (1, 16), "a_log": (1, 16), "d_skip": (1, 16), "ssm_norm": (1, 1024),
              "w_out": (2048, 1024), "norm_mix_post": (1, 1024), "norm_mlp_pre": (1, 1024), "w_mlp_up": (1024, 4096),
              "w_mlp_down": (4096, 1024), "norm_mlp_post": (1, 1024)}
GATHER_BF16 = ["w_in", "w_q_up", "w_kv_up", "w_out", "w_mlp_up", "w_mlp_down"]
GATHER_F32 = ["meta_tokens", "conv_w"]
ADAM_NATURAL = GATHER_BF16


def _shard_shape(name):
    shp = list(FULL_SHAPE[name])
    if name in SHARD_AXIS:
        shp[SHARD_AXIS[name]] //= N_CHIPS
    return tuple(shp)


PACK_ORDER = sorted(WEIGHTS, key=lambda n: -_shard_shape(n)[0])


def _packed_rows(shape):
    r, c = shape
    return r if c <= PACK_W else -(-c // PACK_W)


def _pack_rows(arrays, row_multiple):
    parts = []
    for a in arrays:
        r, c = a.shape
        if c > PACK_W:
            assert r == 1, a.shape
            folded = _packed_rows(a.shape)
            a = jnp.pad(a, ((0, 0), (0, folded * PACK_W - c))).reshape(folded, PACK_W)
        elif c < PACK_W:
            a = jnp.pad(a, ((0, 0), (0, PACK_W - c)))
        parts.append(a)
    rows = sum(p.shape[0] for p in parts)
    if rows % row_multiple:
        parts.append(jnp.zeros((row_multiple - rows % row_multiple, PACK_W), parts[0].dtype))
    return jnp.concatenate(parts, axis=0)


def _unpack_rows(packed, shapes):
    out, off = [], 0
    for r, c in shapes:
        nr = _packed_rows((r, c))
        blk = packed[off:off + nr]
        out.append(blk[:, :c] if c <= PACK_W else blk.reshape(1, nr * PACK_W)[:, :c])
        off += nr
    return out


def _chip_slice(full, name, t):
    if name not in SHARD_AXIS:
        return full
    ax = SHARD_AXIS[name]
    n = FULL_SHAPE[name][ax] // N_CHIPS
    return lax.slice_in_dim(full, t * n, (t + 1) * n, axis=ax)


HBM_SPEC = pl.BlockSpec(memory_space=pl.ANY)
CHIP_FLIPS = ((1, 0), (0, 1), (1, 1))


def _gather_chips(bufs):
    nb = len(bufs)

    def body(*refs):
        ins, outs = refs[:nb], refs[nb:2 * nb]
        send, recv, loc = refs[2 * nb:]
        x, y, c = lax.axis_index("x"), lax.axis_index("y"), lax.axis_index("c")
        me = 2 * x + y
        sibling = (x, y, 1 - c)
        sends, forwards = [], []
        for b in range(nb):
            half = bufs[b].shape[0] // 2
            mine = pl.ds(c * half, half)
            own = pltpu.make_async_copy(ins[b], outs[b].at[me], loc.at[b])
            own.start()
            sends.append(own)
            for k, (fx, fy) in enumerate(CHIP_FLIPS):
                cp = pltpu.make_async_remote_copy(
                    src_ref=ins[b].at[mine], dst_ref=outs[b].at[me, mine], send_sem=send.at[b, k],
                    recv_sem=recv.at[b, k], device_id=(x ^ fx, y ^ fy, c), device_id_type=MESH_ID)
                cp.start()
                sends.append(cp)
        for b in range(nb):
            half = bufs[b].shape[0] // 2
            mine, theirs = pl.ds(c * half, half), pl.ds((1 - c) * half, half)
            for k, (fx, fy) in enumerate(CHIP_FLIPS):
                chip = 2 * (x ^ fx) + (y ^ fy)
                landed = outs[b].at[chip, mine]
                pltpu.make_async_remote_copy(src_ref=landed, dst_ref=landed, send_sem=send.at[b, k],
                                             recv_sem=recv.at[b, k], device_id=sibling,
                                             device_id_type=MESH_ID).wait_recv()
                fw = pltpu.make_async_remote_copy(src_ref=landed, dst_ref=landed, send_sem=send.at[b, 3 + k],
                                                  recv_sem=recv.at[b, 3 + k], device_id=sibling,
                                                  device_id_type=MESH_ID)
                fw.start()
                forwards.append((fw, outs[b].at[chip, theirs], b, k))
        for fw, arriving, b, k in forwards:
            pltpu.make_async_remote_copy(src_ref=arriving, dst_ref=arriving, send_sem=send.at[b, 3 + k],
                                         recv_sem=recv.at[b, 3 + k], device_id=sibling,
                                         device_id_type=MESH_ID).wait_recv()
            fw.wait_send()
        for cp in sends[1::4] + sends[2::4] + sends[3::4]:
            cp.wait_send()
        for own in sends[0::4]:
            own.wait()

    return pl.pallas_call(
        body,
        out_shape=[jax.ShapeDtypeStruct((N_CHIPS,) + b.shape, b.dtype) for b in bufs],
        in_specs=[HBM_SPEC] * nb, out_specs=[HBM_SPEC] * nb,
        scratch_shapes=[pltpu.SemaphoreType.DMA((nb, 6)), pltpu.SemaphoreType.DMA((nb, 6)),
                        pltpu.SemaphoreType.DMA((nb,))],
        name="gather_chips")(*bufs)


def _sibling_swap(name, buf):
    def body(src, dst, send, recv):
        x, y, c = lax.axis_index("x"), lax.axis_index("y"), lax.axis_index("c")
        cp = pltpu.make_async_remote_copy(src_ref=src, dst_ref=dst, send_sem=send, recv_sem=recv,
                                          device_id=(x, y, 1 - c), device_id_type=MESH_ID)
        cp.start()
        cp.wait()

    return pl.pallas_call(
        body, out_shape=jax.ShapeDtypeStruct(buf.shape, buf.dtype), in_specs=[HBM_SPEC], out_specs=HBM_SPEC,
        scratch_shapes=[pltpu.SemaphoreType.DMA, pltpu.SemaphoreType.DMA], name=name)(buf)


def _scatter_chips(parts):
    nb = len(parts)

    def body(*refs):
        srcs, dsts = refs[:nb], refs[nb:2 * nb]
        send, recv = refs[2 * nb:]
        x, y, c = lax.axis_index("x"), lax.axis_index("y"), lax.axis_index("c")
        copies = []
        for b in range(nb):
            for k, (fx, fy) in enumerate(CHIP_FLIPS):
                tx, ty = x ^ fx, y ^ fy
                cp = pltpu.make_async_remote_copy(
                    src_ref=srcs[b].at[2 * tx + ty], dst_ref=dsts[b].at[k], send_sem=send.at[b, k],
                    recv_sem=recv.at[b, k], device_id=(tx, ty, c), device_id_type=MESH_ID)
                cp.start()
                copies.append(cp)
        for cp in copies:
            cp.wait()

    return pl.pallas_call(
        body, out_shape=[jax.ShapeDtypeStruct((3,) + p.shape[1:], p.dtype) for p in parts], in_specs=[HBM_SPEC] * nb,
        out_specs=[HBM_SPEC] * nb,
        scratch_shapes=[pltpu.SemaphoreType.DMA((nb, 3)), pltpu.SemaphoreType.DMA((nb, 3))],
        name="scatter_chips")(*parts)


def _add_rows(name, terms, also_bf16=False):
    rows = terms[0].shape[0]
    t = _row_tile(rows)
    n_out = 2 if also_bf16 else 1

    def body(*refs):
        acc = refs[0][...].astype(F32)
        for r in refs[1:-n_out]:
            acc = acc + r[...].astype(F32)
        refs[-n_out][...] = acc
        if also_bf16:
            refs[-1][...] = acc.astype(BF16)

    res = pl.pallas_call(
        body, out_shape=[jax.ShapeDtypeStruct(terms[0].shape, dt) for dt in (F32, BF16)[:n_out]], grid=(rows // t,),
        in_specs=[_rb(t, PACK_W)] * len(terms), out_specs=[_rb(t, PACK_W)] * n_out,
        name=name, compiler_params=_params(("parallel",)))(*terms)
    return res if also_bf16 else res[0]


def _sum_chip_order(name, parts, received, me):
    stack = jnp.concatenate([lax.dynamic_index_in_dim(parts, me, axis=0, keepdims=True), received], axis=0)
    terms = []
    for chip in range(N_CHIPS):
        xr = me ^ chip
        where = jnp.where(xr == 0, 0, jnp.where(xr == 2, 1, jnp.where(xr == 1, 2, 3)))
        terms.append(lax.dynamic_index_in_dim(stack, where, axis=0, keepdims=False))
    return _add_rows(name, terms)


def _row_tile(rows):
    assert rows % 16 == 0, rows
    return max(t for t in range(16, 513, 16) if rows % t == 0)


def _adamw(name, g, w, m, v):
    rows, cols = g.shape
    t = max(tt for tt in range(8, 257, 8) if rows % tt == 0)
    c1 = 1.0 - ADAM_B1 ** ADAM_STEP
    c2 = 1.0 - ADAM_B2 ** ADAM_STEP

    def body(g_ref, w_ref, m_ref, v_ref, d_ref, mo_ref, vo_ref):
        gg = g_ref[...]
        mn = ADAM_B1 * m_ref[...] + (1.0 - ADAM_B1) * gg
        vn = ADAM_B2 * v_ref[...] + (1.0 - ADAM_B2) * (gg * gg)
        d_ref[...] = -ADAM_LR * ((mn / c1) / (jnp.sqrt(vn / c2) + ADAM_EPS) + ADAM_WD * w_ref[...])
        mo_ref[...] = mn
        vo_ref[...] = vn

    return pl.pallas_call(
        body, out_shape=[jax.ShapeDtypeStruct(g.shape, F32)] * 3, grid=(rows // t,),
        in_specs=[_rb(t, cols)] * 4, out_specs=[_rb(t, cols)] * 3,
        name=name, compiler_params=_params(("parallel",)))(g, w, m, v)


def kernel(x, meta_tokens, norm_mix_pre, w_in, q_a_norm, w_q_up, kv_a_norm, w_kv_up, conv_w, conv_b, dt_bias, a_log, d_skip, ssm_norm, w_out, norm_mix_post, norm_mlp_pre, w_mlp_up, w_mlp_down, norm_mlp_post, loss_target, m_meta_tokens, m_norm_mix_pre, m_w_in, m_q_a_norm, m_w_q_up, m_kv_a_norm, m_w_kv_up, m_conv_w, m_conv_b, m_dt_bias, m_a_log, m_d_skip, m_ssm_norm, m_w_out, m_norm_mix_post, m_norm_mlp_pre, m_w_mlp_up, m_w_mlp_down, m_norm_mlp_post, v_meta_tokens, v_norm_mix_pre, v_w_in, v_q_a_norm, v_w_q_up, v_kv_a_norm, v_w_kv_up, v_conv_w, v_conv_b, v_dt_bias, v_a_log, v_d_skip, v_ssm_norm, v_w_out, v_norm_mix_post, v_norm_mlp_pre, v_w_mlp_up, v_w_mlp_down, v_norm_mlp_post):
    given = dict(locals())
    drop = lambda name, a: a[0] if a.ndim == 3 else a
    w_loc = {n: drop(n, given[n]) for n in WEIGHTS}
    m_loc = {n: drop(n, given["m_" + n]) for n in WEIGHTS}
    v_loc = {n: drop(n, given["v_" + n]) for n in WEIGHTS}
    ix, iy, ic = lax.axis_index("x"), lax.axis_index("y"), lax.axis_index("c")
    me = 2 * ix + iy

    sent16 = _pack_rows([w_loc[n].astype(BF16) for n in GATHER_BF16], 32)
    sent32 = _pack_rows([w_loc[n] for n in GATHER_F32], 16)
    got16, got32 = _gather_chips([sent16, sent32])
    w_full = {n: w_loc[n] for n in WEIGHTS if n not in SHARD_AXIS}
    for names, got in ((GATHER_BF16, got16), (GATHER_F32, got32)):
        per_chip = [_unpack_rows(got[t], [_shard_shape(n) for n in names]) for t in range(N_CHIPS)]
        for k, n in enumerate(names):
            w_full[n] = jnp.concatenate([per_chip[t][k] for t in range(N_CHIPS)], axis=SHARD_AXIS[n])

    loss, dh0, g_full = _local_step(x[0], loss_target[0], w_full)
    n_real = x.shape[1]
    grad_x = dh0[N_META:N_META + n_real][None]

    shapes = [_shard_shape(n) for n in PACK_ORDER]
    slots = [_pack_rows([_chip_slice(g_full[n], n, t) for n in PACK_ORDER], 16) for t in range(N_CHIPS)]
    rows = slots[0].shape[0]
    half = rows // 2
    halves = lambda hh: jnp.concatenate([lax.dynamic_slice_in_dim(s, hh * half, half, axis=0) for s in slots], axis=0)
    keep, give = halves(ic), halves(1 - ic)
    from_sibling = _sibling_swap("sibling_swap", give)
    part32, part16 = _add_rows("chip_partial", [keep, from_sibling], also_bf16=True)
    part32 = part32.reshape(N_CHIPS, half, PACK_W)
    part16 = part16.reshape(N_CHIPS, half, PACK_W)
    tail_start = rows - sum(_packed_rows(_shard_shape(n)) for n in PACK_ORDER if n not in SHARD_AXIS)
    assert tail_start + TAIL_ROWS >= rows and all(n in SHARD_AXIS for n in PACK_ORDER[:len(SHARD_AXIS)])
    tail32 = part32[:, half - TAIL_ROWS:, :]
    from16, from_tail = _scatter_chips([part16, tail32])
    my_half = _sum_chip_order("chip_total", part16, from16, me)
    my_tail = _sum_chip_order("chip_total_tail", tail32, from_tail, me)
    my_half = lax.dynamic_update_slice(my_half, my_tail, (half - TAIL_ROWS, 0))
    other_half = _sibling_swap("sibling_gather", my_half)
    g_red = jnp.where(ic == 0, jnp.concatenate([my_half, other_half], axis=0),
                      jnp.concatenate([other_half, my_half], axis=0))

    g_own = dict(zip(PACK_ORDER, _unpack_rows(g_red, shapes)))
    small = [n for n in PACK_ORDER if n not in ADAM_NATURAL]
    pack_small = lambda d: _pack_rows([d[n] for n in small], 16)
    packed_upd = _adamw("adamw_small", pack_small(g_own), pack_small(w_loc), pack_small(m_loc), pack_small(v_loc))
    upd = [dict(zip(small, _unpack_rows(p, [_shard_shape(n) for n in small]))) for p in packed_upd]
    for n in ADAM_NATURAL:
        for k, res in enumerate(_adamw("adamw_" + n, g_own[n], w_loc[n], m_loc[n], v_loc[n])):
            upd[k][n] = res

    def outputs(parts):
        return [parts[n][None] if given[n].ndim == 3 else parts[n] for n in WEIGHTS]

    total = lax.psum(loss[0, 0], ("x", "y", "c"))
    return (total, grad_x, *outputs(g_own), *outputs(upd[0]), *outputs(upd[1]), *outputs(upd[2]))
```

```python
import functools

import numpy as np
import jax
import jax.numpy as jnp
from jax import lax
from jax.experimental import pallas as pl
from jax.experimental.pallas import tpu as pltpu

F32 = jnp.float32
BF16 = jnp.bfloat16

D_MODEL = 1024
N_META = 16
EPS = 1e-6
ATT_HEADS = 8
Q_LORA = 384
KV_LORA = 256
QK_NOPE = 128
QK_ROPE = 64
V_HEAD = 128
ROPE_THETA = 10000.0
SSM_HEADS = 16
SSM_HEAD_DIM = 64
SSM_WIDTH = 1024
SSM_STATE = 128
CONV_K = 4
D_FF = 4096
ATT_SCALE = float((QK_NOPE + QK_ROPE) ** -0.5)
ATT_SCALE_LOG2 = float(ATT_SCALE * np.log2(np.e))

ADAM_LR = 0.001
ADAM_B1 = 0.9
ADAM_B2 = 0.999
ADAM_EPS = 1e-08
ADAM_WD = 0.01
ADAM_STEP = 10

SEG_KV, SEG_KR, SEG_CQ, SEG_DT, SEG_Z, SEG_XS, SEG_BC = 0, 256, 384, 768, 1024, 2048, 3072
PROJ_W = 3584
QP_W = 256

ROW_BLOCK = 512
MM_BLOCK = 512
SSD_CHUNK = 256
ATT_SPLIT = 1
ATT_BLOCK = 768
ATT_BLOCK_FWD = 1536
ATT_BLOCK_Q_BWD = 1536
VMEM_LIMIT = 56 * 1024 * 1024
NEG_BIG = -1e30

PACK_W = 1024
TAIL_ROWS = 352
SLOT_ROWS_MULTIPLE = 512
N_CHIPS = 4
MESH_ID = pl.DeviceIdType.MESH


def _params(sem):
    return pltpu.CompilerParams(dimension_semantics=sem, vmem_limit_bytes=VMEM_LIMIT)


def _rb(rows, width, cb=0):
    return pl.BlockSpec((rows, width), lambda i: (i, cb))


def _full(shape):
    zeros = (0,) * len(shape)
    return pl.BlockSpec(shape, lambda i: zeros)


def _acc_add(ref, val):
    first = pl.program_id(0) == 0

    @pl.when(first)
    def _():
        ref[...] = val

    @pl.when(jnp.logical_not(first))
    def _():
        ref[...] += val


def _rms(x, g):
    r = lax.rsqrt(jnp.mean(x * x, axis=-1, keepdims=True) + EPS)
    return x * r * g


def _rms_bwd(x, g, dy):
    r = lax.rsqrt(jnp.mean(x * x, axis=-1, keepdims=True) + EPS)
    dyg = dy * g
    dx = r * dyg - x * (r * r * r) * jnp.mean(x * dyg, axis=-1, keepdims=True)
    dg = jnp.sum(dy * x * r, axis=0, keepdims=True)
    return dx, dg


def _sigmoid(x):
    return 1.0 / (1.0 + jnp.exp(-x))


def _swap32(x):
    lane = lax.broadcasted_iota(jnp.int32, x.shape, 1)
    return jnp.where((lane % 64) < 32, pltpu.roll(x, 96, 1), pltpu.roll(x, 32, 1))


def _rope(x, cos_t, sin_t):
    return x * cos_t + _swap32(x) * sin_t


def _rope_bwd(dr, cos_t, sin_t):
    return dr * cos_t + _swap32(dr * sin_t)


def _tile(n, cap):
    if n <= cap:
        return n
    best = 128
    for t in range(128, cap + 1, 128):
        if n % t == 0:
            best = t
    assert n % best == 0, (n, cap)
    return best


MM_VMEM_BUDGET = 40 * 1024 * 1024
TN_ROWS_CAP = 1536


def _mm(name, a, b, outs=((F32, None),), epi_ins=(), trans_b=False):
    a_parts = tuple(a) if isinstance(a, (tuple, list)) else (a,)
    assert len(a_parts) == 1 or not trans_b
    m = a_parts[0].shape[0]
    widths = [p.shape[1] for p in a_parts]
    k = sum(widths)
    n = b.shape[0] if trans_b else b.shape[1]
    tm = MM_BLOCK
    n_a, n_epi = len(a_parts), len(epi_ins)
    out_bytes = sum(jnp.dtype(dt).itemsize for dt, _ in outs) + sum(e.dtype.itemsize for e in epi_ins)
    a_bytes = sum(p.shape[1] * p.dtype.itemsize for p in a_parts)
    step_bytes = lambda tn: 2 * (tm * a_bytes + k * tn * b.dtype.itemsize + tm * tn * out_bytes)
    tn = n
    while step_bytes(tn) > MM_VMEM_BUDGET and tn % 256 == 0:
        tn //= 2
    assert n % tn == 0 and step_bytes(tn) <= MM_VMEM_BUDGET, (name, n, tn)

    def body(*refs):
        a_refs, b_ref = refs[:n_a], refs[n_a]
        epi_refs = refs[n_a + 1:n_a + 1 + n_epi]
        out_refs = refs[n_a + 1 + n_epi:]
        if trans_b:
            r = _nt(a_refs[0][...].astype(BF16), b_ref[...].astype(BF16))
        else:
            r, off = None, 0
            for a_ref, wd in zip(a_refs, widths):
                part = jnp.dot(a_ref[...].astype(BF16), b_ref[off:off + wd, :].astype(BF16),
                               preferred_element_type=F32)
                r = part if r is None else r + part
                off += wd
        blocks = [e[...] for e in epi_refs]
        for o_ref, (dt, fn) in zip(out_refs, outs):
            o_ref[...] = (r if fn is None else fn(r, *blocks)).astype(dt)

    out_spec = pl.BlockSpec((tm, tn), lambda j, i: (i, j))
    b_spec = pl.BlockSpec((tn, k), lambda j, i: (j, 0)) if trans_b else pl.BlockSpec((k, tn), lambda j, i: (0, j))
    res = pl.pallas_call(
        body,
        out_shape=[jax.ShapeDtypeStruct((m, n), dt) for dt, _ in outs],
        grid=(n // tn, m // tm),
        in_specs=[pl.BlockSpec((tm, wd), lambda j, i: (i, 0)) for wd in widths] + [b_spec] + [out_spec] * n_epi,
        out_specs=[out_spec] * len(outs),
        name=name,
        compiler_params=_params(("parallel", "parallel")),
    )(*a_parts, b, *epi_ins)
    return res[0] if len(outs) == 1 else res


def _mm_tn(name, x, dy, ta_cap=1024, tn_cap=1024):
    l, a = x.shape
    n = dy.shape[1]
    ta, tn = _tile(a, ta_cap), _tile(n, tn_cap)
    tl = max(t for t in range(MM_BLOCK, TN_ROWS_CAP + 1, MM_BLOCK) if l % t == 0)
    nl = l // tl

    def body(x_ref, dy_ref, o_ref):
        ll = pl.program_id(2)

        @pl.when(ll == 0)
        def _():
            o_ref[...] = jnp.zeros_like(o_ref)

        o_ref[...] += lax.dot_general(x_ref[...].astype(BF16), dy_ref[...].astype(BF16), (((0,), (0,)), ((), ())),
                                      preferred_element_type=F32)

    return pl.pallas_call(
        body,
        out_shape=jax.ShapeDtypeStruct((a, n), F32),
        grid=(a // ta, n // tn, nl),
        in_specs=[pl.BlockSpec((tl, ta), lambda i, j, ll: (ll, i)), pl.BlockSpec((tl, tn), lambda i, j, ll: (ll, j))],
        out_specs=pl.BlockSpec((ta, tn), lambda i, j, ll: (i, j)),
        name=name,
        compiler_params=_params(("parallel", "parallel", "arbitrary")),
    )(x, dy)


def _norm_in(h0, g_pre):
    lp = h0.shape[0]
    t = ROW_BLOCK

    def body(h_ref, g_ref, o_ref):
        o_ref[...] = _rms(h_ref[...], g_ref[...]).astype(BF16)

    return pl.pallas_call(
        body, out_shape=jax.ShapeDtypeStruct((lp, D_MODEL), BF16), grid=(lp // t,),
        in_specs=[_rb(t, D_MODEL), _full((1, D_MODEL))], out_specs=_rb(t, D_MODEL),
        name="norm_in", compiler_params=_params(("parallel",)))(h0, g_pre)


def _attn_prep(proj, g_q, g_kv, cos_t, sin_t):
    lp = proj.shape[0]
    t = ROW_BLOCK

    def body(ckv_ref, kr_ref, cq_ref, gq_ref, gkv_ref, cos_ref, sin_ref, cqn_ref, ckvn_ref, krr_ref):
        cqn_ref[...] = _rms(cq_ref[...], gq_ref[...]).astype(BF16)
        ckvn_ref[...] = _rms(ckv_ref[...], gkv_ref[...]).astype(BF16)
        roped = _rope(kr_ref[...], cos_ref[...], sin_ref[...])
        krr_ref[...] = roped + pltpu.roll(roped, 64, 1)

    return pl.pallas_call(
        body,
        out_shape=[jax.ShapeDtypeStruct((lp, Q_LORA), BF16), jax.ShapeDtypeStruct((lp, KV_LORA), BF16),
                   jax.ShapeDtypeStruct((lp, 128), F32)],
        grid=(lp // t,),
        in_specs=[_rb(t, KV_LORA, SEG_KV // KV_LORA), _rb(t, 128, SEG_KR // 128), _rb(t, Q_LORA, SEG_CQ // Q_LORA),
                  _full((1, Q_LORA)), _full((1, KV_LORA)), _rb(t, 128), _rb(t, 128)],
        out_specs=[_rb(t, Q_LORA), _rb(t, KV_LORA), _rb(t, 128)],
        name="attn_prep", compiler_params=_params(("parallel",)))(proj, proj, proj, g_q, g_kv, cos_t, sin_t)


def _qk_pack(q, kv, krr, cos_t, sin_t):
    lp = q.shape[0]
    t = ROW_BLOCK

    def body(q_ref, kv_ref, krr_ref, cos_ref, sin_ref, qs_ref, ks_ref, vs_ref, vts_ref):
        lane = lax.broadcasted_iota(jnp.int32, (t, 128), 1)
        lo = lane < 64
        krr = krr_ref[...].astype(BF16)
        for j in range(ATT_HEADS // 2):
            pr = _rope(q_ref[:, 1024 + 128 * j:1024 + 128 * (j + 1)], cos_ref[...], sin_ref[...])
            for h, keep in ((2 * j, lo), (2 * j + 1, jnp.logical_not(lo))):
                qs_ref[h, :, 0:128] = q_ref[:, 128 * h:128 * (h + 1)].astype(BF16)
                qs_ref[h, :, 128:256] = jnp.where(keep, pr, 0.0).astype(BF16)
        for h in range(ATT_HEADS):
            ks_ref[h, :, 0:128] = kv_ref[:, 256 * h:256 * h + 128].astype(BF16)
            ks_ref[h, :, 128:256] = krr
            v = kv_ref[:, 256 * h + 128:256 * (h + 1)]
            vs_ref[h] = v.astype(BF16)
            vts_ref[h] = v.astype(F32).T.astype(BF16)

    slab = lambda w: pl.BlockSpec((ATT_HEADS, t, w), lambda i: (0, i, 0))
    return pl.pallas_call(
        body,
        out_shape=[jax.ShapeDtypeStruct((ATT_HEADS, lp, QP_W), BF16), jax.ShapeDtypeStruct((ATT_HEADS, lp, QP_W), BF16),
                   jax.ShapeDtypeStruct((ATT_HEADS, lp, V_HEAD), BF16), jax.ShapeDtypeStruct((ATT_HEADS, V_HEAD, lp), BF16)],
        grid=(lp // t,),
        in_specs=[_rb(t, 1536), _rb(t, 2048), _rb(t, 128), _rb(t, 128), _rb(t, 128)],
        out_specs=[slab(QP_W), slab(QP_W), slab(V_HEAD), pl.BlockSpec((ATT_HEADS, V_HEAD, t), lambda i: (0, 0, i))],
        name="qk_pack", compiler_params=_params(("parallel",)))(q, kv, krr, cos_t, sin_t)


def _shifted(ext, t, shift):
    if shift == 0:
        return ext[8:, :]
    return pltpu.roll(ext, shift, 0)[8:, :]


def _conv_fwd(name, proj, seg, width, conv_w, conv_b):
    lp = proj.shape[0]
    t = ROW_BLOCK
    cb = seg // width

    def body(u_ref, halo_ref, w_ref, b_ref, pre_ref, act_ref):
        i = pl.program_id(0)
        u = u_ref[...]
        halo = jnp.where(i > 0, halo_ref[...], 0.0)
        ext = jnp.concatenate([halo, u], axis=0)
        pre = jnp.broadcast_to(b_ref[...], (t, width))
        for k in range(CONV_K):
            pre = pre + w_ref[k:k + 1, :] * _shifted(ext, t, CONV_K - 1 - k)
        pre_ref[...] = pre
        act_ref[...] = pre * _sigmoid(pre)

    return pl.pallas_call(
        body,
        out_shape=[jax.ShapeDtypeStruct((lp, width), F32)] * 2,
        grid=(lp // t,),
        in_specs=[_rb(t, width, cb),
                  pl.BlockSpec((8, width), lambda i: (jnp.maximum(i * (t // 8) - 1, 0), cb)),
                  _full((CONV_K, width)), _full((1, width))],
        out_specs=[_rb(t, width), _rb(t, width)],
        name=name, compiler_params=_params(("parallel",)))(proj, proj, conv_w, conv_b)


def _softplus(x):
    return jnp.maximum(x, 0.0) + jnp.log1p(jnp.exp(-jnp.abs(x)))


def _dt_fwd(proj, dt_bias_pad):
    lp = proj.shape[0]
    t = ROW_BLOCK

    def body(x_ref, b_ref, o_ref):
        o_ref[...] = _softplus(x_ref[...] + b_ref[...])

    return pl.pallas_call(
        body, out_shape=jax.ShapeDtypeStruct((lp, 128), F32), grid=(lp // t,),
        in_specs=[_rb(t, 128, SEG_DT // 128), _full((1, 128))], out_specs=_rb(t, 128),
        name="dt_fwd", compiler_params=_params(("parallel",)))(proj, dt_bias_pad)


def _gated_norm_group(y, z, w):
    g = y * (z * _sigmoid(z))
    return g * lax.rsqrt(jnp.mean(g * g, axis=-1, keepdims=True) + EPS) * w


def _gated_norm_fwd(y, proj, w):
    lp = y.shape[0]
    t = ROW_BLOCK
    gw = SSM_WIDTH // 2

    def body(y0, y1, z0, z1, w0, w1, o_ref):
        o_ref[:, 0:gw] = _gated_norm_group(y0[...], z0[...], w0[...]).astype(BF16)
        o_ref[:, gw:] = _gated_norm_group(y1[...], z1[...], w1[...]).astype(BF16)

    zb = SEG_Z // gw
    return pl.pallas_call(
        body, out_shape=jax.ShapeDtypeStruct((lp, SSM_WIDTH), BF16), grid=(lp // t,),
        in_specs=[_rb(t, gw, 0), _rb(t, gw, 1), _rb(t, gw, zb), _rb(t, gw, zb + 1),
                  pl.BlockSpec((1, gw), lambda i: (0, 0)), pl.BlockSpec((1, gw), lambda i: (0, 1))],
        out_specs=_rb(t, SSM_WIDTH),
        name="gated_norm_fwd", compiler_params=_params(("parallel",)))(y, y, proj, proj, w, w)


def _gated_norm_bwd(y, proj, w, dssm):
    lp = y.shape[0]
    t = ROW_BLOCK
    gw = SSM_WIDTH // 2

    def body(y0, y1, z0, z1, w0, w1, d0, d1, dy_ref, dz_ref, dw_ref):
        dws = []
        for g, (yr, zr, wr, dr) in enumerate(((y0, z0, w0, d0), (y1, z1, w1, d1))):
            _, vjp = jax.vjp(_gated_norm_group, yr[...], zr[...], wr[...])
            dyg, dzg, dwg = vjp(dr[...])
            dy_ref[:, g * gw:(g + 1) * gw] = dyg
            dz_ref[:, g * gw:(g + 1) * gw] = dzg
            dws.append(dwg)
        _acc_add(dw_ref, jnp.concatenate(dws, axis=1))

    zb = SEG_Z // gw
    return pl.pallas_call(
        body,
        out_shape=[jax.ShapeDtypeStruct((lp, SSM_WIDTH), F32), jax.ShapeDtypeStruct((lp, SSM_WIDTH), F32),
                   jax.ShapeDtypeStruct((1, SSM_WIDTH), F32)],
        grid=(lp // t,),
        in_specs=[_rb(t, gw, 0), _rb(t, gw, 1), _rb(t, gw, zb), _rb(t, gw, zb + 1),
                  pl.BlockSpec((1, gw), lambda i: (0, 0)), pl.BlockSpec((1, gw), lambda i: (0, 1)),
                  _rb(t, gw, 2), _rb(t, gw, 3)],
        out_specs=[_rb(t, SSM_WIDTH), _rb(t, SSM_WIDTH), _full((1, SSM_WIDTH))],
        name="gated_norm_bwd", compiler_params=_params(("arbitrary",)))(y, y, proj, proj, w, w, dssm, dssm)


def _mix_residual(h0, mix, g_post, g_mlp_pre):
    lp = h0.shape[0]
    t = ROW_BLOCK

    def body(h_ref, m_ref, gp_ref, gm_ref, h1_ref, n2_ref):
        h1 = h_ref[...] + _rms(m_ref[...], gp_ref[...])
        h1_ref[...] = h1
        n2_ref[...] = _rms(h1, gm_ref[...]).astype(BF16)

    return pl.pallas_call(
        body, out_shape=[jax.ShapeDtypeStruct((lp, D_MODEL), F32), jax.ShapeDtypeStruct((lp, D_MODEL), BF16)],
        grid=(lp // t,),
        in_specs=[_rb(t, D_MODEL), _rb(t, D_MODEL), _full((1, D_MODEL)), _full((1, D_MODEL))],
        out_specs=[_rb(t, D_MODEL), _rb(t, D_MODEL)],
        name="mix_residual", compiler_params=_params(("parallel",)))(h0, mix, g_post, g_mlp_pre)


def _loss_and_grad(h1, f, g_post, tgt, n_real):
    lp = h1.shape[0]
    t = ROW_BLOCK
    assert n_real % t == 0 and t % N_META == 0
    n_tb = n_real // t

    def body(h1_ref, f_ref, g_ref, halo_ref, t_ref, loss_ref, dh2_ref, df_ref, dg_ref):
        i = pl.program_id(0)
        fx = f_ref[...]
        h2 = h1_ref[...] + _rms(fx, g_ref[...])
        row = i * t + lax.broadcasted_iota(jnp.int32, (t, 1), 0)
        real = jnp.logical_and(row >= N_META, row < N_META + n_real)
        target = jnp.concatenate([halo_ref[...], t_ref[0:t - N_META, :]], axis=0)
        diff = jnp.where(real, h2 - target, 0.0)
        part = 0.5 * jnp.sum(jnp.sum(diff * diff, axis=-1, keepdims=True) / D_MODEL, axis=0, keepdims=True)
        _acc_add(loss_ref, jnp.broadcast_to(part, (1, 128)))
        dh2 = diff / D_MODEL
        dh2_ref[...] = dh2
        dfx, dg = _rms_bwd(fx, g_ref[...], dh2)
        df_ref[...] = dfx.astype(BF16)
        _acc_add(dg_ref, dg)

    return pl.pallas_call(
        body,
        out_shape=[jax.ShapeDtypeStruct((1, 128), F32), jax.ShapeDtypeStruct((lp, D_MODEL), F32),
                   jax.ShapeDtypeStruct((lp, D_MODEL), BF16), jax.ShapeDtypeStruct((1, D_MODEL), F32)],
        grid=(lp // t,),
        in_specs=[_rb(t, D_MODEL), _rb(t, D_MODEL), _full((1, D_MODEL)),
                  pl.BlockSpec((N_META, D_MODEL),
                               lambda i: (jnp.clip(i * (t // N_META) - 1, 0, n_real // N_META - 1), 0)),
                  pl.BlockSpec((t, D_MODEL), lambda i: (jnp.minimum(i, n_tb - 1), 0))],
        out_specs=[_full((1, 128)), _rb(t, D_MODEL), _rb(t, D_MODEL), _full((1, D_MODEL))],
        name="loss_and_grad", compiler_params=_params(("arbitrary",)))(h1, f, g_post, tgt, tgt)


def _mlp_residual_bwd(dh2, dn2, h1, g_mlp_pre, mix, g_post):
    lp = h1.shape[0]
    t = ROW_BLOCK

    def body(dh2_ref, dn2_ref, h1_ref, gm_ref, mix_ref, gp_ref, dh1_ref, dmix_ref, dgm_ref, dgp_ref):
        dx, dgm = _rms_bwd(h1_ref[...], gm_ref[...], dn2_ref[...])
        dh1 = dh2_ref[...] + dx
        dh1_ref[...] = dh1
        dmix, dgp = _rms_bwd(mix_ref[...], gp_ref[...], dh1)
        dmix_ref[...] = dmix.astype(BF16)
        _acc_add(dgm_ref, dgm)
        _acc_add(dgp_ref, dgp)

    return pl.pallas_call(
        body,
        out_shape=[jax.ShapeDtypeStruct((lp, D_MODEL), F32), jax.ShapeDtypeStruct((lp, D_MODEL), BF16),
                   jax.ShapeDtypeStruct((1, D_MODEL), F32), jax.ShapeDtypeStruct((1, D_MODEL), F32)],
        grid=(lp // t,),
        in_specs=[_rb(t, D_MODEL), _rb(t, D_MODEL), _rb(t, D_MODEL), _full((1, D_MODEL)), _rb(t, D_MODEL),
                  _full((1, D_MODEL))],
        out_specs=[_rb(t, D_MODEL), _rb(t, D_MODEL), _full((1, D_MODEL)), _full((1, D_MODEL))],
        name="mlp_residual_bwd", compiler_params=_params(("arbitrary",)))(dh2, dn2, h1, g_mlp_pre, mix, g_post)


def _input_norm_bwd(dh1, dn1, h0, g_pre):
    lp = h0.shape[0]
    t = ROW_BLOCK

    def body(dh1_ref, dn1_ref, h0_ref, g_ref, dh0_ref, dg_ref):
        dx, dg = _rms_bwd(h0_ref[...], g_ref[...], dn1_ref[...])
        dh0_ref[...] = dh1_ref[...] + dx
        _acc_add(dg_ref, dg)

    return pl.pallas_call(
        body, out_shape=[jax.ShapeDtypeStruct((lp, D_MODEL), F32), jax.ShapeDtypeStruct((1, D_MODEL), F32)],
        grid=(lp // t,),
        in_specs=[_rb(t, D_MODEL), _rb(t, D_MODEL), _rb(t, D_MODEL), _full((1, D_MODEL))],
        out_specs=[_rb(t, D_MODEL), _full((1, D_MODEL))],
        name="input_norm_bwd", compiler_params=_params(("arbitrary",)))(dh1, dn1, h0, g_pre)


def _conv_bwd(name, dact, pre, proj, seg, width, conv_w):
    lp = proj.shape[0]
    t = ROW_BLOCK
    cb = seg // width
    nblk = lp // t

    def dsilu(p):
        s = _sigmoid(p)
        return s * (1.0 + p * (1.0 - s))

    def body(da_ref, dan_ref, pre_ref, pren_ref, u_ref, halo_ref, w_ref, du_ref, dw_ref, db_ref):
        i = pl.program_id(0)
        dpre = da_ref[...] * dsilu(pre_ref[...])
        dpre_next = jnp.where(i < nblk - 1, dan_ref[...] * dsilu(pren_ref[...]), 0.0)
        extd = jnp.concatenate([dpre, dpre_next], axis=0)
        halo = jnp.where(i > 0, halo_ref[...], 0.0)
        ext = jnp.concatenate([halo, u_ref[...]], axis=0)
        du = jnp.zeros((t, width), F32)
        dws = []
        for k in range(CONV_K):
            m = CONV_K - 1 - k
            ahead = dpre if m == 0 else pltpu.roll(extd, t + 8 - m, 0)[:t, :]
            du = du + w_ref[k:k + 1, :] * ahead
            dws.append(jnp.sum(dpre * _shifted(ext, t, m), axis=0, keepdims=True))
        du_ref[...] = du
        _acc_add(dw_ref, jnp.concatenate(dws + [jnp.zeros((8 - CONV_K, width), F32)], axis=0))
        _acc_add(db_ref, jnp.sum(dpre, axis=0, keepdims=True))

    nxt = lambda i: (jnp.minimum((i + 1) * (t // 8), lp // 8 - 1), 0)
    return pl.pallas_call(
        body,
        out_shape=[jax.ShapeDtypeStruct((lp, width), F32), jax.ShapeDtypeStruct((8, width), F32),
                   jax.ShapeDtypeStruct((1, width), F32)],
        grid=(nblk,),
        in_specs=[_rb(t, width), pl.BlockSpec((8, width), nxt), _rb(t, width), pl.BlockSpec((8, width), nxt),
                  _rb(t, width, cb),
                  pl.BlockSpec((8, width), lambda i: (jnp.maximum(i * (t // 8) - 1, 0), cb)),
                  _full((CONV_K, width))],
        out_specs=[_rb(t, width), _full((8, width)), _full((1, width))],
        name=name, compiler_params=_params(("arbitrary",)))(dact, dact, pre, pre, proj, proj, conv_w)


def _qk_unpack_bwd(dqs, dks, dvs, cos_t, sin_t):
    lp = dqs.shape[1]
    t = ROW_BLOCK

    def body(dqs_ref, dks_ref, dvs_ref, cos_ref, sin_ref, dq_ref, dkv_ref, dkr_ref):
        lane = lax.broadcasted_iota(jnp.int32, (t, 128), 1)
        lo = lane < 64
        for j in range(ATT_HEADS // 2):
            dpr = jnp.where(lo, dqs_ref[2 * j, :, 128:256], dqs_ref[2 * j + 1, :, 128:256])
            dq_ref[:, 1024 + 128 * j:1024 + 128 * (j + 1)] = _rope_bwd(dpr, cos_ref[...], sin_ref[...]).astype(BF16)
        dkrr = jnp.zeros((t, 128), F32)
        for h in range(ATT_HEADS):
            dq_ref[:, 128 * h:128 * (h + 1)] = dqs_ref[h, :, 0:128].astype(BF16)
            dkv_ref[:, 256 * h:256 * h + 128] = dks_ref[h, :, 0:128].astype(BF16)
            dkv_ref[:, 256 * h + 128:256 * (h + 1)] = dvs_ref[h].astype(BF16)
            dkrr = dkrr + dks_ref[h, :, 128:256]
        droped = jnp.where(lo, dkrr + pltpu.roll(dkrr, 64, 1), 0.0)
        dkr_ref[...] = _rope_bwd(droped, cos_ref[...], sin_ref[...])

    slab = lambda w: pl.BlockSpec((ATT_HEADS, t, w), lambda i: (0, i, 0))
    return pl.pallas_call(
        body,
        out_shape=[jax.ShapeDtypeStruct((lp, 1536), BF16), jax.ShapeDtypeStruct((lp, 2048), BF16),
                   jax.ShapeDtypeStruct((lp, 128), F32)],
        grid=(lp // t,),
        in_specs=[slab(QP_W), slab(QP_W), slab(V_HEAD), _rb(t, 128), _rb(t, 128)],
        out_specs=[_rb(t, 1536), _rb(t, 2048), _rb(t, 128)],
        name="qk_unpack_bwd", compiler_params=_params(("parallel",)))(dqs, dks, dvs, cos_t, sin_t)


def _proj_grad(proj, dcqn, dckvn, g_q, g_kv, dkr, ddt_pad, dt_bias_pad, dz, dxs, dbc):
    lp = proj.shape[0]
    t = ROW_BLOCK

    def body(ckv_ref, cq_ref, pdt_ref, dcq_ref, dckv_ref, gq_ref, gkv_ref, dkr_ref, ddt_ref, b_ref, dz_ref, dxs_ref,
             dbc_ref, dp_ref, dgq_ref, dgkv_ref, db_ref):
        dckv, dgkv = _rms_bwd(ckv_ref[...], gkv_ref[...], dckv_ref[...])
        dcq, dgq = _rms_bwd(cq_ref[...], gq_ref[...], dcq_ref[...])
        ddt_raw = ddt_ref[...] * _sigmoid(pdt_ref[...] + b_ref[...])
        dp_ref[:, SEG_KV:SEG_KV + KV_LORA] = dckv.astype(BF16)
        dp_ref[:, SEG_KR:SEG_KR + 128] = dkr_ref[...].astype(BF16)
        dp_ref[:, SEG_CQ:SEG_CQ + Q_LORA] = dcq.astype(BF16)
        dp_ref[:, SEG_DT:SEG_DT + 128] = ddt_raw.astype(BF16)
        dp_ref[:, SEG_DT + 128:SEG_Z] = jnp.zeros((t, SEG_Z - SEG_DT - 128), BF16)
        dp_ref[:, SEG_Z:SEG_XS] = dz_ref[...].astype(BF16)
        dp_ref[:, SEG_XS:SEG_BC] = dxs_ref[...].astype(BF16)
        dp_ref[:, SEG_BC:PROJ_W] = dbc_ref[...].astype(BF16)
        _acc_add(dgq_ref, dgq)
        _acc_add(dgkv_ref, dgkv)
        _acc_add(db_ref, jnp.sum(ddt_raw, axis=0, keepdims=True))

    return pl.pallas_call(
        body,
        out_shape=[jax.ShapeDtypeStruct((lp, PROJ_W), BF16), jax.ShapeDtypeStruct((1, Q_LORA), F32),
                   jax.ShapeDtypeStruct((1, KV_LORA), F32), jax.ShapeDtypeStruct((1, 128), F32)],
        grid=(lp // t,),
        in_specs=[_rb(t, KV_LORA, SEG_KV // KV_LORA), _rb(t, Q_LORA, SEG_CQ // Q_LORA), _rb(t, 128, SEG_DT // 128),
                  _rb(t, Q_LORA), _rb(t, KV_LORA), _full((1, Q_LORA)), _full((1, KV_LORA)), _rb(t, 128), _rb(t, 128),
                  _full((1, 128)), _rb(t, SSM_WIDTH), _rb(t, SSM_WIDTH), _rb(t, 512)],
        out_specs=[_rb(t, PROJ_W), _full((1, Q_LORA)), _full((1, KV_LORA)), _full((1, 128))],
        name="proj_grad", compiler_params=_params(("arbitrary",)))(
            proj, proj, proj, dcqn, dckvn, g_q, g_kv, dkr, ddt_pad, dt_bias_pad, dz, dxs, dbc)


def _pair_tables(n):
    qmaj = [(i, j) for i in range(n) for j in range(i + 1)]
    kmaj = [(i, j) for j in range(n) for i in range(j, n)]
    to = lambda ps, c: jnp.asarray(np.array([p[c] for p in ps], np.int32))
    return (to(qmaj, 0), to(qmaj, 1)), (to(kmaj, 0), to(kmaj, 1))


def _att_block(lp, edge=ATT_BLOCK):
    return edge if lp % edge == 0 else MM_BLOCK


def _nt(a, b):
    return lax.dot_general(a, b, (((1,), (1,)), ((), ())), preferred_element_type=F32)


def _attn_fwd(qs, ks, vts):
    lp = qs.shape[1]
    t = _att_block(lp, ATT_BLOCK_FWD)
    n = lp // t
    (qi, kj), _ = _pair_tables(n)
    tc = t // ATT_SPLIT

    def body(qi_ref, kj_ref, q_ref, k_ref, vt_ref, o_ref, o16_ref, lse_ref, m_s, l_s, acc_s):
        p = pl.program_id(1)
        i, j = qi_ref[p], kj_ref[p]

        @pl.when(j == 0)
        def _():
            m_s[...] = jnp.full_like(m_s, NEG_BIG)
            l_s[...] = jnp.zeros_like(l_s)
            acc_s[...] = jnp.zeros_like(acc_s)

        def update(masked):
            m_all, l_all, acc_all = m_s[...], l_s[...], acc_s[...]
            m_out, l_out, acc_out = [], [], []
            qk = lambda c: _nt(k_ref[0], q_ref[0, c * tc:(c + 1) * tc, :])
            ahead = qk(0)
            for c in range(ATT_SPLIT):
                cols = slice(c * tc, (c + 1) * tc)
                sc = ahead * ATT_SCALE_LOG2
                if c + 1 < ATT_SPLIT:
                    ahead = qk(c + 1)
                if masked:
                    keep = (lax.broadcasted_iota(jnp.int32, (t, tc), 1) + c * tc
                            >= lax.broadcasted_iota(jnp.int32, (t, tc), 0))
                    sc = jnp.where(keep, sc, NEG_BIG)
                m_prev = m_all[:, cols]
                m_new = jnp.maximum(m_prev, jnp.max(sc, axis=0, keepdims=True))
                alpha = jnp.exp2(m_prev - m_new)
                pexp = jnp.exp2(sc - m_new)
                l_out.append(alpha * l_all[:, cols] + jnp.sum(pexp, axis=0, keepdims=True))
                acc_out.append(alpha * acc_all[:, cols] + jnp.dot(vt_ref[0], pexp.astype(BF16),
                                                                  preferred_element_type=F32))
                m_out.append(m_new)
            cat = lambda parts: parts[0] if len(parts) == 1 else jnp.concatenate(parts, axis=1)
            m_s[...], l_s[...], acc_s[...] = cat(m_out), cat(l_out), cat(acc_out)

        @pl.when(j < i)
        def _():
            update(False)

        @pl.when(j == i)
        def _():
            update(True)
            out = (acc_s[...] / l_s[...]).T
            o_ref[...] = out
            o16_ref[...] = out.astype(BF16)
            lse_ref[0] = m_s[...] + jnp.log2(l_s[...])

    grid_spec = pltpu.PrefetchScalarGridSpec(
        num_scalar_prefetch=2, grid=(ATT_HEADS, int(qi.shape[0])),
        in_specs=[pl.BlockSpec((1, t, QP_W), lambda h, p, qi, kj: (h, qi[p], 0)),
                  pl.BlockSpec((1, t, QP_W), lambda h, p, qi, kj: (h, kj[p], 0)),
                  pl.BlockSpec((1, V_HEAD, t), lambda h, p, qi, kj: (h, 0, kj[p]))],
        out_specs=[pl.BlockSpec((t, V_HEAD), lambda h, p, qi, kj: (qi[p], h)),
                   pl.BlockSpec((t, V_HEAD), lambda h, p, qi, kj: (qi[p], h)),
                   pl.BlockSpec((1, 1, t), lambda h, p, qi, kj: (h, 0, qi[p]))],
        scratch_shapes=[pltpu.VMEM((1, t), F32), pltpu.VMEM((1, t), F32), pltpu.VMEM((V_HEAD, t), F32)])
    return pl.pallas_call(
        body, grid_spec=grid_spec,
        out_shape=[jax.ShapeDtypeStruct((lp, ATT_HEADS * V_HEAD), F32),
                   jax.ShapeDtypeStruct((lp, ATT_HEADS * V_HEAD), BF16), jax.ShapeDtypeStruct((ATT_HEADS, 1, lp), F32)],
        name="attn_fwd", compiler_params=_params(("parallel", "arbitrary")))(qi, kj, qs, ks, vts)


def _attn_delta(datt, att):
    lp = att.shape[0]
    t = MM_BLOCK
    w = ATT_HEADS * V_HEAD

    def body(do_ref, o_ref, d_ref):
        ones = jnp.ones((8, V_HEAD), BF16)
        for h in range(ATT_HEADS):
            cols = slice(h * V_HEAD, (h + 1) * V_HEAD)
            prod = do_ref[:, cols] * o_ref[:, cols]
            hi = prod.astype(BF16)
            lo = (prod - hi.astype(F32)).astype(BF16)
            d_ref[h] = (_nt(ones, hi) + _nt(ones, lo))[0:1, :]

    return pl.pallas_call(
        body, out_shape=jax.ShapeDtypeStruct((ATT_HEADS, 1, lp), F32), grid=(lp // t,),
        in_specs=[_rb(t, w), _rb(t, w)], out_specs=pl.BlockSpec((ATT_HEADS, 1, t), lambda i: (0, 0, i)),
        name="attn_delta", compiler_params=_params(("parallel",)))(datt, att)


def _attn_bwd(qs, ks, vs, datt16, lse2, delta):
    lp = qs.shape[1]
    tk = _att_block(lp)
    tq = ATT_BLOCK_Q_BWD if lp % ATT_BLOCK_Q_BWD == 0 and ATT_BLOCK_Q_BWD % tk == 0 else tk
    r = tq // tk
    nk, nq = lp // tk, lp // tq
    pairs = [(i, j) for j in range(nk) for i in range(j // r, nq)]
    qi = jnp.asarray(np.array([p[0] for p in pairs], np.int32))
    kj = jnp.asarray(np.array([p[1] for p in pairs], np.int32))
    n_pairs = len(pairs)

    def body(qi_ref, kj_ref, k_ref, v_ref, q_ref, do_ref, lse_ref, dl_ref, dq_hbm, dk_ref, dv_ref, dq_s, dk_s, dv_s,
             sem):
        h, p = pl.program_id(0), pl.program_id(1)
        i, j = qi_ref[p], kj_ref[p]
        first = i == j // r

        @pl.when(p == 0)
        def _():
            dq_s[...] = jnp.zeros_like(dq_s)

        @pl.when(first)
        def _():
            dk_s[...] = jnp.zeros_like(dk_s)
            dv_s[...] = jnp.zeros_like(dv_s)

        def step(masked, skip=0):
            q0 = skip * tk
            nrows = tq - q0
            q = q_ref[0, q0:, :]
            do = do_ref[q0:, :]
            pt = jnp.exp2(_nt(k_ref[0], q) * ATT_SCALE_LOG2 - lse_ref[0, :, q0:])
            if masked:
                keep = (lax.broadcasted_iota(jnp.int32, (tk, nrows), 1)
                        >= lax.broadcasted_iota(jnp.int32, (tk, nrows), 0))
                pt = jnp.where(keep, pt, 0.0)
            dst = (pt * (_nt(v_ref[0], do) - dl_ref[0, :, q0:]) * ATT_SCALE).astype(BF16)
            dv_s[...] += jnp.dot(pt.astype(BF16), do, preferred_element_type=F32)
            dk_s[...] += jnp.dot(dst, q, preferred_element_type=F32)
            rows = pl.ds(pl.multiple_of(i * tq + q0, tk), nrows)
            dq_s[rows, :] += lax.dot_general(dst, k_ref[0], (((0,), (0,)), ((), ())), preferred_element_type=F32)

        @pl.when(jnp.logical_not(first))
        def _():
            step(False)

        for sub in range(r):
            @pl.when(jnp.logical_and(first, j % r == sub))
            def _(sub=sub):
                step(True, sub)

        @pl.when(i == nq - 1)
        def _():
            dk_ref[0] = dk_s[...]
            dv_ref[0] = dv_s[...]

        @pl.when(p == n_pairs - 1)
        def _():
            out = pltpu.make_async_copy(dq_s, dq_hbm.at[h], sem)
            out.start()
            out.wait()

    grid_spec = pltpu.PrefetchScalarGridSpec(
        num_scalar_prefetch=2, grid=(ATT_HEADS, n_pairs),
        in_specs=[pl.BlockSpec((1, tk, QP_W), lambda h, p, qi, kj: (h, kj[p], 0)),
                  pl.BlockSpec((1, tk, V_HEAD), lambda h, p, qi, kj: (h, kj[p], 0)),
                  pl.BlockSpec((1, tq, QP_W), lambda h, p, qi, kj: (h, qi[p], 0)),
                  pl.BlockSpec((tq, V_HEAD), lambda h, p, qi, kj: (qi[p], h)),
                  pl.BlockSpec((1, 1, tq), lambda h, p, qi, kj: (h, 0, qi[p])),
                  pl.BlockSpec((1, 1, tq), lambda h, p, qi, kj: (h, 0, qi[p]))],
        out_specs=[pl.BlockSpec(memory_space=pl.ANY),
                   pl.BlockSpec((1, tk, QP_W), lambda h, p, qi, kj: (h, kj[p], 0)),
                   pl.BlockSpec((1, tk, V_HEAD), lambda h, p, qi, kj: (h, kj[p], 0))],
        scratch_shapes=[pltpu.VMEM((lp, QP_W), F32), pltpu.VMEM((tk, QP_W), F32), pltpu.VMEM((tk, V_HEAD), F32),
                        pltpu.SemaphoreType.DMA])
    return pl.pallas_call(
        body, grid_spec=grid_spec,
        out_shape=[jax.ShapeDtypeStruct((ATT_HEADS, lp, QP_W), F32), jax.ShapeDtypeStruct((ATT_HEADS, lp, QP_W), F32),
                   jax.ShapeDtypeStruct((ATT_HEADS, lp, V_HEAD), F32)],
        name="attn_bwd", compiler_params=_params(("arbitrary", "arbitrary")))(
            qi, kj, ks, vs, qs, datt16, lse2, delta)


N_PAIRS = SSM_HEADS // 2
HI = lax.Precision.HIGHEST


def _ssd_chunk(xp, bs, cs, dt, dt_t, alr, alc, dsk, st):
    q = dt.shape[0]
    li = lax.broadcasted_iota(jnp.int32, (q, q), 0)
    si = lax.broadcasted_iota(jnp.int32, (q, q), 1)
    tri = (si <= li).astype(F32)
    tri_t = (li <= si).astype(F32)
    lo = lax.broadcasted_iota(jnp.int32, (1, 128), 1) < 64
    h_r = lax.broadcasted_iota(jnp.int32, (1, SSM_HEADS), 1)
    h_c = lax.broadcasted_iota(jnp.int32, (SSM_HEADS, 1), 0)
    a = dt * (-jnp.exp(alr))
    a_t = dt_t * (-jnp.exp(alc))
    acum = jnp.dot(tri, a, precision=HI, preferred_element_type=F32)
    acum_t = jnp.dot(a_t, tri_t, precision=HI, preferred_element_type=F32)
    last = (lax.broadcasted_iota(jnp.int32, (q, 1), 0) == q - 1).astype(F32)
    alast = jnp.sum(acum * last, axis=0, keepdims=True)
    e = jnp.exp(acum)
    rdt = jnp.exp(alast - acum) * dt
    e_last = jnp.exp(alast)

    def col(m, h):
        return jnp.sum(m * (h_r == h).astype(F32), axis=1, keepdims=True)

    def row(m, h):
        return jnp.sum(m * (h_c == h).astype(F32), axis=0, keepdims=True)

    def pair(m, ha):
        return jnp.where(lo, col(m, ha), col(m, ha + 1))

    ys, st_new = [], []
    for g in range(2):
        c_b = cs[g].astype(BF16)
        b_b = bs[g].astype(BF16)
        cb = _nt(c_b, b_b)
        for j in range(N_PAIRS // 2):
            p = (N_PAIRS // 2) * g + j
            ha = 2 * p
            x = xp[p]
            x_b = x.astype(BF16)

            def w_of(h):
                seg = col(acum, h) - row(acum_t, h)
                return (cb * jnp.exp(jnp.minimum(seg, 0.0)) * tri * row(dt_t, h)).astype(BF16)

            y_diag = jnp.where(lo, jnp.dot(w_of(ha), x_b, preferred_element_type=F32),
                               jnp.dot(w_of(ha + 1), x_b, preferred_element_type=F32))
            y_off = jnp.dot(c_b, st[p].astype(BF16), preferred_element_type=F32) * pair(e, ha)
            ys.append(y_diag + y_off + pair(dsk, ha) * x)
            xw = (x * pair(rdt, ha)).astype(BF16)
            st_new.append(st[p] * pair(e_last, ha)
                          + lax.dot_general(b_b, xw, (((0,), (0,)), ((), ())), preferred_element_type=F32))
    return ys, st_new


def _ssd_fwd(xs, bc, dt, dt_t, alr, alc, dsk):
    lp = xs.shape[0]
    q = SSD_CHUNK
    nc = lp // q

    def body(x_ref, b_ref, c_ref, dt_ref, dtt_ref, alr_ref, alc_ref, dsk_ref, y_ref, sp_ref, st_s):
        @pl.when(pl.program_id(0) == 0)
        def _():
            st_s[...] = jnp.zeros_like(st_s)

        sp_ref[0] = st_s[...]
        xp = [x_ref[:, 128 * p:128 * (p + 1)] for p in range(N_PAIRS)]
        bs = [b_ref[:, 0:128], b_ref[:, 128:256]]
        cs = [c_ref[:, 0:128], c_ref[:, 128:256]]
        ys, st_new = _ssd_chunk(xp, bs, cs, dt_ref[...], dtt_ref[...], alr_ref[...], alc_ref[...], dsk_ref[...],
                                [st_s[p] for p in range(N_PAIRS)])
        for p in range(N_PAIRS):
            y_ref[:, 128 * p:128 * (p + 1)] = ys[p]
            st_s[p] = st_new[p]

    return pl.pallas_call(
        body,
        out_shape=[jax.ShapeDtypeStruct((lp, SSM_WIDTH), F32), jax.ShapeDtypeStruct((nc, N_PAIRS, 128, 128), F32)],
        grid=(nc,),
        in_specs=[_rb(q, SSM_WIDTH), _rb(q, 256, 0), _rb(q, 256, 1), _rb(q, SSM_HEADS),
                  pl.BlockSpec((SSM_HEADS, q), lambda i: (0, i)),
                  _full((1, SSM_HEADS)), _full((SSM_HEADS, 1)), _full((1, SSM_HEADS))],
        out_specs=[_rb(q, SSM_WIDTH), pl.BlockSpec((1, N_PAIRS, 128, 128), lambda i: (i, 0, 0, 0))],
        scratch_shapes=[pltpu.VMEM((N_PAIRS, 128, 128), F32)],
        name="ssd_fwd", compiler_params=_params(("arbitrary",)))(xs, bc, bc, dt, dt_t, alr, alc, dsk)


def _ssd_bwd(xs, bc, dt, dt_t, alr, alc, dsk, sprev, dy):
    lp = xs.shape[0]
    q = SSD_CHUNK
    nc = lp // q

    def body(x_ref, b_ref, c_ref, dt_ref, dtt_ref, alr_ref, alc_ref, dsk_ref, sp_ref, dy_ref,
             dx_ref, dbc_ref, ddt_ref, ddtt_ref, dalr_ref, dalc_ref, ddsk_ref, ds_s):
        @pl.when(pl.program_id(0) == 0)
        def _():
            ds_s[...] = jnp.zeros_like(ds_s)

        xp = [x_ref[:, 128 * p:128 * (p + 1)] for p in range(N_PAIRS)]
        bs = [b_ref[:, 0:128], b_ref[:, 128:256]]
        cs = [c_ref[:, 0:128], c_ref[:, 128:256]]
        st = [sp_ref[0, p] for p in range(N_PAIRS)]
        _, vjp = jax.vjp(_ssd_chunk, xp, bs, cs, dt_ref[...], dtt_ref[...], alr_ref[...], alc_ref[...], dsk_ref[...],
                         st)
        dys = [dy_ref[:, 128 * p:128 * (p + 1)] for p in range(N_PAIRS)]
        dxp, dbs, dcs, ddt, ddtt, dalr, dalc, ddsk, dst = vjp((dys, [ds_s[p] for p in range(N_PAIRS)]))
        for p in range(N_PAIRS):
            dx_ref[:, 128 * p:128 * (p + 1)] = dxp[p]
            ds_s[p] = dst[p]
        for g in range(2):
            dbc_ref[:, 128 * g:128 * (g + 1)] = dbs[g]
            dbc_ref[:, 256 + 128 * g:256 + 128 * (g + 1)] = dcs[g]
        ddt_ref[...] = ddt
        ddtt_ref[...] = ddtt
        _acc_add(dalr_ref, dalr)
        _acc_add(dalc_ref, dalc)
        _acc_add(ddsk_ref, ddsk)

    rev = lambda width, cb=0: pl.BlockSpec((q, width), lambda i: (nc - 1 - i, cb))
    return pl.pallas_call(
        body,
        out_shape=[jax.ShapeDtypeStruct((lp, SSM_WIDTH), F32), jax.ShapeDtypeStruct((lp, 512), F32),
                   jax.ShapeDtypeStruct((lp, SSM_HEADS), F32), jax.ShapeDtypeStruct((SSM_HEADS, lp), F32),
                   jax.ShapeDtypeStruct((1, SSM_HEADS), F32), jax.ShapeDtypeStruct((SSM_HEADS, 1), F32),
                   jax.ShapeDtypeStruct((1, SSM_HEADS), F32)],
        grid=(nc,),
        in_specs=[rev(SSM_WIDTH), rev(256, 0), rev(256, 1), rev(SSM_HEADS),
                  pl.BlockSpec((SSM_HEADS, q), lambda i: (0, nc - 1 - i)),
                  _full((1, SSM_HEADS)), _full((SSM_HEADS, 1)), _full((1, SSM_HEADS)),
                  pl.BlockSpec((1, N_PAIRS, 128, 128), lambda i: (nc - 1 - i, 0, 0, 0)), rev(SSM_WIDTH)],
        out_specs=[rev(SSM_WIDTH), rev(512), rev(SSM_HEADS), pl.BlockSpec((SSM_HEADS, q), lambda i: (0, nc - 1 - i)),
                   _full((1, SSM_HEADS)), _full((SSM_HEADS, 1)), _full((1, SSM_HEADS))],
        scratch_shapes=[pltpu.VMEM((N_PAIRS, 128, 128), F32)],
        name="ssd_bwd", compiler_params=_params(("arbitrary",)))(xs, bc, bc, dt, dt_t, alr, alc, dsk, sprev, dy)


def _q_to_slab_order(w):
    hd = QK_NOPE + QK_ROPE
    nope = [w[:, h * hd:h * hd + QK_NOPE] for h in range(ATT_HEADS)]
    rope = [w[:, h * hd + QK_NOPE:(h + 1) * hd] for h in range(ATT_HEADS)]
    return jnp.concatenate(nope + rope, axis=1)


def _q_from_slab_order(wp):
    base = ATT_HEADS * QK_NOPE
    parts = []
    for h in range(ATT_HEADS):
        parts += [wp[:, QK_NOPE * h:QK_NOPE * (h + 1)], wp[:, base + QK_ROPE * h:base + QK_ROPE * (h + 1)]]
    return jnp.concatenate(parts, axis=1)


_IN_CQ, _IN_CKV, _IN_KR, _IN_Z, _IN_XS, _IN_BC, _IN_DT = (0, 384), (384, 640), (640, 704), (704, 1728), (1728, 2752), \
    (2752, 3264), (3264, 3280)


def _pack_w_in(w):
    z = lambda n: jnp.zeros((w.shape[0], n), w.dtype)
    s = lambda r: w[:, r[0]:r[1]]
    return jnp.concatenate([s(_IN_CKV), s(_IN_KR), z(64), s(_IN_CQ), s(_IN_DT), z(112), z(128), s(_IN_Z), s(_IN_XS),
                            s(_IN_BC)], axis=1)


def _unpack_w_in(wp):
    s = lambda off, n: wp[:, off:off + n]
    return jnp.concatenate([s(SEG_CQ, 384), s(SEG_KV, 256), s(SEG_KR, 64), s(SEG_Z, 1024), s(SEG_XS, 1024),
                            s(SEG_BC, 512), s(SEG_DT, 16)], axis=1)


def _rope_tables(lp):
    inv_freq = ROPE_THETA ** (-jnp.arange(0, QK_ROPE, 2, dtype=F32) / QK_ROPE)
    ang = jnp.arange(lp, dtype=F32)[:, None] * inv_freq[None, :]
    cos, sin = jnp.cos(ang), jnp.sin(ang)
    return jnp.tile(cos, (1, 4)), jnp.concatenate([-sin, sin, -sin, sin], axis=1)


def _local_step(x, tgt, w):
    n_real = x.shape[0]
    l = N_META + n_real
    lp = -(-l // MM_BLOCK) * MM_BLOCK
    h0 = lax.optimization_barrier(jnp.concatenate([w["meta_tokens"], x, jnp.zeros((lp - l, D_MODEL), F32)], axis=0))
    cos_t, sin_t = _rope_tables(lp)

    w_in_p = _pack_w_in(w["w_in"])
    w_q_p = _q_to_slab_order(w["w_q_up"])
    w_kv = w["w_kv_up"]
    if "token" in w:
        cos_t = cos_t + w["token"][0, 0]
    conv_w, conv_b = w["conv_w"], w["conv_b"]
    dt_bias_pad = jnp.concatenate([w["dt_bias"], jnp.zeros((1, 128 - SSM_HEADS), F32)], axis=1)
    alr, dsk = w["a_log"], w["d_skip"]
    alc = alr.reshape(SSM_HEADS, 1)

    n1 = _norm_in(h0, w["norm_mix_pre"])
    proj = _mm("proj", n1, w_in_p)
    cqn, ckvn, krr = _attn_prep(proj, w["q_a_norm"], w["kv_a_norm"], cos_t, sin_t)
    q = _mm("q_up", cqn, w_q_p)
    kv = _mm("kv_up", ckvn, w_kv, outs=((BF16, None),))
    qs, ks, vs, vts = _qk_pack(q, kv, krr, cos_t, sin_t)
    att, att16, lse2 = _attn_fwd(qs, ks, vts)
    xs_pre, xs_act = _conv_fwd("conv_xs_fwd", proj, SEG_XS, SSM_WIDTH, conv_w[:, :SSM_WIDTH], conv_b[:, :SSM_WIDTH])
    bc_pre, bc_act = _conv_fwd("conv_bc_fwd", proj, SEG_BC, 512, conv_w[:, SSM_WIDTH:], conv_b[:, SSM_WIDTH:])
    dt = _dt_fwd(proj, dt_bias_pad)[:, :SSM_HEADS]
    dt_t = dt.T
    y, sprev = _ssd_fwd(xs_act, bc_act, dt, dt_t, alr, alc, dsk)
    ssm = _gated_norm_fwd(y, proj, w["ssm_norm"])
    late = w["late_weights"](ssm) if "late_weights" in w else w
    w_out, w_up, w_down = late["w_out"], late["w_mlp_up"], late["w_mlp_down"]
    mix = _mm("out_proj", (att16, ssm), w_out)
    h1, n2 = _mix_residual(h0, mix, w["norm_mix_post"], w["norm_mlp_pre"])
    relu2 = lambda r: jnp.square(jnp.maximum(r, 0.0))
    act = _mm("mlp_up", n2, w_up, outs=((BF16, relu2),))
    f = _mm("mlp_down", act, w_down)
    loss, dh2, df, dg_mlp_post = _loss_and_grad(h1, f, w["norm_mlp_post"], tgt, n_real)

    g = {"norm_mlp_post": dg_mlp_post}
    g["w_mlp_down"] = _mm_tn("d_w_mlp_down", act, df)
    du = _mm("d_mlp_act", df, w_down, outs=((BF16, lambda r, ab: r * (2.0 * jnp.sqrt(ab.astype(F32)))),),
             epi_ins=(act,), trans_b=True)
    g["w_mlp_up"] = _mm_tn("d_w_mlp_up", n2, du)
    dn2 = _mm("d_n2", du, w_up, trans_b=True)
    dh1, dmix, g["norm_mlp_pre"], g["norm_mix_post"] = _mlp_residual_bwd(dh2, dn2, h1, w["norm_mlp_pre"], mix,
                                                                         w["norm_mix_post"])
    g["w_out"] = jnp.concatenate([_mm_tn("d_w_out_att", att16, dmix), _mm_tn("d_w_out_ssm", ssm, dmix)], axis=0)
    dcat, dcat16 = _mm("d_cat", dmix, w_out, outs=((F32, None), (BF16, None)), trans_b=True)
    dy, dz, g["ssm_norm"] = _gated_norm_bwd(y, proj, w["ssm_norm"], dcat)
    dxs_act, dbc_act, ddt, ddt_t, dalr, dalc, g["d_skip"] = _ssd_bwd(xs_act, bc_act, dt, dt_t, alr, alc, dsk, sprev, dy)
    g["a_log"] = dalr + dalc.reshape(1, SSM_HEADS)
    dxs, dcw_xs, dcb_xs = _conv_bwd("conv_xs_bwd", dxs_act, xs_pre, proj, SEG_XS, SSM_WIDTH, conv_w[:, :SSM_WIDTH])
    dbc, dcw_bc, dcb_bc = _conv_bwd("conv_bc_bwd", dbc_act, bc_pre, proj, SEG_BC, 512, conv_w[:, SSM_WIDTH:])
    g["conv_w"] = jnp.concatenate([dcw_xs[:CONV_K], dcw_bc[:CONV_K]], axis=1)
    g["conv_b"] = jnp.concatenate([dcb_xs, dcb_bc], axis=1)
    ddt_pad = jnp.concatenate([ddt + ddt_t.T, jnp.zeros((lp, 128 - SSM_HEADS), F32)], axis=1)

    dqs, dks, dvs = _attn_bwd(qs, ks, vs, dcat16, lse2, _attn_delta(dcat, att))
    dq, dkv, dkr = _qk_unpack_bwd(dqs, dks, dvs, cos_t, sin_t)
    g["w_q_up"] = _q_from_slab_order(_mm_tn("d_w_q_up", cqn, dq))
    g["w_kv_up"] = _mm_tn("d_w_kv_up", ckvn, dkv)
    dcqn = _mm("d_cqn", dq, w_q_p, trans_b=True)
    dckvn = _mm("d_ckvn", dkv, w_kv, trans_b=True)
    dproj, g["q_a_norm"], g["kv_a_norm"], ddtb = _proj_grad(proj, dcqn, dckvn, w["q_a_norm"], w["kv_a_norm"], dkr,
                                                          ddt_pad, dt_bias_pad, dz, dxs, dbc)
    g["dt_bias"] = ddtb[:, :SSM_HEADS]
    g["w_in"] = _unpack_w_in(_mm_tn("d_w_in", n1, dproj))
    dn1 = _mm("d_n1", dproj, w_in_p, trans_b=True)
    dh0, g["norm_mix_pre"] = _input_norm_bwd(dh1, dn1, h0, w["norm_mix_pre"])
    g["meta_tokens"] = dh0[:N_META]
    return loss, dh0, g


WEIGHTS = ["meta_tokens", "norm_mix_pre", "w_in", "q_a_norm", "w_q_up", "kv_a_norm", "w_kv_up", "conv_w", "conv_b",
           "dt_bias", "a_log", "d_skip", "ssm_norm", "w_out", "norm_mix_post", "norm_mlp_pre", "w_mlp_up",
           "w_mlp_down", "norm_mlp_post"]
SHARD_AXIS = {"meta_tokens": 1, "w_in": 1, "w_q_up": 1, "w_kv_up": 1, "conv_w": 1, "w_out": 0, "w_mlp_up": 1,
              "w_mlp_down": 0}
FULL_SHAPE = {"meta_tokens": (16, 1024), "norm_mix_pre": (1, 1024), "w_in": (1024, 3280), "q_a_norm": (1, 384),
              "w_q_up": (384, 1536), "kv_a_norm": (1, 256), "w_kv_up": (256, 2048), "conv_w": (4, 1536),
              "conv_b": (1, 1536), "dt_bias": (1, 16), "a_log": (1, 16), "d_skip": (1, 16), "ssm_norm": (1, 1024),
              "w_out": (2048, 1024), "norm_mix_post": (1, 1024), "norm_mlp_pre": (1, 1024), "w_mlp_up": (1024, 4096),
              "w_mlp_down": (4096, 1024), "norm_mlp_post": (1, 1024)}
GATHER_BF16 = ["w_in", "w_q_up", "w_kv_up", "w_out", "w_mlp_up", "w_mlp_down"]
GATHER_F32 = ["meta_tokens", "conv_w"]
GATHER_EARLY = ["w_in", "w_q_up", "w_kv_up"]
GATHER_LATE = ["w_out", "w_mlp_up", "w_mlp_down"]
ADAM_NATURAL = GATHER_BF16


def _shard_shape(name):
    shp = list(FULL_SHAPE[name])
    if name in SHARD_AXIS:
        shp[SHARD_AXIS[name]] //= N_CHIPS
    return tuple(shp)


PACK_ORDER = sorted(WEIGHTS, key=lambda n: -_shard_shape(n)[0])


PART_ROWS = 16


def _data_rows(shape):
    r, c = shape
    return r if c <= PACK_W else -(-c // PACK_W)


def _packed_rows(shape):
    return -(-_data_rows(shape) // PART_ROWS) * PART_ROWS


def _pack_rows(arrays, row_multiple):
    parts = []
    for a in arrays:
        r, c = a.shape
        if c > PACK_W:
            assert r == 1, a.shape
            r = _data_rows(a.shape)
            a = jnp.pad(a, ((0, 0), (0, r * PACK_W - c))).reshape(r, PACK_W)
            c = PACK_W
        parts.append(jnp.pad(a, ((0, _packed_rows((r, c)) - r), (0, PACK_W - c))))
    rows = sum(p.shape[0] for p in parts)
    if rows % row_multiple:
        parts.append(jnp.zeros((row_multiple - rows % row_multiple, PACK_W), parts[0].dtype))
    return jnp.concatenate(parts, axis=0)


def _unpack_rows(packed, shapes):
    out, off = [], 0
    for r, c in shapes:
        nr = _data_rows((r, c))
        blk = packed[off:off + nr]
        out.append(blk[:, :c] if c <= PACK_W else blk.reshape(1, nr * PACK_W)[:, :c])
        off += _packed_rows((r, c))
    return out


def _chip_slice(full, name, t):
    if name not in SHARD_AXIS:
        return full
    ax = SHARD_AXIS[name]
    n = FULL_SHAPE[name][ax] // N_CHIPS
    return lax.slice_in_dim(full, t * n, (t + 1) * n, axis=ax)


HBM_SPEC = pl.BlockSpec(memory_space=pl.ANY)
CHIP_FLIPS = ((1, 0), (0, 1), (1, 1))


def _gather_chips(bufs):
    nb = len(bufs)

    def body(*refs):
        ins, outs = refs[:nb], refs[nb:2 * nb]
        send, recv, loc = refs[2 * nb:]
        x, y, c = lax.axis_index("x"), lax.axis_index("y"), lax.axis_index("c")
        me = 2 * x + y
        sibling = (x, y, 1 - c)
        sends, forwards = [], []
        for b in range(nb):
            half = bufs[b].shape[0] // 2
            mine = pl.ds(c * half, half)
            own = pltpu.make_async_copy(ins[b], outs[b].at[me], loc.at[b])
            own.start()
            sends.append(own)
            for k, (fx, fy) in enumerate(CHIP_FLIPS):
                cp = pltpu.make_async_remote_copy(
                    src_ref=ins[b].at[mine], dst_ref=outs[b].at[me, mine], send_sem=send.at[b, k],
                    recv_sem=recv.at[b, k], device_id=(x ^ fx, y ^ fy, c), device_id_type=MESH_ID)
                cp.start()
                sends.append(cp)
        for b in range(nb):
            half = bufs[b].shape[0] // 2
            mine, theirs = pl.ds(c * half, half), pl.ds((1 - c) * half, half)
            for k, (fx, fy) in enumerate(CHIP_FLIPS):
                chip = 2 * (x ^ fx) + (y ^ fy)
                landed = outs[b].at[chip, mine]
                pltpu.make_async_remote_copy(src_ref=landed, dst_ref=landed, send_sem=send.at[b, k],
                                             recv_sem=recv.at[b, k], device_id=sibling,
                                             device_id_type=MESH_ID).wait_recv()
                fw = pltpu.make_async_remote_copy(src_ref=landed, dst_ref=landed, send_sem=send.at[b, 3 + k],
                                                  recv_sem=recv.at[b, 3 + k], device_id=sibling,
                                                  device_id_type=MESH_ID)
                fw.start()
                forwards.append((fw, outs[b].at[chip, theirs], b, k))
        for fw, arriving, b, k in forwards:
            pltpu.make_async_remote_copy(src_ref=arriving, dst_ref=arriving, send_sem=send.at[b, 3 + k],
                                         recv_sem=recv.at[b, 3 + k], device_id=sibling,
                                         device_id_type=MESH_ID).wait_recv()
            fw.wait_send()
        for cp in sends[1::4] + sends[2::4] + sends[3::4]:
            cp.wait_send()
        for own in sends[0::4]:
            own.wait()

    return pl.pallas_call(
        body,
        out_shape=[jax.ShapeDtypeStruct((N_CHIPS,) + b.shape, b.dtype) for b in bufs],
        in_specs=[HBM_SPEC] * nb, out_specs=[HBM_SPEC] * nb,
        scratch_shapes=[pltpu.SemaphoreType.DMA((nb, 6)), pltpu.SemaphoreType.DMA((nb, 6)),
                        pltpu.SemaphoreType.DMA((nb,))],
        name="gather_chips")(*bufs)


HBM_ONLY = pl.BlockSpec(memory_space=pltpu.HBM)
SEM_SPEC = pl.BlockSpec(memory_space=pltpu.SEMAPHORE)
SPLIT_COPY_EFFECT = pltpu.SideEffectType.DATAFLOW_SIDE_EFFECTING


def _late_copies(src_ref, land_ref, sends, recvs):
    x, y, c = lax.axis_index("x"), lax.axis_index("y"), lax.axis_index("c")
    me = 2 * x + y
    return [pltpu.make_async_remote_copy(src_ref=src_ref, dst_ref=land_ref.at[me], send_sem=sends[k], recv_sem=recvs[k],
                                         device_id=(x ^ fx, y ^ fy, c), device_id_type=MESH_ID)
            for k, (fx, fy) in enumerate(CHIP_FLIPS)]


def _gather_late_start(buf):
    def body(src_ref, land_ref, s0, s1, s2, r0, r1, r2, src_thru, land_thru, token):
        for cp in _late_copies(src_ref, land_ref, (s0, s1, s2), (r0, r1, r2)):
            cp.start()
        token[...] = jnp.zeros_like(token)

    land = lax.empty((N_CHIPS,) + buf.shape, buf.dtype)
    sem = pltpu.SemaphoreType.DMA(())
    res = pl.pallas_call(
        body, name="gather_late_start",
        out_shape=(sem,) * 6 + (pltpu.HBM(buf.shape, buf.dtype), pltpu.HBM(land.shape, land.dtype),
                                jax.ShapeDtypeStruct((8, 128), F32)),
        in_specs=(HBM_ONLY, HBM_ONLY),
        out_specs=(SEM_SPEC,) * 6 + (HBM_ONLY, HBM_ONLY, pl.BlockSpec(memory_space=pltpu.VMEM)),
        input_output_aliases={0: 6, 1: 7},
        compiler_params=pltpu.CompilerParams(has_side_effects=SPLIT_COPY_EFFECT),
    )(pltpu.with_memory_space_constraint(buf, pltpu.HBM), pltpu.with_memory_space_constraint(land, pltpu.HBM))
    return res[:6], res[6], res[7], res[8]


def _gather_late_wait(sems, src_thru, land_thru, after):
    def body(src_ref, land_ref, s0, s1, s2, r0, r1, r2, after_ref, src_dead, got_ref):
        for cp in _late_copies(src_ref, land_ref, (s0, s1, s2), (r0, r1, r2)):
            cp.wait_send()
            cp.wait_recv()

    return pl.pallas_call(
        body, name="gather_late_wait",
        out_shape=(pltpu.HBM(src_thru.shape, src_thru.dtype), pltpu.HBM(land_thru.shape, land_thru.dtype)),
        in_specs=(HBM_ONLY, HBM_ONLY) + (SEM_SPEC,) * 6 + (pl.BlockSpec(memory_space=pl.ANY),),
        out_specs=(HBM_ONLY, HBM_ONLY), input_output_aliases={0: 0, 1: 1},
        compiler_params=pltpu.CompilerParams(has_side_effects=SPLIT_COPY_EFFECT),
    )(src_thru, land_thru, *sems, after)[1]


def _sibling_swap(name, buf):
    def body(src, dst, send, recv):
        x, y, c = lax.axis_index("x"), lax.axis_index("y"), lax.axis_index("c")
        cp = pltpu.make_async_remote_copy(src_ref=src, dst_ref=dst, send_sem=send, recv_sem=recv,
                                          device_id=(x, y, 1 - c), device_id_type=MESH_ID)
        cp.start()
        cp.wait()

    return pl.pallas_call(
        body, out_shape=jax.ShapeDtypeStruct(buf.shape, buf.dtype), in_specs=[HBM_SPEC], out_specs=HBM_SPEC,
        scratch_shapes=[pltpu.SemaphoreType.DMA, pltpu.SemaphoreType.DMA], name=name)(buf)


def _scatter_chips(parts):
    nb = len(parts)

    def body(*refs):
        srcs, dsts = refs[:nb], refs[nb:2 * nb]
        send, recv = refs[2 * nb:]
        x, y, c = lax.axis_index("x"), lax.axis_index("y"), lax.axis_index("c")
        copies = []
        for b in range(nb):
            for k, (fx, fy) in enumerate(CHIP_FLIPS):
                tx, ty = x ^ fx, y ^ fy
                cp = pltpu.make_async_remote_copy(
                    src_ref=srcs[b].at[2 * tx + ty], dst_ref=dsts[b].at[k], send_sem=send.at[b, k],
                    recv_sem=recv.at[b, k], device_id=(tx, ty, c), device_id_type=MESH_ID)
                cp.start()
                copies.append(cp)
        for cp in copies:
            cp.wait()

    return pl.pallas_call(
        body, out_shape=[jax.ShapeDtypeStruct((3,) + p.shape[1:], p.dtype) for p in parts], in_specs=[HBM_SPEC] * nb,
        out_specs=[HBM_SPEC] * nb,
        scratch_shapes=[pltpu.SemaphoreType.DMA((nb, 3)), pltpu.SemaphoreType.DMA((nb, 3))],
        name="scatter_chips")(*parts)


def _add_rows(name, terms, also_bf16=False):
    rows = terms[0].shape[0]
    t = _row_tile(rows)
    n_out = 2 if also_bf16 else 1

    def body(*refs):
        acc = refs[0][...].astype(F32)
        for r in refs[1:-n_out]:
            acc = acc + r[...].astype(F32)
        refs[-n_out][...] = acc
        if also_bf16:
            refs[-1][...] = acc.astype(BF16)

    res = pl.pallas_call(
        body, out_shape=[jax.ShapeDtypeStruct(terms[0].shape, dt) for dt in (F32, BF16)[:n_out]], grid=(rows // t,),
        in_specs=[_rb(t, PACK_W)] * len(terms), out_specs=[_rb(t, PACK_W)] * n_out,
        name=name, compiler_params=_params(("parallel",)))(*terms)
    return res if also_bf16 else res[0]


def _sum_chip_order(name, parts, received, me):
    stack = jnp.concatenate([lax.dynamic_index_in_dim(parts, me, axis=0, keepdims=True), received], axis=0)
    terms = []
    for chip in range(N_CHIPS):
        xr = me ^ chip
        where = jnp.where(xr == 0, 0, jnp.where(xr == 2, 1, jnp.where(xr == 1, 2, 3)))
        terms.append(lax.dynamic_index_in_dim(stack, where, axis=0, keepdims=False))
    return _add_rows(name, terms)


def _row_tile(rows):
    assert rows % 16 == 0, rows
    return max(t for t in range(16, 513, 16) if rows % t == 0)


def _adamw(name, g, w, m, v):
    rows, cols = g.shape
    t = max(tt for tt in range(8, 257, 8) if rows % tt == 0)
    c1 = 1.0 - ADAM_B1 ** ADAM_STEP
    c2 = 1.0 - ADAM_B2 ** ADAM_STEP

    def body(g_ref, w_ref, m_ref, v_ref, d_ref, mo_ref, vo_ref):
        gg = g_ref[...]
        mn = ADAM_B1 * m_ref[...] + (1.0 - ADAM_B1) * gg
        vn = ADAM_B2 * v_ref[...] + (1.0 - ADAM_B2) * (gg * gg)
        d_ref[...] = -ADAM_LR * ((mn / c1) / (jnp.sqrt(vn / c2) + ADAM_EPS) + ADAM_WD * w_ref[...])
        mo_ref[...] = mn
        vo_ref[...] = vn

    return pl.pallas_call(
        body, out_shape=[jax.ShapeDtypeStruct(g.shape, F32)] * 3, grid=(rows // t,),
        in_specs=[_rb(t, cols)] * 4, out_specs=[_rb(t, cols)] * 3,
        name=name, compiler_params=_params(("parallel",)))(g, w, m, v)


def kernel(x, meta_tokens, norm_mix_pre, w_in, q_a_norm, w_q_up, kv_a_norm, w_kv_up, conv_w, conv_b, dt_bias, a_log, d_skip, ssm_norm, w_out, norm_mix_post, norm_mlp_pre, w_mlp_up, w_mlp_down, norm_mlp_post, loss_target, m_meta_tokens, m_norm_mix_pre, m_w_in, m_q_a_norm, m_w_q_up, m_kv_a_norm, m_w_kv_up, m_conv_w, m_conv_b, m_dt_bias, m_a_log, m_d_skip, m_ssm_norm, m_w_out, m_norm_mix_post, m_norm_mlp_pre, m_w_mlp_up, m_w_mlp_down, m_norm_mlp_post, v_meta_tokens, v_norm_mix_pre, v_w_in, v_q_a_norm, v_w_q_up, v_kv_a_norm, v_w_kv_up, v_conv_w, v_conv_b, v_dt_bias, v_a_log, v_d_skip, v_ssm_norm, v_w_out, v_norm_mix_post, v_norm_mlp_pre, v_w_mlp_up, v_w_mlp_down, v_norm_mlp_post):
    given = dict(locals())
    drop = lambda name, a: a[0] if a.ndim == 3 else a
    w_loc = {n: drop(n, given[n]) for n in WEIGHTS}
    m_loc = {n: drop(n, given["m_" + n]) for n in WEIGHTS}
    v_loc = {n: drop(n, given["v_" + n]) for n in WEIGHTS}
    ix, iy, ic = lax.axis_index("x"), lax.axis_index("y"), lax.axis_index("c")
    me = 2 * ix + iy

    def assemble(names, got, own_slot=None):
        per_chip = [_unpack_rows(got[t], [_shard_shape(n) for n in names]) for t in range(N_CHIPS)]
        full = {}
        for k, n in enumerate(names):
            pieces = [per_chip[t][k] for t in range(N_CHIPS)]
            if own_slot is not None:
                pieces = [jnp.where(me == t, own_slot[n], pieces[t]) for t in range(N_CHIPS)]
            full[n] = jnp.concatenate(pieces, axis=SHARD_AXIS[n])
        return full

    bf16_of = lambda names: _pack_rows([w_loc[n].astype(BF16) for n in names], 32)
    sems, src_thru, land_thru, token = _gather_late_start(bf16_of(GATHER_LATE))
    got16, got32 = _gather_chips([bf16_of(GATHER_EARLY), _pack_rows([w_loc[n] for n in GATHER_F32], 16)])
    w_full = {n: w_loc[n] for n in WEIGHTS if n not in SHARD_AXIS}
    w_full.update(assemble(GATHER_EARLY, got16))
    w_full.update(assemble(GATHER_F32, got32))
    w_full["token"] = token
    own16 = {n: w_loc[n].astype(BF16) for n in GATHER_LATE}
    w_full["late_weights"] = lambda after: assemble(GATHER_LATE, _gather_late_wait(sems, src_thru, land_thru, after),
                                                    own_slot=own16)

    loss, dh0, g_full = _local_step(x[0], loss_target[0], w_full)
    n_real = x.shape[1]
    grad_x = dh0[N_META:N_META + n_real][None]

    shapes = [_shard_shape(n) for n in PACK_ORDER]
    slots = [_pack_rows([_chip_slice(g_full[n], n, t) for n in PACK_ORDER], SLOT_ROWS_MULTIPLE) for t in range(N_CHIPS)]
    rows = slots[0].shape[0]
    half = rows // 2
    halves = lambda hh: jnp.concatenate([lax.dynamic_slice_in_dim(s, hh * half, half, axis=0) for s in slots], axis=0)
    keep, give = halves(ic), halves(1 - ic)
    from_sibling = _sibling_swap("sibling_swap", give)
    part32, part16 = _add_rows("chip_partial", [keep, from_sibling], also_bf16=True)
    part32 = part32.reshape(N_CHIPS, half, PACK_W)
    part16 = part16.reshape(N_CHIPS, half, PACK_W)
    assert all(n in SHARD_AXIS for n in PACK_ORDER[:len(SHARD_AXIS)])
    tail_start = sum(_packed_rows(_shard_shape(n)) for n in PACK_ORDER[:len(SHARD_AXIS)])
    assert rows - tail_start <= TAIL_ROWS <= half
    tail32 = part32[:, half - TAIL_ROWS:, :]
    from16, from_tail = _scatter_chips([part16, tail32])
    my_half = _sum_chip_order("chip_total", part16, from16, me)
    my_tail = _sum_chip_order("chip_total_tail", tail32, from_tail, me)
    my_half = lax.dynamic_update_slice(my_half, my_tail, (half - TAIL_ROWS, 0))
    other_half = _sibling_swap("sibling_gather", my_half)
    g_red = jnp.where(ic == 0, jnp.concatenate([my_half, other_half], axis=0),
                      jnp.concatenate([other_half, my_half], axis=0))

    g_own = dict(zip(PACK_ORDER, _unpack_rows(g_red, shapes)))
    small = [n for n in PACK_ORDER if n not in ADAM_NATURAL]
    pack_small = lambda d: _pack_rows([d[n] for n in small], 16)
    packed_upd = _adamw("adamw_small", pack_small(g_own), pack_small(w_loc), pack_small(m_loc), pack_small(v_loc))
    upd = [dict(zip(small, _unpack_rows(p, [_shard_shape(n) for n in small]))) for p in packed_upd]
    for n in ADAM_NATURAL:
        for k, res in enumerate(_adamw("adamw_" + n, g_own[n], w_loc[n], m_loc[n], v_loc[n])):
            upd[k][n] = res

    def outputs(parts):
        return [parts[n][None] if given[n].ndim == 3 else parts[n] for n in WEIGHTS]

    total = lax.psum(loss[0, 0], ("x", "y", "c"))
    return (total, grad_x, *outputs(g_own), *outputs(upd[0]), *outputs(upd[1]), *outputs(upd[2]))
```

```python
import functools

import numpy as np
import jax
import jax.numpy as jnp
from jax import lax
from jax.experimental import pallas as pl
from jax.experimental.pallas import tpu as pltpu

F32 = jnp.float32
BF16 = jnp.bfloat16

D_MODEL = 1024
N_META = 16
EPS = 1e-6
ATT_HEADS = 8
Q_LORA = 384
KV_LORA = 256
QK_NOPE = 128
QK_ROPE = 64
V_HEAD = 128
ROPE_THETA = 10000.0
SSM_HEADS = 16
SSM_HEAD_DIM = 64
SSM_WIDTH = 1024
SSM_STATE = 128
CONV_K = 4
D_FF = 4096
ATT_SCALE = float((QK_NOPE + QK_ROPE) ** -0.5)
ATT_SCALE_LOG2 = float(ATT_SCALE * np.log2(np.e))

ADAM_LR = 0.001
ADAM_B1 = 0.9
ADAM_B2 = 0.999
ADAM_EPS = 1e-08
ADAM_WD = 0.01
ADAM_STEP = 10

SEG_KV, SEG_KR, SEG_CQ, SEG_DT, SEG_Z, SEG_XS, SEG_BC = 0, 256, 384, 768, 1024, 2048, 3072
PROJ_W = 3584
QP_W = 256

ROW_BLOCK = 512
MM_BLOCK = 512
SSD_CHUNK = 256
ATT_BLOCK = 768
ATT_BLOCK_FWD = 1536
ATT_BLOCK_Q_BWD = 1536
VMEM_LIMIT = 56 * 1024 * 1024
NEG_BIG = -1e30

PACK_W = 1024
TAIL_ROWS = 96
SLOT_ROWS_MULTIPLE = 256
N_CHIPS = 4
MESH_ID = pl.DeviceIdType.MESH


def _params(sem):
    return pltpu.CompilerParams(dimension_semantics=sem, vmem_limit_bytes=VMEM_LIMIT)


def _rb(rows, width, cb=0):
    return pl.BlockSpec((rows, width), lambda i: (i, cb))


def _full(shape):
    zeros = (0,) * len(shape)
    return pl.BlockSpec(shape, lambda i: zeros)


def _acc_add(ref, val):
    first = pl.program_id(0) == 0

    @pl.when(first)
    def _():
        ref[...] = val

    @pl.when(jnp.logical_not(first))
    def _():
        ref[...] += val


def _rms(x, g):
    r = lax.rsqrt(jnp.mean(x * x, axis=-1, keepdims=True) + EPS)
    return x * r * g


def _rms_bwd(x, g, dy):
    r = lax.rsqrt(jnp.mean(x * x, axis=-1, keepdims=True) + EPS)
    dyg = dy * g
    dx = r * dyg - x * (r * r * r) * jnp.mean(x * dyg, axis=-1, keepdims=True)
    dg = jnp.sum(dy * x * r, axis=0, keepdims=True)
    return dx, dg


def _sigmoid(x):
    return 1.0 / (1.0 + jnp.exp(-x))


def _swap32(x):
    lane = lax.broadcasted_iota(jnp.int32, x.shape, 1)
    return jnp.where((lane % 64) < 32, pltpu.roll(x, 96, 1), pltpu.roll(x, 32, 1))


def _rope(x, cos_t, sin_t):
    return x * cos_t + _swap32(x) * sin_t


def _rope_bwd(dr, cos_t, sin_t):
    return dr * cos_t + _swap32(dr * sin_t)


def _tile(n, cap):
    if n <= cap:
        return n
    best = 128
    for t in range(128, cap + 1, 128):
        if n % t == 0:
            best = t
    assert n % best == 0, (n, cap)
    return best


MM_VMEM_BUDGET = 40 * 1024 * 1024
TN_ROWS_CAP = 1536


def _mm(name, a, b, outs=((F32, None),), epi_ins=(), trans_b=False):
    a_parts = tuple(a) if isinstance(a, (tuple, list)) else (a,)
    assert len(a_parts) == 1 or not trans_b
    m = a_parts[0].shape[0]
    widths = [p.shape[1] for p in a_parts]
    k = sum(widths)
    n = b.shape[0] if trans_b else b.shape[1]
    tm = MM_BLOCK
    n_a, n_epi = len(a_parts), len(epi_ins)
    out_bytes = sum(jnp.dtype(dt).itemsize for dt, _ in outs) + sum(e.dtype.itemsize for e in epi_ins)
    a_bytes = sum(p.shape[1] * p.dtype.itemsize for p in a_parts)
    step_bytes = lambda tn: 2 * (tm * a_bytes + k * tn * b.dtype.itemsize + tm * tn * out_bytes)
    tn = n
    while step_bytes(tn) > MM_VMEM_BUDGET and tn % 256 == 0:
        tn //= 2
    assert n % tn == 0 and step_bytes(tn) <= MM_VMEM_BUDGET, (name, n, tn)

    def body(*refs):
        a_refs, b_ref = refs[:n_a], refs[n_a]
        epi_refs = refs[n_a + 1:n_a + 1 + n_epi]
        out_refs = refs[n_a + 1 + n_epi:]
        if trans_b:
            r = _nt(a_refs[0][...].astype(BF16), b_ref[...].astype(BF16))
        else:
            r, off = None, 0
            for a_ref, wd in zip(a_refs, widths):
                part = jnp.dot(a_ref[...].astype(BF16), b_ref[off:off + wd, :].astype(BF16),
                               preferred_element_type=F32)
                r = part if r is None else r + part
                off += wd
        blocks = [e[...] for e in epi_refs]
        for o_ref, (dt, fn) in zip(out_refs, outs):
            o_ref[...] = (r if fn is None else fn(r, *blocks)).astype(dt)

    out_spec = pl.BlockSpec((tm, tn), lambda j, i: (i, j))
    b_spec = pl.BlockSpec((tn, k), lambda j, i: (j, 0)) if trans_b else pl.BlockSpec((k, tn), lambda j, i: (0, j))
    res = pl.pallas_call(
        body,
        out_shape=[jax.ShapeDtypeStruct((m, n), dt) for dt, _ in outs],
        grid=(n // tn, m // tm),
        in_specs=[pl.BlockSpec((tm, wd), lambda j, i: (i, 0)) for wd in widths] + [b_spec] + [out_spec] * n_epi,
        out_specs=[out_spec] * len(outs),
        name=name,
        compiler_params=_params(("parallel", "parallel")),
    )(*a_parts, b, *epi_ins)
    return res[0] if len(outs) == 1 else res


def _mm_tn(name, x, dy, ta_cap=1024, tn_cap=1024):
    l, a = x.shape
    n = dy.shape[1]
    ta, tn = _tile(a, ta_cap), _tile(n, tn_cap)
    tl = max(t for t in range(MM_BLOCK, TN_ROWS_CAP + 1, MM_BLOCK) if l % t == 0)
    nl = l // tl

    def body(x_ref, dy_ref, o_ref):
        ll = pl.program_id(2)

        @pl.when(ll == 0)
        def _():
            o_ref[...] = jnp.zeros_like(o_ref)

        o_ref[...] += lax.dot_general(x_ref[...].astype(BF16), dy_ref[...].astype(BF16), (((0,), (0,)), ((), ())),
                                      preferred_element_type=F32)

    return pl.pallas_call(
        body,
        out_shape=jax.ShapeDtypeStruct((a, n), F32),
        grid=(a // ta, n // tn, nl),
        in_specs=[pl.BlockSpec((tl, ta), lambda i, j, ll: (ll, i)), pl.BlockSpec((tl, tn), lambda i, j, ll: (ll, j))],
        out_specs=pl.BlockSpec((ta, tn), lambda i, j, ll: (i, j)),
        name=name,
        compiler_params=_params(("parallel", "parallel", "arbitrary")),
    )(x, dy)


def _norm_in(h0, g_pre):
    lp = h0.shape[0]
    t = ROW_BLOCK

    def body(h_ref, g_ref, o_ref):
        o_ref[...] = _rms(h_ref[...], g_ref[...]).astype(BF16)

    return pl.pallas_call(
        body, out_shape=jax.ShapeDtypeStruct((lp, D_MODEL), BF16), grid=(lp // t,),
        in_specs=[_rb(t, D_MODEL), _full((1, D_MODEL))], out_specs=_rb(t, D_MODEL),
        name="norm_in", compiler_params=_params(("parallel",)))(h0, g_pre)


def _attn_prep(proj, g_q, g_kv, cos_t, sin_t):
    lp = proj.shape[0]
    t = ROW_BLOCK

    def body(ckv_ref, kr_ref, cq_ref, gq_ref, gkv_ref, cos_ref, sin_ref, cqn_ref, ckvn_ref, krr_ref):
        cqn_ref[...] = _rms(cq_ref[...], gq_ref[...]).astype(BF16)
        ckvn_ref[...] = _rms(ckv_ref[...], gkv_ref[...]).astype(BF16)
        roped = _rope(kr_ref[...], cos_ref[...], sin_ref[...])
        krr_ref[...] = roped + pltpu.roll(roped, 64, 1)

    return pl.pallas_call(
        body,
        out_shape=[jax.ShapeDtypeStruct((lp, Q_LORA), BF16), jax.ShapeDtypeStruct((lp, KV_LORA), BF16),
                   jax.ShapeDtypeStruct((lp, 128), F32)],
        grid=(lp // t,),
        in_specs=[_rb(t, KV_LORA, SEG_KV // KV_LORA), _rb(t, 128, SEG_KR // 128), _rb(t, Q_LORA, SEG_CQ // Q_LORA),
                  _full((1, Q_LORA)), _full((1, KV_LORA)), _rb(t, 128), _rb(t, 128)],
        out_specs=[_rb(t, Q_LORA), _rb(t, KV_LORA), _rb(t, 128)],
        name="attn_prep", compiler_params=_params(("parallel",)))(proj, proj, proj, g_q, g_kv, cos_t, sin_t)


def _qk_pack(q, kv, krr, cos_t, sin_t):
    lp = q.shape[0]
    t = ROW_BLOCK

    def body(q_ref, kv_ref, krr_ref, cos_ref, sin_ref, qs_ref, ks_ref, vs_ref, vts_ref):
        lane = lax.broadcasted_iota(jnp.int32, (t, 128), 1)
        lo = lane < 64
        krr = krr_ref[...].astype(BF16)
        for j in range(ATT_HEADS // 2):
            pr = _rope(q_ref[:, 1024 + 128 * j:1024 + 128 * (j + 1)], cos_ref[...], sin_ref[...])
            for h, keep in ((2 * j, lo), (2 * j + 1, jnp.logical_not(lo))):
                qs_ref[h, :, 0:128] = q_ref[:, 128 * h:128 * (h + 1)].astype(BF16)
                qs_ref[h, :, 128:256] = jnp.where(keep, pr, 0.0).astype(BF16)
        for h in range(ATT_HEADS):
            ks_ref[h, :, 0:128] = kv_ref[:, 256 * h:256 * h + 128].astype(BF16)
            ks_ref[h, :, 128:256] = krr
            v = kv_ref[:, 256 * h + 128:256 * (h + 1)]
            vs_ref[h] = v.astype(BF16)
            vts_ref[h] = v.astype(F32).T.astype(BF16)

    slab = lambda w: pl.BlockSpec((ATT_HEADS, t, w), lambda i: (0, i, 0))
    return pl.pallas_call(
        body,
        out_shape=[jax.ShapeDtypeStruct((ATT_HEADS, lp, QP_W), BF16), jax.ShapeDtypeStruct((ATT_HEADS, lp, QP_W), BF16),
                   jax.ShapeDtypeStruct((ATT_HEADS, lp, V_HEAD), BF16), jax.ShapeDtypeStruct((ATT_HEADS, V_HEAD, lp), BF16)],
        grid=(lp // t,),
        in_specs=[_rb(t, 1536), _rb(t, 2048), _rb(t, 128), _rb(t, 128), _rb(t, 128)],
        out_specs=[slab(QP_W), slab(QP_W), slab(V_HEAD), pl.BlockSpec((ATT_HEADS, V_HEAD, t), lambda i: (0, 0, i))],
        name="qk_pack", compiler_params=_params(("parallel",)))(q, kv, krr, cos_t, sin_t)


def _shifted(ext, t, shift):
    if shift == 0:
        return ext[8:, :]
    return pltpu.roll(ext, shift, 0)[8:, :]


def _conv_fwd(name, proj, seg, width, conv_w, conv_b):
    lp = proj.shape[0]
    t = ROW_BLOCK
    cb = seg // width

    def body(u_ref, halo_ref, w_ref, b_ref, pre_ref, act_ref):
        i = pl.program_id(0)
        u = u_ref[...]
        halo = jnp.where(i > 0, halo_ref[...], 0.0)
        ext = jnp.concatenate([halo, u], axis=0)
        pre = jnp.broadcast_to(b_ref[...], (t, width))
        for k in range(CONV_K):
            pre = pre + w_ref[k:k + 1, :] * _shifted(ext, t, CONV_K - 1 - k)
        pre_ref[...] = pre
        act_ref[...] = pre * _sigmoid(pre)

    return pl.pallas_call(
        body,
        out_shape=[jax.ShapeDtypeStruct((lp, width), F32)] * 2,
        grid=(lp // t,),
        in_specs=[_rb(t, width, cb),
                  pl.BlockSpec((8, width), lambda i: (jnp.maximum(i * (t // 8) - 1, 0), cb)),
                  _full((CONV_K, width)), _full((1, width))],
        out_specs=[_rb(t, width), _rb(t, width)],
        name=name, compiler_params=_params(("parallel",)))(proj, proj, conv_w, conv_b)


def _softplus(x):
    return jnp.maximum(x, 0.0) + jnp.log1p(jnp.exp(-jnp.abs(x)))


def _dt_fwd(proj, dt_bias_pad):
    lp = proj.shape[0]
    t = ROW_BLOCK

    def body(x_ref, b_ref, o_ref):
        o_ref[...] = _softplus(x_ref[...] + b_ref[...])

    return pl.pallas_call(
        body, out_shape=jax.ShapeDtypeStruct((lp, 128), F32), grid=(lp // t,),
        in_specs=[_rb(t, 128, SEG_DT // 128), _full((1, 128))], out_specs=_rb(t, 128),
        name="dt_fwd", compiler_params=_params(("parallel",)))(proj, dt_bias_pad)


def _gated_norm_group(y, z, w):
    g = y * (z * _sigmoid(z))
    return g * lax.rsqrt(jnp.mean(g * g, axis=-1, keepdims=True) + EPS) * w


def _gated_norm_fwd(y, proj, w):
    lp = y.shape[0]
    t = ROW_BLOCK
    gw = SSM_WIDTH // 2

    def body(y0, y1, z0, z1, w0, w1, o_ref):
        o_ref[:, 0:gw] = _gated_norm_group(y0[...], z0[...], w0[...]).astype(BF16)
        o_ref[:, gw:] = _gated_norm_group(y1[...], z1[...], w1[...]).astype(BF16)

    zb = SEG_Z // gw
    return pl.pallas_call(
        body, out_shape=jax.ShapeDtypeStruct((lp, SSM_WIDTH), BF16), grid=(lp // t,),
        in_specs=[_rb(t, gw, 0), _rb(t, gw, 1), _rb(t, gw, zb), _rb(t, gw, zb + 1),
                  pl.BlockSpec((1, gw), lambda i: (0, 0)), pl.BlockSpec((1, gw), lambda i: (0, 1))],
        out_specs=_rb(t, SSM_WIDTH),
        name="gated_norm_fwd", compiler_params=_params(("parallel",)))(y, y, proj, proj, w, w)


def _gated_norm_bwd(y, proj, w, dssm):
    lp = y.shape[0]
    t = ROW_BLOCK
    gw = SSM_WIDTH // 2

    def body(y0, y1, z0, z1, w0, w1, d0, d1, dy_ref, dz_ref, dw_ref):
        dws = []
        for g, (yr, zr, wr, dr) in enumerate(((y0, z0, w0, d0), (y1, z1, w1, d1))):
            _, vjp = jax.vjp(_gated_norm_group, yr[...], zr[...], wr[...])
            dyg, dzg, dwg = vjp(dr[...])
            dy_ref[:, g * gw:(g + 1) * gw] = dyg
            dz_ref[:, g * gw:(g + 1) * gw] = dzg
            dws.append(dwg)
        _acc_add(dw_ref, jnp.concatenate(dws, axis=1))

    zb = SEG_Z // gw
    return pl.pallas_call(
        body,
        out_shape=[jax.ShapeDtypeStruct((lp, SSM_WIDTH), F32), jax.ShapeDtypeStruct((lp, SSM_WIDTH), F32),
                   jax.ShapeDtypeStruct((1, SSM_WIDTH), F32)],
        grid=(lp // t,),
        in_specs=[_rb(t, gw, 0), _rb(t, gw, 1), _rb(t, gw, zb), _rb(t, gw, zb + 1),
                  pl.BlockSpec((1, gw), lambda i: (0, 0)), pl.BlockSpec((1, gw), lambda i: (0, 1)),
                  _rb(t, gw, 2), _rb(t, gw, 3)],
        out_specs=[_rb(t, SSM_WIDTH), _rb(t, SSM_WIDTH), _full((1, SSM_WIDTH))],
        name="gated_norm_bwd", compiler_params=_params(("arbitrary",)))(y, y, proj, proj, w, w, dssm, dssm)


def _mix_residual(h0, mix, g_post, g_mlp_pre):
    lp = h0.shape[0]
    t = ROW_BLOCK

    def body(h_ref, m_ref, gp_ref, gm_ref, h1_ref, n2_ref):
        h1 = h_ref[...] + _rms(m_ref[...], gp_ref[...])
        h1_ref[...] = h1
        n2_ref[...] = _rms(h1, gm_ref[...]).astype(BF16)

    return pl.pallas_call(
        body, out_shape=[jax.ShapeDtypeStruct((lp, D_MODEL), F32), jax.ShapeDtypeStruct((lp, D_MODEL), BF16)],
        grid=(lp // t,),
        in_specs=[_rb(t, D_MODEL), _rb(t, D_MODEL), _full((1, D_MODEL)), _full((1, D_MODEL))],
        out_specs=[_rb(t, D_MODEL), _rb(t, D_MODEL)],
        name="mix_residual", compiler_params=_params(("parallel",)))(h0, mix, g_post, g_mlp_pre)


def _loss_and_grad(h1, f, g_post, tgt, n_real):
    lp = h1.shape[0]
    t = ROW_BLOCK
    assert n_real % t == 0 and t % N_META == 0
    n_tb = n_real // t

    def body(h1_ref, f_ref, g_ref, halo_ref, t_ref, loss_ref, dh2_ref, df_ref, dg_ref):
        i = pl.program_id(0)
        fx = f_ref[...]
        h2 = h1_ref[...] + _rms(fx, g_ref[...])
        row = i * t + lax.broadcasted_iota(jnp.int32, (t, 1), 0)
        real = jnp.logical_and(row >= N_META, row < N_META + n_real)
        target = jnp.concatenate([halo_ref[...], t_ref[0:t - N_META, :]], axis=0)
        diff = jnp.where(real, h2 - target, 0.0)
        part = 0.5 * jnp.sum(jnp.sum(diff * diff, axis=-1, keepdims=True) / D_MODEL, axis=0, keepdims=True)
        _acc_add(loss_ref, jnp.broadcast_to(part, (1, 128)))
        dh2 = diff / D_MODEL
        dh2_ref[...] = dh2
        dfx, dg = _rms_bwd(fx, g_ref[...], dh2)
        df_ref[...] = dfx.astype(BF16)
        _acc_add(dg_ref, dg)

    return pl.pallas_call(
        body,
        out_shape=[jax.ShapeDtypeStruct((1, 128), F32), jax.ShapeDtypeStruct((lp, D_MODEL), F32),
                   jax.ShapeDtypeStruct((lp, D_MODEL), BF16), jax.ShapeDtypeStruct((1, D_MODEL), F32)],
        grid=(lp // t,),
        in_specs=[_rb(t, D_MODEL), _rb(t, D_MODEL), _full((1, D_MODEL)),
                  pl.BlockSpec((N_META, D_MODEL),
                               lambda i: (jnp.clip(i * (t // N_META) - 1, 0, n_real // N_META - 1), 0)),
                  pl.BlockSpec((t, D_MODEL), lambda i: (jnp.minimum(i, n_tb - 1), 0))],
        out_specs=[_full((1, 128)), _rb(t, D_MODEL), _rb(t, D_MODEL), _full((1, D_MODEL))],
        name="loss_and_grad", compiler_params=_params(("arbitrary",)))(h1, f, g_post, tgt, tgt)


def _mlp_residual_bwd(dh2, dn2, h1, g_mlp_pre, mix, g_post):
    lp = h1.shape[0]
    t = ROW_BLOCK

    def body(dh2_ref, dn2_ref, h1_ref, gm_ref, mix_ref, gp_ref, dh1_ref, dmix_ref, dgm_ref, dgp_ref):
        dx, dgm = _rms_bwd(h1_ref[...], gm_ref[...], dn2_ref[...])
        dh1 = dh2_ref[...] + dx
        dh1_ref[...] = dh1
        dmix, dgp = _rms_bwd(mix_ref[...], gp_ref[...], dh1)
        dmix_ref[...] = dmix.astype(BF16)
        _acc_add(dgm_ref, dgm)
        _acc_add(dgp_ref, dgp)

    return pl.pallas_call(
        body,
        out_shape=[jax.ShapeDtypeStruct((lp, D_MODEL), F32), jax.ShapeDtypeStruct((lp, D_MODEL), BF16),
                   jax.ShapeDtypeStruct((1, D_MODEL), F32), jax.ShapeDtypeStruct((1, D_MODEL), F32)],
        grid=(lp // t,),
        in_specs=[_rb(t, D_MODEL), _rb(t, D_MODEL), _rb(t, D_MODEL), _full((1, D_MODEL)), _rb(t, D_MODEL),
                  _full((1, D_MODEL))],
        out_specs=[_rb(t, D_MODEL), _rb(t, D_MODEL), _full((1, D_MODEL)), _full((1, D_MODEL))],
        name="mlp_residual_bwd", compiler_params=_params(("arbitrary",)))(dh2, dn2, h1, g_mlp_pre, mix, g_post)


def _input_norm_bwd(dh1, dn1, h0, g_pre):
    lp = h0.shape[0]
    t = ROW_BLOCK

    def body(dh1_ref, dn1_ref, h0_ref, g_ref, dh0_ref, dg_ref):
        dx, dg = _rms_bwd(h0_ref[...], g_ref[...], dn1_ref[...])
        dh0_ref[...] = dh1_ref[...] + dx
        _acc_add(dg_ref, dg)

    return pl.pallas_call(
        body, out_shape=[jax.ShapeDtypeStruct((lp, D_MODEL), F32), jax.ShapeDtypeStruct((1, D_MODEL), F32)],
        grid=(lp // t,),
        in_specs=[_rb(t, D_MODEL), _rb(t, D_MODEL), _rb(t, D_MODEL), _full((1, D_MODEL))],
        out_specs=[_rb(t, D_MODEL), _full((1, D_MODEL))],
        name="input_norm_bwd", compiler_params=_params(("arbitrary",)))(dh1, dn1, h0, g_pre)


def _conv_bwd(name, dact, pre, proj, seg, width, conv_w):
    lp = proj.shape[0]
    t = ROW_BLOCK
    cb = seg // width
    nblk = lp // t

    def dsilu(p):
        s = _sigmoid(p)
        return s * (1.0 + p * (1.0 - s))

    def body(da_ref, dan_ref, pre_ref, pren_ref, u_ref, halo_ref, w_ref, du_ref, dw_ref, db_ref):
        i = pl.program_id(0)
        dpre = da_ref[...] * dsilu(pre_ref[...])
        dpre_next = jnp.where(i < nblk - 1, dan_ref[...] * dsilu(pren_ref[...]), 0.0)
        extd = jnp.concatenate([dpre, dpre_next], axis=0)
        halo = jnp.where(i > 0, halo_ref[...], 0.0)
        ext = jnp.concatenate([halo, u_ref[...]], axis=0)
        du = jnp.zeros((t, width), F32)
        dws = []
        for k in range(CONV_K):
            m = CONV_K - 1 - k
            ahead = dpre if m == 0 else pltpu.roll(extd, t + 8 - m, 0)[:t, :]
            du = du + w_ref[k:k + 1, :] * ahead
            dws.append(jnp.sum(dpre * _shifted(ext, t, m), axis=0, keepdims=True))
        du_ref[...] = du
        _acc_add(dw_ref, jnp.concatenate(dws + [jnp.zeros((8 - CONV_K, width), F32)], axis=0))
        _acc_add(db_ref, jnp.sum(dpre, axis=0, keepdims=True))

    nxt = lambda i: (jnp.minimum((i + 1) * (t // 8), lp // 8 - 1), 0)
    return pl.pallas_call(
        body,
        out_shape=[jax.ShapeDtypeStruct((lp, width), F32), jax.ShapeDtypeStruct((8, width), F32),
                   jax.ShapeDtypeStruct((1, width), F32)],
        grid=(nblk,),
        in_specs=[_rb(t, width), pl.BlockSpec((8, width), nxt), _rb(t, width), pl.BlockSpec((8, width), nxt),
                  _rb(t, width, cb),
                  pl.BlockSpec((8, width), lambda i: (jnp.maximum(i * (t // 8) - 1, 0), cb)),
                  _full((CONV_K, width))],
        out_specs=[_rb(t, width), _full((8, width)), _full((1, width))],
        name=name, compiler_params=_params(("arbitrary",)))(dact, dact, pre, pre, proj, proj, conv_w)


def _qk_unpack_bwd(dqs, dks, dvs, cos_t, sin_t):
    lp = dqs.shape[1]
    t = ROW_BLOCK

    def body(dqs_ref, dks_ref, dvs_ref, cos_ref, sin_ref, dq_ref, dkv_ref, dkr_ref):
        lane = lax.broadcasted_iota(jnp.int32, (t, 128), 1)
        lo = lane < 64
        for j in range(ATT_HEADS // 2):
            dpr = jnp.where(lo, dqs_ref[2 * j, :, 128:256], dqs_ref[2 * j + 1, :, 128:256])
            dq_ref[:, 1024 + 128 * j:1024 + 128 * (j + 1)] = _rope_bwd(dpr, cos_ref[...], sin_ref[...]).astype(BF16)
        dkrr = jnp.zeros((t, 128), F32)
        for h in range(ATT_HEADS):
            dq_ref[:, 128 * h:128 * (h + 1)] = dqs_ref[h, :, 0:128].astype(BF16)
            dkv_ref[:, 256 * h:256 * h + 128] = dks_ref[h, :, 0:128].astype(BF16)
            dkv_ref[:, 256 * h + 128:256 * (h + 1)] = dvs_ref[h].astype(BF16)
            dkrr = dkrr + dks_ref[h, :, 128:256]
        droped = jnp.where(lo, dkrr + pltpu.roll(dkrr, 64, 1), 0.0)
        dkr_ref[...] = _rope_bwd(droped, cos_ref[...], sin_ref[...])

    slab = lambda w: pl.BlockSpec((ATT_HEADS, t, w), lambda i: (0, i, 0))
    return pl.pallas_call(
        body,
        out_shape=[jax.ShapeDtypeStruct((lp, 1536), BF16), jax.ShapeDtypeStruct((lp, 2048), BF16),
                   jax.ShapeDtypeStruct((lp, 128), F32)],
        grid=(lp // t,),
        in_specs=[slab(QP_W), slab(QP_W), slab(V_HEAD), _rb(t, 128), _rb(t, 128)],
        out_specs=[_rb(t, 1536), _rb(t, 2048), _rb(t, 128)],
        name="qk_unpack_bwd", compiler_params=_params(("parallel",)))(dqs, dks, dvs, cos_t, sin_t)


def _proj_grad(proj, dcqn, dckvn, g_q, g_kv, dkr, ddt_pad, dt_bias_pad, dz, dxs, dbc):
    lp = proj.shape[0]
    t = ROW_BLOCK

    def body(ckv_ref, cq_ref, pdt_ref, dcq_ref, dckv_ref, gq_ref, gkv_ref, dkr_ref, ddt_ref, b_ref, dz_ref, dxs_ref,
             dbc_ref, dp_ref, dgq_ref, dgkv_ref, db_ref):
        dckv, dgkv = _rms_bwd(ckv_ref[...], gkv_ref[...], dckv_ref[...])
        dcq, dgq = _rms_bwd(cq_ref[...], gq_ref[...], dcq_ref[...])
        ddt_raw = ddt_ref[...] * _sigmoid(pdt_ref[...] + b_ref[...])
        dp_ref[:, SEG_KV:SEG_KV + KV_LORA] = dckv.astype(BF16)
        dp_ref[:, SEG_KR:SEG_KR + 128] = dkr_ref[...].astype(BF16)
        dp_ref[:, SEG_CQ:SEG_CQ + Q_LORA] = dcq.astype(BF16)
        dp_ref[:, SEG_DT:SEG_DT + 128] = ddt_raw.astype(BF16)
        dp_ref[:, SEG_DT + 128:SEG_Z] = jnp.zeros((t, SEG_Z - SEG_DT - 128), BF16)
        dp_ref[:, SEG_Z:SEG_XS] = dz_ref[...].astype(BF16)
        dp_ref[:, SEG_XS:SEG_BC] = dxs_ref[...].astype(BF16)
        dp_ref[:, SEG_BC:PROJ_W] = dbc_ref[...].astype(BF16)
        _acc_add(dgq_ref, dgq)
        _acc_add(dgkv_ref, dgkv)
        _acc_add(db_ref, jnp.sum(ddt_raw, axis=0, keepdims=True))

    return pl.pallas_call(
        body,
        out_shape=[jax.ShapeDtypeStruct((lp, PROJ_W), BF16), jax.ShapeDtypeStruct((1, Q_LORA), F32),
                   jax.ShapeDtypeStruct((1, KV_LORA), F32), jax.ShapeDtypeStruct((1, 128), F32)],
        grid=(lp // t,),
        in_specs=[_rb(t, KV_LORA, SEG_KV // KV_LORA), _rb(t, Q_LORA, SEG_CQ // Q_LORA), _rb(t, 128, SEG_DT // 128),
                  _rb(t, Q_LORA), _rb(t, KV_LORA), _full((1, Q_LORA)), _full((1, KV_LORA)), _rb(t, 128), _rb(t, 128),
                  _full((1, 128)), _rb(t, SSM_WIDTH), _rb(t, SSM_WIDTH), _rb(t, 512)],
        out_specs=[_rb(t, PROJ_W), _full((1, Q_LORA)), _full((1, KV_LORA)), _full((1, 128))],
        name="proj_grad", compiler_params=_params(("arbitrary",)))(
            proj, proj, proj, dcqn, dckvn, g_q, g_kv, dkr, ddt_pad, dt_bias_pad, dz, dxs, dbc)


def _pair_tables(n):
    qmaj = [(i, j) for i in range(n) for j in range(i + 1)]
    kmaj = [(i, j) for j in range(n) for i in range(j, n)]
    to = lambda ps, c: jnp.asarray(np.array([p[c] for p in ps], np.int32))
    return (to(qmaj, 0), to(qmaj, 1)), (to(kmaj, 0), to(kmaj, 1))


def _att_block(lp, edge=ATT_BLOCK):
    return edge if lp % edge == 0 else MM_BLOCK


def _nt(a, b):
    return lax.dot_general(a, b, (((1,), (1,)), ((), ())), preferred_element_type=F32)


def _attn_fwd(qs, ks, vts):
    lp = qs.shape[1]
    t = _att_block(lp, ATT_BLOCK_FWD)
    n = lp // t
    (qi, kj), _ = _pair_tables(n)
    th = t // 2

    def body(qi_ref, kj_ref, q_ref, k_ref, vt_ref, o_ref, o16_ref, lse_ref, m_s, l_s, acc_s):
        p = pl.program_id(1)
        i, j = qi_ref[p], kj_ref[p]

        @pl.when(j == 0)
        def _():
            m_s[...] = jnp.full_like(m_s, NEG_BIG)
            l_s[...] = jnp.zeros_like(l_s)
            acc_s[...] = jnp.zeros_like(acc_s)

        def update(masked, k0=0, kn=t, q0=0, qn=t):
            cols = slice(q0, q0 + qn)
            sc = _nt(k_ref[0, k0:k0 + kn, :], q_ref[0, cols, :]) * ATT_SCALE_LOG2
            if masked:
                keep = (lax.broadcasted_iota(jnp.int32, (kn, qn), 1) + q0
                        >= lax.broadcasted_iota(jnp.int32, (kn, qn), 0) + k0)
                sc = jnp.where(keep, sc, NEG_BIG)
            m_prev = m_s[:, cols]
            m_new = jnp.maximum(m_prev, jnp.max(sc, axis=0, keepdims=True))
            alpha = jnp.exp2(m_prev - m_new)
            pexp = jnp.exp2(sc - m_new)
            l_s[:, cols] = alpha * l_s[:, cols] + jnp.sum(pexp, axis=0, keepdims=True)
            acc_s[:, cols] = alpha * acc_s[:, cols] + jnp.dot(vt_ref[0, :, k0:k0 + kn], pexp.astype(BF16),
                                                              preferred_element_type=F32)
            m_s[:, cols] = m_new

        @pl.when(j < i)
        def _():
            update(False)

        @pl.when(j == i)
        def _():
            if th % 128 == 0:
                update(True, 0, th, 0, t)
                update(True, th, th, th, th)
            else:
                update(True)
            out = (acc_s[...] / l_s[...]).T
            o_ref[...] = out
            o16_ref[...] = out.astype(BF16)
            lse_ref[0] = m_s[...] + jnp.log2(l_s[...])

    grid_spec = pltpu.PrefetchScalarGridSpec(
        num_scalar_prefetch=2, grid=(ATT_HEADS, int(qi.shape[0])),
        in_specs=[pl.BlockSpec((1, t, QP_W), lambda h, p, qi, kj: (h, qi[p], 0)),
                  pl.BlockSpec((1, t, QP_W), lambda h, p, qi, kj: (h, kj[p], 0)),
                  pl.BlockSpec((1, V_HEAD, t), lambda h, p, qi, kj: (h, 0, kj[p]))],
        out_specs=[pl.BlockSpec((t, V_HEAD), lambda h, p, qi, kj: (qi[p], h)),
                   pl.BlockSpec((t, V_HEAD), lambda h, p, qi, kj: (qi[p], h)),
                   pl.BlockSpec((1, 1, t), lambda h, p, qi, kj: (h, 0, qi[p]))],
        scratch_shapes=[pltpu.VMEM((1, t), F32), pltpu.VMEM((1, t), F32), pltpu.VMEM((V_HEAD, t), F32)])
    return pl.pallas_call(
        body, grid_spec=grid_spec,
        out_shape=[jax.ShapeDtypeStruct((lp, ATT_HEADS * V_HEAD), F32),
                   jax.ShapeDtypeStruct((lp, ATT_HEADS * V_HEAD), BF16), jax.ShapeDtypeStruct((ATT_HEADS, 1, lp), F32)],
        name="attn_fwd", compiler_params=_params(("parallel", "arbitrary")))(qi, kj, qs, ks, vts)


def _attn_delta(datt, att):
    lp = att.shape[0]
    t = MM_BLOCK
    w = ATT_HEADS * V_HEAD

    def body(do_ref, o_ref, d_ref):
        ones = jnp.ones((8, V_HEAD), BF16)
        for h in range(ATT_HEADS):
            cols = slice(h * V_HEAD, (h + 1) * V_HEAD)
            prod = do_ref[:, cols] * o_ref[:, cols]
            hi = prod.astype(BF16)
            lo = (prod - hi.astype(F32)).astype(BF16)
            d_ref[h] = (_nt(ones, hi) + _nt(ones, lo))[0:1, :]

    return pl.pallas_call(
        body, out_shape=jax.ShapeDtypeStruct((ATT_HEADS, 1, lp), F32), grid=(lp // t,),
        in_specs=[_rb(t, w), _rb(t, w)], out_specs=pl.BlockSpec((ATT_HEADS, 1, t), lambda i: (0, 0, i)),
        name="attn_delta", compiler_params=_params(("parallel",)))(datt, att)


def _attn_bwd(qs, ks, vs, datt16, lse2, delta):
    lp = qs.shape[1]
    tk = _att_block(lp)
    tq = ATT_BLOCK_Q_BWD if lp % ATT_BLOCK_Q_BWD == 0 and ATT_BLOCK_Q_BWD % tk == 0 else tk
    r = tq // tk
    nk, nq = lp // tk, lp // tq
    pairs = [(i, j) for j in range(nk) for i in range(j // r, nq)]
    qi = jnp.asarray(np.array([p[0] for p in pairs], np.int32))
    kj = jnp.asarray(np.array([p[1] for p in pairs], np.int32))
    n_pairs = len(pairs)

    def body(qi_ref, kj_ref, k_ref, v_ref, q_ref, do_ref, lse_ref, dl_ref, dq_hbm, dk_ref, dv_ref, dq_s, dk_s, dv_s,
             sem):
        h, p = pl.program_id(0), pl.program_id(1)
        i, j = qi_ref[p], kj_ref[p]
        first = i == j // r

        @pl.when(p == 0)
        def _():
            dq_s[...] = jnp.zeros_like(dq_s)

        @pl.when(first)
        def _():
            dk_s[...] = jnp.zeros_like(dk_s)
            dv_s[...] = jnp.zeros_like(dv_s)

        def step(masked, skip=0):
            q0 = skip * tk
            nrows = tq - q0
            q = q_ref[0, q0:, :]
            do = do_ref[q0:, :]
            pt = jnp.exp2(_nt(k_ref[0], q) * ATT_SCALE_LOG2 - lse_ref[0, :, q0:])
            if masked:
                keep = (lax.broadcasted_iota(jnp.int32, (tk, nrows), 1)
                        >= lax.broadcasted_iota(jnp.int32, (tk, nrows), 0))
                pt = jnp.where(keep, pt, 0.0)
            dst = (pt * (_nt(v_ref[0], do) - dl_ref[0, :, q0:]) * ATT_SCALE).astype(BF16)
            dv_s[...] += jnp.dot(pt.astype(BF16), do, preferred_element_type=F32)
            dk_s[...] += jnp.dot(dst, q, preferred_element_type=F32)
            rows = pl.ds(pl.multiple_of(i * tq + q0, tk), nrows)
            dq_s[rows, :] += lax.dot_general(dst, k_ref[0], (((0,), (0,)), ((), ())), preferred_element_type=F32)

        @pl.when(jnp.logical_not(first))
        def _():
            step(False)

        for sub in range(r):
            @pl.when(jnp.logical_and(first, j % r == sub))
            def _(sub=sub):
                step(True, sub)

        @pl.when(i == nq - 1)
        def _():
            dk_ref[0] = dk_s[...]
            dv_ref[0] = dv_s[...]

        @pl.when(p == n_pairs - 1)
        def _():
            out = pltpu.make_async_copy(dq_s, dq_hbm.at[h], sem)
            out.start()
            out.wait()

    grid_spec = pltpu.PrefetchScalarGridSpec(
        num_scalar_prefetch=2, grid=(ATT_HEADS, n_pairs),
        in_specs=[pl.BlockSpec((1, tk, QP_W), lambda h, p, qi, kj: (h, kj[p], 0)),
                  pl.BlockSpec((1, tk, V_HEAD), lambda h, p, qi, kj: (h, kj[p], 0)),
                  pl.BlockSpec((1, tq, QP_W), lambda h, p, qi, kj: (h, qi[p], 0)),
                  pl.BlockSpec((tq, V_HEAD), lambda h, p, qi, kj: (qi[p], h)),
                  pl.BlockSpec((1, 1, tq), lambda h, p, qi, kj: (h, 0, qi[p])),
                  pl.BlockSpec((1, 1, tq), lambda h, p, qi, kj: (h, 0, qi[p]))],
        out_specs=[pl.BlockSpec(memory_space=pl.ANY),
                   pl.BlockSpec((1, tk, QP_W), lambda h, p, qi, kj: (h, kj[p], 0)),
                   pl.BlockSpec((1, tk, V_HEAD), lambda h, p, qi, kj: (h, kj[p], 0))],
        scratch_shapes=[pltpu.VMEM((lp, QP_W), F32), pltpu.VMEM((tk, QP_W), F32), pltpu.VMEM((tk, V_HEAD), F32),
                        pltpu.SemaphoreType.DMA])
    return pl.pallas_call(
        body, grid_spec=grid_spec,
        out_shape=[jax.ShapeDtypeStruct((ATT_HEADS, lp, QP_W), F32), jax.ShapeDtypeStruct((ATT_HEADS, lp, QP_W), F32),
                   jax.ShapeDtypeStruct((ATT_HEADS, lp, V_HEAD), F32)],
        name="attn_bwd", compiler_params=_params(("arbitrary", "arbitrary")))(
            qi, kj, ks, vs, qs, datt16, lse2, delta)


N_PAIRS = SSM_HEADS // 2
HI = lax.Precision.HIGHEST


def _ssd_chunk(xp, bs, cs, dt, dt_t, alr, alc, dsk, st):
    q = dt.shape[0]
    li = lax.broadcasted_iota(jnp.int32, (q, q), 0)
    si = lax.broadcasted_iota(jnp.int32, (q, q), 1)
    tri = (si <= li).astype(F32)
    tri_t = (li <= si).astype(F32)
    lo = lax.broadcasted_iota(jnp.int32, (1, 128), 1) < 64
    h_r = lax.broadcasted_iota(jnp.int32, (1, SSM_HEADS), 1)
    h_c = lax.broadcasted_iota(jnp.int32, (SSM_HEADS, 1), 0)
    a = dt * (-jnp.exp(alr))
    a_t = dt_t * (-jnp.exp(alc))
    acum = jnp.dot(tri, a, precision=HI, preferred_element_type=F32)
    acum_t = jnp.dot(a_t, tri_t, precision=HI, preferred_element_type=F32)
    last = (lax.broadcasted_iota(jnp.int32, (q, 1), 0) == q - 1).astype(F32)
    alast = jnp.sum(acum * last, axis=0, keepdims=True)
    e = jnp.exp(acum)
    rdt = jnp.exp(alast - acum) * dt
    e_last = jnp.exp(alast)

    def col(m, h):
        return jnp.sum(m * (h_r == h).astype(F32), axis=1, keepdims=True)

    def row(m, h):
        return jnp.sum(m * (h_c == h).astype(F32), axis=0, keepdims=True)

    def pair(m, ha):
        return jnp.where(lo, col(m, ha), col(m, ha + 1))

    ys, st_new = [], []
    for g in range(2):
        c_b = cs[g].astype(BF16)
        b_b = bs[g].astype(BF16)
        cb = _nt(c_b, b_b)
        for j in range(N_PAIRS // 2):
            p = (N_PAIRS // 2) * g + j
            ha = 2 * p
            x = xp[p]
            x_b = x.astype(BF16)

            def w_of(h):
                seg = col(acum, h) - row(acum_t, h)
                return (cb * jnp.exp(jnp.minimum(seg, 0.0)) * tri * row(dt_t, h)).astype(BF16)

            y_diag = jnp.where(lo, jnp.dot(w_of(ha), x_b, preferred_element_type=F32),
                               jnp.dot(w_of(ha + 1), x_b, preferred_element_type=F32))
            y_off = jnp.dot(c_b, st[p].astype(BF16), preferred_element_type=F32) * pair(e, ha)
            ys.append(y_diag + y_off + pair(dsk, ha) * x)
            xw = (x * pair(rdt, ha)).astype(BF16)
            st_new.append(st[p] * pair(e_last, ha)
                          + lax.dot_general(b_b, xw, (((0,), (0,)), ((), ())), preferred_element_type=F32))
    return ys, st_new


def _ssd_fwd(xs, bc, dt, dt_t, alr, alc, dsk):
    lp = xs.shape[0]
    q = SSD_CHUNK
    nc = lp // q

    def body(x_ref, b_ref, c_ref, dt_ref, dtt_ref, alr_ref, alc_ref, dsk_ref, y_ref, sp_ref, st_s):
        @pl.when(pl.program_id(0) == 0)
        def _():
            st_s[...] = jnp.zeros_like(st_s)

        sp_ref[0] = st_s[...]
        xp = [x_ref[:, 128 * p:128 * (p + 1)] for p in range(N_PAIRS)]
        bs = [b_ref[:, 0:128], b_ref[:, 128:256]]
        cs = [c_ref[:, 0:128], c_ref[:, 128:256]]
        ys, st_new = _ssd_chunk(xp, bs, cs, dt_ref[...], dtt_ref[...], alr_ref[...], alc_ref[...], dsk_ref[...],
                                [st_s[p] for p in range(N_PAIRS)])
        for p in range(N_PAIRS):
            y_ref[:, 128 * p:128 * (p + 1)] = ys[p]
            st_s[p] = st_new[p]

    return pl.pallas_call(
        body,
        out_shape=[jax.ShapeDtypeStruct((lp, SSM_WIDTH), F32), jax.ShapeDtypeStruct((nc, N_PAIRS, 128, 128), F32)],
        grid=(nc,),
        in_specs=[_rb(q, SSM_WIDTH), _rb(q, 256, 0), _rb(q, 256, 1), _rb(q, SSM_HEADS),
                  pl.BlockSpec((SSM_HEADS, q), lambda i: (0, i)),
                  _full((1, SSM_HEADS)), _full((SSM_HEADS, 1)), _full((1, SSM_HEADS))],
        out_specs=[_rb(q, SSM_WIDTH), pl.BlockSpec((1, N_PAIRS, 128, 128), lambda i: (i, 0, 0, 0))],
        scratch_shapes=[pltpu.VMEM((N_PAIRS, 128, 128), F32)],
        name="ssd_fwd", compiler_params=_params(("arbitrary",)))(xs, bc, bc, dt, dt_t, alr, alc, dsk)


def _ssd_bwd(xs, bc, dt, dt_t, alr, alc, dsk, sprev, dy):
    lp = xs.shape[0]
    q = SSD_CHUNK
    nc = lp // q

    def body(x_ref, b_ref, c_ref, dt_ref, dtt_ref, alr_ref, alc_ref, dsk_ref, sp_ref, dy_ref,
             dx_ref, dbc_ref, ddt_ref, ddtt_ref, dalr_ref, dalc_ref, ddsk_ref, ds_s):
        @pl.when(pl.program_id(0) == 0)
        def _():
            ds_s[...] = jnp.zeros_like(ds_s)

        xp = [x_ref[:, 128 * p:128 * (p + 1)] for p in range(N_PAIRS)]
        bs = [b_ref[:, 0:128], b_ref[:, 128:256]]
        cs = [c_ref[:, 0:128], c_ref[:, 128:256]]
        st = [sp_ref[0, p] for p in range(N_PAIRS)]
        _, vjp = jax.vjp(_ssd_chunk, xp, bs, cs, dt_ref[...], dtt_ref[...], alr_ref[...], alc_ref[...], dsk_ref[...],
                         st)
        dys = [dy_ref[:, 128 * p:128 * (p + 1)] for p in range(N_PAIRS)]
        dxp, dbs, dcs, ddt, ddtt, dalr, dalc, ddsk, dst = vjp((dys, [ds_s[p] for p in range(N_PAIRS)]))
        for p in range(N_PAIRS):
            dx_ref[:, 128 * p:128 * (p + 1)] = dxp[p]
            ds_s[p] = dst[p]
        for g in range(2):
            dbc_ref[:, 128 * g:128 * (g + 1)] = dbs[g]
            dbc_ref[:, 256 + 128 * g:256 + 128 * (g + 1)] = dcs[g]
        ddt_ref[...] = ddt
        ddtt_ref[...] = ddtt
        _acc_add(dalr_ref, dalr)
        _acc_add(dalc_ref, dalc)
        _acc_add(ddsk_ref, ddsk)

    rev = lambda width, cb=0: pl.BlockSpec((q, width), lambda i: (nc - 1 - i, cb))
    return pl.pallas_call(
        body,
        out_shape=[jax.ShapeDtypeStruct((lp, SSM_WIDTH), F32), jax.ShapeDtypeStruct((lp, 512), F32),
                   jax.ShapeDtypeStruct((lp, SSM_HEADS), F32), jax.ShapeDtypeStruct((SSM_HEADS, lp), F32),
                   jax.ShapeDtypeStruct((1, SSM_HEADS), F32), jax.ShapeDtypeStruct((SSM_HEADS, 1), F32),
                   jax.ShapeDtypeStruct((1, SSM_HEADS), F32)],
        grid=(nc,),
        in_specs=[rev(SSM_WIDTH), rev(256, 0), rev(256, 1), rev(SSM_HEADS),
                  pl.BlockSpec((SSM_HEADS, q), lambda i: (0, nc - 1 - i)),
                  _full((1, SSM_HEADS)), _full((SSM_HEADS, 1)), _full((1, SSM_HEADS)),
                  pl.BlockSpec((1, N_PAIRS, 128, 128), lambda i: (nc - 1 - i, 0, 0, 0)), rev(SSM_WIDTH)],
        out_specs=[rev(SSM_WIDTH), rev(512), rev(SSM_HEADS), pl.BlockSpec((SSM_HEADS, q), lambda i: (0, nc - 1 - i)),
                   _full((1, SSM_HEADS)), _full((SSM_HEADS, 1)), _full((1, SSM_HEADS))],
        scratch_shapes=[pltpu.VMEM((N_PAIRS, 128, 128), F32)],
        name="ssd_bwd", compiler_params=_params(("arbitrary",)))(xs, bc, bc, dt, dt_t, alr, alc, dsk, sprev, dy)


def _q_to_slab_order(w):
    hd = QK_NOPE + QK_ROPE
    nope = [w[:, h * hd:h * hd + QK_NOPE] for h in range(ATT_HEADS)]
    rope = [w[:, h * hd + QK_NOPE:(h + 1) * hd] for h in range(ATT_HEADS)]
    return jnp.concatenate(nope + rope, axis=1)


def _q_from_slab_order(wp):
    base = ATT_HEADS * QK_NOPE
    parts = []
    for h in range(ATT_HEADS):
        parts += [wp[:, QK_NOPE * h:QK_NOPE * (h + 1)], wp[:, base + QK_ROPE * h:base + QK_ROPE * (h + 1)]]
    return jnp.concatenate(parts, axis=1)


_IN_CQ, _IN_CKV, _IN_KR, _IN_Z, _IN_XS, _IN_BC, _IN_DT = (0, 384), (384, 640), (640, 704), (704, 1728), (1728, 2752), \
    (2752, 3264), (3264, 3280)


def _pack_w_in(w):
    z = lambda n: jnp.zeros((w.shape[0], n), w.dtype)
    s = lambda r: w[:, r[0]:r[1]]
    return jnp.concatenate([s(_IN_CKV), s(_IN_KR), z(64), s(_IN_CQ), s(_IN_DT), z(112), z(128), s(_IN_Z), s(_IN_XS),
                            s(_IN_BC)], axis=1)


def _unpack_w_in(wp):
    s = lambda off, n: wp[:, off:off + n]
    return jnp.concatenate([s(SEG_CQ, 384), s(SEG_KV, 256), s(SEG_KR, 64), s(SEG_Z, 1024), s(SEG_XS, 1024),
                            s(SEG_BC, 512), s(SEG_DT, 16)], axis=1)


def _rope_tables(lp):
    inv_freq = ROPE_THETA ** (-jnp.arange(0, QK_ROPE, 2, dtype=F32) / QK_ROPE)
    ang = jnp.arange(lp, dtype=F32)[:, None] * inv_freq[None, :]
    cos, sin = jnp.cos(ang), jnp.sin(ang)
    return jnp.tile(cos, (1, 4)), jnp.concatenate([-sin, sin, -sin, sin], axis=1)


def _local_step(x, tgt, w):
    n_real = x.shape[0]
    l = N_META + n_real
    lp = -(-l // MM_BLOCK) * MM_BLOCK
    h0 = lax.optimization_barrier(jnp.concatenate([w["meta_tokens"], x, jnp.zeros((lp - l, D_MODEL), F32)], axis=0))
    cos_t, sin_t = _rope_tables(lp)

    w_in_p = _pack_w_in(w["w_in"])
    w_q_p = _q_to_slab_order(w["w_q_up"])
    w_kv = w["w_kv_up"]
    if "token" in w:
        cos_t = cos_t + w["token"][0, 0]
    conv_w, conv_b = w["conv_w"], w["conv_b"]
    dt_bias_pad = jnp.concatenate([w["dt_bias"], jnp.zeros((1, 128 - SSM_HEADS), F32)], axis=1)
    alr, dsk = w["a_log"], w["d_skip"]
    alc = alr.reshape(SSM_HEADS, 1)

    n1 = _norm_in(h0, w["norm_mix_pre"])
    proj = _mm("proj", n1, w_in_p)
    cqn, ckvn, krr = _attn_prep(proj, w["q_a_norm"], w["kv_a_norm"], cos_t, sin_t)
    q = _mm("q_up", cqn, w_q_p)
    kv = _mm("kv_up", ckvn, w_kv, outs=((BF16, None),))
    qs, ks, vs, vts = _qk_pack(q, kv, krr, cos_t, sin_t)
    att, att16, lse2 = _attn_fwd(qs, ks, vts)
    xs_pre, xs_act = _conv_fwd("conv_xs_fwd", proj, SEG_XS, SSM_WIDTH, conv_w[:, :SSM_WIDTH], conv_b[:, :SSM_WIDTH])
    bc_pre, bc_act = _conv_fwd("conv_bc_fwd", proj, SEG_BC, 512, conv_w[:, SSM_WIDTH:], conv_b[:, SSM_WIDTH:])
    dt = _dt_fwd(proj, dt_bias_pad)[:, :SSM_HEADS]
    dt_t = dt.T
    y, sprev = _ssd_fwd(xs_act, bc_act, dt, dt_t, alr, alc, dsk)
    ssm = _gated_norm_fwd(y, proj, w["ssm_norm"])
    late = w["late_weights"](ssm) if "late_weights" in w else w
    w_out, w_up, w_down = late["w_out"], late["w_mlp_up"], late["w_mlp_down"]
    mix = _mm("out_proj", (att16, ssm), w_out)
    h1, n2 = _mix_residual(h0, mix, w["norm_mix_post"], w["norm_mlp_pre"])
    relu2 = lambda r: jnp.square(jnp.maximum(r, 0.0))
    act = _mm("mlp_up", n2, w_up, outs=((BF16, relu2),))
    f = _mm("mlp_down", act, w_down)
    loss, dh2, df, dg_mlp_post = _loss_and_grad(h1, f, w["norm_mlp_post"], tgt, n_real)

    g = {"norm_mlp_post": dg_mlp_post}
    g["w_mlp_down"] = _mm_tn("d_w_mlp_down", act, df)
    du = _mm("d_mlp_act", df, w_down, outs=((BF16, lambda r, ab: r * (2.0 * jnp.sqrt(ab.astype(F32)))),),
             epi_ins=(act,), trans_b=True)
    g["w_mlp_up"] = _mm_tn("d_w_mlp_up", n2, du)
    dn2 = _mm("d_n2", du, w_up, trans_b=True)
    dh1, dmix, g["norm_mlp_pre"], g["norm_mix_post"] = _mlp_residual_bwd(dh2, dn2, h1, w["norm_mlp_pre"], mix,
                                                                         w["norm_mix_post"])
    g["w_out"] = jnp.concatenate([_mm_tn("d_w_out_att", att16, dmix), _mm_tn("d_w_out_ssm", ssm, dmix)], axis=0)
    dcat, dcat16 = _mm("d_cat", dmix, w_out, outs=((F32, None), (BF16, None)), trans_b=True)
    dy, dz, g["ssm_norm"] = _gated_norm_bwd(y, proj, w["ssm_norm"], dcat)
    dxs_act, dbc_act, ddt, ddt_t, dalr, dalc, g["d_skip"] = _ssd_bwd(xs_act, bc_act, dt, dt_t, alr, alc, dsk, sprev, dy)
    g["a_log"] = dalr + dalc.reshape(1, SSM_HEADS)
    dxs, dcw_xs, dcb_xs = _conv_bwd("conv_xs_bwd", dxs_act, xs_pre, proj, SEG_XS, SSM_WIDTH, conv_w[:, :SSM_WIDTH])
    dbc, dcw_bc, dcb_bc = _conv_bwd("conv_bc_bwd", dbc_act, bc_pre, proj, SEG_BC, 512, conv_w[:, SSM_WIDTH:])
    g["conv_w"] = jnp.concatenate([dcw_xs[:CONV_K], dcw_bc[:CONV_K]], axis=1)
    g["conv_b"] = jnp.concatenate([dcb_xs, dcb_bc], axis=1)
    ddt_pad = jnp.concatenate([ddt + ddt_t.T, jnp.zeros((lp, 128 - SSM_HEADS), F32)], axis=1)

    dqs, dks, dvs = _attn_bwd(qs, ks, vs, dcat16, lse2, _attn_delta(dcat, att))
    dq, dkv, dkr = _qk_unpack_bwd(dqs, dks, dvs, cos_t, sin_t)
    g["w_q_up"] = _q_from_slab_order(_mm_tn("d_w_q_up", cqn, dq))
    g["w_kv_up"] = _mm_tn("d_w_kv_up", ckvn, dkv)
    dcqn = _mm("d_cqn", dq, w_q_p, trans_b=True)
    dckvn = _mm("d_ckvn", dkv, w_kv, trans_b=True)
    dproj, g["q_a_norm"], g["kv_a_norm"], ddtb = _proj_grad(proj, dcqn, dckvn, w["q_a_norm"], w["kv_a_norm"], dkr,
                                                          ddt_pad, dt_bias_pad, dz, dxs, dbc)
    g["dt_bias"] = ddtb[:, :SSM_HEADS]
    g["w_in"] = _unpack_w_in(_mm_tn("d_w_in", n1, dproj))
    dn1 = _mm("d_n1", dproj, w_in_p, trans_b=True)
    dh0, g["norm_mix_pre"] = _input_norm_bwd(dh1, dn1, h0, w["norm_mix_pre"])
    g["meta_tokens"] = dh0[:N_META]
    return loss, dh0, g


WEIGHTS = ["meta_tokens", "norm_mix_pre", "w_in", "q_a_norm", "w_q_up", "kv_a_norm", "w_kv_up", "conv_w", "conv_b",
           "dt_bias", "a_log", "d_skip", "ssm_norm", "w_out", "norm_mix_post", "norm_mlp_pre", "w_mlp_up",
           "w_mlp_down", "norm_mlp_post"]
SHARD_AXIS = {"meta_tokens": 1, "w_in": 1, "w_q_up": 1, "w_kv_up": 1, "conv_w": 1, "w_out": 0, "w_mlp_up": 1,
              "w_mlp_down": 0}
FULL_SHAPE = {"meta_tokens": (16, 1024), "norm_mix_pre": (1, 1024), "w_in": (1024, 3280), "q_a_norm": (1, 384),
              "w_q_up": (384, 1536), "kv_a_norm": (1, 256), "w_kv_up": (256, 2048), "conv_w": (4, 1536),
              "conv_b": (1, 1536), "dt_bias": (1, 16), "a_log": (1, 16), "d_skip": (1, 16), "ssm_norm": (1, 1024),
              "w_out": (2048, 1024), "norm_mix_post": (1, 1024), "norm_mlp_pre": (1, 1024), "w_mlp_up": (1024, 4096),
              "w_mlp_down": (4096, 1024), "norm_mlp_post": (1, 1024)}
GATHER_BF16 = ["w_in", "w_q_up", "w_kv_up", "w_out", "w_mlp_up", "w_mlp_down"]
GATHER_F32 = ["meta_tokens", "conv_w"]
GATHER_EARLY = ["w_in", "w_q_up", "w_kv_up"]
GATHER_LATE = ["w_out", "w_mlp_up", "w_mlp_down"]
ADAM_NATURAL = GATHER_BF16


def _shard_shape(name):
    shp = list(FULL_SHAPE[name])
    if name in SHARD_AXIS:
        shp[SHARD_AXIS[name]] //= N_CHIPS
    return tuple(shp)


PACK_ORDER = sorted(WEIGHTS, key=lambda n: -_shard_shape(n)[0])


PART_ROWS = 8


def _data_rows(shape):
    r, c = shape
    return r if c <= PACK_W else -(-c // PACK_W)


def _packed_rows(shape):
    return -(-_data_rows(shape) // PART_ROWS) * PART_ROWS


def _fill_rows(shapes, row_multiple):
    return -sum(_packed_rows(s) for s in shapes) % row_multiple


def _pack_rows(arrays, row_multiple, fill_before=None):
    assert arrays[0].dtype.itemsize == 4 or all(a.shape[0] % 16 == 0 for a in arrays)
    parts = []
    for a in arrays:
        r, c = a.shape
        if c > PACK_W:
            assert r == 1, a.shape
            r = _data_rows(a.shape)
            a = jnp.pad(a, ((0, 0), (0, r * PACK_W - c))).reshape(r, PACK_W)
            c = PACK_W
        parts.append(jnp.pad(a, ((0, _packed_rows((r, c)) - r), (0, PACK_W - c))))
    fill = _fill_rows([a.shape for a in arrays], row_multiple)
    if fill:
        at = len(parts) if fill_before is None else fill_before
        parts.insert(at, jnp.zeros((fill, PACK_W), parts[0].dtype))
    return jnp.concatenate(parts, axis=0)


def _unpack_rows(packed, shapes, row_multiple=1, fill_before=None):
    out, off = [], 0
    for k, (r, c) in enumerate(shapes):
        if k == fill_before:
            off += _fill_rows(shapes, row_multiple)
        nr = _data_rows((r, c))
        blk = packed[off:off + nr]
        out.append(blk[:, :c] if c <= PACK_W else blk.reshape(1, nr * PACK_W)[:, :c])
        off += _packed_rows((r, c))
    return out


def _chip_slice(full, name, t):
    if name not in SHARD_AXIS:
        return full
    ax = SHARD_AXIS[name]
    n = FULL_SHAPE[name][ax] // N_CHIPS
    return lax.slice_in_dim(full, t * n, (t + 1) * n, axis=ax)


HBM_SPEC = pl.BlockSpec(memory_space=pl.ANY)
CHIP_FLIPS = ((1, 0), (0, 1), (1, 1))


def _gather_chips(bufs):
    nb = len(bufs)

    def body(*refs):
        ins, outs = refs[:nb], refs[nb:2 * nb]
        send, recv, loc = refs[2 * nb:]
        x, y, c = lax.axis_index("x"), lax.axis_index("y"), lax.axis_index("c")
        me = 2 * x + y
        sibling = (x, y, 1 - c)
        sends, forwards = [], []
        for b in range(nb):
            half = bufs[b].shape[0] // 2
            mine = pl.ds(c * half, half)
            own = pltpu.make_async_copy(ins[b], outs[b].at[me], loc.at[b])
            own.start()
            sends.append(own)
            for k, (fx, fy) in enumerate(CHIP_FLIPS):
                cp = pltpu.make_async_remote_copy(
                    src_ref=ins[b].at[mine], dst_ref=outs[b].at[me, mine], send_sem=send.at[b, k],
                    recv_sem=recv.at[b, k], device_id=(x ^ fx, y ^ fy, c), device_id_type=MESH_ID)
                cp.start()
                sends.append(cp)
        for b in range(nb):
            half = bufs[b].shape[0] // 2
            mine, theirs = pl.ds(c * half, half), pl.ds((1 - c) * half, half)
            for k, (fx, fy) in enumerate(CHIP_FLIPS):
                chip = 2 * (x ^ fx) + (y ^ fy)
                landed = outs[b].at[chip, mine]
                pltpu.make_async_remote_copy(src_ref=landed, dst_ref=landed, send_sem=send.at[b, k],
                                             recv_sem=recv.at[b, k], device_id=sibling,
                                             device_id_type=MESH_ID).wait_recv()
                fw = pltpu.make_async_remote_copy(src_ref=landed, dst_ref=landed, send_sem=send.at[b, 3 + k],
                                                  recv_sem=recv.at[b, 3 + k], device_id=sibling,
                                                  device_id_type=MESH_ID)
                fw.start()
                forwards.append((fw, outs[b].at[chip, theirs], b, k))
        for fw, arriving, b, k in forwards:
            pltpu.make_async_remote_copy(src_ref=arriving, dst_ref=arriving, send_sem=send.at[b, 3 + k],
                                         recv_sem=recv.at[b, 3 + k], device_id=sibling,
                                         device_id_type=MESH_ID).wait_recv()
            fw.wait_send()
        for cp in sends[1::4] + sends[2::4] + sends[3::4]:
            cp.wait_send()
        for own in sends[0::4]:
            own.wait()

    return pl.pallas_call(
        body,
        out_shape=[jax.ShapeDtypeStruct((N_CHIPS,) + b.shape, b.dtype) for b in bufs],
        in_specs=[HBM_SPEC] * nb, out_specs=[HBM_SPEC] * nb,
        scratch_shapes=[pltpu.SemaphoreType.DMA((nb, 6)), pltpu.SemaphoreType.DMA((nb, 6)),
                        pltpu.SemaphoreType.DMA((nb,))],
        name="gather_chips")(*bufs)


HBM_ONLY = pl.BlockSpec(memory_space=pltpu.HBM)
SEM_SPEC = pl.BlockSpec(memory_space=pltpu.SEMAPHORE)
SPLIT_COPY_EFFECT = pltpu.SideEffectType.DATAFLOW_SIDE_EFFECTING


def _late_copies(src_ref, land_ref, sends, recvs):
    x, y, c = lax.axis_index("x"), lax.axis_index("y"), lax.axis_index("c")
    me = 2 * x + y
    return [pltpu.make_async_remote_copy(src_ref=src_ref, dst_ref=land_ref.at[me], send_sem=sends[k], recv_sem=recvs[k],
                                         device_id=(x ^ fx, y ^ fy, c), device_id_type=MESH_ID)
            for k, (fx, fy) in enumerate(CHIP_FLIPS)]


def _gather_late_start(buf):
    def body(src_ref, land_ref, s0, s1, s2, r0, r1, r2, src_thru, land_thru, token):
        for cp in _late_copies(src_ref, land_ref, (s0, s1, s2), (r0, r1, r2)):
            cp.start()
        token[...] = jnp.zeros_like(token)

    land = lax.empty((N_CHIPS,) + buf.shape, buf.dtype)
    sem = pltpu.SemaphoreType.DMA(())
    res = pl.pallas_call(
        body, name="gather_late_start",
        out_shape=(sem,) * 6 + (pltpu.HBM(buf.shape, buf.dtype), pltpu.HBM(land.shape, land.dtype),
                                jax.ShapeDtypeStruct((8, 128), F32)),
        in_specs=(HBM_ONLY, HBM_ONLY),
        out_specs=(SEM_SPEC,) * 6 + (HBM_ONLY, HBM_ONLY, pl.BlockSpec(memory_space=pltpu.VMEM)),
        input_output_aliases={0: 6, 1: 7},
        compiler_params=pltpu.CompilerParams(has_side_effects=SPLIT_COPY_EFFECT),
    )(pltpu.with_memory_space_constraint(buf, pltpu.HBM), pltpu.with_memory_space_constraint(land, pltpu.HBM))
    return res[:6], res[6], res[7], res[8]


def _gather_late_wait(sems, src_thru, land_thru, after):
    def body(src_ref, land_ref, s0, s1, s2, r0, r1, r2, after_ref, src_dead, got_ref):
        for cp in _late_copies(src_ref, land_ref, (s0, s1, s2), (r0, r1, r2)):
            cp.wait_send()
            cp.wait_recv()

    return pl.pallas_call(
        body, name="gather_late_wait",
        out_shape=(pltpu.HBM(src_thru.shape, src_thru.dtype), pltpu.HBM(land_thru.shape, land_thru.dtype)),
        in_specs=(HBM_ONLY, HBM_ONLY) + (SEM_SPEC,) * 6 + (pl.BlockSpec(memory_space=pl.ANY),),
        out_specs=(HBM_ONLY, HBM_ONLY), input_output_aliases={0: 0, 1: 1},
        compiler_params=pltpu.CompilerParams(has_side_effects=SPLIT_COPY_EFFECT),
    )(src_thru, land_thru, *sems, after)[1]


def _sibling_swap(name, buf):
    def body(src, dst, send, recv):
        x, y, c = lax.axis_index("x"), lax.axis_index("y"), lax.axis_index("c")
        cp = pltpu.make_async_remote_copy(src_ref=src, dst_ref=dst, send_sem=send, recv_sem=recv,
                                          device_id=(x, y, 1 - c), device_id_type=MESH_ID)
        cp.start()
        cp.wait()

    return pl.pallas_call(
        body, out_shape=jax.ShapeDtypeStruct(buf.shape, buf.dtype), in_specs=[HBM_SPEC], out_specs=HBM_SPEC,
        scratch_shapes=[pltpu.SemaphoreType.DMA, pltpu.SemaphoreType.DMA], name=name)(buf)


def _scatter_chips(parts):
    nb = len(parts)

    def body(*refs):
        srcs, dsts = refs[:nb], refs[nb:2 * nb]
        send, recv = refs[2 * nb:]
        x, y, c = lax.axis_index("x"), lax.axis_index("y"), lax.axis_index("c")
        copies = []
        for b in range(nb):
            for k, (fx, fy) in enumerate(CHIP_FLIPS):
                tx, ty = x ^ fx, y ^ fy
                cp = pltpu.make_async_remote_copy(
                    src_ref=srcs[b].at[2 * tx + ty], dst_ref=dsts[b].at[k], send_sem=send.at[b, k],
                    recv_sem=recv.at[b, k], device_id=(tx, ty, c), device_id_type=MESH_ID)
                cp.start()
                copies.append(cp)
        for cp in copies:
            cp.wait()

    return pl.pallas_call(
        body, out_shape=[jax.ShapeDtypeStruct((3,) + p.shape[1:], p.dtype) for p in parts], in_specs=[HBM_SPEC] * nb,
        out_specs=[HBM_SPEC] * nb,
        scratch_shapes=[pltpu.SemaphoreType.DMA((nb, 3)), pltpu.SemaphoreType.DMA((nb, 3))],
        name="scatter_chips")(*parts)


def _add_rows(name, terms, also_bf16=False):
    rows = terms[0].shape[0]
    t = _row_tile(rows)
    n_out = 2 if also_bf16 else 1

    def body(*refs):
        acc = refs[0][...].astype(F32)
        for r in refs[1:-n_out]:
            acc = acc + r[...].astype(F32)
        refs[-n_out][...] = acc
        if also_bf16:
            refs[-1][...] = acc.astype(BF16)

    res = pl.pallas_call(
        body, out_shape=[jax.ShapeDtypeStruct(terms[0].shape, dt) for dt in (F32, BF16)[:n_out]], grid=(rows // t,),
        in_specs=[_rb(t, PACK_W)] * len(terms), out_specs=[_rb(t, PACK_W)] * n_out,
        name=name, compiler_params=_params(("parallel",)))(*terms)
    return res if also_bf16 else res[0]


def _sum_chip_order(name, parts, received, me):
    stack = jnp.concatenate([lax.dynamic_index_in_dim(parts, me, axis=0, keepdims=True), received], axis=0)
    terms = []
    for chip in range(N_CHIPS):
        xr = me ^ chip
        where = jnp.where(xr == 0, 0, jnp.where(xr == 2, 1, jnp.where(xr == 1, 2, 3)))
        terms.append(lax.dynamic_index_in_dim(stack, where, axis=0, keepdims=False))
    return _add_rows(name, terms)


def _row_tile(rows):
    assert rows % 16 == 0, rows
    return max(t for t in range(16, 513, 16) if rows % t == 0)


def _adamw(name, g, w, m, v):
    rows, cols = g.shape
    t = max(tt for tt in range(8, 257, 8) if rows % tt == 0)
    c1 = 1.0 - ADAM_B1 ** ADAM_STEP
    c2 = 1.0 - ADAM_B2 ** ADAM_STEP

    def body(g_ref, w_ref, m_ref, v_ref, d_ref, mo_ref, vo_ref):
        gg = g_ref[...]
        mn = ADAM_B1 * m_ref[...] + (1.0 - ADAM_B1) * gg
        vn = ADAM_B2 * v_ref[...] + (1.0 - ADAM_B2) * (gg * gg)
        d_ref[...] = -ADAM_LR * ((mn / c1) / (jnp.sqrt(vn / c2) + ADAM_EPS) + ADAM_WD * w_ref[...])
        mo_ref[...] = mn
        vo_ref[...] = vn

    return pl.pallas_call(
        body, out_shape=[jax.ShapeDtypeStruct(g.shape, F32)] * 3, grid=(rows // t,),
        in_specs=[_rb(t, cols)] * 4, out_specs=[_rb(t, cols)] * 3,
        name=name, compiler_params=_params(("parallel",)))(g, w, m, v)


def kernel(x, meta_tokens, norm_mix_pre, w_in, q_a_norm, w_q_up, kv_a_norm, w_kv_up, conv_w, conv_b, dt_bias, a_log, d_skip, ssm_norm, w_out, norm_mix_post, norm_mlp_pre, w_mlp_up, w_mlp_down, norm_mlp_post, loss_target, m_meta_tokens, m_norm_mix_pre, m_w_in, m_q_a_norm, m_w_q_up, m_kv_a_norm, m_w_kv_up, m_conv_w, m_conv_b, m_dt_bias, m_a_log, m_d_skip, m_ssm_norm, m_w_out, m_norm_mix_post, m_norm_mlp_pre, m_w_mlp_up, m_w_mlp_down, m_norm_mlp_post, v_meta_tokens, v_norm_mix_pre, v_w_in, v_q_a_norm, v_w_q_up, v_kv_a_norm, v_w_kv_up, v_conv_w, v_conv_b, v_dt_bias, v_a_log, v_d_skip, v_ssm_norm, v_w_out, v_norm_mix_post, v_norm_mlp_pre, v_w_mlp_up, v_w_mlp_down, v_norm_mlp_post):
    given = dict(locals())
    drop = lambda name, a: a[0] if a.ndim == 3 else a
    w_loc = {n: drop(n, given[n]) for n in WEIGHTS}
    m_loc = {n: drop(n, given["m_" + n]) for n in WEIGHTS}
    v_loc = {n: drop(n, given["v_" + n]) for n in WEIGHTS}
    ix, iy, ic = lax.axis_index("x"), lax.axis_index("y"), lax.axis_index("c")
    me = 2 * ix + iy

    def assemble(names, got, own_slot=None):
        per_chip = [_unpack_rows(got[t], [_shard_shape(n) for n in names]) for t in range(N_CHIPS)]
        full = {}
        for k, n in enumerate(names):
            pieces = [per_chip[t][k] for t in range(N_CHIPS)]
            if own_slot is not None:
                pieces = [jnp.where(me == t, own_slot[n], pieces[t]) for t in range(N_CHIPS)]
            full[n] = jnp.concatenate(pieces, axis=SHARD_AXIS[n])
        return full

    bf16_of = lambda names: _pack_rows([w_loc[n].astype(BF16) for n in names], 32)
    sems, src_thru, land_thru, token = _gather_late_start(bf16_of(GATHER_LATE))
    got16, got32 = _gather_chips([bf16_of(GATHER_EARLY), _pack_rows([w_loc[n] for n in GATHER_F32], 16)])
    w_full = {n: w_loc[n] for n in WEIGHTS if n not in SHARD_AXIS}
    w_full.update(assemble(GATHER_EARLY, got16))
    w_full.update(assemble(GATHER_F32, got32))
    w_full["token"] = token
    own16 = {n: w_loc[n].astype(BF16) for n in GATHER_LATE}
    w_full["late_weights"] = lambda after: assemble(GATHER_LATE, _gather_late_wait(sems, src_thru, land_thru, after),
                                                    own_slot=own16)

    loss, dh0, g_full = _local_step(x[0], loss_target[0], w_full)
    n_real = x.shape[1]
    grad_x = dh0[N_META:N_META + n_real][None]

    shapes = [_shard_shape(n) for n in PACK_ORDER]
    n_sharded = len(SHARD_AXIS)
    slots = [_pack_rows([_chip_slice(g_full[n], n, t) for n in PACK_ORDER], SLOT_ROWS_MULTIPLE, fill_before=n_sharded)
             for t in range(N_CHIPS)]
    rows = slots[0].shape[0]
    half = rows // 2
    halves = lambda hh: jnp.concatenate([lax.dynamic_slice_in_dim(s, hh * half, half, axis=0) for s in slots], axis=0)
    keep, give = halves(ic), halves(1 - ic)
    from_sibling = _sibling_swap("sibling_swap", give)
    part32, part16 = _add_rows("chip_partial", [keep, from_sibling], also_bf16=True)
    part32 = part32.reshape(N_CHIPS, half, PACK_W)
    part16 = part16.reshape(N_CHIPS, half, PACK_W)
    assert all(n in SHARD_AXIS for n in PACK_ORDER[:n_sharded])
    replicated_rows = sum(_packed_rows(_shard_shape(n)) for n in PACK_ORDER[n_sharded:])
    assert replicated_rows <= TAIL_ROWS <= half
    tail32 = part32[:, half - TAIL_ROWS:, :]
    from16, from_tail = _scatter_chips([part16, tail32])
    my_half = _sum_chip_order("chip_total", part16, from16, me)
    my_tail = _sum_chip_order("chip_total_tail", tail32, from_tail, me)
    my_half = lax.dynamic_update_slice(my_half, my_tail, (half - TAIL_ROWS, 0))
    other_half = _sibling_swap("sibling_gather", my_half)
    g_red = jnp.where(ic == 0, jnp.concatenate([my_half, other_half], axis=0),
                      jnp.concatenate([other_half, my_half], axis=0))

    g_own = dict(zip(PACK_ORDER, _unpack_rows(g_red, shapes, SLOT_ROWS_MULTIPLE, n_sharded)))
    small = [n for n in PACK_ORDER if n not in ADAM_NATURAL]
    pack_small = lambda d: _pack_rows([d[n] for n in small], 16)
    packed_upd = _adamw("adamw_small", pack_small(g_own), pack_small(w_loc), pack_small(m_loc), pack_small(v_loc))
    upd = [dict(zip(small, _unpack_rows(p, [_shard_shape(n) for n in small]))) for p in packed_upd]
    for n in ADAM_NATURAL:
        for k, res in enumerate(_adamw("adamw_" + n, g_own[n], w_loc[n], m_loc[n], v_loc[n])):
            upd[k][n] = res

    def outputs(parts):
        return [parts[n][None] if given[n].ndim == 3 else parts[n] for n in WEIGHTS]

    total = lax.psum(loss[0, 0], ("x", "y", "c"))
    return (total, grad_x, *outputs(g_own), *outputs(upd[0]), *outputs(upd[1]), *outputs(upd[2]))
```

```python
import functools

import numpy as np
import jax
import jax.numpy as jnp
from jax import lax
from jax.experimental import pallas as pl
from jax.experimental.pallas import tpu as pltpu

F32 = jnp.float32
BF16 = jnp.bfloat16

D_MODEL = 1024
N_META = 16
EPS = 1e-6
ATT_HEADS = 8
Q_LORA = 384
KV_LORA = 256
QK_NOPE = 128
QK_ROPE = 64
V_HEAD = 128
ROPE_THETA = 10000.0
SSM_HEADS = 16
SSM_HEAD_DIM = 64
SSM_WIDTH = 1024
SSM_STATE = 128
CONV_K = 4
D_FF = 4096
ATT_SCALE = float((QK_NOPE + QK_ROPE) ** -0.5)
ATT_SCALE_LOG2 = float(ATT_SCALE * np.log2(np.e))

ADAM_LR = 0.001
ADAM_B1 = 0.9
ADAM_B2 = 0.999
ADAM_EPS = 1e-08
ADAM_WD = 0.01
ADAM_STEP = 10

SEG_KV, SEG_KR, SEG_CQ, SEG_DT, SEG_Z, SEG_XS, SEG_BC = 0, 256, 384, 768, 1024, 2048, 3072
PROJ_W = 3584
QP_W = 256

ROW_BLOCK = 768
LOSS_ROW_BLOCK = 512
MM_BLOCK = 512
SSD_CHUNK = 256
ATT_BLOCK = 768
ATT_BLOCK_FWD = 1536
ATT_BLOCK_Q_BWD = 1536
VMEM_LIMIT = 56 * 1024 * 1024
NEG_BIG = -1e30

PACK_W = 1024
TAIL_ROWS = 96
SLOT_ROWS_MULTIPLE = 256
N_CHIPS = 4
MESH_ID = pl.DeviceIdType.MESH


def _params(sem):
    return pltpu.CompilerParams(dimension_semantics=sem, vmem_limit_bytes=VMEM_LIMIT)


def _rb(rows, width, cb=0):
    return pl.BlockSpec((rows, width), lambda i: (i, cb))


def _full(shape):
    zeros = (0,) * len(shape)
    return pl.BlockSpec(shape, lambda i: zeros)


def _acc_add(ref, val):
    first = pl.program_id(0) == 0

    @pl.when(first)
    def _():
        ref[...] = val

    @pl.when(jnp.logical_not(first))
    def _():
        ref[...] += val


def _rms(x, g):
    r = lax.rsqrt(jnp.mean(x * x, axis=-1, keepdims=True) + EPS)
    return x * r * g


def _rms_bwd(x, g, dy):
    r = lax.rsqrt(jnp.mean(x * x, axis=-1, keepdims=True) + EPS)
    dyg = dy * g
    dx = r * dyg - x * (r * r * r) * jnp.mean(x * dyg, axis=-1, keepdims=True)
    dg = jnp.sum(dy * x * r, axis=0, keepdims=True)
    return dx, dg


def _sigmoid(x):
    return 1.0 / (1.0 + jnp.exp(-x))


def _swap32(x):
    lane = lax.broadcasted_iota(jnp.int32, x.shape, 1)
    return jnp.where((lane % 64) < 32, pltpu.roll(x, 96, 1), pltpu.roll(x, 32, 1))


def _rope(x, cos_t, sin_t):
    return x * cos_t + _swap32(x) * sin_t


def _rope_bwd(dr, cos_t, sin_t):
    return dr * cos_t + _swap32(dr * sin_t)


def _tile(n, cap):
    if n <= cap:
        return n
    best = 128
    for t in range(128, cap + 1, 128):
        if n % t == 0:
            best = t
    assert n % best == 0, (n, cap)
    return best


MM_VMEM_BUDGET = 40 * 1024 * 1024
TN_ROWS_CAP = 1536


def _mm(name, a, b, outs=((F32, None),), epi_ins=(), trans_b=False):
    a_parts = tuple(a) if isinstance(a, (tuple, list)) else (a,)
    assert len(a_parts) == 1 or not trans_b
    m = a_parts[0].shape[0]
    widths = [p.shape[1] for p in a_parts]
    k = sum(widths)
    n = b.shape[0] if trans_b else b.shape[1]
    tm = MM_BLOCK
    n_a, n_epi = len(a_parts), len(epi_ins)
    out_bytes = sum(jnp.dtype(dt).itemsize for dt, _ in outs) + sum(e.dtype.itemsize for e in epi_ins)
    a_bytes = sum(p.shape[1] * p.dtype.itemsize for p in a_parts)
    step_bytes = lambda tn: 2 * (tm * a_bytes + k * tn * b.dtype.itemsize + tm * tn * out_bytes)
    tn = n
    while step_bytes(tn) > MM_VMEM_BUDGET and tn % 256 == 0:
        tn //= 2
    assert n % tn == 0 and step_bytes(tn) <= MM_VMEM_BUDGET, (name, n, tn)

    def body(*refs):
        a_refs, b_ref = refs[:n_a], refs[n_a]
        epi_refs = refs[n_a + 1:n_a + 1 + n_epi]
        out_refs = refs[n_a + 1 + n_epi:]
        if trans_b:
            r = _nt(a_refs[0][...].astype(BF16), b_ref[...].astype(BF16))
        else:
            r, off = None, 0
            for a_ref, wd in zip(a_refs, widths):
                part = jnp.dot(a_ref[...].astype(BF16), b_ref[off:off + wd, :].astype(BF16),
                               preferred_element_type=F32)
                r = part if r is None else r + part
                off += wd
        blocks = [e[...] for e in epi_refs]
        for o_ref, (dt, fn) in zip(out_refs, outs):
            o_ref[...] = (r if fn is None else fn(r, *blocks)).astype(dt)

    out_spec = pl.BlockSpec((tm, tn), lambda j, i: (i, j))
    b_spec = pl.BlockSpec((tn, k), lambda j, i: (j, 0)) if trans_b else pl.BlockSpec((k, tn), lambda j, i: (0, j))
    res = pl.pallas_call(
        body,
        out_shape=[jax.ShapeDtypeStruct((m, n), dt) for dt, _ in outs],
        grid=(n // tn, m // tm),
        in_specs=[pl.BlockSpec((tm, wd), lambda j, i: (i, 0)) for wd in widths] + [b_spec] + [out_spec] * n_epi,
        out_specs=[out_spec] * len(outs),
        name=name,
        compiler_params=_params(("parallel", "parallel")),
    )(*a_parts, b, *epi_ins)
    return res[0] if len(outs) == 1 else res


def _mm_tn(name, x, dy, ta_cap=1024, tn_cap=1024):
    l, a = x.shape
    n = dy.shape[1]
    ta, tn = _tile(a, ta_cap), _tile(n, tn_cap)
    tl = max(t for t in range(MM_BLOCK, TN_ROWS_CAP + 1, MM_BLOCK) if l % t == 0)
    nl = l // tl

    def body(x_ref, dy_ref, o_ref):
        ll = pl.program_id(2)

        @pl.when(ll == 0)
        def _():
            o_ref[...] = jnp.zeros_like(o_ref)

        o_ref[...] += lax.dot_general(x_ref[...].astype(BF16), dy_ref[...].astype(BF16), (((0,), (0,)), ((), ())),
                                      preferred_element_type=F32)

    return pl.pallas_call(
        body,
        out_shape=jax.ShapeDtypeStruct((a, n), F32),
        grid=(a // ta, n // tn, nl),
        in_specs=[pl.BlockSpec((tl, ta), lambda i, j, ll: (ll, i)), pl.BlockSpec((tl, tn), lambda i, j, ll: (ll, j))],
        out_specs=pl.BlockSpec((ta, tn), lambda i, j, ll: (i, j)),
        name=name,
        compiler_params=_params(("parallel", "parallel", "arbitrary")),
    )(x, dy)


def _norm_in(h0, g_pre):
    lp = h0.shape[0]
    t = ROW_BLOCK

    def body(h_ref, g_ref, o_ref):
        o_ref[...] = _rms(h_ref[...], g_ref[...]).astype(BF16)

    return pl.pallas_call(
        body, out_shape=jax.ShapeDtypeStruct((lp, D_MODEL), BF16), grid=(lp // t,),
        in_specs=[_rb(t, D_MODEL), _full((1, D_MODEL))], out_specs=_rb(t, D_MODEL),
        name="norm_in", compiler_params=_params(("parallel",)))(h0, g_pre)


def _attn_prep(proj, g_q, g_kv, cos_t, sin_t):
    lp = proj.shape[0]
    t = ROW_BLOCK

    def body(ckv_ref, kr_ref, cq_ref, gq_ref, gkv_ref, cos_ref, sin_ref, cqn_ref, ckvn_ref, krr_ref):
        cqn_ref[...] = _rms(cq_ref[...], gq_ref[...]).astype(BF16)
        ckvn_ref[...] = _rms(ckv_ref[...], gkv_ref[...]).astype(BF16)
        roped = _rope(kr_ref[...], cos_ref[...], sin_ref[...])
        krr_ref[...] = roped + pltpu.roll(roped, 64, 1)

    return pl.pallas_call(
        body,
        out_shape=[jax.ShapeDtypeStruct((lp, Q_LORA), BF16), jax.ShapeDtypeStruct((lp, KV_LORA), BF16),
                   jax.ShapeDtypeStruct((lp, 128), F32)],
        grid=(lp // t,),
        in_specs=[_rb(t, KV_LORA, SEG_KV // KV_LORA), _rb(t, 128, SEG_KR // 128), _rb(t, Q_LORA, SEG_CQ // Q_LORA),
                  _full((1, Q_LORA)), _full((1, KV_LORA)), _rb(t, 128), _rb(t, 128)],
        out_specs=[_rb(t, Q_LORA), _rb(t, KV_LORA), _rb(t, 128)],
        name="attn_prep", compiler_params=_params(("parallel",)))(proj, proj, proj, g_q, g_kv, cos_t, sin_t)


def _qk_pack(q, kv, krr, cos_t, sin_t):
    lp = q.shape[0]
    t = ROW_BLOCK

    def body(q_ref, kv_ref, krr_ref, cos_ref, sin_ref, qs_ref, ks_ref, vs_ref, vts_ref):
        lane = lax.broadcasted_iota(jnp.int32, (t, 128), 1)
        lo = lane < 64
        krr = krr_ref[...].astype(BF16)
        for j in range(ATT_HEADS // 2):
            pr = _rope(q_ref[:, 1024 + 128 * j:1024 + 128 * (j + 1)], cos_ref[...], sin_ref[...])
            for h, keep in ((2 * j, lo), (2 * j + 1, jnp.logical_not(lo))):
                qs_ref[h, :, 0:128] = q_ref[:, 128 * h:128 * (h + 1)].astype(BF16)
                qs_ref[h, :, 128:256] = jnp.where(keep, pr, 0.0).astype(BF16)
        for h in range(ATT_HEADS):
            ks_ref[h, :, 0:128] = kv_ref[:, 256 * h:256 * h + 128].astype(BF16)
            ks_ref[h, :, 128:256] = krr
            v = kv_ref[:, 256 * h + 128:256 * (h + 1)]
            vs_ref[h] = v.astype(BF16)
            vts_ref[h] = v.astype(F32).T.astype(BF16)

    slab = lambda w: pl.BlockSpec((ATT_HEADS, t, w), lambda i: (0, i, 0))
    return pl.pallas_call(
        body,
        out_shape=[jax.ShapeDtypeStruct((ATT_HEADS, lp, QP_W), BF16), jax.ShapeDtypeStruct((ATT_HEADS, lp, QP_W), BF16),
                   jax.ShapeDtypeStruct((ATT_HEADS, lp, V_HEAD), BF16), jax.ShapeDtypeStruct((ATT_HEADS, V_HEAD, lp), BF16)],
        grid=(lp // t,),
        in_specs=[_rb(t, 1536), _rb(t, 2048), _rb(t, 128), _rb(t, 128), _rb(t, 128)],
        out_specs=[slab(QP_W), slab(QP_W), slab(V_HEAD), pl.BlockSpec((ATT_HEADS, V_HEAD, t), lambda i: (0, 0, i))],
        name="qk_pack", compiler_params=_params(("parallel",)))(q, kv, krr, cos_t, sin_t)


def _shifted(ext, t, shift):
    if shift == 0:
        return ext[8:, :]
    return pltpu.roll(ext, shift, 0)[8:, :]


def _conv_fwd(name, proj, seg, width, conv_w, conv_b):
    lp = proj.shape[0]
    t = ROW_BLOCK
    cb = seg // width

    def body(u_ref, halo_ref, w_ref, b_ref, pre_ref, act_ref):
        i = pl.program_id(0)
        u = u_ref[...]
        halo = jnp.where(i > 0, halo_ref[...], 0.0)
        ext = jnp.concatenate([halo, u], axis=0)
        pre = jnp.broadcast_to(b_ref[...], (t, width))
        for k in range(CONV_K):
            pre = pre + w_ref[k:k + 1, :] * _shifted(ext, t, CONV_K - 1 - k)
        pre_ref[...] = pre
        act_ref[...] = pre * _sigmoid(pre)

    return pl.pallas_call(
        body,
        out_shape=[jax.ShapeDtypeStruct((lp, width), F32)] * 2,
        grid=(lp // t,),
        in_specs=[_rb(t, width, cb),
                  pl.BlockSpec((8, width), lambda i: (jnp.maximum(i * (t // 8) - 1, 0), cb)),
                  _full((CONV_K, width)), _full((1, width))],
        out_specs=[_rb(t, width), _rb(t, width)],
        name=name, compiler_params=_params(("parallel",)))(proj, proj, conv_w, conv_b)


def _softplus(x):
    return jnp.maximum(x, 0.0) + jnp.log1p(jnp.exp(-jnp.abs(x)))


def _dt_fwd(proj, dt_bias_pad):
    lp = proj.shape[0]
    t = ROW_BLOCK

    def body(x_ref, b_ref, o_ref):
        o_ref[...] = _softplus(x_ref[...] + b_ref[...])

    return pl.pallas_call(
        body, out_shape=jax.ShapeDtypeStruct((lp, 128), F32), grid=(lp // t,),
        in_specs=[_rb(t, 128, SEG_DT // 128), _full((1, 128))], out_specs=_rb(t, 128),
        name="dt_fwd", compiler_params=_params(("parallel",)))(proj, dt_bias_pad)


def _gated_norm_group(y, z, w):
    g = y * (z * _sigmoid(z))
    return g * lax.rsqrt(jnp.mean(g * g, axis=-1, keepdims=True) + EPS) * w


def _gated_norm_fwd(y, proj, w):
    lp = y.shape[0]
    t = ROW_BLOCK
    gw = SSM_WIDTH // 2

    def body(y0, y1, z0, z1, w0, w1, o_ref):
        o_ref[:, 0:gw] = _gated_norm_group(y0[...], z0[...], w0[...]).astype(BF16)
        o_ref[:, gw:] = _gated_norm_group(y1[...], z1[...], w1[...]).astype(BF16)

    zb = SEG_Z // gw
    return pl.pallas_call(
        body, out_shape=jax.ShapeDtypeStruct((lp, SSM_WIDTH), BF16), grid=(lp // t,),
        in_specs=[_rb(t, gw, 0), _rb(t, gw, 1), _rb(t, gw, zb), _rb(t, gw, zb + 1),
                  pl.BlockSpec((1, gw), lambda i: (0, 0)), pl.BlockSpec((1, gw), lambda i: (0, 1))],
        out_specs=_rb(t, SSM_WIDTH),
        name="gated_norm_fwd", compiler_params=_params(("parallel",)))(y, y, proj, proj, w, w)


def _gated_norm_bwd(y, proj, w, dssm):
    lp = y.shape[0]
    t = ROW_BLOCK
    gw = SSM_WIDTH // 2

    def body(y0, y1, z0, z1, w0, w1, d0, d1, dy_ref, dz_ref, dw_ref):
        dws = []
        for g, (yr, zr, wr, dr) in enumerate(((y0, z0, w0, d0), (y1, z1, w1, d1))):
            _, vjp = jax.vjp(_gated_norm_group, yr[...], zr[...], wr[...])
            dyg, dzg, dwg = vjp(dr[...])
            dy_ref[:, g * gw:(g + 1) * gw] = dyg
            dz_ref[:, g * gw:(g + 1) * gw] = dzg
            dws.append(dwg)
        _acc_add(dw_ref, jnp.concatenate(dws, axis=1))

    zb = SEG_Z // gw
    return pl.pallas_call(
        body,
        out_shape=[jax.ShapeDtypeStruct((lp, SSM_WIDTH), F32), jax.ShapeDtypeStruct((lp, SSM_WIDTH), F32),
                   jax.ShapeDtypeStruct((1, SSM_WIDTH), F32)],
        grid=(lp // t,),
        in_specs=[_rb(t, gw, 0), _rb(t, gw, 1), _rb(t, gw, zb), _rb(t, gw, zb + 1),
                  pl.BlockSpec((1, gw), lambda i: (0, 0)), pl.BlockSpec((1, gw), lambda i: (0, 1)),
                  _rb(t, gw, 2), _rb(t, gw, 3)],
        out_specs=[_rb(t, SSM_WIDTH), _rb(t, SSM_WIDTH), _full((1, SSM_WIDTH))],
        name="gated_norm_bwd", compiler_params=_params(("arbitrary",)))(y, y, proj, proj, w, w, dssm, dssm)


def _mix_residual(h0, mix, g_post, g_mlp_pre):
    lp = h0.shape[0]
    t = ROW_BLOCK

    def body(h_ref, m_ref, gp_ref, gm_ref, h1_ref, n2_ref):
        h1 = h_ref[...] + _rms(m_ref[...], gp_ref[...])
        h1_ref[...] = h1
        n2_ref[...] = _rms(h1, gm_ref[...]).astype(BF16)

    return pl.pallas_call(
        body, out_shape=[jax.ShapeDtypeStruct((lp, D_MODEL), F32), jax.ShapeDtypeStruct((lp, D_MODEL), BF16)],
        grid=(lp // t,),
        in_specs=[_rb(t, D_MODEL), _rb(t, D_MODEL), _full((1, D_MODEL)), _full((1, D_MODEL))],
        out_specs=[_rb(t, D_MODEL), _rb(t, D_MODEL)],
        name="mix_residual", compiler_params=_params(("parallel",)))(h0, mix, g_post, g_mlp_pre)


def _loss_and_grad(h1, f, g_post, tgt, n_real):
    lp = h1.shape[0]
    t = LOSS_ROW_BLOCK
    assert n_real % t == 0 and t % N_META == 0 and lp % t == 0
    n_tb = n_real // t

    def body(h1_ref, f_ref, g_ref, halo_ref, t_ref, loss_ref, dh2_ref, df_ref, dg_ref):
        i = pl.program_id(0)
        fx = f_ref[...]
        h2 = h1_ref[...] + _rms(fx, g_ref[...])
        row = i * t + lax.broadcasted_iota(jnp.int32, (t, 1), 0)
        real = jnp.logical_and(row >= N_META, row < N_META + n_real)
        target = jnp.concatenate([halo_ref[...], t_ref[0:t - N_META, :]], axis=0)
        diff = jnp.where(real, h2 - target, 0.0)
        part = 0.5 * jnp.sum(jnp.sum(diff * diff, axis=-1, keepdims=True) / D_MODEL, axis=0, keepdims=True)
        _acc_add(loss_ref, jnp.broadcast_to(part, (1, 128)))
        dh2 = diff / D_MODEL
        dh2_ref[...] = dh2
        dfx, dg = _rms_bwd(fx, g_ref[...], dh2)
        df_ref[...] = dfx.astype(BF16)
        _acc_add(dg_ref, dg)

    return pl.pallas_call(
        body,
        out_shape=[jax.ShapeDtypeStruct((1, 128), F32), jax.ShapeDtypeStruct((lp, D_MODEL), F32),
                   jax.ShapeDtypeStruct((lp, D_MODEL), BF16), jax.ShapeDtypeStruct((1, D_MODEL), F32)],
        grid=(lp // t,),
        in_specs=[_rb(t, D_MODEL), _rb(t, D_MODEL), _full((1, D_MODEL)),
                  pl.BlockSpec((N_META, D_MODEL),
                               lambda i: (jnp.clip(i * (t // N_META) - 1, 0, n_real // N_META - 1), 0)),
                  pl.BlockSpec((t, D_MODEL), lambda i: (jnp.minimum(i, n_tb - 1), 0))],
        out_specs=[_full((1, 128)), _rb(t, D_MODEL), _rb(t, D_MODEL), _full((1, D_MODEL))],
        name="loss_and_grad", compiler_params=_params(("arbitrary",)))(h1, f, g_post, tgt, tgt)


def _mlp_residual_bwd(dh2, dn2, h1, g_mlp_pre, mix, g_post):
    lp = h1.shape[0]
    t = ROW_BLOCK

    def body(dh2_ref, dn2_ref, h1_ref, gm_ref, mix_ref, gp_ref, dh1_ref, dmix_ref, dgm_ref, dgp_ref):
        dx, dgm = _rms_bwd(h1_ref[...], gm_ref[...], dn2_ref[...])
        dh1 = dh2_ref[...] + dx
        dh1_ref[...] = dh1
        dmix, dgp = _rms_bwd(mix_ref[...], gp_ref[...], dh1)
        dmix_ref[...] = dmix.astype(BF16)
        _acc_add(dgm_ref, dgm)
        _acc_add(dgp_ref, dgp)

    return pl.pallas_call(
        body,
        out_shape=[jax.ShapeDtypeStruct((lp, D_MODEL), F32), jax.ShapeDtypeStruct((lp, D_MODEL), BF16),
                   jax.ShapeDtypeStruct((1, D_MODEL), F32), jax.ShapeDtypeStruct((1, D_MODEL), F32)],
        grid=(lp // t,),
        in_specs=[_rb(t, D_MODEL), _rb(t, D_MODEL), _rb(t, D_MODEL), _full((1, D_MODEL)), _rb(t, D_MODEL),
                  _full((1, D_MODEL))],
        out_specs=[_rb(t, D_MODEL), _rb(t, D_MODEL), _full((1, D_MODEL)), _full((1, D_MODEL))],
        name="mlp_residual_bwd", compiler_params=_params(("arbitrary",)))(dh2, dn2, h1, g_mlp_pre, mix, g_post)


def _input_norm_bwd(dh1, dn1, h0, g_pre):
    lp = h0.shape[0]
    t = ROW_BLOCK

    def body(dh1_ref, dn1_ref, h0_ref, g_ref, dh0_ref, dg_ref):
        dx, dg = _rms_bwd(h0_ref[...], g_ref[...], dn1_ref[...])
        dh0_ref[...] = dh1_ref[...] + dx
        _acc_add(dg_ref, dg)

    return pl.pallas_call(
        body, out_shape=[jax.ShapeDtypeStruct((lp, D_MODEL), F32), jax.ShapeDtypeStruct((1, D_MODEL), F32)],
        grid=(lp // t,),
        in_specs=[_rb(t, D_MODEL), _rb(t, D_MODEL), _rb(t, D_MODEL), _full((1, D_MODEL))],
        out_specs=[_rb(t, D_MODEL), _full((1, D_MODEL))],
        name="input_norm_bwd", compiler_params=_params(("arbitrary",)))(dh1, dn1, h0, g_pre)


def _conv_bwd(name, dact, pre, proj, seg, width, conv_w):
    lp = proj.shape[0]
    t = ROW_BLOCK
    cb = seg // width
    nblk = lp // t

    def dsilu(p):
        s = _sigmoid(p)
        return s * (1.0 + p * (1.0 - s))

    def body(da_ref, dan_ref, pre_ref, pren_ref, u_ref, halo_ref, w_ref, du_ref, dw_ref, db_ref):
        i = pl.program_id(0)
        dpre = da_ref[...] * dsilu(pre_ref[...])
        dpre_next = jnp.where(i < nblk - 1, dan_ref[...] * dsilu(pren_ref[...]), 0.0)
        extd = jnp.concatenate([dpre, dpre_next], axis=0)
        halo = jnp.where(i > 0, halo_ref[...], 0.0)
        ext = jnp.concatenate([halo, u_ref[...]], axis=0)
        du = jnp.zeros((t, width), F32)
        dws = []
        for k in range(CONV_K):
            m = CONV_K - 1 - k
            ahead = dpre if m == 0 else pltpu.roll(extd, t + 8 - m, 0)[:t, :]
            du = du + w_ref[k:k + 1, :] * ahead
            dws.append(jnp.sum(dpre * _shifted(ext, t, m), axis=0, keepdims=True))
        du_ref[...] = du
        _acc_add(dw_ref, jnp.concatenate(dws + [jnp.zeros((8 - CONV_K, width), F32)], axis=0))
        _acc_add(db_ref, jnp.sum(dpre, axis=0, keepdims=True))

    nxt = lambda i: (jnp.minimum((i + 1) * (t // 8), lp // 8 - 1), 0)
    return pl.pallas_call(
        body,
        out_shape=[jax.ShapeDtypeStruct((lp, width), F32), jax.ShapeDtypeStruct((8, width), F32),
                   jax.ShapeDtypeStruct((1, width), F32)],
        grid=(nblk,),
        in_specs=[_rb(t, width), pl.BlockSpec((8, width), nxt), _rb(t, width), pl.BlockSpec((8, width), nxt),
                  _rb(t, width, cb),
                  pl.BlockSpec((8, width), lambda i: (jnp.maximum(i * (t // 8) - 1, 0), cb)),
                  _full((CONV_K, width))],
        out_specs=[_rb(t, width), _full((8, width)), _full((1, width))],
        name=name, compiler_params=_params(("arbitrary",)))(dact, dact, pre, pre, proj, proj, conv_w)


def _qk_unpack_bwd(dqs, dks, dvs, cos_t, sin_t):
    lp = dqs.shape[1]
    t = ROW_BLOCK

    def body(dqs_ref, dks_ref, dvs_ref, cos_ref, sin_ref, dq_ref, dkv_ref, dkr_ref):
        lane = lax.broadcasted_iota(jnp.int32, (t, 128), 1)
        lo = lane < 64
        for j in range(ATT_HEADS // 2):
            dpr = jnp.where(lo, dqs_ref[2 * j, :, 128:256], dqs_ref[2 * j + 1, :, 128:256])
            dq_ref[:, 1024 + 128 * j:1024 + 128 * (j + 1)] = _rope_bwd(dpr, cos_ref[...], sin_ref[...]).astype(BF16)
        dkrr = jnp.zeros((t, 128), F32)
        for h in range(ATT_HEADS):
            dq_ref[:, 128 * h:128 * (h + 1)] = dqs_ref[h, :, 0:128].astype(BF16)
            dkv_ref[:, 256 * h:256 * h + 128] = dks_ref[h, :, 0:128].astype(BF16)
            dkv_ref[:, 256 * h + 128:256 * (h + 1)] = dvs_ref[h].astype(BF16)
            dkrr = dkrr + dks_ref[h, :, 128:256]
        droped = jnp.where(lo, dkrr + pltpu.roll(dkrr, 64, 1), 0.0)
        dkr_ref[...] = _rope_bwd(droped, cos_ref[...], sin_ref[...])

    slab = lambda w: pl.BlockSpec((ATT_HEADS, t, w), lambda i: (0, i, 0))
    return pl.pallas_call(
        body,
        out_shape=[jax.ShapeDtypeStruct((lp, 1536), BF16), jax.ShapeDtypeStruct((lp, 2048), BF16),
                   jax.ShapeDtypeStruct((lp, 128), F32)],
        grid=(lp // t,),
        in_specs=[slab(QP_W), slab(QP_W), slab(V_HEAD), _rb(t, 128), _rb(t, 128)],
        out_specs=[_rb(t, 1536), _rb(t, 2048), _rb(t, 128)],
        name="qk_unpack_bwd", compiler_params=_params(("parallel",)))(dqs, dks, dvs, cos_t, sin_t)


def _proj_grad(proj, dcqn, dckvn, g_q, g_kv, dkr, ddt_pad, dt_bias_pad, dz, dxs, dbc):
    lp = proj.shape[0]
    t = ROW_BLOCK

    def body(ckv_ref, cq_ref, pdt_ref, dcq_ref, dckv_ref, gq_ref, gkv_ref, dkr_ref, ddt_ref, b_ref, dz_ref, dxs_ref,
             dbc_ref, dp_ref, dgq_ref, dgkv_ref, db_ref):
        dckv, dgkv = _rms_bwd(ckv_ref[...], gkv_ref[...], dckv_ref[...])
        dcq, dgq = _rms_bwd(cq_ref[...], gq_ref[...], dcq_ref[...])
        ddt_raw = ddt_ref[...] * _sigmoid(pdt_ref[...] + b_ref[...])
        dp_ref[:, SEG_KV:SEG_KV + KV_LORA] = dckv.astype(BF16)
        dp_ref[:, SEG_KR:SEG_KR + 128] = dkr_ref[...].astype(BF16)
        dp_ref[:, SEG_CQ:SEG_CQ + Q_LORA] = dcq.astype(BF16)
        dp_ref[:, SEG_DT:SEG_DT + 128] = ddt_raw.astype(BF16)
        dp_ref[:, SEG_DT + 128:SEG_Z] = jnp.zeros((t, SEG_Z - SEG_DT - 128), BF16)
        dp_ref[:, SEG_Z:SEG_XS] = dz_ref[...].astype(BF16)
        dp_ref[:, SEG_XS:SEG_BC] = dxs_ref[...].astype(BF16)
        dp_ref[:, SEG_BC:PROJ_W] = dbc_ref[...].astype(BF16)
        _acc_add(dgq_ref, dgq)
        _acc_add(dgkv_ref, dgkv)
        _acc_add(db_ref, jnp.sum(ddt_raw, axis=0, keepdims=True))

    return pl.pallas_call(
        body,
        out_shape=[jax.ShapeDtypeStruct((lp, PROJ_W), BF16), jax.ShapeDtypeStruct((1, Q_LORA), F32),
                   jax.ShapeDtypeStruct((1, KV_LORA), F32), jax.ShapeDtypeStruct((1, 128), F32)],
        grid=(lp // t,),
        in_specs=[_rb(t, KV_LORA, SEG_KV // KV_LORA), _rb(t, Q_LORA, SEG_CQ // Q_LORA), _rb(t, 128, SEG_DT // 128),
                  _rb(t, Q_LORA), _rb(t, KV_LORA), _full((1, Q_LORA)), _full((1, KV_LORA)), _rb(t, 128), _rb(t, 128),
                  _full((1, 128)), _rb(t, SSM_WIDTH), _rb(t, SSM_WIDTH), _rb(t, 512)],
        out_specs=[_rb(t, PROJ_W), _full((1, Q_LORA)), _full((1, KV_LORA)), _full((1, 128))],
        name="proj_grad", compiler_params=_params(("arbitrary",)))(
            proj, proj, proj, dcqn, dckvn, g_q, g_kv, dkr, ddt_pad, dt_bias_pad, dz, dxs, dbc)


def _pair_tables(n):
    qmaj = [(i, j) for i in range(n) for j in range(i + 1)]
    kmaj = [(i, j) for j in range(n) for i in range(j, n)]
    to = lambda ps, c: jnp.asarray(np.array([p[c] for p in ps], np.int32))
    return (to(qmaj, 0), to(qmaj, 1)), (to(kmaj, 0), to(kmaj, 1))


def _att_block(lp, edge=ATT_BLOCK):
    return edge if lp % edge == 0 else MM_BLOCK


def _nt(a, b):
    return lax.dot_general(a, b, (((1,), (1,)), ((), ())), preferred_element_type=F32)


def _attn_fwd(qs, ks, vts):
    lp = qs.shape[1]
    t = _att_block(lp, ATT_BLOCK_FWD)
    n = lp // t
    (qi, kj), _ = _pair_tables(n)
    th = t // 2

    def body(qi_ref, kj_ref, q_ref, k_ref, vt_ref, o_ref, o16_ref, lse_ref, m_s, l_s, acc_s):
        p = pl.program_id(1)
        i, j = qi_ref[p], kj_ref[p]

        @pl.when(j == 0)
        def _():
            m_s[...] = jnp.full_like(m_s, NEG_BIG)
            l_s[...] = jnp.zeros_like(l_s)
            acc_s[...] = jnp.zeros_like(acc_s)

        def update(masked, k0=0, kn=t, q0=0, qn=t):
            cols = slice(q0, q0 + qn)
            sc = _nt(k_ref[0, k0:k0 + kn, :], q_ref[0, cols, :]) * ATT_SCALE_LOG2
            if masked:
                keep = (lax.broadcasted_iota(jnp.int32, (kn, qn), 1) + q0
                        >= lax.broadcasted_iota(jnp.int32, (kn, qn), 0) + k0)
                sc = jnp.where(keep, sc, NEG_BIG)
            m_prev = m_s[:, cols]
            m_new = jnp.maximum(m_prev, jnp.max(sc, axis=0, keepdims=True))
            alpha = jnp.exp2(m_prev - m_new)
            pexp = jnp.exp2(sc - m_new)
            l_s[:, cols] = alpha * l_s[:, cols] + jnp.sum(pexp, axis=0, keepdims=True)
            acc_s[:, cols] = alpha * acc_s[:, cols] + jnp.dot(vt_ref[0, :, k0:k0 + kn], pexp.astype(BF16),
                                                              preferred_element_type=F32)
            m_s[:, cols] = m_new

        @pl.when(j < i)
        def _():
            update(False)

        @pl.when(j == i)
        def _():
            if th % 128 == 0:
                update(True, 0, th, 0, t)
                update(True, th, th, th, th)
            else:
                update(True)
            out = (acc_s[...] / l_s[...]).T
            o_ref[...] = out
            o16_ref[...] = out.astype(BF16)
            lse_ref[0] = m_s[...] + jnp.log2(l_s[...])

    grid_spec = pltpu.PrefetchScalarGridSpec(
        num_scalar_prefetch=2, grid=(ATT_HEADS, int(qi.shape[0])),
        in_specs=[pl.BlockSpec((1, t, QP_W), lambda h, p, qi, kj: (h, qi[p], 0)),
                  pl.BlockSpec((1, t, QP_W), lambda h, p, qi, kj: (h, kj[p], 0)),
                  pl.BlockSpec((1, V_HEAD, t), lambda h, p, qi, kj: (h, 0, kj[p]))],
        out_specs=[pl.BlockSpec((t, V_HEAD), lambda h, p, qi, kj: (qi[p], h)),
                   pl.BlockSpec((t, V_HEAD), lambda h, p, qi, kj: (qi[p], h)),
                   pl.BlockSpec((1, 1, t), lambda h, p, qi, kj: (h, 0, qi[p]))],
        scratch_shapes=[pltpu.VMEM((1, t), F32), pltpu.VMEM((1, t), F32), pltpu.VMEM((V_HEAD, t), F32)])
    return pl.pallas_call(
        body, grid_spec=grid_spec,
        out_shape=[jax.ShapeDtypeStruct((lp, ATT_HEADS * V_HEAD), F32),
                   jax.ShapeDtypeStruct((lp, ATT_HEADS * V_HEAD), BF16), jax.ShapeDtypeStruct((ATT_HEADS, 1, lp), F32)],
        name="attn_fwd", compiler_params=_params(("parallel", "arbitrary")))(qi, kj, qs, ks, vts)


def _attn_delta(datt, att):
    lp = att.shape[0]
    t = MM_BLOCK
    w = ATT_HEADS * V_HEAD

    def body(do_ref, o_ref, d_ref):
        ones = jnp.ones((8, V_HEAD), BF16)
        for h in range(ATT_HEADS):
            cols = slice(h * V_HEAD, (h + 1) * V_HEAD)
            prod = do_ref[:, cols] * o_ref[:, cols]
            hi = prod.astype(BF16)
            lo = (prod - hi.astype(F32)).astype(BF16)
            d_ref[h] = (_nt(ones, hi) + _nt(ones, lo))[0:1, :]

    return pl.pallas_call(
        body, out_shape=jax.ShapeDtypeStruct((ATT_HEADS, 1, lp), F32), grid=(lp // t,),
        in_specs=[_rb(t, w), _rb(t, w)], out_specs=pl.BlockSpec((ATT_HEADS, 1, t), lambda i: (0, 0, i)),
        name="attn_delta", compiler_params=_params(("parallel",)))(datt, att)


def _attn_bwd(qs, ks, vs, datt16, lse2, delta):
    lp = qs.shape[1]
    tk = _att_block(lp)
    tq = ATT_BLOCK_Q_BWD if lp % ATT_BLOCK_Q_BWD == 0 and ATT_BLOCK_Q_BWD % tk == 0 else tk
    r = tq // tk
    nk, nq = lp // tk, lp // tq
    pairs = [(i, j) for j in range(nk) for i in range(j // r, nq)]
    qi = jnp.asarray(np.array([p[0] for p in pairs], np.int32))
    kj = jnp.asarray(np.array([p[1] for p in pairs], np.int32))
    n_pairs = len(pairs)

    def body(qi_ref, kj_ref, k_ref, v_ref, q_ref, do_ref, lse_ref, dl_ref, dq_hbm, dk_ref, dv_ref, dq_s, dk_s, dv_s,
             sem):
        h, p = pl.program_id(0), pl.program_id(1)
        i, j = qi_ref[p], kj_ref[p]
        first = i == j // r

        @pl.when(p == 0)
        def _():
            dq_s[...] = jnp.zeros_like(dq_s)

        @pl.when(first)
        def _():
            dk_s[...] = jnp.zeros_like(dk_s)
            dv_s[...] = jnp.zeros_like(dv_s)

        def step(masked, skip=0):
            q0 = skip * tk
            nrows = tq - q0
            q = q_ref[0, q0:, :]
            do = do_ref[q0:, :]
            pt = jnp.exp2(_nt(k_ref[0], q) * ATT_SCALE_LOG2 - lse_ref[0, :, q0:])
            if masked:
                keep = (lax.broadcasted_iota(jnp.int32, (tk, nrows), 1)
                        >= lax.broadcasted_iota(jnp.int32, (tk, nrows), 0))
                pt = jnp.where(keep, pt, 0.0)
            dst = (pt * (_nt(v_ref[0], do) - dl_ref[0, :, q0:]) * ATT_SCALE).astype(BF16)
            dv_s[...] += jnp.dot(pt.astype(BF16), do, preferred_element_type=F32)
            dk_s[...] += jnp.dot(dst, q, preferred_element_type=F32)
            rows = pl.ds(pl.multiple_of(i * tq + q0, tk), nrows)
            dq_s[rows, :] += lax.dot_general(dst, k_ref[0], (((0,), (0,)), ((), ())), preferred_element_type=F32)

        @pl.when(jnp.logical_not(first))
        def _():
            step(False)

        for sub in range(r):
            @pl.when(jnp.logical_and(first, j % r == sub))
            def _(sub=sub):
                step(True, sub)

        @pl.when(i == nq - 1)
        def _():
            dk_ref[0] = dk_s[...]
            dv_ref[0] = dv_s[...]

        @pl.when(p == n_pairs - 1)
        def _():
            out = pltpu.make_async_copy(dq_s, dq_hbm.at[h], sem)
            out.start()
            out.wait()

    grid_spec = pltpu.PrefetchScalarGridSpec(
        num_scalar_prefetch=2, grid=(ATT_HEADS, n_pairs),
        in_specs=[pl.BlockSpec((1, tk, QP_W), lambda h, p, qi, kj: (h, kj[p], 0)),
                  pl.BlockSpec((1, tk, V_HEAD), lambda h, p, qi, kj: (h, kj[p], 0)),
                  pl.BlockSpec((1, tq, QP_W), lambda h, p, qi, kj: (h, qi[p], 0)),
                  pl.BlockSpec((tq, V_HEAD), lambda h, p, qi, kj: (qi[p], h)),
                  pl.BlockSpec((1, 1, tq), lambda h, p, qi, kj: (h, 0, qi[p])),
                  pl.BlockSpec((1, 1, tq), lambda h, p, qi, kj: (h, 0, qi[p]))],
        out_specs=[pl.BlockSpec(memory_space=pl.ANY),
                   pl.BlockSpec((1, tk, QP_W), lambda h, p, qi, kj: (h, kj[p], 0)),
                   pl.BlockSpec((1, tk, V_HEAD), lambda h, p, qi, kj: (h, kj[p], 0))],
        scratch_shapes=[pltpu.VMEM((lp, QP_W), F32), pltpu.VMEM((tk, QP_W), F32), pltpu.VMEM((tk, V_HEAD), F32),
                        pltpu.SemaphoreType.DMA])
    return pl.pallas_call(
        body, grid_spec=grid_spec,
        out_shape=[jax.ShapeDtypeStruct((ATT_HEADS, lp, QP_W), F32), jax.ShapeDtypeStruct((ATT_HEADS, lp, QP_W), F32),
                   jax.ShapeDtypeStruct((ATT_HEADS, lp, V_HEAD), F32)],
        name="attn_bwd", compiler_params=_params(("arbitrary", "arbitrary")))(
            qi, kj, ks, vs, qs, datt16, lse2, delta)


N_PAIRS = SSM_HEADS // 2
HI = lax.Precision.HIGHEST


def _ssd_chunk(xp, bs, cs, dt, dt_t, alr, alc, dsk, st):
    q = dt.shape[0]
    li = lax.broadcasted_iota(jnp.int32, (q, q), 0)
    si = lax.broadcasted_iota(jnp.int32, (q, q), 1)
    tri = (si <= li).astype(F32)
    tri_t = (li <= si).astype(F32)
    lo = lax.broadcasted_iota(jnp.int32, (1, 128), 1) < 64
    h_r = lax.broadcasted_iota(jnp.int32, (1, SSM_HEADS), 1)
    h_c = lax.broadcasted_iota(jnp.int32, (SSM_HEADS, 1), 0)
    a = dt * (-jnp.exp(alr))
    a_t = dt_t * (-jnp.exp(alc))
    acum = jnp.dot(tri, a, precision=HI, preferred_element_type=F32)
    acum_t = jnp.dot(a_t, tri_t, precision=HI, preferred_element_type=F32)
    last = (lax.broadcasted_iota(jnp.int32, (q, 1), 0) == q - 1).astype(F32)
    alast = jnp.sum(acum * last, axis=0, keepdims=True)
    e = jnp.exp(acum)
    rdt = jnp.exp(alast - acum) * dt
    e_last = jnp.exp(alast)

    def col(m, h):
        return jnp.sum(m * (h_r == h).astype(F32), axis=1, keepdims=True)

    def row(m, h):
        return jnp.sum(m * (h_c == h).astype(F32), axis=0, keepdims=True)

    def pair(m, ha):
        return jnp.where(lo, col(m, ha), col(m, ha + 1))

    ys, st_new = [], []
    for g in range(2):
        c_b = cs[g].astype(BF16)
        b_b = bs[g].astype(BF16)
        cb = _nt(c_b, b_b)
        for j in range(N_PAIRS // 2):
            p = (N_PAIRS // 2) * g + j
            ha = 2 * p
            x = xp[p]
            x_b = x.astype(BF16)

            def w_of(h):
                seg = col(acum, h) - row(acum_t, h)
                return (cb * jnp.exp(jnp.minimum(seg, 0.0)) * tri * row(dt_t, h)).astype(BF16)

            y_diag = jnp.where(lo, jnp.dot(w_of(ha), x_b, preferred_element_type=F32),
                               jnp.dot(w_of(ha + 1), x_b, preferred_element_type=F32))
            y_off = jnp.dot(c_b, st[p].astype(BF16), preferred_element_type=F32) * pair(e, ha)
            ys.append(y_diag + y_off + pair(dsk, ha) * x)
            xw = (x * pair(rdt, ha)).astype(BF16)
            st_new.append(st[p] * pair(e_last, ha)
                          + lax.dot_general(b_b, xw, (((0,), (0,)), ((), ())), preferred_element_type=F32))
    return ys, st_new


def _ssd_fwd(xs, bc, dt, dt_t, alr, alc, dsk):
    lp = xs.shape[0]
    q = SSD_CHUNK
    nc = lp // q

    def body(x_ref, b_ref, c_ref, dt_ref, dtt_ref, alr_ref, alc_ref, dsk_ref, y_ref, sp_ref, st_s):
        @pl.when(pl.program_id(0) == 0)
        def _():
            st_s[...] = jnp.zeros_like(st_s)

        sp_ref[0] = st_s[...]
        xp = [x_ref[:, 128 * p:128 * (p + 1)] for p in range(N_PAIRS)]
        bs = [b_ref[:, 0:128], b_ref[:, 128:256]]
        cs = [c_ref[:, 0:128], c_ref[:, 128:256]]
        ys, st_new = _ssd_chunk(xp, bs, cs, dt_ref[...], dtt_ref[...], alr_ref[...], alc_ref[...], dsk_ref[...],
                                [st_s[p] for p in range(N_PAIRS)])
        for p in range(N_PAIRS):
            y_ref[:, 128 * p:128 * (p + 1)] = ys[p]
            st_s[p] = st_new[p]

    return pl.pallas_call(
        body,
        out_shape=[jax.ShapeDtypeStruct((lp, SSM_WIDTH), F32), jax.ShapeDtypeStruct((nc, N_PAIRS, 128, 128), F32)],
        grid=(nc,),
        in_specs=[_rb(q, SSM_WIDTH), _rb(q, 256, 0), _rb(q, 256, 1), _rb(q, SSM_HEADS),
                  pl.BlockSpec((SSM_HEADS, q), lambda i: (0, i)),
                  _full((1, SSM_HEADS)), _full((SSM_HEADS, 1)), _full((1, SSM_HEADS))],
        out_specs=[_rb(q, SSM_WIDTH), pl.BlockSpec((1, N_PAIRS, 128, 128), lambda i: (i, 0, 0, 0))],
        scratch_shapes=[pltpu.VMEM((N_PAIRS, 128, 128), F32)],
        name="ssd_fwd", compiler_params=_params(("arbitrary",)))(xs, bc, bc, dt, dt_t, alr, alc, dsk)


def _ssd_bwd(xs, bc, dt, dt_t, alr, alc, dsk, sprev, dy):
    lp = xs.shape[0]
    q = SSD_CHUNK
    nc = lp // q

    def body(x_ref, b_ref, c_ref, dt_ref, dtt_ref, alr_ref, alc_ref, dsk_ref, sp_ref, dy_ref,
             dx_ref, dbc_ref, ddt_ref, ddtt_ref, dalr_ref, dalc_ref, ddsk_ref, ds_s):
        @pl.when(pl.program_id(0) == 0)
        def _():
            ds_s[...] = jnp.zeros_like(ds_s)

        xp = [x_ref[:, 128 * p:128 * (p + 1)] for p in range(N_PAIRS)]
        bs = [b_ref[:, 0:128], b_ref[:, 128:256]]
        cs = [c_ref[:, 0:128], c_ref[:, 128:256]]
        st = [sp_ref[0, p] for p in range(N_PAIRS)]
        _, vjp = jax.vjp(_ssd_chunk, xp, bs, cs, dt_ref[...], dtt_ref[...], alr_ref[...], alc_ref[...], dsk_ref[...],
                         st)
        dys = [dy_ref[:, 128 * p:128 * (p + 1)] for p in range(N_PAIRS)]
        dxp, dbs, dcs, ddt, ddtt, dalr, dalc, ddsk, dst = vjp((dys, [ds_s[p] for p in range(N_PAIRS)]))
        for p in range(N_PAIRS):
            dx_ref[:, 128 * p:128 * (p + 1)] = dxp[p]
            ds_s[p] = dst[p]
        for g in range(2):
            dbc_ref[:, 128 * g:128 * (g + 1)] = dbs[g]
            dbc_ref[:, 256 + 128 * g:256 + 128 * (g + 1)] = dcs[g]
        ddt_ref[...] = ddt
        ddtt_ref[...] = ddtt
        _acc_add(dalr_ref, dalr)
        _acc_add(dalc_ref, dalc)
        _acc_add(ddsk_ref, ddsk)

    rev = lambda width, cb=0: pl.BlockSpec((q, width), lambda i: (nc - 1 - i, cb))
    return pl.pallas_call(
        body,
        out_shape=[jax.ShapeDtypeStruct((lp, SSM_WIDTH), F32), jax.ShapeDtypeStruct((lp, 512), F32),
                   jax.ShapeDtypeStruct((lp, SSM_HEADS), F32), jax.ShapeDtypeStruct((SSM_HEADS, lp), F32),
                   jax.ShapeDtypeStruct((1, SSM_HEADS), F32), jax.ShapeDtypeStruct((SSM_HEADS, 1), F32),
                   jax.ShapeDtypeStruct((1, SSM_HEADS), F32)],
        grid=(nc,),
        in_specs=[rev(SSM_WIDTH), rev(256, 0), rev(256, 1), rev(SSM_HEADS),
                  pl.BlockSpec((SSM_HEADS, q), lambda i: (0, nc - 1 - i)),
                  _full((1, SSM_HEADS)), _full((SSM_HEADS, 1)), _full((1, SSM_HEADS)),
                  pl.BlockSpec((1, N_PAIRS, 128, 128), lambda i: (nc - 1 - i, 0, 0, 0)), rev(SSM_WIDTH)],
        out_specs=[rev(SSM_WIDTH), rev(512), rev(SSM_HEADS), pl.BlockSpec((SSM_HEADS, q), lambda i: (0, nc - 1 - i)),
                   _full((1, SSM_HEADS)), _full((SSM_HEADS, 1)), _full((1, SSM_HEADS))],
        scratch_shapes=[pltpu.VMEM((N_PAIRS, 128, 128), F32)],
        name="ssd_bwd", compiler_params=_params(("arbitrary",)))(xs, bc, bc, dt, dt_t, alr, alc, dsk, sprev, dy)


def _q_to_slab_order(w):
    hd = QK_NOPE + QK_ROPE
    nope = [w[:, h * hd:h * hd + QK_NOPE] for h in range(ATT_HEADS)]
    rope = [w[:, h * hd + QK_NOPE:(h + 1) * hd] for h in range(ATT_HEADS)]
    return jnp.concatenate(nope + rope, axis=1)


def _q_from_slab_order(wp):
    base = ATT_HEADS * QK_NOPE
    parts = []
    for h in range(ATT_HEADS):
        parts += [wp[:, QK_NOPE * h:QK_NOPE * (h + 1)], wp[:, base + QK_ROPE * h:base + QK_ROPE * (h + 1)]]
    return jnp.concatenate(parts, axis=1)


_IN_CQ, _IN_CKV, _IN_KR, _IN_Z, _IN_XS, _IN_BC, _IN_DT = (0, 384), (384, 640), (640, 704), (704, 1728), (1728, 2752), \
    (2752, 3264), (3264, 3280)


def _pack_w_in(w):
    z = lambda n: jnp.zeros((w.shape[0], n), w.dtype)
    s = lambda r: w[:, r[0]:r[1]]
    return jnp.concatenate([s(_IN_CKV), s(_IN_KR), z(64), s(_IN_CQ), s(_IN_DT), z(112), z(128), s(_IN_Z), s(_IN_XS),
                            s(_IN_BC)], axis=1)


def _unpack_w_in(wp):
    s = lambda off, n: wp[:, off:off + n]
    return jnp.concatenate([s(SEG_CQ, 384), s(SEG_KV, 256), s(SEG_KR, 64), s(SEG_Z, 1024), s(SEG_XS, 1024),
                            s(SEG_BC, 512), s(SEG_DT, 16)], axis=1)


def _rope_tables(lp):
    inv_freq = ROPE_THETA ** (-jnp.arange(0, QK_ROPE, 2, dtype=F32) / QK_ROPE)
    ang = jnp.arange(lp, dtype=F32)[:, None] * inv_freq[None, :]
    cos, sin = jnp.cos(ang), jnp.sin(ang)
    return jnp.tile(cos, (1, 4)), jnp.concatenate([-sin, sin, -sin, sin], axis=1)


def _local_step(x, tgt, w):
    n_real = x.shape[0]
    l = N_META + n_real
    lp = -(-l // MM_BLOCK) * MM_BLOCK
    assert lp % ROW_BLOCK == 0 and lp % SSD_CHUNK == 0, lp
    h0 = lax.optimization_barrier(jnp.concatenate([w["meta_tokens"], x, jnp.zeros((lp - l, D_MODEL), F32)], axis=0))
    cos_t, sin_t = _rope_tables(lp)

    w_in_p = _pack_w_in(w["w_in"])
    w_q_p = _q_to_slab_order(w["w_q_up"])
    w_kv = w["w_kv_up"]
    if "token" in w:
        cos_t = cos_t + w["token"][0, 0]
    conv_w, conv_b = w["conv_w"], w["conv_b"]
    dt_bias_pad = jnp.concatenate([w["dt_bias"], jnp.zeros((1, 128 - SSM_HEADS), F32)], axis=1)
    alr, dsk = w["a_log"], w["d_skip"]
    alc = alr.reshape(SSM_HEADS, 1)

    n1 = _norm_in(h0, w["norm_mix_pre"])
    proj = _mm("proj", n1, w_in_p)
    cqn, ckvn, krr = _attn_prep(proj, w["q_a_norm"], w["kv_a_norm"], cos_t, sin_t)
    q = _mm("q_up", cqn, w_q_p)
    kv = _mm("kv_up", ckvn, w_kv, outs=((BF16, None),))
    qs, ks, vs, vts = _qk_pack(q, kv, krr, cos_t, sin_t)
    att, att16, lse2 = _attn_fwd(qs, ks, vts)
    xs_pre, xs_act = _conv_fwd("conv_xs_fwd", proj, SEG_XS, SSM_WIDTH, conv_w[:, :SSM_WIDTH], conv_b[:, :SSM_WIDTH])
    bc_pre, bc_act = _conv_fwd("conv_bc_fwd", proj, SEG_BC, 512, conv_w[:, SSM_WIDTH:], conv_b[:, SSM_WIDTH:])
    dt = _dt_fwd(proj, dt_bias_pad)[:, :SSM_HEADS]
    dt_t = dt.T
    y, sprev = _ssd_fwd(xs_act, bc_act, dt, dt_t, alr, alc, dsk)
    ssm = _gated_norm_fwd(y, proj, w["ssm_norm"])
    late = w["late_weights"](ssm) if "late_weights" in w else w
    w_out, w_up, w_down = late["w_out"], late["w_mlp_up"], late["w_mlp_down"]
    mix = _mm("out_proj", (att16, ssm), w_out)
    h1, n2 = _mix_residual(h0, mix, w["norm_mix_post"], w["norm_mlp_pre"])
    relu2 = lambda r: jnp.square(jnp.maximum(r, 0.0))
    act = _mm("mlp_up", n2, w_up, outs=((BF16, relu2),))
    f = _mm("mlp_down", act, w_down)
    loss, dh2, df, dg_mlp_post = _loss_and_grad(h1, f, w["norm_mlp_post"], tgt, n_real)

    g = {"norm_mlp_post": dg_mlp_post}
    g["w_mlp_down"] = _mm_tn("d_w_mlp_down", act, df)
    du = _mm("d_mlp_act", df, w_down, outs=((BF16, lambda r, ab: r * (2.0 * jnp.sqrt(ab.astype(F32)))),),
             epi_ins=(act,), trans_b=True)
    g["w_mlp_up"] = _mm_tn("d_w_mlp_up", n2, du)
    dn2 = _mm("d_n2", du, w_up, trans_b=True)
    dh1, dmix, g["norm_mlp_pre"], g["norm_mix_post"] = _mlp_residual_bwd(dh2, dn2, h1, w["norm_mlp_pre"], mix,
                                                                         w["norm_mix_post"])
    g["w_out"] = jnp.concatenate([_mm_tn("d_w_out_att", att16, dmix), _mm_tn("d_w_out_ssm", ssm, dmix)], axis=0)
    dcat, dcat16 = _mm("d_cat", dmix, w_out, outs=((F32, None), (BF16, None)), trans_b=True)
    dy, dz, g["ssm_norm"] = _gated_norm_bwd(y, proj, w["ssm_norm"], dcat)
    dxs_act, dbc_act, ddt, ddt_t, dalr, dalc, g["d_skip"] = _ssd_bwd(xs_act, bc_act, dt, dt_t, alr, alc, dsk, sprev, dy)
    g["a_log"] = dalr + dalc.reshape(1, SSM_HEADS)
    dxs, dcw_xs, dcb_xs = _conv_bwd("conv_xs_bwd", dxs_act, xs_pre, proj, SEG_XS, SSM_WIDTH, conv_w[:, :SSM_WIDTH])
    dbc, dcw_bc, dcb_bc = _conv_bwd("conv_bc_bwd", dbc_act, bc_pre, proj, SEG_BC, 512, conv_w[:, SSM_WIDTH:])
    g["conv_w"] = jnp.concatenate([dcw_xs[:CONV_K], dcw_bc[:CONV_K]], axis=1)
    g["conv_b"] = jnp.concatenate([dcb_xs, dcb_bc], axis=1)
    ddt_pad = jnp.concatenate([ddt + ddt_t.T, jnp.zeros((lp, 128 - SSM_HEADS), F32)], axis=1)

    dqs, dks, dvs = _attn_bwd(qs, ks, vs, dcat16, lse2, _attn_delta(dcat, att))
    dq, dkv, dkr = _qk_unpack_bwd(dqs, dks, dvs, cos_t, sin_t)
    g["w_q_up"] = _q_from_slab_order(_mm_tn("d_w_q_up", cqn, dq))
    g["w_kv_up"] = _mm_tn("d_w_kv_up", ckvn, dkv)
    dcqn = _mm("d_cqn", dq, w_q_p, trans_b=True)
    dckvn = _mm("d_ckvn", dkv, w_kv, trans_b=True)
    dproj, g["q_a_norm"], g["kv_a_norm"], ddtb = _proj_grad(proj, dcqn, dckvn, w["q_a_norm"], w["kv_a_norm"], dkr,
                                                          ddt_pad, dt_bias_pad, dz, dxs, dbc)
    g["dt_bias"] = ddtb[:, :SSM_HEADS]
    g["w_in"] = _unpack_w_in(_mm_tn("d_w_in", n1, dproj))
    dn1 = _mm("d_n1", dproj, w_in_p, trans_b=True)
    dh0, g["norm_mix_pre"] = _input_norm_bwd(dh1, dn1, h0, w["norm_mix_pre"])
    g["meta_tokens"] = dh0[:N_META]
    return loss, dh0, g


WEIGHTS = ["meta_tokens", "norm_mix_pre", "w_in", "q_a_norm", "w_q_up", "kv_a_norm", "w_kv_up", "conv_w", "conv_b",
           "dt_bias", "a_log", "d_skip", "ssm_norm", "w_out", "norm_mix_post", "norm_mlp_pre", "w_mlp_up",
           "w_mlp_down", "norm_mlp_post"]
SHARD_AXIS = {"meta_tokens": 1, "w_in": 1, "w_q_up": 1, "w_kv_up": 1, "conv_w": 1, "w_out": 0, "w_mlp_up": 1,
              "w_mlp_down": 0}
FULL_SHAPE = {"meta_tokens": (16, 1024), "norm_mix_pre": (1, 1024), "w_in": (1024, 3280), "q_a_norm": (1, 384),
              "w_q_up": (384, 1536), "kv_a_norm": (1, 256), "w_kv_up": (256, 2048), "conv_w": (4, 1536),
              "conv_b": (1, 1536), "dt_bias": (1, 16), "a_log": (1, 16), "d_skip": (1, 16), "ssm_norm": (1, 1024),
              "w_out": (2048, 1024), "norm_mix_post": (1, 1024), "norm_mlp_pre": (1, 1024), "w_mlp_up": (1024, 4096),
              "w_mlp_down": (4096, 1024), "norm_mlp_post": (1, 1024)}
GATHER_BF16 = ["w_in", "w_q_up", "w_kv_up", "w_out", "w_mlp_up", "w_mlp_down"]
GATHER_F32 = ["meta_tokens", "conv_w"]
GATHER_EARLY = ["w_in", "w_q_up", "w_kv_up"]
GATHER_LATE = ["w_out", "w_mlp_up", "w_mlp_down"]
ADAM_NATURAL = GATHER_BF16


def _shard_shape(name):
    shp = list(FULL_SHAPE[name])
    if name in SHARD_AXIS:
        shp[SHARD_AXIS[name]] //= N_CHIPS
    return tuple(shp)


PACK_ORDER = sorted(WEIGHTS, key=lambda n: -_shard_shape(n)[0])


PART_ROWS = 8


def _data_rows(shape):
    r, c = shape
    return r if c <= PACK_W else -(-c // PACK_W)


def _packed_rows(shape):
    return -(-_data_rows(shape) // PART_ROWS) * PART_ROWS


def _fill_rows(shapes, row_multiple):
    return -sum(_packed_rows(s) for s in shapes) % row_multiple


def _pack_rows(arrays, row_multiple, fill_before=None):
    assert arrays[0].dtype.itemsize == 4 or all(a.shape[0] % 16 == 0 for a in arrays)
    parts = []
    for a in arrays:
        r, c = a.shape
        if c > PACK_W:
            assert r == 1, a.shape
            r = _data_rows(a.shape)
            a = jnp.pad(a, ((0, 0), (0, r * PACK_W - c))).reshape(r, PACK_W)
            c = PACK_W
        parts.append(jnp.pad(a, ((0, _packed_rows((r, c)) - r), (0, PACK_W - c))))
    fill = _fill_rows([a.shape for a in arrays], row_multiple)
    if fill:
        at = len(parts) if fill_before is None else fill_before
        parts.insert(at, jnp.zeros((fill, PACK_W), parts[0].dtype))
    return jnp.concatenate(parts, axis=0)


def _unpack_rows(packed, shapes, row_multiple=1, fill_before=None):
    out, off = [], 0
    for k, (r, c) in enumerate(shapes):
        if k == fill_before:
            off += _fill_rows(shapes, row_multiple)
        nr = _data_rows((r, c))
        blk = packed[off:off + nr]
        out.append(blk[:, :c] if c <= PACK_W else blk.reshape(1, nr * PACK_W)[:, :c])
        off += _packed_rows((r, c))
    return out


def _chip_slice(full, name, t):
    if name not in SHARD_AXIS:
        return full
    ax = SHARD_AXIS[name]
    n = FULL_SHAPE[name][ax] // N_CHIPS
    return lax.slice_in_dim(full, t * n, (t + 1) * n, axis=ax)


HBM_SPEC = pl.BlockSpec(memory_space=pl.ANY)
CHIP_FLIPS = ((1, 0), (0, 1), (1, 1))


def _gather_chips(bufs):
    nb = len(bufs)

    def body(*refs):
        ins, outs = refs[:nb], refs[nb:2 * nb]
        send, recv, loc = refs[2 * nb:]
        x, y, c = lax.axis_index("x"), lax.axis_index("y"), lax.axis_index("c")
        me = 2 * x + y
        sibling = (x, y, 1 - c)
        sends, forwards = [], []
        for b in range(nb):
            half = bufs[b].shape[0] // 2
            mine = pl.ds(c * half, half)
            own = pltpu.make_async_copy(ins[b], outs[b].at[me], loc.at[b])
            own.start()
            sends.append(own)
            for k, (fx, fy) in enumerate(CHIP_FLIPS):
                cp = pltpu.make_async_remote_copy(
                    src_ref=ins[b].at[mine], dst_ref=outs[b].at[me, mine], send_sem=send.at[b, k],
                    recv_sem=recv.at[b, k], device_id=(x ^ fx, y ^ fy, c), device_id_type=MESH_ID)
                cp.start()
                sends.append(cp)
        for b in range(nb):
            half = bufs[b].shape[0] // 2
            mine, theirs = pl.ds(c * half, half), pl.ds((1 - c) * half, half)
            for k, (fx, fy) in enumerate(CHIP_FLIPS):
                chip = 2 * (x ^ fx) + (y ^ fy)
                landed = outs[b].at[chip, mine]
                pltpu.make_async_remote_copy(src_ref=landed, dst_ref=landed, send_sem=send.at[b, k],
                                             recv_sem=recv.at[b, k], device_id=sibling,
                                             device_id_type=MESH_ID).wait_recv()
                fw = pltpu.make_async_remote_copy(src_ref=landed, dst_ref=landed, send_sem=send.at[b, 3 + k],
                                                  recv_sem=recv.at[b, 3 + k], device_id=sibling,
                                                  device_id_type=MESH_ID)
                fw.start()
                forwards.append((fw, outs[b].at[chip, theirs], b, k))
        for fw, arriving, b, k in forwards:
            pltpu.make_async_remote_copy(src_ref=arriving, dst_ref=arriving, send_sem=send.at[b, 3 + k],
                                         recv_sem=recv.at[b, 3 + k], device_id=sibling,
                                         device_id_type=MESH_ID).wait_recv()
            fw.wait_send()
        for cp in sends[1::4] + sends[2::4] + sends[3::4]:
            cp.wait_send()
        for own in sends[0::4]:
            own.wait()

    return pl.pallas_call(
        body,
        out_shape=[jax.ShapeDtypeStruct((N_CHIPS,) + b.shape, b.dtype) for b in bufs],
        in_specs=[HBM_SPEC] * nb, out_specs=[HBM_SPEC] * nb,
        scratch_shapes=[pltpu.SemaphoreType.DMA((nb, 6)), pltpu.SemaphoreType.DMA((nb, 6)),
                        pltpu.SemaphoreType.DMA((nb,))],
        name="gather_chips")(*bufs)


HBM_ONLY = pl.BlockSpec(memory_space=pltpu.HBM)
SEM_SPEC = pl.BlockSpec(memory_space=pltpu.SEMAPHORE)
SPLIT_COPY_EFFECT = pltpu.SideEffectType.DATAFLOW_SIDE_EFFECTING


def _late_copies(src_ref, land_ref, sends, recvs):
    x, y, c = lax.axis_index("x"), lax.axis_index("y"), lax.axis_index("c")
    me = 2 * x + y
    return [pltpu.make_async_remote_copy(src_ref=src_ref, dst_ref=land_ref.at[me], send_sem=sends[k], recv_sem=recvs[k],
                                         device_id=(x ^ fx, y ^ fy, c), device_id_type=MESH_ID)
            for k, (fx, fy) in enumerate(CHIP_FLIPS)]


def _gather_late_start(buf):
    def body(src_ref, land_ref, s0, s1, s2, r0, r1, r2, src_thru, land_thru, token):
        for cp in _late_copies(src_ref, land_ref, (s0, s1, s2), (r0, r1, r2)):
            cp.start()
        token[...] = jnp.zeros_like(token)

    land = lax.empty((N_CHIPS,) + buf.shape, buf.dtype)
    sem = pltpu.SemaphoreType.DMA(())
    res = pl.pallas_call(
        body, name="gather_late_start",
        out_shape=(sem,) * 6 + (pltpu.HBM(buf.shape, buf.dtype), pltpu.HBM(land.shape, land.dtype),
                                jax.ShapeDtypeStruct((8, 128), F32)),
        in_specs=(HBM_ONLY, HBM_ONLY),
        out_specs=(SEM_SPEC,) * 6 + (HBM_ONLY, HBM_ONLY, pl.BlockSpec(memory_space=pltpu.VMEM)),
        input_output_aliases={0: 6, 1: 7},
        compiler_params=pltpu.CompilerParams(has_side_effects=SPLIT_COPY_EFFECT),
    )(pltpu.with_memory_space_constraint(buf, pltpu.HBM), pltpu.with_memory_space_constraint(land, pltpu.HBM))
    return res[:6], res[6], res[7], res[8]


def _gather_late_wait(sems, src_thru, land_thru, after):
    def body(src_ref, land_ref, s0, s1, s2, r0, r1, r2, after_ref, src_dead, got_ref):
        for cp in _late_copies(src_ref, land_ref, (s0, s1, s2), (r0, r1, r2)):
            cp.wait_send()
            cp.wait_recv()

    return pl.pallas_call(
        body, name="gather_late_wait",
        out_shape=(pltpu.HBM(src_thru.shape, src_thru.dtype), pltpu.HBM(land_thru.shape, land_thru.dtype)),
        in_specs=(HBM_ONLY, HBM_ONLY) + (SEM_SPEC,) * 6 + (pl.BlockSpec(memory_space=pl.ANY),),
        out_specs=(HBM_ONLY, HBM_ONLY), input_output_aliases={0: 0, 1: 1},
        compiler_params=pltpu.CompilerParams(has_side_effects=SPLIT_COPY_EFFECT),
    )(src_thru, land_thru, *sems, after)[1]


def _sibling_swap(name, buf):
    def body(src, dst, send, recv):
        x, y, c = lax.axis_index("x"), lax.axis_index("y"), lax.axis_index("c")
        cp = pltpu.make_async_remote_copy(src_ref=src, dst_ref=dst, send_sem=send, recv_sem=recv,
                                          device_id=(x, y, 1 - c), device_id_type=MESH_ID)
        cp.start()
        cp.wait()

    return pl.pallas_call(
        body, out_shape=jax.ShapeDtypeStruct(buf.shape, buf.dtype), in_specs=[HBM_SPEC], out_specs=HBM_SPEC,
        scratch_shapes=[pltpu.SemaphoreType.DMA, pltpu.SemaphoreType.DMA], name=name)(buf)


def _scatter_chips(parts):
    nb = len(parts)

    def body(*refs):
        srcs, dsts = refs[:nb], refs[nb:2 * nb]
        send, recv = refs[2 * nb:]
        x, y, c = lax.axis_index("x"), lax.axis_index("y"), lax.axis_index("c")
        copies = []
        for b in range(nb):
            for k, (fx, fy) in enumerate(CHIP_FLIPS):
                tx, ty = x ^ fx, y ^ fy
                cp = pltpu.make_async_remote_copy(
                    src_ref=srcs[b].at[2 * tx + ty], dst_ref=dsts[b].at[k], send_sem=send.at[b, k],
                    recv_sem=recv.at[b, k], device_id=(tx, ty, c), device_id_type=MESH_ID)
                cp.start()
                copies.append(cp)
        for cp in copies:
            cp.wait()

    return pl.pallas_call(
        body, out_shape=[jax.ShapeDtypeStruct((3,) + p.shape[1:], p.dtype) for p in parts], in_specs=[HBM_SPEC] * nb,
        out_specs=[HBM_SPEC] * nb,
        scratch_shapes=[pltpu.SemaphoreType.DMA((nb, 3)), pltpu.SemaphoreType.DMA((nb, 3))],
        name="scatter_chips")(*parts)


def _add_rows(name, terms, also_bf16=False):
    rows = terms[0].shape[0]
    t = _row_tile(rows)
    n_out = 2 if also_bf16 else 1

    def body(*refs):
        acc = refs[0][...].astype(F32)
        for r in refs[1:-n_out]:
            acc = acc + r[...].astype(F32)
        refs[-n_out][...] = acc
        if also_bf16:
            refs[-1][...] = acc.astype(BF16)

    res = pl.pallas_call(
        body, out_shape=[jax.ShapeDtypeStruct(terms[0].shape, dt) for dt in (F32, BF16)[:n_out]], grid=(rows // t,),
        in_specs=[_rb(t, PACK_W)] * len(terms), out_specs=[_rb(t, PACK_W)] * n_out,
        name=name, compiler_params=_params(("parallel",)))(*terms)
    return res if also_bf16 else res[0]


def _sum_chip_order(name, parts, received, me):
    stack = jnp.concatenate([lax.dynamic_index_in_dim(parts, me, axis=0, keepdims=True), received], axis=0)
    terms = []
    for chip in range(N_CHIPS):
        xr = me ^ chip
        where = jnp.where(xr == 0, 0, jnp.where(xr == 2, 1, jnp.where(xr == 1, 2, 3)))
        terms.append(lax.dynamic_index_in_dim(stack, where, axis=0, keepdims=False))
    return _add_rows(name, terms)


def _row_tile(rows):
    assert rows % 16 == 0, rows
    return max(t for t in range(16, 513, 16) if rows % t == 0)


def _adamw(name, g, w, m, v):
    rows, cols = g.shape
    t = max(tt for tt in range(8, 257, 8) if rows % tt == 0)
    c1 = 1.0 - ADAM_B1 ** ADAM_STEP
    c2 = 1.0 - ADAM_B2 ** ADAM_STEP

    def body(g_ref, w_ref, m_ref, v_ref, d_ref, mo_ref, vo_ref):
        gg = g_ref[...]
        mn = ADAM_B1 * m_ref[...] + (1.0 - ADAM_B1) * gg
        vn = ADAM_B2 * v_ref[...] + (1.0 - ADAM_B2) * (gg * gg)
        d_ref[...] = -ADAM_LR * ((mn / c1) / (jnp.sqrt(vn / c2) + ADAM_EPS) + ADAM_WD * w_ref[...])
        mo_ref[...] = mn
        vo_ref[...] = vn

    return pl.pallas_call(
        body, out_shape=[jax.ShapeDtypeStruct(g.shape, F32)] * 3, grid=(rows // t,),
        in_specs=[_rb(t, cols)] * 4, out_specs=[_rb(t, cols)] * 3,
        name=name, compiler_params=_params(("parallel",)))(g, w, m, v)


def kernel(x, meta_tokens, norm_mix_pre, w_in, q_a_norm, w_q_up, kv_a_norm, w_kv_up, conv_w, conv_b, dt_bias, a_log, d_skip, ssm_norm, w_out, norm_mix_post, norm_mlp_pre, w_mlp_up, w_mlp_down, norm_mlp_post, loss_target, m_meta_tokens, m_norm_mix_pre, m_w_in, m_q_a_norm, m_w_q_up, m_kv_a_norm, m_w_kv_up, m_conv_w, m_conv_b, m_dt_bias, m_a_log, m_d_skip, m_ssm_norm, m_w_out, m_norm_mix_post, m_norm_mlp_pre, m_w_mlp_up, m_w_mlp_down, m_norm_mlp_post, v_meta_tokens, v_norm_mix_pre, v_w_in, v_q_a_norm, v_w_q_up, v_kv_a_norm, v_w_kv_up, v_conv_w, v_conv_b, v_dt_bias, v_a_log, v_d_skip, v_ssm_norm, v_w_out, v_norm_mix_post, v_norm_mlp_pre, v_w_mlp_up, v_w_mlp_down, v_norm_mlp_post):
    given = dict(locals())
    drop = lambda name, a: a[0] if a.ndim == 3 else a
    w_loc = {n: drop(n, given[n]) for n in WEIGHTS}
    m_loc = {n: drop(n, given["m_" + n]) for n in WEIGHTS}
    v_loc = {n: drop(n, given["v_" + n]) for n in WEIGHTS}
    ix, iy, ic = lax.axis_index("x"), lax.axis_index("y"), lax.axis_index("c")
    me = 2 * ix + iy

    def assemble(names, got, own_slot=None):
        per_chip = [_unpack_rows(got[t], [_shard_shape(n) for n in names]) for t in range(N_CHIPS)]
        full = {}
        for k, n in enumerate(names):
            pieces = [per_chip[t][k] for t in range(N_CHIPS)]
            if own_slot is not None:
                pieces = [jnp.where(me == t, own_slot[n], pieces[t]) for t in range(N_CHIPS)]
            full[n] = jnp.concatenate(pieces, axis=SHARD_AXIS[n])
        return full

    bf16_of = lambda names: _pack_rows([w_loc[n].astype(BF16) for n in names], 32)
    sems, src_thru, land_thru, token = _gather_late_start(bf16_of(GATHER_LATE))
    got16, got32 = _gather_chips([bf16_of(GATHER_EARLY), _pack_rows([w_loc[n] for n in GATHER_F32], 16)])
    w_full = {n: w_loc[n] for n in WEIGHTS if n not in SHARD_AXIS}
    w_full.update(assemble(GATHER_EARLY, got16))
    w_full.update(assemble(GATHER_F32, got32))
    w_full["token"] = token
    own16 = {n: w_loc[n].astype(BF16) for n in GATHER_LATE}
    w_full["late_weights"] = lambda after: assemble(GATHER_LATE, _gather_late_wait(sems, src_thru, land_thru, after),
                                                    own_slot=own16)

    loss, dh0, g_full = _local_step(x[0], loss_target[0], w_full)
    n_real = x.shape[1]
    grad_x = dh0[N_META:N_META + n_real][None]

    shapes = [_shard_shape(n) for n in PACK_ORDER]
    n_sharded = len(SHARD_AXIS)
    slots = [_pack_rows([_chip_slice(g_full[n], n, t) for n in PACK_ORDER], SLOT_ROWS_MULTIPLE, fill_before=n_sharded)
             for t in range(N_CHIPS)]
    rows = slots[0].shape[0]
    half = rows // 2
    halves = lambda hh: jnp.concatenate([lax.dynamic_slice_in_dim(s, hh * half, half, axis=0) for s in slots], axis=0)
    keep, give = halves(ic), halves(1 - ic)
    from_sibling = _sibling_swap("sibling_swap", give)
    part32, part16 = _add_rows("chip_partial", [keep, from_sibling], also_bf16=True)
    part32 = part32.reshape(N_CHIPS, half, PACK_W)
    part16 = part16.reshape(N_CHIPS, half, PACK_W)
    assert all(n in SHARD_AXIS for n in PACK_ORDER[:n_sharded])
    replicated_rows = sum(_packed_rows(_shard_shape(n)) for n in PACK_ORDER[n_sharded:])
    assert replicated_rows <= TAIL_ROWS <= half
    tail32 = part32[:, half - TAIL_ROWS:, :]
    from16, from_tail = _scatter_chips([part16, tail32])
    my_half = _sum_chip_order("chip_total", part16, from16, me)
    my_tail = _sum_chip_order("chip_total_tail", tail32, from_tail, me)
    my_half = lax.dynamic_update_slice(my_half, my_tail, (half - TAIL_ROWS, 0))
    other_half = _sibling_swap("sibling_gather", my_half)
    g_red = jnp.where(ic == 0, jnp.concatenate([my_half, other_half], axis=0),
                      jnp.concatenate([other_half, my_half], axis=0))

    g_own = dict(zip(PACK_ORDER, _unpack_rows(g_red, shapes, SLOT_ROWS_MULTIPLE, n_sharded)))
    small = [n for n in PACK_ORDER if n not in ADAM_NATURAL]
    pack_small = lambda d: _pack_rows([d[n] for n in small], 16)
    packed_upd = _adamw("adamw_small", pack_small(g_own), pack_small(w_loc), pack_small(m_loc), pack_small(v_loc))
    upd = [dict(zip(small, _unpack_rows(p, [_shard_shape(n) for n in small]))) for p in packed_upd]
    for n in ADAM_NATURAL:
        for k, res in enumerate(_adamw("adamw_" + n, g_own[n], w_loc[n], m_loc[n], v_loc[n])):
            upd[k][n] = res

    def outputs(parts):
        return [parts[n][None] if given[n].ndim == 3 else parts[n] for n in WEIGHTS]

    total = lax.psum(loss[0, 0], ("x", "y", "c"))
    return (total, grad_x, *outputs(g_own), *outputs(upd[0]), *outputs(upd[1]), *outputs(upd[2]))
```

```python
import functools

import numpy as np
import jax
import jax.numpy as jnp
from jax import lax
from jax.experimental import pallas as pl
from jax.experimental.pallas import tpu as pltpu

F32 = jnp.float32
BF16 = jnp.bfloat16

D_MODEL = 1024
N_META = 16
EPS = 1e-6
ATT_HEADS = 8
Q_LORA = 384
KV_LORA = 256
QK_NOPE = 128
QK_ROPE = 64
V_HEAD = 128
ROPE_THETA = 10000.0
SSM_HEADS = 16
SSM_HEAD_DIM = 64
SSM_WIDTH = 1024
SSM_STATE = 128
CONV_K = 4
D_FF = 4096
ATT_SCALE = float((QK_NOPE + QK_ROPE) ** -0.5)
ATT_SCALE_LOG2 = float(ATT_SCALE * np.log2(np.e))

ADAM_LR = 0.001
ADAM_B1 = 0.9
ADAM_B2 = 0.999
ADAM_EPS = 1e-08
ADAM_WD = 0.01
ADAM_STEP = 10

SEG_KV, SEG_KR, SEG_CQ, SEG_DT, SEG_Z, SEG_XS, SEG_BC = 0, 256, 384, 768, 1024, 2048, 3072
PROJ_W = 3584
QP_W = 256

ROW_BLOCK = 768
LOSS_ROW_BLOCK = 512
MM_BLOCK = 512
SSD_CHUNK = 256
ATT_BLOCK = 768
ATT_BLOCK_FWD = 1536
ATT_BLOCK_Q_BWD = 1536
VMEM_LIMIT = 56 * 1024 * 1024
NEG_BIG = -1e30

PACK_W = 1024
TAIL_ROWS = 96
SLOT_ROWS_MULTIPLE = 256
N_CHIPS = 4
MESH_ID = pl.DeviceIdType.MESH


def _params(sem):
    return pltpu.CompilerParams(dimension_semantics=sem, vmem_limit_bytes=VMEM_LIMIT)


def _rb(rows, width, cb=0):
    return pl.BlockSpec((rows, width), lambda i: (i, cb))


def _full(shape):
    zeros = (0,) * len(shape)
    return pl.BlockSpec(shape, lambda i: zeros)


def _acc_add(ref, val):
    first = pl.program_id(0) == 0

    @pl.when(first)
    def _():
        ref[...] = val

    @pl.when(jnp.logical_not(first))
    def _():
        ref[...] += val


def _rms(x, g):
    r = lax.rsqrt(jnp.mean(x * x, axis=-1, keepdims=True) + EPS)
    return x * r * g


def _rms_bwd(x, g, dy):
    r = lax.rsqrt(jnp.mean(x * x, axis=-1, keepdims=True) + EPS)
    dyg = dy * g
    dx = r * dyg - x * (r * r * r) * jnp.mean(x * dyg, axis=-1, keepdims=True)
    dg = jnp.sum(dy * x * r, axis=0, keepdims=True)
    return dx, dg


def _sigmoid(x):
    return 1.0 / (1.0 + jnp.exp(-x))


def _swap32(x):
    lane = lax.broadcasted_iota(jnp.int32, x.shape, 1)
    return jnp.where((lane % 64) < 32, pltpu.roll(x, 96, 1), pltpu.roll(x, 32, 1))


def _rope(x, cos_t, sin_t):
    return x * cos_t + _swap32(x) * sin_t


def _rope_bwd(dr, cos_t, sin_t):
    return dr * cos_t + _swap32(dr * sin_t)


def _tile(n, cap):
    if n <= cap:
        return n
    best = 128
    for t in range(128, cap + 1, 128):
        if n % t == 0:
            best = t
    assert n % best == 0, (n, cap)
    return best


MM_VMEM_BUDGET = 40 * 1024 * 1024
TN_ROWS_CAP = 1536
MM_ROW_BLOCKS = (1536, 768)


def _mm(name, a, b, outs=((F32, None),), epi_ins=(), trans_b=False):
    a_parts = tuple(a) if isinstance(a, (tuple, list)) else (a,)
    assert len(a_parts) == 1 or not trans_b
    m = a_parts[0].shape[0]
    widths = [p.shape[1] for p in a_parts]
    k = sum(widths)
    n = b.shape[0] if trans_b else b.shape[1]
    n_a, n_epi = len(a_parts), len(epi_ins)
    out_bytes = sum(jnp.dtype(dt).itemsize for dt, _ in outs) + sum(e.dtype.itemsize for e in epi_ins)
    a_bytes = sum(p.shape[1] * p.dtype.itemsize for p in a_parts)
    step_bytes = lambda tm, tn: 2 * (tm * a_bytes + k * tn * b.dtype.itemsize + tm * tn * out_bytes)
    tm, tn = MM_BLOCK, n
    while step_bytes(tm, tn) > MM_VMEM_BUDGET and tn % 256 == 0:
        tn //= 2
    assert n % tn == 0 and step_bytes(tm, tn) <= MM_VMEM_BUDGET, (name, n, tn)
    for rows in MM_ROW_BLOCKS:
        if m % rows == 0 and tn == n and step_bytes(rows, n) + 4 * rows * n <= MM_VMEM_BUDGET:
            tm = rows
            break

    def body(*refs):
        a_refs, b_ref = refs[:n_a], refs[n_a]
        epi_refs = refs[n_a + 1:n_a + 1 + n_epi]
        out_refs = refs[n_a + 1 + n_epi:]
        if trans_b:
            r = _nt(a_refs[0][...].astype(BF16), b_ref[...].astype(BF16))
        else:
            r, off = None, 0
            for a_ref, wd in zip(a_refs, widths):
                part = jnp.dot(a_ref[...].astype(BF16), b_ref[off:off + wd, :].astype(BF16),
                               preferred_element_type=F32)
                r = part if r is None else r + part
                off += wd
        blocks = [e[...] for e in epi_refs]
        for o_ref, (dt, fn) in zip(out_refs, outs):
            o_ref[...] = (r if fn is None else fn(r, *blocks)).astype(dt)

    out_spec = pl.BlockSpec((tm, tn), lambda j, i: (i, j))
    b_spec = pl.BlockSpec((tn, k), lambda j, i: (j, 0)) if trans_b else pl.BlockSpec((k, tn), lambda j, i: (0, j))
    res = pl.pallas_call(
        body,
        out_shape=[jax.ShapeDtypeStruct((m, n), dt) for dt, _ in outs],
        grid=(n // tn, m // tm),
        in_specs=[pl.BlockSpec((tm, wd), lambda j, i: (i, 0)) for wd in widths] + [b_spec] + [out_spec] * n_epi,
        out_specs=[out_spec] * len(outs),
        name=name,
        compiler_params=_params(("parallel", "parallel")),
    )(*a_parts, b, *epi_ins)
    return res[0] if len(outs) == 1 else res


def _mm_tn(name, x, dy, ta_cap=1024, tn_cap=1024):
    l, a = x.shape
    n = dy.shape[1]
    ta, tn = _tile(a, ta_cap), _tile(n, tn_cap)
    tl = max(t for t in range(MM_BLOCK, TN_ROWS_CAP + 1, MM_BLOCK) if l % t == 0)
    nl = l // tl

    def body(x_ref, dy_ref, o_ref):
        ll = pl.program_id(2)

        @pl.when(ll == 0)
        def _():
            o_ref[...] = jnp.zeros_like(o_ref)

        o_ref[...] += lax.dot_general(x_ref[...].astype(BF16), dy_ref[...].astype(BF16), (((0,), (0,)), ((), ())),
                                      preferred_element_type=F32)

    return pl.pallas_call(
        body,
        out_shape=jax.ShapeDtypeStruct((a, n), F32),
        grid=(a // ta, n // tn, nl),
        in_specs=[pl.BlockSpec((tl, ta), lambda i, j, ll: (ll, i)), pl.BlockSpec((tl, tn), lambda i, j, ll: (ll, j))],
        out_specs=pl.BlockSpec((ta, tn), lambda i, j, ll: (i, j)),
        name=name,
        compiler_params=_params(("parallel", "parallel", "arbitrary")),
    )(x, dy)


def _norm_in(h0, g_pre):
    lp = h0.shape[0]
    t = ROW_BLOCK

    def body(h_ref, g_ref, o_ref):
        o_ref[...] = _rms(h_ref[...], g_ref[...]).astype(BF16)

    return pl.pallas_call(
        body, out_shape=jax.ShapeDtypeStruct((lp, D_MODEL), BF16), grid=(lp // t,),
        in_specs=[_rb(t, D_MODEL), _full((1, D_MODEL))], out_specs=_rb(t, D_MODEL),
        name="norm_in", compiler_params=_params(("parallel",)))(h0, g_pre)


def _attn_prep(proj, g_q, g_kv, cos_t, sin_t):
    lp = proj.shape[0]
    t = ROW_BLOCK

    def body(ckv_ref, kr_ref, cq_ref, gq_ref, gkv_ref, cos_ref, sin_ref, cqn_ref, ckvn_ref, krr_ref):
        cqn_ref[...] = _rms(cq_ref[...], gq_ref[...]).astype(BF16)
        ckvn_ref[...] = _rms(ckv_ref[...], gkv_ref[...]).astype(BF16)
        roped = _rope(kr_ref[...], cos_ref[...], sin_ref[...])
        krr_ref[...] = roped + pltpu.roll(roped, 64, 1)

    return pl.pallas_call(
        body,
        out_shape=[jax.ShapeDtypeStruct((lp, Q_LORA), BF16), jax.ShapeDtypeStruct((lp, KV_LORA), BF16),
                   jax.ShapeDtypeStruct((lp, 128), F32)],
        grid=(lp // t,),
        in_specs=[_rb(t, KV_LORA, SEG_KV // KV_LORA), _rb(t, 128, SEG_KR // 128), _rb(t, Q_LORA, SEG_CQ // Q_LORA),
                  _full((1, Q_LORA)), _full((1, KV_LORA)), _rb(t, 128), _rb(t, 128)],
        out_specs=[_rb(t, Q_LORA), _rb(t, KV_LORA), _rb(t, 128)],
        name="attn_prep", compiler_params=_params(("parallel",)))(proj, proj, proj, g_q, g_kv, cos_t, sin_t)


def _qk_pack(q, kv, krr, cos_t, sin_t):
    lp = q.shape[0]
    t = ROW_BLOCK

    def body(q_ref, kv_ref, krr_ref, cos_ref, sin_ref, qs_ref, ks_ref, vs_ref, vts_ref):
        lane = lax.broadcasted_iota(jnp.int32, (t, 128), 1)
        lo = lane < 64
        krr = krr_ref[...].astype(BF16)
        for j in range(ATT_HEADS // 2):
            pr = _rope(q_ref[:, 1024 + 128 * j:1024 + 128 * (j + 1)], cos_ref[...], sin_ref[...])
            for h, keep in ((2 * j, lo), (2 * j + 1, jnp.logical_not(lo))):
                qs_ref[h, :, 0:128] = q_ref[:, 128 * h:128 * (h + 1)].astype(BF16)
                qs_ref[h, :, 128:256] = jnp.where(keep, pr, 0.0).astype(BF16)
        for h in range(ATT_HEADS):
            ks_ref[h, :, 0:128] = kv_ref[:, 256 * h:256 * h + 128].astype(BF16)
            ks_ref[h, :, 128:256] = krr
            v = kv_ref[:, 256 * h + 128:256 * (h + 1)]
            vs_ref[h] = v.astype(BF16)
            vts_ref[h] = v.astype(F32).T.astype(BF16)

    slab = lambda w: pl.BlockSpec((ATT_HEADS, t, w), lambda i: (0, i, 0))
    return pl.pallas_call(
        body,
        out_shape=[jax.ShapeDtypeStruct((ATT_HEADS, lp, QP_W), BF16), jax.ShapeDtypeStruct((ATT_HEADS, lp, QP_W), BF16),
                   jax.ShapeDtypeStruct((ATT_HEADS, lp, V_HEAD), BF16), jax.ShapeDtypeStruct((ATT_HEADS, V_HEAD, lp), BF16)],
        grid=(lp // t,),
        in_specs=[_rb(t, 1536), _rb(t, 2048), _rb(t, 128), _rb(t, 128), _rb(t, 128)],
        out_specs=[slab(QP_W), slab(QP_W), slab(V_HEAD), pl.BlockSpec((ATT_HEADS, V_HEAD, t), lambda i: (0, 0, i))],
        name="qk_pack", compiler_params=_params(("parallel",)))(q, kv, krr, cos_t, sin_t)


def _shifted(ext, t, shift):
    if shift == 0:
        return ext[8:, :]
    return pltpu.roll(ext, shift, 0)[8:, :]


def _conv_fwd(name, proj, seg, width, conv_w, conv_b):
    lp = proj.shape[0]
    t = ROW_BLOCK
    cb = seg // width

    def body(u_ref, halo_ref, w_ref, b_ref, pre_ref, act_ref):
        i = pl.program_id(0)
        u = u_ref[...]
        halo = jnp.where(i > 0, halo_ref[...], 0.0)
        ext = jnp.concatenate([halo, u], axis=0)
        pre = jnp.broadcast_to(b_ref[...], (t, width))
        for k in range(CONV_K):
            pre = pre + w_ref[k:k + 1, :] * _shifted(ext, t, CONV_K - 1 - k)
        pre_ref[...] = pre
        act_ref[...] = pre * _sigmoid(pre)

    return pl.pallas_call(
        body,
        out_shape=[jax.ShapeDtypeStruct((lp, width), F32)] * 2,
        grid=(lp // t,),
        in_specs=[_rb(t, width, cb),
                  pl.BlockSpec((8, width), lambda i: (jnp.maximum(i * (t // 8) - 1, 0), cb)),
                  _full((CONV_K, width)), _full((1, width))],
        out_specs=[_rb(t, width), _rb(t, width)],
        name=name, compiler_params=_params(("parallel",)))(proj, proj, conv_w, conv_b)


def _softplus(x):
    return jnp.maximum(x, 0.0) + jnp.log1p(jnp.exp(-jnp.abs(x)))


def _dt_fwd(proj, dt_bias_pad):
    lp = proj.shape[0]
    t = ROW_BLOCK

    def body(x_ref, b_ref, o_ref):
        o_ref[...] = _softplus(x_ref[...] + b_ref[...])

    return pl.pallas_call(
        body, out_shape=jax.ShapeDtypeStruct((lp, 128), F32), grid=(lp // t,),
        in_specs=[_rb(t, 128, SEG_DT // 128), _full((1, 128))], out_specs=_rb(t, 128),
        name="dt_fwd", compiler_params=_params(("parallel",)))(proj, dt_bias_pad)


def _gated_norm_group(y, z, w):
    g = y * (z * _sigmoid(z))
    return g * lax.rsqrt(jnp.mean(g * g, axis=-1, keepdims=True) + EPS) * w


def _gated_norm_fwd(y, proj, w):
    lp = y.shape[0]
    t = ROW_BLOCK
    gw = SSM_WIDTH // 2

    def body(y0, y1, z0, z1, w0, w1, o_ref):
        o_ref[:, 0:gw] = _gated_norm_group(y0[...], z0[...], w0[...]).astype(BF16)
        o_ref[:, gw:] = _gated_norm_group(y1[...], z1[...], w1[...]).astype(BF16)

    zb = SEG_Z // gw
    return pl.pallas_call(
        body, out_shape=jax.ShapeDtypeStruct((lp, SSM_WIDTH), BF16), grid=(lp // t,),
        in_specs=[_rb(t, gw, 0), _rb(t, gw, 1), _rb(t, gw, zb), _rb(t, gw, zb + 1),
                  pl.BlockSpec((1, gw), lambda i: (0, 0)), pl.BlockSpec((1, gw), lambda i: (0, 1))],
        out_specs=_rb(t, SSM_WIDTH),
        name="gated_norm_fwd", compiler_params=_params(("parallel",)))(y, y, proj, proj, w, w)


def _gated_norm_bwd(y, proj, w, dssm):
    lp = y.shape[0]
    t = ROW_BLOCK
    gw = SSM_WIDTH // 2

    def body(y0, y1, z0, z1, w0, w1, d0, d1, dy_ref, dz_ref, dw_ref):
        dws = []
        for g, (yr, zr, wr, dr) in enumerate(((y0, z0, w0, d0), (y1, z1, w1, d1))):
            _, vjp = jax.vjp(_gated_norm_group, yr[...], zr[...], wr[...])
            dyg, dzg, dwg = vjp(dr[...])
            dy_ref[:, g * gw:(g + 1) * gw] = dyg
            dz_ref[:, g * gw:(g + 1) * gw] = dzg
            dws.append(dwg)
        _acc_add(dw_ref, jnp.concatenate(dws, axis=1))

    zb = SEG_Z // gw
    return pl.pallas_call(
        body,
        out_shape=[jax.ShapeDtypeStruct((lp, SSM_WIDTH), F32), jax.ShapeDtypeStruct((lp, SSM_WIDTH), F32),
                   jax.ShapeDtypeStruct((1, SSM_WIDTH), F32)],
        grid=(lp // t,),
        in_specs=[_rb(t, gw, 0), _rb(t, gw, 1), _rb(t, gw, zb), _rb(t, gw, zb + 1),
                  pl.BlockSpec((1, gw), lambda i: (0, 0)), pl.BlockSpec((1, gw), lambda i: (0, 1)),
                  _rb(t, gw, 2), _rb(t, gw, 3)],
        out_specs=[_rb(t, SSM_WIDTH), _rb(t, SSM_WIDTH), _full((1, SSM_WIDTH))],
        name="gated_norm_bwd", compiler_params=_params(("arbitrary",)))(y, y, proj, proj, w, w, dssm, dssm)


def _mix_residual(h0, mix, g_post, g_mlp_pre):
    lp = h0.shape[0]
    t = ROW_BLOCK

    def body(h_ref, m_ref, gp_ref, gm_ref, h1_ref, n2_ref):
        h1 = h_ref[...] + _rms(m_ref[...], gp_ref[...])
        h1_ref[...] = h1
        n2_ref[...] = _rms(h1, gm_ref[...]).astype(BF16)

    return pl.pallas_call(
        body, out_shape=[jax.ShapeDtypeStruct((lp, D_MODEL), F32), jax.ShapeDtypeStruct((lp, D_MODEL), BF16)],
        grid=(lp // t,),
        in_specs=[_rb(t, D_MODEL), _rb(t, D_MODEL), _full((1, D_MODEL)), _full((1, D_MODEL))],
        out_specs=[_rb(t, D_MODEL), _rb(t, D_MODEL)],
        name="mix_residual", compiler_params=_params(("parallel",)))(h0, mix, g_post, g_mlp_pre)


def _loss_and_grad(h1, f, g_post, tgt, n_real):
    lp = h1.shape[0]
    t = LOSS_ROW_BLOCK
    assert n_real % t == 0 and t % N_META == 0 and lp % t == 0
    n_tb = n_real // t

    def body(h1_ref, f_ref, g_ref, halo_ref, t_ref, loss_ref, dh2_ref, df_ref, dg_ref):
        i = pl.program_id(0)
        fx = f_ref[...]
        h2 = h1_ref[...] + _rms(fx, g_ref[...])
        row = i * t + lax.broadcasted_iota(jnp.int32, (t, 1), 0)
        real = jnp.logical_and(row >= N_META, row < N_META + n_real)
        target = jnp.concatenate([halo_ref[...], t_ref[0:t - N_META, :]], axis=0)
        diff = jnp.where(real, h2 - target, 0.0)
        part = 0.5 * jnp.sum(jnp.sum(diff * diff, axis=-1, keepdims=True) / D_MODEL, axis=0, keepdims=True)
        _acc_add(loss_ref, jnp.broadcast_to(part, (1, 128)))
        dh2 = diff / D_MODEL
        dh2_ref[...] = dh2
        dfx, dg = _rms_bwd(fx, g_ref[...], dh2)
        df_ref[...] = dfx.astype(BF16)
        _acc_add(dg_ref, dg)

    return pl.pallas_call(
        body,
        out_shape=[jax.ShapeDtypeStruct((1, 128), F32), jax.ShapeDtypeStruct((lp, D_MODEL), F32),
                   jax.ShapeDtypeStruct((lp, D_MODEL), BF16), jax.ShapeDtypeStruct((1, D_MODEL), F32)],
        grid=(lp // t,),
        in_specs=[_rb(t, D_MODEL), _rb(t, D_MODEL), _full((1, D_MODEL)),
                  pl.BlockSpec((N_META, D_MODEL),
                               lambda i: (jnp.clip(i * (t // N_META) - 1, 0, n_real // N_META - 1), 0)),
                  pl.BlockSpec((t, D_MODEL), lambda i: (jnp.minimum(i, n_tb - 1), 0))],
        out_specs=[_full((1, 128)), _rb(t, D_MODEL), _rb(t, D_MODEL), _full((1, D_MODEL))],
        name="loss_and_grad", compiler_params=_params(("arbitrary",)))(h1, f, g_post, tgt, tgt)


def _mlp_residual_bwd(dh2, dn2, h1, g_mlp_pre, mix, g_post):
    lp = h1.shape[0]
    t = ROW_BLOCK

    def body(dh2_ref, dn2_ref, h1_ref, gm_ref, mix_ref, gp_ref, dh1_ref, dmix_ref, dgm_ref, dgp_ref):
        dx, dgm = _rms_bwd(h1_ref[...], gm_ref[...], dn2_ref[...])
        dh1 = dh2_ref[...] + dx
        dh1_ref[...] = dh1
        dmix, dgp = _rms_bwd(mix_ref[...], gp_ref[...], dh1)
        dmix_ref[...] = dmix.astype(BF16)
        _acc_add(dgm_ref, dgm)
        _acc_add(dgp_ref, dgp)

    return pl.pallas_call(
        body,
        out_shape=[jax.ShapeDtypeStruct((lp, D_MODEL), F32), jax.ShapeDtypeStruct((lp, D_MODEL), BF16),
                   jax.ShapeDtypeStruct((1, D_MODEL), F32), jax.ShapeDtypeStruct((1, D_MODEL), F32)],
        grid=(lp // t,),
        in_specs=[_rb(t, D_MODEL), _rb(t, D_MODEL), _rb(t, D_MODEL), _full((1, D_MODEL)), _rb(t, D_MODEL),
                  _full((1, D_MODEL))],
        out_specs=[_rb(t, D_MODEL), _rb(t, D_MODEL), _full((1, D_MODEL)), _full((1, D_MODEL))],
        name="mlp_residual_bwd", compiler_params=_params(("arbitrary",)))(dh2, dn2, h1, g_mlp_pre, mix, g_post)


def _input_norm_bwd(dh1, dn1, h0, g_pre):
    lp = h0.shape[0]
    t = ROW_BLOCK

    def body(dh1_ref, dn1_ref, h0_ref, g_ref, dh0_ref, dg_ref):
        dx, dg = _rms_bwd(h0_ref[...], g_ref[...], dn1_ref[...])
        dh0_ref[...] = dh1_ref[...] + dx
        _acc_add(dg_ref, dg)

    return pl.pallas_call(
        body, out_shape=[jax.ShapeDtypeStruct((lp, D_MODEL), F32), jax.ShapeDtypeStruct((1, D_MODEL), F32)],
        grid=(lp // t,),
        in_specs=[_rb(t, D_MODEL), _rb(t, D_MODEL), _rb(t, D_MODEL), _full((1, D_MODEL))],
        out_specs=[_rb(t, D_MODEL), _full((1, D_MODEL))],
        name="input_norm_bwd", compiler_params=_params(("arbitrary",)))(dh1, dn1, h0, g_pre)


def _conv_bwd(name, dact, pre, proj, seg, width, conv_w):
    lp = proj.shape[0]
    t = ROW_BLOCK
    cb = seg // width
    nblk = lp // t

    def dsilu(p):
        s = _sigmoid(p)
        return s * (1.0 + p * (1.0 - s))

    def body(da_ref, dan_ref, pre_ref, pren_ref, u_ref, halo_ref, w_ref, du_ref, dw_ref, db_ref):
        i = pl.program_id(0)
        dpre = da_ref[...] * dsilu(pre_ref[...])
        dpre_next = jnp.where(i < nblk - 1, dan_ref[...] * dsilu(pren_ref[...]), 0.0)
        extd = jnp.concatenate([dpre, dpre_next], axis=0)
        halo = jnp.where(i > 0, halo_ref[...], 0.0)
        ext = jnp.concatenate([halo, u_ref[...]], axis=0)
        du = jnp.zeros((t, width), F32)
        dws = []
        for k in range(CONV_K):
            m = CONV_K - 1 - k
            ahead = dpre if m == 0 else pltpu.roll(extd, t + 8 - m, 0)[:t, :]
            du = du + w_ref[k:k + 1, :] * ahead
            dws.append(jnp.sum(dpre * _shifted(ext, t, m), axis=0, keepdims=True))
        du_ref[...] = du
        _acc_add(dw_ref, jnp.concatenate(dws + [jnp.zeros((8 - CONV_K, width), F32)], axis=0))
        _acc_add(db_ref, jnp.sum(dpre, axis=0, keepdims=True))

    nxt = lambda i: (jnp.minimum((i + 1) * (t // 8), lp // 8 - 1), 0)
    return pl.pallas_call(
        body,
        out_shape=[jax.ShapeDtypeStruct((lp, width), F32), jax.ShapeDtypeStruct((8, width), F32),
                   jax.ShapeDtypeStruct((1, width), F32)],
        grid=(nblk,),
        in_specs=[_rb(t, width), pl.BlockSpec((8, width), nxt), _rb(t, width), pl.BlockSpec((8, width), nxt),
                  _rb(t, width, cb),
                  pl.BlockSpec((8, width), lambda i: (jnp.maximum(i * (t // 8) - 1, 0), cb)),
                  _full((CONV_K, width))],
        out_specs=[_rb(t, width), _full((8, width)), _full((1, width))],
        name=name, compiler_params=_params(("arbitrary",)))(dact, dact, pre, pre, proj, proj, conv_w)


def _qk_unpack_bwd(dqs, dks, dvs, cos_t, sin_t):
    lp = dqs.shape[1]
    t = ROW_BLOCK

    def body(dqs_ref, dks_ref, dvs_ref, cos_ref, sin_ref, dq_ref, dkv_ref, dkr_ref):
        lane = lax.broadcasted_iota(jnp.int32, (t, 128), 1)
        lo = lane < 64
        for j in range(ATT_HEADS // 2):
            dpr = jnp.where(lo, dqs_ref[2 * j, :, 128:256], dqs_ref[2 * j + 1, :, 128:256])
            dq_ref[:, 1024 + 128 * j:1024 + 128 * (j + 1)] = _rope_bwd(dpr, cos_ref[...], sin_ref[...]).astype(BF16)
        dkrr = jnp.zeros((t, 128), F32)
        for h in range(ATT_HEADS):
            dq_ref[:, 128 * h:128 * (h + 1)] = dqs_ref[h, :, 0:128].astype(BF16)
            dkv_ref[:, 256 * h:256 * h + 128] = dks_ref[h, :, 0:128].astype(BF16)
            dkv_ref[:, 256 * h + 128:256 * (h + 1)] = dvs_ref[h].astype(BF16)
            dkrr = dkrr + dks_ref[h, :, 128:256]
        droped = jnp.where(lo, dkrr + pltpu.roll(dkrr, 64, 1), 0.0)
        dkr_ref[...] = _rope_bwd(droped, cos_ref[...], sin_ref[...])

    slab = lambda w: pl.BlockSpec((ATT_HEADS, t, w), lambda i: (0, i, 0))
    return pl.pallas_call(
        body,
        out_shape=[jax.ShapeDtypeStruct((lp, 1536), BF16), jax.ShapeDtypeStruct((lp, 2048), BF16),
                   jax.ShapeDtypeStruct((lp, 128), F32)],
        grid=(lp // t,),
        in_specs=[slab(QP_W), slab(QP_W), slab(V_HEAD), _rb(t, 128), _rb(t, 128)],
        out_specs=[_rb(t, 1536), _rb(t, 2048), _rb(t, 128)],
        name="qk_unpack_bwd", compiler_params=_params(("parallel",)))(dqs, dks, dvs, cos_t, sin_t)


def _proj_grad(proj, dcqn, dckvn, g_q, g_kv, dkr, ddt_pad, dt_bias_pad, dz, dxs, dbc):
    lp = proj.shape[0]
    t = ROW_BLOCK

    def body(ckv_ref, cq_ref, pdt_ref, dcq_ref, dckv_ref, gq_ref, gkv_ref, dkr_ref, ddt_ref, b_ref, dz_ref, dxs_ref,
             dbc_ref, dp_ref, dgq_ref, dgkv_ref, db_ref):
        dckv, dgkv = _rms_bwd(ckv_ref[...], gkv_ref[...], dckv_ref[...])
        dcq, dgq = _rms_bwd(cq_ref[...], gq_ref[...], dcq_ref[...])
        ddt_raw = ddt_ref[...] * _sigmoid(pdt_ref[...] + b_ref[...])
        dp_ref[:, SEG_KV:SEG_KV + KV_LORA] = dckv.astype(BF16)
        dp_ref[:, SEG_KR:SEG_KR + 128] = dkr_ref[...].astype(BF16)
        dp_ref[:, SEG_CQ:SEG_CQ + Q_LORA] = dcq.astype(BF16)
        dp_ref[:, SEG_DT:SEG_DT + 128] = ddt_raw.astype(BF16)
        dp_ref[:, SEG_DT + 128:SEG_Z] = jnp.zeros((t, SEG_Z - SEG_DT - 128), BF16)
        dp_ref[:, SEG_Z:SEG_XS] = dz_ref[...].astype(BF16)
        dp_ref[:, SEG_XS:SEG_BC] = dxs_ref[...].astype(BF16)
        dp_ref[:, SEG_BC:PROJ_W] = dbc_ref[...].astype(BF16)
        _acc_add(dgq_ref, dgq)
        _acc_add(dgkv_ref, dgkv)
        _acc_add(db_ref, jnp.sum(ddt_raw, axis=0, keepdims=True))

    return pl.pallas_call(
        body,
        out_shape=[jax.ShapeDtypeStruct((lp, PROJ_W), BF16), jax.ShapeDtypeStruct((1, Q_LORA), F32),
                   jax.ShapeDtypeStruct((1, KV_LORA), F32), jax.ShapeDtypeStruct((1, 128), F32)],
        grid=(lp // t,),
        in_specs=[_rb(t, KV_LORA, SEG_KV // KV_LORA), _rb(t, Q_LORA, SEG_CQ // Q_LORA), _rb(t, 128, SEG_DT // 128),
                  _rb(t, Q_LORA), _rb(t, KV_LORA), _full((1, Q_LORA)), _full((1, KV_LORA)), _rb(t, 128), _rb(t, 128),
                  _full((1, 128)), _rb(t, SSM_WIDTH), _rb(t, SSM_WIDTH), _rb(t, 512)],
        out_specs=[_rb(t, PROJ_W), _full((1, Q_LORA)), _full((1, KV_LORA)), _full((1, 128))],
        name="proj_grad", compiler_params=_params(("arbitrary",)))(
            proj, proj, proj, dcqn, dckvn, g_q, g_kv, dkr, ddt_pad, dt_bias_pad, dz, dxs, dbc)


def _pair_tables(n):
    qmaj = [(i, j) for i in range(n) for j in range(i + 1)]
    kmaj = [(i, j) for j in range(n) for i in range(j, n)]
    to = lambda ps, c: jnp.asarray(np.array([p[c] for p in ps], np.int32))
    return (to(qmaj, 0), to(qmaj, 1)), (to(kmaj, 0), to(kmaj, 1))


def _att_block(lp, edge=ATT_BLOCK):
    return edge if lp % edge == 0 else MM_BLOCK


def _nt(a, b):
    return lax.dot_general(a, b, (((1,), (1,)), ((), ())), preferred_element_type=F32)


def _attn_fwd(qs, ks, vts):
    lp = qs.shape[1]
    t = _att_block(lp, ATT_BLOCK_FWD)
    n = lp // t
    (qi, kj), _ = _pair_tables(n)
    th = t // 2

    def body(qi_ref, kj_ref, q_ref, k_ref, vt_ref, o_ref, o16_ref, lse_ref, m_s, l_s, acc_s):
        p = pl.program_id(1)
        i, j = qi_ref[p], kj_ref[p]

        @pl.when(j == 0)
        def _():
            m_s[...] = jnp.full_like(m_s, NEG_BIG)
            l_s[...] = jnp.zeros_like(l_s)
            acc_s[...] = jnp.zeros_like(acc_s)

        def update(masked, k0=0, kn=t, q0=0, qn=t):
            cols = slice(q0, q0 + qn)
            sc = _nt(k_ref[0, k0:k0 + kn, :], q_ref[0, cols, :]) * ATT_SCALE_LOG2
            if masked:
                keep = (lax.broadcasted_iota(jnp.int32, (kn, qn), 1) + q0
                        >= lax.broadcasted_iota(jnp.int32, (kn, qn), 0) + k0)
                sc = jnp.where(keep, sc, NEG_BIG)
            m_prev = m_s[:, cols]
            m_new = jnp.maximum(m_prev, jnp.max(sc, axis=0, keepdims=True))
            alpha = jnp.exp2(m_prev - m_new)
            pexp = jnp.exp2(sc - m_new)
            l_s[:, cols] = alpha * l_s[:, cols] + jnp.sum(pexp, axis=0, keepdims=True)
            acc_s[:, cols] = alpha * acc_s[:, cols] + jnp.dot(vt_ref[0, :, k0:k0 + kn], pexp.astype(BF16),
                                                              preferred_element_type=F32)
            m_s[:, cols] = m_new

        @pl.when(j < i)
        def _():
            update(False)

        @pl.when(j == i)
        def _():
            if th % 128 == 0:
                update(True, 0, th, 0, t)
                update(True, th, th, th, th)
            else:
                update(True)
            out = (acc_s[...] / l_s[...]).T
            o_ref[...] = out
            o16_ref[...] = out.astype(BF16)
            lse_ref[0] = m_s[...] + jnp.log2(l_s[...])

    grid_spec = pltpu.PrefetchScalarGridSpec(
        num_scalar_prefetch=2, grid=(ATT_HEADS, int(qi.shape[0])),
        in_specs=[pl.BlockSpec((1, t, QP_W), lambda h, p, qi, kj: (h, qi[p], 0)),
                  pl.BlockSpec((1, t, QP_W), lambda h, p, qi, kj: (h, kj[p], 0)),
                  pl.BlockSpec((1, V_HEAD, t), lambda h, p, qi, kj: (h, 0, kj[p]))],
        out_specs=[pl.BlockSpec((t, V_HEAD), lambda h, p, qi, kj: (qi[p], h)),
                   pl.BlockSpec((t, V_HEAD), lambda h, p, qi, kj: (qi[p], h)),
                   pl.BlockSpec((1, 1, t), lambda h, p, qi, kj: (h, 0, qi[p]))],
        scratch_shapes=[pltpu.VMEM((1, t), F32), pltpu.VMEM((1, t), F32), pltpu.VMEM((V_HEAD, t), F32)])
    return pl.pallas_call(
        body, grid_spec=grid_spec,
        out_shape=[jax.ShapeDtypeStruct((lp, ATT_HEADS * V_HEAD), F32),
                   jax.ShapeDtypeStruct((lp, ATT_HEADS * V_HEAD), BF16), jax.ShapeDtypeStruct((ATT_HEADS, 1, lp), F32)],
        name="attn_fwd", compiler_params=_params(("parallel", "arbitrary")))(qi, kj, qs, ks, vts)


def _attn_delta(datt, att):
    lp = att.shape[0]
    t = MM_BLOCK
    w = ATT_HEADS * V_HEAD

    def body(do_ref, o_ref, d_ref):
        ones = jnp.ones((8, V_HEAD), BF16)
        for h in range(ATT_HEADS):
            cols = slice(h * V_HEAD, (h + 1) * V_HEAD)
            prod = do_ref[:, cols] * o_ref[:, cols]
            hi = prod.astype(BF16)
            lo = (prod - hi.astype(F32)).astype(BF16)
            d_ref[h] = (_nt(ones, hi) + _nt(ones, lo))[0:1, :]

    return pl.pallas_call(
        body, out_shape=jax.ShapeDtypeStruct((ATT_HEADS, 1, lp), F32), grid=(lp // t,),
        in_specs=[_rb(t, w), _rb(t, w)], out_specs=pl.BlockSpec((ATT_HEADS, 1, t), lambda i: (0, 0, i)),
        name="attn_delta", compiler_params=_params(("parallel",)))(datt, att)


def _attn_bwd(qs, ks, vs, datt16, lse2, delta):
    lp = qs.shape[1]
    tk = _att_block(lp)
    tq = ATT_BLOCK_Q_BWD if lp % ATT_BLOCK_Q_BWD == 0 and ATT_BLOCK_Q_BWD % tk == 0 else tk
    r = tq // tk
    nk, nq = lp // tk, lp // tq
    pairs = [(i, j) for j in range(nk) for i in range(j // r, nq)]
    qi = jnp.asarray(np.array([p[0] for p in pairs], np.int32))
    kj = jnp.asarray(np.array([p[1] for p in pairs], np.int32))
    n_pairs = len(pairs)

    def body(qi_ref, kj_ref, k_ref, v_ref, q_ref, do_ref, lse_ref, dl_ref, dq_hbm, dk_ref, dv_ref, dq_s, dk_s, dv_s,
             sem):
        h, p = pl.program_id(0), pl.program_id(1)
        i, j = qi_ref[p], kj_ref[p]
        first = i == j // r

        @pl.when(p == 0)
        def _():
            dq_s[...] = jnp.zeros_like(dq_s)

        @pl.when(first)
        def _():
            dk_s[...] = jnp.zeros_like(dk_s)
            dv_s[...] = jnp.zeros_like(dv_s)

        def step(masked, skip=0):
            q0 = skip * tk
            nrows = tq - q0
            q = q_ref[0, q0:, :]
            do = do_ref[q0:, :]
            pt = jnp.exp2(_nt(k_ref[0], q) * ATT_SCALE_LOG2 - lse_ref[0, :, q0:])
            if masked:
                keep = (lax.broadcasted_iota(jnp.int32, (tk, nrows), 1)
                        >= lax.broadcasted_iota(jnp.int32, (tk, nrows), 0))
                pt = jnp.where(keep, pt, 0.0)
            dst = (pt * (_nt(v_ref[0], do) - dl_ref[0, :, q0:]) * ATT_SCALE).astype(BF16)
            dv_s[...] += jnp.dot(pt.astype(BF16), do, preferred_element_type=F32)
            dk_s[...] += jnp.dot(dst, q, preferred_element_type=F32)
            rows = pl.ds(pl.multiple_of(i * tq + q0, tk), nrows)
            dq_s[rows, :] += lax.dot_general(dst, k_ref[0], (((0,), (0,)), ((), ())), preferred_element_type=F32)

        @pl.when(jnp.logical_not(first))
        def _():
            step(False)

        for sub in range(r):
            @pl.when(jnp.logical_and(first, j % r == sub))
            def _(sub=sub):
                step(True, sub)

        @pl.when(i == nq - 1)
        def _():
            dk_ref[0] = dk_s[...]
            dv_ref[0] = dv_s[...]

        @pl.when(p == n_pairs - 1)
        def _():
            out = pltpu.make_async_copy(dq_s, dq_hbm.at[h], sem)
            out.start()
            out.wait()

    grid_spec = pltpu.PrefetchScalarGridSpec(
        num_scalar_prefetch=2, grid=(ATT_HEADS, n_pairs),
        in_specs=[pl.BlockSpec((1, tk, QP_W), lambda h, p, qi, kj: (h, kj[p], 0)),
                  pl.BlockSpec((1, tk, V_HEAD), lambda h, p, qi, kj: (h, kj[p], 0)),
                  pl.BlockSpec((1, tq, QP_W), lambda h, p, qi, kj: (h, qi[p], 0)),
                  pl.BlockSpec((tq, V_HEAD), lambda h, p, qi, kj: (qi[p], h)),
                  pl.BlockSpec((1, 1, tq), lambda h, p, qi, kj: (h, 0, qi[p])),
                  pl.BlockSpec((1, 1, tq), lambda h, p, qi, kj: (h, 0, qi[p]))],
        out_specs=[pl.BlockSpec(memory_space=pl.ANY),
                   pl.BlockSpec((1, tk, QP_W), lambda h, p, qi, kj: (h, kj[p], 0)),
                   pl.BlockSpec((1, tk, V_HEAD), lambda h, p, qi, kj: (h, kj[p], 0))],
        scratch_shapes=[pltpu.VMEM((lp, QP_W), F32), pltpu.VMEM((tk, QP_W), F32), pltpu.VMEM((tk, V_HEAD), F32),
                        pltpu.SemaphoreType.DMA])
    return pl.pallas_call(
        body, grid_spec=grid_spec,
        out_shape=[jax.ShapeDtypeStruct((ATT_HEADS, lp, QP_W), F32), jax.ShapeDtypeStruct((ATT_HEADS, lp, QP_W), F32),
                   jax.ShapeDtypeStruct((ATT_HEADS, lp, V_HEAD), F32)],
        name="attn_bwd", compiler_params=_params(("arbitrary", "arbitrary")))(
            qi, kj, ks, vs, qs, datt16, lse2, delta)


N_PAIRS = SSM_HEADS // 2
HI = lax.Precision.HIGHEST


def _ssd_chunk(xp, bs, cs, dt, dt_t, alr, alc, dsk, st):
    q = dt.shape[0]
    li = lax.broadcasted_iota(jnp.int32, (q, q), 0)
    si = lax.broadcasted_iota(jnp.int32, (q, q), 1)
    tri = (si <= li).astype(F32)
    tri_t = (li <= si).astype(F32)
    lo = lax.broadcasted_iota(jnp.int32, (1, 128), 1) < 64
    h_r = lax.broadcasted_iota(jnp.int32, (1, SSM_HEADS), 1)
    h_c = lax.broadcasted_iota(jnp.int32, (SSM_HEADS, 1), 0)
    a = dt * (-jnp.exp(alr))
    a_t = dt_t * (-jnp.exp(alc))
    acum = jnp.dot(tri, a, precision=HI, preferred_element_type=F32)
    acum_t = jnp.dot(a_t, tri_t, precision=HI, preferred_element_type=F32)
    last = (lax.broadcasted_iota(jnp.int32, (q, 1), 0) == q - 1).astype(F32)
    alast = jnp.sum(acum * last, axis=0, keepdims=True)
    e = jnp.exp(acum)
    rdt = jnp.exp(alast - acum) * dt
    e_last = jnp.exp(alast)

    def col(m, h):
        return jnp.sum(m * (h_r == h).astype(F32), axis=1, keepdims=True)

    def row(m, h):
        return jnp.sum(m * (h_c == h).astype(F32), axis=0, keepdims=True)

    def pair(m, ha):
        return jnp.where(lo, col(m, ha), col(m, ha + 1))

    ys, st_new = [], []
    for g in range(2):
        c_b = cs[g].astype(BF16)
        b_b = bs[g].astype(BF16)
        cb = _nt(c_b, b_b)
        for j in range(N_PAIRS // 2):
            p = (N_PAIRS // 2) * g + j
            ha = 2 * p
            x = xp[p]
            x_b = x.astype(BF16)

            def w_of(h):
                seg = col(acum, h) - row(acum_t, h)
                return (cb * jnp.exp(jnp.minimum(seg, 0.0)) * tri * row(dt_t, h)).astype(BF16)

            y_diag = jnp.where(lo, jnp.dot(w_of(ha), x_b, preferred_element_type=F32),
                               jnp.dot(w_of(ha + 1), x_b, preferred_element_type=F32))
            y_off = jnp.dot(c_b, st[p].astype(BF16), preferred_element_type=F32) * pair(e, ha)
            ys.append(y_diag + y_off + pair(dsk, ha) * x)
            xw = (x * pair(rdt, ha)).astype(BF16)
            st_new.append(st[p] * pair(e_last, ha)
                          + lax.dot_general(b_b, xw, (((0,), (0,)), ((), ())), preferred_element_type=F32))
    return ys, st_new


def _ssd_fwd(xs, bc, dt, dt_t, alr, alc, dsk):
    lp = xs.shape[0]
    q = SSD_CHUNK
    nc = lp // q

    def body(x_ref, b_ref, c_ref, dt_ref, dtt_ref, alr_ref, alc_ref, dsk_ref, y_ref, sp_ref, st_s):
        @pl.when(pl.program_id(0) == 0)
        def _():
            st_s[...] = jnp.zeros_like(st_s)

        sp_ref[0] = st_s[...]
        xp = [x_ref[:, 128 * p:128 * (p + 1)] for p in range(N_PAIRS)]
        bs = [b_ref[:, 0:128], b_ref[:, 128:256]]
        cs = [c_ref[:, 0:128], c_ref[:, 128:256]]
        ys, st_new = _ssd_chunk(xp, bs, cs, dt_ref[...], dtt_ref[...], alr_ref[...], alc_ref[...], dsk_ref[...],
                                [st_s[p] for p in range(N_PAIRS)])
        for p in range(N_PAIRS):
            y_ref[:, 128 * p:128 * (p + 1)] = ys[p]
            st_s[p] = st_new[p]

    return pl.pallas_call(
        body,
        out_shape=[jax.ShapeDtypeStruct((lp, SSM_WIDTH), F32), jax.ShapeDtypeStruct((nc, N_PAIRS, 128, 128), F32)],
        grid=(nc,),
        in_specs=[_rb(q, SSM_WIDTH), _rb(q, 256, 0), _rb(q, 256, 1), _rb(q, SSM_HEADS),
                  pl.BlockSpec((SSM_HEADS, q), lambda i: (0, i)),
                  _full((1, SSM_HEADS)), _full((SSM_HEADS, 1)), _full((1, SSM_HEADS))],
        out_specs=[_rb(q, SSM_WIDTH), pl.BlockSpec((1, N_PAIRS, 128, 128), lambda i: (i, 0, 0, 0))],
        scratch_shapes=[pltpu.VMEM((N_PAIRS, 128, 128), F32)],
        name="ssd_fwd", compiler_params=_params(("arbitrary",)))(xs, bc, bc, dt, dt_t, alr, alc, dsk)


def _ssd_bwd(xs, bc, dt, dt_t, alr, alc, dsk, sprev, dy):
    lp = xs.shape[0]
    q = SSD_CHUNK
    nc = lp // q

    def body(x_ref, b_ref, c_ref, dt_ref, dtt_ref, alr_ref, alc_ref, dsk_ref, sp_ref, dy_ref,
             dx_ref, dbc_ref, ddt_ref, ddtt_ref, dalr_ref, dalc_ref, ddsk_ref, ds_s):
        @pl.when(pl.program_id(0) == 0)
        def _():
            ds_s[...] = jnp.zeros_like(ds_s)

        xp = [x_ref[:, 128 * p:128 * (p + 1)] for p in range(N_PAIRS)]
        bs = [b_ref[:, 0:128], b_ref[:, 128:256]]
        cs = [c_ref[:, 0:128], c_ref[:, 128:256]]
        st = [sp_ref[0, p] for p in range(N_PAIRS)]
        _, vjp = jax.vjp(_ssd_chunk, xp, bs, cs, dt_ref[...], dtt_ref[...], alr_ref[...], alc_ref[...], dsk_ref[...],
                         st)
        dys = [dy_ref[:, 128 * p:128 * (p + 1)] for p in range(N_PAIRS)]
        dxp, dbs, dcs, ddt, ddtt, dalr, dalc, ddsk, dst = vjp((dys, [ds_s[p] for p in range(N_PAIRS)]))
        for p in range(N_PAIRS):
            dx_ref[:, 128 * p:128 * (p + 1)] = dxp[p]
            ds_s[p] = dst[p]
        for g in range(2):
            dbc_ref[:, 128 * g:128 * (g + 1)] = dbs[g]
            dbc_ref[:, 256 + 128 * g:256 + 128 * (g + 1)] = dcs[g]
        ddt_ref[...] = ddt
        ddtt_ref[...] = ddtt
        _acc_add(dalr_ref, dalr)
        _acc_add(dalc_ref, dalc)
        _acc_add(ddsk_ref, ddsk)

    rev = lambda width, cb=0: pl.BlockSpec((q, width), lambda i: (nc - 1 - i, cb))
    return pl.pallas_call(
        body,
        out_shape=[jax.ShapeDtypeStruct((lp, SSM_WIDTH), F32), jax.ShapeDtypeStruct((lp, 512), F32),
                   jax.ShapeDtypeStruct((lp, SSM_HEADS), F32), jax.ShapeDtypeStruct((SSM_HEADS, lp), F32),
                   jax.ShapeDtypeStruct((1, SSM_HEADS), F32), jax.ShapeDtypeStruct((SSM_HEADS, 1), F32),
                   jax.ShapeDtypeStruct((1, SSM_HEADS), F32)],
        grid=(nc,),
        in_specs=[rev(SSM_WIDTH), rev(256, 0), rev(256, 1), rev(SSM_HEADS),
                  pl.BlockSpec((SSM_HEADS, q), lambda i: (0, nc - 1 - i)),
                  _full((1, SSM_HEADS)), _full((SSM_HEADS, 1)), _full((1, SSM_HEADS)),
                  pl.BlockSpec((1, N_PAIRS, 128, 128), lambda i: (nc - 1 - i, 0, 0, 0)), rev(SSM_WIDTH)],
        out_specs=[rev(SSM_WIDTH), rev(512), rev(SSM_HEADS), pl.BlockSpec((SSM_HEADS, q), lambda i: (0, nc - 1 - i)),
                   _full((1, SSM_HEADS)), _full((SSM_HEADS, 1)), _full((1, SSM_HEADS))],
        scratch_shapes=[pltpu.VMEM((N_PAIRS, 128, 128), F32)],
        name="ssd_bwd", compiler_params=_params(("arbitrary",)))(xs, bc, bc, dt, dt_t, alr, alc, dsk, sprev, dy)


def _q_to_slab_order(w):
    hd = QK_NOPE + QK_ROPE
    nope = [w[:, h * hd:h * hd + QK_NOPE] for h in range(ATT_HEADS)]
    rope = [w[:, h * hd + QK_NOPE:(h + 1) * hd] for h in range(ATT_HEADS)]
    return jnp.concatenate(nope + rope, axis=1)


def _q_from_slab_order(wp):
    base = ATT_HEADS * QK_NOPE
    parts = []
    for h in range(ATT_HEADS):
        parts += [wp[:, QK_NOPE * h:QK_NOPE * (h + 1)], wp[:, base + QK_ROPE * h:base + QK_ROPE * (h + 1)]]
    return jnp.concatenate(parts, axis=1)


_IN_CQ, _IN_CKV, _IN_KR, _IN_Z, _IN_XS, _IN_BC, _IN_DT = (0, 384), (384, 640), (640, 704), (704, 1728), (1728, 2752), \
    (2752, 3264), (3264, 3280)


def _pack_w_in(w):
    z = lambda n: jnp.zeros((w.shape[0], n), w.dtype)
    s = lambda r: w[:, r[0]:r[1]]
    return jnp.concatenate([s(_IN_CKV), s(_IN_KR), z(64), s(_IN_CQ), s(_IN_DT), z(112), z(128), s(_IN_Z), s(_IN_XS),
                            s(_IN_BC)], axis=1)


def _unpack_w_in(wp):
    s = lambda off, n: wp[:, off:off + n]
    return jnp.concatenate([s(SEG_CQ, 384), s(SEG_KV, 256), s(SEG_KR, 64), s(SEG_Z, 1024), s(SEG_XS, 1024),
                            s(SEG_BC, 512), s(SEG_DT, 16)], axis=1)


def _rope_tables(lp):
    inv_freq = ROPE_THETA ** (-jnp.arange(0, QK_ROPE, 2, dtype=F32) / QK_ROPE)
    ang = jnp.arange(lp, dtype=F32)[:, None] * inv_freq[None, :]
    cos, sin = jnp.cos(ang), jnp.sin(ang)
    return jnp.tile(cos, (1, 4)), jnp.concatenate([-sin, sin, -sin, sin], axis=1)


def _local_step(x, tgt, w):
    n_real = x.shape[0]
    l = N_META + n_real
    lp = -(-l // MM_BLOCK) * MM_BLOCK
    assert lp % ROW_BLOCK == 0 and lp % SSD_CHUNK == 0, lp
    h0 = lax.optimization_barrier(jnp.concatenate([w["meta_tokens"], x, jnp.zeros((lp - l, D_MODEL), F32)], axis=0))
    cos_t, sin_t = _rope_tables(lp)

    w_in_p = _pack_w_in(w["w_in"])
    w_q_p = _q_to_slab_order(w["w_q_up"])
    w_kv = w["w_kv_up"]
    if "token" in w:
        cos_t = cos_t + w["token"][0, 0]
    conv_w, conv_b = w["conv_w"], w["conv_b"]
    dt_bias_pad = jnp.concatenate([w["dt_bias"], jnp.zeros((1, 128 - SSM_HEADS), F32)], axis=1)
    alr, dsk = w["a_log"], w["d_skip"]
    alc = alr.reshape(SSM_HEADS, 1)

    n1 = _norm_in(h0, w["norm_mix_pre"])
    proj = _mm("proj", n1, w_in_p)
    cqn, ckvn, krr = _attn_prep(proj, w["q_a_norm"], w["kv_a_norm"], cos_t, sin_t)
    q = _mm("q_up", cqn, w_q_p)
    kv = _mm("kv_up", ckvn, w_kv, outs=((BF16, None),))
    qs, ks, vs, vts = _qk_pack(q, kv, krr, cos_t, sin_t)
    att, att16, lse2 = _attn_fwd(qs, ks, vts)
    xs_pre, xs_act = _conv_fwd("conv_xs_fwd", proj, SEG_XS, SSM_WIDTH, conv_w[:, :SSM_WIDTH], conv_b[:, :SSM_WIDTH])
    bc_pre, bc_act = _conv_fwd("conv_bc_fwd", proj, SEG_BC, 512, conv_w[:, SSM_WIDTH:], conv_b[:, SSM_WIDTH:])
    dt = _dt_fwd(proj, dt_bias_pad)[:, :SSM_HEADS]
    dt_t = dt.T
    y, sprev = _ssd_fwd(xs_act, bc_act, dt, dt_t, alr, alc, dsk)
    ssm = _gated_norm_fwd(y, proj, w["ssm_norm"])
    late = w["late_weights"](ssm) if "late_weights" in w else w
    w_out, w_up, w_down = late["w_out"], late["w_mlp_up"], late["w_mlp_down"]
    mix = _mm("out_proj", (att16, ssm), w_out)
    h1, n2 = _mix_residual(h0, mix, w["norm_mix_post"], w["norm_mlp_pre"])
    relu2 = lambda r: jnp.square(jnp.maximum(r, 0.0))
    act = _mm("mlp_up", n2, w_up, outs=((BF16, relu2),))
    f = _mm("mlp_down", act, w_down)
    loss, dh2, df, dg_mlp_post = _loss_and_grad(h1, f, w["norm_mlp_post"], tgt, n_real)

    g = {"norm_mlp_post": dg_mlp_post}
    g["w_mlp_down"] = _mm_tn("d_w_mlp_down", act, df)
    du = _mm("d_mlp_act", df, w_down, outs=((BF16, lambda r, ab: r * (2.0 * jnp.sqrt(ab.astype(F32)))),),
             epi_ins=(act,), trans_b=True)
    g["w_mlp_up"] = _mm_tn("d_w_mlp_up", n2, du)
    dn2 = _mm("d_n2", du, w_up, trans_b=True)
    dh1, dmix, g["norm_mlp_pre"], g["norm_mix_post"] = _mlp_residual_bwd(dh2, dn2, h1, w["norm_mlp_pre"], mix,
                                                                         w["norm_mix_post"])
    g["w_out"] = jnp.concatenate([_mm_tn("d_w_out_att", att16, dmix), _mm_tn("d_w_out_ssm", ssm, dmix)], axis=0)
    dcat, dcat16 = _mm("d_cat", dmix, w_out, outs=((F32, None), (BF16, None)), trans_b=True)
    dy, dz, g["ssm_norm"] = _gated_norm_bwd(y, proj, w["ssm_norm"], dcat)
    dxs_act, dbc_act, ddt, ddt_t, dalr, dalc, g["d_skip"] = _ssd_bwd(xs_act, bc_act, dt, dt_t, alr, alc, dsk, sprev, dy)
    g["a_log"] = dalr + dalc.reshape(1, SSM_HEADS)
    dxs, dcw_xs, dcb_xs = _conv_bwd("conv_xs_bwd", dxs_act, xs_pre, proj, SEG_XS, SSM_WIDTH, conv_w[:, :SSM_WIDTH])
    dbc, dcw_bc, dcb_bc = _conv_bwd("conv_bc_bwd", dbc_act, bc_pre, proj, SEG_BC, 512, conv_w[:, SSM_WIDTH:])
    g["conv_w"] = jnp.concatenate([dcw_xs[:CONV_K], dcw_bc[:CONV_K]], axis=1)
    g["conv_b"] = jnp.concatenate([dcb_xs, dcb_bc], axis=1)
    ddt_pad = jnp.concatenate([ddt + ddt_t.T, jnp.zeros((lp, 128 - SSM_HEADS), F32)], axis=1)

    dqs, dks, dvs = _attn_bwd(qs, ks, vs, dcat16, lse2, _attn_delta(dcat, att))
    dq, dkv, dkr = _qk_unpack_bwd(dqs, dks, dvs, cos_t, sin_t)
    g["w_q_up"] = _q_from_slab_order(_mm_tn("d_w_q_up", cqn, dq))
    g["w_kv_up"] = _mm_tn("d_w_kv_up", ckvn, dkv)
    dcqn = _mm("d_cqn", dq, w_q_p, trans_b=True)
    dckvn = _mm("d_ckvn", dkv, w_kv, trans_b=True)
    dproj, g["q_a_norm"], g["kv_a_norm"], ddtb = _proj_grad(proj, dcqn, dckvn, w["q_a_norm"], w["kv_a_norm"], dkr,
                                                          ddt_pad, dt_bias_pad, dz, dxs, dbc)
    g["dt_bias"] = ddtb[:, :SSM_HEADS]
    g["w_in"] = _unpack_w_in(_mm_tn("d_w_in", n1, dproj))
    dn1 = _mm("d_n1", dproj, w_in_p, trans_b=True)
    dh0, g["norm_mix_pre"] = _input_norm_bwd(dh1, dn1, h0, w["norm_mix_pre"])
    g["meta_tokens"] = dh0[:N_META]
    return loss, dh0, g


WEIGHTS = ["meta_tokens", "norm_mix_pre", "w_in", "q_a_norm", "w_q_up", "kv_a_norm", "w_kv_up", "conv_w", "conv_b",
           "dt_bias", "a_log", "d_skip", "ssm_norm", "w_out", "norm_mix_post", "norm_mlp_pre", "w_mlp_up",
           "w_mlp_down", "norm_mlp_post"]
SHARD_AXIS = {"meta_tokens": 1, "w_in": 1, "w_q_up": 1, "w_kv_up": 1, "conv_w": 1, "w_out": 0, "w_mlp_up": 1,
              "w_mlp_down": 0}
FULL_SHAPE = {"meta_tokens": (16, 1024), "norm_mix_pre": (1, 1024), "w_in": (1024, 3280), "q_a_norm": (1, 384),
              "w_q_up": (384, 1536), "kv_a_norm": (1, 256), "w_kv_up": (256, 2048), "conv_w": (4, 1536),
              "conv_b": (1, 1536), "dt_bias": (1, 16), "a_log": (1, 16), "d_skip": (1, 16), "ssm_norm": (1, 1024),
              "w_out": (2048, 1024), "norm_mix_post": (1, 1024), "norm_mlp_pre": (1, 1024), "w_mlp_up": (1024, 4096),
              "w_mlp_down": (4096, 1024), "norm_mlp_post": (1, 1024)}
GATHER_BF16 = ["w_in", "w_q_up", "w_kv_up", "w_out", "w_mlp_up", "w_mlp_down"]
GATHER_F32 = ["meta_tokens", "conv_w"]
GATHER_EARLY = ["w_in", "w_q_up", "w_kv_up"]
GATHER_LATE = ["w_out", "w_mlp_up", "w_mlp_down"]
ADAM_NATURAL = GATHER_BF16


def _shard_shape(name):
    shp = list(FULL_SHAPE[name])
    if name in SHARD_AXIS:
        shp[SHARD_AXIS[name]] //= N_CHIPS
    return tuple(shp)


PACK_ORDER = sorted(WEIGHTS, key=lambda n: -_shard_shape(n)[0])


PART_ROWS = 8


def _data_rows(shape):
    r, c = shape
    return r if c <= PACK_W else -(-c // PACK_W)


def _packed_rows(shape):
    return -(-_data_rows(shape) // PART_ROWS) * PART_ROWS


def _fill_rows(shapes, row_multiple):
    return -sum(_packed_rows(s) for s in shapes) % row_multiple


def _pack_rows(arrays, row_multiple, fill_before=None):
    assert arrays[0].dtype.itemsize == 4 or all(a.shape[0] % 16 == 0 for a in arrays)
    parts = []
    for a in arrays:
        r, c = a.shape
        if c > PACK_W:
            assert r == 1, a.shape
            r = _data_rows(a.shape)
            a = jnp.pad(a, ((0, 0), (0, r * PACK_W - c))).reshape(r, PACK_W)
            c = PACK_W
        parts.append(jnp.pad(a, ((0, _packed_rows((r, c)) - r), (0, PACK_W - c))))
    fill = _fill_rows([a.shape for a in arrays], row_multiple)
    if fill:
        at = len(parts) if fill_before is None else fill_before
        parts.insert(at, jnp.zeros((fill, PACK_W), parts[0].dtype))
    return jnp.concatenate(parts, axis=0)


def _unpack_rows(packed, shapes, row_multiple=1, fill_before=None):
    out, off = [], 0
    for k, (r, c) in enumerate(shapes):
        if k == fill_before:
            off += _fill_rows(shapes, row_multiple)
        nr = _data_rows((r, c))
        blk = packed[off:off + nr]
        out.append(blk[:, :c] if c <= PACK_W else blk.reshape(1, nr * PACK_W)[:, :c])
        off += _packed_rows((r, c))
    return out


def _chip_slice(full, name, t):
    if name not in SHARD_AXIS:
        return full
    ax = SHARD_AXIS[name]
    n = FULL_SHAPE[name][ax] // N_CHIPS
    return lax.slice_in_dim(full, t * n, (t + 1) * n, axis=ax)


HBM_SPEC = pl.BlockSpec(memory_space=pl.ANY)
CHIP_FLIPS = ((1, 0), (0, 1), (1, 1))


def _gather_chips(bufs):
    nb = len(bufs)

    def body(*refs):
        ins, outs = refs[:nb], refs[nb:2 * nb]
        send, recv, loc = refs[2 * nb:]
        x, y, c = lax.axis_index("x"), lax.axis_index("y"), lax.axis_index("c")
        me = 2 * x + y
        sibling = (x, y, 1 - c)
        sends, forwards = [], []
        for b in range(nb):
            half = bufs[b].shape[0] // 2
            mine = pl.ds(c * half, half)
            own = pltpu.make_async_copy(ins[b], outs[b].at[me], loc.at[b])
            own.start()
            sends.append(own)
            for k, (fx, fy) in enumerate(CHIP_FLIPS):
                cp = pltpu.make_async_remote_copy(
                    src_ref=ins[b].at[mine], dst_ref=outs[b].at[me, mine], send_sem=send.at[b, k],
                    recv_sem=recv.at[b, k], device_id=(x ^ fx, y ^ fy, c), device_id_type=MESH_ID)
                cp.start()
                sends.append(cp)
        for b in range(nb):
            half = bufs[b].shape[0] // 2
            mine, theirs = pl.ds(c * half, half), pl.ds((1 - c) * half, half)
            for k, (fx, fy) in enumerate(CHIP_FLIPS):
                chip = 2 * (x ^ fx) + (y ^ fy)
                landed = outs[b].at[chip, mine]
                pltpu.make_async_remote_copy(src_ref=landed, dst_ref=landed, send_sem=send.at[b, k],
                                             recv_sem=recv.at[b, k], device_id=sibling,
                                             device_id_type=MESH_ID).wait_recv()
                fw = pltpu.make_async_remote_copy(src_ref=landed, dst_ref=landed, send_sem=send.at[b, 3 + k],
                                                  recv_sem=recv.at[b, 3 + k], device_id=sibling,
                                                  device_id_type=MESH_ID)
                fw.start()
                forwards.append((fw, outs[b].at[chip, theirs], b, k))
        for fw, arriving, b, k in forwards:
            pltpu.make_async_remote_copy(src_ref=arriving, dst_ref=arriving, send_sem=send.at[b, 3 + k],
                                         recv_sem=recv.at[b, 3 + k], device_id=sibling,
                                         device_id_type=MESH_ID).wait_recv()
            fw.wait_send()
        for cp in sends[1::4] + sends[2::4] + sends[3::4]:
            cp.wait_send()
        for own in sends[0::4]:
            own.wait()

    return pl.pallas_call(
        body,
        out_shape=[jax.ShapeDtypeStruct((N_CHIPS,) + b.shape, b.dtype) for b in bufs],
        in_specs=[HBM_SPEC] * nb, out_specs=[HBM_SPEC] * nb,
        scratch_shapes=[pltpu.SemaphoreType.DMA((nb, 6)), pltpu.SemaphoreType.DMA((nb, 6)),
                        pltpu.SemaphoreType.DMA((nb,))],
        name="gather_chips")(*bufs)


HBM_ONLY = pl.BlockSpec(memory_space=pltpu.HBM)
SEM_SPEC = pl.BlockSpec(memory_space=pltpu.SEMAPHORE)
SPLIT_COPY_EFFECT = pltpu.SideEffectType.DATAFLOW_SIDE_EFFECTING


def _late_copies(src_ref, land_ref, sends, recvs):
    x, y, c = lax.axis_index("x"), lax.axis_index("y"), lax.axis_index("c")
    me = 2 * x + y
    return [pltpu.make_async_remote_copy(src_ref=src_ref, dst_ref=land_ref.at[me], send_sem=sends[k], recv_sem=recvs[k],
                                         device_id=(x ^ fx, y ^ fy, c), device_id_type=MESH_ID)
            for k, (fx, fy) in enumerate(CHIP_FLIPS)]


def _gather_late_start(buf):
    def body(src_ref, land_ref, s0, s1, s2, r0, r1, r2, src_thru, land_thru, token):
        for cp in _late_copies(src_ref, land_ref, (s0, s1, s2), (r0, r1, r2)):
            cp.start()
        token[...] = jnp.zeros_like(token)

    land = lax.empty((N_CHIPS,) + buf.shape, buf.dtype)
    sem = pltpu.SemaphoreType.DMA(())
    res = pl.pallas_call(
        body, name="gather_late_start",
        out_shape=(sem,) * 6 + (pltpu.HBM(buf.shape, buf.dtype), pltpu.HBM(land.shape, land.dtype),
                                jax.ShapeDtypeStruct((8, 128), F32)),
        in_specs=(HBM_ONLY, HBM_ONLY),
        out_specs=(SEM_SPEC,) * 6 + (HBM_ONLY, HBM_ONLY, pl.BlockSpec(memory_space=pltpu.VMEM)),
        input_output_aliases={0: 6, 1: 7},
        compiler_params=pltpu.CompilerParams(has_side_effects=SPLIT_COPY_EFFECT),
    )(pltpu.with_memory_space_constraint(buf, pltpu.HBM), pltpu.with_memory_space_constraint(land, pltpu.HBM))
    return res[:6], res[6], res[7], res[8]


def _gather_late_wait(sems, src_thru, land_thru, after):
    def body(src_ref, land_ref, s0, s1, s2, r0, r1, r2, after_ref, src_dead, got_ref):
        for cp in _late_copies(src_ref, land_ref, (s0, s1, s2), (r0, r1, r2)):
            cp.wait_send()
            cp.wait_recv()

    return pl.pallas_call(
        body, name="gather_late_wait",
        out_shape=(pltpu.HBM(src_thru.shape, src_thru.dtype), pltpu.HBM(land_thru.shape, land_thru.dtype)),
        in_specs=(HBM_ONLY, HBM_ONLY) + (SEM_SPEC,) * 6 + (pl.BlockSpec(memory_space=pl.ANY),),
        out_specs=(HBM_ONLY, HBM_ONLY), input_output_aliases={0: 0, 1: 1},
        compiler_params=pltpu.CompilerParams(has_side_effects=SPLIT_COPY_EFFECT),
    )(src_thru, land_thru, *sems, after)[1]


def _sibling_swap(name, buf):
    def body(src, dst, send, recv):
        x, y, c = lax.axis_index("x"), lax.axis_index("y"), lax.axis_index("c")
        cp = pltpu.make_async_remote_copy(src_ref=src, dst_ref=dst, send_sem=send, recv_sem=recv,
                                          device_id=(x, y, 1 - c), device_id_type=MESH_ID)
        cp.start()
        cp.wait()

    return pl.pallas_call(
        body, out_shape=jax.ShapeDtypeStruct(buf.shape, buf.dtype), in_specs=[HBM_SPEC], out_specs=HBM_SPEC,
        scratch_shapes=[pltpu.SemaphoreType.DMA, pltpu.SemaphoreType.DMA], name=name)(buf)


def _scatter_chips(parts):
    nb = len(parts)

    def body(*refs):
        srcs, dsts = refs[:nb], refs[nb:2 * nb]
        send, recv = refs[2 * nb:]
        x, y, c = lax.axis_index("x"), lax.axis_index("y"), lax.axis_index("c")
        copies = []
        for b in range(nb):
            for k, (fx, fy) in enumerate(CHIP_FLIPS):
                tx, ty = x ^ fx, y ^ fy
                cp = pltpu.make_async_remote_copy(
                    src_ref=srcs[b].at[2 * tx + ty], dst_ref=dsts[b].at[k], send_sem=send.at[b, k],
                    recv_sem=recv.at[b, k], device_id=(tx, ty, c), device_id_type=MESH_ID)
                cp.start()
                copies.append(cp)
        for cp in copies:
            cp.wait()

    return pl.pallas_call(
        body, out_shape=[jax.ShapeDtypeStruct((3,) + p.shape[1:], p.dtype) for p in parts], in_specs=[HBM_SPEC] * nb,
        out_specs=[HBM_SPEC] * nb,
        scratch_shapes=[pltpu.SemaphoreType.DMA((nb, 3)), pltpu.SemaphoreType.DMA((nb, 3))],
        name="scatter_chips")(*parts)


def _add_rows(name, terms, also_bf16=False):
    rows = terms[0].shape[0]
    t = _row_tile(rows)
    n_out = 2 if also_bf16 else 1

    def body(*refs):
        acc = refs[0][...].astype(F32)
        for r in refs[1:-n_out]:
            acc = acc + r[...].astype(F32)
        refs[-n_out][...] = acc
        if also_bf16:
            refs[-1][...] = acc.astype(BF16)

    res = pl.pallas_call(
        body, out_shape=[jax.ShapeDtypeStruct(terms[0].shape, dt) for dt in (F32, BF16)[:n_out]], grid=(rows // t,),
        in_specs=[_rb(t, PACK_W)] * len(terms), out_specs=[_rb(t, PACK_W)] * n_out,
        name=name, compiler_params=_params(("parallel",)))(*terms)
    return res if also_bf16 else res[0]


def _sum_chip_order(name, parts, received, me):
    stack = jnp.concatenate([lax.dynamic_index_in_dim(parts, me, axis=0, keepdims=True), received], axis=0)
    terms = []
    for chip in range(N_CHIPS):
        xr = me ^ chip
        where = jnp.where(xr == 0, 0, jnp.where(xr == 2, 1, jnp.where(xr == 1, 2, 3)))
        terms.append(lax.dynamic_index_in_dim(stack, where, axis=0, keepdims=False))
    return _add_rows(name, terms)


def _row_tile(rows):
    assert rows % 16 == 0, rows
    return max(t for t in range(16, 513, 16) if rows % t == 0)


def _adamw(name, g, w, m, v):
    rows, cols = g.shape
    t = max(tt for tt in range(8, 257, 8) if rows % tt == 0)
    c1 = 1.0 - ADAM_B1 ** ADAM_STEP
    c2 = 1.0 - ADAM_B2 ** ADAM_STEP

    def body(g_ref, w_ref, m_ref, v_ref, d_ref, mo_ref, vo_ref):
        gg = g_ref[...]
        mn = ADAM_B1 * m_ref[...] + (1.0 - ADAM_B1) * gg
        vn = ADAM_B2 * v_ref[...] + (1.0 - ADAM_B2) * (gg * gg)
        d_ref[...] = -ADAM_LR * ((mn / c1) / (jnp.sqrt(vn / c2) + ADAM_EPS) + ADAM_WD * w_ref[...])
        mo_ref[...] = mn
        vo_ref[...] = vn

    return pl.pallas_call(
        body, out_shape=[jax.ShapeDtypeStruct(g.shape, F32)] * 3, grid=(rows // t,),
        in_specs=[_rb(t, cols)] * 4, out_specs=[_rb(t, cols)] * 3,
        name=name, compiler_params=_params(("parallel",)))(g, w, m, v)


def kernel(x, meta_tokens, norm_mix_pre, w_in, q_a_norm, w_q_up, kv_a_norm, w_kv_up, conv_w, conv_b, dt_bias, a_log, d_skip, ssm_norm, w_out, norm_mix_post, norm_mlp_pre, w_mlp_up, w_mlp_down, norm_mlp_post, loss_target, m_meta_tokens, m_norm_mix_pre, m_w_in, m_q_a_norm, m_w_q_up, m_kv_a_norm, m_w_kv_up, m_conv_w, m_conv_b, m_dt_bias, m_a_log, m_d_skip, m_ssm_norm, m_w_out, m_norm_mix_post, m_norm_mlp_pre, m_w_mlp_up, m_w_mlp_down, m_norm_mlp_post, v_meta_tokens, v_norm_mix_pre, v_w_in, v_q_a_norm, v_w_q_up, v_kv_a_norm, v_w_kv_up, v_conv_w, v_conv_b, v_dt_bias, v_a_log, v_d_skip, v_ssm_norm, v_w_out, v_norm_mix_post, v_norm_mlp_pre, v_w_mlp_up, v_w_mlp_down, v_norm_mlp_post):
    given = dict(locals())
    drop = lambda name, a: a[0] if a.ndim == 3 else a
    w_loc = {n: drop(n, given[n]) for n in WEIGHTS}
    m_loc = {n: drop(n, given["m_" + n]) for n in WEIGHTS}
    v_loc = {n: drop(n, given["v_" + n]) for n in WEIGHTS}
    ix, iy, ic = lax.axis_index("x"), lax.axis_index("y"), lax.axis_index("c")
    me = 2 * ix + iy

    def assemble(names, got, own_slot=None):
        per_chip = [_unpack_rows(got[t], [_shard_shape(n) for n in names]) for t in range(N_CHIPS)]
        full = {}
        for k, n in enumerate(names):
            pieces = [per_chip[t][k] for t in range(N_CHIPS)]
            if own_slot is not None:
                pieces = [jnp.where(me == t, own_slot[n], pieces[t]) for t in range(N_CHIPS)]
            full[n] = jnp.concatenate(pieces, axis=SHARD_AXIS[n])
        return full

    bf16_of = lambda names: _pack_rows([w_loc[n].astype(BF16) for n in names], 32)
    sems, src_thru, land_thru, token = _gather_late_start(bf16_of(GATHER_LATE))
    got16, got32 = _gather_chips([bf16_of(GATHER_EARLY), _pack_rows([w_loc[n] for n in GATHER_F32], 16)])
    w_full = {n: w_loc[n] for n in WEIGHTS if n not in SHARD_AXIS}
    w_full.update(assemble(GATHER_EARLY, got16))
    w_full.update(assemble(GATHER_F32, got32))
    w_full["token"] = token
    own16 = {n: w_loc[n].astype(BF16) for n in GATHER_LATE}
    w_full["late_weights"] = lambda after: assemble(GATHER_LATE, _gather_late_wait(sems, src_thru, land_thru, after),
                                                    own_slot=own16)

    loss, dh0, g_full = _local_step(x[0], loss_target[0], w_full)
    n_real = x.shape[1]
    grad_x = dh0[N_META:N_META + n_real][None]

    shapes = [_shard_shape(n) for n in PACK_ORDER]
    n_sharded = len(SHARD_AXIS)
    slots = [_pack_rows([_chip_slice(g_full[n], n, t) for n in PACK_ORDER], SLOT_ROWS_MULTIPLE, fill_before=n_sharded)
             for t in range(N_CHIPS)]
    rows = slots[0].shape[0]
    half = rows // 2
    halves = lambda hh: jnp.concatenate([lax.dynamic_slice_in_dim(s, hh * half, half, axis=0) for s in slots], axis=0)
    keep, give = halves(ic), halves(1 - ic)
    from_sibling = _sibling_swap("sibling_swap", give)
    part32, part16 = _add_rows("chip_partial", [keep, from_sibling], also_bf16=True)
    part32 = part32.reshape(N_CHIPS, half, PACK_W)
    part16 = part16.reshape(N_CHIPS, half, PACK_W)
    assert all(n in SHARD_AXIS for n in PACK_ORDER[:n_sharded])
    replicated_rows = sum(_packed_rows(_shard_shape(n)) for n in PACK_ORDER[n_sharded:])
    assert replicated_rows <= TAIL_ROWS <= half
    tail32 = part32[:, half - TAIL_ROWS:, :]
    from16, from_tail = _scatter_chips([part16, tail32])
    my_half = _sum_chip_order("chip_total", part16, from16, me)
    my_tail = _sum_chip_order("chip_total_tail", tail32, from_tail, me)
    my_half = lax.dynamic_update_slice(my_half, my_tail, (half - TAIL_ROWS, 0))
    other_half = _sibling_swap("sibling_gather", my_half)
    g_red = jnp.where(ic == 0, jnp.concatenate([my_half, other_half], axis=0),
                      jnp.concatenate([other_half, my_half], axis=0))

    g_own = dict(zip(PACK_ORDER, _unpack_rows(g_red, shapes, SLOT_ROWS_MULTIPLE, n_sharded)))
    small = [n for n in PACK_ORDER if n not in ADAM_NATURAL]
    pack_small = lambda d: _pack_rows([d[n] for n in small], 16)
    packed_upd = _adamw("adamw_small", pack_small(g_own), pack_small(w_loc), pack_small(m_loc), pack_small(v_loc))
    upd = [dict(zip(small, _unpack_rows(p, [_shard_shape(n) for n in small]))) for p in packed_upd]
    for n in ADAM_NATURAL:
        for k, res in enumerate(_adamw("adamw_" + n, g_own[n], w_loc[n], m_loc[n], v_loc[n])):
            upd[k][n] = res

    def outputs(parts):
        return [parts[n][None] if given[n].ndim == 3 else parts[n] for n in WEIGHTS]

    total = lax.psum(loss[0, 0], ("x", "y", "c"))
    return (total, grad_x, *outputs(g_own), *outputs(upd[0]), *outputs(upd[1]), *outputs(upd[2]))
```
